```python
import jax, jax.numpy as jnp
from jax import lax
import numpy as np

D_MODEL = 1024
BATCH = 8
SEQ = 4096
DEPTH = 2

CHUNK = 64
D_MIX = D_MODEL

GLA_DV = 96
GLA_DK = 48
GLA_WIDTH = 3 * D_MODEL // 8
GLA_HEADS = GLA_WIDTH // GLA_DV
GLA_KEY_WIDTH = GLA_HEADS * GLA_DK
GLA_GATE_RANK = 16
GLA_GATE_TAU = 16.0

CONV_WIDTH = D_MODEL // 4
CONV_KERNEL = 31

ATT_HEAD_DIM = 64
ATT_WIDTH = 3 * D_MODEL // 8
ATT_HEADS = ATT_WIDTH // ATT_HEAD_DIM
ATT_LEFT_CHUNKS = 8
ATT_BAND_CHUNKS = ATT_LEFT_CHUNKS + 1
ATT_BAND = ATT_BAND_CHUNKS * CHUNK
MAX_REL_DIST = 128
N_REL = 2 * MAX_REL_DIST + 1

D_FF = 4 * D_MODEL

EPS = 1e-6
NEG_INF = -1e30

IN_SIZES = (
    GLA_KEY_WIDTH,
    GLA_KEY_WIDTH,
    GLA_WIDTH,
    GLA_WIDTH,
    GLA_GATE_RANK,
    2 * CONV_WIDTH,
    ATT_WIDTH,
    ATT_WIDTH,
    ATT_WIDTH,
)
D_IN = int(sum(IN_SIZES))
IN_SPLITS = [int(s) for s in np.cumsum(IN_SIZES)[:-1]]

kernel_name = "hybrid_gla_conformer_chunkattn_encoder"


def rmsnorm(x, g):
    xf = x.astype(jnp.float32)
    y = xf * lax.rsqrt(jnp.mean(xf * xf, axis=-1, keepdims=True) + EPS)
    return (y * g.astype(jnp.float32)).astype(x.dtype)


def gla_mixer(q, k, v, g, gate_lr, w_gate, b_gate, out_norm):
    dtype = v.dtype
    B, S = q.shape[:2]
    nc = S // CHUNK
    f32 = jnp.float32
    shp_k = (B, nc, CHUNK, GLA_HEADS, GLA_DK)
    qf = q.astype(f32).reshape(shp_k) * (GLA_DK ** -0.5)
    kf = k.astype(f32).reshape(shp_k)
    vf = v.astype(f32).reshape(B, nc, CHUNK, GLA_HEADS, GLA_DV)
    z = gate_lr.astype(f32) @ w_gate.astype(f32) + b_gate.astype(f32)
    log_a = (jax.nn.log_sigmoid(z) / GLA_GATE_TAU).reshape(shp_k)
    cum = jnp.cumsum(log_a, axis=2)
    end = cum[:, :, -1:]
    k_dec = kf * jnp.exp(end - cum)
    chunk_decay = jnp.exp(end[:, :, 0])
    kv = jnp.einsum('bnchk,bnchv->bnhkv', k_dec, vf)

    def step(state, inp):
        a, kv_c = inp
        state = a[..., None] * state + kv_c
        return state, state

    init = jnp.zeros((B, GLA_HEADS, GLA_DK, GLA_DV), f32)
    _, states = lax.scan(step, init, (jnp.moveaxis(chunk_decay, 1, 0), jnp.moveaxis(kv, 1, 0)))
    o = jnp.einsum('bnchk,nbhkv->bnchv', qf, states)
    o = o.reshape(B, S, GLA_HEADS, GLA_DV)
    o = o * lax.rsqrt(jnp.mean(o * o, axis=-1, keepdims=True) + EPS)
    o = o.reshape(B, S, GLA_WIDTH) * out_norm.astype(f32)
    o = o * jax.nn.silu(g.astype(f32))
    return o.astype(dtype)


def conv_mixer(u, w_dw, b_dw, ln_g, ln_b):
    dtype = u.dtype
    a, b = jnp.split(u, 2, axis=-1)
    h = a * jax.nn.sigmoid(b)
    h = lax.conv_general_dilated(
        h, w_dw.reshape(CONV_KERNEL, 1, CONV_WIDTH).astype(h.dtype),
        window_strides=(1,), padding=((CONV_KERNEL - 1, 0),),
        dimension_numbers=('NWC', 'WIO', 'NWC'), feature_group_count=CONV_WIDTH)
    hf = h.astype(jnp.float32) + b_dw.astype(jnp.float32)
    mu = jnp.mean(hf, axis=-1, keepdims=True)
    var = jnp.mean(jnp.square(hf - mu), axis=-1, keepdims=True)
    hf = (hf - mu) * lax.rsqrt(var + EPS) * ln_g.astype(jnp.float32) + ln_b.astype(jnp.float32)
    return jax.nn.silu(hf).astype(dtype)


def chunk_attention(q, k, v, rel_bias):
    dtype = v.dtype
    B, S = q.shape[:2]
    nc = S // CHUNK
    shp = (B, nc, CHUNK, ATT_HEADS, ATT_HEAD_DIM)
    qc, kc, vc = q.reshape(shp), k.reshape(shp), v.reshape(shp)
    pad = ((0, 0), (ATT_LEFT_CHUNKS, 0), (0, 0), (0, 0), (0, 0))
    kp, vp = jnp.pad(kc, pad), jnp.pad(vc, pad)
    k_band = jnp.concatenate([kp[:, w:w + nc] for w in range(ATT_BAND_CHUNKS)], axis=2)
    v_band = jnp.concatenate([vp[:, w:w + nc] for w in range(ATT_BAND_CHUNKS)], axis=2)
    scores = jnp.einsum('bnqhd,bnkhd->bnhqk', qc, k_band).astype(jnp.float32) * (ATT_HEAD_DIM ** -0.5)
    q_pos = np.arange(CHUNK)[:, None]
    k_pos = np.arange(ATT_BAND)[None, :] - ATT_LEFT_CHUNKS * CHUNK
    rel_idx = np.clip(q_pos - k_pos, -MAX_REL_DIST, MAX_REL_DIST) + MAX_REL_DIST
    bias = rel_bias.astype(jnp.float32)[:, rel_idx]
    key_chunk = np.arange(nc)[:, None] - ATT_LEFT_CHUNKS + np.repeat(np.arange(ATT_BAND_CHUNKS), CHUNK)[None, :]
    valid = key_chunk >= 0
    scores = jnp.where(valid[None, :, None, None, :], scores + bias[None, None], NEG_INF)
    p = jax.nn.softmax(scores, axis=-1).astype(dtype)
    o = jnp.einsum('bnhqk,bnkhd->bnqhd', p, v_band)
    return o.reshape(B, S, ATT_WIDTH)


def sq_relu_mlp(x, w_up, w_down):
    h = jax.nn.relu(x @ w_up)
    return (h * h) @ w_down


def _fwd_setup_inputs(seed: int = 0) -> dict:
    key = jax.random.key(seed)
    ks = jax.random.split(key, 20)
    f32 = jnp.float32
    nrm = lambda k, shape, s: (jax.random.normal(k, shape, f32) * s)
    return {
        "x": nrm(ks[0], (BATCH, SEQ, D_MODEL), 1.0),
        "norm_mix": 1.0 + nrm(ks[1], (DEPTH, D_MODEL), 0.01),
        "w_in": nrm(ks[2], (DEPTH, D_MODEL, D_IN), D_MODEL ** -0.5),
        "w_gla_gate": nrm(ks[3], (DEPTH, GLA_GATE_RANK, GLA_KEY_WIDTH), GLA_GATE_RANK ** -0.5),
        "b_gla_gate": nrm(ks[4], (DEPTH, GLA_KEY_WIDTH), 0.1),
        "gla_norm": 1.0 + nrm(ks[5], (DEPTH, GLA_WIDTH), 0.01),
        "w_dw": nrm(ks[6], (DEPTH, CONV_KERNEL, CONV_WIDTH), CONV_KERNEL ** -0.5),
        "b_dw": nrm(ks[7], (DEPTH, CONV_WIDTH), 0.01),
        "conv_ln_g": 1.0 + nrm(ks[8], (DEPTH, CONV_WIDTH), 0.01),
        "conv_ln_b": nrm(ks[9], (DEPTH, CONV_WIDTH), 0.01),
        "rel_bias": nrm(ks[10], (DEPTH, ATT_HEADS, N_REL), 0.1),
        "w_out": nrm(ks[11], (DEPTH, D_MIX, D_MODEL), D_MIX ** -0.5),
        "norm_ffn": 1.0 + nrm(ks[12], (DEPTH, D_MODEL), 0.01),
        "w_up": nrm(ks[13], (DEPTH, D_MODEL, D_FF), D_MODEL ** -0.5),
        "w_down": nrm(ks[14], (DEPTH, D_FF, D_MODEL), D_FF ** -0.5),
        "norm_final": 1.0 + nrm(ks[15], (D_MODEL,), 0.01),
    }


def _fwd_reference(x, norm_mix, w_in, w_gla_gate, b_gla_gate, gla_norm, w_dw, b_dw,
              conv_ln_g, conv_ln_b, rel_bias, w_out, norm_ffn, w_up, w_down, norm_final):
    h = x
    for l in range(DEPTH):
        xn = rmsnorm(h, norm_mix[l])
        proj = xn @ w_in[l]
        (g_q, g_k, g_v, g_g, g_lr, c_u, a_q, a_k, a_v) = jnp.split(proj, IN_SPLITS, axis=-1)
        o_gla = gla_mixer(g_q, g_k, g_v, g_g, g_lr, w_gla_gate[l], b_gla_gate[l], gla_norm[l])
        o_conv = conv_mixer(c_u, w_dw[l], b_dw[l], conv_ln_g[l], conv_ln_b[l])
        o_att = chunk_attention(a_q, a_k, a_v, rel_bias[l])
        mixed = jnp.concatenate([o_gla, o_conv, o_att], axis=-1)
        h = h + mixed @ w_out[l]
        h = h + sq_relu_mlp(rmsnorm(h, norm_ffn[l]), w_up[l], w_down[l])
    return rmsnorm(h, norm_final)


import jax as _jax
import jax.numpy as _jnp

TWIN_FORMAT = 'train_step'
FWD_PARAMS = ['x', 'norm_mix', 'w_in', 'w_gla_gate', 'b_gla_gate', 'gla_norm', 'w_dw', 'b_dw', 'conv_ln_g', 'conv_ln_b', 'rel_bias', 'w_out', 'norm_ffn', 'w_up', 'w_down', 'norm_final']
TWIN_WEIGHTS = ['norm_mix', 'w_in', 'w_gla_gate', 'b_gla_gate', 'gla_norm', 'w_dw', 'b_dw', 'conv_ln_g', 'conv_ln_b', 'rel_bias', 'w_out', 'norm_ffn', 'w_up', 'w_down', 'norm_final']
TWIN_DIFF_INPUT = 'x'
TWIN_INPUTS = ['x', 'norm_mix', 'w_in', 'w_gla_gate', 'b_gla_gate', 'gla_norm', 'w_dw', 'b_dw', 'conv_ln_g', 'conv_ln_b', 'rel_bias', 'w_out', 'norm_ffn', 'w_up', 'w_down', 'norm_final', 'loss_target', 'm_norm_mix', 'm_w_in', 'm_w_gla_gate', 'm_b_gla_gate', 'm_gla_norm', 'm_w_dw', 'm_b_dw', 'm_conv_ln_g', 'm_conv_ln_b', 'm_rel_bias', 'm_w_out', 'm_norm_ffn', 'm_w_up', 'm_w_down', 'm_norm_final', 'v_norm_mix', 'v_w_in', 'v_w_gla_gate', 'v_b_gla_gate', 'v_gla_norm', 'v_w_dw', 'v_b_dw', 'v_conv_ln_g', 'v_conv_ln_b', 'v_rel_bias', 'v_w_out', 'v_norm_ffn', 'v_w_up', 'v_w_down', 'v_norm_final']
TWIN_OUTPUTS = ['loss', 'grad_x', 'grad_norm_mix', 'grad_w_in', 'grad_w_gla_gate', 'grad_b_gla_gate', 'grad_gla_norm', 'grad_w_dw', 'grad_b_dw', 'grad_conv_ln_g', 'grad_conv_ln_b', 'grad_rel_bias', 'grad_w_out', 'grad_norm_ffn', 'grad_w_up', 'grad_w_down', 'grad_norm_final', 'delta_norm_mix', 'delta_w_in', 'delta_w_gla_gate', 'delta_b_gla_gate', 'delta_gla_norm', 'delta_w_dw', 'delta_b_dw', 'delta_conv_ln_g', 'delta_conv_ln_b', 'delta_rel_bias', 'delta_w_out', 'delta_norm_ffn', 'delta_w_up', 'delta_w_down', 'delta_norm_final', 'new_m_norm_mix', 'new_m_w_in', 'new_m_w_gla_gate', 'new_m_b_gla_gate', 'new_m_gla_norm', 'new_m_w_dw', 'new_m_b_dw', 'new_m_conv_ln_g', 'new_m_conv_ln_b', 'new_m_rel_bias', 'new_m_w_out', 'new_m_norm_ffn', 'new_m_w_up', 'new_m_w_down', 'new_m_norm_final', 'new_v_norm_mix', 'new_v_w_in', 'new_v_w_gla_gate', 'new_v_b_gla_gate', 'new_v_gla_norm', 'new_v_w_dw', 'new_v_b_dw', 'new_v_conv_ln_g', 'new_v_conv_ln_b', 'new_v_rel_bias', 'new_v_w_out', 'new_v_norm_ffn', 'new_v_w_up', 'new_v_w_down', 'new_v_norm_final']
TWIN_LEAF_KINDS = {'loss': 'loss', 'grad_x': 'grad_x', 'grad_norm_mix': 'grad_w', 'grad_w_in': 'grad_w', 'grad_w_gla_gate': 'grad_w', 'grad_b_gla_gate': 'grad_w', 'grad_gla_norm': 'grad_w', 'grad_w_dw': 'grad_w', 'grad_b_dw': 'grad_w', 'grad_conv_ln_g': 'grad_w', 'grad_conv_ln_b': 'grad_w', 'grad_rel_bias': 'grad_w', 'grad_w_out': 'grad_w', 'grad_norm_ffn': 'grad_w', 'grad_w_up': 'grad_w', 'grad_w_down': 'grad_w', 'grad_norm_final': 'grad_w', 'delta_norm_mix': 'delta_w', 'delta_w_in': 'delta_w', 'delta_w_gla_gate': 'delta_w', 'delta_b_gla_gate': 'delta_w', 'delta_gla_norm': 'delta_w', 'delta_w_dw': 'delta_w', 'delta_b_dw': 'delta_w', 'delta_conv_ln_g': 'delta_w', 'delta_conv_ln_b': 'delta_w', 'delta_rel_bias': 'delta_w', 'delta_w_out': 'delta_w', 'delta_norm_ffn': 'delta_w', 'delta_w_up': 'delta_w', 'delta_w_down': 'delta_w', 'delta_norm_final': 'delta_w', 'new_m_norm_mix': 'new_m', 'new_m_w_in': 'new_m', 'new_m_w_gla_gate': 'new_m', 'new_m_b_gla_gate': 'new_m', 'new_m_gla_norm': 'new_m', 'new_m_w_dw': 'new_m', 'new_m_b_dw': 'new_m', 'new_m_conv_ln_g': 'new_m', 'new_m_conv_ln_b': 'new_m', 'new_m_rel_bias': 'new_m', 'new_m_w_out': 'new_m', 'new_m_norm_ffn': 'new_m', 'new_m_w_up': 'new_m', 'new_m_w_down': 'new_m', 'new_m_norm_final': 'new_m', 'new_v_norm_mix': 'new_v', 'new_v_w_in': 'new_v', 'new_v_w_gla_gate': 'new_v', 'new_v_b_gla_gate': 'new_v', 'new_v_gla_norm': 'new_v', 'new_v_w_dw': 'new_v', 'new_v_b_dw': 'new_v', 'new_v_conv_ln_g': 'new_v', 'new_v_conv_ln_b': 'new_v', 'new_v_rel_bias': 'new_v', 'new_v_w_out': 'new_v', 'new_v_norm_ffn': 'new_v', 'new_v_w_up': 'new_v', 'new_v_w_down': 'new_v', 'new_v_norm_final': 'new_v'}


def _forward(args):
    return _fwd_reference(*[args[k] for k in FWD_PARAMS])


def _output_shape():
    out = _jax.eval_shape(lambda: _forward(_fwd_setup_inputs(0)))
    return out.shape, out.dtype

N_MICROBATCH = 1
ADAM_LR = 0.001
ADAM_B1 = 0.9
ADAM_B2 = 0.999
ADAM_EPS = 1e-08
ADAM_WD = 0.01
ADAM_STEP = 10
PER_EXAMPLE_BATCH_AXIS = {'x': 0, 'loss_target': 0}
SHARED_INPUTS = []
_WEIGHT_DTYPES = {'norm_mix': _jnp.float32, 'w_in': _jnp.float32, 'w_gla_gate': _jnp.float32, 'b_gla_gate': _jnp.float32, 'gla_norm': _jnp.float32, 'w_dw': _jnp.float32, 'b_dw': _jnp.float32, 'conv_ln_g': _jnp.float32, 'conv_ln_b': _jnp.float32, 'rel_bias': _jnp.float32, 'w_out': _jnp.float32, 'norm_ffn': _jnp.float32, 'w_up': _jnp.float32, 'w_down': _jnp.float32, 'norm_final': _jnp.float32}
MOMENT_SCALE = {'norm_mix': 1.323788e-01, 'w_in': 8.044618e-02, 'w_gla_gate': 2.123812e-02, 'b_gla_gate': 6.403332e-02, 'gla_norm': 9.231870e-02, 'w_dw': 1.008734e-01, 'b_dw': 2.077283e-01, 'conv_ln_g': 1.449916e-01, 'conv_ln_b': 1.165888e-01, 'rel_bias': 1.054847e-02, 'w_out': 7.785678e-02, 'norm_ffn': 1.575493e-01, 'w_up': 7.358919e-02, 'w_down': 1.332760e-01, 'norm_final': 3.257524e+01}


def _to_microbatches(a, axis):
    t = _jnp.moveaxis(a, axis, 0)
    t = t.reshape((N_MICROBATCH, t.shape[0] // N_MICROBATCH) + t.shape[1:])
    return _jnp.moveaxis(t, 1, axis + 1)


def setup_inputs(seed: int = 0) -> dict:
    inp = _fwd_setup_inputs(seed)
    key = _jax.random.fold_in(_jax.random.key(seed), 7919)
    shape, _ = _output_shape()
    out = dict(inp)
    out["loss_target"] = _jax.random.normal(_jax.random.fold_in(key, 0), shape, _jnp.float32)
    for i, name in enumerate(TWIN_WEIGHTS):
        w = inp[name].astype(_jnp.float32)
        if MOMENT_SCALE is None:
            s = _jnp.sqrt(_jnp.mean(_jnp.square(w)) + 1e-30)
        else:
            s = MOMENT_SCALE[name]
        km, kv = _jax.random.split(_jax.random.fold_in(key, i + 1))
        out[name] = w
        out["m_" + name] = s * _jax.random.normal(km, w.shape, _jnp.float32)
        out["v_" + name] = (s * s) * _jax.random.uniform(kv, w.shape, _jnp.float32, 0.5, 1.5)
    if N_MICROBATCH > 1:
        for name, axis in PER_EXAMPLE_BATCH_AXIS.items():
            out[name] = _to_microbatches(out[name], axis)
    return {'x': out['x'], 'norm_mix': out['norm_mix'], 'w_in': out['w_in'], 'w_gla_gate': out['w_gla_gate'], 'b_gla_gate': out['b_gla_gate'], 'gla_norm': out['gla_norm'], 'w_dw': out['w_dw'], 'b_dw': out['b_dw'], 'conv_ln_g': out['conv_ln_g'], 'conv_ln_b': out['conv_ln_b'], 'rel_bias': out['rel_bias'], 'w_out': out['w_out'], 'norm_ffn': out['norm_ffn'], 'w_up': out['w_up'], 'w_down': out['w_down'], 'norm_final': out['norm_final'], 'loss_target': out['loss_target'], 'm_norm_mix': out['m_norm_mix'], 'm_w_in': out['m_w_in'], 'm_w_gla_gate': out['m_w_gla_gate'], 'm_b_gla_gate': out['m_b_gla_gate'], 'm_gla_norm': out['m_gla_norm'], 'm_w_dw': out['m_w_dw'], 'm_b_dw': out['m_b_dw'], 'm_conv_ln_g': out['m_conv_ln_g'], 'm_conv_ln_b': out['m_conv_ln_b'], 'm_rel_bias': out['m_rel_bias'], 'm_w_out': out['m_w_out'], 'm_norm_ffn': out['m_norm_ffn'], 'm_w_up': out['m_w_up'], 'm_w_down': out['m_w_down'], 'm_norm_final': out['m_norm_final'], 'v_norm_mix': out['v_norm_mix'], 'v_w_in': out['v_w_in'], 'v_w_gla_gate': out['v_w_gla_gate'], 'v_b_gla_gate': out['v_b_gla_gate'], 'v_gla_norm': out['v_gla_norm'], 'v_w_dw': out['v_w_dw'], 'v_b_dw': out['v_b_dw'], 'v_conv_ln_g': out['v_conv_ln_g'], 'v_conv_ln_b': out['v_conv_ln_b'], 'v_rel_bias': out['v_rel_bias'], 'v_w_out': out['v_w_out'], 'v_norm_ffn': out['v_norm_ffn'], 'v_w_up': out['v_w_up'], 'v_w_down': out['v_w_down'], 'v_norm_final': out['v_norm_final']}


def _loss(weights, diff, rest, loss_target):
    with _jax.named_scope("forward"):
        args = {**rest, TWIN_DIFF_INPUT: diff, **{k: w.astype(_WEIGHT_DTYPES[k]) for k, w in weights.items()}}
        y = _forward(args)
    with _jax.named_scope("loss_head"):
        err = _jnp.square(y.astype(_jnp.float32) - loss_target)
        return 0.5 * _jnp.sum(_jnp.mean(err, axis=-1)) if err.ndim else 0.5 * err


def _adamw(w, g, m, v):
    m = ADAM_B1 * m + (1.0 - ADAM_B1) * g
    v = ADAM_B2 * v + (1.0 - ADAM_B2) * _jnp.square(g)
    m_hat = m / (1.0 - ADAM_B1 ** ADAM_STEP)
    v_hat = v / (1.0 - ADAM_B2 ** ADAM_STEP)
    delta = -ADAM_LR * (m_hat / (_jnp.sqrt(v_hat) + ADAM_EPS) + ADAM_WD * w)
    return delta, m, v


def reference(x, norm_mix, w_in, w_gla_gate, b_gla_gate, gla_norm, w_dw, b_dw, conv_ln_g, conv_ln_b, rel_bias, w_out, norm_ffn, w_up, w_down, norm_final, loss_target, m_norm_mix, m_w_in, m_w_gla_gate, m_b_gla_gate, m_gla_norm, m_w_dw, m_b_dw, m_conv_ln_g, m_conv_ln_b, m_rel_bias, m_w_out, m_norm_ffn, m_w_up, m_w_down, m_norm_final, v_norm_mix, v_w_in, v_w_gla_gate, v_b_gla_gate, v_gla_norm, v_w_dw, v_b_dw, v_conv_ln_g, v_conv_ln_b, v_rel_bias, v_w_out, v_norm_ffn, v_w_up, v_w_down, v_norm_final):
    given = dict(x=x, norm_mix=norm_mix, w_in=w_in, w_gla_gate=w_gla_gate, b_gla_gate=b_gla_gate, gla_norm=gla_norm, w_dw=w_dw, b_dw=b_dw, conv_ln_g=conv_ln_g, conv_ln_b=conv_ln_b, rel_bias=rel_bias, w_out=w_out, norm_ffn=norm_ffn, w_up=w_up, w_down=w_down, norm_final=norm_final, loss_target=loss_target, m_norm_mix=m_norm_mix, m_w_in=m_w_in, m_w_gla_gate=m_w_gla_gate, m_b_gla_gate=m_b_gla_gate, m_gla_norm=m_gla_norm, m_w_dw=m_w_dw, m_b_dw=m_b_dw, m_conv_ln_g=m_conv_ln_g, m_conv_ln_b=m_conv_ln_b, m_rel_bias=m_rel_bias, m_w_out=m_w_out, m_norm_ffn=m_norm_ffn, m_w_up=m_w_up, m_w_down=m_w_down, m_norm_final=m_norm_final, v_norm_mix=v_norm_mix, v_w_in=v_w_in, v_w_gla_gate=v_w_gla_gate, v_b_gla_gate=v_b_gla_gate, v_gla_norm=v_gla_norm, v_w_dw=v_w_dw, v_b_dw=v_b_dw, v_conv_ln_g=v_conv_ln_g, v_conv_ln_b=v_conv_ln_b, v_rel_bias=v_rel_bias, v_w_out=v_w_out, v_norm_ffn=v_norm_ffn, v_w_up=v_w_up, v_w_down=v_w_down, v_norm_final=v_norm_final)
    weights = {n: given[n] for n in TWIN_WEIGHTS}
    shared = {n: given[n] for n in SHARED_INPUTS}
    per_example = {n: given[n] for n in ['x']}
    grad_fn = _jax.value_and_grad(_loss, argnums=(0, 1))

    def one_microbatch(ex, loss_target):
        ex = dict(ex)
        diff = ex.pop(TWIN_DIFF_INPUT)
        return grad_fn(weights, diff, {**shared, **ex}, loss_target)

    if N_MICROBATCH == 1:
        loss, (grad_w, grad_x) = one_microbatch(per_example, given["loss_target"])
    else:
        def body(carry, xs):
            loss_sum, grad_sum = carry
            l_k, (gw_k, gx_k) = one_microbatch(xs[0], xs[1])
            with _jax.named_scope("update"):
                return (loss_sum + l_k, _jax.tree.map(_jnp.add, grad_sum, gw_k)), gx_k

        init = (_jnp.zeros((), _jnp.float32), _jax.tree.map(_jnp.zeros_like, weights))
        (loss, grad_w), grad_x = _jax.lax.scan(body, init, (per_example, given["loss_target"]))
    with _jax.named_scope("update"):
        delta_w, new_m, new_v = {}, {}, {}
        for n in TWIN_WEIGHTS:
            delta_w[n], new_m[n], new_v[n] = _adamw(weights[n], grad_w[n], given["m_" + n], given["v_" + n])
    return (loss, grad_x, *[grad_w[n] for n in TWIN_WEIGHTS], *[delta_w[n] for n in TWIN_WEIGHTS],
            *[new_m[n] for n in TWIN_WEIGHTS], *[new_v[n] for n in TWIN_WEIGHTS])
```

```python
import functools

import jax
import jax.numpy as jnp
from jax import lax
from jax.experimental import pallas as pl
from jax.experimental.pallas import tpu as pltpu

F32 = jnp.float32
BF16 = jnp.bfloat16
MESH = pl.DeviceIdType.MESH

D = 1024
DEPTH = 2
CH = 64
EPS = 1e-6
NEG = -1e30
N_DEV = 8
N_REL = 257
Q_SCALE = 48.0 ** -0.5
A_SCALE = 64.0 ** -0.5
GATE_TAU = 16.0
KCONV = 31

OQ, OKK, OV, OG, OCU, OAQ, OAK, OAV, OLR, DINP = 0, 256, 512, 896, 1280, 1792, 2176, 2560, 2944, 3072
IN_GROUPS = ((OQ, 256), (OKK, 256), (OV, 384), (OG, 384), (OCU, 512), (OAQ, 384), (OAK, 384), (OAV, 384), (OLR, 128))

AQ_BLK = 256
AK_WIN = 768
AK_PAD = 512
RB_W = 1536

ADAM_LR, ADAM_B1, ADAM_B2, ADAM_EPS, ADAM_WD, ADAM_STEP = 0.001, 0.9, 0.999, 1e-08, 0.01, 10

SMALL_ROWS = 128


def _cp(sem=None, vmem=None):
    kw = {}
    if sem is not None:
        kw["dimension_semantics"] = sem
    if vmem is not None:
        kw["vmem_limit_bytes"] = vmem * 1024 * 1024
    return pltpu.CompilerParams(**kw)


def _dot(a, b):
    return jnp.dot(a, b, preferred_element_type=F32)


def _dot_nt(a, b):
    return lax.dot_general(a, b, (((1,), (1,)), ((), ())), preferred_element_type=F32)


def _dot_tn(a, b):
    return lax.dot_general(a, b, (((0,), (0,)), ((), ())), preferred_element_type=F32)


def _split2(a):
    hi = a.astype(BF16)
    lo = (a - hi.astype(F32)).astype(BF16)
    return hi, lo


def _split3(a):
    hi = a.astype(BF16)
    r1 = a - hi.astype(F32)
    mid = r1.astype(BF16)
    lo = (r1 - mid.astype(F32)).astype(BF16)
    return hi, mid, lo


def _sigmoid(x):
    return 1.0 / (1.0 + jnp.exp(-x))


def _group(idx, size, n):
    g = jnp.zeros_like(idx)
    for t in range(1, n):
        g = g + (idx >= t * size).astype(jnp.int32)
    return g


def _rms_bwd(dy, x, r, gamma):
    xh = x * r
    dxh = dy * gamma
    dx = r * (dxh - xh * jnp.mean(dxh * xh, axis=-1, keepdims=True))
    return dx, jnp.sum(dy * xh, axis=0, keepdims=True)


def _row_spec(tm, n):
    return pl.BlockSpec((tm, n), lambda i: (i, 0))


def _full_spec(shape):
    nd = len(shape)
    return pl.BlockSpec(shape, lambda *_: (0,) * nd)


def _inproj_fwd(h, gamma, w):
    T = h.shape[0]
    tm = 512

    def body(h_ref, g_ref, w_ref, *outs):
        x = h_ref[...]
        r = lax.rsqrt(jnp.mean(x * x, axis=-1, keepdims=True) + EPS)
        xn = (x * r * g_ref[...]).astype(BF16)
        p = _dot(xn, w_ref[...])
        for o_ref, (off, n) in zip(outs, IN_GROUPS):
            o_ref[...] = p[:, off:off + n].astype(BF16)

    return pl.pallas_call(
        body, name="inproj_fwd", grid=(T // tm,),
        in_specs=[_row_spec(tm, D), _full_spec((1, D)), _full_spec((D, DINP))],
        out_specs=[_row_spec(tm, n) for _, n in IN_GROUPS],
        out_shape=[jax.ShapeDtypeStruct((T, n), BF16) for _, n in IN_GROUPS],
        compiler_params=_cp(("parallel",), 48),
    )(h, gamma, w)


def _inproj_bwd(h, dh_in, gamma, w, dparts):
    T = h.shape[0]
    tm = 256
    nt = T // tm

    def body(h_ref, dhin_ref, g_ref, w_ref, *rest):
        dp_refs = rest[:9]
        dh_ref, dw_ref, dg_ref, acc = rest[9:]
        i = pl.program_id(0)

        @pl.when(i == 0)
        def _():
            acc[...] = jnp.zeros_like(acc)
            dg_ref[...] = jnp.zeros_like(dg_ref)

        x = h_ref[...]
        r = lax.rsqrt(jnp.mean(x * x, axis=-1, keepdims=True) + EPS)
        gamma_ = g_ref[...]
        xn = (x * r * gamma_).astype(BF16)
        dxn = jnp.zeros((tm, D), F32)
        for d_ref, (off, n) in zip(dp_refs, IN_GROUPS):
            d = d_ref[...]
            acc[:, off:off + n] += _dot_tn(xn, d)
            dxn = dxn + _dot_nt(d, w_ref[:, off:off + n])
        dx, dgam = _rms_bwd(dxn, x, r, gamma_)
        dh_ref[...] = dhin_ref[...] + dx
        dg_ref[...] += dgam

        @pl.when(i == nt - 1)
        def _():
            dw_ref[...] = acc[...].astype(BF16)

    return pl.pallas_call(
        body, name="inproj_bwd", grid=(nt,),
        in_specs=[_row_spec(tm, D), _row_spec(tm, D), _full_spec((1, D)), _full_spec((D, DINP))]
        + [_row_spec(tm, n) for _, n in IN_GROUPS],
        out_specs=[_row_spec(tm, D), _full_spec((D, DINP)), _full_spec((1, D))],
        out_shape=[jax.ShapeDtypeStruct((T, D), F32), jax.ShapeDtypeStruct((D, DINP), BF16),
                   jax.ShapeDtypeStruct((1, D), F32)],
        scratch_shapes=[pltpu.VMEM((D, DINP), F32)],
        compiler_params=_cp(("arbitrary",), 60),
    )(h, dh_in, gamma, w, *dparts)


GLA_ROWS = 512
GLA_NC = GLA_ROWS // CH


def _gla_consts():
    ri = lax.broadcasted_iota(jnp.int32, (CH, CH), 0)
    ci = lax.broadcasted_iota(jnp.int32, (CH, CH), 1)
    upper = (ci > ri).astype(BF16)
    vv = lax.broadcasted_iota(jnp.int32, (384, 256), 0)
    kk = lax.broadcasted_iota(jnp.int32, (384, 256), 1)
    mask_t = ((_group(vv, 96, 4) == _group(kk, 48, 4)) & (kk < 192)).astype(F32)
    pi = lax.broadcasted_iota(jnp.int32, (384, 384), 0)
    pj = lax.broadcasted_iota(jnp.int32, (384, 384), 1)
    same_head = (_group(pi, 96, 4) == _group(pj, 96, 4)).astype(BF16)
    return upper, mask_t, same_head


def _gla_gate(lr_ref, wg_ref, bg_ref):
    z = _dot(lr_ref[...], wg_ref[...]) + bg_ref[...]
    la = (jnp.minimum(z, 0.0) - jnp.log(1.0 + jnp.exp(-jnp.abs(z)))) * (1.0 / GATE_TAU)
    return z, la


def _gla_chunk_decay(la_c, upper):
    hi, lo = _split2(la_c)
    dec = _dot(upper, hi) + _dot(upper, lo)
    end = jnp.sum(la_c, axis=0, keepdims=True)
    return jnp.exp(dec), jnp.exp(end)


def _head_mean(x, same_head):
    hi, lo = _split2(x)
    return (_dot(hi, same_head) + _dot(lo, same_head)) * (1.0 / 96.0)


def _gla_fwd(q, k, v, g, lr, wg, bg, gn):
    T = q.shape[0]
    nb = T // GLA_ROWS

    def body(q_ref, k_ref, v_ref, g_ref, lr_ref, wg_ref, bg_ref, gn_ref, y_ref, st_ref, s_scr, o_scr):
        upper, mask_t, same_head = _gla_consts()

        @pl.when(pl.program_id(0) == 0)
        def _():
            s_scr[...] = jnp.zeros_like(s_scr)

        _, la = _gla_gate(lr_ref, wg_ref, bg_ref)
        for c in range(GLA_NC):
            rs = slice(c * CH, (c + 1) * CH)
            w, a = _gla_chunk_decay(la[rs], upper)
            kd = (k_ref[rs, :].astype(F32) * w).astype(BF16)
            kv_t = _dot_tn(v_ref[rs, :], kd)
            s_new = s_scr[...] * a + kv_t * mask_t
            s_scr[...] = s_new
            sb = s_new.astype(BF16)
            st_ref[c] = sb
            qs = (q_ref[rs, :].astype(F32) * Q_SCALE).astype(BF16)
            o_scr[rs, :] = _dot_nt(qs, sb)
        o = o_scr[...]
        r = lax.rsqrt(_head_mean(o * o, same_head) + EPS)
        gf = g_ref[...].astype(F32)
        y_ref[...] = (o * r * gn_ref[...] * (gf * _sigmoid(gf))).astype(BF16)

    return pl.pallas_call(
        body, name="gla_fwd", grid=(nb,),
        in_specs=[_row_spec(GLA_ROWS, 256), _row_spec(GLA_ROWS, 256), _row_spec(GLA_ROWS, 384),
                  _row_spec(GLA_ROWS, 384), _row_spec(GLA_ROWS, 128),
                  _full_spec((128, 256)), _full_spec((1, 256)), _full_spec((1, 384))],
        out_specs=[_row_spec(GLA_ROWS, 384), pl.BlockSpec((GLA_NC, 384, 256), lambda i: (i, 0, 0))],
        out_shape=[jax.ShapeDtypeStruct((T, 384), BF16), jax.ShapeDtypeStruct((T // CH, 384, 256), BF16)],
        scratch_shapes=[pltpu.VMEM((384, 256), F32), pltpu.VMEM((GLA_ROWS, 384), F32)],
        compiler_params=_cp(("arbitrary",), 32),
    )(q, k, v, g, lr, wg, bg, gn)


def _gla_bwd(q, k, v, g, lr, states, dy, wg, bg, gn):
    T = q.shape[0]
    nb = T // GLA_ROWS

    def rev(s):
        return nb - 1 - s

    def body(q_ref, k_ref, v_ref, g_ref, lr_ref, st_ref, stp_ref, dy_ref, wg_ref, bg_ref, gn_ref,
             dq_ref, dk_ref, dv_ref, dg_ref, dlr_ref, dwg_ref, dbg_ref, dgn_ref,
             d_scr, an_scr, o_scr, do_scr, dla_scr):
        upper, mask_t, same_head = _gla_consts()
        s = pl.program_id(0)
        blk = rev(s)

        @pl.when(s == 0)
        def _():
            d_scr[...] = jnp.zeros_like(d_scr)
            an_scr[...] = jnp.zeros_like(an_scr)
            dwg_ref[...] = jnp.zeros_like(dwg_ref)
            dbg_ref[...] = jnp.zeros_like(dbg_ref)
            dgn_ref[...] = jnp.zeros_like(dgn_ref)

        z, la = _gla_gate(lr_ref, wg_ref, bg_ref)
        ws, as_, qss, kds = [], [], [], []
        for c in range(GLA_NC):
            rs = slice(c * CH, (c + 1) * CH)
            w, a = _gla_chunk_decay(la[rs], upper)
            ws.append(w)
            as_.append(a)
            qs = (q_ref[rs, :].astype(F32) * Q_SCALE).astype(BF16)
            qss.append(qs)
            kds.append((k_ref[rs, :].astype(F32) * w).astype(BF16))
            o_scr[rs, :] = _dot_nt(qs, st_ref[c])
        o = o_scr[...]
        r = lax.rsqrt(_head_mean(o * o, same_head) + EPS)
        on = o * r
        gf = g_ref[...].astype(F32)
        sg = _sigmoid(gf)
        si = gf * sg
        dyf = dy_ref[...].astype(F32)
        gn_ = gn_ref[...]
        dgn_ref[...] += jnp.sum(dyf * si * on, axis=0, keepdims=True)
        dg_ref[...] = (dyf * on * gn_ * (sg * (1.0 + gf * (1.0 - sg)))).astype(BF16)
        d_on = dyf * si * gn_
        do_scr[...] = r * (d_on - on * _head_mean(d_on * on, same_head))

        first = (blk > 0).astype(F32)
        for c in reversed(range(GLA_NC)):
            rs = slice(c * CH, (c + 1) * CH)
            dob = do_scr[rs, :].astype(BF16)
            sb = st_ref[c]
            if c > 0:
                s_prev = st_ref[c - 1].astype(F32)
            else:
                s_prev = stp_ref[0].astype(F32) * first
            dq_ref[rs, :] = (_dot(dob, sb) * Q_SCALE).astype(BF16)
            dt = d_scr[...] * an_scr[...] + _dot_tn(dob, qss[c]) * mask_t
            d_scr[...] = dt
            da = jnp.sum(dt * s_prev, axis=0, keepdims=True)
            db = dt.astype(BF16)
            dkd = _dot(v_ref[rs, :], db)
            dv_ref[rs, :] = _dot_nt(kds[c], db).astype(BF16)
            dk_ref[rs, :] = (dkd * ws[c]).astype(BF16)
            ddec = dkd * k_ref[rs, :].astype(F32) * ws[c]
            hi, lo = _split2(ddec)
            dla_scr[rs, :] = _dot_tn(upper, hi) + _dot_tn(upper, lo) + as_[c] * da
            an_scr[...] = as_[c]

        dz = dla_scr[...] * (1.0 - _sigmoid(z)) * (1.0 / GATE_TAU)
        dzb = dz.astype(BF16)
        dlr_ref[...] = _dot_nt(dzb, wg_ref[...]).astype(BF16)
        dwg_ref[...] += _dot_tn(lr_ref[...], dzb)
        dbg_ref[...] += jnp.sum(dz, axis=0, keepdims=True)

    def rspec(n):
        return pl.BlockSpec((GLA_ROWS, n), lambda s: (rev(s), 0))

    return pl.pallas_call(
        body, name="gla_bwd", grid=(nb,),
        in_specs=[rspec(256), rspec(256), rspec(384), rspec(384), rspec(128),
                  pl.BlockSpec((GLA_NC, 384, 256), lambda s: (rev(s), 0, 0)),
                  pl.BlockSpec((1, 384, 256), lambda s: (jnp.maximum(rev(s) * GLA_NC - 1, 0), 0, 0)),
                  rspec(384), _full_spec((128, 256)), _full_spec((1, 256)), _full_spec((1, 384))],
        out_specs=[rspec(256), rspec(256), rspec(384), rspec(384), rspec(128),
                   _full_spec((128, 256)), _full_spec((1, 256)), _full_spec((1, 384))],
        out_shape=[jax.ShapeDtypeStruct((T, 256), BF16), jax.ShapeDtypeStruct((T, 256), BF16),
                   jax.ShapeDtypeStruct((T, 384), BF16), jax.ShapeDtypeStruct((T, 384), BF16),
                   jax.ShapeDtypeStruct((T, 128), BF16),
                   jax.ShapeDtypeStruct((128, 256), F32), jax.ShapeDtypeStruct((1, 256), F32),
                   jax.ShapeDtypeStruct((1, 384), F32)],
        scratch_shapes=[pltpu.VMEM((384, 256), F32), pltpu.VMEM((1, 256), F32),
                        pltpu.VMEM((GLA_ROWS, 384), F32), pltpu.VMEM((GLA_ROWS, 384), F32),
                        pltpu.VMEM((GLA_ROWS, 256), F32)],
        compiler_params=_cp(("arbitrary",), 32),
    )(q, k, v, g, lr, states, states, dy, wg, bg, gn)


CONV_ROWS = 512
HALO = 32


def _conv_common(cu_ref, halo_ref, w_ref, b_ref, lg_ref, lb_ref, buf, blk):
    u = cu_ref[...].astype(F32)
    a = u[:, :256]
    sb = _sigmoid(u[:, 256:])
    uh = halo_ref[...].astype(F32)
    hh = uh[:, :256] * _sigmoid(uh[:, 256:]) * (blk > 0).astype(F32)
    buf[0:HALO, :] = hh
    buf[HALO:HALO + CONV_ROWS, :] = a * sb
    acc = jnp.zeros((CONV_ROWS, 256), F32)
    for j in range(KCONV):
        acc = acc + w_ref[j:j + 1, :] * buf[pl.ds(HALO - (KCONV - 1) + j, CONV_ROWS), :]
    cc = acc + b_ref[...]
    mu = jnp.mean(cc, axis=-1, keepdims=True)
    xc = cc - mu
    rstd = lax.rsqrt(jnp.mean(xc * xc, axis=-1, keepdims=True) + EPS)
    n = xc * rstd
    yln = n * lg_ref[...] + lb_ref[...]
    return a, sb, n, rstd, yln


def _conv_fwd(cu, w, b, lg, lb):
    T = cu.shape[0]
    nb = T // CONV_ROWS
    per = CONV_ROWS // HALO

    def body(cu_ref, halo_ref, w_ref, b_ref, lg_ref, lb_ref, y_ref, buf):
        _, _, _, _, yln = _conv_common(cu_ref, halo_ref, w_ref, b_ref, lg_ref, lb_ref, buf, pl.program_id(0))
        y_ref[...] = (yln * _sigmoid(yln)).astype(BF16)

    return pl.pallas_call(
        body, name="conv_fwd", grid=(nb,),
        in_specs=[_row_spec(CONV_ROWS, 512),
                  pl.BlockSpec((HALO, 512), lambda i: (jnp.maximum(i * per - 1, 0), 0)),
                  _full_spec((32, 256)), _full_spec((1, 256)), _full_spec((1, 256)), _full_spec((1, 256))],
        out_specs=_row_spec(CONV_ROWS, 256),
        out_shape=jax.ShapeDtypeStruct((T, 256), BF16),
        scratch_shapes=[pltpu.VMEM((CONV_ROWS + HALO, 256), F32)],
        compiler_params=_cp(("parallel",), 32),
    )(cu, cu, w, b, lg, lb)


def _conv_bwd(cu, dy, w, b, lg, lb):
    T = cu.shape[0]
    nb = T // CONV_ROWS
    per = CONV_ROWS // HALO

    def rev(s):
        return nb - 1 - s

    def body(cu_ref, halo_ref, dy_ref, w_ref, b_ref, lg_ref, lb_ref,
             dcu_ref, dw_ref, db_ref, dlg_ref, dlb_ref, buf, dcbuf, carry):
        s = pl.program_id(0)

        @pl.when(s == 0)
        def _():
            carry[...] = jnp.zeros_like(carry)
            dw_ref[...] = jnp.zeros_like(dw_ref)
            db_ref[...] = jnp.zeros_like(db_ref)
            dlg_ref[...] = jnp.zeros_like(dlg_ref)
            dlb_ref[...] = jnp.zeros_like(dlb_ref)

        a, sb, n, rstd, yln = _conv_common(cu_ref, halo_ref, w_ref, b_ref, lg_ref, lb_ref, buf, rev(s))
        sg = _sigmoid(yln)
        dyln = dy_ref[...].astype(F32) * (sg * (1.0 + yln * (1.0 - sg)))
        dlg_ref[...] += jnp.sum(dyln * n, axis=0, keepdims=True)
        dlb_ref[...] += jnp.sum(dyln, axis=0, keepdims=True)
        dn = dyln * lg_ref[...]
        dc = rstd * (dn - jnp.mean(dn, axis=-1, keepdims=True) - n * jnp.mean(dn * n, axis=-1, keepdims=True))
        db_ref[...] += jnp.sum(dc, axis=0, keepdims=True)
        dcbuf[0:CONV_ROWS, :] = dc
        dcbuf[CONV_ROWS:CONV_ROWS + HALO, :] = carry[...]
        dhg = jnp.zeros((CONV_ROWS, 256), F32)
        for j in range(KCONV):
            tap = buf[pl.ds(HALO - (KCONV - 1) + j, CONV_ROWS), :]
            dw_ref[j:j + 1, :] += jnp.sum(dc * tap, axis=0, keepdims=True)
            dhg = dhg + w_ref[j:j + 1, :] * dcbuf[pl.ds(KCONV - 1 - j, CONV_ROWS), :]
        carry[...] = dc[0:HALO, :]
        dcu_ref[...] = jnp.concatenate([dhg * sb, dhg * a * sb * (1.0 - sb)], axis=1).astype(BF16)

    def rspec(n):
        return pl.BlockSpec((CONV_ROWS, n), lambda s: (rev(s), 0))

    return pl.pallas_call(
        body, name="conv_bwd", grid=(nb,),
        in_specs=[rspec(512),
                  pl.BlockSpec((HALO, 512), lambda s: (jnp.maximum(rev(s) * per - 1, 0), 0)),
                  rspec(256),
                  _full_spec((32, 256)), _full_spec((1, 256)), _full_spec((1, 256)), _full_spec((1, 256))],
        out_specs=[rspec(512), _full_spec((32, 256)), _full_spec((1, 256)), _full_spec((1, 256)),
                   _full_spec((1, 256))],
        out_shape=[jax.ShapeDtypeStruct((T, 512), BF16), jax.ShapeDtypeStruct((32, 256), F32),
                   jax.ShapeDtypeStruct((1, 256), F32), jax.ShapeDtypeStruct((1, 256), F32),
                   jax.ShapeDtypeStruct((1, 256), F32)],
        scratch_shapes=[pltpu.VMEM((CONV_ROWS + HALO, 256), F32), pltpu.VMEM((CONV_ROWS + HALO, 256), F32),
                        pltpu.VMEM((HALO, 256), F32)],
        compiler_params=_cp(("arbitrary",), 32),
    )(cu, cu, dy, w, b, lg, lb)


def _rel_onehot_t(shift=0):
    r = lax.broadcasted_iota(jnp.int32, (384, RB_W), 0)
    n = lax.broadcasted_iota(jnp.int32, (384, RB_W), 1) - shift
    idx = jnp.clip(1024 - n, -128, 128) + 128
    return (idx == r).astype(BF16)


def _relbias_expand(rb):
    def body(rb_ref, out_ref):
        oh = _rel_onehot_t()
        hi, mid, lo = _split3(rb_ref[...])
        strip = _dot(hi, oh) + _dot(mid, oh) + _dot(lo, oh)
        qi = _group(lax.broadcasted_iota(jnp.int32, (AQ_BLK, AK_WIN), 0), CH, 4)
        kj = _group(lax.broadcasted_iota(jnp.int32, (AQ_BLK, AK_WIN), 1), CH, 12)
        valid = (kj >= qi) & (kj <= qi + 8)
        for hd in range(6):
            x = jnp.broadcast_to(strip[hd:hd + 1, :], (AQ_BLK, RB_W))
            xr = pltpu.roll(x, 0, 1, stride=1, stride_axis=0)
            out_ref[hd] = jnp.where(valid, xr[:, 512:512 + AK_WIN], NEG)

    return pl.pallas_call(
        body, name="relbias_expand",
        out_shape=jax.ShapeDtypeStruct((6, AQ_BLK, AK_WIN), F32),
        compiler_params=_cp(None, 32),
    )(rb)


def _relbias_grad(dbias):
    def body(db_ref, out_ref):
        oh = _rel_onehot_t(AQ_BLK - 1)
        ri = lax.broadcasted_iota(jnp.int32, (AQ_BLK, AQ_BLK), 0)
        ci = lax.broadcasted_iota(jnp.int32, (AQ_BLK, AQ_BLK), 1)
        flip = (ri + ci == AQ_BLK - 1).astype(BF16)
        rows = []
        for hd in range(6):
            hi, mid, lo = _split3(db_ref[hd])
            rev = _dot(flip, hi) + _dot(flip, mid) + _dot(flip, lo)
            x = jnp.concatenate([jnp.zeros((AQ_BLK, 512), F32), rev,
                                 jnp.zeros((AQ_BLK, RB_W - 512 - AK_WIN), F32)], axis=1)
            xr = pltpu.roll(x, 0, 1, stride=1, stride_axis=0)
            rows.append(jnp.sum(xr, axis=0, keepdims=True))
        rows.append(jnp.zeros((2, RB_W), F32))
        dstrip = jnp.concatenate(rows, axis=0)
        hi, mid, lo = _split3(dstrip)
        out_ref[...] = _dot_nt(hi, oh) + _dot_nt(mid, oh) + _dot_nt(lo, oh)

    return pl.pallas_call(
        body, name="relbias_grad",
        out_shape=jax.ShapeDtypeStruct((8, 384), F32),
        compiler_params=_cp(None, 32),
    )(dbias)


def _att_scores(q_h, k_h, bias_h, kvalid):
    s = _dot_nt(q_h, k_h) * A_SCALE + bias_h
    s = jnp.where(kvalid, s, NEG)
    m = jnp.max(s, axis=-1, keepdims=True)
    p = jnp.exp(s - m)
    return p / jnp.sum(p, axis=-1, keepdims=True)


def _att_kvalid(i):
    col = _group(lax.broadcasted_iota(jnp.int32, (1, AK_WIN), 1), CH, 12)
    return col >= 8 - 4 * i


def _att_fwd(q, kpad, vpad, bias):
    T = q.shape[0]
    nb = T // AQ_BLK

    def body(q_ref, k_hbm, v_hbm, b_ref, o_ref, kw_scr, vw_scr):
        i = pl.program_id(0)
        start = pl.multiple_of(i * AQ_BLK, AQ_BLK)
        pltpu.sync_copy(k_hbm.at[pl.ds(start, AK_WIN), :], kw_scr)
        pltpu.sync_copy(v_hbm.at[pl.ds(start, AK_WIN), :], vw_scr)
        kw = kw_scr[...]
        vw = vw_scr[...]
        qb = q_ref[...]
        kvalid = _att_kvalid(i)
        for hd in range(6):
            ls = slice(hd * 64, (hd + 1) * 64)
            p = _att_scores(qb[:, ls], kw[:, ls], b_ref[hd], kvalid)
            o_ref[:, ls] = _dot(p.astype(BF16), vw[:, ls]).astype(BF16)

    return pl.pallas_call(
        body, name="att_fwd", grid=(nb,),
        in_specs=[_row_spec(AQ_BLK, 384), _any_spec(), _any_spec(), _full_spec((6, AQ_BLK, AK_WIN))],
        out_specs=_row_spec(AQ_BLK, 384),
        out_shape=jax.ShapeDtypeStruct((T, 384), BF16),
        scratch_shapes=[pltpu.VMEM((AK_WIN, 384), BF16), pltpu.VMEM((AK_WIN, 384), BF16)],
        compiler_params=_cp(("arbitrary",), 48),
    )(q, kpad, vpad, bias)


def _att_bwd(q, kpad, vpad, bias, do):
    T = q.shape[0]
    nb = T // AQ_BLK

    def body(q_ref, k_hbm, v_hbm, b_ref, do_ref, dq_ref, dk_ref, dv_ref, db_ref, dk_acc, dv_acc, kw_scr, vw_scr):
        i = pl.program_id(0)

        @pl.when(i == 0)
        def _():
            dk_acc[...] = jnp.zeros_like(dk_acc)
            dv_acc[...] = jnp.zeros_like(dv_acc)
            db_ref[...] = jnp.zeros_like(db_ref)

        start = pl.multiple_of(i * AQ_BLK, AQ_BLK)
        pltpu.sync_copy(k_hbm.at[pl.ds(start, AK_WIN), :], kw_scr)
        pltpu.sync_copy(v_hbm.at[pl.ds(start, AK_WIN), :], vw_scr)
        kw = kw_scr[...]
        vw = vw_scr[...]
        qb = q_ref[...]
        dob = do_ref[...]
        kvalid = _att_kvalid(i)
        for hd in range(6):
            ls = slice(hd * 64, (hd + 1) * 64)
            q_h, k_h, v_h, do_h = qb[:, ls], kw[:, ls], vw[:, ls], dob[:, ls]
            p = _att_scores(q_h, k_h, b_ref[hd], kvalid)
            dv_acc[pl.ds(start, AK_WIN), ls] += _dot_tn(p.astype(BF16), do_h)
            dp = _dot_nt(do_h, v_h)
            ds = p * (dp - jnp.sum(p * dp, axis=-1, keepdims=True))
            db_ref[hd] += ds
            dsb = ds.astype(BF16)
            dq_ref[:, ls] = (_dot(dsb, k_h) * A_SCALE).astype(BF16)
            dk_acc[pl.ds(start, AK_WIN), ls] += _dot_tn(dsb, q_h) * A_SCALE

        @pl.when(i == nb - 1)
        def _():
            dk_ref[...] = dk_acc[AK_PAD:, :].astype(BF16)
            dv_ref[...] = dv_acc[AK_PAD:, :].astype(BF16)

    return pl.pallas_call(
        body, name="att_bwd", grid=(nb,),
        in_specs=[_row_spec(AQ_BLK, 384), _any_spec(), _any_spec(),
                  _full_spec((6, AQ_BLK, AK_WIN)), _row_spec(AQ_BLK, 384)],
        out_specs=[_row_spec(AQ_BLK, 384), _full_spec((T, 384)), _full_spec((T, 384)),
                   _full_spec((6, AQ_BLK, AK_WIN))],
        out_shape=[jax.ShapeDtypeStruct((T, 384), BF16), jax.ShapeDtypeStruct((T, 384), BF16),
                   jax.ShapeDtypeStruct((T, 384), BF16), jax.ShapeDtypeStruct((6, AQ_BLK, AK_WIN), F32)],
        scratch_shapes=[pltpu.VMEM((T + AK_PAD, 384), F32), pltpu.VMEM((T + AK_PAD, 384), F32),
                        pltpu.VMEM((AK_WIN, 384), BF16), pltpu.VMEM((AK_WIN, 384), BF16)],
        compiler_params=_cp(("arbitrary",), 56),
    )(q, kpad, vpad, bias, do)


FF_BLK = 512
N_FF = 4096 // FF_BLK


def _outproj_mlp_fwd(h, o_gla, o_conv, o_att, w_out, gamma, w_up, w_down):
    T = h.shape[0]
    tm = 512

    def body(h_ref, og_ref, oc_ref, oa_ref, wo_ref, g_ref, wu_ref, wd_ref, h1_ref, xn_ref, h2_ref, acc):
        j = pl.program_id(1)

        @pl.when(j == 0)
        def _():
            wo = wo_ref[...]
            h1 = (h_ref[...] + _dot(og_ref[...], wo[0:384]) + _dot(oc_ref[...], wo[384:640])
                  + _dot(oa_ref[...], wo[640:1024]))
            h1_ref[...] = h1
            r = lax.rsqrt(jnp.mean(h1 * h1, axis=-1, keepdims=True) + EPS)
            xn_ref[...] = (h1 * r * g_ref[...]).astype(BF16)
            acc[...] = h1

        a = jnp.maximum(_dot(xn_ref[...], wu_ref[0]), 0.0)
        acc[...] += _dot((a * a).astype(BF16), wd_ref[0])

        @pl.when(j == N_FF - 1)
        def _():
            h2_ref[...] = acc[...]

    row = lambda n: pl.BlockSpec((tm, n), lambda i, j: (i, 0))
    return pl.pallas_call(
        body, name="outproj_mlp_fwd", grid=(T // tm, N_FF),
        in_specs=[row(D), row(384), row(256), row(384),
                  pl.BlockSpec((D, D), lambda i, j: (0, 0)), pl.BlockSpec((1, D), lambda i, j: (0, 0)),
                  pl.BlockSpec((1, D, FF_BLK), lambda i, j: (j, 0, 0)),
                  pl.BlockSpec((1, FF_BLK, D), lambda i, j: (j, 0, 0))],
        out_specs=[row(D), row(D), row(D)],
        out_shape=[jax.ShapeDtypeStruct((T, D), F32), jax.ShapeDtypeStruct((T, D), BF16),
                   jax.ShapeDtypeStruct((T, D), F32)],
        scratch_shapes=[pltpu.VMEM((tm, D), F32)],
        compiler_params=_cp(("parallel", "arbitrary"), 48),
    )(h, o_gla, o_conv, o_att, w_out, gamma, w_up, w_down)


def _mlp_bwd(xn2, h1, dh2, gamma, w_up, w_down):
    T = xn2.shape[0]
    tm = 512
    nt = T // tm
    last = N_FF - 1

    def body(xn_ref, h1_ref, dy_ref, g_ref, wu_ref, wd_ref, dh1_ref, dwu_ref, dwd_ref, dg_ref,
             dxn_acc, acc_u, acc_d):
        j = pl.program_id(0)
        i = pl.program_id(1)
        x = xn_ref[...]
        dy = dy_ref[...]
        dyb = dy.astype(BF16)
        wu = wu_ref[0]
        wd = wd_ref[0]
        a = jnp.maximum(_dot(x, wu), 0.0)
        hh = (a * a).astype(BF16)
        du = (_dot_nt(dyb, wd) * (2.0 * a)).astype(BF16)
        cu_ = _dot_tn(x, du)
        cd_ = _dot_tn(hh, dyb)

        @pl.when(i == 0)
        def _():
            acc_u[...] = cu_
            acc_d[...] = cd_

        @pl.when(i > 0)
        def _():
            acc_u[...] += cu_
            acc_d[...] += cd_

        @pl.when(i == nt - 1)
        def _():
            dwu_ref[0, 0] = acc_u[...].astype(BF16)
            dwd_ref[0, 0] = acc_d[...].astype(BF16)

        rows = pl.ds(pl.multiple_of(i * tm, tm), tm)
        dxn = _dot_nt(du, wu)

        @pl.when(j == 0)
        def _():
            dxn_acc[rows, :] = dxn

        @pl.when(j > 0)
        def _():
            dxn_acc[rows, :] += dxn

        @pl.when(j == last)
        def _():
            @pl.when(i == 0)
            def _():
                dg_ref[...] = jnp.zeros_like(dg_ref)

            h1 = h1_ref[...]
            r = lax.rsqrt(jnp.mean(h1 * h1, axis=-1, keepdims=True) + EPS)
            dx, dgam = _rms_bwd(dxn_acc[rows, :], h1, r, g_ref[...])
            dh1_ref[...] = dy + dx
            dg_ref[...] += dgam

    late = lambda j, i: (jnp.where(j == last, i, 0), 0)
    return pl.pallas_call(
        body, name="mlp_bwd", grid=(N_FF, nt),
        in_specs=[pl.BlockSpec((tm, D), lambda j, i: (i, 0)), pl.BlockSpec((tm, D), late),
                  pl.BlockSpec((tm, D), lambda j, i: (i, 0)), pl.BlockSpec((1, D), lambda j, i: (0, 0)),
                  pl.BlockSpec((1, D, FF_BLK), lambda j, i: (j, 0, 0)),
                  pl.BlockSpec((1, FF_BLK, D), lambda j, i: (j, 0, 0))],
        out_specs=[pl.BlockSpec((tm, D), late),
                   pl.BlockSpec((1, 1, D, FF_BLK), lambda j, i: (j % 2, j // 2, 0, 0)),
                   pl.BlockSpec((1, 1, FF_BLK, D), lambda j, i: (j % 2, j // 2, 0, 0)),
                   pl.BlockSpec((1, D), lambda j, i: (0, 0))],
        out_shape=[jax.ShapeDtypeStruct((T, D), F32), jax.ShapeDtypeStruct((2, 4, D, FF_BLK), BF16),
                   jax.ShapeDtypeStruct((2, 4, FF_BLK, D), BF16), jax.ShapeDtypeStruct((1, D), F32)],
        scratch_shapes=[pltpu.VMEM((T, D), F32), pltpu.VMEM((D, FF_BLK), F32), pltpu.VMEM((FF_BLK, D), F32)],
        compiler_params=_cp(("arbitrary", "arbitrary"), 56),
    )(xn2, h1, dh2, gamma, w_up, w_down)


def _outproj_bwd(dh1, o_gla, o_conv, o_att, w_out):
    T = dh1.shape[0]
    tm = 512
    nt = T // tm

    def body(dy_ref, og_ref, oc_ref, oa_ref, wo_ref, dg_ref, dc_ref, da_ref, dw_ref, acc):
        i = pl.program_id(0)
        dyb = dy_ref[...].astype(BF16)
        dm = _dot_nt(dyb, wo_ref[...])
        dg_ref[...] = dm[:, 0:384].astype(BF16)
        dc_ref[...] = dm[:, 384:640].astype(BF16)
        da_ref[...] = dm[:, 640:1024].astype(BF16)
        mixed = jnp.concatenate([og_ref[...], oc_ref[...], oa_ref[...]], axis=1)
        contrib = _dot_tn(mixed, dyb)

        @pl.when(i == 0)
        def _():
            acc[...] = contrib

        @pl.when(i > 0)
        def _():
            acc[...] += contrib

        @pl.when(i == nt - 1)
        def _():
            for j in range(N_DEV):
                dw_ref[j % 2, j // 2] = acc[j * 128:(j + 1) * 128, :].astype(BF16)

    return pl.pallas_call(
        body, name="outproj_bwd", grid=(nt,),
        in_specs=[_row_spec(tm, D), _row_spec(tm, 384), _row_spec(tm, 256), _row_spec(tm, 384),
                  _full_spec((D, D))],
        out_specs=[_row_spec(tm, 384), _row_spec(tm, 256), _row_spec(tm, 384), _full_spec((2, 4, 128, D))],
        out_shape=[jax.ShapeDtypeStruct((T, 384), BF16), jax.ShapeDtypeStruct((T, 256), BF16),
                   jax.ShapeDtypeStruct((T, 384), BF16), jax.ShapeDtypeStruct((2, 4, 128, D), BF16)],
        scratch_shapes=[pltpu.VMEM((D, D), F32)],
        compiler_params=_cp(("arbitrary",), 40),
    )(dh1, o_gla, o_conv, o_att, w_out)


def _loss_fwd_bwd(h, gamma, target):
    T = h.shape[0]
    tm = 512

    def body(h_ref, g_ref, t_ref, loss_ref, dh_ref, dg_ref):
        @pl.when(pl.program_id(0) == 0)
        def _():
            loss_ref[...] = jnp.zeros_like(loss_ref)
            dg_ref[...] = jnp.zeros_like(dg_ref)

        x = h_ref[...]
        r = lax.rsqrt(jnp.mean(x * x, axis=-1, keepdims=True) + EPS)
        gamma_ = g_ref[...]
        e = x * r * gamma_ - t_ref[...]
        loss_ref[...] += 0.5 * jnp.sum(jnp.mean(e * e, axis=-1, keepdims=True), axis=0, keepdims=True)
        dx, dgam = _rms_bwd(e * (1.0 / D), x, r, gamma_)
        dh_ref[...] = dx
        dg_ref[...] += dgam

    return pl.pallas_call(
        body, name="loss_fwd_bwd", grid=(T // tm,),
        in_specs=[_row_spec(tm, D), _full_spec((1, D)), _row_spec(tm, D)],
        out_specs=[_full_spec((8, 128)), _row_spec(tm, D), _full_spec((1, D))],
        out_shape=[jax.ShapeDtypeStruct((8, 128), F32), jax.ShapeDtypeStruct((T, D), F32),
                   jax.ShapeDtypeStruct((1, D), F32)],
        compiler_params=_cp(("arbitrary",), 32),
    )(h, gamma, target)


def _adamw_math(w, g, m, v):
    m = ADAM_B1 * m + (1.0 - ADAM_B1) * g
    v = ADAM_B2 * v + (1.0 - ADAM_B2) * (g * g)
    m_hat = m / (1.0 - ADAM_B1 ** ADAM_STEP)
    v_hat = v / (1.0 - ADAM_B2 ** ADAM_STEP)
    delta = -ADAM_LR * (m_hat / (jnp.sqrt(v_hat) + ADAM_EPS) + ADAM_WD * w)
    return delta, m, v


def _rs_adamw(a_own, r2, w, m, v, chip_idx, rows_blk):
    R, C = w.shape
    nblk = R // rows_blk

    def body(chip_ref, a_ref, r_ref, w_ref, m_ref, v_ref, g_out, d_out, m_out, v_out):
        g = (a_ref[0].astype(F32) + r_ref[0].astype(F32)) + (r_ref[1].astype(F32) + r_ref[2].astype(F32))
        delta, m_new, v_new = _adamw_math(w_ref[...], g, m_ref[...], v_ref[...])
        g_out[...] = g
        d_out[...] = delta
        m_out[...] = m_new
        v_out[...] = v_new

    blk = pl.BlockSpec((rows_blk, C), lambda i, chip: (i, 0))
    grid_spec = pltpu.PrefetchScalarGridSpec(
        num_scalar_prefetch=1, grid=(nblk,),
        in_specs=[pl.BlockSpec((1, rows_blk, C), lambda i, chip: (chip[0], i, 0)),
                  pl.BlockSpec((3, rows_blk, C), lambda i, chip: (0, i, 0)), blk, blk, blk],
        out_specs=[blk, blk, blk, blk])
    return pl.pallas_call(
        body, name="rs_adamw", grid_spec=grid_spec,
        out_shape=[jax.ShapeDtypeStruct((R, C), F32)] * 4,
        compiler_params=_cp(("arbitrary",), 32),
    )(chip_idx, a_own, r2, w, m, v)


def _pair_sum(g, r1, core_idx, rows_blk):
    _, _, R, C = g.shape
    nblk = R // rows_blk

    def body(core_ref, g_ref, r_ref, o_ref):
        o_ref[...] = (g_ref[0].astype(F32) + r_ref[...].astype(F32)).astype(BF16)

    grid_spec = pltpu.PrefetchScalarGridSpec(
        num_scalar_prefetch=1, grid=(4, nblk),
        in_specs=[pl.BlockSpec((1, 1, rows_blk, C), lambda k, i, core: (core[0], k, i, 0)),
                  pl.BlockSpec((1, rows_blk, C), lambda k, i, core: (k, i, 0))],
        out_specs=pl.BlockSpec((1, rows_blk, C), lambda k, i, core: (k, i, 0)))
    return pl.pallas_call(
        body, name="rs_pair_sum", grid_spec=grid_spec,
        out_shape=jax.ShapeDtypeStruct((4, R, C), BF16),
        compiler_params=_cp(("arbitrary", "arbitrary"), 32),
    )(core_idx, g, r1)


def _small_sum(gathered):
    def body(g_ref, o_ref):
        acc = g_ref[0]
        for d in range(1, N_DEV):
            acc = acc + g_ref[d]
        o_ref[...] = acc

    return pl.pallas_call(
        body, name="small_sum",
        out_shape=jax.ShapeDtypeStruct((SMALL_ROWS, 1024), F32),
        compiler_params=_cp(None, 32),
    )(gathered)


def _adamw_small(ws, gs, ms, vs):
    n = len(ws)

    def body(*refs):
        w_r, g_r, m_r, v_r = refs[0:n], refs[n:2 * n], refs[2 * n:3 * n], refs[3 * n:4 * n]
        d_o, m_o, v_o = refs[4 * n:5 * n], refs[5 * n:6 * n], refs[6 * n:7 * n]
        for t in range(n):
            delta, m_new, v_new = _adamw_math(w_r[t][...], g_r[t][...], m_r[t][...], v_r[t][...])
            d_o[t][...] = delta
            m_o[t][...] = m_new
            v_o[t][...] = v_new

    shapes = [jax.ShapeDtypeStruct(w.shape, F32) for w in ws]
    outs = pl.pallas_call(
        body, name="adamw_small", out_shape=shapes * 3, compiler_params=_cp(None, 32),
    )(*ws, *gs, *ms, *vs)
    return outs[0:n], outs[n:2 * n], outs[2 * n:3 * n]


def _mesh_pos():
    return lax.axis_index("x"), lax.axis_index("y"), lax.axis_index("c")


def _any_spec():
    return pl.BlockSpec(memory_space=pl.ANY)


def _allgather(arrs, name):
    n = len(arrs)

    def body(*refs):
        ins, outs = refs[:n], refs[n:2 * n]
        send_sems, recv_sems, local_sems = refs[2 * n:]
        x, y, c = _mesh_pos()
        me, sibling = (x, y, c), (x, y, 1 - c)
        chips = [(1 - x, y), (x, 1 - y), (1 - x, 1 - y)]

        def slot(a, pos):
            return outs[a].at[4 * pos[0] + 2 * pos[1] + pos[2]]

        def copy(a, k, block, to, src=None):
            return pltpu.make_async_remote_copy(
                src_ref=slot(a, block) if src is None else src, dst_ref=slot(a, block),
                send_sem=send_sems.at[a, k], recv_sem=recv_sems.at[a, k],
                device_id=to, device_id_type=MESH)

        mine = [pltpu.make_async_copy(ins[a], slot(a, me), local_sems.at[a]) for a in range(n)]
        for cp in mine:
            cp.start()
        first = []
        for a in range(n):
            first.append(copy(a, 0, me, sibling, src=ins[a]))
            first += [copy(a, 1 + j, me, (*chip, c), src=ins[a]) for j, chip in enumerate(chips)]
        for cp in first:
            cp.start()
        passed = []
        for j, chip in enumerate(chips):
            for a in range(n):
                copy(a, 1 + j, (*chip, c), me).wait_recv()
                fwd = copy(a, 4 + j, (*chip, c), sibling)
                fwd.start()
                passed.append(fwd)
        for a in range(n):
            copy(a, 0, sibling, me).wait_recv()
            for j, chip in enumerate(chips):
                copy(a, 4 + j, (*chip, 1 - c), me).wait_recv()
        for cp in first + passed:
            cp.wait_send()
        for cp in mine:
            cp.wait()

    return pl.pallas_call(
        body, name=name,
        in_specs=[_any_spec()] * n, out_specs=[_any_spec()] * n,
        out_shape=[jax.ShapeDtypeStruct((N_DEV,) + a.shape, a.dtype) for a in arrs],
        scratch_shapes=[pltpu.SemaphoreType.DMA((n, 7)), pltpu.SemaphoreType.DMA((n, 7)),
                        pltpu.SemaphoreType.DMA((n,))],
    )(*arrs)


def _rs_swap(arrs, name):
    n = len(arrs)

    def body(*refs):
        ins, outs = refs[:n], refs[n:2 * n]
        send_sems, recv_sems = refs[2 * n:]
        x, y, c = _mesh_pos()
        copies = [pltpu.make_async_remote_copy(
            src_ref=ins[a].at[1 - c], dst_ref=outs[a], send_sem=send_sems.at[a], recv_sem=recv_sems.at[a],
            device_id=(x, y, 1 - c), device_id_type=MESH) for a in range(n)]
        for cp in copies:
            cp.start()
        for cp in copies:
            cp.wait()

    return pl.pallas_call(
        body, name=name,
        in_specs=[_any_spec()] * n, out_specs=[_any_spec()] * n,
        out_shape=[jax.ShapeDtypeStruct(a.shape[1:], a.dtype) for a in arrs],
        scratch_shapes=[pltpu.SemaphoreType.DMA((n,)), pltpu.SemaphoreType.DMA((n,))],
    )(*arrs)


def _rs_ici(arrs, name):
    n = len(arrs)

    def body(*refs):
        ins, outs = refs[:n], refs[n:2 * n]
        send_sems, recv_sems = refs[2 * n:]
        x, y, c = _mesh_pos()
        chips = [(1 - x, y), (x, 1 - y), (1 - x, 1 - y)]
        copies = []
        for a in range(n):
            for j, chip in enumerate(chips):
                copies.append(pltpu.make_async_remote_copy(
                    src_ref=ins[a].at[2 * chip[0] + chip[1]], dst_ref=outs[a].at[j],
                    send_sem=send_sems.at[a, j], recv_sem=recv_sems.at[a, j],
                    device_id=(*chip, c), device_id_type=MESH))
        for cp in copies:
            cp.start()
        for cp in copies:
            cp.wait()

    return pl.pallas_call(
        body, name=name,
        in_specs=[_any_spec()] * n, out_specs=[_any_spec()] * n,
        out_shape=[jax.ShapeDtypeStruct((3,) + a.shape[1:], a.dtype) for a in arrs],
        scratch_shapes=[pltpu.SemaphoreType.DMA((n, 3)), pltpu.SemaphoreType.DMA((n, 3))],
    )(*arrs)


def _permute_w_in(w):
    z = lambda n: jnp.zeros((w.shape[0], n), w.dtype)
    return jnp.concatenate([
        w[:, 0:192], z(64), w[:, 192:384], z(64), w[:, 384:768], w[:, 768:1152], w[:, 1168:1680],
        w[:, 1680:2064], w[:, 2064:2448], w[:, 2448:2832], w[:, 1152:1168], z(112)], axis=1)


def _unpermute_dw_in(dw):
    return jnp.concatenate([
        dw[:, OQ:OQ + 192], dw[:, OKK:OKK + 192], dw[:, OV:OV + 384], dw[:, OG:OG + 384],
        dw[:, OLR:OLR + 16], dw[:, OCU:OCU + 512], dw[:, OAQ:OAQ + 384], dw[:, OAK:OAK + 384],
        dw[:, OAV:OAV + 384]], axis=1)


def _pad_to(a, shape):
    return jnp.pad(a, [(0, s - d) for d, s in zip(a.shape, shape)])


SMALL_LAYOUT = (
    ("norm_mix", 2, 1024), ("norm_ffn", 2, 1024), ("norm_final", 1, 1024), ("gla_norm", 2, 384),
    ("b_gla_gate", 2, 192), ("b_dw", 2, 256), ("conv_ln_g", 2, 256), ("conv_ln_b", 2, 256),
    ("rel_bias", 12, 257), ("w_gla_gate", 32, 192), ("w_dw", 62, 256),
)


def _pack_small(parts):
    rows = [_pad_to(parts[name], (r, 1024)) for name, r, _ in SMALL_LAYOUT]
    used = sum(r for _, r, _ in SMALL_LAYOUT)
    rows.append(jnp.zeros((SMALL_ROWS - used, 1024), F32))
    return jnp.concatenate(rows, axis=0)


def _unpack_small(packed):
    out, r0 = {}, 0
    for name, r, lanes in SMALL_LAYOUT:
        out[name] = packed[r0:r0 + r, 0:lanes]
        r0 += r
    return out


def _local_grads(x, target, weights):
    T = x.shape[0]
    h = x
    saved = []
    for wl in weights:
        q, k, v, g, cu, aq, ak, av, lr = _inproj_fwd(h, wl["norm_mix"], wl["w_in"])
        o_gla, states = _gla_fwd(q, k, v, g, lr, wl["wg"], wl["bg"], wl["gn"])
        o_conv = _conv_fwd(cu, wl["w_dw"], wl["b_dw"], wl["ln_g"], wl["ln_b"])
        bias = _relbias_expand(wl["rb"])
        kpad = jnp.pad(ak, ((AK_PAD, 0), (0, 0)))
        vpad = jnp.pad(av, ((AK_PAD, 0), (0, 0)))
        o_att = _att_fwd(aq, kpad, vpad, bias)
        h1, xn2, h2 = _outproj_mlp_fwd(h, o_gla, o_conv, o_att, wl["w_out"], wl["norm_ffn"],
                                       wl["w_up"], wl["w_down"])
        saved.append(dict(h=h, q=q, k=k, v=v, g=g, cu=cu, aq=aq, kpad=kpad, vpad=vpad, lr=lr,
                          o_gla=o_gla, o_conv=o_conv, o_att=o_att, states=states, bias=bias,
                          h1=h1, xn2=xn2))
        h = h2
    return h, saved


def _local_backward(h_final, saved, weights, norm_final, target):
    loss8, dh, d_nf = _loss_fwd_bwd(h_final, norm_final, target)
    grads = [None] * len(weights)
    for l in reversed(range(len(weights))):
        wl, sv = weights[l], saved[l]
        dh1, dw_up, dw_down, d_nffn = _mlp_bwd(sv["xn2"], sv["h1"], dh, wl["norm_ffn"], wl["w_up"], wl["w_down"])
        d_ogla, d_oconv, d_oatt, dw_out = _outproj_bwd(dh1, sv["o_gla"], sv["o_conv"], sv["o_att"], wl["w_out"])
        daq, dak, dav, dbias = _att_bwd(sv["aq"], sv["kpad"], sv["vpad"], sv["bias"], d_oatt)
        d_rb = _relbias_grad(dbias)
        dcu, dw_dw, db_dw, dln_g, dln_b = _conv_bwd(sv["cu"], d_oconv, wl["w_dw"], wl["b_dw"], wl["ln_g"], wl["ln_b"])
        dq, dk, dv, dg, dlr, dwg, dbg, dgn = _gla_bwd(sv["q"], sv["k"], sv["v"], sv["g"], sv["lr"], sv["states"],
                                                      d_ogla, wl["wg"], wl["bg"], wl["gn"])
        dh, dw_in, d_nmix = _inproj_bwd(sv["h"], dh1, wl["norm_mix"], wl["w_in"],
                                        (dq, dk, dv, dg, dcu, daq, dak, dav, dlr))
        grads[l] = dict(w_in=dw_in, w_out=dw_out, w_up=dw_up, w_down=dw_down, norm_mix=d_nmix, norm_ffn=d_nffn,
                        wg=dwg, bg=dbg, gn=dgn, w_dw=dw_dw, b_dw=db_dw, ln_g=dln_g, ln_b=dln_b, rb=d_rb)
    return loss8, dh, grads, d_nf


def _layer_weights(l, w_in_p, w_out, w_up, w_down, w_dw_full, norm_mix, w_gla_gate, b_gla_gate, gla_norm,
                   b_dw, conv_ln_g, conv_ln_b, rel_bias, norm_ffn):
    return dict(
        w_in=w_in_p, w_out=w_out, w_up=w_up, w_down=w_down,
        norm_mix=norm_mix[l][None, :], norm_ffn=norm_ffn[l][None, :],
        wg=_pad_to(w_gla_gate[l], (128, 256)).astype(BF16), bg=_pad_to(b_gla_gate[l][None, :], (1, 256)),
        gn=gla_norm[l][None, :], w_dw=_pad_to(w_dw_full, (32, 256)), b_dw=b_dw[l][None, :],
        ln_g=conv_ln_g[l][None, :], ln_b=conv_ln_b[l][None, :], rb=_pad_to(rel_bias[l], (8, 384)))


def kernel(x, norm_mix, w_in, w_gla_gate, b_gla_gate, gla_norm, w_dw, b_dw, conv_ln_g, conv_ln_b, rel_bias, w_out, norm_ffn, w_up, w_down, norm_final, loss_target, m_norm_mix, m_w_in, m_w_gla_gate, m_b_gla_gate, m_gla_norm, m_w_dw, m_b_dw, m_conv_ln_g, m_conv_ln_b, m_rel_bias, m_w_out, m_norm_ffn, m_w_up, m_w_down, m_norm_final, v_norm_mix, v_w_in, v_w_gla_gate, v_b_gla_gate, v_gla_norm, v_w_dw, v_b_dw, v_conv_ln_g, v_conv_ln_b, v_rel_bias, v_w_out, v_norm_ffn, v_w_up, v_w_down, v_norm_final):
    mx, my, mc = _mesh_pos()
    me = 4 * mx + 2 * my + mc
    chip_idx = (2 * mx + my).astype(jnp.int32).reshape(1)
    core_idx = mc.astype(jnp.int32).reshape(1)

    shards = []
    for l in range(DEPTH):
        shards += [w_in[l].astype(BF16), w_out[l].astype(BF16), w_up[l].astype(BF16), w_down[l].astype(BF16)]
    dw_flat = _pad_to(w_dw, (DEPTH, 32, 32)).reshape(16, 128)
    gathered = _allgather(shards + [dw_flat], "allgather_weights")
    dw_all = gathered[-1].reshape(N_DEV, DEPTH, 32, 32)[:, :, :KCONV, :]
    dw_all = jnp.transpose(dw_all, (1, 2, 0, 3)).reshape(DEPTH, KCONV, 256)

    weights = []
    for l in range(DEPTH):
        g_in, g_out, g_up, g_down = gathered[4 * l:4 * l + 4]
        w_in_full = jnp.transpose(g_in, (1, 0, 2)).reshape(D, N_DEV * 354)
        weights.append(_layer_weights(
            l, _permute_w_in(w_in_full), g_out.reshape(D, D), g_up, g_down, dw_all[l],
            norm_mix, w_gla_gate, b_gla_gate, gla_norm, b_dw, conv_ln_g, conv_ln_b, rel_bias, norm_ffn))

    h_final, saved = _local_grads(x[0], loss_target[0], weights)
    loss8, dx, grads, d_nf = _local_backward(h_final, saved, weights, norm_final[None, :], loss_target[0])
    loss = lax.psum(loss8[0, 0], ("x", "y", "c"))

    big = []
    for l in range(DEPTH):
        gl = grads[l]
        dwi = _unpermute_dw_in(gl["w_in"]).reshape(D, 4, 2, 354)
        big += [jnp.transpose(dwi, (2, 1, 0, 3)), gl["w_out"], gl["w_up"], gl["w_down"]]
    r1 = _rs_swap(big, "rs_swap")
    rows_blk = (256, 128, 256, 256)
    pair = [_pair_sum(big[t], r1[t], core_idx, rows_blk[t % 4]) for t in range(len(big))]
    r2 = _rs_ici(pair, "rs_ici")

    big_w = (w_in, w_out, w_up, w_down)
    big_m = (m_w_in, m_w_out, m_w_up, m_w_down)
    big_v = (v_w_in, v_w_out, v_w_up, v_w_down)
    big_res = [[None] * DEPTH for _ in range(4)]
    for l in range(DEPTH):
        for t in range(4):
            big_res[t][l] = _rs_adamw(pair[4 * l + t], r2[4 * l + t], big_w[t][l], big_m[t][l], big_v[t][l],
                                      chip_idx, rows_blk[t])
    big_out = [[jnp.stack([big_res[t][l][o] for l in range(DEPTH)]) for o in range(4)] for t in range(4)]

    parts = dict(
        norm_mix=jnp.concatenate([grads[l]["norm_mix"] for l in range(DEPTH)], axis=0),
        norm_ffn=jnp.concatenate([grads[l]["norm_ffn"] for l in range(DEPTH)], axis=0),
        norm_final=d_nf,
        gla_norm=jnp.concatenate([grads[l]["gn"] for l in range(DEPTH)], axis=0),
        b_gla_gate=jnp.concatenate([grads[l]["bg"][:, :192] for l in range(DEPTH)], axis=0),
        b_dw=jnp.concatenate([grads[l]["b_dw"] for l in range(DEPTH)], axis=0),
        conv_ln_g=jnp.concatenate([grads[l]["ln_g"] for l in range(DEPTH)], axis=0),
        conv_ln_b=jnp.concatenate([grads[l]["ln_b"] for l in range(DEPTH)], axis=0),
        rel_bias=jnp.concatenate([grads[l]["rb"][:6, :N_REL] for l in range(DEPTH)], axis=0),
        w_gla_gate=jnp.concatenate([grads[l]["wg"][:16, :192] for l in range(DEPTH)], axis=0),
        w_dw=jnp.concatenate([grads[l]["w_dw"][:KCONV] for l in range(DEPTH)], axis=0),
    )
    small_all = _allgather([_pack_small(parts)], "allgather_small")[0]
    sg = _unpack_small(_small_sum(small_all))
    dw_grad = lax.dynamic_slice_in_dim(sg["w_dw"].reshape(DEPTH, KCONV, 256), me * 32, 32, axis=2)
    small_g = dict(
        norm_mix=sg["norm_mix"], w_gla_gate=sg["w_gla_gate"].reshape(DEPTH, 16, 192), b_gla_gate=sg["b_gla_gate"],
        gla_norm=sg["gla_norm"], w_dw=dw_grad, b_dw=sg["b_dw"], conv_ln_g=sg["conv_ln_g"],
        conv_ln_b=sg["conv_ln_b"], rel_bias=sg["rel_bias"].reshape(DEPTH, 6, N_REL), norm_ffn=sg["norm_ffn"],
        norm_final=sg["norm_final"].reshape(D))
    small_names = ("norm_mix", "w_gla_gate", "b_gla_gate", "gla_norm", "w_dw", "b_dw", "conv_ln_g", "conv_ln_b",
                   "rel_bias", "norm_ffn", "norm_final")
    small_w = dict(norm_mix=norm_mix, w_gla_gate=w_gla_gate, b_gla_gate=b_gla_gate, gla_norm=gla_norm, w_dw=w_dw,
                   b_dw=b_dw, conv_ln_g=conv_ln_g, conv_ln_b=conv_ln_b, rel_bias=rel_bias, norm_ffn=norm_ffn,
                   norm_final=norm_final)
    small_m = dict(norm_mix=m_norm_mix, w_gla_gate=m_w_gla_gate, b_gla_gate=m_b_gla_gate, gla_norm=m_gla_norm,
                   w_dw=m_w_dw, b_dw=m_b_dw, conv_ln_g=m_conv_ln_g, conv_ln_b=m_conv_ln_b, rel_bias=m_rel_bias,
                   norm_ffn=m_norm_ffn, norm_final=m_norm_final)
    small_v = dict(norm_mix=v_norm_mix, w_gla_gate=v_w_gla_gate, b_gla_gate=v_b_gla_gate, gla_norm=v_gla_norm,
                   w_dw=v_w_dw, b_dw=v_b_dw, conv_ln_g=v_conv_ln_g, conv_ln_b=v_conv_ln_b, rel_bias=v_rel_bias,
                   norm_ffn=v_norm_ffn, norm_final=v_norm_final)
    s_delta, s_m, s_v = _adamw_small([small_w[n] for n in small_names], [small_g[n] for n in small_names],
                                     [small_m[n] for n in small_names], [small_v[n] for n in small_names])
    s_idx = {n: t for t, n in enumerate(small_names)}

    order = ("norm_mix", "w_in", "w_gla_gate", "b_gla_gate", "gla_norm", "w_dw", "b_dw", "conv_ln_g", "conv_ln_b",
             "rel_bias", "w_out", "norm_ffn", "w_up", "w_down", "norm_final")
    big_idx = {"w_in": 0, "w_out": 1, "w_up": 2, "w_down": 3}

    def pick(kind, name):
        if name in big_idx:
            return big_out[big_idx[name]][kind]
        t = s_idx[name]
        return (small_g[name], s_delta[t], s_m[t], s_v[t])[kind]

    outs = [loss, dx[None]]
    for kind in range(4):
        outs += [pick(kind, n) for n in order]
    return tuple(outs)
```

```python
import functools

import jax
import jax.numpy as jnp
from jax import lax
from jax.experimental import pallas as pl
from jax.experimental.pallas import tpu as pltpu

F32 = jnp.float32
BF16 = jnp.bfloat16
MESH = pl.DeviceIdType.MESH

D = 1024
DEPTH = 2
CH = 64
EPS = 1e-6
NEG = -1e30
N_DEV = 8
N_REL = 257
Q_SCALE = 48.0 ** -0.5
A_SCALE = 64.0 ** -0.5
GATE_TAU = 16.0
KCONV = 31

OQ, OKK, OV, OG, OCU, OAQ, OAK, OAV, OLR, DINP = 0, 256, 512, 896, 1280, 1792, 2176, 2560, 2944, 3072
IN_GROUPS = ((OQ, 256), (OKK, 256), (OV, 384), (OG, 384), (OCU, 512), (OAQ, 384), (OAK, 384), (OAV, 384), (OLR, 128))

AQ_BLK = 256
AK_WIN = 768
AK_PAD = 512
RB_W = 1536

ADAM_LR, ADAM_B1, ADAM_B2, ADAM_EPS, ADAM_WD, ADAM_STEP = 0.001, 0.9, 0.999, 1e-08, 0.01, 10


V7X_VMEM_MIB = 64
VMEM_LIMIT_MIB = V7X_VMEM_MIB - 1


def _cp(sem=None):
    kw = {"vmem_limit_bytes": VMEM_LIMIT_MIB * 1024 * 1024}
    if sem is not None:
        kw["dimension_semantics"] = sem
    return pltpu.CompilerParams(**kw)


def _dot(a, b):
    return jnp.dot(a, b, preferred_element_type=F32)


def _dot_nt(a, b):
    return lax.dot_general(a, b, (((1,), (1,)), ((), ())), preferred_element_type=F32)


def _dot_tn(a, b):
    return lax.dot_general(a, b, (((0,), (0,)), ((), ())), preferred_element_type=F32)


def _split2(a):
    hi = a.astype(BF16)
    lo = (a - hi.astype(F32)).astype(BF16)
    return hi, lo


def _split3(a):
    hi = a.astype(BF16)
    r1 = a - hi.astype(F32)
    mid = r1.astype(BF16)
    lo = (r1 - mid.astype(F32)).astype(BF16)
    return hi, mid, lo


def _sigmoid(x):
    return 1.0 / (1.0 + jnp.exp(-x))


def _group(idx, size, n):
    g = jnp.zeros_like(idx)
    for t in range(1, n):
        g = g + (idx >= t * size).astype(jnp.int32)
    return g


def _rms_bwd(dy, x, r, gamma):
    xh = x * r
    dxh = dy * gamma
    dx = r * (dxh - xh * jnp.mean(dxh * xh, axis=-1, keepdims=True))
    return dx, jnp.sum(dy * xh, axis=0, keepdims=True)


def _row_spec(tm, n):
    return pl.BlockSpec((tm, n), lambda i: (i, 0))


def _full_spec(shape):
    nd = len(shape)
    return pl.BlockSpec(shape, lambda *_: (0,) * nd)


def _any_spec():
    return pl.BlockSpec(memory_space=pl.ANY)


class _Job:
    def __init__(self, operands, out_shapes, sems, start, finish, aliases=None):
        self.operands, self.out_shapes, self.sems = list(operands), list(out_shapes), list(sems)
        self.start, self.finish, self.aliases = start, finish, dict(aliases or {})


def _pcall(body, *, name, grid, in_specs, out_specs, out_shape, operands, scratch_shapes=(), sem=None, jobs=()):
    jobs = list(jobs)
    in_specs, out_specs, out_shape = list(in_specs), list(out_specs), list(out_shape)
    scratch_shapes = list(scratch_shapes)
    n_in, n_out, n_scr = len(in_specs), len(out_specs), len(scratch_shapes)
    j_in = [a for j in jobs for a in j.operands]
    j_out = [s for j in jobs for s in j.out_shapes]
    j_sem = [s for j in jobs for s in j.sems]
    aliases, io, oo = {}, n_in, n_out
    for j in jobs:
        for a, b in j.aliases.items():
            aliases[io + a] = oo + b
        io += len(j.operands)
        oo += len(j.out_shapes)

    def wrapped(*refs):
        own_in, ji = refs[:n_in], refs[n_in:n_in + len(j_in)]
        o0 = n_in + len(j_in)
        own_out, jo = refs[o0:o0 + n_out], refs[o0 + n_out:o0 + n_out + len(j_out)]
        s0 = o0 + n_out + len(j_out)
        own_scr, js = refs[s0:s0 + n_scr], refs[s0 + n_scr:]

        def each_job(fn_name):
            a = b = c = 0
            for j in jobs:
                na, nb, nc = len(j.operands), len(j.out_shapes), len(j.sems)
                getattr(j, fn_name)(ji[a:a + na], jo[b:b + nb], js[c:c + nc])
                a, b, c = a + na, b + nb, c + nc

        if jobs and grid:
            pids = [pl.program_id(d) for d in range(len(grid))]
            first = functools.reduce(jnp.logical_and, [p == 0 for p in pids])
            last = functools.reduce(jnp.logical_and, [p == g - 1 for p, g in zip(pids, grid)])
            pl.when(first)(lambda: each_job("start"))
        elif jobs:
            each_job("start")

        body(*own_in, *own_out, *own_scr)

        if jobs and grid:
            pl.when(last)(lambda: each_job("finish"))
        elif jobs:
            each_job("finish")

    res = pl.pallas_call(
        wrapped, name=name, grid=grid,
        in_specs=in_specs + [_any_spec()] * len(j_in), out_specs=out_specs + [_any_spec()] * len(j_out),
        out_shape=out_shape + j_out, scratch_shapes=scratch_shapes + j_sem,
        input_output_aliases=aliases, compiler_params=_cp(sem),
    )(*operands, *j_in)
    return res[:n_out], res[n_out:]


def _inproj_fwd(h, gamma, w):
    T = h.shape[0]
    tm = 512

    def body(h_ref, g_ref, w_ref, *outs):
        x = h_ref[...]
        r = lax.rsqrt(jnp.mean(x * x, axis=-1, keepdims=True) + EPS)
        xn = (x * r * g_ref[...]).astype(BF16)
        p = _dot(xn, w_ref[...])
        for o_ref, (off, n) in zip(outs, IN_GROUPS):
            o_ref[...] = p[:, off:off + n].astype(BF16)

    return pl.pallas_call(
        body, name="inproj_fwd", grid=(T // tm,),
        in_specs=[_row_spec(tm, D), _full_spec((1, D)), _full_spec((D, DINP))],
        out_specs=[_row_spec(tm, n) for _, n in IN_GROUPS],
        out_shape=[jax.ShapeDtypeStruct((T, n), BF16) for _, n in IN_GROUPS],
        compiler_params=_cp(("parallel",)),
    )(h, gamma, w)


def _inproj_bwd(h, dh_in, gamma, w, dparts):
    T = h.shape[0]
    tm = 256
    nt = T // tm

    def body(h_ref, dhin_ref, g_ref, w_ref, *rest):
        dp_refs = rest[:9]
        dh_ref, dw_ref, dg_ref, acc = rest[9:]
        i = pl.program_id(0)

        @pl.when(i == 0)
        def _():
            acc[...] = jnp.zeros_like(acc)
            dg_ref[...] = jnp.zeros_like(dg_ref)

        x = h_ref[...]
        r = lax.rsqrt(jnp.mean(x * x, axis=-1, keepdims=True) + EPS)
        gamma_ = g_ref[...]
        xn = (x * r * gamma_).astype(BF16)
        dxn = jnp.zeros((tm, D), F32)
        for d_ref, (off, n) in zip(dp_refs, IN_GROUPS):
            d = d_ref[...]
            acc[:, off:off + n] += _dot_tn(xn, d)
            dxn = dxn + _dot_nt(d, w_ref[:, off:off + n])
        dx, dgam = _rms_bwd(dxn, x, r, gamma_)
        dh_ref[...] = dhin_ref[...] + dx
        dg_ref[...] += dgam

        @pl.when(i == nt - 1)
        def _():
            dw_ref[...] = acc[...].astype(BF16)

    return pl.pallas_call(
        body, name="inproj_bwd", grid=(nt,),
        in_specs=[_row_spec(tm, D), _row_spec(tm, D), _full_spec((1, D)), _full_spec((D, DINP))]
        + [_row_spec(tm, n) for _, n in IN_GROUPS],
        out_specs=[_row_spec(tm, D), _full_spec((D, DINP)), _full_spec((1, D))],
        out_shape=[jax.ShapeDtypeStruct((T, D), F32), jax.ShapeDtypeStruct((D, DINP), BF16),
                   jax.ShapeDtypeStruct((1, D), F32)],
        scratch_shapes=[pltpu.VMEM((D, DINP), F32)],
        compiler_params=_cp(("arbitrary",)),
    )(h, dh_in, gamma, w, *dparts)


GLA_ROWS = 512
GLA_NC = GLA_ROWS // CH


def _gla_consts():
    ri = lax.broadcasted_iota(jnp.int32, (CH, CH), 0)
    ci = lax.broadcasted_iota(jnp.int32, (CH, CH), 1)
    upper = (ci > ri).astype(BF16)
    vv = lax.broadcasted_iota(jnp.int32, (384, 256), 0)
    kk = lax.broadcasted_iota(jnp.int32, (384, 256), 1)
    mask_t = ((_group(vv, 96, 4) == _group(kk, 48, 4)) & (kk < 192)).astype(F32)
    pi = lax.broadcasted_iota(jnp.int32, (384, 384), 0)
    pj = lax.broadcasted_iota(jnp.int32, (384, 384), 1)
    same_head = (_group(pi, 96, 4) == _group(pj, 96, 4)).astype(BF16)
    return upper, mask_t, same_head


def _gla_gate(lr_ref, wg_ref, bg_ref):
    z = _dot(lr_ref[...], wg_ref[...]) + bg_ref[...]
    la = (jnp.minimum(z, 0.0) - jnp.log(1.0 + jnp.exp(-jnp.abs(z)))) * (1.0 / GATE_TAU)
    return z, la


def _gla_chunk_decay(la_c, upper):
    hi, lo = _split2(la_c)
    dec = _dot(upper, hi) + _dot(upper, lo)
    end = jnp.sum(la_c, axis=0, keepdims=True)
    return jnp.exp(dec), jnp.exp(end)


def _head_mean(x, same_head):
    hi, lo = _split2(x)
    return (_dot(hi, same_head) + _dot(lo, same_head)) * (1.0 / 96.0)


def _gla_fwd(q, k, v, g, lr, wg, bg, gn, jobs=()):
    T = q.shape[0]
    nb = T // GLA_ROWS

    def body(q_ref, k_ref, v_ref, g_ref, lr_ref, wg_ref, bg_ref, gn_ref, y_ref, st_ref, s_scr, o_scr):
        upper, mask_t, same_head = _gla_consts()

        @pl.when(pl.program_id(0) == 0)
        def _():
            s_scr[...] = jnp.zeros_like(s_scr)

        _, la = _gla_gate(lr_ref, wg_ref, bg_ref)
        for c in range(GLA_NC):
            rs = slice(c * CH, (c + 1) * CH)
            w, a = _gla_chunk_decay(la[rs], upper)
            kd = (k_ref[rs, :].astype(F32) * w).astype(BF16)
            kv_t = _dot_tn(v_ref[rs, :], kd)
            s_new = s_scr[...] * a + kv_t * mask_t
            s_scr[...] = s_new
            sb = s_new.astype(BF16)
            st_ref[c] = sb
            qs = (q_ref[rs, :].astype(F32) * Q_SCALE).astype(BF16)
            o_scr[rs, :] = _dot_nt(qs, sb)
        o = o_scr[...]
        r = lax.rsqrt(_head_mean(o * o, same_head) + EPS)
        gf = g_ref[...].astype(F32)
        y_ref[...] = (o * r * gn_ref[...] * (gf * _sigmoid(gf))).astype(BF16)

    return _pcall(
        body, name="gla_fwd", grid=(nb,),
        in_specs=[_row_spec(GLA_ROWS, 256), _row_spec(GLA_ROWS, 256), _row_spec(GLA_ROWS, 384),
                  _row_spec(GLA_ROWS, 384), _row_spec(GLA_ROWS, 128),
                  _full_spec((128, 256)), _full_spec((1, 256)), _full_spec((1, 384))],
        out_specs=[_row_spec(GLA_ROWS, 384), pl.BlockSpec((GLA_NC, 384, 256), lambda i: (i, 0, 0))],
        out_shape=[jax.ShapeDtypeStruct((T, 384), BF16), jax.ShapeDtypeStruct((T // CH, 384, 256), BF16)],
        scratch_shapes=[pltpu.VMEM((384, 256), F32), pltpu.VMEM((GLA_ROWS, 384), F32)],
        sem=("arbitrary",), operands=(q, k, v, g, lr, wg, bg, gn), jobs=jobs)


def _gla_bwd(q, k, v, g, lr, states, dy, wg, bg, gn):
    T = q.shape[0]
    nb = T // GLA_ROWS

    def rev(s):
        return nb - 1 - s

    def body(q_ref, k_ref, v_ref, g_ref, lr_ref, st_ref, stp_ref, dy_ref, wg_ref, bg_ref, gn_ref,
             dq_ref, dk_ref, dv_ref, dg_ref, dlr_ref, dwg_ref, dbg_ref, dgn_ref,
             d_scr, an_scr, o_scr, do_scr, dla_scr):
        upper, mask_t, same_head = _gla_consts()
        s = pl.program_id(0)
        blk = rev(s)

        @pl.when(s == 0)
        def _():
            d_scr[...] = jnp.zeros_like(d_scr)
            an_scr[...] = jnp.zeros_like(an_scr)
            dwg_ref[...] = jnp.zeros_like(dwg_ref)
            dbg_ref[...] = jnp.zeros_like(dbg_ref)
            dgn_ref[...] = jnp.zeros_like(dgn_ref)

        z, la = _gla_gate(lr_ref, wg_ref, bg_ref)
        ws, as_, qss, kds = [], [], [], []
        for c in range(GLA_NC):
            rs = slice(c * CH, (c + 1) * CH)
            w, a = _gla_chunk_decay(la[rs], upper)
            ws.append(w)
            as_.append(a)
            qs = (q_ref[rs, :].astype(F32) * Q_SCALE).astype(BF16)
            qss.append(qs)
            kds.append((k_ref[rs, :].astype(F32) * w).astype(BF16))
            o_scr[rs, :] = _dot_nt(qs, st_ref[c])
        o = o_scr[...]
        r = lax.rsqrt(_head_mean(o * o, same_head) + EPS)
        on = o * r
        gf = g_ref[...].astype(F32)
        sg = _sigmoid(gf)
        si = gf * sg
        dyf = dy_ref[...].astype(F32)
        gn_ = gn_ref[...]
        dgn_ref[...] += jnp.sum(dyf * si * on, axis=0, keepdims=True)
        dg_ref[...] = (dyf * on * gn_ * (sg * (1.0 + gf * (1.0 - sg)))).astype(BF16)
        d_on = dyf * si * gn_
        do_scr[...] = r * (d_on - on * _head_mean(d_on * on, same_head))

        first = (blk > 0).astype(F32)
        for c in reversed(range(GLA_NC)):
            rs = slice(c * CH, (c + 1) * CH)
            dob = do_scr[rs, :].astype(BF16)
            sb = st_ref[c]
            if c > 0:
                s_prev = st_ref[c - 1].astype(F32)
            else:
                s_prev = stp_ref[0].astype(F32) * first
            dq_ref[rs, :] = (_dot(dob, sb) * Q_SCALE).astype(BF16)
            dt = d_scr[...] * an_scr[...] + _dot_tn(dob, qss[c]) * mask_t
            d_scr[...] = dt
            da = jnp.sum(dt * s_prev, axis=0, keepdims=True)
            db = dt.astype(BF16)
            dkd = _dot(v_ref[rs, :], db)
            dv_ref[rs, :] = _dot_nt(kds[c], db).astype(BF16)
            dk_ref[rs, :] = (dkd * ws[c]).astype(BF16)
            ddec = dkd * k_ref[rs, :].astype(F32) * ws[c]
            hi, lo = _split2(ddec)
            dla_scr[rs, :] = _dot_tn(upper, hi) + _dot_tn(upper, lo) + as_[c] * da
            an_scr[...] = as_[c]

        dz = dla_scr[...] * (1.0 - _sigmoid(z)) * (1.0 / GATE_TAU)
        dzb = dz.astype(BF16)
        dlr_ref[...] = _dot_nt(dzb, wg_ref[...]).astype(BF16)
        dwg_ref[...] += _dot_tn(lr_ref[...], dzb)
        dbg_ref[...] += jnp.sum(dz, axis=0, keepdims=True)

    def rspec(n):
        return pl.BlockSpec((GLA_ROWS, n), lambda s: (rev(s), 0))

    return pl.pallas_call(
        body, name="gla_bwd", grid=(nb,),
        in_specs=[rspec(256), rspec(256), rspec(384), rspec(384), rspec(128),
                  pl.BlockSpec((GLA_NC, 384, 256), lambda s: (rev(s), 0, 0)),
                  pl.BlockSpec((1, 384, 256), lambda s: (jnp.maximum(rev(s) * GLA_NC - 1, 0), 0, 0)),
                  rspec(384), _full_spec((128, 256)), _full_spec((1, 256)), _full_spec((1, 384))],
        out_specs=[rspec(256), rspec(256), rspec(384), rspec(384), rspec(128),
                   _full_spec((128, 256)), _full_spec((1, 256)), _full_spec((1, 384))],
        out_shape=[jax.ShapeDtypeStruct((T, 256), BF16), jax.ShapeDtypeStruct((T, 256), BF16),
                   jax.ShapeDtypeStruct((T, 384), BF16), jax.ShapeDtypeStruct((T, 384), BF16),
                   jax.ShapeDtypeStruct((T, 128), BF16),
                   jax.ShapeDtypeStruct((128, 256), F32), jax.ShapeDtypeStruct((1, 256), F32),
                   jax.ShapeDtypeStruct((1, 384), F32)],
        scratch_shapes=[pltpu.VMEM((384, 256), F32), pltpu.VMEM((1, 256), F32),
                        pltpu.VMEM((GLA_ROWS, 384), F32), pltpu.VMEM((GLA_ROWS, 384), F32),
                        pltpu.VMEM((GLA_ROWS, 256), F32)],
        compiler_params=_cp(("arbitrary",)),
    )(q, k, v, g, lr, states, states, dy, wg, bg, gn)


CONV_ROWS = 512
HALO = 32


def _conv_common(cu_ref, halo_ref, w_ref, b_ref, lg_ref, lb_ref, buf, blk):
    u = cu_ref[...].astype(F32)
    a = u[:, :256]
    sb = _sigmoid(u[:, 256:])
    uh = halo_ref[...].astype(F32)
    hh = uh[:, :256] * _sigmoid(uh[:, 256:]) * (blk > 0).astype(F32)
    buf[0:HALO, :] = hh
    buf[HALO:HALO + CONV_ROWS, :] = a * sb
    acc = jnp.zeros((CONV_ROWS, 256), F32)
    for j in range(KCONV):
        acc = acc + w_ref[j:j + 1, :] * buf[pl.ds(HALO - (KCONV - 1) + j, CONV_ROWS), :]
    cc = acc + b_ref[...]
    mu = jnp.mean(cc, axis=-1, keepdims=True)
    xc = cc - mu
    rstd = lax.rsqrt(jnp.mean(xc * xc, axis=-1, keepdims=True) + EPS)
    n = xc * rstd
    yln = n * lg_ref[...] + lb_ref[...]
    return a, sb, n, rstd, yln


def _conv_fwd(cu, w, b, lg, lb):
    T = cu.shape[0]
    nb = T // CONV_ROWS
    per = CONV_ROWS // HALO

    def body(cu_ref, halo_ref, w_ref, b_ref, lg_ref, lb_ref, y_ref, buf):
        _, _, _, _, yln = _conv_common(cu_ref, halo_ref, w_ref, b_ref, lg_ref, lb_ref, buf, pl.program_id(0))
        y_ref[...] = (yln * _sigmoid(yln)).astype(BF16)

    return pl.pallas_call(
        body, name="conv_fwd", grid=(nb,),
        in_specs=[_row_spec(CONV_ROWS, 512),
                  pl.BlockSpec((HALO, 512), lambda i: (jnp.maximum(i * per - 1, 0), 0)),
                  _full_spec((32, 256)), _full_spec((1, 256)), _full_spec((1, 256)), _full_spec((1, 256))],
        out_specs=_row_spec(CONV_ROWS, 256),
        out_shape=jax.ShapeDtypeStruct((T, 256), BF16),
        scratch_shapes=[pltpu.VMEM((CONV_ROWS + HALO, 256), F32)],
        compiler_params=_cp(("parallel",)),
    )(cu, cu, w, b, lg, lb)


def _conv_bwd(cu, dy, w, b, lg, lb):
    T = cu.shape[0]
    nb = T // CONV_ROWS
    per = CONV_ROWS // HALO

    def rev(s):
        return nb - 1 - s

    def body(cu_ref, halo_ref, dy_ref, w_ref, b_ref, lg_ref, lb_ref,
             dcu_ref, dw_ref, db_ref, dlg_ref, dlb_ref, buf, dcbuf, carry):
        s = pl.program_id(0)

        @pl.when(s == 0)
        def _():
            carry[...] = jnp.zeros_like(carry)
            dw_ref[...] = jnp.zeros_like(dw_ref)
            db_ref[...] = jnp.zeros_like(db_ref)
            dlg_ref[...] = jnp.zeros_like(dlg_ref)
            dlb_ref[...] = jnp.zeros_like(dlb_ref)

        a, sb, n, rstd, yln = _conv_common(cu_ref, halo_ref, w_ref, b_ref, lg_ref, lb_ref, buf, rev(s))
        sg = _sigmoid(yln)
        dyln = dy_ref[...].astype(F32) * (sg * (1.0 + yln * (1.0 - sg)))
        dlg_ref[...] += jnp.sum(dyln * n, axis=0, keepdims=True)
        dlb_ref[...] += jnp.sum(dyln, axis=0, keepdims=True)
        dn = dyln * lg_ref[...]
        dc = rstd * (dn - jnp.mean(dn, axis=-1, keepdims=True) - n * jnp.mean(dn * n, axis=-1, keepdims=True))
        db_ref[...] += jnp.sum(dc, axis=0, keepdims=True)
        dcbuf[0:CONV_ROWS, :] = dc
        dcbuf[CONV_ROWS:CONV_ROWS + HALO, :] = carry[...]
        dhg = jnp.zeros((CONV_ROWS, 256), F32)
        for j in range(KCONV):
            tap = buf[pl.ds(HALO - (KCONV - 1) + j, CONV_ROWS), :]
            dw_ref[j:j + 1, :] += jnp.sum(dc * tap, axis=0, keepdims=True)
            dhg = dhg + w_ref[j:j + 1, :] * dcbuf[pl.ds(KCONV - 1 - j, CONV_ROWS), :]
        carry[...] = dc[0:HALO, :]
        dcu_ref[...] = jnp.concatenate([dhg * sb, dhg * a * sb * (1.0 - sb)], axis=1).astype(BF16)

    def rspec(n):
        return pl.BlockSpec((CONV_ROWS, n), lambda s: (rev(s), 0))

    return pl.pallas_call(
        body, name="conv_bwd", grid=(nb,),
        in_specs=[rspec(512),
                  pl.BlockSpec((HALO, 512), lambda s: (jnp.maximum(rev(s) * per - 1, 0), 0)),
                  rspec(256),
                  _full_spec((32, 256)), _full_spec((1, 256)), _full_spec((1, 256)), _full_spec((1, 256))],
        out_specs=[rspec(512), _full_spec((32, 256)), _full_spec((1, 256)), _full_spec((1, 256)),
                   _full_spec((1, 256))],
        out_shape=[jax.ShapeDtypeStruct((T, 512), BF16), jax.ShapeDtypeStruct((32, 256), F32),
                   jax.ShapeDtypeStruct((1, 256), F32), jax.ShapeDtypeStruct((1, 256), F32),
                   jax.ShapeDtypeStruct((1, 256), F32)],
        scratch_shapes=[pltpu.VMEM((CONV_ROWS + HALO, 256), F32), pltpu.VMEM((CONV_ROWS + HALO, 256), F32),
                        pltpu.VMEM((HALO, 256), F32)],
        compiler_params=_cp(("arbitrary",)),
    )(cu, cu, dy, w, b, lg, lb)


def _rel_onehot_t(shift=0):
    r = lax.broadcasted_iota(jnp.int32, (384, RB_W), 0)
    n = lax.broadcasted_iota(jnp.int32, (384, RB_W), 1) - shift
    idx = jnp.clip(1024 - n, -128, 128) + 128
    return (idx == r).astype(BF16)


def _relbias_expand(rb):
    def body(rb_ref, out_ref):
        oh = _rel_onehot_t()
        hi, mid, lo = _split3(rb_ref[...])
        strip = _dot(hi, oh) + _dot(mid, oh) + _dot(lo, oh)
        qi = _group(lax.broadcasted_iota(jnp.int32, (AQ_BLK, AK_WIN), 0), CH, 4)
        kj = _group(lax.broadcasted_iota(jnp.int32, (AQ_BLK, AK_WIN), 1), CH, 12)
        valid = (kj >= qi) & (kj <= qi + 8)
        for hd in range(6):
            x = jnp.broadcast_to(strip[hd:hd + 1, :], (AQ_BLK, RB_W))
            xr = pltpu.roll(x, 0, 1, stride=1, stride_axis=0)
            out_ref[hd] = jnp.where(valid, xr[:, 512:512 + AK_WIN], NEG)

    return pl.pallas_call(
        body, name="relbias_expand",
        out_shape=jax.ShapeDtypeStruct((6, AQ_BLK, AK_WIN), F32),
        compiler_params=_cp(),
    )(rb)


def _relbias_grad(dbias):
    def body(db_ref, out_ref):
        oh = _rel_onehot_t(AQ_BLK - 1)
        ri = lax.broadcasted_iota(jnp.int32, (AQ_BLK, AQ_BLK), 0)
        ci = lax.broadcasted_iota(jnp.int32, (AQ_BLK, AQ_BLK), 1)
        flip = (ri + ci == AQ_BLK - 1).astype(BF16)
        rows = []
        for hd in range(6):
            hi, mid, lo = _split3(db_ref[hd])
            rev = _dot(flip, hi) + _dot(flip, mid) + _dot(flip, lo)
            x = jnp.concatenate([jnp.zeros((AQ_BLK, 512), F32), rev,
                                 jnp.zeros((AQ_BLK, RB_W - 512 - AK_WIN), F32)], axis=1)
            xr = pltpu.roll(x, 0, 1, stride=1, stride_axis=0)
            rows.append(jnp.sum(xr, axis=0, keepdims=True))
        rows.append(jnp.zeros((2, RB_W), F32))
        dstrip = jnp.concatenate(rows, axis=0)
        hi, mid, lo = _split3(dstrip)
        out_ref[...] = _dot_nt(hi, oh) + _dot_nt(mid, oh) + _dot_nt(lo, oh)

    return pl.pallas_call(
        body, name="relbias_grad",
        out_shape=jax.ShapeDtypeStruct((8, 384), F32),
        compiler_params=_cp(),
    )(dbias)


def _att_scores(q_h, k_h, bias_h, kvalid):
    s = _dot_nt(q_h, k_h) * A_SCALE + bias_h
    s = jnp.where(kvalid, s, NEG)
    m = jnp.max(s, axis=-1, keepdims=True)
    p = jnp.exp(s - m)
    return p / jnp.sum(p, axis=-1, keepdims=True)


def _att_kvalid(i):
    col = _group(lax.broadcasted_iota(jnp.int32, (1, AK_WIN), 1), CH, 12)
    return col >= 8 - 4 * i


def _att_fwd(q, kpad, vpad, bias, jobs=()):
    T = q.shape[0]
    nb = T // AQ_BLK

    def body(q_ref, k_hbm, v_hbm, b_ref, o_ref, kw_scr, vw_scr):
        i = pl.program_id(0)
        start = pl.multiple_of(i * AQ_BLK, AQ_BLK)
        pltpu.sync_copy(k_hbm.at[pl.ds(start, AK_WIN), :], kw_scr)
        pltpu.sync_copy(v_hbm.at[pl.ds(start, AK_WIN), :], vw_scr)
        kw = kw_scr[...]
        vw = vw_scr[...]
        qb = q_ref[...]
        kvalid = _att_kvalid(i)
        for hd in range(6):
            ls = slice(hd * 64, (hd + 1) * 64)
            p = _att_scores(qb[:, ls], kw[:, ls], b_ref[hd], kvalid)
            o_ref[:, ls] = _dot(p.astype(BF16), vw[:, ls]).astype(BF16)

    return _pcall(
        body, name="att_fwd", grid=(nb,),
        in_specs=[_row_spec(AQ_BLK, 384), _any_spec(), _any_spec(), _full_spec((6, AQ_BLK, AK_WIN))],
        out_specs=[_row_spec(AQ_BLK, 384)],
        out_shape=[jax.ShapeDtypeStruct((T, 384), BF16)],
        scratch_shapes=[pltpu.VMEM((AK_WIN, 384), BF16), pltpu.VMEM((AK_WIN, 384), BF16)],
        sem=("arbitrary",), operands=(q, kpad, vpad, bias), jobs=jobs)


def _att_bwd(q, kpad, vpad, bias, do, jobs=()):
    T = q.shape[0]
    nb = T // AQ_BLK

    def body(q_ref, k_hbm, v_hbm, b_ref, do_ref, dq_ref, dk_ref, dv_ref, db_ref, dk_acc, dv_acc, kw_scr, vw_scr):
        i = pl.program_id(0)

        @pl.when(i == 0)
        def _():
            dk_acc[...] = jnp.zeros_like(dk_acc)
            dv_acc[...] = jnp.zeros_like(dv_acc)
            db_ref[...] = jnp.zeros_like(db_ref)

        start = pl.multiple_of(i * AQ_BLK, AQ_BLK)
        pltpu.sync_copy(k_hbm.at[pl.ds(start, AK_WIN), :], kw_scr)
        pltpu.sync_copy(v_hbm.at[pl.ds(start, AK_WIN), :], vw_scr)
        kw = kw_scr[...]
        vw = vw_scr[...]
        qb = q_ref[...]
        dob = do_ref[...]
        kvalid = _att_kvalid(i)
        for hd in range(6):
            ls = slice(hd * 64, (hd + 1) * 64)
            q_h, k_h, v_h, do_h = qb[:, ls], kw[:, ls], vw[:, ls], dob[:, ls]
            p = _att_scores(q_h, k_h, b_ref[hd], kvalid)
            dv_acc[pl.ds(start, AK_WIN), ls] += _dot_tn(p.astype(BF16), do_h)
            dp = _dot_nt(do_h, v_h)
            ds = p * (dp - jnp.sum(p * dp, axis=-1, keepdims=True))
            db_ref[hd] += ds
            dsb = ds.astype(BF16)
            dq_ref[:, ls] = (_dot(dsb, k_h) * A_SCALE).astype(BF16)
            dk_acc[pl.ds(start, AK_WIN), ls] += _dot_tn(dsb, q_h) * A_SCALE

        @pl.when(i == nb - 1)
        def _():
            dk_ref[...] = dk_acc[AK_PAD:, :].astype(BF16)
            dv_ref[...] = dv_acc[AK_PAD:, :].astype(BF16)

    return _pcall(
        body, name="att_bwd", grid=(nb,),
        in_specs=[_row_spec(AQ_BLK, 384), _any_spec(), _any_spec(),
                  _full_spec((6, AQ_BLK, AK_WIN)), _row_spec(AQ_BLK, 384)],
        out_specs=[_row_spec(AQ_BLK, 384), _full_spec((T, 384)), _full_spec((T, 384)),
                   _full_spec((6, AQ_BLK, AK_WIN))],
        out_shape=[jax.ShapeDtypeStruct((T, 384), BF16), jax.ShapeDtypeStruct((T, 384), BF16),
                   jax.ShapeDtypeStruct((T, 384), BF16), jax.ShapeDtypeStruct((6, AQ_BLK, AK_WIN), F32)],
        scratch_shapes=[pltpu.VMEM((T + AK_PAD, 384), F32), pltpu.VMEM((T + AK_PAD, 384), F32),
                        pltpu.VMEM((AK_WIN, 384), BF16), pltpu.VMEM((AK_WIN, 384), BF16)],
        sem=("arbitrary",), operands=(q, kpad, vpad, bias, do), jobs=jobs)


FF_BLK = 512
N_FF = 4096 // FF_BLK


def _outproj_mlp_fwd(h, o_gla, o_conv, o_att, w_out, gamma, w_up, w_down, jobs=()):
    T = h.shape[0]
    tm = 512

    def body(h_ref, og_ref, oc_ref, oa_ref, wo_ref, g_ref, wu_ref, wd_ref, h1_ref, xn_ref, h2_ref, acc):
        j = pl.program_id(1)

        @pl.when(j == 0)
        def _():
            wo = wo_ref[...]
            h1 = (h_ref[...] + _dot(og_ref[...], wo[0:384]) + _dot(oc_ref[...], wo[384:640])
                  + _dot(oa_ref[...], wo[640:1024]))
            h1_ref[...] = h1
            r = lax.rsqrt(jnp.mean(h1 * h1, axis=-1, keepdims=True) + EPS)
            xn_ref[...] = (h1 * r * g_ref[...]).astype(BF16)
            acc[...] = h1

        a = jnp.maximum(_dot(xn_ref[...], wu_ref[0]), 0.0)
        acc[...] += _dot((a * a).astype(BF16), wd_ref[0])

        @pl.when(j == N_FF - 1)
        def _():
            h2_ref[...] = acc[...]

    row = lambda n: pl.BlockSpec((tm, n), lambda i, j: (i, 0))
    return _pcall(
        body, name="outproj_mlp_fwd", grid=(T // tm, N_FF),
        in_specs=[row(D), row(384), row(256), row(384),
                  pl.BlockSpec((D, D), lambda i, j: (0, 0)), pl.BlockSpec((1, D), lambda i, j: (0, 0)),
                  pl.BlockSpec((1, D, FF_BLK), lambda i, j: (j, 0, 0)),
                  pl.BlockSpec((1, FF_BLK, D), lambda i, j: (j, 0, 0))],
        out_specs=[row(D), row(D), row(D)],
        out_shape=[jax.ShapeDtypeStruct((T, D), F32), jax.ShapeDtypeStruct((T, D), BF16),
                   jax.ShapeDtypeStruct((T, D), F32)],
        scratch_shapes=[pltpu.VMEM((tm, D), F32)],
        sem=("arbitrary", "arbitrary"), operands=(h, o_gla, o_conv, o_att, w_out, gamma, w_up, w_down), jobs=jobs)


def _mlp_bwd(xn2, h1, dh2, gamma, w_up, w_down, jobs=()):
    T = xn2.shape[0]
    tm = 512
    nt = T // tm
    last = N_FF - 1

    def body(xn_ref, h1_ref, dy_ref, g_ref, wu_ref, wd_ref, dh1_ref, dwu_ref, dwd_ref, dg_ref,
             dxn_acc, acc_u, acc_d):
        j = pl.program_id(0)
        i = pl.program_id(1)
        x = xn_ref[...]
        dy = dy_ref[...]
        dyb = dy.astype(BF16)
        wu = wu_ref[0]
        wd = wd_ref[0]
        a = jnp.maximum(_dot(x, wu), 0.0)
        hh = (a * a).astype(BF16)
        du = (_dot_nt(dyb, wd) * (2.0 * a)).astype(BF16)
        cu_ = _dot_tn(x, du)
        cd_ = _dot_tn(hh, dyb)

        @pl.when(i == 0)
        def _():
            acc_u[...] = cu_
            acc_d[...] = cd_

        @pl.when(i > 0)
        def _():
            acc_u[...] += cu_
            acc_d[...] += cd_

        @pl.when(i == nt - 1)
        def _():
            dwu_ref[0, 0] = acc_u[...].astype(BF16)
            dwd_ref[0, 0] = acc_d[...].astype(BF16)

        rows = pl.ds(pl.multiple_of(i * tm, tm), tm)
        dxn = _dot_nt(du, wu)

        @pl.when(j == 0)
        def _():
            dxn_acc[rows, :] = dxn

        @pl.when(j > 0)
        def _():
            dxn_acc[rows, :] += dxn

        @pl.when(j == last)
        def _():
            @pl.when(i == 0)
            def _():
                dg_ref[...] = jnp.zeros_like(dg_ref)

            h1 = h1_ref[...]
            r = lax.rsqrt(jnp.mean(h1 * h1, axis=-1, keepdims=True) + EPS)
            dx, dgam = _rms_bwd(dxn_acc[rows, :], h1, r, g_ref[...])
            dh1_ref[...] = dy + dx
            dg_ref[...] += dgam

    late = lambda j, i: (jnp.where(j == last, i, 0), 0)
    return _pcall(
        body, name="mlp_bwd", grid=(N_FF, nt),
        in_specs=[pl.BlockSpec((tm, D), lambda j, i: (i, 0)), pl.BlockSpec((tm, D), late),
                  pl.BlockSpec((tm, D), lambda j, i: (i, 0)), pl.BlockSpec((1, D), lambda j, i: (0, 0)),
                  pl.BlockSpec((1, D, FF_BLK), lambda j, i: (j, 0, 0)),
                  pl.BlockSpec((1, FF_BLK, D), lambda j, i: (j, 0, 0))],
        out_specs=[pl.BlockSpec((tm, D), late),
                   pl.BlockSpec((1, 1, D, FF_BLK), lambda j, i: (j % 2, j // 2, 0, 0)),
                   pl.BlockSpec((1, 1, FF_BLK, D), lambda j, i: (j % 2, j // 2, 0, 0)),
                   pl.BlockSpec((1, D), lambda j, i: (0, 0))],
        out_shape=[jax.ShapeDtypeStruct((T, D), F32), jax.ShapeDtypeStruct((2, 4, D, FF_BLK), BF16),
                   jax.ShapeDtypeStruct((2, 4, FF_BLK, D), BF16), jax.ShapeDtypeStruct((1, D), F32)],
        scratch_shapes=[pltpu.VMEM((T, D), F32), pltpu.VMEM((D, FF_BLK), F32), pltpu.VMEM((FF_BLK, D), F32)],
        sem=("arbitrary", "arbitrary"), operands=(xn2, h1, dh2, gamma, w_up, w_down), jobs=jobs)


def _outproj_bwd(dh1, o_gla, o_conv, o_att, w_out, jobs=()):
    T = dh1.shape[0]
    tm = 512
    nt = T // tm

    def body(dy_ref, og_ref, oc_ref, oa_ref, wo_ref, dg_ref, dc_ref, da_ref, dw_ref, acc):
        i = pl.program_id(0)
        dyb = dy_ref[...].astype(BF16)
        dm = _dot_nt(dyb, wo_ref[...])
        dg_ref[...] = dm[:, 0:384].astype(BF16)
        dc_ref[...] = dm[:, 384:640].astype(BF16)
        da_ref[...] = dm[:, 640:1024].astype(BF16)
        mixed = jnp.concatenate([og_ref[...], oc_ref[...], oa_ref[...]], axis=1)
        contrib = _dot_tn(mixed, dyb)

        @pl.when(i == 0)
        def _():
            acc[...] = contrib

        @pl.when(i > 0)
        def _():
            acc[...] += contrib

        @pl.when(i == nt - 1)
        def _():
            for j in range(N_DEV):
                dw_ref[j % 2, j // 2] = acc[j * 128:(j + 1) * 128, :].astype(BF16)

    return _pcall(
        body, name="outproj_bwd", grid=(nt,),
        in_specs=[_row_spec(tm, D), _row_spec(tm, 384), _row_spec(tm, 256), _row_spec(tm, 384),
                  _full_spec((D, D))],
        out_specs=[_row_spec(tm, 384), _row_spec(tm, 256), _row_spec(tm, 384), _full_spec((2, 4, 128, D))],
        out_shape=[jax.ShapeDtypeStruct((T, 384), BF16), jax.ShapeDtypeStruct((T, 256), BF16),
                   jax.ShapeDtypeStruct((T, 384), BF16), jax.ShapeDtypeStruct((2, 4, 128, D), BF16)],
        scratch_shapes=[pltpu.VMEM((D, D), F32)],
        sem=("arbitrary",), operands=(dh1, o_gla, o_conv, o_att, w_out), jobs=jobs)


def _loss_fwd_bwd(h, gamma, target):
    T = h.shape[0]
    tm = 512

    def body(h_ref, g_ref, t_ref, loss_ref, dh_ref, dg_ref):
        @pl.when(pl.program_id(0) == 0)
        def _():
            loss_ref[...] = jnp.zeros_like(loss_ref)
            dg_ref[...] = jnp.zeros_like(dg_ref)

        x = h_ref[...]
        r = lax.rsqrt(jnp.mean(x * x, axis=-1, keepdims=True) + EPS)
        gamma_ = g_ref[...]
        e = x * r * gamma_ - t_ref[...]
        loss_ref[...] += 0.5 * jnp.sum(jnp.mean(e * e, axis=-1, keepdims=True), axis=0, keepdims=True)
        dx, dgam = _rms_bwd(e * (1.0 / D), x, r, gamma_)
        dh_ref[...] = dx
        dg_ref[...] += dgam

    return pl.pallas_call(
        body, name="loss_fwd_bwd", grid=(T // tm,),
        in_specs=[_row_spec(tm, D), _full_spec((1, D)), _row_spec(tm, D)],
        out_specs=[_full_spec((8, 128)), _row_spec(tm, D), _full_spec((1, D))],
        out_shape=[jax.ShapeDtypeStruct((8, 128), F32), jax.ShapeDtypeStruct((T, D), F32),
                   jax.ShapeDtypeStruct((1, D), F32)],
        compiler_params=_cp(("arbitrary",)),
    )(h, gamma, target)


def _adamw_math(w, g, m, v):
    m = ADAM_B1 * m + (1.0 - ADAM_B1) * g
    v = ADAM_B2 * v + (1.0 - ADAM_B2) * (g * g)
    m_hat = m / (1.0 - ADAM_B1 ** ADAM_STEP)
    v_hat = v / (1.0 - ADAM_B2 ** ADAM_STEP)
    delta = -ADAM_LR * (m_hat / (jnp.sqrt(v_hat) + ADAM_EPS) + ADAM_WD * w)
    return delta, m, v


def _rs_adamw(a_own, r2, w, m, v, layer, chip_idx, rows_blk, prev=None):
    _, R, C = w.shape
    nblk = R // rows_blk

    def body(chip_ref, a_ref, r_ref, w_ref, m_ref, v_ref, *rest):
        g_out, d_out, m_out, v_out = rest[-4:]
        g = (a_ref[0].astype(F32) + r_ref[0].astype(F32)) + (r_ref[1].astype(F32) + r_ref[2].astype(F32))
        delta, m_new, v_new = _adamw_math(w_ref[0], g, m_ref[0], v_ref[0])
        g_out[0] = g
        d_out[0] = delta
        m_out[0] = m_new
        v_out[0] = v_new

    blk = pl.BlockSpec((1, rows_blk, C), lambda i, chip: (layer, i, 0))
    n_prev = 0 if prev is None else 4
    grid_spec = pltpu.PrefetchScalarGridSpec(
        num_scalar_prefetch=1, grid=(nblk,),
        in_specs=[pl.BlockSpec((1, rows_blk, C), lambda i, chip: (chip[0], i, 0)),
                  pl.BlockSpec((3, rows_blk, C), lambda i, chip: (0, i, 0)), blk, blk, blk]
        + [_any_spec()] * n_prev,
        out_specs=[blk, blk, blk, blk])
    return pl.pallas_call(
        body, name="rs_adamw", grid_spec=grid_spec,
        out_shape=[jax.ShapeDtypeStruct((DEPTH, R, C), F32)] * 4,
        input_output_aliases={6 + t: t for t in range(n_prev)},
        compiler_params=_cp(("arbitrary",)),
    )(chip_idx, a_own, r2, w, m, v, *(prev or ()))


def _pair_sum(g, r1, core_idx, rows_blk):
    _, _, R, C = g.shape
    nblk = R // rows_blk

    def body(core_ref, g_ref, r_ref, o_ref):
        o_ref[...] = (g_ref[0].astype(F32) + r_ref[...].astype(F32)).astype(BF16)

    grid_spec = pltpu.PrefetchScalarGridSpec(
        num_scalar_prefetch=1, grid=(4, nblk),
        in_specs=[pl.BlockSpec((1, 1, rows_blk, C), lambda k, i, core: (core[0], k, i, 0)),
                  pl.BlockSpec((1, rows_blk, C), lambda k, i, core: (k, i, 0))],
        out_specs=pl.BlockSpec((1, rows_blk, C), lambda k, i, core: (k, i, 0)))
    return pl.pallas_call(
        body, name="rs_pair_sum", grid_spec=grid_spec,
        out_shape=jax.ShapeDtypeStruct((4, R, C), BF16),
        compiler_params=_cp(("arbitrary", "arbitrary")),
    )(core_idx, g, r1)


def _small_sum(gathered):
    def body(g_ref, o_ref):
        acc = g_ref[0]
        for d in range(1, N_DEV):
            acc = acc + g_ref[d]
        o_ref[...] = acc

    return pl.pallas_call(
        body, name="small_sum",
        out_shape=jax.ShapeDtypeStruct((SMALL_ROWS, 1024), F32),
        compiler_params=_cp(),
    )(gathered)


def _adamw_small(ws, gs, ms, vs):
    n = len(ws)

    def body(*refs):
        w_r, g_r, m_r, v_r = refs[0:n], refs[n:2 * n], refs[2 * n:3 * n], refs[3 * n:4 * n]
        d_o, m_o, v_o = refs[4 * n:5 * n], refs[5 * n:6 * n], refs[6 * n:7 * n]
        for t in range(n):
            delta, m_new, v_new = _adamw_math(w_r[t][...], g_r[t][...], m_r[t][...], v_r[t][...])
            d_o[t][...] = delta
            m_o[t][...] = m_new
            v_o[t][...] = v_new

    shapes = [jax.ShapeDtypeStruct(w.shape, F32) for w in ws]
    outs = pl.pallas_call(
        body, name="adamw_small", out_shape=shapes * 3, compiler_params=_cp(),
    )(*ws, *gs, *ms, *vs)
    return outs[0:n], outs[n:2 * n], outs[2 * n:3 * n]


def _mesh_pos():
    return lax.axis_index("x"), lax.axis_index("y"), lax.axis_index("c")


def _peers():
    x, y, c = _mesh_pos()
    return (x, y, c), (x, y, 1 - c), [(1 - x, y), (x, 1 - y), (1 - x, 1 - y)]


def _slot(ref, pos):
    return ref.at[4 * pos[0] + 2 * pos[1] + pos[2]]


def _remote(src, dst, send_sem, recv_sem, to):
    return pltpu.make_async_remote_copy(src_ref=src, dst_ref=dst, send_sem=send_sem, recv_sem=recv_sem,
                                        device_id=to, device_id_type=MESH)


def _ag_spread(shards):
    n = len(shards)

    def copies(ins, outs, sems):
        send, recv, loc = sems
        me, sibling, chips = _peers()
        peers = [sibling] + [(*chip, me[2]) for chip in chips]
        local = [pltpu.make_async_copy(ins[a], _slot(outs[a], me), loc.at[a]) for a in range(n)]
        sends = [_remote(ins[a], _slot(outs[a], me), send.at[a, k], recv.at[a, k], p)
                 for a in range(n) for k, p in enumerate(peers)]
        recvs = [_remote(ins[a], _slot(outs[a], p), send.at[a, k], recv.at[a, k], p)
                 for a in range(n) for k, p in enumerate(peers)]
        return local, sends, recvs

    def start(ins, outs, sems):
        local, sends, _ = copies(ins, outs, sems)
        for cp in local + sends:
            cp.start()

    def finish(ins, outs, sems):
        local, sends, recvs = copies(ins, outs, sems)
        for cp in sends:
            cp.wait_send()
        for cp in recvs:
            cp.wait_recv()
        for cp in local:
            cp.wait()

    return _Job(shards, [jax.ShapeDtypeStruct((N_DEV,) + a.shape, a.dtype) for a in shards],
                [pltpu.SemaphoreType.DMA((n, 4)), pltpu.SemaphoreType.DMA((n, 4)), pltpu.SemaphoreType.DMA((n,))],
                start, finish)


def _ag_pass(stacks):
    n = len(stacks)

    def copies(ins, outs, sems):
        send, recv = sems
        me, sibling, chips = _peers()
        sends = [_remote(_slot(ins[a], (*chip, me[2])), _slot(outs[a], (*chip, me[2])), send.at[a, j], recv.at[a, j],
                         sibling) for a in range(n) for j, chip in enumerate(chips)]
        recvs = [_remote(_slot(ins[a], (*chip, me[2])), _slot(outs[a], (*chip, 1 - me[2])), send.at[a, j],
                         recv.at[a, j], sibling) for a in range(n) for j, chip in enumerate(chips)]
        return sends, recvs

    def start(ins, outs, sems):
        for cp in copies(ins, outs, sems)[0]:
            cp.start()

    def finish(ins, outs, sems):
        sends, recvs = copies(ins, outs, sems)
        for cp in sends:
            cp.wait_send()
        for cp in recvs:
            cp.wait_recv()

    return _Job(stacks, [jax.ShapeDtypeStruct(a.shape, a.dtype) for a in stacks],
                [pltpu.SemaphoreType.DMA((n, 3)), pltpu.SemaphoreType.DMA((n, 3))],
                start, finish, aliases={a: a for a in range(n)})


def _rs_swap(parts):
    n = len(parts)

    def copies(ins, outs, sems):
        send, recv = sems
        me, sibling, _ = _peers()
        return [_remote(ins[a].at[1 - me[2]], outs[a], send.at[a], recv.at[a], sibling) for a in range(n)]

    def start(ins, outs, sems):
        for cp in copies(ins, outs, sems):
            cp.start()

    def finish(ins, outs, sems):
        for cp in copies(ins, outs, sems):
            cp.wait()

    return _Job(parts, [jax.ShapeDtypeStruct(a.shape[1:], a.dtype) for a in parts],
                [pltpu.SemaphoreType.DMA((n,)), pltpu.SemaphoreType.DMA((n,))], start, finish)


def _rs_ici(pairs):
    n = len(pairs)

    def copies(ins, outs, sems):
        send, recv = sems
        me, _, chips = _peers()
        return [_remote(ins[a].at[2 * chip[0] + chip[1]], outs[a].at[j], send.at[a, j], recv.at[a, j],
                        (*chip, me[2])) for a in range(n) for j, chip in enumerate(chips)]

    def start(ins, outs, sems):
        for cp in copies(ins, outs, sems):
            cp.start()

    def finish(ins, outs, sems):
        for cp in copies(ins, outs, sems):
            cp.wait()

    return _Job(pairs, [jax.ShapeDtypeStruct((3,) + a.shape[1:], a.dtype) for a in pairs],
                [pltpu.SemaphoreType.DMA((n, 3)), pltpu.SemaphoreType.DMA((n, 3))], start, finish)


def _comm_call(jobs, name):
    def body():
        pass

    return _pcall(body, name=name, grid=(), in_specs=[], out_specs=[], out_shape=[], operands=(), jobs=jobs)[1]


def _allgather(arrs, name):
    n = len(arrs)

    def body(*refs):
        ins, outs = refs[:n], refs[n:2 * n]
        send_sems, recv_sems, local_sems = refs[2 * n:]
        x, y, c = _mesh_pos()
        me, sibling = (x, y, c), (x, y, 1 - c)
        chips = [(1 - x, y), (x, 1 - y), (1 - x, 1 - y)]

        def slot(a, pos):
            return outs[a].at[4 * pos[0] + 2 * pos[1] + pos[2]]

        def copy(a, k, block, to, src=None):
            return pltpu.make_async_remote_copy(
                src_ref=slot(a, block) if src is None else src, dst_ref=slot(a, block),
                send_sem=send_sems.at[a, k], recv_sem=recv_sems.at[a, k],
                device_id=to, device_id_type=MESH)

        mine = [pltpu.make_async_copy(ins[a], slot(a, me), local_sems.at[a]) for a in range(n)]
        for cp in mine:
            cp.start()
        first = []
        for a in range(n):
            first.append(copy(a, 0, me, sibling, src=ins[a]))
            first += [copy(a, 1 + j, me, (*chip, c), src=ins[a]) for j, chip in enumerate(chips)]
        for cp in first:
            cp.start()
        passed = []
        for j, chip in enumerate(chips):
            for a in range(n):
                copy(a, 1 + j, (*chip, c), me).wait_recv()
                fwd = copy(a, 4 + j, (*chip, c), sibling)
                fwd.start()
                passed.append(fwd)
        for a in range(n):
            copy(a, 0, sibling, me).wait_recv()
            for j, chip in enumerate(chips):
                copy(a, 4 + j, (*chip, 1 - c), me).wait_recv()
        for cp in first + passed:
            cp.wait_send()
        for cp in mine:
            cp.wait()

    return pl.pallas_call(
        body, name=name,
        in_specs=[_any_spec()] * n, out_specs=[_any_spec()] * n,
        out_shape=[jax.ShapeDtypeStruct((N_DEV,) + a.shape, a.dtype) for a in arrs],
        scratch_shapes=[pltpu.SemaphoreType.DMA((n, 7)), pltpu.SemaphoreType.DMA((n, 7)),
                        pltpu.SemaphoreType.DMA((n,))],
        compiler_params=_cp(),
    )(*arrs)


W_IN_SHARD = 354
W_IN_COLS = ((0, 192, OQ), (192, 192, OKK), (384, 384, OV), (768, 384, OG), (1152, 16, OLR), (1168, 512, OCU),
             (1680, 384, OAQ), (2064, 384, OAK), (2448, 384, OAV))


def _w_in_padded(stack):
    new_to_ref = {new: (start, width) for start, width, new in W_IN_COLS}
    cols = []
    for new, padded in IN_GROUPS:
        start, width = new_to_ref[new]
        a = start
        while a < start + width:
            j = a // W_IN_SHARD
            b = min(start + width, (j + 1) * W_IN_SHARD)
            cols.append(stack[j][:, a - j * W_IN_SHARD:b - j * W_IN_SHARD])
            a = b
        if padded > width:
            cols.append(jnp.zeros((stack.shape[1], padded - width), stack.dtype))
    return jnp.concatenate(cols, axis=1)


def _dw_in_shards(dw):
    shards = []
    for j in range(N_DEV):
        lo, hi = j * W_IN_SHARD, (j + 1) * W_IN_SHARD
        segs = []
        for start, width, new in W_IN_COLS:
            a, b = max(lo, start), min(hi, start + width)
            if a < b:
                segs.append(dw[:, new + a - start:new + b - start])
        shards.append(jnp.concatenate(segs, axis=1))
    return jnp.stack([jnp.stack([shards[2 * chip + core] for chip in range(4)]) for core in range(2)])


def _pad_to(a, shape):
    return jnp.pad(a, [(0, s - d) for d, s in zip(a.shape, shape)])


SMALL_LAYOUT = (
    ("norm_mix", 2, 1024), ("norm_ffn", 2, 1024), ("norm_final", 1, 1024), ("gla_norm", 2, 384),
    ("b_gla_gate", 2, 192), ("b_dw", 2, 256), ("conv_ln_g", 2, 256), ("conv_ln_b", 2, 256),
    ("rel_bias", 12, 257), ("w_gla_gate", 32, 192), ("w_dw", 62, 256),
)
SUBLANES = 8


def _tile_rows(r):
    return -(-r // SUBLANES) * SUBLANES


SMALL_ROWS = sum(_tile_rows(r) for _, r, _ in SMALL_LAYOUT)


def _pack_small(parts):
    return jnp.concatenate([_pad_to(parts[name], (_tile_rows(r), 1024)) for name, r, _ in SMALL_LAYOUT], axis=0)


def _unpack_small(packed):
    out, r0 = {}, 0
    for name, r, lanes in SMALL_LAYOUT:
        out[name] = packed[r0:r0 + r, 0:lanes]
        r0 += _tile_rows(r)
    return out


def _mixers_fwd(h, wl, w_in_p, att_jobs=(), gla_jobs_fn=None):
    q, k, v, g, cu, aq, ak, av, lr = _inproj_fwd(h, wl["norm_mix"], w_in_p)
    bias = _relbias_expand(wl["rb"])
    kpad = jnp.pad(ak, ((AK_PAD, 0), (0, 0)))
    vpad = jnp.pad(av, ((AK_PAD, 0), (0, 0)))
    (o_att,), att_res = _att_fwd(aq, kpad, vpad, bias, jobs=att_jobs)
    gla_jobs = gla_jobs_fn(att_res) if gla_jobs_fn is not None else ()
    (o_gla, states), gla_res = _gla_fwd(q, k, v, g, lr, wl["wg"], wl["bg"], wl["gn"], jobs=gla_jobs)
    o_conv = _conv_fwd(cu, wl["w_dw"], wl["b_dw"], wl["ln_g"], wl["ln_b"])
    sv = dict(h=h, w_in=w_in_p, q=q, k=k, v=v, g=g, cu=cu, aq=aq, kpad=kpad, vpad=vpad, lr=lr,
              o_gla=o_gla, o_conv=o_conv, o_att=o_att, states=states, bias=bias)
    return sv, att_res, gla_res


def _mixers_bwd(sv, wl, dh1, d_ogla, d_oconv, att_grads):
    daq, dak, dav, dbias = att_grads
    d_rb = _relbias_grad(dbias)
    dcu, dw_dw, db_dw, dln_g, dln_b = _conv_bwd(sv["cu"], d_oconv, wl["w_dw"], wl["b_dw"], wl["ln_g"], wl["ln_b"])
    dq, dk, dv, dg, dlr, dwg, dbg, dgn = _gla_bwd(sv["q"], sv["k"], sv["v"], sv["g"], sv["lr"], sv["states"],
                                                  d_ogla, wl["wg"], wl["bg"], wl["gn"])
    dh, dw_in, d_nmix = _inproj_bwd(sv["h"], dh1, wl["norm_mix"], sv["w_in"],
                                    (dq, dk, dv, dg, dcu, daq, dak, dav, dlr))
    small = dict(norm_mix=d_nmix, wg=dwg, bg=dbg, gn=dgn, w_dw=dw_dw, b_dw=db_dw, ln_g=dln_g, ln_b=dln_b, rb=d_rb)
    return dh, dw_in, small


def _layer_small(l, w_dw_full, norm_mix, w_gla_gate, b_gla_gate, gla_norm, b_dw, conv_ln_g, conv_ln_b, rel_bias,
                 norm_ffn):
    return dict(
        norm_mix=norm_mix[l][None, :], norm_ffn=norm_ffn[l][None, :],
        wg=_pad_to(w_gla_gate[l], (128, 256)).astype(BF16), bg=_pad_to(b_gla_gate[l][None, :], (1, 256)),
        gn=gla_norm[l][None, :], w_dw=_pad_to(w_dw_full, (32, 256)), b_dw=b_dw[l][None, :],
        ln_g=conv_ln_g[l][None, :], ln_b=conv_ln_b[l][None, :], rb=_pad_to(rel_bias[l], (8, 384)))


RS_ROWS = dict(w_in=256, w_out=128, w_up=256, w_down=256)


def kernel(x, norm_mix, w_in, w_gla_gate, b_gla_gate, gla_norm, w_dw, b_dw, conv_ln_g, conv_ln_b, rel_bias, w_out, norm_ffn, w_up, w_down, norm_final, loss_target, m_norm_mix, m_w_in, m_w_gla_gate, m_b_gla_gate, m_gla_norm, m_w_dw, m_b_dw, m_conv_ln_g, m_conv_ln_b, m_rel_bias, m_w_out, m_norm_ffn, m_w_up, m_w_down, m_norm_final, v_norm_mix, v_w_in, v_w_gla_gate, v_b_gla_gate, v_gla_norm, v_w_dw, v_b_dw, v_conv_ln_g, v_conv_ln_b, v_rel_bias, v_w_out, v_norm_ffn, v_w_up, v_w_down, v_norm_final):
    mx, my, mc = _mesh_pos()
    me = 4 * mx + 2 * my + mc
    chip_idx = (2 * mx + my).astype(jnp.int32).reshape(1)
    core_idx = mc.astype(jnp.int32).reshape(1)
    x0, target = x[0], loss_target[0]

    def pair_sums(parts, r1, names):
        return [_pair_sum(p, r, core_idx, RS_ROWS[n]) for p, r, n in zip(parts, r1, names)]

    sh = [dict(w_in=w_in[l].astype(BF16), w_out=w_out[l].astype(BF16), w_up=w_up[l].astype(BF16),
               w_down=w_down[l].astype(BF16)) for l in range(DEPTH)]
    dw_flat = _pad_to(w_dw, (DEPTH, 32, 32)).reshape(16, 128)
    st_in0, st_dw = _allgather([sh[0]["w_in"], dw_flat], "allgather_first")
    dw_all = st_dw.reshape(N_DEV, DEPTH, 32, 32)[:, :, :KCONV, :]
    dw_all = jnp.transpose(dw_all, (1, 2, 0, 3)).reshape(DEPTH, KCONV, 256)
    wl = [_layer_small(l, dw_all[l], norm_mix, w_gla_gate, b_gla_gate, gla_norm, b_dw, conv_ln_g, conv_ln_b,
                       rel_bias, norm_ffn) for l in range(DEPTH)]

    sv0, _, (st_out0, st_up0, st_down0) = _mixers_fwd(
        x0, wl[0], _w_in_padded(st_in0),
        att_jobs=[_ag_spread([sh[0]["w_out"], sh[0]["w_up"], sh[0]["w_down"]])],
        gla_jobs_fn=lambda spread: [_ag_pass(spread)])
    wo0 = st_out0.reshape(D, D)
    (h1_0, xn2_0, h2_0), st1 = _outproj_mlp_fwd(
        x0, sv0["o_gla"], sv0["o_conv"], sv0["o_att"], wo0, wl[0]["norm_ffn"], st_up0, st_down0,
        jobs=[_ag_spread([sh[1]["w_in"], sh[1]["w_out"], sh[1]["w_up"], sh[1]["w_down"]])])

    (st_in1,) = _comm_call([_ag_pass([st1[0]])], "allgather_pass_w_in")
    sv1, (st_out1, st_up1, st_down1), _ = _mixers_fwd(
        h2_0, wl[1], _w_in_padded(st_in1), att_jobs=[_ag_pass(st1[1:])])
    wo1 = st_out1.reshape(D, D)
    (h1_1, xn2_1, h2_1), _ = _outproj_mlp_fwd(
        h2_0, sv1["o_gla"], sv1["o_conv"], sv1["o_att"], wo1, wl[1]["norm_ffn"], st_up1, st_down1)

    loss8, dh, d_nf = _loss_fwd_bwd(h2_1, norm_final[None, :], target)
    loss = lax.psum(loss8[0, 0], ("x", "y", "c"))

    (dh1, dw_up1, dw_down1, d_nffn1), _ = _mlp_bwd(xn2_1, h1_1, dh, wl[1]["norm_ffn"], st_up1, st_down1)
    ud1 = [dw_up1, dw_down1]
    (d_ogla, d_oconv, d_oatt, dw_out1), r1 = _outproj_bwd(
        dh1, sv1["o_gla"], sv1["o_conv"], sv1["o_att"], wo1, jobs=[_rs_swap(ud1)])
    pair_ud1 = pair_sums(ud1, r1, ("w_up", "w_down"))
    att_grads, r2_ud1 = _att_bwd(sv1["aq"], sv1["kpad"], sv1["vpad"], sv1["bias"], d_oatt, jobs=[_rs_ici(pair_ud1)])
    dh, dw_in1, small1 = _mixers_bwd(sv1, wl[1], dh1, d_ogla, d_oconv, att_grads)
    small1["norm_ffn"] = d_nffn1

    io1 = [_dw_in_shards(dw_in1), dw_out1]
    (dh1, dw_up0, dw_down0, d_nffn0), r1 = _mlp_bwd(xn2_0, h1_0, dh, wl[0]["norm_ffn"], st_up0, st_down0,
                                                     jobs=[_rs_swap(io1)])
    pair_io1 = pair_sums(io1, r1, ("w_in", "w_out"))
    ud0 = [dw_up0, dw_down0]
    (d_ogla, d_oconv, d_oatt, dw_out0), r1 = _outproj_bwd(
        dh1, sv0["o_gla"], sv0["o_conv"], sv0["o_att"], wo0, jobs=[_rs_swap(ud0)])
    pair_ud0 = pair_sums(ud0, r1, ("w_up", "w_down"))
    att_grads, r2 = _att_bwd(sv0["aq"], sv0["kpad"], sv0["vpad"], sv0["bias"], d_oatt,
                             jobs=[_rs_ici(pair_io1), _rs_ici(pair_ud0)])
    r2_io1, r2_ud0 = r2[:2], r2[2:]
    dx, dw_in0, small0 = _mixers_bwd(sv0, wl[0], dh1, d_ogla, d_oconv, att_grads)
    small0["norm_ffn"] = d_nffn0

    io0 = [_dw_in_shards(dw_in0), dw_out0]
    r1 = _comm_call([_rs_swap(io0)], "rs_swap_last")
    pair_io0 = pair_sums(io0, r1, ("w_in", "w_out"))
    r2_io0 = _comm_call([_rs_ici(pair_io0)], "rs_ici_last")

    big_w = dict(w_in=(w_in, m_w_in, v_w_in), w_out=(w_out, m_w_out, v_w_out), w_up=(w_up, m_w_up, v_w_up),
                 w_down=(w_down, m_w_down, v_w_down))
    pairs = {1: dict(w_in=(pair_io1[0], r2_io1[0]), w_out=(pair_io1[1], r2_io1[1]),
                     w_up=(pair_ud1[0], r2_ud1[0]), w_down=(pair_ud1[1], r2_ud1[1])),
             0: dict(w_in=(pair_io0[0], r2_io0[0]), w_out=(pair_io0[1], r2_io0[1]),
                     w_up=(pair_ud0[0], r2_ud0[0]), w_down=(pair_ud0[1], r2_ud0[1]))}
    big_out = {}
    for name, (w_, m_, v_) in big_w.items():
        res = None
        for l in (1, 0):
            a_own, r2_ = pairs[l][name]
            res = _rs_adamw(a_own, r2_, w_, m_, v_, l, chip_idx, RS_ROWS[name], prev=res)
        big_out[name] = res

    grads = (small0, small1)
    parts = dict(
        norm_mix=jnp.concatenate([grads[l]["norm_mix"] for l in range(DEPTH)], axis=0),
        norm_ffn=jnp.concatenate([grads[l]["norm_ffn"] for l in range(DEPTH)], axis=0),
        norm_final=d_nf,
        gla_norm=jnp.concatenate([grads[l]["gn"] for l in range(DEPTH)], axis=0),
        b_gla_gate=jnp.concatenate([grads[l]["bg"][:, :192] for l in range(DEPTH)], axis=0),
        b_dw=jnp.concatenate([grads[l]["b_dw"] for l in range(DEPTH)], axis=0),
        conv_ln_g=jnp.concatenate([grads[l]["ln_g"] for l in range(DEPTH)], axis=0),
        conv_ln_b=jnp.concatenate([grads[l]["ln_b"] for l in range(DEPTH)], axis=0),
        rel_bias=jnp.concatenate([grads[l]["rb"][:6, :N_REL] for l in range(DEPTH)], axis=0),
        w_gla_gate=jnp.concatenate([grads[l]["wg"][:16, :192] for l in range(DEPTH)], axis=0),
        w_dw=jnp.concatenate([grads[l]["w_dw"][:KCONV] for l in range(DEPTH)], axis=0),
    )
    small_all = _allgather([_pack_small(parts)], "allgather_small")[0]
    sg = _unpack_small(_small_sum(small_all))
    dw_grad = lax.dynamic_slice_in_dim(sg["w_dw"].reshape(DEPTH, KCONV, 256), me * 32, 32, axis=2)
    small_g = dict(
        norm_mix=sg["norm_mix"], w_gla_gate=sg["w_gla_gate"].reshape(DEPTH, 16, 192), b_gla_gate=sg["b_gla_gate"],
        gla_norm=sg["gla_norm"], w_dw=dw_grad, b_dw=sg["b_dw"], conv_ln_g=sg["conv_ln_g"],
        conv_ln_b=sg["conv_ln_b"], rel_bias=sg["rel_bias"].reshape(DEPTH, 6, N_REL), norm_ffn=sg["norm_ffn"],
        norm_final=sg["norm_final"].reshape(D))
    small_names = ("norm_mix", "w_gla_gate", "b_gla_gate", "gla_norm", "w_dw", "b_dw", "conv_ln_g", "conv_ln_b",
                   "rel_bias", "norm_ffn", "norm_final")
    small_w = dict(norm_mix=norm_mix, w_gla_gate=w_gla_gate, b_gla_gate=b_gla_gate, gla_norm=gla_norm, w_dw=w_dw,
                   b_dw=b_dw, conv_ln_g=conv_ln_g, conv_ln_b=conv_ln_b, rel_bias=rel_bias, norm_ffn=norm_ffn,
                   norm_final=norm_final)
    small_m = dict(norm_mix=m_norm_mix, w_gla_gate=m_w_gla_gate, b_gla_gate=m_b_gla_gate, gla_norm=m_gla_norm,
                   w_dw=m_w_dw, b_dw=m_b_dw, conv_ln_g=m_conv_ln_g, conv_ln_b=m_conv_ln_b, rel_bias=m_rel_bias,
                   norm_ffn=m_norm_ffn, norm_final=m_norm_final)
    small_v = dict(norm_mix=v_norm_mix, w_gla_gate=v_w_gla_gate, b_gla_gate=v_b_gla_gate, gla_norm=v_gla_norm,
                   w_dw=v_w_dw, b_dw=v_b_dw, conv_ln_g=v_conv_ln_g, conv_ln_b=v_conv_ln_b, rel_bias=v_rel_bias,
                   norm_ffn=v_norm_ffn, norm_final=v_norm_final)
    s_delta, s_m, s_v = _adamw_small([small_w[n] for n in small_names], [small_g[n] for n in small_names],
                                     [small_m[n] for n in small_names], [small_v[n] for n in small_names])
    s_idx = {n: t for t, n in enumerate(small_names)}

    order = ("norm_mix", "w_in", "w_gla_gate", "b_gla_gate", "gla_norm", "w_dw", "b_dw", "conv_ln_g", "conv_ln_b",
             "rel_bias", "w_out", "norm_ffn", "w_up", "w_down", "norm_final")

    def pick(kind, name):
        if name in big_out:
            return big_out[name][kind]
        t = s_idx[name]
        return (small_g[name], s_delta[t], s_m[t], s_v[t])[kind]

    outs = [loss, dx[None]]
    for kind in range(4):
        outs += [pick(kind, n) for n in order]
    return tuple(outs)
```

```python
import functools

import jax
import jax.numpy as jnp
from jax import lax
from jax.experimental import pallas as pl
from jax.experimental.pallas import tpu as pltpu

F32 = jnp.float32
BF16 = jnp.bfloat16
MESH = pl.DeviceIdType.MESH

D = 1024
DEPTH = 2
CH = 64
EPS = 1e-6
NEG = -1e30
N_DEV = 8
N_REL = 257
Q_SCALE = 48.0 ** -0.5
A_SCALE = 64.0 ** -0.5
GATE_TAU = 16.0
KCONV = 31

OQ, OKK, OV, OG, OCU, OAQ, OAK, OAV, OLR, DINP = 0, 256, 512, 896, 1280, 1792, 2176, 2560, 2944, 3072
IN_GROUPS = ((OQ, 256), (OKK, 256), (OV, 384), (OG, 384), (OCU, 512), (OAQ, 384), (OAK, 384), (OAV, 384), (OLR, 128))

AQ_BLK = 256
AK_WIN = 768
AK_PAD = 512
RB_W = 1536

ADAM_LR, ADAM_B1, ADAM_B2, ADAM_EPS, ADAM_WD, ADAM_STEP = 0.001, 0.9, 0.999, 1e-08, 0.01, 10


V7X_VMEM_MIB = 64
VMEM_LIMIT_MIB = V7X_VMEM_MIB - 1


def _cp(sem=None):
    kw = {"vmem_limit_bytes": VMEM_LIMIT_MIB * 1024 * 1024}
    if sem is not None:
        kw["dimension_semantics"] = sem
    return pltpu.CompilerParams(**kw)


def _dot(a, b):
    return jnp.dot(a, b, preferred_element_type=F32)


def _dot_nt(a, b):
    return lax.dot_general(a, b, (((1,), (1,)), ((), ())), preferred_element_type=F32)


def _dot_tn(a, b):
    return lax.dot_general(a, b, (((0,), (0,)), ((), ())), preferred_element_type=F32)


def _split2(a):
    hi = a.astype(BF16)
    lo = (a - hi.astype(F32)).astype(BF16)
    return hi, lo


def _split3(a):
    hi = a.astype(BF16)
    r1 = a - hi.astype(F32)
    mid = r1.astype(BF16)
    lo = (r1 - mid.astype(F32)).astype(BF16)
    return hi, mid, lo


def _sigmoid(x):
    return 1.0 / (1.0 + jnp.exp(-x))


def _group(idx, size, n):
    g = jnp.zeros_like(idx)
    for t in range(1, n):
        g = g + (idx >= t * size).astype(jnp.int32)
    return g


def _rms_bwd(dy, x, r, gamma):
    xh = x * r
    dxh = dy * gamma
    dx = r * (dxh - xh * jnp.mean(dxh * xh, axis=-1, keepdims=True))
    return dx, jnp.sum(dy * xh, axis=0, keepdims=True)


def _row_spec(tm, n):
    return pl.BlockSpec((tm, n), lambda i: (i, 0))


def _full_spec(shape):
    nd = len(shape)
    return pl.BlockSpec(shape, lambda *_: (0,) * nd)


def _any_spec():
    return pl.BlockSpec(memory_space=pl.ANY)


class _Job:
    def __init__(self, operands, out_shapes, sems, start, finish, aliases=None):
        self.operands, self.out_shapes, self.sems = list(operands), list(out_shapes), list(sems)
        self.start, self.finish, self.aliases = start, finish, dict(aliases or {})


def _pcall(body, *, name, grid, in_specs, out_specs, out_shape, operands, scratch_shapes=(), sem=None, jobs=()):
    jobs = list(jobs)
    in_specs, out_specs, out_shape = list(in_specs), list(out_specs), list(out_shape)
    scratch_shapes = list(scratch_shapes)
    n_in, n_out, n_scr = len(in_specs), len(out_specs), len(scratch_shapes)
    j_in = [a for j in jobs for a in j.operands]
    j_out = [s for j in jobs for s in j.out_shapes]
    j_sem = [s for j in jobs for s in j.sems]
    aliases, io, oo = {}, n_in, n_out
    for j in jobs:
        for a, b in j.aliases.items():
            aliases[io + a] = oo + b
        io += len(j.operands)
        oo += len(j.out_shapes)

    def wrapped(*refs):
        own_in, ji = refs[:n_in], refs[n_in:n_in + len(j_in)]
        o0 = n_in + len(j_in)
        own_out, jo = refs[o0:o0 + n_out], refs[o0 + n_out:o0 + n_out + len(j_out)]
        s0 = o0 + n_out + len(j_out)
        own_scr, js = refs[s0:s0 + n_scr], refs[s0 + n_scr:]

        def each_job(fn_name):
            a = b = c = 0
            for j in jobs:
                na, nb, nc = len(j.operands), len(j.out_shapes), len(j.sems)
                getattr(j, fn_name)(ji[a:a + na], jo[b:b + nb], js[c:c + nc])
                a, b, c = a + na, b + nb, c + nc

        if jobs and grid:
            pids = [pl.program_id(d) for d in range(len(grid))]
            first = functools.reduce(jnp.logical_and, [p == 0 for p in pids])
            last = functools.reduce(jnp.logical_and, [p == g - 1 for p, g in zip(pids, grid)])
            pl.when(first)(lambda: each_job("start"))
        elif jobs:
            each_job("start")

        body(*own_in, *own_out, *own_scr)

        if jobs and grid:
            pl.when(last)(lambda: each_job("finish"))
        elif jobs:
            each_job("finish")

    res = pl.pallas_call(
        wrapped, name=name, grid=grid,
        in_specs=in_specs + [_any_spec()] * len(j_in), out_specs=out_specs + [_any_spec()] * len(j_out),
        out_shape=out_shape + j_out, scratch_shapes=scratch_shapes + j_sem,
        input_output_aliases=aliases, compiler_params=_cp(sem),
    )(*operands, *j_in)
    return res[:n_out], res[n_out:]


def _inproj_fwd(h, gamma, w):
    T = h.shape[0]
    tm = 512

    def body(h_ref, g_ref, w_ref, *outs):
        x = h_ref[...]
        r = lax.rsqrt(jnp.mean(x * x, axis=-1, keepdims=True) + EPS)
        xn = (x * r * g_ref[...]).astype(BF16)
        p = _dot(xn, w_ref[...])
        for o_ref, (off, n) in zip(outs, IN_GROUPS):
            o_ref[...] = p[:, off:off + n].astype(BF16)

    return pl.pallas_call(
        body, name="inproj_fwd", grid=(T // tm,),
        in_specs=[_row_spec(tm, D), _full_spec((1, D)), _full_spec((D, DINP))],
        out_specs=[_row_spec(tm, n) for _, n in IN_GROUPS],
        out_shape=[jax.ShapeDtypeStruct((T, n), BF16) for _, n in IN_GROUPS],
        compiler_params=_cp(("parallel",)),
    )(h, gamma, w)


def _inproj_bwd(h, dh_in, gamma, w, dparts):
    T = h.shape[0]
    tm = 256
    nt = T // tm

    def body(h_ref, dhin_ref, g_ref, w_ref, *rest):
        dp_refs = rest[:9]
        dh_ref, dw_ref, dg_ref, acc = rest[9:]
        i = pl.program_id(0)

        @pl.when(i == 0)
        def _():
            acc[...] = jnp.zeros_like(acc)
            dg_ref[...] = jnp.zeros_like(dg_ref)

        x = h_ref[...]
        r = lax.rsqrt(jnp.mean(x * x, axis=-1, keepdims=True) + EPS)
        gamma_ = g_ref[...]
        xn = (x * r * gamma_).astype(BF16)
        dxn = jnp.zeros((tm, D), F32)
        for d_ref, (off, n) in zip(dp_refs, IN_GROUPS):
            d = d_ref[...]
            acc[:, off:off + n] += _dot_tn(xn, d)
            dxn = dxn + _dot_nt(d, w_ref[:, off:off + n])
        dx, dgam = _rms_bwd(dxn, x, r, gamma_)
        dh_ref[...] = dhin_ref[...] + dx
        dg_ref[...] += dgam

        @pl.when(i == nt - 1)
        def _():
            dw_ref[...] = acc[...].astype(BF16)

    return pl.pallas_call(
        body, name="inproj_bwd", grid=(nt,),
        in_specs=[_row_spec(tm, D), _row_spec(tm, D), _full_spec((1, D)), _full_spec((D, DINP))]
        + [_row_spec(tm, n) for _, n in IN_GROUPS],
        out_specs=[_row_spec(tm, D), _full_spec((D, DINP)), _full_spec((1, D))],
        out_shape=[jax.ShapeDtypeStruct((T, D), F32), jax.ShapeDtypeStruct((D, DINP), BF16),
                   jax.ShapeDtypeStruct((1, D), F32)],
        scratch_shapes=[pltpu.VMEM((D, DINP), F32)],
        compiler_params=_cp(("arbitrary",)),
    )(h, dh_in, gamma, w, *dparts)


GLA_ROWS = 512
GLA_NC = GLA_ROWS // CH


def _gla_consts():
    ri = lax.broadcasted_iota(jnp.int32, (CH, CH), 0)
    ci = lax.broadcasted_iota(jnp.int32, (CH, CH), 1)
    upper = (ci > ri).astype(BF16)
    vv = lax.broadcasted_iota(jnp.int32, (384, 256), 0)
    kk = lax.broadcasted_iota(jnp.int32, (384, 256), 1)
    mask_t = ((_group(vv, 96, 4) == _group(kk, 48, 4)) & (kk < 192)).astype(F32)
    pi = lax.broadcasted_iota(jnp.int32, (384, 384), 0)
    pj = lax.broadcasted_iota(jnp.int32, (384, 384), 1)
    same_head = (_group(pi, 96, 4) == _group(pj, 96, 4)).astype(BF16)
    return upper, mask_t, same_head


def _gla_gate(lr_ref, wg_ref, bg_ref):
    z = _dot(lr_ref[...], wg_ref[...]) + bg_ref[...]
    la = (jnp.minimum(z, 0.0) - jnp.log(1.0 + jnp.exp(-jnp.abs(z)))) * (1.0 / GATE_TAU)
    return z, la


def _gla_chunk_decay(la_c, upper):
    hi, lo = _split2(la_c)
    dec = _dot(upper, hi) + _dot(upper, lo)
    end = jnp.sum(la_c, axis=0, keepdims=True)
    return jnp.exp(dec), jnp.exp(end)


def _head_mean(x, same_head):
    hi, lo = _split2(x)
    return (_dot(hi, same_head) + _dot(lo, same_head)) * (1.0 / 96.0)


def _gla_fwd(q, k, v, g, lr, wg, bg, gn, jobs=()):
    T = q.shape[0]
    nb = T // GLA_ROWS

    def body(q_ref, k_ref, v_ref, g_ref, lr_ref, wg_ref, bg_ref, gn_ref, y_ref, st_ref, s_scr, o_scr):
        upper, mask_t, same_head = _gla_consts()

        @pl.when(pl.program_id(0) == 0)
        def _():
            s_scr[...] = jnp.zeros_like(s_scr)

        _, la = _gla_gate(lr_ref, wg_ref, bg_ref)
        for c in range(GLA_NC):
            rs = slice(c * CH, (c + 1) * CH)
            w, a = _gla_chunk_decay(la[rs], upper)
            kd = (k_ref[rs, :].astype(F32) * w).astype(BF16)
            kv_t = _dot_tn(v_ref[rs, :], kd)
            s_new = s_scr[...] * a + kv_t * mask_t
            s_scr[...] = s_new
            sb = s_new.astype(BF16)
            st_ref[c] = sb
            qs = (q_ref[rs, :].astype(F32) * Q_SCALE).astype(BF16)
            o_scr[rs, :] = _dot_nt(qs, sb)
        o = o_scr[...]
        r = lax.rsqrt(_head_mean(o * o, same_head) + EPS)
        gf = g_ref[...].astype(F32)
        y_ref[...] = (o * r * gn_ref[...] * (gf * _sigmoid(gf))).astype(BF16)

    return _pcall(
        body, name="gla_fwd", grid=(nb,),
        in_specs=[_row_spec(GLA_ROWS, 256), _row_spec(GLA_ROWS, 256), _row_spec(GLA_ROWS, 384),
                  _row_spec(GLA_ROWS, 384), _row_spec(GLA_ROWS, 128),
                  _full_spec((128, 256)), _full_spec((1, 256)), _full_spec((1, 384))],
        out_specs=[_row_spec(GLA_ROWS, 384), pl.BlockSpec((GLA_NC, 384, 256), lambda i: (i, 0, 0))],
        out_shape=[jax.ShapeDtypeStruct((T, 384), BF16), jax.ShapeDtypeStruct((T // CH, 384, 256), BF16)],
        scratch_shapes=[pltpu.VMEM((384, 256), F32), pltpu.VMEM((GLA_ROWS, 384), F32)],
        sem=("arbitrary",), operands=(q, k, v, g, lr, wg, bg, gn), jobs=jobs)


def _gla_bwd(q, k, v, g, lr, states, dy, wg, bg, gn):
    T = q.shape[0]
    nb = T // GLA_ROWS

    def rev(s):
        return nb - 1 - s

    def body(q_ref, k_ref, v_ref, g_ref, lr_ref, st_ref, stp_ref, dy_ref, wg_ref, bg_ref, gn_ref,
             dq_ref, dk_ref, dv_ref, dg_ref, dlr_ref, dwg_ref, dbg_ref, dgn_ref,
             d_scr, an_scr, o_scr, do_scr, dla_scr):
        upper, mask_t, same_head = _gla_consts()
        s = pl.program_id(0)
        blk = rev(s)

        @pl.when(s == 0)
        def _():
            d_scr[...] = jnp.zeros_like(d_scr)
            an_scr[...] = jnp.zeros_like(an_scr)
            dwg_ref[...] = jnp.zeros_like(dwg_ref)
            dbg_ref[...] = jnp.zeros_like(dbg_ref)
            dgn_ref[...] = jnp.zeros_like(dgn_ref)

        z, la = _gla_gate(lr_ref, wg_ref, bg_ref)
        ws, as_, qss, kds = [], [], [], []
        for c in range(GLA_NC):
            rs = slice(c * CH, (c + 1) * CH)
            w, a = _gla_chunk_decay(la[rs], upper)
            ws.append(w)
            as_.append(a)
            qs = (q_ref[rs, :].astype(F32) * Q_SCALE).astype(BF16)
            qss.append(qs)
            kds.append((k_ref[rs, :].astype(F32) * w).astype(BF16))
            o_scr[rs, :] = _dot_nt(qs, st_ref[c])
        o = o_scr[...]
        r = lax.rsqrt(_head_mean(o * o, same_head) + EPS)
        on = o * r
        gf = g_ref[...].astype(F32)
        sg = _sigmoid(gf)
        si = gf * sg
        dyf = dy_ref[...].astype(F32)
        gn_ = gn_ref[...]
        dgn_ref[...] += jnp.sum(dyf * si * on, axis=0, keepdims=True)
        dg_ref[...] = (dyf * on * gn_ * (sg * (1.0 + gf * (1.0 - sg)))).astype(BF16)
        d_on = dyf * si * gn_
        do_scr[...] = r * (d_on - on * _head_mean(d_on * on, same_head))

        first = (blk > 0).astype(F32)
        for c in reversed(range(GLA_NC)):
            rs = slice(c * CH, (c + 1) * CH)
            dob = do_scr[rs, :].astype(BF16)
            sb = st_ref[c]
            if c > 0:
                s_prev = st_ref[c - 1].astype(F32)
            else:
                s_prev = stp_ref[0].astype(F32) * first
            dq_ref[rs, :] = (_dot(dob, sb) * Q_SCALE).astype(BF16)
            dt = d_scr[...] * an_scr[...] + _dot_tn(dob, qss[c]) * mask_t
            d_scr[...] = dt
            da = jnp.sum(dt * s_prev, axis=0, keepdims=True)
            db = dt.astype(BF16)
            dkd = _dot(v_ref[rs, :], db)
            dv_ref[rs, :] = _dot_nt(kds[c], db).astype(BF16)
            dk_ref[rs, :] = (dkd * ws[c]).astype(BF16)
            ddec = dkd * k_ref[rs, :].astype(F32) * ws[c]
            hi, lo = _split2(ddec)
            dla_scr[rs, :] = _dot_tn(upper, hi) + _dot_tn(upper, lo) + as_[c] * da
            an_scr[...] = as_[c]

        dz = dla_scr[...] * (1.0 - _sigmoid(z)) * (1.0 / GATE_TAU)
        dzb = dz.astype(BF16)
        dlr_ref[...] = _dot_nt(dzb, wg_ref[...]).astype(BF16)
        dwg_ref[...] += _dot_tn(lr_ref[...], dzb)
        dbg_ref[...] += jnp.sum(dz, axis=0, keepdims=True)

    def rspec(n):
        return pl.BlockSpec((GLA_ROWS, n), lambda s: (rev(s), 0))

    return pl.pallas_call(
        body, name="gla_bwd", grid=(nb,),
        in_specs=[rspec(256), rspec(256), rspec(384), rspec(384), rspec(128),
                  pl.BlockSpec((GLA_NC, 384, 256), lambda s: (rev(s), 0, 0)),
                  pl.BlockSpec((1, 384, 256), lambda s: (jnp.maximum(rev(s) * GLA_NC - 1, 0), 0, 0)),
                  rspec(384), _full_spec((128, 256)), _full_spec((1, 256)), _full_spec((1, 384))],
        out_specs=[rspec(256), rspec(256), rspec(384), rspec(384), rspec(128),
                   _full_spec((128, 256)), _full_spec((1, 256)), _full_spec((1, 384))],
        out_shape=[jax.ShapeDtypeStruct((T, 256), BF16), jax.ShapeDtypeStruct((T, 256), BF16),
                   jax.ShapeDtypeStruct((T, 384), BF16), jax.ShapeDtypeStruct((T, 384), BF16),
                   jax.ShapeDtypeStruct((T, 128), BF16),
                   jax.ShapeDtypeStruct((128, 256), F32), jax.ShapeDtypeStruct((1, 256), F32),
                   jax.ShapeDtypeStruct((1, 384), F32)],
        scratch_shapes=[pltpu.VMEM((384, 256), F32), pltpu.VMEM((1, 256), F32),
                        pltpu.VMEM((GLA_ROWS, 384), F32), pltpu.VMEM((GLA_ROWS, 384), F32),
                        pltpu.VMEM((GLA_ROWS, 256), F32)],
        compiler_params=_cp(("arbitrary",)),
    )(q, k, v, g, lr, states, states, dy, wg, bg, gn)


CONV_ROWS = 512
HALO = 32


def _conv_common(cu_ref, halo_ref, w_ref, b_ref, lg_ref, lb_ref, buf, blk):
    u = cu_ref[...].astype(F32)
    a = u[:, :256]
    sb = _sigmoid(u[:, 256:])
    uh = halo_ref[...].astype(F32)
    hh = uh[:, :256] * _sigmoid(uh[:, 256:]) * (blk > 0).astype(F32)
    buf[0:HALO, :] = hh
    buf[HALO:HALO + CONV_ROWS, :] = a * sb
    acc = jnp.zeros((CONV_ROWS, 256), F32)
    for j in range(KCONV):
        acc = acc + w_ref[j:j + 1, :] * buf[pl.ds(HALO - (KCONV - 1) + j, CONV_ROWS), :]
    cc = acc + b_ref[...]
    mu = jnp.mean(cc, axis=-1, keepdims=True)
    xc = cc - mu
    rstd = lax.rsqrt(jnp.mean(xc * xc, axis=-1, keepdims=True) + EPS)
    n = xc * rstd
    yln = n * lg_ref[...] + lb_ref[...]
    return a, sb, n, rstd, yln


def _conv_fwd(cu, w, b, lg, lb):
    T = cu.shape[0]
    nb = T // CONV_ROWS
    per = CONV_ROWS // HALO

    def body(cu_ref, halo_ref, w_ref, b_ref, lg_ref, lb_ref, y_ref, buf):
        _, _, _, _, yln = _conv_common(cu_ref, halo_ref, w_ref, b_ref, lg_ref, lb_ref, buf, pl.program_id(0))
        y_ref[...] = (yln * _sigmoid(yln)).astype(BF16)

    return pl.pallas_call(
        body, name="conv_fwd", grid=(nb,),
        in_specs=[_row_spec(CONV_ROWS, 512),
                  pl.BlockSpec((HALO, 512), lambda i: (jnp.maximum(i * per - 1, 0), 0)),
                  _full_spec((32, 256)), _full_spec((1, 256)), _full_spec((1, 256)), _full_spec((1, 256))],
        out_specs=_row_spec(CONV_ROWS, 256),
        out_shape=jax.ShapeDtypeStruct((T, 256), BF16),
        scratch_shapes=[pltpu.VMEM((CONV_ROWS + HALO, 256), F32)],
        compiler_params=_cp(("parallel",)),
    )(cu, cu, w, b, lg, lb)


def _conv_bwd(cu, dy, w, b, lg, lb):
    T = cu.shape[0]
    nb = T // CONV_ROWS
    per = CONV_ROWS // HALO

    def rev(s):
        return nb - 1 - s

    def body(cu_ref, halo_ref, dy_ref, w_ref, b_ref, lg_ref, lb_ref,
             dcu_ref, dw_ref, db_ref, dlg_ref, dlb_ref, buf, dcbuf, carry):
        s = pl.program_id(0)

        @pl.when(s == 0)
        def _():
            carry[...] = jnp.zeros_like(carry)
            dw_ref[...] = jnp.zeros_like(dw_ref)
            db_ref[...] = jnp.zeros_like(db_ref)
            dlg_ref[...] = jnp.zeros_like(dlg_ref)
            dlb_ref[...] = jnp.zeros_like(dlb_ref)

        a, sb, n, rstd, yln = _conv_common(cu_ref, halo_ref, w_ref, b_ref, lg_ref, lb_ref, buf, rev(s))
        sg = _sigmoid(yln)
        dyln = dy_ref[...].astype(F32) * (sg * (1.0 + yln * (1.0 - sg)))
        dlg_ref[...] += jnp.sum(dyln * n, axis=0, keepdims=True)
        dlb_ref[...] += jnp.sum(dyln, axis=0, keepdims=True)
        dn = dyln * lg_ref[...]
        dc = rstd * (dn - jnp.mean(dn, axis=-1, keepdims=True) - n * jnp.mean(dn * n, axis=-1, keepdims=True))
        db_ref[...] += jnp.sum(dc, axis=0, keepdims=True)
        dcbuf[0:CONV_ROWS, :] = dc
        dcbuf[CONV_ROWS:CONV_ROWS + HALO, :] = carry[...]
        dhg = jnp.zeros((CONV_ROWS, 256), F32)
        for j in range(KCONV):
            tap = buf[pl.ds(HALO - (KCONV - 1) + j, CONV_ROWS), :]
            dw_ref[j:j + 1, :] += jnp.sum(dc * tap, axis=0, keepdims=True)
            dhg = dhg + w_ref[j:j + 1, :] * dcbuf[pl.ds(KCONV - 1 - j, CONV_ROWS), :]
        carry[...] = dc[0:HALO, :]
        dcu_ref[...] = jnp.concatenate([dhg * sb, dhg * a * sb * (1.0 - sb)], axis=1).astype(BF16)

    def rspec(n):
        return pl.BlockSpec((CONV_ROWS, n), lambda s: (rev(s), 0))

    return pl.pallas_call(
        body, name="conv_bwd", grid=(nb,),
        in_specs=[rspec(512),
                  pl.BlockSpec((HALO, 512), lambda s: (jnp.maximum(rev(s) * per - 1, 0), 0)),
                  rspec(256),
                  _full_spec((32, 256)), _full_spec((1, 256)), _full_spec((1, 256)), _full_spec((1, 256))],
        out_specs=[rspec(512), _full_spec((32, 256)), _full_spec((1, 256)), _full_spec((1, 256)),
                   _full_spec((1, 256))],
        out_shape=[jax.ShapeDtypeStruct((T, 512), BF16), jax.ShapeDtypeStruct((32, 256), F32),
                   jax.ShapeDtypeStruct((1, 256), F32), jax.ShapeDtypeStruct((1, 256), F32),
                   jax.ShapeDtypeStruct((1, 256), F32)],
        scratch_shapes=[pltpu.VMEM((CONV_ROWS + HALO, 256), F32), pltpu.VMEM((CONV_ROWS + HALO, 256), F32),
                        pltpu.VMEM((HALO, 256), F32)],
        compiler_params=_cp(("arbitrary",)),
    )(cu, cu, dy, w, b, lg, lb)


def _rel_onehot_t(shift=0):
    r = lax.broadcasted_iota(jnp.int32, (384, RB_W), 0)
    n = lax.broadcasted_iota(jnp.int32, (384, RB_W), 1) - shift
    idx = jnp.clip(1024 - n, -128, 128) + 128
    return (idx == r).astype(BF16)


def _relbias_expand(rb):
    def body(rb_ref, out_ref):
        oh = _rel_onehot_t()
        hi, mid, lo = _split3(rb_ref[...])
        strip = _dot(hi, oh) + _dot(mid, oh) + _dot(lo, oh)
        qi = _group(lax.broadcasted_iota(jnp.int32, (AQ_BLK, AK_WIN), 0), CH, 4)
        kj = _group(lax.broadcasted_iota(jnp.int32, (AQ_BLK, AK_WIN), 1), CH, 12)
        valid = (kj >= qi) & (kj <= qi + 8)
        for hd in range(6):
            x = jnp.broadcast_to(strip[hd:hd + 1, :], (AQ_BLK, RB_W))
            xr = pltpu.roll(x, 0, 1, stride=1, stride_axis=0)
            out_ref[hd] = jnp.where(valid, xr[:, 512:512 + AK_WIN], NEG)

    return pl.pallas_call(
        body, name="relbias_expand",
        out_shape=jax.ShapeDtypeStruct((6, AQ_BLK, AK_WIN), F32),
        compiler_params=_cp(),
    )(rb)


def _relbias_grad(dbias):
    def body(db_ref, out_ref):
        oh = _rel_onehot_t(AQ_BLK - 1)
        ri = lax.broadcasted_iota(jnp.int32, (AQ_BLK, AQ_BLK), 0)
        ci = lax.broadcasted_iota(jnp.int32, (AQ_BLK, AQ_BLK), 1)
        flip = (ri + ci == AQ_BLK - 1).astype(BF16)
        rows = []
        for hd in range(6):
            hi, mid, lo = _split3(db_ref[hd])
            rev = _dot(flip, hi) + _dot(flip, mid) + _dot(flip, lo)
            x = jnp.concatenate([jnp.zeros((AQ_BLK, 512), F32), rev,
                                 jnp.zeros((AQ_BLK, RB_W - 512 - AK_WIN), F32)], axis=1)
            xr = pltpu.roll(x, 0, 1, stride=1, stride_axis=0)
            rows.append(jnp.sum(xr, axis=0, keepdims=True))
        rows.append(jnp.zeros((2, RB_W), F32))
        dstrip = jnp.concatenate(rows, axis=0)
        hi, mid, lo = _split3(dstrip)
        out_ref[...] = _dot_nt(hi, oh) + _dot_nt(mid, oh) + _dot_nt(lo, oh)

    return pl.pallas_call(
        body, name="relbias_grad",
        out_shape=jax.ShapeDtypeStruct((8, 384), F32),
        compiler_params=_cp(),
    )(dbias)


ATT_SLAB = 8


def _att_softmax_slab(s_scr, b_ref, hd, rows, first_key):
    kvalid = lax.broadcasted_iota(jnp.int32, (ATT_SLAB, AK_WIN), 1) >= first_key
    s = jnp.where(kvalid, s_scr[rows, :] + b_ref[hd, rows, :], NEG)
    m = jnp.max(s, axis=-1, keepdims=True)
    p = jnp.exp(s - m)
    return p * (1.0 / jnp.sum(p, axis=-1, keepdims=True))


def _att_first_key(i):
    return (8 - 4 * i) * CH


def _slab_rows(t):
    return pl.ds(t * ATT_SLAB, ATT_SLAB)


ATT_HEADS = 6


def _head_lanes(hd):
    return slice(hd * 64, (hd + 1) * 64)


def _att_fwd(q, kpad, vpad, bias, jobs=()):
    T = q.shape[0]
    nb = T // AQ_BLK

    def body(q_ref, k_hbm, v_hbm, b_ref, o_ref, kw_scr, vw_scr, s_scr):
        i = pl.program_id(0)
        start = pl.multiple_of(i * AQ_BLK, AQ_BLK)
        pltpu.sync_copy(k_hbm.at[pl.ds(start, AK_WIN), :], kw_scr)
        pltpu.sync_copy(v_hbm.at[pl.ds(start, AK_WIN), :], vw_scr)
        kw = kw_scr[...]
        vw = vw_scr[...]
        qb = q_ref[...] * A_SCALE
        first_key = _att_first_key(i)

        def scores(hd):
            ls = _head_lanes(hd)
            s_scr[hd % 2] = _dot_nt(qb[:, ls], kw[:, ls])

        scores(0)
        for hd in range(ATT_HEADS):
            if hd + 1 < ATT_HEADS:
                scores(hd + 1)
            s_h = s_scr.at[hd % 2]
            for t in range(AQ_BLK // ATT_SLAB):
                rows = _slab_rows(t)
                s_h[rows, :] = _att_softmax_slab(s_h, b_ref, hd, rows, first_key)
            ls = _head_lanes(hd)
            o_ref[:, ls] = _dot(s_h[...].astype(BF16), vw[:, ls]).astype(BF16)

    return _pcall(
        body, name="att_fwd", grid=(nb,),
        in_specs=[_row_spec(AQ_BLK, 384), _any_spec(), _any_spec(), _full_spec((6, AQ_BLK, AK_WIN))],
        out_specs=[_row_spec(AQ_BLK, 384)],
        out_shape=[jax.ShapeDtypeStruct((T, 384), BF16)],
        scratch_shapes=[pltpu.VMEM((AK_WIN, 384), BF16), pltpu.VMEM((AK_WIN, 384), BF16),
                        pltpu.VMEM((2, AQ_BLK, AK_WIN), F32)],
        sem=("arbitrary",), operands=(q, kpad, vpad, bias), jobs=jobs)


def _att_bwd(q, kpad, vpad, bias, do, jobs=()):
    T = q.shape[0]
    nb = T // AQ_BLK

    def body(q_ref, k_hbm, v_hbm, b_ref, do_ref, dq_ref, dk_ref, dv_ref, db_ref, dk_acc, dv_acc, kw_scr, vw_scr,
             s_scr, dp_scr):
        i = pl.program_id(0)

        @pl.when(i == 0)
        def _():
            dk_acc[...] = jnp.zeros_like(dk_acc)
            dv_acc[...] = jnp.zeros_like(dv_acc)
            db_ref[...] = jnp.zeros_like(db_ref)

        start = pl.multiple_of(i * AQ_BLK, AQ_BLK)
        pltpu.sync_copy(k_hbm.at[pl.ds(start, AK_WIN), :], kw_scr)
        pltpu.sync_copy(v_hbm.at[pl.ds(start, AK_WIN), :], vw_scr)
        kw = kw_scr[...]
        vw = vw_scr[...]
        qb = q_ref[...] * A_SCALE
        dob = do_ref[...]
        first_key = _att_first_key(i)

        def scores(hd):
            ls = _head_lanes(hd)
            s_scr[hd % 2] = _dot_nt(qb[:, ls], kw[:, ls])
            dp_scr[hd % 2] = _dot_nt(dob[:, ls], vw[:, ls])

        scores(0)
        for hd in range(ATT_HEADS):
            if hd + 1 < ATT_HEADS:
                scores(hd + 1)
            s_h, dp_h = s_scr.at[hd % 2], dp_scr.at[hd % 2]
            for t in range(AQ_BLK // ATT_SLAB):
                rows = _slab_rows(t)
                p = _att_softmax_slab(s_h, b_ref, hd, rows, first_key)
                dp = dp_h[rows, :]
                ds = p * (dp - jnp.sum(p * dp, axis=-1, keepdims=True))
                db_ref[hd, rows, :] += ds
                s_h[rows, :] = p
                dp_h[rows, :] = ds
            ls = _head_lanes(hd)
            dsb = dp_h[...].astype(BF16)
            dv_acc[pl.ds(start, AK_WIN), ls] += _dot_tn(s_h[...].astype(BF16), dob[:, ls])
            dq_ref[:, ls] = (_dot(dsb, kw[:, ls]) * A_SCALE).astype(BF16)
            dk_acc[pl.ds(start, AK_WIN), ls] += _dot_tn(dsb, qb[:, ls])

        @pl.when(i == nb - 1)
        def _():
            dk_ref[...] = dk_acc[AK_PAD:, :].astype(BF16)
            dv_ref[...] = dv_acc[AK_PAD:, :].astype(BF16)

    return _pcall(
        body, name="att_bwd", grid=(nb,),
        in_specs=[_row_spec(AQ_BLK, 384), _any_spec(), _any_spec(),
                  _full_spec((6, AQ_BLK, AK_WIN)), _row_spec(AQ_BLK, 384)],
        out_specs=[_row_spec(AQ_BLK, 384), _full_spec((T, 384)), _full_spec((T, 384)),
                   _full_spec((6, AQ_BLK, AK_WIN))],
        out_shape=[jax.ShapeDtypeStruct((T, 384), BF16), jax.ShapeDtypeStruct((T, 384), BF16),
                   jax.ShapeDtypeStruct((T, 384), BF16), jax.ShapeDtypeStruct((6, AQ_BLK, AK_WIN), F32)],
        scratch_shapes=[pltpu.VMEM((T + AK_PAD, 384), F32), pltpu.VMEM((T + AK_PAD, 384), F32),
                        pltpu.VMEM((AK_WIN, 384), BF16), pltpu.VMEM((AK_WIN, 384), BF16),
                        pltpu.VMEM((2, AQ_BLK, AK_WIN), F32), pltpu.VMEM((2, AQ_BLK, AK_WIN), F32)],
        sem=("arbitrary",), operands=(q, kpad, vpad, bias, do), jobs=jobs)


FF_BLK = 512
N_FF = 4096 // FF_BLK


def _outproj_mlp_fwd(h, o_gla, o_conv, o_att, w_out, gamma, w_up, w_down, jobs=()):
    T = h.shape[0]
    tm = 512

    def body(h_ref, og_ref, oc_ref, oa_ref, wo_ref, g_ref, wu_ref, wd_ref, h1_ref, xn_ref, h2_ref, acc):
        j = pl.program_id(1)

        @pl.when(j == 0)
        def _():
            wo = wo_ref[...]
            h1 = (h_ref[...] + _dot(og_ref[...], wo[0:384]) + _dot(oc_ref[...], wo[384:640])
                  + _dot(oa_ref[...], wo[640:1024]))
            h1_ref[...] = h1
            r = lax.rsqrt(jnp.mean(h1 * h1, axis=-1, keepdims=True) + EPS)
            xn_ref[...] = (h1 * r * g_ref[...]).astype(BF16)
            acc[...] = h1

        a = jnp.maximum(_dot(xn_ref[...], wu_ref[0]), 0.0)
        acc[...] += _dot((a * a).astype(BF16), wd_ref[0])

        @pl.when(j == N_FF - 1)
        def _():
            h2_ref[...] = acc[...]

    row = lambda n: pl.BlockSpec((tm, n), lambda i, j: (i, 0))
    return _pcall(
        body, name="outproj_mlp_fwd", grid=(T // tm, N_FF),
        in_specs=[row(D), row(384), row(256), row(384),
                  pl.BlockSpec((D, D), lambda i, j: (0, 0)), pl.BlockSpec((1, D), lambda i, j: (0, 0)),
                  pl.BlockSpec((1, D, FF_BLK), lambda i, j: (j, 0, 0)),
                  pl.BlockSpec((1, FF_BLK, D), lambda i, j: (j, 0, 0))],
        out_specs=[row(D), row(D), row(D)],
        out_shape=[jax.ShapeDtypeStruct((T, D), F32), jax.ShapeDtypeStruct((T, D), BF16),
                   jax.ShapeDtypeStruct((T, D), F32)],
        scratch_shapes=[pltpu.VMEM((tm, D), F32)],
        sem=("arbitrary", "arbitrary"), operands=(h, o_gla, o_conv, o_att, w_out, gamma, w_up, w_down), jobs=jobs)


def _mlp_bwd(xn2, h1, dh2, gamma, w_up, w_down, jobs=()):
    T = xn2.shape[0]
    tm = 512
    nt = T // tm
    last = N_FF - 1

    def body(xn_ref, h1_ref, dy_ref, g_ref, wu_ref, wd_ref, dh1_ref, dwu_ref, dwd_ref, dg_ref,
             dxn_acc, acc_u, acc_d):
        j = pl.program_id(0)
        i = pl.program_id(1)
        x = xn_ref[...]
        dy = dy_ref[...]
        dyb = dy.astype(BF16)
        wu = wu_ref[0]
        wd = wd_ref[0]
        a = jnp.maximum(_dot(x, wu), 0.0)
        hh = (a * a).astype(BF16)
        du = (_dot_nt(dyb, wd) * (2.0 * a)).astype(BF16)
        cu_ = _dot_tn(x, du)
        cd_ = _dot_tn(hh, dyb)

        @pl.when(i == 0)
        def _():
            acc_u[...] = cu_
            acc_d[...] = cd_

        @pl.when(i > 0)
        def _():
            acc_u[...] += cu_
            acc_d[...] += cd_

        @pl.when(i == nt - 1)
        def _():
            dwu_ref[0, 0] = acc_u[...].astype(BF16)
            dwd_ref[0, 0] = acc_d[...].astype(BF16)

        rows = pl.ds(pl.multiple_of(i * tm, tm), tm)
        dxn = _dot_nt(du, wu)

        @pl.when(j == 0)
        def _():
            dxn_acc[rows, :] = dxn

        @pl.when(j > 0)
        def _():
            dxn_acc[rows, :] += dxn

        @pl.when(j == last)
        def _():
            @pl.when(i == 0)
            def _():
                dg_ref[...] = jnp.zeros_like(dg_ref)

            h1 = h1_ref[...]
            r = lax.rsqrt(jnp.mean(h1 * h1, axis=-1, keepdims=True) + EPS)
            dx, dgam = _rms_bwd(dxn_acc[rows, :], h1, r, g_ref[...])
            dh1_ref[...] = dy + dx
            dg_ref[...] += dgam

    late = lambda j, i: (jnp.where(j == last, i, 0), 0)
    return _pcall(
        body, name="mlp_bwd", grid=(N_FF, nt),
        in_specs=[pl.BlockSpec((tm, D), lambda j, i: (i, 0)), pl.BlockSpec((tm, D), late),
                  pl.BlockSpec((tm, D), lambda j, i: (i, 0)), pl.BlockSpec((1, D), lambda j, i: (0, 0)),
                  pl.BlockSpec((1, D, FF_BLK), lambda j, i: (j, 0, 0)),
                  pl.BlockSpec((1, FF_BLK, D), lambda j, i: (j, 0, 0))],
        out_specs=[pl.BlockSpec((tm, D), late),
                   pl.BlockSpec((1, 1, D, FF_BLK), lambda j, i: (j % 2, j // 2, 0, 0)),
                   pl.BlockSpec((1, 1, FF_BLK, D), lambda j, i: (j % 2, j // 2, 0, 0)),
                   pl.BlockSpec((1, D), lambda j, i: (0, 0))],
        out_shape=[jax.ShapeDtypeStruct((T, D), F32), jax.ShapeDtypeStruct((2, 4, D, FF_BLK), BF16),
                   jax.ShapeDtypeStruct((2, 4, FF_BLK, D), BF16), jax.ShapeDtypeStruct((1, D), F32)],
        scratch_shapes=[pltpu.VMEM((T, D), F32), pltpu.VMEM((D, FF_BLK), F32), pltpu.VMEM((FF_BLK, D), F32)],
        sem=("arbitrary", "arbitrary"), operands=(xn2, h1, dh2, gamma, w_up, w_down), jobs=jobs)


def _outproj_bwd(dh1, o_gla, o_conv, o_att, w_out, jobs=()):
    T = dh1.shape[0]
    tm = 512
    nt = T // tm

    def body(dy_ref, og_ref, oc_ref, oa_ref, wo_ref, dg_ref, dc_ref, da_ref, dw_ref, acc):
        i = pl.program_id(0)
        dyb = dy_ref[...].astype(BF16)
        dm = _dot_nt(dyb, wo_ref[...])
        dg_ref[...] = dm[:, 0:384].astype(BF16)
        dc_ref[...] = dm[:, 384:640].astype(BF16)
        da_ref[...] = dm[:, 640:1024].astype(BF16)
        mixed = jnp.concatenate([og_ref[...], oc_ref[...], oa_ref[...]], axis=1)
        contrib = _dot_tn(mixed, dyb)

        @pl.when(i == 0)
        def _():
            acc[...] = contrib

        @pl.when(i > 0)
        def _():
            acc[...] += contrib

        @pl.when(i == nt - 1)
        def _():
            for j in range(N_DEV):
                dw_ref[j % 2, j // 2] = acc[j * 128:(j + 1) * 128, :].astype(BF16)

    return _pcall(
        body, name="outproj_bwd", grid=(nt,),
        in_specs=[_row_spec(tm, D), _row_spec(tm, 384), _row_spec(tm, 256), _row_spec(tm, 384),
                  _full_spec((D, D))],
        out_specs=[_row_spec(tm, 384), _row_spec(tm, 256), _row_spec(tm, 384), _full_spec((2, 4, 128, D))],
        out_shape=[jax.ShapeDtypeStruct((T, 384), BF16), jax.ShapeDtypeStruct((T, 256), BF16),
                   jax.ShapeDtypeStruct((T, 384), BF16), jax.ShapeDtypeStruct((2, 4, 128, D), BF16)],
        scratch_shapes=[pltpu.VMEM((D, D), F32)],
        sem=("arbitrary",), operands=(dh1, o_gla, o_conv, o_att, w_out), jobs=jobs)


def _loss_fwd_bwd(h, gamma, target):
    T = h.shape[0]
    tm = 512

    def body(h_ref, g_ref, t_ref, loss_ref, dh_ref, dg_ref):
        @pl.when(pl.program_id(0) == 0)
        def _():
            loss_ref[...] = jnp.zeros_like(loss_ref)
            dg_ref[...] = jnp.zeros_like(dg_ref)

        x = h_ref[...]
        r = lax.rsqrt(jnp.mean(x * x, axis=-1, keepdims=True) + EPS)
        gamma_ = g_ref[...]
        e = x * r * gamma_ - t_ref[...]
        loss_ref[...] += 0.5 * jnp.sum(jnp.mean(e * e, axis=-1, keepdims=True), axis=0, keepdims=True)
        dx, dgam = _rms_bwd(e * (1.0 / D), x, r, gamma_)
        dh_ref[...] = dx
        dg_ref[...] += dgam

    return pl.pallas_call(
        body, name="loss_fwd_bwd", grid=(T // tm,),
        in_specs=[_row_spec(tm, D), _full_spec((1, D)), _row_spec(tm, D)],
        out_specs=[_full_spec((8, 128)), _row_spec(tm, D), _full_spec((1, D))],
        out_shape=[jax.ShapeDtypeStruct((8, 128), F32), jax.ShapeDtypeStruct((T, D), F32),
                   jax.ShapeDtypeStruct((1, D), F32)],
        compiler_params=_cp(("arbitrary",)),
    )(h, gamma, target)


def _adamw_math(w, g, m, v):
    m = ADAM_B1 * m + (1.0 - ADAM_B1) * g
    v = ADAM_B2 * v + (1.0 - ADAM_B2) * (g * g)
    m_hat = m / (1.0 - ADAM_B1 ** ADAM_STEP)
    v_hat = v / (1.0 - ADAM_B2 ** ADAM_STEP)
    delta = -ADAM_LR * (m_hat / (jnp.sqrt(v_hat) + ADAM_EPS) + ADAM_WD * w)
    return delta, m, v


def _rs_adamw(a_own, r2, w, m, v, layer, chip_idx, rows_blk, prev=None):
    _, R, C = w.shape
    nblk = R // rows_blk

    def body(chip_ref, a_ref, r_ref, w_ref, m_ref, v_ref, *rest):
        g_out, d_out, m_out, v_out = rest[-4:]
        g = (a_ref[0].astype(F32) + r_ref[0].astype(F32)) + (r_ref[1].astype(F32) + r_ref[2].astype(F32))
        delta, m_new, v_new = _adamw_math(w_ref[0], g, m_ref[0], v_ref[0])
        g_out[0] = g
        d_out[0] = delta
        m_out[0] = m_new
        v_out[0] = v_new

    blk = pl.BlockSpec((1, rows_blk, C), lambda i, chip: (layer, i, 0))
    n_prev = 0 if prev is None else 4
    grid_spec = pltpu.PrefetchScalarGridSpec(
        num_scalar_prefetch=1, grid=(nblk,),
        in_specs=[pl.BlockSpec((1, rows_blk, C), lambda i, chip: (chip[0], i, 0)),
                  pl.BlockSpec((3, rows_blk, C), lambda i, chip: (0, i, 0)), blk, blk, blk]
        + [_any_spec()] * n_prev,
        out_specs=[blk, blk, blk, blk])
    return pl.pallas_call(
        body, name="rs_adamw", grid_spec=grid_spec,
        out_shape=[jax.ShapeDtypeStruct((DEPTH, R, C), F32)] * 4,
        input_output_aliases={6 + t: t for t in range(n_prev)},
        compiler_params=_cp(("arbitrary",)),
    )(chip_idx, a_own, r2, w, m, v, *(prev or ()))


def _pair_sum(g, r1, core_idx, rows_blk):
    _, _, R, C = g.shape
    nblk = R // rows_blk

    def body(core_ref, g_ref, r_ref, o_ref):
        o_ref[...] = (g_ref[0].astype(F32) + r_ref[...].astype(F32)).astype(BF16)

    grid_spec = pltpu.PrefetchScalarGridSpec(
        num_scalar_prefetch=1, grid=(4, nblk),
        in_specs=[pl.BlockSpec((1, 1, rows_blk, C), lambda k, i, core: (core[0], k, i, 0)),
                  pl.BlockSpec((1, rows_blk, C), lambda k, i, core: (k, i, 0))],
        out_specs=pl.BlockSpec((1, rows_blk, C), lambda k, i, core: (k, i, 0)))
    return pl.pallas_call(
        body, name="rs_pair_sum", grid_spec=grid_spec,
        out_shape=jax.ShapeDtypeStruct((4, R, C), BF16),
        compiler_params=_cp(("arbitrary", "arbitrary")),
    )(core_idx, g, r1)


def _small_sum(gathered):
    def body(g_ref, o_ref):
        acc = g_ref[0]
        for d in range(1, N_DEV):
            acc = acc + g_ref[d]
        o_ref[...] = acc

    return pl.pallas_call(
        body, name="small_sum",
        out_shape=jax.ShapeDtypeStruct((SMALL_ROWS, 1024), F32),
        compiler_params=_cp(),
    )(gathered)


def _adamw_small(ws, gs, ms, vs):
    n = len(ws)

    def body(*refs):
        w_r, g_r, m_r, v_r = refs[0:n], refs[n:2 * n], refs[2 * n:3 * n], refs[3 * n:4 * n]
        d_o, m_o, v_o = refs[4 * n:5 * n], refs[5 * n:6 * n], refs[6 * n:7 * n]
        for t in range(n):
            delta, m_new, v_new = _adamw_math(w_r[t][...], g_r[t][...], m_r[t][...], v_r[t][...])
            d_o[t][...] = delta
            m_o[t][...] = m_new
            v_o[t][...] = v_new

    shapes = [jax.ShapeDtypeStruct(w.shape, F32) for w in ws]
    outs = pl.pallas_call(
        body, name="adamw_small", out_shape=shapes * 3, compiler_params=_cp(),
    )(*ws, *gs, *ms, *vs)
    return outs[0:n], outs[n:2 * n], outs[2 * n:3 * n]


def _mesh_pos():
    return lax.axis_index("x"), lax.axis_index("y"), lax.axis_index("c")


def _peers():
    x, y, c = _mesh_pos()
    return (x, y, c), (x, y, 1 - c), [(1 - x, y), (x, 1 - y), (1 - x, 1 - y)]


def _slot(ref, pos):
    return ref.at[4 * pos[0] + 2 * pos[1] + pos[2]]


def _remote(src, dst, send_sem, recv_sem, to):
    return pltpu.make_async_remote_copy(src_ref=src, dst_ref=dst, send_sem=send_sem, recv_sem=recv_sem,
                                        device_id=to, device_id_type=MESH)


def _ag_spread(shards):
    n = len(shards)

    def copies(ins, outs, sems):
        send, recv, loc = sems
        me, sibling, chips = _peers()
        peers = [sibling] + [(*chip, me[2]) for chip in chips]
        local = [pltpu.make_async_copy(ins[a], _slot(outs[a], me), loc.at[a]) for a in range(n)]
        sends = [_remote(ins[a], _slot(outs[a], me), send.at[a, k], recv.at[a, k], p)
                 for a in range(n) for k, p in enumerate(peers)]
        recvs = [_remote(ins[a], _slot(outs[a], p), send.at[a, k], recv.at[a, k], p)
                 for a in range(n) for k, p in enumerate(peers)]
        return local, sends, recvs

    def start(ins, outs, sems):
        local, sends, _ = copies(ins, outs, sems)
        for cp in local + sends:
            cp.start()

    def finish(ins, outs, sems):
        local, sends, recvs = copies(ins, outs, sems)
        for cp in sends:
            cp.wait_send()
        for cp in recvs:
            cp.wait_recv()
        for cp in local:
            cp.wait()

    return _Job(shards, [jax.ShapeDtypeStruct((N_DEV,) + a.shape, a.dtype) for a in shards],
                [pltpu.SemaphoreType.DMA((n, 4)), pltpu.SemaphoreType.DMA((n, 4)), pltpu.SemaphoreType.DMA((n,))],
                start, finish)


def _ag_pass(stacks):
    n = len(stacks)

    def copies(ins, outs, sems):
        send, recv = sems
        me, sibling, chips = _peers()
        sends = [_remote(_slot(ins[a], (*chip, me[2])), _slot(outs[a], (*chip, me[2])), send.at[a, j], recv.at[a, j],
                         sibling) for a in range(n) for j, chip in enumerate(chips)]
        recvs = [_remote(_slot(ins[a], (*chip, me[2])), _slot(outs[a], (*chip, 1 - me[2])), send.at[a, j],
                         recv.at[a, j], sibling) for a in range(n) for j, chip in enumerate(chips)]
        return sends, recvs

    def start(ins, outs, sems):
        for cp in copies(ins, outs, sems)[0]:
            cp.start()

    def finish(ins, outs, sems):
        sends, recvs = copies(ins, outs, sems)
        for cp in sends:
            cp.wait_send()
        for cp in recvs:
            cp.wait_recv()

    return _Job(stacks, [jax.ShapeDtypeStruct(a.shape, a.dtype) for a in stacks],
                [pltpu.SemaphoreType.DMA((n, 3)), pltpu.SemaphoreType.DMA((n, 3))],
                start, finish, aliases={a: a for a in range(n)})


def _rs_swap(parts):
    n = len(parts)

    def copies(ins, outs, sems):
        send, recv = sems
        me, sibling, _ = _peers()
        return [_remote(ins[a].at[1 - me[2]], outs[a], send.at[a], recv.at[a], sibling) for a in range(n)]

    def start(ins, outs, sems):
        for cp in copies(ins, outs, sems):
            cp.start()

    def finish(ins, outs, sems):
        for cp in copies(ins, outs, sems):
            cp.wait()

    return _Job(parts, [jax.ShapeDtypeStruct(a.shape[1:], a.dtype) for a in parts],
                [pltpu.SemaphoreType.DMA((n,)), pltpu.SemaphoreType.DMA((n,))], start, finish)


def _rs_ici(pairs):
    n = len(pairs)

    def copies(ins, outs, sems):
        send, recv = sems
        me, _, chips = _peers()
        return [_remote(ins[a].at[2 * chip[0] + chip[1]], outs[a].at[j], send.at[a, j], recv.at[a, j],
                        (*chip, me[2])) for a in range(n) for j, chip in enumerate(chips)]

    def start(ins, outs, sems):
        for cp in copies(ins, outs, sems):
            cp.start()

    def finish(ins, outs, sems):
        for cp in copies(ins, outs, sems):
            cp.wait()

    return _Job(pairs, [jax.ShapeDtypeStruct((3,) + a.shape[1:], a.dtype) for a in pairs],
                [pltpu.SemaphoreType.DMA((n, 3)), pltpu.SemaphoreType.DMA((n, 3))], start, finish)


def _comm_call(jobs, name):
    def body():
        pass

    return _pcall(body, name=name, grid=(), in_specs=[], out_specs=[], out_shape=[], operands=(), jobs=jobs)[1]


def _allgather(arrs, name):
    n = len(arrs)

    def body(*refs):
        ins, outs = refs[:n], refs[n:2 * n]
        send_sems, recv_sems, local_sems = refs[2 * n:]
        x, y, c = _mesh_pos()
        me, sibling = (x, y, c), (x, y, 1 - c)
        chips = [(1 - x, y), (x, 1 - y), (1 - x, 1 - y)]

        def slot(a, pos):
            return outs[a].at[4 * pos[0] + 2 * pos[1] + pos[2]]

        def copy(a, k, block, to, src=None):
            return pltpu.make_async_remote_copy(
                src_ref=slot(a, block) if src is None else src, dst_ref=slot(a, block),
                send_sem=send_sems.at[a, k], recv_sem=recv_sems.at[a, k],
                device_id=to, device_id_type=MESH)

        mine = [pltpu.make_async_copy(ins[a], slot(a, me), local_sems.at[a]) for a in range(n)]
        for cp in mine:
            cp.start()
        first = []
        for a in range(n):
            first.append(copy(a, 0, me, sibling, src=ins[a]))
            first += [copy(a, 1 + j, me, (*chip, c), src=ins[a]) for j, chip in enumerate(chips)]
        for cp in first:
            cp.start()
        passed = []
        for j, chip in enumerate(chips):
            for a in range(n):
                copy(a, 1 + j, (*chip, c), me).wait_recv()
                fwd = copy(a, 4 + j, (*chip, c), sibling)
                fwd.start()
                passed.append(fwd)
        for a in range(n):
            copy(a, 0, sibling, me).wait_recv()
            for j, chip in enumerate(chips):
                copy(a, 4 + j, (*chip, 1 - c), me).wait_recv()
        for cp in first + passed:
            cp.wait_send()
        for cp in mine:
            cp.wait()

    return pl.pallas_call(
        body, name=name,
        in_specs=[_any_spec()] * n, out_specs=[_any_spec()] * n,
        out_shape=[jax.ShapeDtypeStruct((N_DEV,) + a.shape, a.dtype) for a in arrs],
        scratch_shapes=[pltpu.SemaphoreType.DMA((n, 7)), pltpu.SemaphoreType.DMA((n, 7)),
                        pltpu.SemaphoreType.DMA((n,))],
        compiler_params=_cp(),
    )(*arrs)


W_IN_SHARD = 354
W_IN_COLS = ((0, 192, OQ), (192, 192, OKK), (384, 384, OV), (768, 384, OG), (1152, 16, OLR), (1168, 512, OCU),
             (1680, 384, OAQ), (2064, 384, OAK), (2448, 384, OAV))


def _w_in_padded(stack):
    new_to_ref = {new: (start, width) for start, width, new in W_IN_COLS}
    cols = []
    for new, padded in IN_GROUPS:
        start, width = new_to_ref[new]
        a = start
        while a < start + width:
            j = a // W_IN_SHARD
            b = min(start + width, (j + 1) * W_IN_SHARD)
            cols.append(stack[j][:, a - j * W_IN_SHARD:b - j * W_IN_SHARD])
            a = b
        if padded > width:
            cols.append(jnp.zeros((stack.shape[1], padded - width), stack.dtype))
    return jnp.concatenate(cols, axis=1)


def _dw_in_shards(dw):
    shards = []
    for j in range(N_DEV):
        lo, hi = j * W_IN_SHARD, (j + 1) * W_IN_SHARD
        segs = []
        for start, width, new in W_IN_COLS:
            a, b = max(lo, start), min(hi, start + width)
            if a < b:
                segs.append(dw[:, new + a - start:new + b - start])
        shards.append(jnp.concatenate(segs, axis=1))
    return jnp.stack([jnp.stack([shards[2 * chip + core] for chip in range(4)]) for core in range(2)])


def _pad_to(a, shape):
    return jnp.pad(a, [(0, s - d) for d, s in zip(a.shape, shape)])


SMALL_LAYOUT = (
    ("norm_mix", 2, 1024), ("norm_ffn", 2, 1024), ("norm_final", 1, 1024), ("gla_norm", 2, 384),
    ("b_gla_gate", 2, 192), ("b_dw", 2, 256), ("conv_ln_g", 2, 256), ("conv_ln_b", 2, 256),
    ("rel_bias", 12, 257), ("w_gla_gate", 32, 192), ("w_dw", 62, 256),
)
SUBLANES = 8


def _tile_rows(r):
    return -(-r // SUBLANES) * SUBLANES


SMALL_ROWS = sum(_tile_rows(r) for _, r, _ in SMALL_LAYOUT)


def _pack_small(parts):
    return jnp.concatenate([_pad_to(parts[name], (_tile_rows(r), 1024)) for name, r, _ in SMALL_LAYOUT], axis=0)


def _unpack_small(packed):
    out, r0 = {}, 0
    for name, r, lanes in SMALL_LAYOUT:
        out[name] = packed[r0:r0 + r, 0:lanes]
        r0 += _tile_rows(r)
    return out


def _mixers_fwd(h, wl, w_in_p, att_jobs=(), gla_jobs_fn=None):
    q, k, v, g, cu, aq, ak, av, lr = _inproj_fwd(h, wl["norm_mix"], w_in_p)
    bias = _relbias_expand(wl["rb"])
    kpad = jnp.pad(ak, ((AK_PAD, 0), (0, 0)))
    vpad = jnp.pad(av, ((AK_PAD, 0), (0, 0)))
    (o_att,), att_res = _att_fwd(aq, kpad, vpad, bias, jobs=att_jobs)
    gla_jobs = gla_jobs_fn(att_res) if gla_jobs_fn is not None else ()
    (o_gla, states), gla_res = _gla_fwd(q, k, v, g, lr, wl["wg"], wl["bg"], wl["gn"], jobs=gla_jobs)
    o_conv = _conv_fwd(cu, wl["w_dw"], wl["b_dw"], wl["ln_g"], wl["ln_b"])
    sv = dict(h=h, w_in=w_in_p, q=q, k=k, v=v, g=g, cu=cu, aq=aq, kpad=kpad, vpad=vpad, lr=lr,
              o_gla=o_gla, o_conv=o_conv, o_att=o_att, states=states, bias=bias)
    return sv, att_res, gla_res


def _mixers_bwd(sv, wl, dh1, d_ogla, d_oconv, att_grads):
    daq, dak, dav, dbias = att_grads
    d_rb = _relbias_grad(dbias)
    dcu, dw_dw, db_dw, dln_g, dln_b = _conv_bwd(sv["cu"], d_oconv, wl["w_dw"], wl["b_dw"], wl["ln_g"], wl["ln_b"])
    dq, dk, dv, dg, dlr, dwg, dbg, dgn = _gla_bwd(sv["q"], sv["k"], sv["v"], sv["g"], sv["lr"], sv["states"],
                                                  d_ogla, wl["wg"], wl["bg"], wl["gn"])
    dh, dw_in, d_nmix = _inproj_bwd(sv["h"], dh1, wl["norm_mix"], sv["w_in"],
                                    (dq, dk, dv, dg, dcu, daq, dak, dav, dlr))
    small = dict(norm_mix=d_nmix, wg=dwg, bg=dbg, gn=dgn, w_dw=dw_dw, b_dw=db_dw, ln_g=dln_g, ln_b=dln_b, rb=d_rb)
    return dh, dw_in, small


def _layer_small(l, w_dw_full, norm_mix, w_gla_gate, b_gla_gate, gla_norm, b_dw, conv_ln_g, conv_ln_b, rel_bias,
                 norm_ffn):
    return dict(
        norm_mix=norm_mix[l][None, :], norm_ffn=norm_ffn[l][None, :],
        wg=_pad_to(w_gla_gate[l], (128, 256)).astype(BF16), bg=_pad_to(b_gla_gate[l][None, :], (1, 256)),
        gn=gla_norm[l][None, :], w_dw=_pad_to(w_dw_full, (32, 256)), b_dw=b_dw[l][None, :],
        ln_g=conv_ln_g[l][None, :], ln_b=conv_ln_b[l][None, :], rb=_pad_to(rel_bias[l], (8, 384)))


RS_ROWS = dict(w_in=256, w_out=128, w_up=256, w_down=256)


def kernel(x, norm_mix, w_in, w_gla_gate, b_gla_gate, gla_norm, w_dw, b_dw, conv_ln_g, conv_ln_b, rel_bias, w_out, norm_ffn, w_up, w_down, norm_final, loss_target, m_norm_mix, m_w_in, m_w_gla_gate, m_b_gla_gate, m_gla_norm, m_w_dw, m_b_dw, m_conv_ln_g, m_conv_ln_b, m_rel_bias, m_w_out, m_norm_ffn, m_w_up, m_w_down, m_norm_final, v_norm_mix, v_w_in, v_w_gla_gate, v_b_gla_gate, v_gla_norm, v_w_dw, v_b_dw, v_conv_ln_g, v_conv_ln_b, v_rel_bias, v_w_out, v_norm_ffn, v_w_up, v_w_down, v_norm_final):
    mx, my, mc = _mesh_pos()
    me = 4 * mx + 2 * my + mc
    chip_idx = (2 * mx + my).astype(jnp.int32).reshape(1)
    core_idx = mc.astype(jnp.int32).reshape(1)
    x0, target = x[0], loss_target[0]

    def pair_sums(parts, r1, names):
        return [_pair_sum(p, r, core_idx, RS_ROWS[n]) for p, r, n in zip(parts, r1, names)]

    sh = [dict(w_in=w_in[l].astype(BF16), w_out=w_out[l].astype(BF16), w_up=w_up[l].astype(BF16),
               w_down=w_down[l].astype(BF16)) for l in range(DEPTH)]
    dw_flat = _pad_to(w_dw, (DEPTH, 32, 32)).reshape(16, 128)
    st_in0, st_dw = _allgather([sh[0]["w_in"], dw_flat], "allgather_first")
    dw_all = st_dw.reshape(N_DEV, DEPTH, 32, 32)[:, :, :KCONV, :]
    dw_all = jnp.transpose(dw_all, (1, 2, 0, 3)).reshape(DEPTH, KCONV, 256)
    wl = [_layer_small(l, dw_all[l], norm_mix, w_gla_gate, b_gla_gate, gla_norm, b_dw, conv_ln_g, conv_ln_b,
                       rel_bias, norm_ffn) for l in range(DEPTH)]

    sv0, _, (st_out0, st_up0, st_down0) = _mixers_fwd(
        x0, wl[0], _w_in_padded(st_in0),
        att_jobs=[_ag_spread([sh[0]["w_out"], sh[0]["w_up"], sh[0]["w_down"]])],
        gla_jobs_fn=lambda spread: [_ag_pass(spread)])
    wo0 = st_out0.reshape(D, D)
    (h1_0, xn2_0, h2_0), st1 = _outproj_mlp_fwd(
        x0, sv0["o_gla"], sv0["o_conv"], sv0["o_att"], wo0, wl[0]["norm_ffn"], st_up0, st_down0,
        jobs=[_ag_spread([sh[1]["w_in"], sh[1]["w_out"], sh[1]["w_up"], sh[1]["w_down"]])])

    (st_in1,) = _comm_call([_ag_pass([st1[0]])], "allgather_pass_w_in")
    sv1, (st_out1, st_up1, st_down1), _ = _mixers_fwd(
        h2_0, wl[1], _w_in_padded(st_in1), att_jobs=[_ag_pass(st1[1:])])
    wo1 = st_out1.reshape(D, D)
    (h1_1, xn2_1, h2_1), _ = _outproj_mlp_fwd(
        h2_0, sv1["o_gla"], sv1["o_conv"], sv1["o_att"], wo1, wl[1]["norm_ffn"], st_up1, st_down1)

    loss8, dh, d_nf = _loss_fwd_bwd(h2_1, norm_final[None, :], target)
    loss = lax.psum(loss8[0, 0], ("x", "y", "c"))

    (dh1, dw_up1, dw_down1, d_nffn1), _ = _mlp_bwd(xn2_1, h1_1, dh, wl[1]["norm_ffn"], st_up1, st_down1)
    ud1 = [dw_up1, dw_down1]
    (d_ogla, d_oconv, d_oatt, dw_out1), r1 = _outproj_bwd(
        dh1, sv1["o_gla"], sv1["o_conv"], sv1["o_att"], wo1, jobs=[_rs_swap(ud1)])
    pair_ud1 = pair_sums(ud1, r1, ("w_up", "w_down"))
    att_grads, r2_ud1 = _att_bwd(sv1["aq"], sv1["kpad"], sv1["vpad"], sv1["bias"], d_oatt, jobs=[_rs_ici(pair_ud1)])
    dh, dw_in1, small1 = _mixers_bwd(sv1, wl[1], dh1, d_ogla, d_oconv, att_grads)
    small1["norm_ffn"] = d_nffn1

    io1 = [_dw_in_shards(dw_in1), dw_out1]
    (dh1, dw_up0, dw_down0, d_nffn0), r1 = _mlp_bwd(xn2_0, h1_0, dh, wl[0]["norm_ffn"], st_up0, st_down0,
                                                     jobs=[_rs_swap(io1)])
    pair_io1 = pair_sums(io1, r1, ("w_in", "w_out"))
    ud0 = [dw_up0, dw_down0]
    (d_ogla, d_oconv, d_oatt, dw_out0), r1 = _outproj_bwd(
        dh1, sv0["o_gla"], sv0["o_conv"], sv0["o_att"], wo0, jobs=[_rs_swap(ud0)])
    pair_ud0 = pair_sums(ud0, r1, ("w_up", "w_down"))
    att_grads, r2 = _att_bwd(sv0["aq"], sv0["kpad"], sv0["vpad"], sv0["bias"], d_oatt,
                             jobs=[_rs_ici(pair_io1), _rs_ici(pair_ud0)])
    r2_io1, r2_ud0 = r2[:2], r2[2:]
    dx, dw_in0, small0 = _mixers_bwd(sv0, wl[0], dh1, d_ogla, d_oconv, att_grads)
    small0["norm_ffn"] = d_nffn0

    io0 = [_dw_in_shards(dw_in0), dw_out0]
    r1 = _comm_call([_rs_swap(io0)], "rs_swap_last")
    pair_io0 = pair_sums(io0, r1, ("w_in", "w_out"))
    r2_io0 = _comm_call([_rs_ici(pair_io0)], "rs_ici_last")

    big_w = dict(w_in=(w_in, m_w_in, v_w_in), w_out=(w_out, m_w_out, v_w_out), w_up=(w_up, m_w_up, v_w_up),
                 w_down=(w_down, m_w_down, v_w_down))
    pairs = {1: dict(w_in=(pair_io1[0], r2_io1[0]), w_out=(pair_io1[1], r2_io1[1]),
                     w_up=(pair_ud1[0], r2_ud1[0]), w_down=(pair_ud1[1], r2_ud1[1])),
             0: dict(w_in=(pair_io0[0], r2_io0[0]), w_out=(pair_io0[1], r2_io0[1]),
                     w_up=(pair_ud0[0], r2_ud0[0]), w_down=(pair_ud0[1], r2_ud0[1]))}
    big_out = {}
    for name, (w_, m_, v_) in big_w.items():
        res = None
        for l in (1, 0):
            a_own, r2_ = pairs[l][name]
            res = _rs_adamw(a_own, r2_, w_, m_, v_, l, chip_idx, RS_ROWS[name], prev=res)
        big_out[name] = res

    grads = (small0, small1)
    parts = dict(
        norm_mix=jnp.concatenate([grads[l]["norm_mix"] for l in range(DEPTH)], axis=0),
        norm_ffn=jnp.concatenate([grads[l]["norm_ffn"] for l in range(DEPTH)], axis=0),
        norm_final=d_nf,
        gla_norm=jnp.concatenate([grads[l]["gn"] for l in range(DEPTH)], axis=0),
        b_gla_gate=jnp.concatenate([grads[l]["bg"][:, :192] for l in range(DEPTH)], axis=0),
        b_dw=jnp.concatenate([grads[l]["b_dw"] for l in range(DEPTH)], axis=0),
        conv_ln_g=jnp.concatenate([grads[l]["ln_g"] for l in range(DEPTH)], axis=0),
        conv_ln_b=jnp.concatenate([grads[l]["ln_b"] for l in range(DEPTH)], axis=0),
        rel_bias=jnp.concatenate([grads[l]["rb"][:6, :N_REL] for l in range(DEPTH)], axis=0),
        w_gla_gate=jnp.concatenate([grads[l]["wg"][:16, :192] for l in range(DEPTH)], axis=0),
        w_dw=jnp.concatenate([grads[l]["w_dw"][:KCONV] for l in range(DEPTH)], axis=0),
    )
    small_all = _allgather([_pack_small(parts)], "allgather_small")[0]
    sg = _unpack_small(_small_sum(small_all))
    dw_grad = lax.dynamic_slice_in_dim(sg["w_dw"].reshape(DEPTH, KCONV, 256), me * 32, 32, axis=2)
    small_g = dict(
        norm_mix=sg["norm_mix"], w_gla_gate=sg["w_gla_gate"].reshape(DEPTH, 16, 192), b_gla_gate=sg["b_gla_gate"],
        gla_norm=sg["gla_norm"], w_dw=dw_grad, b_dw=sg["b_dw"], conv_ln_g=sg["conv_ln_g"],
        conv_ln_b=sg["conv_ln_b"], rel_bias=sg["rel_bias"].reshape(DEPTH, 6, N_REL), norm_ffn=sg["norm_ffn"],
        norm_final=sg["norm_final"].reshape(D))
    small_names = ("norm_mix", "w_gla_gate", "b_gla_gate", "gla_norm", "w_dw", "b_dw", "conv_ln_g", "conv_ln_b",
                   "rel_bias", "norm_ffn", "norm_final")
    small_w = dict(norm_mix=norm_mix, w_gla_gate=w_gla_gate, b_gla_gate=b_gla_gate, gla_norm=gla_norm, w_dw=w_dw,
                   b_dw=b_dw, conv_ln_g=conv_ln_g, conv_ln_b=conv_ln_b, rel_bias=rel_bias, norm_ffn=norm_ffn,
                   norm_final=norm_final)
    small_m = dict(norm_mix=m_norm_mix, w_gla_gate=m_w_gla_gate, b_gla_gate=m_b_gla_gate, gla_norm=m_gla_norm,
                   w_dw=m_w_dw, b_dw=m_b_dw, conv_ln_g=m_conv_ln_g, conv_ln_b=m_conv_ln_b, rel_bias=m_rel_bias,
                   norm_ffn=m_norm_ffn, norm_final=m_norm_final)
    small_v = dict(norm_mix=v_norm_mix, w_gla_gate=v_w_gla_gate, b_gla_gate=v_b_gla_gate, gla_norm=v_gla_norm,
                   w_dw=v_w_dw, b_dw=v_b_dw, conv_ln_g=v_conv_ln_g, conv_ln_b=v_conv_ln_b, rel_bias=v_rel_bias,
                   norm_ffn=v_norm_ffn, norm_final=v_norm_final)
    s_delta, s_m, s_v = _adamw_small([small_w[n] for n in small_names], [small_g[n] for n in small_names],
                                     [small_m[n] for n in small_names], [small_v[n] for n in small_names])
    s_idx = {n: t for t, n in enumerate(small_names)}

    order = ("norm_mix", "w_in", "w_gla_gate", "b_gla_gate", "gla_norm", "w_dw", "b_dw", "conv_ln_g", "conv_ln_b",
             "rel_bias", "w_out", "norm_ffn", "w_up", "w_down", "norm_final")

    def pick(kind, name):
        if name in big_out:
            return big_out[name][kind]
        t = s_idx[name]
        return (small_g[name], s_delta[t], s_m[t], s_v[t])[kind]

    outs = [loss, dx[None]]
    for kind in range(4):
        outs += [pick(kind, n) for n in order]
    return tuple(outs)
```

```python
import functools

import jax
import jax.numpy as jnp
from jax import lax
from jax.experimental import pallas as pl
from jax.experimental.pallas import tpu as pltpu

F32 = jnp.float32
BF16 = jnp.bfloat16
MESH = pl.DeviceIdType.MESH

D = 1024
DEPTH = 2
CH = 64
EPS = 1e-6
NEG = -1e30
N_DEV = 8
N_REL = 257
Q_SCALE = 48.0 ** -0.5
A_SCALE = 64.0 ** -0.5
GATE_TAU = 16.0
KCONV = 31

OQ, OKK, OV, OG, OCU, OAQ, OAK, OAV, OLR, DINP = 0, 256, 512, 896, 1280, 1792, 2176, 2560, 2944, 3072
IN_GROUPS = ((OQ, 256), (OKK, 256), (OV, 384), (OG, 384), (OCU, 512), (OAQ, 384), (OAK, 384), (OAV, 384), (OLR, 128))

AQ_BLK = 256
AK_WIN = 768
AK_PAD = 512
RB_W = 1536

ADAM_LR, ADAM_B1, ADAM_B2, ADAM_EPS, ADAM_WD, ADAM_STEP = 0.001, 0.9, 0.999, 1e-08, 0.01, 10


V7X_VMEM_MIB = 64
VMEM_LIMIT_MIB = V7X_VMEM_MIB - 1


def _cp(sem=None):
    kw = {"vmem_limit_bytes": VMEM_LIMIT_MIB * 1024 * 1024}
    if sem is not None:
        kw["dimension_semantics"] = sem
    return pltpu.CompilerParams(**kw)


def _dot(a, b):
    return jnp.dot(a, b, preferred_element_type=F32)


def _dot_nt(a, b):
    return lax.dot_general(a, b, (((1,), (1,)), ((), ())), preferred_element_type=F32)


def _dot_tn(a, b):
    return lax.dot_general(a, b, (((0,), (0,)), ((), ())), preferred_element_type=F32)


def _split2(a):
    hi = a.astype(BF16)
    lo = (a - hi.astype(F32)).astype(BF16)
    return hi, lo


def _split3(a):
    hi = a.astype(BF16)
    r1 = a - hi.astype(F32)
    mid = r1.astype(BF16)
    lo = (r1 - mid.astype(F32)).astype(BF16)
    return hi, mid, lo


def _sigmoid(x):
    return 1.0 / (1.0 + jnp.exp(-x))


def _group(idx, size, n):
    g = jnp.zeros_like(idx)
    for t in range(1, n):
        g = g + (idx >= t * size).astype(jnp.int32)
    return g


def _rms_bwd(dy, x, r, gamma):
    xh = x * r
    dxh = dy * gamma
    dx = r * (dxh - xh * jnp.mean(dxh * xh, axis=-1, keepdims=True))
    return dx, jnp.sum(dy * xh, axis=0, keepdims=True)


def _row_spec(tm, n):
    return pl.BlockSpec((tm, n), lambda i: (i, 0))


def _full_spec(shape):
    nd = len(shape)
    return pl.BlockSpec(shape, lambda *_: (0,) * nd)


def _any_spec():
    return pl.BlockSpec(memory_space=pl.ANY)


class _Job:
    def __init__(self, operands, out_shapes, sems, start, finish, aliases=None):
        self.operands, self.out_shapes, self.sems = list(operands), list(out_shapes), list(sems)
        self.start, self.finish, self.aliases = start, finish, dict(aliases or {})


def _pcall(body, *, name, grid, in_specs, out_specs, out_shape, operands, scratch_shapes=(), sem=None, jobs=()):
    jobs = list(jobs)
    in_specs, out_specs, out_shape = list(in_specs), list(out_specs), list(out_shape)
    scratch_shapes = list(scratch_shapes)
    n_in, n_out, n_scr = len(in_specs), len(out_specs), len(scratch_shapes)
    j_in = [a for j in jobs for a in j.operands]
    j_out = [s for j in jobs for s in j.out_shapes]
    j_sem = [s for j in jobs for s in j.sems]
    aliases, io, oo = {}, n_in, n_out
    for j in jobs:
        for a, b in j.aliases.items():
            aliases[io + a] = oo + b
        io += len(j.operands)
        oo += len(j.out_shapes)

    def wrapped(*refs):
        own_in, ji = refs[:n_in], refs[n_in:n_in + len(j_in)]
        o0 = n_in + len(j_in)
        own_out, jo = refs[o0:o0 + n_out], refs[o0 + n_out:o0 + n_out + len(j_out)]
        s0 = o0 + n_out + len(j_out)
        own_scr, js = refs[s0:s0 + n_scr], refs[s0 + n_scr:]

        def each_job(fn_name):
            a = b = c = 0
            for j in jobs:
                na, nb, nc = len(j.operands), len(j.out_shapes), len(j.sems)
                getattr(j, fn_name)(ji[a:a + na], jo[b:b + nb], js[c:c + nc])
                a, b, c = a + na, b + nb, c + nc

        if jobs and grid:
            pids = [pl.program_id(d) for d in range(len(grid))]
            first = functools.reduce(jnp.logical_and, [p == 0 for p in pids])
            last = functools.reduce(jnp.logical_and, [p == g - 1 for p, g in zip(pids, grid)])
            pl.when(first)(lambda: each_job("start"))
        elif jobs:
            each_job("start")

        body(*own_in, *own_out, *own_scr)

        if jobs and grid:
            pl.when(last)(lambda: each_job("finish"))
        elif jobs:
            each_job("finish")

    res = pl.pallas_call(
        wrapped, name=name, grid=grid,
        in_specs=in_specs + [_any_spec()] * len(j_in), out_specs=out_specs + [_any_spec()] * len(j_out),
        out_shape=out_shape + j_out, scratch_shapes=scratch_shapes + j_sem,
        input_output_aliases=aliases, compiler_params=_cp(sem),
    )(*operands, *j_in)
    return res[:n_out], res[n_out:]


ATT_HEADS = 6
HEAD_PAD = 128
ATT_WIDE = ATT_HEADS * HEAD_PAD
ATT_GROUP_OFFS = (OAQ, OAK, OAV)


def _store_head_padded(o_ref, part):
    o_ref[...] = jnp.zeros_like(o_ref)
    for hd in range(ATT_HEADS):
        o_ref[:, hd * HEAD_PAD:hd * HEAD_PAD + 64] = part[:, hd * 64:(hd + 1) * 64]


def _inproj_fwd(h, gamma, w):
    T = h.shape[0]
    tm = 512

    def body(h_ref, g_ref, w_ref, *outs):
        x = h_ref[...]
        r = lax.rsqrt(jnp.mean(x * x, axis=-1, keepdims=True) + EPS)
        xn = (x * r * g_ref[...]).astype(BF16)
        p = _dot(xn, w_ref[...])
        for o_ref, (off, n) in zip(outs, IN_GROUPS):
            part = p[:, off:off + n].astype(BF16)
            if off in ATT_GROUP_OFFS:
                _store_head_padded(o_ref, part)
            else:
                o_ref[...] = part

    widths = [ATT_WIDE if off in ATT_GROUP_OFFS else n for off, n in IN_GROUPS]
    return pl.pallas_call(
        body, name="inproj_fwd", grid=(T // tm,),
        in_specs=[_row_spec(tm, D), _full_spec((1, D)), _full_spec((D, DINP))],
        out_specs=[_row_spec(tm, n) for n in widths],
        out_shape=[jax.ShapeDtypeStruct((T, n), BF16) for n in widths],
        compiler_params=_cp(("parallel",)),
    )(h, gamma, w)


def _inproj_bwd(h, dh_in, gamma, w, dparts):
    T = h.shape[0]
    tm = 256
    nt = T // tm

    def body(h_ref, dhin_ref, g_ref, w_ref, *rest):
        dp_refs = rest[:9]
        dh_ref, dw_ref, dg_ref, acc = rest[9:]
        i = pl.program_id(0)

        @pl.when(i == 0)
        def _():
            acc[...] = jnp.zeros_like(acc)
            dg_ref[...] = jnp.zeros_like(dg_ref)

        x = h_ref[...]
        r = lax.rsqrt(jnp.mean(x * x, axis=-1, keepdims=True) + EPS)
        gamma_ = g_ref[...]
        xn = (x * r * gamma_).astype(BF16)
        dxn = jnp.zeros((tm, D), F32)
        for d_ref, (off, n) in zip(dp_refs, IN_GROUPS):
            d = d_ref[...]
            acc[:, off:off + n] += _dot_tn(xn, d)
            dxn = dxn + _dot_nt(d, w_ref[:, off:off + n])
        dx, dgam = _rms_bwd(dxn, x, r, gamma_)
        dh_ref[...] = dhin_ref[...] + dx
        dg_ref[...] += dgam

        @pl.when(i == nt - 1)
        def _():
            dw_ref[...] = acc[...].astype(BF16)

    return pl.pallas_call(
        body, name="inproj_bwd", grid=(nt,),
        in_specs=[_row_spec(tm, D), _row_spec(tm, D), _full_spec((1, D)), _full_spec((D, DINP))]
        + [_row_spec(tm, n) for _, n in IN_GROUPS],
        out_specs=[_row_spec(tm, D), _full_spec((D, DINP)), _full_spec((1, D))],
        out_shape=[jax.ShapeDtypeStruct((T, D), F32), jax.ShapeDtypeStruct((D, DINP), BF16),
                   jax.ShapeDtypeStruct((1, D), F32)],
        scratch_shapes=[pltpu.VMEM((D, DINP), F32)],
        compiler_params=_cp(("arbitrary",)),
    )(h, dh_in, gamma, w, *dparts)


GLA_ROWS = 512
GLA_NC = GLA_ROWS // CH


def _gla_consts():
    ri = lax.broadcasted_iota(jnp.int32, (CH, CH), 0)
    ci = lax.broadcasted_iota(jnp.int32, (CH, CH), 1)
    upper = (ci > ri).astype(BF16)
    vv = lax.broadcasted_iota(jnp.int32, (384, 256), 0)
    kk = lax.broadcasted_iota(jnp.int32, (384, 256), 1)
    mask_t = ((_group(vv, 96, 4) == _group(kk, 48, 4)) & (kk < 192)).astype(F32)
    pi = lax.broadcasted_iota(jnp.int32, (384, 384), 0)
    pj = lax.broadcasted_iota(jnp.int32, (384, 384), 1)
    same_head = (_group(pi, 96, 4) == _group(pj, 96, 4)).astype(BF16)
    return upper, mask_t, same_head


def _gla_gate(lr_ref, wg_ref, bg_ref):
    z = _dot(lr_ref[...], wg_ref[...]) + bg_ref[...]
    la = (jnp.minimum(z, 0.0) - jnp.log(1.0 + jnp.exp(-jnp.abs(z)))) * (1.0 / GATE_TAU)
    return z, la


def _gla_chunk_decay(la_c, upper):
    hi, lo = _split2(la_c)
    dec = _dot(upper, hi) + _dot(upper, lo)
    end = jnp.sum(la_c, axis=0, keepdims=True)
    return jnp.exp(dec), jnp.exp(end)


def _head_mean(x, same_head):
    hi, lo = _split2(x)
    return (_dot(hi, same_head) + _dot(lo, same_head)) * (1.0 / 96.0)


def _gla_fwd(q, k, v, g, lr, wg, bg, gn, jobs=()):
    T = q.shape[0]
    nb = T // GLA_ROWS

    def body(q_ref, k_ref, v_ref, g_ref, lr_ref, wg_ref, bg_ref, gn_ref, y_ref, st_ref, s_scr, o_scr):
        upper, mask_t, same_head = _gla_consts()

        @pl.when(pl.program_id(0) == 0)
        def _():
            s_scr[...] = jnp.zeros_like(s_scr)

        _, la = _gla_gate(lr_ref, wg_ref, bg_ref)
        for c in range(GLA_NC):
            rs = slice(c * CH, (c + 1) * CH)
            w, a = _gla_chunk_decay(la[rs], upper)
            kd = (k_ref[rs, :].astype(F32) * w).astype(BF16)
            kv_t = _dot_tn(v_ref[rs, :], kd)
            s_new = s_scr[...] * a + kv_t * mask_t
            s_scr[...] = s_new
            sb = s_new.astype(BF16)
            st_ref[c] = sb
            qs = (q_ref[rs, :].astype(F32) * Q_SCALE).astype(BF16)
            o_scr[rs, :] = _dot_nt(qs, sb)
        o = o_scr[...]
        r = lax.rsqrt(_head_mean(o * o, same_head) + EPS)
        gf = g_ref[...].astype(F32)
        y_ref[...] = (o * r * gn_ref[...] * (gf * _sigmoid(gf))).astype(BF16)

    return _pcall(
        body, name="gla_fwd", grid=(nb,),
        in_specs=[_row_spec(GLA_ROWS, 256), _row_spec(GLA_ROWS, 256), _row_spec(GLA_ROWS, 384),
                  _row_spec(GLA_ROWS, 384), _row_spec(GLA_ROWS, 128),
                  _full_spec((128, 256)), _full_spec((1, 256)), _full_spec((1, 384))],
        out_specs=[_row_spec(GLA_ROWS, 384), pl.BlockSpec((GLA_NC, 384, 256), lambda i: (i, 0, 0))],
        out_shape=[jax.ShapeDtypeStruct((T, 384), BF16), jax.ShapeDtypeStruct((T // CH, 384, 256), BF16)],
        scratch_shapes=[pltpu.VMEM((384, 256), F32), pltpu.VMEM((GLA_ROWS, 384), F32)],
        sem=("arbitrary",), operands=(q, k, v, g, lr, wg, bg, gn), jobs=jobs)


def _gla_bwd(q, k, v, g, lr, states, dy, wg, bg, gn):
    T = q.shape[0]
    nb = T // GLA_ROWS

    def rev(s):
        return nb - 1 - s

    def body(q_ref, k_ref, v_ref, g_ref, lr_ref, st_ref, stp_ref, dy_ref, wg_ref, bg_ref, gn_ref,
             dq_ref, dk_ref, dv_ref, dg_ref, dlr_ref, dwg_ref, dbg_ref, dgn_ref,
             d_scr, an_scr, o_scr, do_scr, dla_scr):
        upper, mask_t, same_head = _gla_consts()
        s = pl.program_id(0)
        blk = rev(s)

        @pl.when(s == 0)
        def _():
            d_scr[...] = jnp.zeros_like(d_scr)
            an_scr[...] = jnp.zeros_like(an_scr)
            dwg_ref[...] = jnp.zeros_like(dwg_ref)
            dbg_ref[...] = jnp.zeros_like(dbg_ref)
            dgn_ref[...] = jnp.zeros_like(dgn_ref)

        z, la = _gla_gate(lr_ref, wg_ref, bg_ref)
        ws, as_, qss, kds = [], [], [], []
        for c in range(GLA_NC):
            rs = slice(c * CH, (c + 1) * CH)
            w, a = _gla_chunk_decay(la[rs], upper)
            ws.append(w)
            as_.append(a)
            qs = (q_ref[rs, :].astype(F32) * Q_SCALE).astype(BF16)
            qss.append(qs)
            kds.append((k_ref[rs, :].astype(F32) * w).astype(BF16))
            o_scr[rs, :] = _dot_nt(qs, st_ref[c])
        o = o_scr[...]
        r = lax.rsqrt(_head_mean(o * o, same_head) + EPS)
        on = o * r
        gf = g_ref[...].astype(F32)
        sg = _sigmoid(gf)
        si = gf * sg
        dyf = dy_ref[...].astype(F32)
        gn_ = gn_ref[...]
        dgn_ref[...] += jnp.sum(dyf * si * on, axis=0, keepdims=True)
        dg_ref[...] = (dyf * on * gn_ * (sg * (1.0 + gf * (1.0 - sg)))).astype(BF16)
        d_on = dyf * si * gn_
        do_scr[...] = r * (d_on - on * _head_mean(d_on * on, same_head))

        first = (blk > 0).astype(F32)
        for c in reversed(range(GLA_NC)):
            rs = slice(c * CH, (c + 1) * CH)
            dob = do_scr[rs, :].astype(BF16)
            sb = st_ref[c]
            if c > 0:
                s_prev = st_ref[c - 1].astype(F32)
            else:
                s_prev = stp_ref[0].astype(F32) * first
            dq_ref[rs, :] = (_dot(dob, sb) * Q_SCALE).astype(BF16)
            dt = d_scr[...] * an_scr[...] + _dot_tn(dob, qss[c]) * mask_t
            d_scr[...] = dt
            da = jnp.sum(dt * s_prev, axis=0, keepdims=True)
            db = dt.astype(BF16)
            dkd = _dot(v_ref[rs, :], db)
            dv_ref[rs, :] = _dot_nt(kds[c], db).astype(BF16)
            dk_ref[rs, :] = (dkd * ws[c]).astype(BF16)
            ddec = dkd * k_ref[rs, :].astype(F32) * ws[c]
            hi, lo = _split2(ddec)
            dla_scr[rs, :] = _dot_tn(upper, hi) + _dot_tn(upper, lo) + as_[c] * da
            an_scr[...] = as_[c]

        dz = dla_scr[...] * (1.0 - _sigmoid(z)) * (1.0 / GATE_TAU)
        dzb = dz.astype(BF16)
        dlr_ref[...] = _dot_nt(dzb, wg_ref[...]).astype(BF16)
        dwg_ref[...] += _dot_tn(lr_ref[...], dzb)
        dbg_ref[...] += jnp.sum(dz, axis=0, keepdims=True)

    def rspec(n):
        return pl.BlockSpec((GLA_ROWS, n), lambda s: (rev(s), 0))

    return pl.pallas_call(
        body, name="gla_bwd", grid=(nb,),
        in_specs=[rspec(256), rspec(256), rspec(384), rspec(384), rspec(128),
                  pl.BlockSpec((GLA_NC, 384, 256), lambda s: (rev(s), 0, 0)),
                  pl.BlockSpec((1, 384, 256), lambda s: (jnp.maximum(rev(s) * GLA_NC - 1, 0), 0, 0)),
                  rspec(384), _full_spec((128, 256)), _full_spec((1, 256)), _full_spec((1, 384))],
        out_specs=[rspec(256), rspec(256), rspec(384), rspec(384), rspec(128),
                   _full_spec((128, 256)), _full_spec((1, 256)), _full_spec((1, 384))],
        out_shape=[jax.ShapeDtypeStruct((T, 256), BF16), jax.ShapeDtypeStruct((T, 256), BF16),
                   jax.ShapeDtypeStruct((T, 384), BF16), jax.ShapeDtypeStruct((T, 384), BF16),
                   jax.ShapeDtypeStruct((T, 128), BF16),
                   jax.ShapeDtypeStruct((128, 256), F32), jax.ShapeDtypeStruct((1, 256), F32),
                   jax.ShapeDtypeStruct((1, 384), F32)],
        scratch_shapes=[pltpu.VMEM((384, 256), F32), pltpu.VMEM((1, 256), F32),
                        pltpu.VMEM((GLA_ROWS, 384), F32), pltpu.VMEM((GLA_ROWS, 384), F32),
                        pltpu.VMEM((GLA_ROWS, 256), F32)],
        compiler_params=_cp(("arbitrary",)),
    )(q, k, v, g, lr, states, states, dy, wg, bg, gn)


CONV_ROWS = 512
HALO = 32
SUBL = 8
CONV_SLAB = 32
PHASE_ROWS = CONV_ROWS + HALO - SUBL
FWD_SHIFT = tuple(HALO - (KCONV - 1) + j for j in range(KCONV))
BWD_SHIFT = tuple(KCONV - 1 - j for j in range(KCONV))


def _fill_phases(buf, ph):
    for f in range(1, SUBL):
        ph[f, 0:PHASE_ROWS, :] = buf[pl.ds(f, PHASE_ROWS), :]


def _tap(buf, ph, shift, r, n):
    f, base = shift % SUBL, shift - shift % SUBL
    src = buf if f == 0 else ph.at[f]
    return src[pl.ds(base + r, n), :]


def _taps_apply(w_ref, buf, ph, shifts, out):
    for r in range(0, CONV_ROWS, CONV_SLAB):
        acc = jnp.zeros((CONV_SLAB, 256), F32)
        for j in range(KCONV):
            acc = acc + w_ref[j:j + 1, :] * _tap(buf, ph, shifts[j], r, CONV_SLAB)
        out[r:r + CONV_SLAB, :] = acc


def _conv_scratch():
    return [pltpu.VMEM((CONV_ROWS + HALO, 256), F32), pltpu.VMEM((SUBL, CONV_ROWS + HALO, 256), F32),
            pltpu.VMEM((CONV_ROWS, 256), F32)]


def _conv_common(cu_ref, halo_ref, w_ref, b_ref, lg_ref, lb_ref, buf, ph, cbuf, blk):
    u = cu_ref[...].astype(F32)
    a = u[:, :256]
    sb = _sigmoid(u[:, 256:])
    uh = halo_ref[...].astype(F32)
    hh = uh[:, :256] * _sigmoid(uh[:, 256:]) * (blk > 0).astype(F32)
    buf[0:HALO, :] = hh
    buf[HALO:HALO + CONV_ROWS, :] = a * sb
    _fill_phases(buf, ph)
    _taps_apply(w_ref, buf, ph, FWD_SHIFT, cbuf)
    cc = cbuf[...] + b_ref[...]
    mu = jnp.mean(cc, axis=-1, keepdims=True)
    xc = cc - mu
    rstd = lax.rsqrt(jnp.mean(xc * xc, axis=-1, keepdims=True) + EPS)
    n = xc * rstd
    yln = n * lg_ref[...] + lb_ref[...]
    return a, sb, n, rstd, yln


def _conv_fwd(cu, w, b, lg, lb):
    T = cu.shape[0]
    nb = T // CONV_ROWS
    per = CONV_ROWS // HALO

    def body(cu_ref, halo_ref, w_ref, b_ref, lg_ref, lb_ref, y_ref, buf, ph, cbuf):
        _, _, _, _, yln = _conv_common(cu_ref, halo_ref, w_ref, b_ref, lg_ref, lb_ref, buf, ph, cbuf,
                                       pl.program_id(0))
        y_ref[...] = (yln * _sigmoid(yln)).astype(BF16)

    return pl.pallas_call(
        body, name="conv_fwd", grid=(nb,),
        in_specs=[_row_spec(CONV_ROWS, 512),
                  pl.BlockSpec((HALO, 512), lambda i: (jnp.maximum(i * per - 1, 0), 0)),
                  _full_spec((32, 256)), _full_spec((1, 256)), _full_spec((1, 256)), _full_spec((1, 256))],
        out_specs=_row_spec(CONV_ROWS, 256),
        out_shape=jax.ShapeDtypeStruct((T, 256), BF16),
        scratch_shapes=_conv_scratch(),
        compiler_params=_cp(("parallel",)),
    )(cu, cu, w, b, lg, lb)


def _conv_bwd(cu, dy, w, b, lg, lb):
    T = cu.shape[0]
    nb = T // CONV_ROWS
    per = CONV_ROWS // HALO

    def rev(s):
        return nb - 1 - s

    def body(cu_ref, halo_ref, dy_ref, w_ref, b_ref, lg_ref, lb_ref,
             dcu_ref, dw_ref, db_ref, dlg_ref, dlb_ref, buf, ph, cbuf, dcbuf, dph, carry):
        s = pl.program_id(0)

        @pl.when(s == 0)
        def _():
            carry[...] = jnp.zeros_like(carry)
            dw_ref[...] = jnp.zeros_like(dw_ref)
            db_ref[...] = jnp.zeros_like(db_ref)
            dlg_ref[...] = jnp.zeros_like(dlg_ref)
            dlb_ref[...] = jnp.zeros_like(dlb_ref)

        a, sb, n, rstd, yln = _conv_common(cu_ref, halo_ref, w_ref, b_ref, lg_ref, lb_ref, buf, ph, cbuf, rev(s))
        sg = _sigmoid(yln)
        dyln = dy_ref[...].astype(F32) * (sg * (1.0 + yln * (1.0 - sg)))
        dlg_ref[...] += jnp.sum(dyln * n, axis=0, keepdims=True)
        dlb_ref[...] += jnp.sum(dyln, axis=0, keepdims=True)
        dn = dyln * lg_ref[...]
        dc = rstd * (dn - jnp.mean(dn, axis=-1, keepdims=True) - n * jnp.mean(dn * n, axis=-1, keepdims=True))
        db_ref[...] += jnp.sum(dc, axis=0, keepdims=True)
        dcbuf[0:CONV_ROWS, :] = dc
        dcbuf[CONV_ROWS:CONV_ROWS + HALO, :] = carry[...]
        carry[...] = dc[0:HALO, :]
        _fill_phases(dcbuf, dph)
        for j in range(KCONV):
            acc = jnp.zeros((SUBL, 256), F32)
            for r in range(0, CONV_ROWS, 2 * CONV_SLAB):
                prod = dcbuf[r:r + 2 * CONV_SLAB, :] * _tap(buf, ph, FWD_SHIFT[j], r, 2 * CONV_SLAB)
                acc = acc + jnp.sum(prod.reshape(2 * CONV_SLAB // SUBL, SUBL, 256), axis=0)
            dw_ref[j:j + 1, :] += jnp.sum(acc, axis=0, keepdims=True)
        _taps_apply(w_ref, dcbuf, dph, BWD_SHIFT, cbuf)
        dhg = cbuf[...]
        dcu_ref[...] = jnp.concatenate([dhg * sb, dhg * a * sb * (1.0 - sb)], axis=1).astype(BF16)

    def rspec(n):
        return pl.BlockSpec((CONV_ROWS, n), lambda s: (rev(s), 0))

    return pl.pallas_call(
        body, name="conv_bwd", grid=(nb,),
        in_specs=[rspec(512),
                  pl.BlockSpec((HALO, 512), lambda s: (jnp.maximum(rev(s) * per - 1, 0), 0)),
                  rspec(256),
                  _full_spec((32, 256)), _full_spec((1, 256)), _full_spec((1, 256)), _full_spec((1, 256))],
        out_specs=[rspec(512), _full_spec((32, 256)), _full_spec((1, 256)), _full_spec((1, 256)),
                   _full_spec((1, 256))],
        out_shape=[jax.ShapeDtypeStruct((T, 512), BF16), jax.ShapeDtypeStruct((32, 256), F32),
                   jax.ShapeDtypeStruct((1, 256), F32), jax.ShapeDtypeStruct((1, 256), F32),
                   jax.ShapeDtypeStruct((1, 256), F32)],
        scratch_shapes=_conv_scratch() + [pltpu.VMEM((CONV_ROWS + HALO, 256), F32),
                                          pltpu.VMEM((SUBL, CONV_ROWS + HALO, 256), F32),
                                          pltpu.VMEM((HALO, 256), F32)],
        compiler_params=_cp(("arbitrary",)),
    )(cu, cu, dy, w, b, lg, lb)


def _rel_onehot_t(shift=0):
    r = lax.broadcasted_iota(jnp.int32, (384, RB_W), 0)
    n = lax.broadcasted_iota(jnp.int32, (384, RB_W), 1) - shift
    idx = jnp.clip(1024 - n, -128, 128) + 128
    return (idx == r).astype(BF16)


def _relbias_expand(rb):
    def body(rb_ref, out_ref):
        oh = _rel_onehot_t()
        hi, mid, lo = _split3(rb_ref[...])
        strip = _dot(hi, oh) + _dot(mid, oh) + _dot(lo, oh)
        qi = _group(lax.broadcasted_iota(jnp.int32, (AQ_BLK, AK_WIN), 0), CH, 4)
        kj = _group(lax.broadcasted_iota(jnp.int32, (AQ_BLK, AK_WIN), 1), CH, 12)
        valid = (kj >= qi) & (kj <= qi + 8)
        for hd in range(6):
            x = jnp.broadcast_to(strip[hd:hd + 1, :], (AQ_BLK, RB_W))
            xr = pltpu.roll(x, 0, 1, stride=1, stride_axis=0)
            out_ref[hd] = jnp.where(valid, xr[:, 512:512 + AK_WIN], NEG)

    return pl.pallas_call(
        body, name="relbias_expand",
        out_shape=jax.ShapeDtypeStruct((6, AQ_BLK, AK_WIN), F32),
        compiler_params=_cp(),
    )(rb)


def _relbias_grad(dbias):
    def body(db_ref, out_ref):
        oh = _rel_onehot_t(AQ_BLK - 1)
        ri = lax.broadcasted_iota(jnp.int32, (AQ_BLK, AQ_BLK), 0)
        ci = lax.broadcasted_iota(jnp.int32, (AQ_BLK, AQ_BLK), 1)
        flip = (ri + ci == AQ_BLK - 1).astype(BF16)
        rows = []
        for hd in range(6):
            hi, mid, lo = _split3(db_ref[hd])
            rev = _dot(flip, hi) + _dot(flip, mid) + _dot(flip, lo)
            x = jnp.concatenate([jnp.zeros((AQ_BLK, 512), F32), rev,
                                 jnp.zeros((AQ_BLK, RB_W - 512 - AK_WIN), F32)], axis=1)
            xr = pltpu.roll(x, 0, 1, stride=1, stride_axis=0)
            rows.append(jnp.sum(xr, axis=0, keepdims=True))
        rows.append(jnp.zeros((2, RB_W), F32))
        dstrip = jnp.concatenate(rows, axis=0)
        hi, mid, lo = _split3(dstrip)
        out_ref[...] = _dot_nt(hi, oh) + _dot_nt(mid, oh) + _dot_nt(lo, oh)

    return pl.pallas_call(
        body, name="relbias_grad",
        out_shape=jax.ShapeDtypeStruct((8, 384), F32),
        compiler_params=_cp(),
    )(dbias)


ATT_SLAB = 8


def _att_softmax_slab(s_scr, b_ref, hd, rows, first_key):
    kvalid = lax.broadcasted_iota(jnp.int32, (ATT_SLAB, AK_WIN), 1) >= first_key
    s = jnp.where(kvalid, s_scr[rows, :] + b_ref[hd, rows, :], NEG)
    m = jnp.max(s, axis=-1, keepdims=True)
    p = jnp.exp(s - m)
    return p * (1.0 / jnp.sum(p, axis=-1, keepdims=True))


def _att_first_key(i):
    return (8 - 4 * i) * CH


def _slab_rows(t):
    return pl.ds(t * ATT_SLAB, ATT_SLAB)


def _head_lanes(hd):
    return slice(hd * 64, (hd + 1) * 64)


WIN_BLKS = AK_WIN // AQ_BLK


def _head_tile(hd):
    return slice(hd * HEAD_PAD, (hd + 1) * HEAD_PAD)


def _win_cols(d):
    return slice(d * AQ_BLK, (d + 1) * AQ_BLK)


def _win_specs():
    return [pl.BlockSpec((AQ_BLK, ATT_WIDE), lambda i, d=d: (i + d, 0)) for d in range(WIN_BLKS)]


def _att_fwd(q, kpad, vpad, bias, jobs=()):
    T = q.shape[0]
    nb = T // AQ_BLK

    def body(q_ref, k0, k1, k2, v0, v1, v2, b_ref, o_ref, s_scr):
        k_refs, v_refs = (k0, k1, k2), (v0, v1, v2)
        first_key = _att_first_key(pl.program_id(0))

        def scores(hd):
            q_h = q_ref[:, _head_tile(hd)] * A_SCALE
            for d in range(WIN_BLKS):
                s_scr[hd % 2, :, _win_cols(d)] = _dot_nt(q_h, k_refs[d][:, _head_tile(hd)])

        scores(0)
        for hd in range(ATT_HEADS):
            if hd + 1 < ATT_HEADS:
                scores(hd + 1)
            s_h = s_scr.at[hd % 2]
            for t in range(AQ_BLK // ATT_SLAB):
                rows = _slab_rows(t)
                s_h[rows, :] = _att_softmax_slab(s_h, b_ref, hd, rows, first_key)
            o_h = _dot(s_h[:, _win_cols(0)].astype(BF16), v_refs[0][:, _head_tile(hd)])
            for d in range(1, WIN_BLKS):
                o_h = o_h + _dot(s_h[:, _win_cols(d)].astype(BF16), v_refs[d][:, _head_tile(hd)])
            o_ref[:, _head_lanes(hd)] = o_h[:, :64].astype(BF16)

    return _pcall(
        body, name="att_fwd", grid=(nb,),
        in_specs=[_row_spec(AQ_BLK, ATT_WIDE)] + _win_specs() + _win_specs() + [_full_spec((6, AQ_BLK, AK_WIN))],
        out_specs=[_row_spec(AQ_BLK, 384)],
        out_shape=[jax.ShapeDtypeStruct((T, 384), BF16)],
        scratch_shapes=[pltpu.VMEM((2, AQ_BLK, AK_WIN), F32)],
        sem=("arbitrary",), operands=(q, kpad, kpad, kpad, vpad, vpad, vpad, bias), jobs=jobs)


def _att_bwd(q, kpad, vpad, bias, do, jobs=()):
    T = q.shape[0]
    nb = T // AQ_BLK

    def body(q_ref, k0, k1, k2, v0, v1, v2, b_ref, do_ref, dq_ref, dk_ref, dv_ref, db_ref, dk_acc, dv_acc,
             s_scr, dp_scr):
        k_refs, v_refs = (k0, k1, k2), (v0, v1, v2)
        i = pl.program_id(0)

        @pl.when(i == 0)
        def _():
            dk_acc[...] = jnp.zeros_like(dk_acc)
            dv_acc[...] = jnp.zeros_like(dv_acc)
            db_ref[...] = jnp.zeros_like(db_ref)

        start = pl.multiple_of(i * AQ_BLK, AQ_BLK)
        first_key = _att_first_key(i)

        def scores(hd):
            q_h = q_ref[:, _head_tile(hd)] * A_SCALE
            do_h = do_ref[:, _head_tile(hd)]
            for d in range(WIN_BLKS):
                s_scr[hd % 2, :, _win_cols(d)] = _dot_nt(q_h, k_refs[d][:, _head_tile(hd)])
                dp_scr[hd % 2, :, _win_cols(d)] = _dot_nt(do_h, v_refs[d][:, _head_tile(hd)])

        scores(0)
        for hd in range(ATT_HEADS):
            if hd + 1 < ATT_HEADS:
                scores(hd + 1)
            s_h, dp_h = s_scr.at[hd % 2], dp_scr.at[hd % 2]
            for t in range(AQ_BLK // ATT_SLAB):
                rows = _slab_rows(t)
                p = _att_softmax_slab(s_h, b_ref, hd, rows, first_key)
                dp = dp_h[rows, :]
                ds = p * (dp - jnp.sum(p * dp, axis=-1, keepdims=True))
                db_ref[hd, rows, :] += ds
                s_h[rows, :] = p
                dp_h[rows, :] = ds
            q_h = q_ref[:, _head_tile(hd)] * A_SCALE
            do_h = do_ref[:, _head_tile(hd)]
            ls = _head_lanes(hd)
            dq_h = jnp.zeros((AQ_BLK, HEAD_PAD), F32)
            for d in range(WIN_BLKS):
                pb = s_h[:, _win_cols(d)].astype(BF16)
                dsb = dp_h[:, _win_cols(d)].astype(BF16)
                rows = pl.ds(start + d * AQ_BLK, AQ_BLK)
                dv_acc[rows, ls] += _dot_tn(pb, do_h)[:, :64]
                dk_acc[rows, ls] += _dot_tn(dsb, q_h)[:, :64]
                dq_h = dq_h + _dot(dsb, k_refs[d][:, _head_tile(hd)])
            dq_ref[:, ls] = (dq_h[:, :64] * A_SCALE).astype(BF16)

        @pl.when(i == nb - 1)
        def _():
            dk_ref[...] = dk_acc[AK_PAD:, :].astype(BF16)
            dv_ref[...] = dv_acc[AK_PAD:, :].astype(BF16)

    return _pcall(
        body, name="att_bwd", grid=(nb,),
        in_specs=[_row_spec(AQ_BLK, ATT_WIDE)] + _win_specs() + _win_specs()
        + [_full_spec((6, AQ_BLK, AK_WIN)), _row_spec(AQ_BLK, ATT_WIDE)],
        out_specs=[_row_spec(AQ_BLK, 384), _full_spec((T, 384)), _full_spec((T, 384)),
                   _full_spec((6, AQ_BLK, AK_WIN))],
        out_shape=[jax.ShapeDtypeStruct((T, 384), BF16), jax.ShapeDtypeStruct((T, 384), BF16),
                   jax.ShapeDtypeStruct((T, 384), BF16), jax.ShapeDtypeStruct((6, AQ_BLK, AK_WIN), F32)],
        scratch_shapes=[pltpu.VMEM((T + AK_PAD, 384), F32), pltpu.VMEM((T + AK_PAD, 384), F32),
                        pltpu.VMEM((2, AQ_BLK, AK_WIN), F32), pltpu.VMEM((2, AQ_BLK, AK_WIN), F32)],
        sem=("arbitrary",), operands=(q, kpad, kpad, kpad, vpad, vpad, vpad, bias, do), jobs=jobs)


FF_BLK = 512
N_FF = 4096 // FF_BLK


def _outproj_mlp_fwd(h, o_gla, o_conv, o_att, w_out, gamma, w_up, w_down, jobs=()):
    T = h.shape[0]
    tm = 512

    def body(h_ref, og_ref, oc_ref, oa_ref, wo_ref, g_ref, wu_ref, wd_ref, h1_ref, xn_ref, h2_ref, acc):
        j = pl.program_id(1)

        @pl.when(j == 0)
        def _():
            wo = wo_ref[...]
            h1 = (h_ref[...] + _dot(og_ref[...], wo[0:384]) + _dot(oc_ref[...], wo[384:640])
                  + _dot(oa_ref[...], wo[640:1024]))
            h1_ref[...] = h1
            r = lax.rsqrt(jnp.mean(h1 * h1, axis=-1, keepdims=True) + EPS)
            xn_ref[...] = (h1 * r * g_ref[...]).astype(BF16)
            acc[...] = h1

        a = jnp.maximum(_dot(xn_ref[...], wu_ref[0]), 0.0)
        acc[...] += _dot((a * a).astype(BF16), wd_ref[0])

        @pl.when(j == N_FF - 1)
        def _():
            h2_ref[...] = acc[...]

    row = lambda n: pl.BlockSpec((tm, n), lambda i, j: (i, 0))
    return _pcall(
        body, name="outproj_mlp_fwd", grid=(T // tm, N_FF),
        in_specs=[row(D), row(384), row(256), row(384),
                  pl.BlockSpec((D, D), lambda i, j: (0, 0)), pl.BlockSpec((1, D), lambda i, j: (0, 0)),
                  pl.BlockSpec((1, D, FF_BLK), lambda i, j: (j, 0, 0)),
                  pl.BlockSpec((1, FF_BLK, D), lambda i, j: (j, 0, 0))],
        out_specs=[row(D), row(D), row(D)],
        out_shape=[jax.ShapeDtypeStruct((T, D), F32), jax.ShapeDtypeStruct((T, D), BF16),
                   jax.ShapeDtypeStruct((T, D), F32)],
        scratch_shapes=[pltpu.VMEM((tm, D), F32)],
        sem=("arbitrary", "arbitrary"), operands=(h, o_gla, o_conv, o_att, w_out, gamma, w_up, w_down), jobs=jobs)


def _mlp_bwd(xn2, h1, dh2, gamma, w_up, w_down, jobs=()):
    T = xn2.shape[0]
    tm = 512
    nt = T // tm
    last = N_FF - 1

    def body(xn_ref, h1_ref, dy_ref, g_ref, wu_ref, wd_ref, dh1_ref, dwu_ref, dwd_ref, dg_ref,
             dxn_acc, acc_u, acc_d):
        j = pl.program_id(0)
        i = pl.program_id(1)
        x = xn_ref[...]
        dy = dy_ref[...]
        dyb = dy.astype(BF16)
        wu = wu_ref[0]
        wd = wd_ref[0]
        a = jnp.maximum(_dot(x, wu), 0.0)
        hh = (a * a).astype(BF16)
        du = (_dot_nt(dyb, wd) * (2.0 * a)).astype(BF16)
        cu_ = _dot_tn(x, du)
        cd_ = _dot_tn(hh, dyb)

        @pl.when(i == 0)
        def _():
            acc_u[...] = cu_
            acc_d[...] = cd_

        @pl.when(i > 0)
        def _():
            acc_u[...] += cu_
            acc_d[...] += cd_

        @pl.when(i == nt - 1)
        def _():
            dwu_ref[0, 0] = acc_u[...].astype(BF16)
            dwd_ref[0, 0] = acc_d[...].astype(BF16)

        rows = pl.ds(pl.multiple_of(i * tm, tm), tm)
        dxn = _dot_nt(du, wu)

        @pl.when(j == 0)
        def _():
            dxn_acc[rows, :] = dxn

        @pl.when(j > 0)
        def _():
            dxn_acc[rows, :] += dxn

        @pl.when(j == last)
        def _():
            @pl.when(i == 0)
            def _():
                dg_ref[...] = jnp.zeros_like(dg_ref)

            h1 = h1_ref[...]
            r = lax.rsqrt(jnp.mean(h1 * h1, axis=-1, keepdims=True) + EPS)
            dx, dgam = _rms_bwd(dxn_acc[rows, :], h1, r, g_ref[...])
            dh1_ref[...] = dy + dx
            dg_ref[...] += dgam

    late = lambda j, i: (jnp.where(j == last, i, 0), 0)
    return _pcall(
        body, name="mlp_bwd", grid=(N_FF, nt),
        in_specs=[pl.BlockSpec((tm, D), lambda j, i: (i, 0)), pl.BlockSpec((tm, D), late),
                  pl.BlockSpec((tm, D), lambda j, i: (i, 0)), pl.BlockSpec((1, D), lambda j, i: (0, 0)),
                  pl.BlockSpec((1, D, FF_BLK), lambda j, i: (j, 0, 0)),
                  pl.BlockSpec((1, FF_BLK, D), lambda j, i: (j, 0, 0))],
        out_specs=[pl.BlockSpec((tm, D), late),
                   pl.BlockSpec((1, 1, D, FF_BLK), lambda j, i: (j % 2, j // 2, 0, 0)),
                   pl.BlockSpec((1, 1, FF_BLK, D), lambda j, i: (j % 2, j // 2, 0, 0)),
                   pl.BlockSpec((1, D), lambda j, i: (0, 0))],
        out_shape=[jax.ShapeDtypeStruct((T, D), F32), jax.ShapeDtypeStruct((2, 4, D, FF_BLK), BF16),
                   jax.ShapeDtypeStruct((2, 4, FF_BLK, D), BF16), jax.ShapeDtypeStruct((1, D), F32)],
        scratch_shapes=[pltpu.VMEM((T, D), F32), pltpu.VMEM((D, FF_BLK), F32), pltpu.VMEM((FF_BLK, D), F32)],
        sem=("arbitrary", "arbitrary"), operands=(xn2, h1, dh2, gamma, w_up, w_down), jobs=jobs)


def _outproj_bwd(dh1, o_gla, o_conv, o_att, w_out, jobs=()):
    T = dh1.shape[0]
    tm = 512
    nt = T // tm

    def body(dy_ref, og_ref, oc_ref, oa_ref, wo_ref, dg_ref, dc_ref, da_ref, dw_ref, acc):
        i = pl.program_id(0)
        dyb = dy_ref[...].astype(BF16)
        dm = _dot_nt(dyb, wo_ref[...])
        dg_ref[...] = dm[:, 0:384].astype(BF16)
        dc_ref[...] = dm[:, 384:640].astype(BF16)
        _store_head_padded(da_ref, dm[:, 640:1024].astype(BF16))
        mixed = jnp.concatenate([og_ref[...], oc_ref[...], oa_ref[...]], axis=1)
        contrib = _dot_tn(mixed, dyb)

        @pl.when(i == 0)
        def _():
            acc[...] = contrib

        @pl.when(i > 0)
        def _():
            acc[...] += contrib

        @pl.when(i == nt - 1)
        def _():
            for j in range(N_DEV):
                dw_ref[j % 2, j // 2] = acc[j * 128:(j + 1) * 128, :].astype(BF16)

    return _pcall(
        body, name="outproj_bwd", grid=(nt,),
        in_specs=[_row_spec(tm, D), _row_spec(tm, 384), _row_spec(tm, 256), _row_spec(tm, 384),
                  _full_spec((D, D))],
        out_specs=[_row_spec(tm, 384), _row_spec(tm, 256), _row_spec(tm, ATT_WIDE), _full_spec((2, 4, 128, D))],
        out_shape=[jax.ShapeDtypeStruct((T, 384), BF16), jax.ShapeDtypeStruct((T, 256), BF16),
                   jax.ShapeDtypeStruct((T, ATT_WIDE), BF16), jax.ShapeDtypeStruct((2, 4, 128, D), BF16)],
        scratch_shapes=[pltpu.VMEM((D, D), F32)],
        sem=("arbitrary",), operands=(dh1, o_gla, o_conv, o_att, w_out), jobs=jobs)


def _loss_fwd_bwd(h, gamma, target):
    T = h.shape[0]
    tm = 512

    def body(h_ref, g_ref, t_ref, loss_ref, dh_ref, dg_ref):
        @pl.when(pl.program_id(0) == 0)
        def _():
            loss_ref[...] = jnp.zeros_like(loss_ref)
            dg_ref[...] = jnp.zeros_like(dg_ref)

        x = h_ref[...]
        r = lax.rsqrt(jnp.mean(x * x, axis=-1, keepdims=True) + EPS)
        gamma_ = g_ref[...]
        e = x * r * gamma_ - t_ref[...]
        loss_ref[...] += 0.5 * jnp.sum(jnp.mean(e * e, axis=-1, keepdims=True), axis=0, keepdims=True)
        dx, dgam = _rms_bwd(e * (1.0 / D), x, r, gamma_)
        dh_ref[...] = dx
        dg_ref[...] += dgam

    return pl.pallas_call(
        body, name="loss_fwd_bwd", grid=(T // tm,),
        in_specs=[_row_spec(tm, D), _full_spec((1, D)), _row_spec(tm, D)],
        out_specs=[_full_spec((8, 128)), _row_spec(tm, D), _full_spec((1, D))],
        out_shape=[jax.ShapeDtypeStruct((8, 128), F32), jax.ShapeDtypeStruct((T, D), F32),
                   jax.ShapeDtypeStruct((1, D), F32)],
        compiler_params=_cp(("arbitrary",)),
    )(h, gamma, target)


def _adamw_math(w, g, m, v):
    m = ADAM_B1 * m + (1.0 - ADAM_B1) * g
    v = ADAM_B2 * v + (1.0 - ADAM_B2) * (g * g)
    m_hat = m / (1.0 - ADAM_B1 ** ADAM_STEP)
    v_hat = v / (1.0 - ADAM_B2 ** ADAM_STEP)
    delta = -ADAM_LR * (m_hat / (jnp.sqrt(v_hat) + ADAM_EPS) + ADAM_WD * w)
    return delta, m, v


def _rs_adamw(a_own, r2, w, m, v, layer, chip_idx, rows_blk, prev=None):
    _, R, C = w.shape
    nblk = R // rows_blk

    def body(chip_ref, a_ref, r_ref, w_ref, m_ref, v_ref, *rest):
        g_out, d_out, m_out, v_out = rest[-4:]
        g = (a_ref[0].astype(F32) + r_ref[0].astype(F32)) + (r_ref[1].astype(F32) + r_ref[2].astype(F32))
        delta, m_new, v_new = _adamw_math(w_ref[0], g, m_ref[0], v_ref[0])
        g_out[0] = g
        d_out[0] = delta
        m_out[0] = m_new
        v_out[0] = v_new

    blk = pl.BlockSpec((1, rows_blk, C), lambda i, chip: (layer, i, 0))
    n_prev = 0 if prev is None else 4
    grid_spec = pltpu.PrefetchScalarGridSpec(
        num_scalar_prefetch=1, grid=(nblk,),
        in_specs=[pl.BlockSpec((1, rows_blk, C), lambda i, chip: (chip[0], i, 0)),
                  pl.BlockSpec((3, rows_blk, C), lambda i, chip: (0, i, 0)), blk, blk, blk]
        + [_any_spec()] * n_prev,
        out_specs=[blk, blk, blk, blk])
    return pl.pallas_call(
        body, name="rs_adamw", grid_spec=grid_spec,
        out_shape=[jax.ShapeDtypeStruct((DEPTH, R, C), F32)] * 4,
        input_output_aliases={6 + t: t for t in range(n_prev)},
        compiler_params=_cp(("arbitrary",)),
    )(chip_idx, a_own, r2, w, m, v, *(prev or ()))


def _pair_sum(g, r1, core_idx, rows_blk):
    _, _, R, C = g.shape
    nblk = R // rows_blk

    def body(core_ref, g_ref, r_ref, o_ref):
        o_ref[...] = (g_ref[0].astype(F32) + r_ref[...].astype(F32)).astype(BF16)

    grid_spec = pltpu.PrefetchScalarGridSpec(
        num_scalar_prefetch=1, grid=(4, nblk),
        in_specs=[pl.BlockSpec((1, 1, rows_blk, C), lambda k, i, core: (core[0], k, i, 0)),
                  pl.BlockSpec((1, rows_blk, C), lambda k, i, core: (k, i, 0))],
        out_specs=pl.BlockSpec((1, rows_blk, C), lambda k, i, core: (k, i, 0)))
    return pl.pallas_call(
        body, name="rs_pair_sum", grid_spec=grid_spec,
        out_shape=jax.ShapeDtypeStruct((4, R, C), BF16),
        compiler_params=_cp(("arbitrary", "arbitrary")),
    )(core_idx, g, r1)


def _small_sum(gathered):
    def body(g_ref, o_ref):
        acc = g_ref[0]
        for d in range(1, N_DEV):
            acc = acc + g_ref[d]
        o_ref[...] = acc

    return pl.pallas_call(
        body, name="small_sum",
        out_shape=jax.ShapeDtypeStruct((SMALL_ROWS, 1024), F32),
        compiler_params=_cp(),
    )(gathered)


def _adamw_small(ws, gs, ms, vs):
    n = len(ws)

    def body(*refs):
        w_r, g_r, m_r, v_r = refs[0:n], refs[n:2 * n], refs[2 * n:3 * n], refs[3 * n:4 * n]
        d_o, m_o, v_o = refs[4 * n:5 * n], refs[5 * n:6 * n], refs[6 * n:7 * n]
        for t in range(n):
            delta, m_new, v_new = _adamw_math(w_r[t][...], g_r[t][...], m_r[t][...], v_r[t][...])
            d_o[t][...] = delta
            m_o[t][...] = m_new
            v_o[t][...] = v_new

    shapes = [jax.ShapeDtypeStruct(w.shape, F32) for w in ws]
    outs = pl.pallas_call(
        body, name="adamw_small", out_shape=shapes * 3, compiler_params=_cp(),
    )(*ws, *gs, *ms, *vs)
    return outs[0:n], outs[n:2 * n], outs[2 * n:3 * n]


def _mesh_pos():
    return lax.axis_index("x"), lax.axis_index("y"), lax.axis_index("c")


def _peers():
    x, y, c = _mesh_pos()
    return (x, y, c), (x, y, 1 - c), [(1 - x, y), (x, 1 - y), (1 - x, 1 - y)]


def _slot(ref, pos):
    return ref.at[4 * pos[0] + 2 * pos[1] + pos[2]]


def _remote(src, dst, send_sem, recv_sem, to):
    return pltpu.make_async_remote_copy(src_ref=src, dst_ref=dst, send_sem=send_sem, recv_sem=recv_sem,
                                        device_id=to, device_id_type=MESH)


def _ag_spread(shards):
    n = len(shards)

    def copies(ins, outs, sems):
        send, recv, loc = sems
        me, sibling, chips = _peers()
        peers = [sibling] + [(*chip, me[2]) for chip in chips]
        local = [pltpu.make_async_copy(ins[a], _slot(outs[a], me), loc.at[a]) for a in range(n)]
        sends = [_remote(ins[a], _slot(outs[a], me), send.at[a, k], recv.at[a, k], p)
                 for a in range(n) for k, p in enumerate(peers)]
        recvs = [_remote(ins[a], _slot(outs[a], p), send.at[a, k], recv.at[a, k], p)
                 for a in range(n) for k, p in enumerate(peers)]
        return local, sends, recvs

    def start(ins, outs, sems):
        local, sends, _ = copies(ins, outs, sems)
        for cp in local + sends:
            cp.start()

    def finish(ins, outs, sems):
        local, sends, recvs = copies(ins, outs, sems)
        for cp in sends:
            cp.wait_send()
        for cp in recvs:
            cp.wait_recv()
        for cp in local:
            cp.wait()

    return _Job(shards, [jax.ShapeDtypeStruct((N_DEV,) + a.shape, a.dtype) for a in shards],
                [pltpu.SemaphoreType.DMA((n, 4)), pltpu.SemaphoreType.DMA((n, 4)), pltpu.SemaphoreType.DMA((n,))],
                start, finish)


def _ag_pass(stacks):
    n = len(stacks)

    def copies(ins, outs, sems):
        send, recv = sems
        me, sibling, chips = _peers()
        sends = [_remote(_slot(ins[a], (*chip, me[2])), _slot(outs[a], (*chip, me[2])), send.at[a, j], recv.at[a, j],
                         sibling) for a in range(n) for j, chip in enumerate(chips)]
        recvs = [_remote(_slot(ins[a], (*chip, me[2])), _slot(outs[a], (*chip, 1 - me[2])), send.at[a, j],
                         recv.at[a, j], sibling) for a in range(n) for j, chip in enumerate(chips)]
        return sends, recvs

    def start(ins, outs, sems):
        for cp in copies(ins, outs, sems)[0]:
            cp.start()

    def finish(ins, outs, sems):
        sends, recvs = copies(ins, outs, sems)
        for cp in sends:
            cp.wait_send()
        for cp in recvs:
            cp.wait_recv()

    return _Job(stacks, [jax.ShapeDtypeStruct(a.shape, a.dtype) for a in stacks],
                [pltpu.SemaphoreType.DMA((n, 3)), pltpu.SemaphoreType.DMA((n, 3))],
                start, finish, aliases={a: a for a in range(n)})


def _rs_swap(parts):
    n = len(parts)

    def copies(ins, outs, sems):
        send, recv = sems
        me, sibling, _ = _peers()
        return [_remote(ins[a].at[1 - me[2]], outs[a], send.at[a], recv.at[a], sibling) for a in range(n)]

    def start(ins, outs, sems):
        for cp in copies(ins, outs, sems):
            cp.start()

    def finish(ins, outs, sems):
        for cp in copies(ins, outs, sems):
            cp.wait()

    return _Job(parts, [jax.ShapeDtypeStruct(a.shape[1:], a.dtype) for a in parts],
                [pltpu.SemaphoreType.DMA((n,)), pltpu.SemaphoreType.DMA((n,))], start, finish)


def _rs_ici(pairs):
    n = len(pairs)

    def copies(ins, outs, sems):
        send, recv = sems
        me, _, chips = _peers()
        return [_remote(ins[a].at[2 * chip[0] + chip[1]], outs[a].at[j], send.at[a, j], recv.at[a, j],
                        (*chip, me[2])) for a in range(n) for j, chip in enumerate(chips)]

    def start(ins, outs, sems):
        for cp in copies(ins, outs, sems):
            cp.start()

    def finish(ins, outs, sems):
        for cp in copies(ins, outs, sems):
            cp.wait()

    return _Job(pairs, [jax.ShapeDtypeStruct((3,) + a.shape[1:], a.dtype) for a in pairs],
                [pltpu.SemaphoreType.DMA((n, 3)), pltpu.SemaphoreType.DMA((n, 3))], start, finish)


def _comm_call(jobs, name):
    def body():
        pass

    return _pcall(body, name=name, grid=(), in_specs=[], out_specs=[], out_shape=[], operands=(), jobs=jobs)[1]


def _allgather(arrs, name):
    n = len(arrs)

    def body(*refs):
        ins, outs = refs[:n], refs[n:2 * n]
        send_sems, recv_sems, local_sems = refs[2 * n:]
        x, y, c = _mesh_pos()
        me, sibling = (x, y, c), (x, y, 1 - c)
        chips = [(1 - x, y), (x, 1 - y), (1 - x, 1 - y)]

        def slot(a, pos):
            return outs[a].at[4 * pos[0] + 2 * pos[1] + pos[2]]

        def copy(a, k, block, to, src=None):
            return pltpu.make_async_remote_copy(
                src_ref=slot(a, block) if src is None else src, dst_ref=slot(a, block),
                send_sem=send_sems.at[a, k], recv_sem=recv_sems.at[a, k],
                device_id=to, device_id_type=MESH)

        mine = [pltpu.make_async_copy(ins[a], slot(a, me), local_sems.at[a]) for a in range(n)]
        for cp in mine:
            cp.start()
        first = []
        for a in range(n):
            first.append(copy(a, 0, me, sibling, src=ins[a]))
            first += [copy(a, 1 + j, me, (*chip, c), src=ins[a]) for j, chip in enumerate(chips)]
        for cp in first:
            cp.start()
        passed = []
        for j, chip in enumerate(chips):
            for a in range(n):
                copy(a, 1 + j, (*chip, c), me).wait_recv()
                fwd = copy(a, 4 + j, (*chip, c), sibling)
                fwd.start()
                passed.append(fwd)
        for a in range(n):
            copy(a, 0, sibling, me).wait_recv()
            for j, chip in enumerate(chips):
                copy(a, 4 + j, (*chip, 1 - c), me).wait_recv()
        for cp in first + passed:
            cp.wait_send()
        for cp in mine:
            cp.wait()

    return pl.pallas_call(
        body, name=name,
        in_specs=[_any_spec()] * n, out_specs=[_any_spec()] * n,
        out_shape=[jax.ShapeDtypeStruct((N_DEV,) + a.shape, a.dtype) for a in arrs],
        scratch_shapes=[pltpu.SemaphoreType.DMA((n, 7)), pltpu.SemaphoreType.DMA((n, 7)),
                        pltpu.SemaphoreType.DMA((n,))],
        compiler_params=_cp(),
    )(*arrs)


W_IN_SHARD = 354
W_IN_COLS = ((0, 192, OQ), (192, 192, OKK), (384, 384, OV), (768, 384, OG), (1152, 16, OLR), (1168, 512, OCU),
             (1680, 384, OAQ), (2064, 384, OAK), (2448, 384, OAV))


def _w_in_padded(stack):
    new_to_ref = {new: (start, width) for start, width, new in W_IN_COLS}
    cols = []
    for new, padded in IN_GROUPS:
        start, width = new_to_ref[new]
        a = start
        while a < start + width:
            j = a // W_IN_SHARD
            b = min(start + width, (j + 1) * W_IN_SHARD)
            cols.append(stack[j][:, a - j * W_IN_SHARD:b - j * W_IN_SHARD])
            a = b
        if padded > width:
            cols.append(jnp.zeros((stack.shape[1], padded - width), stack.dtype))
    return jnp.concatenate(cols, axis=1)


def _dw_in_shards(dw):
    shards = []
    for j in range(N_DEV):
        lo, hi = j * W_IN_SHARD, (j + 1) * W_IN_SHARD
        segs = []
        for start, width, new in W_IN_COLS:
            a, b = max(lo, start), min(hi, start + width)
            if a < b:
                segs.append(dw[:, new + a - start:new + b - start])
        shards.append(jnp.concatenate(segs, axis=1))
    return jnp.stack([jnp.stack([shards[2 * chip + core] for chip in range(4)]) for core in range(2)])


def _pad_to(a, shape):
    return jnp.pad(a, [(0, s - d) for d, s in zip(a.shape, shape)])


SMALL_LAYOUT = (
    ("norm_mix", 2, 1024), ("norm_ffn", 2, 1024), ("norm_final", 1, 1024), ("gla_norm", 2, 384),
    ("b_gla_gate", 2, 192), ("b_dw", 2, 256), ("conv_ln_g", 2, 256), ("conv_ln_b", 2, 256),
    ("rel_bias", 12, 257), ("w_gla_gate", 32, 192), ("w_dw", 62, 256),
)
SUBLANES = 8


def _tile_rows(r):
    return -(-r // SUBLANES) * SUBLANES


SMALL_ROWS = sum(_tile_rows(r) for _, r, _ in SMALL_LAYOUT)


def _pack_small(parts):
    return jnp.concatenate([_pad_to(parts[name], (_tile_rows(r), 1024)) for name, r, _ in SMALL_LAYOUT], axis=0)


def _unpack_small(packed):
    out, r0 = {}, 0
    for name, r, lanes in SMALL_LAYOUT:
        out[name] = packed[r0:r0 + r, 0:lanes]
        r0 += _tile_rows(r)
    return out


def _mixers_fwd(h, wl, w_in_p, att_jobs=(), gla_jobs_fn=None):
    q, k, v, g, cu, aq, ak, av, lr = _inproj_fwd(h, wl["norm_mix"], w_in_p)
    bias = _relbias_expand(wl["rb"])
    kpad = jnp.pad(ak, ((AK_PAD, 0), (0, 0)))
    vpad = jnp.pad(av, ((AK_PAD, 0), (0, 0)))
    (o_att,), att_res = _att_fwd(aq, kpad, vpad, bias, jobs=att_jobs)
    gla_jobs = gla_jobs_fn(att_res) if gla_jobs_fn is not None else ()
    (o_gla, states), gla_res = _gla_fwd(q, k, v, g, lr, wl["wg"], wl["bg"], wl["gn"], jobs=gla_jobs)
    o_conv = _conv_fwd(cu, wl["w_dw"], wl["b_dw"], wl["ln_g"], wl["ln_b"])
    sv = dict(h=h, w_in=w_in_p, q=q, k=k, v=v, g=g, cu=cu, aq=aq, kpad=kpad, vpad=vpad, lr=lr,
              o_gla=o_gla, o_conv=o_conv, o_att=o_att, states=states, bias=bias)
    return sv, att_res, gla_res


def _mixers_bwd(sv, wl, dh1, d_ogla, d_oconv, att_grads):
    daq, dak, dav, dbias = att_grads
    d_rb = _relbias_grad(dbias)
    dcu, dw_dw, db_dw, dln_g, dln_b = _conv_bwd(sv["cu"], d_oconv, wl["w_dw"], wl["b_dw"], wl["ln_g"], wl["ln_b"])
    dq, dk, dv, dg, dlr, dwg, dbg, dgn = _gla_bwd(sv["q"], sv["k"], sv["v"], sv["g"], sv["lr"], sv["states"],
                                                  d_ogla, wl["wg"], wl["bg"], wl["gn"])
    dh, dw_in, d_nmix = _inproj_bwd(sv["h"], dh1, wl["norm_mix"], sv["w_in"],
                                    (dq, dk, dv, dg, dcu, daq, dak, dav, dlr))
    small = dict(norm_mix=d_nmix, wg=dwg, bg=dbg, gn=dgn, w_dw=dw_dw, b_dw=db_dw, ln_g=dln_g, ln_b=dln_b, rb=d_rb)
    return dh, dw_in, small


def _layer_small(l, w_dw_full, norm_mix, w_gla_gate, b_gla_gate, gla_norm, b_dw, conv_ln_g, conv_ln_b, rel_bias,
                 norm_ffn):
    return dict(
        norm_mix=norm_mix[l][None, :], norm_ffn=norm_ffn[l][None, :],
        wg=_pad_to(w_gla_gate[l], (128, 256)).astype(BF16), bg=_pad_to(b_gla_gate[l][None, :], (1, 256)),
        gn=gla_norm[l][None, :], w_dw=_pad_to(w_dw_full, (32, 256)), b_dw=b_dw[l][None, :],
        ln_g=conv_ln_g[l][None, :], ln_b=conv_ln_b[l][None, :], rb=_pad_to(rel_bias[l], (8, 384)))


RS_ROWS = dict(w_in=512, w_out=128, w_up=512, w_down=256)


def kernel(x, norm_mix, w_in, w_gla_gate, b_gla_gate, gla_norm, w_dw, b_dw, conv_ln_g, conv_ln_b, rel_bias, w_out, norm_ffn, w_up, w_down, norm_final, loss_target, m_norm_mix, m_w_in, m_w_gla_gate, m_b_gla_gate, m_gla_norm, m_w_dw, m_b_dw, m_conv_ln_g, m_conv_ln_b, m_rel_bias, m_w_out, m_norm_ffn, m_w_up, m_w_down, m_norm_final, v_norm_mix, v_w_in, v_w_gla_gate, v_b_gla_gate, v_gla_norm, v_w_dw, v_b_dw, v_conv_ln_g, v_conv_ln_b, v_rel_bias, v_w_out, v_norm_ffn, v_w_up, v_w_down, v_norm_final):
    mx, my, mc = _mesh_pos()
    me = 4 * mx + 2 * my + mc
    chip_idx = (2 * mx + my).astype(jnp.int32).reshape(1)
    core_idx = mc.astype(jnp.int32).reshape(1)
    x0, target = x[0], loss_target[0]

    def pair_sums(parts, r1):
        return [_pair_sum(p, r, core_idx, p.shape[2]) for p, r in zip(parts, r1)]

    sh = [dict(w_in=w_in[l].astype(BF16), w_out=w_out[l].astype(BF16), w_up=w_up[l].astype(BF16),
               w_down=w_down[l].astype(BF16)) for l in range(DEPTH)]
    dw_flat = _pad_to(w_dw, (DEPTH, 32, 32)).reshape(16, 128)
    st_in0, st_dw = _allgather([sh[0]["w_in"], dw_flat], "allgather_first")
    dw_all = st_dw.reshape(N_DEV, DEPTH, 32, 32)[:, :, :KCONV, :]
    dw_all = jnp.transpose(dw_all, (1, 2, 0, 3)).reshape(DEPTH, KCONV, 256)
    wl = [_layer_small(l, dw_all[l], norm_mix, w_gla_gate, b_gla_gate, gla_norm, b_dw, conv_ln_g, conv_ln_b,
                       rel_bias, norm_ffn) for l in range(DEPTH)]

    sv0, _, (st_out0, st_up0, st_down0) = _mixers_fwd(
        x0, wl[0], _w_in_padded(st_in0),
        att_jobs=[_ag_spread([sh[0]["w_out"], sh[0]["w_up"], sh[0]["w_down"]])],
        gla_jobs_fn=lambda spread: [_ag_pass(spread)])
    wo0 = st_out0.reshape(D, D)
    (h1_0, xn2_0, h2_0), st1 = _outproj_mlp_fwd(
        x0, sv0["o_gla"], sv0["o_conv"], sv0["o_att"], wo0, wl[0]["norm_ffn"], st_up0, st_down0,
        jobs=[_ag_spread([sh[1]["w_in"], sh[1]["w_out"], sh[1]["w_up"], sh[1]["w_down"]])])

    (st_in1,) = _comm_call([_ag_pass([st1[0]])], "allgather_pass_w_in")
    sv1, (st_out1, st_up1, st_down1), _ = _mixers_fwd(
        h2_0, wl[1], _w_in_padded(st_in1), att_jobs=[_ag_pass(st1[1:])])
    wo1 = st_out1.reshape(D, D)
    (h1_1, xn2_1, h2_1), _ = _outproj_mlp_fwd(
        h2_0, sv1["o_gla"], sv1["o_conv"], sv1["o_att"], wo1, wl[1]["norm_ffn"], st_up1, st_down1)

    loss8, dh, d_nf = _loss_fwd_bwd(h2_1, norm_final[None, :], target)
    loss = lax.psum(loss8[0, 0], ("x", "y", "c"))

    (dh1, dw_up1, dw_down1, d_nffn1), _ = _mlp_bwd(xn2_1, h1_1, dh, wl[1]["norm_ffn"], st_up1, st_down1)
    ud1 = [dw_up1, dw_down1]
    (d_ogla, d_oconv, d_oatt, dw_out1), r1 = _outproj_bwd(
        dh1, sv1["o_gla"], sv1["o_conv"], sv1["o_att"], wo1, jobs=[_rs_swap(ud1)])
    pair_ud1 = pair_sums(ud1, r1)
    att_grads, r2_ud1 = _att_bwd(sv1["aq"], sv1["kpad"], sv1["vpad"], sv1["bias"], d_oatt, jobs=[_rs_ici(pair_ud1)])
    dh, dw_in1, small1 = _mixers_bwd(sv1, wl[1], dh1, d_ogla, d_oconv, att_grads)
    small1["norm_ffn"] = d_nffn1

    io1 = [_dw_in_shards(dw_in1), dw_out1]
    (dh1, dw_up0, dw_down0, d_nffn0), r1 = _mlp_bwd(xn2_0, h1_0, dh, wl[0]["norm_ffn"], st_up0, st_down0,
                                                     jobs=[_rs_swap(io1)])
    pair_io1 = pair_sums(io1, r1)
    ud0 = [dw_up0, dw_down0]
    (d_ogla, d_oconv, d_oatt, dw_out0), r1 = _outproj_bwd(
        dh1, sv0["o_gla"], sv0["o_conv"], sv0["o_att"], wo0, jobs=[_rs_swap(ud0)])
    pair_ud0 = pair_sums(ud0, r1)
    att_grads, r2 = _att_bwd(sv0["aq"], sv0["kpad"], sv0["vpad"], sv0["bias"], d_oatt,
                             jobs=[_rs_ici(pair_io1), _rs_ici(pair_ud0)])
    r2_io1, r2_ud0 = r2[:2], r2[2:]
    dx, dw_in0, small0 = _mixers_bwd(sv0, wl[0], dh1, d_ogla, d_oconv, att_grads)
    small0["norm_ffn"] = d_nffn0

    io0 = [_dw_in_shards(dw_in0), dw_out0]
    r1 = _comm_call([_rs_swap(io0)], "rs_swap_last")
    pair_io0 = pair_sums(io0, r1)
    r2_io0 = _comm_call([_rs_ici(pair_io0)], "rs_ici_last")

    big_w = dict(w_in=(w_in, m_w_in, v_w_in), w_out=(w_out, m_w_out, v_w_out), w_up=(w_up, m_w_up, v_w_up),
                 w_down=(w_down, m_w_down, v_w_down))
    pairs = {1: dict(w_in=(pair_io1[0], r2_io1[0]), w_out=(pair_io1[1], r2_io1[1]),
                     w_up=(pair_ud1[0], r2_ud1[0]), w_down=(pair_ud1[1], r2_ud1[1])),
             0: dict(w_in=(pair_io0[0], r2_io0[0]), w_out=(pair_io0[1], r2_io0[1]),
                     w_up=(pair_ud0[0], r2_ud0[0]), w_down=(pair_ud0[1], r2_ud0[1]))}
    big_out = {}
    for name, (w_, m_, v_) in big_w.items():
        res = None
        for l in (1, 0):
            a_own, r2_ = pairs[l][name]
            res = _rs_adamw(a_own, r2_, w_, m_, v_, l, chip_idx, RS_ROWS[name], prev=res)
        big_out[name] = res

    grads = (small0, small1)
    parts = dict(
        norm_mix=jnp.concatenate([grads[l]["norm_mix"] for l in range(DEPTH)], axis=0),
        norm_ffn=jnp.concatenate([grads[l]["norm_ffn"] for l in range(DEPTH)], axis=0),
        norm_final=d_nf,
        gla_norm=jnp.concatenate([grads[l]["gn"] for l in range(DEPTH)], axis=0),
        b_gla_gate=jnp.concatenate([grads[l]["bg"][:, :192] for l in range(DEPTH)], axis=0),
        b_dw=jnp.concatenate([grads[l]["b_dw"] for l in range(DEPTH)], axis=0),
        conv_ln_g=jnp.concatenate([grads[l]["ln_g"] for l in range(DEPTH)], axis=0),
        conv_ln_b=jnp.concatenate([grads[l]["ln_b"] for l in range(DEPTH)], axis=0),
        rel_bias=jnp.concatenate([grads[l]["rb"][:6, :N_REL] for l in range(DEPTH)], axis=0),
        w_gla_gate=jnp.concatenate([grads[l]["wg"][:16, :192] for l in range(DEPTH)], axis=0),
        w_dw=jnp.concatenate([grads[l]["w_dw"][:KCONV] for l in range(DEPTH)], axis=0),
    )
    small_all = _allgather([_pack_small(parts)], "allgather_small")[0]
    sg = _unpack_small(_small_sum(small_all))
    dw_grad = lax.dynamic_slice_in_dim(sg["w_dw"].reshape(DEPTH, KCONV, 256), me * 32, 32, axis=2)
    small_g = dict(
        norm_mix=sg["norm_mix"], w_gla_gate=sg["w_gla_gate"].reshape(DEPTH, 16, 192), b_gla_gate=sg["b_gla_gate"],
        gla_norm=sg["gla_norm"], w_dw=dw_grad, b_dw=sg["b_dw"], conv_ln_g=sg["conv_ln_g"],
        conv_ln_b=sg["conv_ln_b"], rel_bias=sg["rel_bias"].reshape(DEPTH, 6, N_REL), norm_ffn=sg["norm_ffn"],
        norm_final=sg["norm_final"].reshape(D))
    small_names = ("norm_mix", "w_gla_gate", "b_gla_gate", "gla_norm", "w_dw", "b_dw", "conv_ln_g", "conv_ln_b",
                   "rel_bias", "norm_ffn", "norm_final")
    small_w = dict(norm_mix=norm_mix, w_gla_gate=w_gla_gate, b_gla_gate=b_gla_gate, gla_norm=gla_norm, w_dw=w_dw,
                   b_dw=b_dw, conv_ln_g=conv_ln_g, conv_ln_b=conv_ln_b, rel_bias=rel_bias, norm_ffn=norm_ffn,
                   norm_final=norm_final)
    small_m = dict(norm_mix=m_norm_mix, w_gla_gate=m_w_gla_gate, b_gla_gate=m_b_gla_gate, gla_norm=m_gla_norm,
                   w_dw=m_w_dw, b_dw=m_b_dw, conv_ln_g=m_conv_ln_g, conv_ln_b=m_conv_ln_b, rel_bias=m_rel_bias,
                   norm_ffn=m_norm_ffn, norm_final=m_norm_final)
    small_v = dict(norm_mix=v_norm_mix, w_gla_gate=v_w_gla_gate, b_gla_gate=v_b_gla_gate, gla_norm=v_gla_norm,
                   w_dw=v_w_dw, b_dw=v_b_dw, conv_ln_g=v_conv_ln_g, conv_ln_b=v_conv_ln_b, rel_bias=v_rel_bias,
                   norm_ffn=v_norm_ffn, norm_final=v_norm_final)
    s_delta, s_m, s_v = _adamw_small([small_w[n] for n in small_names], [small_g[n] for n in small_names],
                                     [small_m[n] for n in small_names], [small_v[n] for n in small_names])
    s_idx = {n: t for t, n in enumerate(small_names)}

    order = ("norm_mix", "w_in", "w_gla_gate", "b_gla_gate", "gla_norm", "w_dw", "b_dw", "conv_ln_g", "conv_ln_b",
             "rel_bias", "w_out", "norm_ffn", "w_up", "w_down", "norm_final")

    def pick(kind, name):
        if name in big_out:
            return big_out[name][kind]
        t = s_idx[name]
        return (small_g[name], s_delta[t], s_m[t], s_v[t])[kind]

    outs = [loss, dx[None]]
    for kind in range(4):
        outs += [pick(kind, n) for n in order]
    return tuple(outs)
```

```python
import functools

import jax
import jax.numpy as jnp
from jax import lax
from jax.experimental import pallas as pl
from jax.experimental.pallas import tpu as pltpu

F32 = jnp.float32
BF16 = jnp.bfloat16
MESH = pl.DeviceIdType.MESH

D = 1024
DEPTH = 2
CH = 64
EPS = 1e-6
NEG = -1e30
N_DEV = 8
N_REL = 257
Q_SCALE = 48.0 ** -0.5
A_SCALE = 64.0 ** -0.5
GATE_TAU = 16.0
KCONV = 31

OQ, OKK, OV, OG, OCU, OAQ, OAK, OAV, OLR, DINP = 0, 256, 512, 896, 1280, 1792, 2176, 2560, 2944, 3072
IN_GROUPS = ((OQ, 256), (OKK, 256), (OV, 384), (OG, 384), (OCU, 512), (OAQ, 384), (OAK, 384), (OAV, 384), (OLR, 128))

AQ_BLK = 256
AK_WIN = 768
AK_PAD = 512
RB_W = 1536

ADAM_LR, ADAM_B1, ADAM_B2, ADAM_EPS, ADAM_WD, ADAM_STEP = 0.001, 0.9, 0.999, 1e-08, 0.01, 10


V7X_VMEM_MIB = 64
VMEM_LIMIT_MIB = V7X_VMEM_MIB - 1


def _cp(sem=None):
    kw = {"vmem_limit_bytes": VMEM_LIMIT_MIB * 1024 * 1024}
    if sem is not None:
        kw["dimension_semantics"] = sem
    return pltpu.CompilerParams(**kw)


def _dot(a, b):
    return jnp.dot(a, b, preferred_element_type=F32)


def _dot_nt(a, b):
    return lax.dot_general(a, b, (((1,), (1,)), ((), ())), preferred_element_type=F32)


def _dot_tn(a, b):
    return lax.dot_general(a, b, (((0,), (0,)), ((), ())), preferred_element_type=F32)


def _split2(a):
    hi = a.astype(BF16)
    lo = (a - hi.astype(F32)).astype(BF16)
    return hi, lo


def _split3(a):
    hi = a.astype(BF16)
    r1 = a - hi.astype(F32)
    mid = r1.astype(BF16)
    lo = (r1 - mid.astype(F32)).astype(BF16)
    return hi, mid, lo


def _sigmoid(x):
    return 1.0 / (1.0 + jnp.exp(-x))


def _group(idx, size, n):
    g = jnp.zeros_like(idx)
    for t in range(1, n):
        g = g + (idx >= t * size).astype(jnp.int32)
    return g


def _rms_bwd(dy, x, r, gamma):
    xh = x * r
    dxh = dy * gamma
    dx = r * (dxh - xh * jnp.mean(dxh * xh, axis=-1, keepdims=True))
    return dx, jnp.sum(dy * xh, axis=0, keepdims=True)


def _row_spec(tm, n):
    return pl.BlockSpec((tm, n), lambda i: (i, 0))


def _full_spec(shape):
    nd = len(shape)
    return pl.BlockSpec(shape, lambda *_: (0,) * nd)


def _any_spec():
    return pl.BlockSpec(memory_space=pl.ANY)


class _Job:
    def __init__(self, operands, out_shapes, sems, start, finish, aliases=None):
        self.operands, self.out_shapes, self.sems = list(operands), list(out_shapes), list(sems)
        self.start, self.finish, self.aliases = start, finish, dict(aliases or {})


def _pcall(body, *, name, grid, in_specs, out_specs, out_shape, operands, scratch_shapes=(), sem=None, jobs=()):
    jobs = list(jobs)
    in_specs, out_specs, out_shape = list(in_specs), list(out_specs), list(out_shape)
    scratch_shapes = list(scratch_shapes)
    n_in, n_out, n_scr = len(in_specs), len(out_specs), len(scratch_shapes)
    j_in = [a for j in jobs for a in j.operands]
    j_out = [s for j in jobs for s in j.out_shapes]
    j_sem = [s for j in jobs for s in j.sems]
    aliases, io, oo = {}, n_in, n_out
    for j in jobs:
        for a, b in j.aliases.items():
            aliases[io + a] = oo + b
        io += len(j.operands)
        oo += len(j.out_shapes)

    def wrapped(*refs):
        own_in, ji = refs[:n_in], refs[n_in:n_in + len(j_in)]
        o0 = n_in + len(j_in)
        own_out, jo = refs[o0:o0 + n_out], refs[o0 + n_out:o0 + n_out + len(j_out)]
        s0 = o0 + n_out + len(j_out)
        own_scr, js = refs[s0:s0 + n_scr], refs[s0 + n_scr:]

        def each_job(fn_name):
            a = b = c = 0
            for j in jobs:
                na, nb, nc = len(j.operands), len(j.out_shapes), len(j.sems)
                getattr(j, fn_name)(ji[a:a + na], jo[b:b + nb], js[c:c + nc])
                a, b, c = a + na, b + nb, c + nc

        if jobs and grid:
            pids = [pl.program_id(d) for d in range(len(grid))]
            first = functools.reduce(jnp.logical_and, [p == 0 for p in pids])
            last = functools.reduce(jnp.logical_and, [p == g - 1 for p, g in zip(pids, grid)])
            pl.when(first)(lambda: each_job("start"))
        elif jobs:
            each_job("start")

        body(*own_in, *own_out, *own_scr)

        if jobs and grid:
            pl.when(last)(lambda: each_job("finish"))
        elif jobs:
            each_job("finish")

    res = pl.pallas_call(
        wrapped, name=name, grid=grid,
        in_specs=in_specs + [_any_spec()] * len(j_in), out_specs=out_specs + [_any_spec()] * len(j_out),
        out_shape=out_shape + j_out, scratch_shapes=scratch_shapes + j_sem,
        input_output_aliases=aliases, compiler_params=_cp(sem),
    )(*operands, *j_in)
    return res[:n_out], res[n_out:]


ATT_HEADS = 6
HEAD_PAD = 128
ATT_WIDE = ATT_HEADS * HEAD_PAD
ATT_GROUP_OFFS = (OAQ, OAK, OAV)


def _store_head_padded(o_ref, part):
    o_ref[...] = jnp.zeros_like(o_ref)
    for hd in range(ATT_HEADS):
        o_ref[:, hd * HEAD_PAD:hd * HEAD_PAD + 64] = part[:, hd * 64:(hd + 1) * 64]


def _inproj_fwd(h, gamma, w, jobs=()):
    T = h.shape[0]
    tm = 512

    def body(h_ref, g_ref, w_ref, *outs):
        x = h_ref[...]
        r = lax.rsqrt(jnp.mean(x * x, axis=-1, keepdims=True) + EPS)
        xn = (x * r * g_ref[...]).astype(BF16)
        p = _dot(xn, w_ref[...])
        for o_ref, (off, n) in zip(outs, IN_GROUPS):
            part = p[:, off:off + n].astype(BF16)
            if off in ATT_GROUP_OFFS:
                _store_head_padded(o_ref, part)
            else:
                o_ref[...] = part

    widths = [ATT_WIDE if off in ATT_GROUP_OFFS else n for off, n in IN_GROUPS]
    return _pcall(
        body, name="inproj_fwd", grid=(T // tm,),
        in_specs=[_row_spec(tm, D), _full_spec((1, D)), _full_spec((D, DINP))],
        out_specs=[_row_spec(tm, n) for n in widths],
        out_shape=[jax.ShapeDtypeStruct((T, n), BF16) for n in widths],
        sem=("arbitrary",), operands=(h, gamma, w), jobs=jobs)


def _inproj_bwd(h, dh_in, gamma, w, dparts):
    T = h.shape[0]
    tm = 256
    nt = T // tm

    def body(h_ref, dhin_ref, g_ref, w_ref, *rest):
        dp_refs = rest[:9]
        dh_ref, dw_ref, dg_ref, acc = rest[9:]
        i = pl.program_id(0)

        @pl.when(i == 0)
        def _():
            acc[...] = jnp.zeros_like(acc)
            dg_ref[...] = jnp.zeros_like(dg_ref)

        x = h_ref[...]
        r = lax.rsqrt(jnp.mean(x * x, axis=-1, keepdims=True) + EPS)
        gamma_ = g_ref[...]
        xn = (x * r * gamma_).astype(BF16)
        dxn = jnp.zeros((tm, D), F32)
        for d_ref, (off, n) in zip(dp_refs, IN_GROUPS):
            d = d_ref[...]
            acc[:, off:off + n] += _dot_tn(xn, d)
            dxn = dxn + _dot_nt(d, w_ref[:, off:off + n])
        dx, dgam = _rms_bwd(dxn, x, r, gamma_)
        dh_ref[...] = dhin_ref[...] + dx
        dg_ref[...] += dgam

        @pl.when(i == nt - 1)
        def _():
            dw_ref[...] = acc[...].astype(BF16)

    return pl.pallas_call(
        body, name="inproj_bwd", grid=(nt,),
        in_specs=[_row_spec(tm, D), _row_spec(tm, D), _full_spec((1, D)), _full_spec((D, DINP))]
        + [_row_spec(tm, n) for _, n in IN_GROUPS],
        out_specs=[_row_spec(tm, D), _full_spec((D, DINP)), _full_spec((1, D))],
        out_shape=[jax.ShapeDtypeStruct((T, D), F32), jax.ShapeDtypeStruct((D, DINP), BF16),
                   jax.ShapeDtypeStruct((1, D), F32)],
        scratch_shapes=[pltpu.VMEM((D, DINP), F32)],
        compiler_params=_cp(("arbitrary",)),
    )(h, dh_in, gamma, w, *dparts)


GLA_ROWS = 512
GLA_NC = GLA_ROWS // CH


def _gla_consts():
    ri = lax.broadcasted_iota(jnp.int32, (CH, CH), 0)
    ci = lax.broadcasted_iota(jnp.int32, (CH, CH), 1)
    upper = (ci > ri).astype(BF16)
    vv = lax.broadcasted_iota(jnp.int32, (384, 256), 0)
    kk = lax.broadcasted_iota(jnp.int32, (384, 256), 1)
    mask_t = ((_group(vv, 96, 4) == _group(kk, 48, 4)) & (kk < 192)).astype(F32)
    pi = lax.broadcasted_iota(jnp.int32, (384, 384), 0)
    pj = lax.broadcasted_iota(jnp.int32, (384, 384), 1)
    same_head = (_group(pi, 96, 4) == _group(pj, 96, 4)).astype(BF16)
    return upper, mask_t, same_head


def _gla_gate(lr_ref, wg_ref, bg_ref):
    z = _dot(lr_ref[...], wg_ref[...]) + bg_ref[...]
    la = (jnp.minimum(z, 0.0) - jnp.log(1.0 + jnp.exp(-jnp.abs(z)))) * (1.0 / GATE_TAU)
    return z, la


def _gla_chunk_decay(la_c, upper):
    hi, lo = _split2(la_c)
    dec = _dot(upper, hi) + _dot(upper, lo)
    end = jnp.sum(la_c, axis=0, keepdims=True)
    return jnp.exp(dec), jnp.exp(end)


def _head_mean(x, same_head):
    hi, lo = _split2(x)
    return (_dot(hi, same_head) + _dot(lo, same_head)) * (1.0 / 96.0)


def _gla_fwd(q, k, v, g, lr, wg, bg, gn, jobs=()):
    T = q.shape[0]
    nb = T // GLA_ROWS

    def body(q_ref, k_ref, v_ref, g_ref, lr_ref, wg_ref, bg_ref, gn_ref, y_ref, st_ref, s_scr, o_scr):
        upper, mask_t, same_head = _gla_consts()

        @pl.when(pl.program_id(0) == 0)
        def _():
            s_scr[...] = jnp.zeros_like(s_scr)

        _, la = _gla_gate(lr_ref, wg_ref, bg_ref)
        for c in range(GLA_NC):
            rs = slice(c * CH, (c + 1) * CH)
            w, a = _gla_chunk_decay(la[rs], upper)
            kd = (k_ref[rs, :].astype(F32) * w).astype(BF16)
            kv_t = _dot_tn(v_ref[rs, :], kd)
            s_new = s_scr[...] * a + kv_t * mask_t
            s_scr[...] = s_new
            sb = s_new.astype(BF16)
            st_ref[c] = sb
            qs = (q_ref[rs, :].astype(F32) * Q_SCALE).astype(BF16)
            o_scr[rs, :] = _dot_nt(qs, sb)
        o = o_scr[...]
        r = lax.rsqrt(_head_mean(o * o, same_head) + EPS)
        gf = g_ref[...].astype(F32)
        y_ref[...] = (o * r * gn_ref[...] * (gf * _sigmoid(gf))).astype(BF16)

    return _pcall(
        body, name="gla_fwd", grid=(nb,),
        in_specs=[_row_spec(GLA_ROWS, 256), _row_spec(GLA_ROWS, 256), _row_spec(GLA_ROWS, 384),
                  _row_spec(GLA_ROWS, 384), _row_spec(GLA_ROWS, 128),
                  _full_spec((128, 256)), _full_spec((1, 256)), _full_spec((1, 384))],
        out_specs=[_row_spec(GLA_ROWS, 384), pl.BlockSpec((GLA_NC, 384, 256), lambda i: (i, 0, 0))],
        out_shape=[jax.ShapeDtypeStruct((T, 384), BF16), jax.ShapeDtypeStruct((T // CH, 384, 256), BF16)],
        scratch_shapes=[pltpu.VMEM((384, 256), F32), pltpu.VMEM((GLA_ROWS, 384), F32)],
        sem=("arbitrary",), operands=(q, k, v, g, lr, wg, bg, gn), jobs=jobs)


def _gla_bwd(q, k, v, g, lr, states, dy, wg, bg, gn):
    T = q.shape[0]
    nb = T // GLA_ROWS

    def rev(s):
        return nb - 1 - s

    def body(q_ref, k_ref, v_ref, g_ref, lr_ref, st_ref, stp_ref, dy_ref, wg_ref, bg_ref, gn_ref,
             dq_ref, dk_ref, dv_ref, dg_ref, dlr_ref, dwg_ref, dbg_ref, dgn_ref,
             d_scr, an_scr, o_scr, do_scr, dla_scr):
        upper, mask_t, same_head = _gla_consts()
        s = pl.program_id(0)
        blk = rev(s)

        @pl.when(s == 0)
        def _():
            d_scr[...] = jnp.zeros_like(d_scr)
            an_scr[...] = jnp.zeros_like(an_scr)
            dwg_ref[...] = jnp.zeros_like(dwg_ref)
            dbg_ref[...] = jnp.zeros_like(dbg_ref)
            dgn_ref[...] = jnp.zeros_like(dgn_ref)

        z, la = _gla_gate(lr_ref, wg_ref, bg_ref)
        ws, as_, qss, kds = [], [], [], []
        for c in range(GLA_NC):
            rs = slice(c * CH, (c + 1) * CH)
            w, a = _gla_chunk_decay(la[rs], upper)
            ws.append(w)
            as_.append(a)
            qs = (q_ref[rs, :].astype(F32) * Q_SCALE).astype(BF16)
            qss.append(qs)
            kds.append((k_ref[rs, :].astype(F32) * w).astype(BF16))
            o_scr[rs, :] = _dot_nt(qs, st_ref[c])
        o = o_scr[...]
        r = lax.rsqrt(_head_mean(o * o, same_head) + EPS)
        on = o * r
        gf = g_ref[...].astype(F32)
        sg = _sigmoid(gf)
        si = gf * sg
        dyf = dy_ref[...].astype(F32)
        gn_ = gn_ref[...]
        dgn_ref[...] += jnp.sum(dyf * si * on, axis=0, keepdims=True)
        dg_ref[...] = (dyf * on * gn_ * (sg * (1.0 + gf * (1.0 - sg)))).astype(BF16)
        d_on = dyf * si * gn_
        do_scr[...] = r * (d_on - on * _head_mean(d_on * on, same_head))

        first = (blk > 0).astype(F32)
        for c in reversed(range(GLA_NC)):
            rs = slice(c * CH, (c + 1) * CH)
            dob = do_scr[rs, :].astype(BF16)
            sb = st_ref[c]
            if c > 0:
                s_prev = st_ref[c - 1].astype(F32)
            else:
                s_prev = stp_ref[0].astype(F32) * first
            dq_ref[rs, :] = (_dot(dob, sb) * Q_SCALE).astype(BF16)
            dt = d_scr[...] * an_scr[...] + _dot_tn(dob, qss[c]) * mask_t
            d_scr[...] = dt
            da = jnp.sum(dt * s_prev, axis=0, keepdims=True)
            db = dt.astype(BF16)
            dkd = _dot(v_ref[rs, :], db)
            dv_ref[rs, :] = _dot_nt(kds[c], db).astype(BF16)
            dk_ref[rs, :] = (dkd * ws[c]).astype(BF16)
            ddec = dkd * k_ref[rs, :].astype(F32) * ws[c]
            hi, lo = _split2(ddec)
            dla_scr[rs, :] = _dot_tn(upper, hi) + _dot_tn(upper, lo) + as_[c] * da
            an_scr[...] = as_[c]

        dz = dla_scr[...] * (1.0 - _sigmoid(z)) * (1.0 / GATE_TAU)
        dzb = dz.astype(BF16)
        dlr_ref[...] = _dot_nt(dzb, wg_ref[...]).astype(BF16)
        dwg_ref[...] += _dot_tn(lr_ref[...], dzb)
        dbg_ref[...] += jnp.sum(dz, axis=0, keepdims=True)

    def rspec(n):
        return pl.BlockSpec((GLA_ROWS, n), lambda s: (rev(s), 0))

    return pl.pallas_call(
        body, name="gla_bwd", grid=(nb,),
        in_specs=[rspec(256), rspec(256), rspec(384), rspec(384), rspec(128),
                  pl.BlockSpec((GLA_NC, 384, 256), lambda s: (rev(s), 0, 0)),
                  pl.BlockSpec((1, 384, 256), lambda s: (jnp.maximum(rev(s) * GLA_NC - 1, 0), 0, 0)),
                  rspec(384), _full_spec((128, 256)), _full_spec((1, 256)), _full_spec((1, 384))],
        out_specs=[rspec(256), rspec(256), rspec(384), rspec(384), rspec(128),
                   _full_spec((128, 256)), _full_spec((1, 256)), _full_spec((1, 384))],
        out_shape=[jax.ShapeDtypeStruct((T, 256), BF16), jax.ShapeDtypeStruct((T, 256), BF16),
                   jax.ShapeDtypeStruct((T, 384), BF16), jax.ShapeDtypeStruct((T, 384), BF16),
                   jax.ShapeDtypeStruct((T, 128), BF16),
                   jax.ShapeDtypeStruct((128, 256), F32), jax.ShapeDtypeStruct((1, 256), F32),
                   jax.ShapeDtypeStruct((1, 384), F32)],
        scratch_shapes=[pltpu.VMEM((384, 256), F32), pltpu.VMEM((1, 256), F32),
                        pltpu.VMEM((GLA_ROWS, 384), F32), pltpu.VMEM((GLA_ROWS, 384), F32),
                        pltpu.VMEM((GLA_ROWS, 256), F32)],
        compiler_params=_cp(("arbitrary",)),
    )(q, k, v, g, lr, states, states, dy, wg, bg, gn)


CONV_ROWS = 512
HALO = 32
SUBL = 8
CONV_SLAB = 32
PHASE_ROWS = CONV_ROWS + HALO - SUBL
FWD_SHIFT = tuple(HALO - (KCONV - 1) + j for j in range(KCONV))
BWD_SHIFT = tuple(KCONV - 1 - j for j in range(KCONV))


def _fill_phases(buf, ph):
    for f in range(1, SUBL):
        ph[f, 0:PHASE_ROWS, :] = buf[pl.ds(f, PHASE_ROWS), :]


def _tap(buf, ph, shift, r, n):
    f, base = shift % SUBL, shift - shift % SUBL
    src = buf if f == 0 else ph.at[f]
    return src[pl.ds(base + r, n), :]


def _taps_apply(w_ref, buf, ph, shifts, out):
    for r in range(0, CONV_ROWS, CONV_SLAB):
        acc = jnp.zeros((CONV_SLAB, 256), F32)
        for j in range(KCONV):
            acc = acc + w_ref[j:j + 1, :] * _tap(buf, ph, shifts[j], r, CONV_SLAB)
        out[r:r + CONV_SLAB, :] = acc


def _conv_scratch():
    return [pltpu.VMEM((CONV_ROWS + HALO, 256), F32), pltpu.VMEM((SUBL, CONV_ROWS + HALO, 256), F32),
            pltpu.VMEM((CONV_ROWS, 256), F32)]


def _conv_common(cu_ref, halo_ref, w_ref, b_ref, lg_ref, lb_ref, buf, ph, cbuf, blk):
    u = cu_ref[...].astype(F32)
    a = u[:, :256]
    sb = _sigmoid(u[:, 256:])
    uh = halo_ref[...].astype(F32)
    hh = uh[:, :256] * _sigmoid(uh[:, 256:]) * (blk > 0).astype(F32)
    buf[0:HALO, :] = hh
    buf[HALO:HALO + CONV_ROWS, :] = a * sb
    _fill_phases(buf, ph)
    _taps_apply(w_ref, buf, ph, FWD_SHIFT, cbuf)
    cc = cbuf[...] + b_ref[...]
    mu = jnp.mean(cc, axis=-1, keepdims=True)
    xc = cc - mu
    rstd = lax.rsqrt(jnp.mean(xc * xc, axis=-1, keepdims=True) + EPS)
    n = xc * rstd
    yln = n * lg_ref[...] + lb_ref[...]
    return a, sb, n, rstd, yln


def _conv_fwd(cu, w, b, lg, lb, jobs=()):
    T = cu.shape[0]
    nb = T // CONV_ROWS
    per = CONV_ROWS // HALO

    def body(cu_ref, halo_ref, w_ref, b_ref, lg_ref, lb_ref, y_ref, buf, ph, cbuf):
        _, _, _, _, yln = _conv_common(cu_ref, halo_ref, w_ref, b_ref, lg_ref, lb_ref, buf, ph, cbuf,
                                       pl.program_id(0))
        y_ref[...] = (yln * _sigmoid(yln)).astype(BF16)

    return _pcall(
        body, name="conv_fwd", grid=(nb,),
        in_specs=[_row_spec(CONV_ROWS, 512),
                  pl.BlockSpec((HALO, 512), lambda i: (jnp.maximum(i * per - 1, 0), 0)),
                  _full_spec((32, 256)), _full_spec((1, 256)), _full_spec((1, 256)), _full_spec((1, 256))],
        out_specs=[_row_spec(CONV_ROWS, 256)],
        out_shape=[jax.ShapeDtypeStruct((T, 256), BF16)],
        scratch_shapes=_conv_scratch(),
        sem=("arbitrary",), operands=(cu, cu, w, b, lg, lb), jobs=jobs)


def _conv_bwd(cu, dy, w, b, lg, lb, jobs=()):
    T = cu.shape[0]
    nb = T // CONV_ROWS
    per = CONV_ROWS // HALO

    def rev(s):
        return nb - 1 - s

    def body(cu_ref, halo_ref, dy_ref, w_ref, b_ref, lg_ref, lb_ref,
             dcu_ref, dw_ref, db_ref, dlg_ref, dlb_ref, buf, ph, cbuf, dcbuf, dph, carry):
        s = pl.program_id(0)

        @pl.when(s == 0)
        def _():
            carry[...] = jnp.zeros_like(carry)
            dw_ref[...] = jnp.zeros_like(dw_ref)
            db_ref[...] = jnp.zeros_like(db_ref)
            dlg_ref[...] = jnp.zeros_like(dlg_ref)
            dlb_ref[...] = jnp.zeros_like(dlb_ref)

        a, sb, n, rstd, yln = _conv_common(cu_ref, halo_ref, w_ref, b_ref, lg_ref, lb_ref, buf, ph, cbuf, rev(s))
        sg = _sigmoid(yln)
        dyln = dy_ref[...].astype(F32) * (sg * (1.0 + yln * (1.0 - sg)))
        dlg_ref[...] += jnp.sum(dyln * n, axis=0, keepdims=True)
        dlb_ref[...] += jnp.sum(dyln, axis=0, keepdims=True)
        dn = dyln * lg_ref[...]
        dc = rstd * (dn - jnp.mean(dn, axis=-1, keepdims=True) - n * jnp.mean(dn * n, axis=-1, keepdims=True))
        db_ref[...] += jnp.sum(dc, axis=0, keepdims=True)
        dcbuf[0:CONV_ROWS, :] = dc
        dcbuf[CONV_ROWS:CONV_ROWS + HALO, :] = carry[...]
        carry[...] = dc[0:HALO, :]
        _fill_phases(dcbuf, dph)
        for j in range(KCONV):
            acc = jnp.zeros((SUBL, 256), F32)
            for r in range(0, CONV_ROWS, 2 * CONV_SLAB):
                prod = dcbuf[r:r + 2 * CONV_SLAB, :] * _tap(buf, ph, FWD_SHIFT[j], r, 2 * CONV_SLAB)
                acc = acc + jnp.sum(prod.reshape(2 * CONV_SLAB // SUBL, SUBL, 256), axis=0)
            dw_ref[j:j + 1, :] += jnp.sum(acc, axis=0, keepdims=True)
        _taps_apply(w_ref, dcbuf, dph, BWD_SHIFT, cbuf)
        dhg = cbuf[...]
        dcu_ref[...] = jnp.concatenate([dhg * sb, dhg * a * sb * (1.0 - sb)], axis=1).astype(BF16)

    def rspec(n):
        return pl.BlockSpec((CONV_ROWS, n), lambda s: (rev(s), 0))

    return _pcall(
        body, name="conv_bwd", grid=(nb,),
        in_specs=[rspec(512),
                  pl.BlockSpec((HALO, 512), lambda s: (jnp.maximum(rev(s) * per - 1, 0), 0)),
                  rspec(256),
                  _full_spec((32, 256)), _full_spec((1, 256)), _full_spec((1, 256)), _full_spec((1, 256))],
        out_specs=[rspec(512), _full_spec((32, 256)), _full_spec((1, 256)), _full_spec((1, 256)),
                   _full_spec((1, 256))],
        out_shape=[jax.ShapeDtypeStruct((T, 512), BF16), jax.ShapeDtypeStruct((32, 256), F32),
                   jax.ShapeDtypeStruct((1, 256), F32), jax.ShapeDtypeStruct((1, 256), F32),
                   jax.ShapeDtypeStruct((1, 256), F32)],
        scratch_shapes=_conv_scratch() + [pltpu.VMEM((CONV_ROWS + HALO, 256), F32),
                                          pltpu.VMEM((SUBL, CONV_ROWS + HALO, 256), F32),
                                          pltpu.VMEM((HALO, 256), F32)],
        sem=("arbitrary",), operands=(cu, cu, dy, w, b, lg, lb), jobs=jobs)


def _rel_onehot_t(shift=0):
    r = lax.broadcasted_iota(jnp.int32, (384, RB_W), 0)
    n = lax.broadcasted_iota(jnp.int32, (384, RB_W), 1) - shift
    idx = jnp.clip(1024 - n, -128, 128) + 128
    return (idx == r).astype(BF16)


def _relbias_expand(rb):
    def body(rb_ref, out_ref):
        oh = _rel_onehot_t()
        hi, mid, lo = _split3(rb_ref[...])
        strip = _dot(hi, oh) + _dot(mid, oh) + _dot(lo, oh)
        qi = _group(lax.broadcasted_iota(jnp.int32, (AQ_BLK, AK_WIN), 0), CH, 4)
        kj = _group(lax.broadcasted_iota(jnp.int32, (AQ_BLK, AK_WIN), 1), CH, 12)
        valid = (kj >= qi) & (kj <= qi + 8)
        for hd in range(6):
            x = jnp.broadcast_to(strip[hd:hd + 1, :], (AQ_BLK, RB_W))
            xr = pltpu.roll(x, 0, 1, stride=1, stride_axis=0)
            out_ref[hd] = jnp.where(valid, xr[:, 512:512 + AK_WIN], NEG)

    return pl.pallas_call(
        body, name="relbias_expand",
        out_shape=jax.ShapeDtypeStruct((6, AQ_BLK, AK_WIN), F32),
        compiler_params=_cp(),
    )(rb)


def _relbias_grad(dbias):
    def body(db_ref, out_ref):
        oh = _rel_onehot_t(AQ_BLK - 1)
        ri = lax.broadcasted_iota(jnp.int32, (AQ_BLK, AQ_BLK), 0)
        ci = lax.broadcasted_iota(jnp.int32, (AQ_BLK, AQ_BLK), 1)
        flip = (ri + ci == AQ_BLK - 1).astype(BF16)
        rows = []
        for hd in range(6):
            hi, mid, lo = _split3(db_ref[hd])
            rev = _dot(flip, hi) + _dot(flip, mid) + _dot(flip, lo)
            x = jnp.concatenate([jnp.zeros((AQ_BLK, 512), F32), rev,
                                 jnp.zeros((AQ_BLK, RB_W - 512 - AK_WIN), F32)], axis=1)
            xr = pltpu.roll(x, 0, 1, stride=1, stride_axis=0)
            rows.append(jnp.sum(xr, axis=0, keepdims=True))
        rows.append(jnp.zeros((2, RB_W), F32))
        dstrip = jnp.concatenate(rows, axis=0)
        hi, mid, lo = _split3(dstrip)
        out_ref[...] = _dot_nt(hi, oh) + _dot_nt(mid, oh) + _dot_nt(lo, oh)

    return pl.pallas_call(
        body, name="relbias_grad",
        out_shape=jax.ShapeDtypeStruct((8, 384), F32),
        compiler_params=_cp(),
    )(dbias)


ATT_SLAB = 8


def _att_softmax_slab(s_scr, b_ref, hd, rows, first_key):
    kvalid = lax.broadcasted_iota(jnp.int32, (ATT_SLAB, AK_WIN), 1) >= first_key
    s = jnp.where(kvalid, s_scr[rows, :] + b_ref[hd, rows, :], NEG)
    m = jnp.max(s, axis=-1, keepdims=True)
    p = jnp.exp(s - m)
    return p * (1.0 / jnp.sum(p, axis=-1, keepdims=True))


def _att_first_key(i):
    return (8 - 4 * i) * CH


def _slab_rows(t):
    return pl.ds(t * ATT_SLAB, ATT_SLAB)


def _head_lanes(hd):
    return slice(hd * 64, (hd + 1) * 64)


WIN_BLKS = AK_WIN // AQ_BLK


def _head_tile(hd):
    return slice(hd * HEAD_PAD, (hd + 1) * HEAD_PAD)


def _win_cols(d):
    return slice(d * AQ_BLK, (d + 1) * AQ_BLK)


def _win_specs():
    return [pl.BlockSpec((AQ_BLK, ATT_WIDE), lambda i, d=d: (i + d, 0)) for d in range(WIN_BLKS)]


def _att_fwd(q, kpad, vpad, bias, jobs=()):
    T = q.shape[0]
    nb = T // AQ_BLK

    def body(q_ref, k0, k1, k2, v0, v1, v2, b_ref, o_ref, s_scr):
        k_refs, v_refs = (k0, k1, k2), (v0, v1, v2)
        first_key = _att_first_key(pl.program_id(0))

        def scores(hd):
            q_h = q_ref[:, _head_tile(hd)] * A_SCALE
            for d in range(WIN_BLKS):
                s_scr[hd % 2, :, _win_cols(d)] = _dot_nt(q_h, k_refs[d][:, _head_tile(hd)])

        scores(0)
        for hd in range(ATT_HEADS):
            if hd + 1 < ATT_HEADS:
                scores(hd + 1)
            s_h = s_scr.at[hd % 2]
            for t in range(AQ_BLK // ATT_SLAB):
                rows = _slab_rows(t)
                s_h[rows, :] = _att_softmax_slab(s_h, b_ref, hd, rows, first_key)
            o_h = _dot(s_h[:, _win_cols(0)].astype(BF16), v_refs[0][:, _head_tile(hd)])
            for d in range(1, WIN_BLKS):
                o_h = o_h + _dot(s_h[:, _win_cols(d)].astype(BF16), v_refs[d][:, _head_tile(hd)])
            o_ref[:, _head_lanes(hd)] = o_h[:, :64].astype(BF16)

    return _pcall(
        body, name="att_fwd", grid=(nb,),
        in_specs=[_row_spec(AQ_BLK, ATT_WIDE)] + _win_specs() + _win_specs() + [_full_spec((6, AQ_BLK, AK_WIN))],
        out_specs=[_row_spec(AQ_BLK, 384)],
        out_shape=[jax.ShapeDtypeStruct((T, 384), BF16)],
        scratch_shapes=[pltpu.VMEM((2, AQ_BLK, AK_WIN), F32)],
        sem=("arbitrary",), operands=(q, kpad, kpad, kpad, vpad, vpad, vpad, bias), jobs=jobs)


def _att_bwd(q, kpad, vpad, bias, do, jobs=()):
    T = q.shape[0]
    nb = T // AQ_BLK

    def body(q_ref, k0, k1, k2, v0, v1, v2, b_ref, do_ref, dq_ref, dk_ref, dv_ref, db_ref, dk_acc, dv_acc,
             s_scr, dp_scr):
        k_refs, v_refs = (k0, k1, k2), (v0, v1, v2)
        i = pl.program_id(0)

        @pl.when(i == 0)
        def _():
            dk_acc[...] = jnp.zeros_like(dk_acc)
            dv_acc[...] = jnp.zeros_like(dv_acc)
            db_ref[...] = jnp.zeros_like(db_ref)

        start = pl.multiple_of(i * AQ_BLK, AQ_BLK)
        first_key = _att_first_key(i)

        def scores(hd):
            q_h = q_ref[:, _head_tile(hd)] * A_SCALE
            do_h = do_ref[:, _head_tile(hd)]
            for d in range(WIN_BLKS):
                s_scr[hd % 2, :, _win_cols(d)] = _dot_nt(q_h, k_refs[d][:, _head_tile(hd)])
                dp_scr[hd % 2, :, _win_cols(d)] = _dot_nt(do_h, v_refs[d][:, _head_tile(hd)])

        scores(0)
        for hd in range(ATT_HEADS):
            if hd + 1 < ATT_HEADS:
                scores(hd + 1)
            s_h, dp_h = s_scr.at[hd % 2], dp_scr.at[hd % 2]
            for t in range(AQ_BLK // ATT_SLAB):
                rows = _slab_rows(t)
                p = _att_softmax_slab(s_h, b_ref, hd, rows, first_key)
                dp = dp_h[rows, :]
                ds = p * (dp - jnp.sum(p * dp, axis=-1, keepdims=True))
                db_ref[hd, rows, :] += ds
                s_h[rows, :] = p
                dp_h[rows, :] = ds
            q_h = q_ref[:, _head_tile(hd)] * A_SCALE
            do_h = do_ref[:, _head_tile(hd)]
            ls = _head_lanes(hd)
            dq_h = jnp.zeros((AQ_BLK, HEAD_PAD), F32)
            for d in range(WIN_BLKS):
                pb = s_h[:, _win_cols(d)].astype(BF16)
                dsb = dp_h[:, _win_cols(d)].astype(BF16)
                rows = pl.ds(start + d * AQ_BLK, AQ_BLK)
                dv_acc[rows, ls] += _dot_tn(pb, do_h)[:, :64]
                dk_acc[rows, ls] += _dot_tn(dsb, q_h)[:, :64]
                dq_h = dq_h + _dot(dsb, k_refs[d][:, _head_tile(hd)])
            dq_ref[:, ls] = (dq_h[:, :64] * A_SCALE).astype(BF16)

        @pl.when(i == nb - 1)
        def _():
            dk_ref[...] = dk_acc[AK_PAD:, :].astype(BF16)
            dv_ref[...] = dv_acc[AK_PAD:, :].astype(BF16)

    return _pcall(
        body, name="att_bwd", grid=(nb,),
        in_specs=[_row_spec(AQ_BLK, ATT_WIDE)] + _win_specs() + _win_specs()
        + [_full_spec((6, AQ_BLK, AK_WIN)), _row_spec(AQ_BLK, ATT_WIDE)],
        out_specs=[_row_spec(AQ_BLK, 384), _full_spec((T, 384)), _full_spec((T, 384)),
                   _full_spec((6, AQ_BLK, AK_WIN))],
        out_shape=[jax.ShapeDtypeStruct((T, 384), BF16), jax.ShapeDtypeStruct((T, 384), BF16),
                   jax.ShapeDtypeStruct((T, 384), BF16), jax.ShapeDtypeStruct((6, AQ_BLK, AK_WIN), F32)],
        scratch_shapes=[pltpu.VMEM((T + AK_PAD, 384), F32), pltpu.VMEM((T + AK_PAD, 384), F32),
                        pltpu.VMEM((2, AQ_BLK, AK_WIN), F32), pltpu.VMEM((2, AQ_BLK, AK_WIN), F32)],
        sem=("arbitrary",), operands=(q, kpad, kpad, kpad, vpad, vpad, vpad, bias, do), jobs=jobs)


FF_BLK = 512
N_FF = 4096 // FF_BLK


def _outproj_mlp_fwd(h, o_gla, o_conv, o_att, w_out, gamma, w_up, w_down, jobs=()):
    T = h.shape[0]
    tm = 512

    def body(h_ref, og_ref, oc_ref, oa_ref, wo_ref, g_ref, wu_ref, wd_ref, h1_ref, xn_ref, h2_ref, acc):
        j = pl.program_id(1)

        @pl.when(j == 0)
        def _():
            wo = wo_ref[...]
            h1 = (h_ref[...] + _dot(og_ref[...], wo[0:384]) + _dot(oc_ref[...], wo[384:640])
                  + _dot(oa_ref[...], wo[640:1024]))
            h1_ref[...] = h1
            r = lax.rsqrt(jnp.mean(h1 * h1, axis=-1, keepdims=True) + EPS)
            xn_ref[...] = (h1 * r * g_ref[...]).astype(BF16)
            acc[...] = h1

        a = jnp.maximum(_dot(xn_ref[...], wu_ref[0]), 0.0)
        acc[...] += _dot((a * a).astype(BF16), wd_ref[0])

        @pl.when(j == N_FF - 1)
        def _():
            h2_ref[...] = acc[...]

    row = lambda n: pl.BlockSpec((tm, n), lambda i, j: (i, 0))
    return _pcall(
        body, name="outproj_mlp_fwd", grid=(T // tm, N_FF),
        in_specs=[row(D), row(384), row(256), row(384),
                  pl.BlockSpec((D, D), lambda i, j: (0, 0)), pl.BlockSpec((1, D), lambda i, j: (0, 0)),
                  pl.BlockSpec((1, D, FF_BLK), lambda i, j: (j, 0, 0)),
                  pl.BlockSpec((1, FF_BLK, D), lambda i, j: (j, 0, 0))],
        out_specs=[row(D), row(D), row(D)],
        out_shape=[jax.ShapeDtypeStruct((T, D), F32), jax.ShapeDtypeStruct((T, D), BF16),
                   jax.ShapeDtypeStruct((T, D), F32)],
        scratch_shapes=[pltpu.VMEM((tm, D), F32)],
        sem=("arbitrary", "arbitrary"), operands=(h, o_gla, o_conv, o_att, w_out, gamma, w_up, w_down), jobs=jobs)


def _mlp_bwd(xn2, h1, dh2, gamma, w_up, w_down, jobs=()):
    T = xn2.shape[0]
    tm = 512
    nt = T // tm
    last = N_FF - 1

    def body(xn_ref, h1_ref, dy_ref, g_ref, wu_ref, wd_ref, dh1_ref, dwu_ref, dwd_ref, dg_ref,
             dxn_acc, acc_u, acc_d):
        j = pl.program_id(0)
        i = pl.program_id(1)
        x = xn_ref[...]
        dy = dy_ref[...]
        dyb = dy.astype(BF16)
        wu = wu_ref[0]
        wd = wd_ref[0]
        a = jnp.maximum(_dot(x, wu), 0.0)
        hh = (a * a).astype(BF16)
        du = (_dot_nt(dyb, wd) * (2.0 * a)).astype(BF16)
        cu_ = _dot_tn(x, du)
        cd_ = _dot_tn(hh, dyb)

        @pl.when(i == 0)
        def _():
            acc_u[...] = cu_
            acc_d[...] = cd_

        @pl.when(i > 0)
        def _():
            acc_u[...] += cu_
            acc_d[...] += cd_

        @pl.when(i == nt - 1)
        def _():
            dwu_ref[0, 0] = acc_u[...].astype(BF16)
            dwd_ref[0, 0] = acc_d[...].astype(BF16)

        rows = pl.ds(pl.multiple_of(i * tm, tm), tm)
        dxn = _dot_nt(du, wu)

        @pl.when(j == 0)
        def _():
            dxn_acc[rows, :] = dxn

        @pl.when(j > 0)
        def _():
            dxn_acc[rows, :] += dxn

        @pl.when(j == last)
        def _():
            @pl.when(i == 0)
            def _():
                dg_ref[...] = jnp.zeros_like(dg_ref)

            h1 = h1_ref[...]
            r = lax.rsqrt(jnp.mean(h1 * h1, axis=-1, keepdims=True) + EPS)
            dx, dgam = _rms_bwd(dxn_acc[rows, :], h1, r, g_ref[...])
            dh1_ref[...] = dy + dx
            dg_ref[...] += dgam

    late = lambda j, i: (jnp.where(j == last, i, 0), 0)
    return _pcall(
        body, name="mlp_bwd", grid=(N_FF, nt),
        in_specs=[pl.BlockSpec((tm, D), lambda j, i: (i, 0)), pl.BlockSpec((tm, D), late),
                  pl.BlockSpec((tm, D), lambda j, i: (i, 0)), pl.BlockSpec((1, D), lambda j, i: (0, 0)),
                  pl.BlockSpec((1, D, FF_BLK), lambda j, i: (j, 0, 0)),
                  pl.BlockSpec((1, FF_BLK, D), lambda j, i: (j, 0, 0))],
        out_specs=[pl.BlockSpec((tm, D), late),
                   pl.BlockSpec((1, 1, D, FF_BLK), lambda j, i: (j % 2, j // 2, 0, 0)),
                   pl.BlockSpec((1, 1, FF_BLK, D), lambda j, i: (j % 2, j // 2, 0, 0)),
                   pl.BlockSpec((1, D), lambda j, i: (0, 0))],
        out_shape=[jax.ShapeDtypeStruct((T, D), F32), jax.ShapeDtypeStruct((2, 4, D, FF_BLK), BF16),
                   jax.ShapeDtypeStruct((2, 4, FF_BLK, D), BF16), jax.ShapeDtypeStruct((1, D), F32)],
        scratch_shapes=[pltpu.VMEM((T, D), F32), pltpu.VMEM((D, FF_BLK), F32), pltpu.VMEM((FF_BLK, D), F32)],
        sem=("arbitrary", "arbitrary"), operands=(xn2, h1, dh2, gamma, w_up, w_down), jobs=jobs)


def _outproj_bwd(dh1, o_gla, o_conv, o_att, w_out, jobs=()):
    T = dh1.shape[0]
    tm = 512
    nt = T // tm

    def body(dy_ref, og_ref, oc_ref, oa_ref, wo_ref, dg_ref, dc_ref, da_ref, dw_ref, acc):
        i = pl.program_id(0)
        dyb = dy_ref[...].astype(BF16)
        dm = _dot_nt(dyb, wo_ref[...])
        dg_ref[...] = dm[:, 0:384].astype(BF16)
        dc_ref[...] = dm[:, 384:640].astype(BF16)
        _store_head_padded(da_ref, dm[:, 640:1024].astype(BF16))
        mixed = jnp.concatenate([og_ref[...], oc_ref[...], oa_ref[...]], axis=1)
        contrib = _dot_tn(mixed, dyb)

        @pl.when(i == 0)
        def _():
            acc[...] = contrib

        @pl.when(i > 0)
        def _():
            acc[...] += contrib

        @pl.when(i == nt - 1)
        def _():
            for j in range(N_DEV):
                dw_ref[j % 2, j // 2] = acc[j * 128:(j + 1) * 128, :].astype(BF16)

    return _pcall(
        body, name="outproj_bwd", grid=(nt,),
        in_specs=[_row_spec(tm, D), _row_spec(tm, 384), _row_spec(tm, 256), _row_spec(tm, 384),
                  _full_spec((D, D))],
        out_specs=[_row_spec(tm, 384), _row_spec(tm, 256), _row_spec(tm, ATT_WIDE), _full_spec((2, 4, 128, D))],
        out_shape=[jax.ShapeDtypeStruct((T, 384), BF16), jax.ShapeDtypeStruct((T, 256), BF16),
                   jax.ShapeDtypeStruct((T, ATT_WIDE), BF16), jax.ShapeDtypeStruct((2, 4, 128, D), BF16)],
        scratch_shapes=[pltpu.VMEM((D, D), F32)],
        sem=("arbitrary",), operands=(dh1, o_gla, o_conv, o_att, w_out), jobs=jobs)


def _loss_fwd_bwd(h, gamma, target):
    T = h.shape[0]
    tm = 512

    def body(h_ref, g_ref, t_ref, loss_ref, dh_ref, dg_ref):
        @pl.when(pl.program_id(0) == 0)
        def _():
            loss_ref[...] = jnp.zeros_like(loss_ref)
            dg_ref[...] = jnp.zeros_like(dg_ref)

        x = h_ref[...]
        r = lax.rsqrt(jnp.mean(x * x, axis=-1, keepdims=True) + EPS)
        gamma_ = g_ref[...]
        e = x * r * gamma_ - t_ref[...]
        loss_ref[...] += 0.5 * jnp.sum(jnp.mean(e * e, axis=-1, keepdims=True), axis=0, keepdims=True)
        dx, dgam = _rms_bwd(e * (1.0 / D), x, r, gamma_)
        dh_ref[...] = dx
        dg_ref[...] += dgam

    return pl.pallas_call(
        body, name="loss_fwd_bwd", grid=(T // tm,),
        in_specs=[_row_spec(tm, D), _full_spec((1, D)), _row_spec(tm, D)],
        out_specs=[_full_spec((8, 128)), _row_spec(tm, D), _full_spec((1, D))],
        out_shape=[jax.ShapeDtypeStruct((8, 128), F32), jax.ShapeDtypeStruct((T, D), F32),
                   jax.ShapeDtypeStruct((1, D), F32)],
        compiler_params=_cp(("arbitrary",)),
    )(h, gamma, target)


def _adamw_math(w, g, m, v):
    m = ADAM_B1 * m + (1.0 - ADAM_B1) * g
    v = ADAM_B2 * v + (1.0 - ADAM_B2) * (g * g)
    m_hat = m / (1.0 - ADAM_B1 ** ADAM_STEP)
    v_hat = v / (1.0 - ADAM_B2 ** ADAM_STEP)
    delta = -ADAM_LR * (m_hat / (jnp.sqrt(v_hat) + ADAM_EPS) + ADAM_WD * w)
    return delta, m, v


def _rs_adamw(a_own, r2, w, m, v, layer, chip_idx, rows_blk, prev=None):
    _, R, C = w.shape
    nblk = R // rows_blk

    def body(chip_ref, a_ref, r_ref, w_ref, m_ref, v_ref, *rest):
        g_out, d_out, m_out, v_out = rest[-4:]
        g = (a_ref[0].astype(F32) + r_ref[0].astype(F32)) + (r_ref[1].astype(F32) + r_ref[2].astype(F32))
        delta, m_new, v_new = _adamw_math(w_ref[0], g, m_ref[0], v_ref[0])
        g_out[0] = g
        d_out[0] = delta
        m_out[0] = m_new
        v_out[0] = v_new

    blk = pl.BlockSpec((1, rows_blk, C), lambda i, chip: (layer, i, 0))
    n_prev = 0 if prev is None else 4
    grid_spec = pltpu.PrefetchScalarGridSpec(
        num_scalar_prefetch=1, grid=(nblk,),
        in_specs=[pl.BlockSpec((1, rows_blk, C), lambda i, chip: (chip[0], i, 0)),
                  pl.BlockSpec((3, rows_blk, C), lambda i, chip: (0, i, 0)), blk, blk, blk]
        + [_any_spec()] * n_prev,
        out_specs=[blk, blk, blk, blk])
    return pl.pallas_call(
        body, name="rs_adamw", grid_spec=grid_spec,
        out_shape=[jax.ShapeDtypeStruct((DEPTH, R, C), F32)] * 4,
        input_output_aliases={6 + t: t for t in range(n_prev)},
        compiler_params=_cp(("arbitrary",)),
    )(chip_idx, a_own, r2, w, m, v, *(prev or ()))


def _pair_sum(g, r1, core_idx, rows_blk):
    _, _, R, C = g.shape
    nblk = R // rows_blk

    def body(core_ref, g_ref, r_ref, o_ref):
        o_ref[...] = (g_ref[0].astype(F32) + r_ref[...].astype(F32)).astype(BF16)

    grid_spec = pltpu.PrefetchScalarGridSpec(
        num_scalar_prefetch=1, grid=(4, nblk),
        in_specs=[pl.BlockSpec((1, 1, rows_blk, C), lambda k, i, core: (core[0], k, i, 0)),
                  pl.BlockSpec((1, rows_blk, C), lambda k, i, core: (k, i, 0))],
        out_specs=pl.BlockSpec((1, rows_blk, C), lambda k, i, core: (k, i, 0)))
    return pl.pallas_call(
        body, name="rs_pair_sum", grid_spec=grid_spec,
        out_shape=jax.ShapeDtypeStruct((4, R, C), BF16),
        compiler_params=_cp(("arbitrary", "arbitrary")),
    )(core_idx, g, r1)


def _small_sum(gathered):
    def body(g_ref, o_ref):
        acc = g_ref[0]
        for d in range(1, N_DEV):
            acc = acc + g_ref[d]
        o_ref[...] = acc

    return pl.pallas_call(
        body, name="small_sum",
        out_shape=jax.ShapeDtypeStruct(gathered.shape[1:], F32),
        compiler_params=_cp(),
    )(gathered)


def _adamw_small(ws, gs, ms, vs):
    n = len(ws)

    def body(*refs):
        w_r, g_r, m_r, v_r = refs[0:n], refs[n:2 * n], refs[2 * n:3 * n], refs[3 * n:4 * n]
        d_o, m_o, v_o = refs[4 * n:5 * n], refs[5 * n:6 * n], refs[6 * n:7 * n]
        for t in range(n):
            delta, m_new, v_new = _adamw_math(w_r[t][...], g_r[t][...], m_r[t][...], v_r[t][...])
            d_o[t][...] = delta
            m_o[t][...] = m_new
            v_o[t][...] = v_new

    shapes = [jax.ShapeDtypeStruct(w.shape, F32) for w in ws]
    outs = pl.pallas_call(
        body, name="adamw_small", out_shape=shapes * 3, compiler_params=_cp(),
    )(*ws, *gs, *ms, *vs)
    return outs[0:n], outs[n:2 * n], outs[2 * n:3 * n]


def _mesh_pos():
    return lax.axis_index("x"), lax.axis_index("y"), lax.axis_index("c")


def _peers():
    x, y, c = _mesh_pos()
    return (x, y, c), (x, y, 1 - c), [(1 - x, y), (x, 1 - y), (1 - x, 1 - y)]


def _slot(ref, pos):
    return ref.at[4 * pos[0] + 2 * pos[1] + pos[2]]


def _remote(src, dst, send_sem, recv_sem, to):
    return pltpu.make_async_remote_copy(src_ref=src, dst_ref=dst, send_sem=send_sem, recv_sem=recv_sem,
                                        device_id=to, device_id_type=MESH)


def _ag_spread(shards):
    n = len(shards)

    def copies(ins, outs, sems):
        send, recv, loc = sems
        me, sibling, chips = _peers()
        peers = [sibling] + [(*chip, me[2]) for chip in chips]
        local = [pltpu.make_async_copy(ins[a], _slot(outs[a], me), loc.at[a]) for a in range(n)]
        sends = [_remote(ins[a], _slot(outs[a], me), send.at[a, k], recv.at[a, k], p)
                 for a in range(n) for k, p in enumerate(peers)]
        recvs = [_remote(ins[a], _slot(outs[a], p), send.at[a, k], recv.at[a, k], p)
                 for a in range(n) for k, p in enumerate(peers)]
        return local, sends, recvs

    def start(ins, outs, sems):
        local, sends, _ = copies(ins, outs, sems)
        for cp in local + sends:
            cp.start()

    def finish(ins, outs, sems):
        local, sends, recvs = copies(ins, outs, sems)
        for cp in sends:
            cp.wait_send()
        for cp in recvs:
            cp.wait_recv()
        for cp in local:
            cp.wait()

    return _Job(shards, [jax.ShapeDtypeStruct((N_DEV,) + a.shape, a.dtype) for a in shards],
                [pltpu.SemaphoreType.DMA((n, 4)), pltpu.SemaphoreType.DMA((n, 4)), pltpu.SemaphoreType.DMA((n,))],
                start, finish)


def _ag_pass(stacks):
    n = len(stacks)

    def copies(ins, outs, sems):
        send, recv = sems
        me, sibling, chips = _peers()
        sends = [_remote(_slot(ins[a], (*chip, me[2])), _slot(outs[a], (*chip, me[2])), send.at[a, j], recv.at[a, j],
                         sibling) for a in range(n) for j, chip in enumerate(chips)]
        recvs = [_remote(_slot(ins[a], (*chip, me[2])), _slot(outs[a], (*chip, 1 - me[2])), send.at[a, j],
                         recv.at[a, j], sibling) for a in range(n) for j, chip in enumerate(chips)]
        return sends, recvs

    def start(ins, outs, sems):
        for cp in copies(ins, outs, sems)[0]:
            cp.start()

    def finish(ins, outs, sems):
        sends, recvs = copies(ins, outs, sems)
        for cp in sends:
            cp.wait_send()
        for cp in recvs:
            cp.wait_recv()

    return _Job(stacks, [jax.ShapeDtypeStruct(a.shape, a.dtype) for a in stacks],
                [pltpu.SemaphoreType.DMA((n, 3)), pltpu.SemaphoreType.DMA((n, 3))],
                start, finish, aliases={a: a for a in range(n)})


def _rs_swap(parts):
    n = len(parts)

    def copies(ins, outs, sems):
        send, recv = sems
        me, sibling, _ = _peers()
        return [_remote(ins[a].at[1 - me[2]], outs[a], send.at[a], recv.at[a], sibling) for a in range(n)]

    def start(ins, outs, sems):
        for cp in copies(ins, outs, sems):
            cp.start()

    def finish(ins, outs, sems):
        for cp in copies(ins, outs, sems):
            cp.wait()

    return _Job(parts, [jax.ShapeDtypeStruct(a.shape[1:], a.dtype) for a in parts],
                [pltpu.SemaphoreType.DMA((n,)), pltpu.SemaphoreType.DMA((n,))], start, finish)


def _rs_ici(pairs):
    n = len(pairs)

    def copies(ins, outs, sems):
        send, recv = sems
        me, _, chips = _peers()
        return [_remote(ins[a].at[2 * chip[0] + chip[1]], outs[a].at[j], send.at[a, j], recv.at[a, j],
                        (*chip, me[2])) for a in range(n) for j, chip in enumerate(chips)]

    def start(ins, outs, sems):
        for cp in copies(ins, outs, sems):
            cp.start()

    def finish(ins, outs, sems):
        for cp in copies(ins, outs, sems):
            cp.wait()

    return _Job(pairs, [jax.ShapeDtypeStruct((3,) + a.shape[1:], a.dtype) for a in pairs],
                [pltpu.SemaphoreType.DMA((n, 3)), pltpu.SemaphoreType.DMA((n, 3))], start, finish)


def _comm_call(jobs, name):
    def body():
        pass

    return _pcall(body, name=name, grid=(), in_specs=[], out_specs=[], out_shape=[], operands=(), jobs=jobs)[1]


def _allgather(arrs, name):
    n = len(arrs)

    def body(*refs):
        ins, outs = refs[:n], refs[n:2 * n]
        send_sems, recv_sems, local_sems = refs[2 * n:]
        x, y, c = _mesh_pos()
        me, sibling = (x, y, c), (x, y, 1 - c)
        chips = [(1 - x, y), (x, 1 - y), (1 - x, 1 - y)]

        def slot(a, pos):
            return outs[a].at[4 * pos[0] + 2 * pos[1] + pos[2]]

        def copy(a, k, block, to, src=None):
            return pltpu.make_async_remote_copy(
                src_ref=slot(a, block) if src is None else src, dst_ref=slot(a, block),
                send_sem=send_sems.at[a, k], recv_sem=recv_sems.at[a, k],
                device_id=to, device_id_type=MESH)

        mine = [pltpu.make_async_copy(ins[a], slot(a, me), local_sems.at[a]) for a in range(n)]
        for cp in mine:
            cp.start()
        first = []
        for a in range(n):
            first.append(copy(a, 0, me, sibling, src=ins[a]))
            first += [copy(a, 1 + j, me, (*chip, c), src=ins[a]) for j, chip in enumerate(chips)]
        for cp in first:
            cp.start()
        passed = []
        for j, chip in enumerate(chips):
            for a in range(n):
                copy(a, 1 + j, (*chip, c), me).wait_recv()
                fwd = copy(a, 4 + j, (*chip, c), sibling)
                fwd.start()
                passed.append(fwd)
        for a in range(n):
            copy(a, 0, sibling, me).wait_recv()
            for j, chip in enumerate(chips):
                copy(a, 4 + j, (*chip, 1 - c), me).wait_recv()
        for cp in first + passed:
            cp.wait_send()
        for cp in mine:
            cp.wait()

    return pl.pallas_call(
        body, name=name,
        in_specs=[_any_spec()] * n, out_specs=[_any_spec()] * n,
        out_shape=[jax.ShapeDtypeStruct((N_DEV,) + a.shape, a.dtype) for a in arrs],
        scratch_shapes=[pltpu.SemaphoreType.DMA((n, 7)), pltpu.SemaphoreType.DMA((n, 7)),
                        pltpu.SemaphoreType.DMA((n,))],
        compiler_params=_cp(),
    )(*arrs)


W_IN_SHARD = 354
W_IN_COLS = ((0, 192, OQ), (192, 192, OKK), (384, 384, OV), (768, 384, OG), (1152, 16, OLR), (1168, 512, OCU),
             (1680, 384, OAQ), (2064, 384, OAK), (2448, 384, OAV))


def _w_in_padded(stack):
    new_to_ref = {new: (start, width) for start, width, new in W_IN_COLS}
    cols = []
    for new, padded in IN_GROUPS:
        start, width = new_to_ref[new]
        a = start
        while a < start + width:
            j = a // W_IN_SHARD
            b = min(start + width, (j + 1) * W_IN_SHARD)
            cols.append(stack[j][:, a - j * W_IN_SHARD:b - j * W_IN_SHARD])
            a = b
        if padded > width:
            cols.append(jnp.zeros((stack.shape[1], padded - width), stack.dtype))
    return jnp.concatenate(cols, axis=1)


def _dw_in_shards(dw):
    shards = []
    for j in range(N_DEV):
        lo, hi = j * W_IN_SHARD, (j + 1) * W_IN_SHARD
        segs = []
        for start, width, new in W_IN_COLS:
            a, b = max(lo, start), min(hi, start + width)
            if a < b:
                segs.append(dw[:, new + a - start:new + b - start])
        shards.append(jnp.concatenate(segs, axis=1))
    return jnp.stack([jnp.stack([shards[2 * chip + core] for chip in range(4)]) for core in range(2)])


def _pad_to(a, shape):
    return jnp.pad(a, [(0, s - d) for d, s in zip(a.shape, shape)])


SMALL_LAYOUT = (
    ("norm_mix", 2, 1024), ("norm_ffn", 2, 1024), ("norm_final", 1, 1024), ("gla_norm", 2, 384),
    ("b_gla_gate", 2, 192), ("b_dw", 2, 256), ("conv_ln_g", 2, 256), ("conv_ln_b", 2, 256),
    ("rel_bias", 12, 257), ("w_gla_gate", 32, 192), ("w_dw", 62, 256),
)
SMALL_LANES = 128
SMALL_TILE = 8 * SMALL_LANES


def _small_tile_rows(r, lanes):
    return -(-(r * lanes) // SMALL_TILE) * 8


SMALL_ROWS = sum(_small_tile_rows(r, lanes) for _, r, lanes in SMALL_LAYOUT)


def _pack_small(parts):
    tiles = []
    for name, r, lanes in SMALL_LAYOUT:
        rows = _small_tile_rows(r, lanes)
        flat = _pad_to(parts[name].reshape(r * lanes), (rows * SMALL_LANES,))
        tiles.append(flat.reshape(rows, SMALL_LANES))
    return jnp.concatenate(tiles, axis=0)


def _unpack_small(packed):
    out, r0 = {}, 0
    for name, r, lanes in SMALL_LAYOUT:
        rows = _small_tile_rows(r, lanes)
        out[name] = packed[r0:r0 + rows].reshape(rows * SMALL_LANES)[:r * lanes].reshape(r, lanes)
        r0 += rows
    return out


def _mixers_fwd(h, wl, w_in_p, plan=None):
    plan, res = plan or {}, {}

    def jobs(host):
        return plan[host](res) if host in plan else ()

    (q, k, v, g, cu, aq, ak, av, lr), res["inproj"] = _inproj_fwd(h, wl["norm_mix"], w_in_p, jobs=jobs("inproj"))
    bias = _relbias_expand(wl["rb"])
    kpad = jnp.pad(ak, ((AK_PAD, 0), (0, 0)))
    vpad = jnp.pad(av, ((AK_PAD, 0), (0, 0)))
    (o_att,), res["att"] = _att_fwd(aq, kpad, vpad, bias, jobs=jobs("att"))
    (o_gla, states), res["gla"] = _gla_fwd(q, k, v, g, lr, wl["wg"], wl["bg"], wl["gn"], jobs=jobs("gla"))
    (o_conv,), res["conv"] = _conv_fwd(cu, wl["w_dw"], wl["b_dw"], wl["ln_g"], wl["ln_b"], jobs=jobs("conv"))
    sv = dict(h=h, w_in=w_in_p, q=q, k=k, v=v, g=g, cu=cu, aq=aq, kpad=kpad, vpad=vpad, lr=lr,
              o_gla=o_gla, o_conv=o_conv, o_att=o_att, states=states, bias=bias)
    return sv, res


def _mixers_bwd(sv, wl, dh1, d_ogla, d_oconv, att_grads, conv_jobs=()):
    daq, dak, dav, dbias = att_grads
    d_rb = _relbias_grad(dbias)
    (dcu, dw_dw, db_dw, dln_g, dln_b), conv_res = _conv_bwd(
        sv["cu"], d_oconv, wl["w_dw"], wl["b_dw"], wl["ln_g"], wl["ln_b"], jobs=conv_jobs)
    dq, dk, dv, dg, dlr, dwg, dbg, dgn = _gla_bwd(sv["q"], sv["k"], sv["v"], sv["g"], sv["lr"], sv["states"],
                                                  d_ogla, wl["wg"], wl["bg"], wl["gn"])
    dh, dw_in, d_nmix = _inproj_bwd(sv["h"], dh1, wl["norm_mix"], sv["w_in"],
                                    (dq, dk, dv, dg, dcu, daq, dak, dav, dlr))
    small = dict(norm_mix=d_nmix, wg=dwg, bg=dbg, gn=dgn, w_dw=dw_dw, b_dw=db_dw, ln_g=dln_g, ln_b=dln_b, rb=d_rb)
    return dh, dw_in, small, conv_res


def _layer_small(l, w_dw_full, norm_mix, w_gla_gate, b_gla_gate, gla_norm, b_dw, conv_ln_g, conv_ln_b, rel_bias,
                 norm_ffn):
    return dict(
        norm_mix=norm_mix[l][None, :], norm_ffn=norm_ffn[l][None, :],
        wg=_pad_to(w_gla_gate[l], (128, 256)).astype(BF16), bg=_pad_to(b_gla_gate[l][None, :], (1, 256)),
        gn=gla_norm[l][None, :], w_dw=_pad_to(w_dw_full, (32, 256)), b_dw=b_dw[l][None, :],
        ln_g=conv_ln_g[l][None, :], ln_b=conv_ln_b[l][None, :], rb=_pad_to(rel_bias[l], (8, 384)))


RS_ROWS = dict(w_in=512, w_out=128, w_up=512, w_down=256)


def kernel(x, norm_mix, w_in, w_gla_gate, b_gla_gate, gla_norm, w_dw, b_dw, conv_ln_g, conv_ln_b, rel_bias, w_out, norm_ffn, w_up, w_down, norm_final, loss_target, m_norm_mix, m_w_in, m_w_gla_gate, m_b_gla_gate, m_gla_norm, m_w_dw, m_b_dw, m_conv_ln_g, m_conv_ln_b, m_rel_bias, m_w_out, m_norm_ffn, m_w_up, m_w_down, m_norm_final, v_norm_mix, v_w_in, v_w_gla_gate, v_b_gla_gate, v_gla_norm, v_w_dw, v_b_dw, v_conv_ln_g, v_conv_ln_b, v_rel_bias, v_w_out, v_norm_ffn, v_w_up, v_w_down, v_norm_final):
    mx, my, mc = _mesh_pos()
    me = 4 * mx + 2 * my + mc
    chip_idx = (2 * mx + my).astype(jnp.int32).reshape(1)
    core_idx = mc.astype(jnp.int32).reshape(1)
    x0, target = x[0], loss_target[0]

    def pair_sums(parts, r1):
        return [_pair_sum(p, r, core_idx, p.shape[2]) for p, r in zip(parts, r1)]

    sh = [dict(w_in=w_in[l].astype(BF16), w_out=w_out[l].astype(BF16), w_up=w_up[l].astype(BF16),
               w_down=w_down[l].astype(BF16)) for l in range(DEPTH)]
    dw_flat = _pad_to(w_dw, (DEPTH, 32, 32)).reshape(16, 128)
    st_in0, st_dw = _allgather([sh[0]["w_in"], dw_flat], "allgather_first")
    dw_all = st_dw.reshape(N_DEV, DEPTH, 32, 32)[:, :, :KCONV, :]
    dw_all = jnp.transpose(dw_all, (1, 2, 0, 3)).reshape(DEPTH, KCONV, 256)
    wl = [_layer_small(l, dw_all[l], norm_mix, w_gla_gate, b_gla_gate, gla_norm, b_dw, conv_ln_g, conv_ln_b,
                       rel_bias, norm_ffn) for l in range(DEPTH)]

    s0, s1 = sh[0], sh[1]
    sv0, g0 = _mixers_fwd(x0, wl[0], _w_in_padded(st_in0), plan=dict(
        inproj=lambda r: [_ag_spread([s0["w_out"], s1["w_in"]])],
        att=lambda r: [_ag_spread([s0["w_up"]]), _ag_pass(r["inproj"])],
        gla=lambda r: [_ag_spread([s0["w_down"]]), _ag_pass(r["att"][:1])],
        conv=lambda r: [_ag_spread([s1["w_out"]]), _ag_pass(r["gla"][:1])]))
    st_out0, st_in1 = g0["att"][1:]
    st_up0, st_down0 = g0["gla"][1], g0["conv"][1]
    wo0 = st_out0.reshape(D, D)
    (h1_0, xn2_0, h2_0), (up1_half, down1_half, st_out1) = _outproj_mlp_fwd(
        x0, sv0["o_gla"], sv0["o_conv"], sv0["o_att"], wo0, wl[0]["norm_ffn"], st_up0, st_down0,
        jobs=[_ag_spread([s1["w_up"], s1["w_down"]]), _ag_pass(g0["conv"][:1])])

    sv1, g1 = _mixers_fwd(h2_0, wl[1], _w_in_padded(st_in1), plan=dict(
        att=lambda r: [_ag_pass([up1_half, down1_half])]))
    st_up1, st_down1 = g1["att"]
    wo1 = st_out1.reshape(D, D)
    (h1_1, xn2_1, h2_1), _ = _outproj_mlp_fwd(
        h2_0, sv1["o_gla"], sv1["o_conv"], sv1["o_att"], wo1, wl[1]["norm_ffn"], st_up1, st_down1)

    loss8, dh, d_nf = _loss_fwd_bwd(h2_1, norm_final[None, :], target)
    loss = lax.psum(loss8[0, 0], ("x", "y", "c"))

    def layer_bwd(dh, sv, wl_l, xn2, h1, wo, st_up, st_down, mlp_jobs):
        (dh1, dw_up, dw_down, d_nffn), mlp_res = _mlp_bwd(xn2, h1, dh, wl_l["norm_ffn"], st_up, st_down,
                                                           jobs=mlp_jobs)
        ud = [dw_up, dw_down]
        (d_ogla, d_oconv, d_oatt, dw_out), r1 = _outproj_bwd(
            dh1, sv["o_gla"], sv["o_conv"], sv["o_att"], wo, jobs=[_rs_swap(ud)])
        pair_ud = pair_sums(ud, r1)
        att_grads, r = _att_bwd(sv["aq"], sv["kpad"], sv["vpad"], sv["bias"], d_oatt,
                                jobs=[_rs_ici(pair_ud), _rs_swap([dw_out])])
        r2_ud, r1_out = r[:2], r[2:]
        pair_out = pair_sums([dw_out], r1_out)
        dh_in, dw_in, small, r2_out = _mixers_bwd(sv, wl_l, dh1, d_ogla, d_oconv, att_grads,
                                                   conv_jobs=[_rs_ici(pair_out)])
        small["norm_ffn"] = d_nffn
        sums = dict(w_out=(pair_out[0], r2_out[0]), w_up=(pair_ud[0], r2_ud[0]), w_down=(pair_ud[1], r2_ud[1]))
        return dh_in, dw_in, small, sums, mlp_res

    dh, dw_in1, small1, sums1, _ = layer_bwd(dh, sv1, wl[1], xn2_1, h1_1, wo1, st_up1, st_down1, ())
    in1 = [_dw_in_shards(dw_in1)]
    pair_in1 = pair_sums(in1, _comm_call([_rs_swap(in1)], "rs_swap_w_in_1"))
    dx, dw_in0, small0, sums0, r2_in1 = layer_bwd(dh, sv0, wl[0], xn2_0, h1_0, wo0, st_up0, st_down0,
                                                  [_rs_ici(pair_in1)])
    sums1["w_in"] = (pair_in1[0], r2_in1[0])

    in0 = [_dw_in_shards(dw_in0)]
    pair_in0 = pair_sums(in0, _comm_call([_rs_swap(in0)], "rs_swap_w_in_0"))
    sums0["w_in"] = (pair_in0[0], _comm_call([_rs_ici(pair_in0)], "rs_ici_w_in_0")[0])

    big_w = dict(w_in=(w_in, m_w_in, v_w_in), w_out=(w_out, m_w_out, v_w_out), w_up=(w_up, m_w_up, v_w_up),
                 w_down=(w_down, m_w_down, v_w_down))
    pairs = {1: sums1, 0: sums0}
    big_out = {}
    for name, (w_, m_, v_) in big_w.items():
        res = None
        for l in (1, 0):
            a_own, r2_ = pairs[l][name]
            res = _rs_adamw(a_own, r2_, w_, m_, v_, l, chip_idx, RS_ROWS[name], prev=res)
        big_out[name] = res

    grads = (small0, small1)
    parts = dict(
        norm_mix=jnp.concatenate([grads[l]["norm_mix"] for l in range(DEPTH)], axis=0),
        norm_ffn=jnp.concatenate([grads[l]["norm_ffn"] for l in range(DEPTH)], axis=0),
        norm_final=d_nf,
        gla_norm=jnp.concatenate([grads[l]["gn"] for l in range(DEPTH)], axis=0),
        b_gla_gate=jnp.concatenate([grads[l]["bg"][:, :192] for l in range(DEPTH)], axis=0),
        b_dw=jnp.concatenate([grads[l]["b_dw"] for l in range(DEPTH)], axis=0),
        conv_ln_g=jnp.concatenate([grads[l]["ln_g"] for l in range(DEPTH)], axis=0),
        conv_ln_b=jnp.concatenate([grads[l]["ln_b"] for l in range(DEPTH)], axis=0),
        rel_bias=jnp.concatenate([grads[l]["rb"][:6, :N_REL] for l in range(DEPTH)], axis=0),
        w_gla_gate=jnp.concatenate([grads[l]["wg"][:16, :192] for l in range(DEPTH)], axis=0),
        w_dw=jnp.concatenate([grads[l]["w_dw"][:KCONV] for l in range(DEPTH)], axis=0),
    )
    small_all = _allgather([_pack_small(parts)], "allgather_small")[0]
    sg = _unpack_small(_small_sum(small_all))
    dw_grad = lax.dynamic_slice_in_dim(sg["w_dw"].reshape(DEPTH, KCONV, 256), me * 32, 32, axis=2)
    small_g = dict(
        norm_mix=sg["norm_mix"], w_gla_gate=sg["w_gla_gate"].reshape(DEPTH, 16, 192), b_gla_gate=sg["b_gla_gate"],
        gla_norm=sg["gla_norm"], w_dw=dw_grad, b_dw=sg["b_dw"], conv_ln_g=sg["conv_ln_g"],
        conv_ln_b=sg["conv_ln_b"], rel_bias=sg["rel_bias"].reshape(DEPTH, 6, N_REL), norm_ffn=sg["norm_ffn"],
        norm_final=sg["norm_final"].reshape(D))
    small_names = ("norm_mix", "w_gla_gate", "b_gla_gate", "gla_norm", "w_dw", "b_dw", "conv_ln_g", "conv_ln_b",
                   "rel_bias", "norm_ffn", "norm_final")
    small_w = dict(norm_mix=norm_mix, w_gla_gate=w_gla_gate, b_gla_gate=b_gla_gate, gla_norm=gla_norm, w_dw=w_dw,
                   b_dw=b_dw, conv_ln_g=conv_ln_g, conv_ln_b=conv_ln_b, rel_bias=rel_bias, norm_ffn=norm_ffn,
                   norm_final=norm_final)
    small_m = dict(norm_mix=m_norm_mix, w_gla_gate=m_w_gla_gate, b_gla_gate=m_b_gla_gate, gla_norm=m_gla_norm,
                   w_dw=m_w_dw, b_dw=m_b_dw, conv_ln_g=m_conv_ln_g, conv_ln_b=m_conv_ln_b, rel_bias=m_rel_bias,
                   norm_ffn=m_norm_ffn, norm_final=m_norm_final)
    small_v = dict(norm_mix=v_norm_mix, w_gla_gate=v_w_gla_gate, b_gla_gate=v_b_gla_gate, gla_norm=v_gla_norm,
                   w_dw=v_w_dw, b_dw=v_b_dw, conv_ln_g=v_conv_ln_g, conv_ln_b=v_conv_ln_b, rel_bias=v_rel_bias,
                   norm_ffn=v_norm_ffn, norm_final=v_norm_final)
    s_delta, s_m, s_v = _adamw_small([small_w[n] for n in small_names], [small_g[n] for n in small_names],
                                     [small_m[n] for n in small_names], [small_v[n] for n in small_names])
    s_idx = {n: t for t, n in enumerate(small_names)}

    order = ("norm_mix", "w_in", "w_gla_gate", "b_gla_gate", "gla_norm", "w_dw", "b_dw", "conv_ln_g", "conv_ln_b",
             "rel_bias", "w_out", "norm_ffn", "w_up", "w_down", "norm_final")

    def pick(kind, name):
        if name in big_out:
            return big_out[name][kind]
        t = s_idx[name]
        return (small_g[name], s_delta[t], s_m[t], s_v[t])[kind]

    outs = [loss, dx[None]]
    for kind in range(4):
        outs += [pick(kind, n) for n in order]
    return tuple(outs)
```

```python
import functools

import jax
import jax.numpy as jnp
from jax import lax
from jax.experimental import pallas as pl
from jax.experimental.pallas import tpu as pltpu

F32 = jnp.float32
BF16 = jnp.bfloat16
MESH = pl.DeviceIdType.MESH

D = 1024
DEPTH = 2
CH = 64
EPS = 1e-6
NEG = -1e30
N_DEV = 8
N_REL = 257
Q_SCALE = 48.0 ** -0.5
A_SCALE = 64.0 ** -0.5
GATE_TAU = 16.0
KCONV = 31

OQ, OKK, OV, OG, OCU, OAQ, OAK, OAV, OLR, DINP = 0, 256, 512, 896, 1280, 1792, 2176, 2560, 2944, 3072
IN_GROUPS = ((OQ, 256), (OKK, 256), (OV, 384), (OG, 384), (OCU, 512), (OAQ, 384), (OAK, 384), (OAV, 384), (OLR, 128))

AQ_BLK = 256
AK_WIN = 768
WIN_LEFT = 2
RB_W = 1536

ADAM_LR, ADAM_B1, ADAM_B2, ADAM_EPS, ADAM_WD, ADAM_STEP = 0.001, 0.9, 0.999, 1e-08, 0.01, 10


V7X_VMEM_MIB = 64
VMEM_LIMIT_MIB = V7X_VMEM_MIB - 1


def _cp(sem=None):
    kw = {"vmem_limit_bytes": VMEM_LIMIT_MIB * 1024 * 1024}
    if sem is not None:
        kw["dimension_semantics"] = sem
    return pltpu.CompilerParams(**kw)


def _dot(a, b):
    return jnp.dot(a, b, preferred_element_type=F32)


def _dot_nt(a, b):
    return lax.dot_general(a, b, (((1,), (1,)), ((), ())), preferred_element_type=F32)


def _dot_tn(a, b):
    return lax.dot_general(a, b, (((0,), (0,)), ((), ())), preferred_element_type=F32)


def _split2(a):
    hi = a.astype(BF16)
    lo = (a - hi.astype(F32)).astype(BF16)
    return hi, lo


def _split3(a):
    hi = a.astype(BF16)
    r1 = a - hi.astype(F32)
    mid = r1.astype(BF16)
    lo = (r1 - mid.astype(F32)).astype(BF16)
    return hi, mid, lo


def _sigmoid(x):
    return 1.0 / (1.0 + jnp.exp(-x))


def _group(idx, size, n):
    g = jnp.zeros_like(idx)
    for t in range(1, n):
        g = g + (idx >= t * size).astype(jnp.int32)
    return g


def _rms_bwd(dy, x, r, gamma):
    xh = x * r
    dxh = dy * gamma
    dx = r * (dxh - xh * jnp.mean(dxh * xh, axis=-1, keepdims=True))
    return dx, jnp.sum(dy * xh, axis=0, keepdims=True)


def _row_spec(tm, n):
    return pl.BlockSpec((tm, n), lambda i: (i, 0))


def _full_spec(shape):
    nd = len(shape)
    return pl.BlockSpec(shape, lambda *_: (0,) * nd)


def _any_spec():
    return pl.BlockSpec(memory_space=pl.ANY)


class _Job:
    def __init__(self, operands, out_shapes, sems, start, finish, aliases=None):
        self.operands, self.out_shapes, self.sems = list(operands), list(out_shapes), list(sems)
        self.start, self.finish, self.aliases = start, finish, dict(aliases or {})


def _pcall(body, *, name, grid, in_specs, out_specs, out_shape, operands, scratch_shapes=(), sem=None, jobs=()):
    jobs = list(jobs)
    in_specs, out_specs, out_shape = list(in_specs), list(out_specs), list(out_shape)
    scratch_shapes = list(scratch_shapes)
    n_in, n_out, n_scr = len(in_specs), len(out_specs), len(scratch_shapes)
    j_in = [a for j in jobs for a in j.operands]
    j_out = [s for j in jobs for s in j.out_shapes]
    j_sem = [s for j in jobs for s in j.sems]
    aliases, io, oo = {}, n_in, n_out
    for j in jobs:
        for a, b in j.aliases.items():
            aliases[io + a] = oo + b
        io += len(j.operands)
        oo += len(j.out_shapes)

    def wrapped(*refs):
        own_in, ji = refs[:n_in], refs[n_in:n_in + len(j_in)]
        o0 = n_in + len(j_in)
        own_out, jo = refs[o0:o0 + n_out], refs[o0 + n_out:o0 + n_out + len(j_out)]
        s0 = o0 + n_out + len(j_out)
        own_scr, js = refs[s0:s0 + n_scr], refs[s0 + n_scr:]

        def each_job(fn_name):
            a = b = c = 0
            for j in jobs:
                na, nb, nc = len(j.operands), len(j.out_shapes), len(j.sems)
                getattr(j, fn_name)(ji[a:a + na], jo[b:b + nb], js[c:c + nc])
                a, b, c = a + na, b + nb, c + nc

        if jobs and grid:
            pids = [pl.program_id(d) for d in range(len(grid))]
            first = functools.reduce(jnp.logical_and, [p == 0 for p in pids])
            last = functools.reduce(jnp.logical_and, [p == g - 1 for p, g in zip(pids, grid)])
            pl.when(first)(lambda: each_job("start"))
        elif jobs:
            each_job("start")

        body(*own_in, *own_out, *own_scr)

        if jobs and grid:
            pl.when(last)(lambda: each_job("finish"))
        elif jobs:
            each_job("finish")

    res = pl.pallas_call(
        wrapped, name=name, grid=grid,
        in_specs=in_specs + [_any_spec()] * len(j_in), out_specs=out_specs + [_any_spec()] * len(j_out),
        out_shape=out_shape + j_out, scratch_shapes=scratch_shapes + j_sem,
        input_output_aliases=aliases, compiler_params=_cp(sem),
    )(*operands, *j_in)
    return res[:n_out], res[n_out:]


ATT_HEADS = 6
HEAD_PAD = 128
ATT_WIDE = ATT_HEADS * HEAD_PAD
ATT_GROUP_OFFS = (OAQ, OAK, OAV)


def _store_head_padded(o_ref, part):
    o_ref[...] = jnp.zeros_like(o_ref)
    for hd in range(ATT_HEADS):
        o_ref[:, hd * HEAD_PAD:hd * HEAD_PAD + 64] = part[:, hd * 64:(hd + 1) * 64]


def _inproj_fwd(h, gamma, w, jobs=()):
    T = h.shape[0]
    tm = 512

    def body(h_ref, g_ref, w_ref, *outs):
        x = h_ref[...]
        r = lax.rsqrt(jnp.mean(x * x, axis=-1, keepdims=True) + EPS)
        xn = (x * r * g_ref[...]).astype(BF16)
        p = _dot(xn, w_ref[...])
        for o_ref, (off, n) in zip(outs, IN_GROUPS):
            part = p[:, off:off + n].astype(BF16)
            if off in ATT_GROUP_OFFS:
                _store_head_padded(o_ref, part)
            else:
                o_ref[...] = part

    widths = [ATT_WIDE if off in ATT_GROUP_OFFS else n for off, n in IN_GROUPS]
    return _pcall(
        body, name="inproj_fwd", grid=(T // tm,),
        in_specs=[_row_spec(tm, D), _full_spec((1, D)), _full_spec((D, DINP))],
        out_specs=[_row_spec(tm, n) for n in widths],
        out_shape=[jax.ShapeDtypeStruct((T, n), BF16) for n in widths],
        sem=("arbitrary",), operands=(h, gamma, w), jobs=jobs)


def _inproj_bwd(h, dh_in, gamma, w, dparts):
    T = h.shape[0]
    tm = 256
    nt = T // tm

    def body(h_ref, dhin_ref, g_ref, w_ref, *rest):
        dp_refs = rest[:9]
        dh_ref, dw_ref, dg_ref, acc = rest[9:]
        i = pl.program_id(0)

        @pl.when(i == 0)
        def _():
            acc[...] = jnp.zeros_like(acc)
            dg_ref[...] = jnp.zeros_like(dg_ref)

        x = h_ref[...]
        r = lax.rsqrt(jnp.mean(x * x, axis=-1, keepdims=True) + EPS)
        gamma_ = g_ref[...]
        xn = (x * r * gamma_).astype(BF16)
        dxn = jnp.zeros((tm, D), F32)
        for d_ref, (off, n) in zip(dp_refs, IN_GROUPS):
            d = d_ref[...]
            acc[:, off:off + n] += _dot_tn(xn, d)
            dxn = dxn + _dot_nt(d, w_ref[:, off:off + n])
        dx, dgam = _rms_bwd(dxn, x, r, gamma_)
        dh_ref[...] = dhin_ref[...] + dx
        dg_ref[...] += dgam

        @pl.when(i == nt - 1)
        def _():
            dw_ref[...] = acc[...].astype(BF16)

    return pl.pallas_call(
        body, name="inproj_bwd", grid=(nt,),
        in_specs=[_row_spec(tm, D), _row_spec(tm, D), _full_spec((1, D)), _full_spec((D, DINP))]
        + [_row_spec(tm, n) for _, n in IN_GROUPS],
        out_specs=[_row_spec(tm, D), _full_spec((D, DINP)), _full_spec((1, D))],
        out_shape=[jax.ShapeDtypeStruct((T, D), F32), jax.ShapeDtypeStruct((D, DINP), BF16),
                   jax.ShapeDtypeStruct((1, D), F32)],
        scratch_shapes=[pltpu.VMEM((D, DINP), F32)],
        compiler_params=_cp(("arbitrary",)),
    )(h, dh_in, gamma, w, *dparts)


GLA_ROWS = 512
GLA_NC = GLA_ROWS // CH


def _gla_consts():
    ri = lax.broadcasted_iota(jnp.int32, (CH, CH), 0)
    ci = lax.broadcasted_iota(jnp.int32, (CH, CH), 1)
    upper = (ci > ri).astype(BF16)
    vv = lax.broadcasted_iota(jnp.int32, (384, 256), 0)
    kk = lax.broadcasted_iota(jnp.int32, (384, 256), 1)
    mask_t = ((_group(vv, 96, 4) == _group(kk, 48, 4)) & (kk < 192)).astype(F32)
    pi = lax.broadcasted_iota(jnp.int32, (384, 384), 0)
    pj = lax.broadcasted_iota(jnp.int32, (384, 384), 1)
    same_head = (_group(pi, 96, 4) == _group(pj, 96, 4)).astype(BF16)
    return upper, mask_t, same_head


def _gla_gate(lr_ref, wg_ref, bg_ref):
    z = _dot(lr_ref[...], wg_ref[...]) + bg_ref[...]
    la = (jnp.minimum(z, 0.0) - jnp.log(1.0 + jnp.exp(-jnp.abs(z)))) * (1.0 / GATE_TAU)
    return z, la


def _gla_chunk_decay(la_c, upper):
    hi, lo = _split2(la_c)
    dec = _dot(upper, hi) + _dot(upper, lo)
    end = jnp.sum(la_c, axis=0, keepdims=True)
    return jnp.exp(dec), jnp.exp(end)


def _head_mean(x, same_head):
    hi, lo = _split2(x)
    return (_dot(hi, same_head) + _dot(lo, same_head)) * (1.0 / 96.0)


def _gla_fwd(q, k, v, g, lr, wg, bg, gn, jobs=()):
    T = q.shape[0]
    nb = T // GLA_ROWS

    def body(q_ref, k_ref, v_ref, g_ref, lr_ref, wg_ref, bg_ref, gn_ref, y_ref, st_ref, s_scr, o_scr):
        upper, mask_t, same_head = _gla_consts()

        @pl.when(pl.program_id(0) == 0)
        def _():
            s_scr[...] = jnp.zeros_like(s_scr)

        _, la = _gla_gate(lr_ref, wg_ref, bg_ref)
        for c in range(GLA_NC):
            rs = slice(c * CH, (c + 1) * CH)
            w, a = _gla_chunk_decay(la[rs], upper)
            kd = (k_ref[rs, :].astype(F32) * w).astype(BF16)
            kv_t = _dot_tn(v_ref[rs, :], kd)
            s_new = s_scr[...] * a + kv_t * mask_t
            s_scr[...] = s_new
            sb = s_new.astype(BF16)
            st_ref[c] = sb
            qs = (q_ref[rs, :].astype(F32) * Q_SCALE).astype(BF16)
            o_scr[rs, :] = _dot_nt(qs, sb)
        o = o_scr[...]
        r = lax.rsqrt(_head_mean(o * o, same_head) + EPS)
        gf = g_ref[...].astype(F32)
        y_ref[...] = (o * r * gn_ref[...] * (gf * _sigmoid(gf))).astype(BF16)

    return _pcall(
        body, name="gla_fwd", grid=(nb,),
        in_specs=[_row_spec(GLA_ROWS, 256), _row_spec(GLA_ROWS, 256), _row_spec(GLA_ROWS, 384),
                  _row_spec(GLA_ROWS, 384), _row_spec(GLA_ROWS, 128),
                  _full_spec((128, 256)), _full_spec((1, 256)), _full_spec((1, 384))],
        out_specs=[_row_spec(GLA_ROWS, 384), pl.BlockSpec((GLA_NC, 384, 256), lambda i: (i, 0, 0))],
        out_shape=[jax.ShapeDtypeStruct((T, 384), BF16), jax.ShapeDtypeStruct((T // CH, 384, 256), BF16)],
        scratch_shapes=[pltpu.VMEM((384, 256), F32), pltpu.VMEM((GLA_ROWS, 384), F32)],
        sem=("arbitrary",), operands=(q, k, v, g, lr, wg, bg, gn), jobs=jobs)


def _gla_bwd(q, k, v, g, lr, states, dy, wg, bg, gn):
    T = q.shape[0]
    nb = T // GLA_ROWS

    def rev(s):
        return nb - 1 - s

    def body(q_ref, k_ref, v_ref, g_ref, lr_ref, st_ref, stp_ref, dy_ref, wg_ref, bg_ref, gn_ref,
             dq_ref, dk_ref, dv_ref, dg_ref, dlr_ref, dwg_ref, dbg_ref, dgn_ref,
             d_scr, an_scr, o_scr, do_scr, dla_scr):
        upper, mask_t, same_head = _gla_consts()
        s = pl.program_id(0)
        blk = rev(s)

        @pl.when(s == 0)
        def _():
            d_scr[...] = jnp.zeros_like(d_scr)
            an_scr[...] = jnp.zeros_like(an_scr)
            dwg_ref[...] = jnp.zeros_like(dwg_ref)
            dbg_ref[...] = jnp.zeros_like(dbg_ref)
            dgn_ref[...] = jnp.zeros_like(dgn_ref)

        z, la = _gla_gate(lr_ref, wg_ref, bg_ref)
        ws, as_, qss, kds = [], [], [], []
        for c in range(GLA_NC):
            rs = slice(c * CH, (c + 1) * CH)
            w, a = _gla_chunk_decay(la[rs], upper)
            ws.append(w)
            as_.append(a)
            qs = (q_ref[rs, :].astype(F32) * Q_SCALE).astype(BF16)
            qss.append(qs)
            kds.append((k_ref[rs, :].astype(F32) * w).astype(BF16))
            o_scr[rs, :] = _dot_nt(qs, st_ref[c])
        o = o_scr[...]
        r = lax.rsqrt(_head_mean(o * o, same_head) + EPS)
        on = o * r
        gf = g_ref[...].astype(F32)
        sg = _sigmoid(gf)
        si = gf * sg
        dyf = dy_ref[...].astype(F32)
        gn_ = gn_ref[...]
        dgn_ref[...] += jnp.sum(dyf * si * on, axis=0, keepdims=True)
        dg_ref[...] = (dyf * on * gn_ * (sg * (1.0 + gf * (1.0 - sg)))).astype(BF16)
        d_on = dyf * si * gn_
        do_scr[...] = r * (d_on - on * _head_mean(d_on * on, same_head))

        first = (blk > 0).astype(F32)
        for c in reversed(range(GLA_NC)):
            rs = slice(c * CH, (c + 1) * CH)
            dob = do_scr[rs, :].astype(BF16)
            sb = st_ref[c]
            if c > 0:
                s_prev = st_ref[c - 1].astype(F32)
            else:
                s_prev = stp_ref[0].astype(F32) * first
            dq_ref[rs, :] = (_dot(dob, sb) * Q_SCALE).astype(BF16)
            dt = d_scr[...] * an_scr[...] + _dot_tn(dob, qss[c]) * mask_t
            d_scr[...] = dt
            da = jnp.sum(dt * s_prev, axis=0, keepdims=True)
            db = dt.astype(BF16)
            dkd = _dot(v_ref[rs, :], db)
            dv_ref[rs, :] = _dot_nt(kds[c], db).astype(BF16)
            dk_ref[rs, :] = (dkd * ws[c]).astype(BF16)
            ddec = dkd * k_ref[rs, :].astype(F32) * ws[c]
            hi, lo = _split2(ddec)
            dla_scr[rs, :] = _dot_tn(upper, hi) + _dot_tn(upper, lo) + as_[c] * da
            an_scr[...] = as_[c]

        dz = dla_scr[...] * (1.0 - _sigmoid(z)) * (1.0 / GATE_TAU)
        dzb = dz.astype(BF16)
        dlr_ref[...] = _dot_nt(dzb, wg_ref[...]).astype(BF16)
        dwg_ref[...] += _dot_tn(lr_ref[...], dzb)
        dbg_ref[...] += jnp.sum(dz, axis=0, keepdims=True)

    def rspec(n):
        return pl.BlockSpec((GLA_ROWS, n), lambda s: (rev(s), 0))

    return pl.pallas_call(
        body, name="gla_bwd", grid=(nb,),
        in_specs=[rspec(256), rspec(256), rspec(384), rspec(384), rspec(128),
                  pl.BlockSpec((GLA_NC, 384, 256), lambda s: (rev(s), 0, 0)),
                  pl.BlockSpec((1, 384, 256), lambda s: (jnp.maximum(rev(s) * GLA_NC - 1, 0), 0, 0)),
                  rspec(384), _full_spec((128, 256)), _full_spec((1, 256)), _full_spec((1, 384))],
        out_specs=[rspec(256), rspec(256), rspec(384), rspec(384), rspec(128),
                   _full_spec((128, 256)), _full_spec((1, 256)), _full_spec((1, 384))],
        out_shape=[jax.ShapeDtypeStruct((T, 256), BF16), jax.ShapeDtypeStruct((T, 256), BF16),
                   jax.ShapeDtypeStruct((T, 384), BF16), jax.ShapeDtypeStruct((T, 384), BF16),
                   jax.ShapeDtypeStruct((T, 128), BF16),
                   jax.ShapeDtypeStruct((128, 256), F32), jax.ShapeDtypeStruct((1, 256), F32),
                   jax.ShapeDtypeStruct((1, 384), F32)],
        scratch_shapes=[pltpu.VMEM((384, 256), F32), pltpu.VMEM((1, 256), F32),
                        pltpu.VMEM((GLA_ROWS, 384), F32), pltpu.VMEM((GLA_ROWS, 384), F32),
                        pltpu.VMEM((GLA_ROWS, 256), F32)],
        compiler_params=_cp(("arbitrary",)),
    )(q, k, v, g, lr, states, states, dy, wg, bg, gn)


CONV_ROWS = 512
HALO = 32
SUBL = 8
CONV_SLAB = 32
PHASE_ROWS = CONV_ROWS + HALO - SUBL
FWD_SHIFT = tuple(HALO - (KCONV - 1) + j for j in range(KCONV))
BWD_SHIFT = tuple(KCONV - 1 - j for j in range(KCONV))


def _fill_phases(buf, ph):
    for f in range(1, SUBL):
        ph[f, 0:PHASE_ROWS, :] = buf[pl.ds(f, PHASE_ROWS), :]


def _tap(buf, ph, shift, r, n):
    f, base = shift % SUBL, shift - shift % SUBL
    src = buf if f == 0 else ph.at[f]
    return src[pl.ds(base + r, n), :]


def _taps_apply(w_ref, buf, ph, shifts, out):
    for r in range(0, CONV_ROWS, CONV_SLAB):
        acc = jnp.zeros((CONV_SLAB, 256), F32)
        for j in range(KCONV):
            acc = acc + w_ref[j:j + 1, :] * _tap(buf, ph, shifts[j], r, CONV_SLAB)
        out[r:r + CONV_SLAB, :] = acc


def _conv_scratch():
    return [pltpu.VMEM((CONV_ROWS + HALO, 256), F32), pltpu.VMEM((SUBL, CONV_ROWS + HALO, 256), F32),
            pltpu.VMEM((CONV_ROWS, 256), F32)]


def _conv_common(cu_ref, halo_ref, w_ref, b_ref, lg_ref, lb_ref, buf, ph, cbuf, blk):
    u = cu_ref[...].astype(F32)
    a = u[:, :256]
    sb = _sigmoid(u[:, 256:])
    uh = halo_ref[...].astype(F32)
    hh = uh[:, :256] * _sigmoid(uh[:, 256:]) * (blk > 0).astype(F32)
    buf[0:HALO, :] = hh
    buf[HALO:HALO + CONV_ROWS, :] = a * sb
    _fill_phases(buf, ph)
    _taps_apply(w_ref, buf, ph, FWD_SHIFT, cbuf)
    cc = cbuf[...] + b_ref[...]
    mu = jnp.mean(cc, axis=-1, keepdims=True)
    xc = cc - mu
    rstd = lax.rsqrt(jnp.mean(xc * xc, axis=-1, keepdims=True) + EPS)
    n = xc * rstd
    yln = n * lg_ref[...] + lb_ref[...]
    return a, sb, n, rstd, yln


def _conv_fwd(cu, w, b, lg, lb, jobs=()):
    T = cu.shape[0]
    nb = T // CONV_ROWS
    per = CONV_ROWS // HALO

    def body(cu_ref, halo_ref, w_ref, b_ref, lg_ref, lb_ref, y_ref, buf, ph, cbuf):
        _, _, _, _, yln = _conv_common(cu_ref, halo_ref, w_ref, b_ref, lg_ref, lb_ref, buf, ph, cbuf,
                                       pl.program_id(0))
        y_ref[...] = (yln * _sigmoid(yln)).astype(BF16)

    return _pcall(
        body, name="conv_fwd", grid=(nb,),
        in_specs=[_row_spec(CONV_ROWS, 512),
                  pl.BlockSpec((HALO, 512), lambda i: (jnp.maximum(i * per - 1, 0), 0)),
                  _full_spec((32, 256)), _full_spec((1, 256)), _full_spec((1, 256)), _full_spec((1, 256))],
        out_specs=[_row_spec(CONV_ROWS, 256)],
        out_shape=[jax.ShapeDtypeStruct((T, 256), BF16)],
        scratch_shapes=_conv_scratch(),
        sem=("arbitrary",), operands=(cu, cu, w, b, lg, lb), jobs=jobs)


def _conv_bwd(cu, dy, w, b, lg, lb, jobs=()):
    T = cu.shape[0]
    nb = T // CONV_ROWS
    per = CONV_ROWS // HALO

    def rev(s):
        return nb - 1 - s

    def body(cu_ref, halo_ref, dy_ref, w_ref, b_ref, lg_ref, lb_ref,
             dcu_ref, dw_ref, db_ref, dlg_ref, dlb_ref, buf, ph, cbuf, dcbuf, dph, carry):
        s = pl.program_id(0)

        @pl.when(s == 0)
        def _():
            carry[...] = jnp.zeros_like(carry)
            dw_ref[...] = jnp.zeros_like(dw_ref)
            db_ref[...] = jnp.zeros_like(db_ref)
            dlg_ref[...] = jnp.zeros_like(dlg_ref)
            dlb_ref[...] = jnp.zeros_like(dlb_ref)

        a, sb, n, rstd, yln = _conv_common(cu_ref, halo_ref, w_ref, b_ref, lg_ref, lb_ref, buf, ph, cbuf, rev(s))
        sg = _sigmoid(yln)
        dyln = dy_ref[...].astype(F32) * (sg * (1.0 + yln * (1.0 - sg)))
        dlg_ref[...] += jnp.sum(dyln * n, axis=0, keepdims=True)
        dlb_ref[...] += jnp.sum(dyln, axis=0, keepdims=True)
        dn = dyln * lg_ref[...]
        dc = rstd * (dn - jnp.mean(dn, axis=-1, keepdims=True) - n * jnp.mean(dn * n, axis=-1, keepdims=True))
        db_ref[...] += jnp.sum(dc, axis=0, keepdims=True)
        dcbuf[0:CONV_ROWS, :] = dc
        dcbuf[CONV_ROWS:CONV_ROWS + HALO, :] = carry[...]
        carry[...] = dc[0:HALO, :]
        _fill_phases(dcbuf, dph)
        for j in range(KCONV):
            acc = jnp.zeros((SUBL, 256), F32)
            for r in range(0, CONV_ROWS, 2 * CONV_SLAB):
                prod = dcbuf[r:r + 2 * CONV_SLAB, :] * _tap(buf, ph, FWD_SHIFT[j], r, 2 * CONV_SLAB)
                acc = acc + jnp.sum(prod.reshape(2 * CONV_SLAB // SUBL, SUBL, 256), axis=0)
            dw_ref[j:j + 1, :] += jnp.sum(acc, axis=0, keepdims=True)
        _taps_apply(w_ref, dcbuf, dph, BWD_SHIFT, cbuf)
        dhg = cbuf[...]
        dcu_ref[...] = jnp.concatenate([dhg * sb, dhg * a * sb * (1.0 - sb)], axis=1).astype(BF16)

    def rspec(n):
        return pl.BlockSpec((CONV_ROWS, n), lambda s: (rev(s), 0))

    return _pcall(
        body, name="conv_bwd", grid=(nb,),
        in_specs=[rspec(512),
                  pl.BlockSpec((HALO, 512), lambda s: (jnp.maximum(rev(s) * per - 1, 0), 0)),
                  rspec(256),
                  _full_spec((32, 256)), _full_spec((1, 256)), _full_spec((1, 256)), _full_spec((1, 256))],
        out_specs=[rspec(512), _full_spec((32, 256)), _full_spec((1, 256)), _full_spec((1, 256)),
                   _full_spec((1, 256))],
        out_shape=[jax.ShapeDtypeStruct((T, 512), BF16), jax.ShapeDtypeStruct((32, 256), F32),
                   jax.ShapeDtypeStruct((1, 256), F32), jax.ShapeDtypeStruct((1, 256), F32),
                   jax.ShapeDtypeStruct((1, 256), F32)],
        scratch_shapes=_conv_scratch() + [pltpu.VMEM((CONV_ROWS + HALO, 256), F32),
                                          pltpu.VMEM((SUBL, CONV_ROWS + HALO, 256), F32),
                                          pltpu.VMEM((HALO, 256), F32)],
        sem=("arbitrary",), operands=(cu, cu, dy, w, b, lg, lb), jobs=jobs)


def _rel_onehot_t(shift=0):
    r = lax.broadcasted_iota(jnp.int32, (384, RB_W), 0)
    n = lax.broadcasted_iota(jnp.int32, (384, RB_W), 1) - shift
    idx = jnp.clip(1024 - n, -128, 128) + 128
    return (idx == r).astype(BF16)


def _relbias_expand(rb):
    def body(rb_ref, out_ref):
        oh = _rel_onehot_t()
        hi, mid, lo = _split3(rb_ref[...])
        strip = _dot(hi, oh) + _dot(mid, oh) + _dot(lo, oh)
        qi = _group(lax.broadcasted_iota(jnp.int32, (AQ_BLK, AK_WIN), 0), CH, 4)
        kj = _group(lax.broadcasted_iota(jnp.int32, (AQ_BLK, AK_WIN), 1), CH, 12)
        valid = (kj >= qi) & (kj <= qi + 8)
        for hd in range(6):
            x = jnp.broadcast_to(strip[hd:hd + 1, :], (AQ_BLK, RB_W))
            xr = pltpu.roll(x, 0, 1, stride=1, stride_axis=0)
            out_ref[hd] = jnp.where(valid, xr[:, 512:512 + AK_WIN], NEG)

    return pl.pallas_call(
        body, name="relbias_expand",
        out_shape=jax.ShapeDtypeStruct((6, AQ_BLK, AK_WIN), F32),
        compiler_params=_cp(),
    )(rb)


def _relbias_grad(dbias):
    def body(db_ref, out_ref):
        oh = _rel_onehot_t(AQ_BLK - 1)
        ri = lax.broadcasted_iota(jnp.int32, (AQ_BLK, AQ_BLK), 0)
        ci = lax.broadcasted_iota(jnp.int32, (AQ_BLK, AQ_BLK), 1)
        flip = (ri + ci == AQ_BLK - 1).astype(BF16)
        rows = []
        for hd in range(6):
            hi, mid, lo = _split3(db_ref[hd])
            rev = _dot(flip, hi) + _dot(flip, mid) + _dot(flip, lo)
            x = jnp.concatenate([jnp.zeros((AQ_BLK, 512), F32), rev,
                                 jnp.zeros((AQ_BLK, RB_W - 512 - AK_WIN), F32)], axis=1)
            xr = pltpu.roll(x, 0, 1, stride=1, stride_axis=0)
            rows.append(jnp.sum(xr, axis=0, keepdims=True))
        rows.append(jnp.zeros((2, RB_W), F32))
        dstrip = jnp.concatenate(rows, axis=0)
        hi, mid, lo = _split3(dstrip)
        out_ref[...] = _dot_nt(hi, oh) + _dot_nt(mid, oh) + _dot_nt(lo, oh)

    return pl.pallas_call(
        body, name="relbias_grad",
        out_shape=jax.ShapeDtypeStruct((8, 384), F32),
        compiler_params=_cp(),
    )(dbias)


ATT_SLAB = 8


def _att_softmax_slab(s_scr, b_ref, hd, rows, first_key):
    kvalid = lax.broadcasted_iota(jnp.int32, (ATT_SLAB, AK_WIN), 1) >= first_key
    s = jnp.where(kvalid, s_scr[rows, :] + b_ref[hd, rows, :], NEG)
    m = jnp.max(s, axis=-1, keepdims=True)
    p = jnp.exp(s - m)
    return p * (1.0 / jnp.sum(p, axis=-1, keepdims=True))


def _att_first_key(i):
    return (8 - 4 * i) * CH


def _slab_rows(t):
    return pl.ds(t * ATT_SLAB, ATT_SLAB)


def _head_lanes(hd):
    return slice(hd * 64, (hd + 1) * 64)


WIN_BLKS = AK_WIN // AQ_BLK


def _head_tile(hd):
    return slice(hd * HEAD_PAD, (hd + 1) * HEAD_PAD)


def _win_cols(d):
    return slice(d * AQ_BLK, (d + 1) * AQ_BLK)


def _win_block(i, d):
    return jnp.maximum(i + d - WIN_LEFT, 0)


def _win_specs():
    return [pl.BlockSpec((AQ_BLK, ATT_WIDE), lambda i, d=d: (_win_block(i, d), 0)) for d in range(WIN_BLKS)]


def _att_fwd(q, k, v, bias, jobs=()):
    T = q.shape[0]
    nb = T // AQ_BLK

    def body(q_ref, k0, k1, k2, v0, v1, v2, b_ref, o_ref, s_scr):
        k_refs, v_refs = (k0, k1, k2), (v0, v1, v2)
        first_key = _att_first_key(pl.program_id(0))

        def scores(hd):
            q_h = q_ref[:, _head_tile(hd)] * A_SCALE
            for d in range(WIN_BLKS):
                s_scr[hd % 2, :, _win_cols(d)] = _dot_nt(q_h, k_refs[d][:, _head_tile(hd)])

        scores(0)
        for hd in range(ATT_HEADS):
            if hd + 1 < ATT_HEADS:
                scores(hd + 1)
            s_h = s_scr.at[hd % 2]
            for t in range(AQ_BLK // ATT_SLAB):
                rows = _slab_rows(t)
                s_h[rows, :] = _att_softmax_slab(s_h, b_ref, hd, rows, first_key)
            o_h = _dot(s_h[:, _win_cols(0)].astype(BF16), v_refs[0][:, _head_tile(hd)])
            for d in range(1, WIN_BLKS):
                o_h = o_h + _dot(s_h[:, _win_cols(d)].astype(BF16), v_refs[d][:, _head_tile(hd)])
            o_ref[:, _head_lanes(hd)] = o_h[:, :64].astype(BF16)

    return _pcall(
        body, name="att_fwd", grid=(nb,),
        in_specs=[_row_spec(AQ_BLK, ATT_WIDE)] + _win_specs() + _win_specs() + [_full_spec((6, AQ_BLK, AK_WIN))],
        out_specs=[_row_spec(AQ_BLK, 384)],
        out_shape=[jax.ShapeDtypeStruct((T, 384), BF16)],
        scratch_shapes=[pltpu.VMEM((2, AQ_BLK, AK_WIN), F32)],
        sem=("arbitrary",), operands=(q, k, k, k, v, v, v, bias), jobs=jobs)


def _att_bwd(q, k, v, bias, do, jobs=()):
    T = q.shape[0]
    nb = T // AQ_BLK

    def body(q_ref, k0, k1, k2, v0, v1, v2, b_ref, do_ref, dq_ref, dk_ref, dv_ref, db_ref, dk_acc, dv_acc,
             s_scr, dp_scr):
        k_refs, v_refs = (k0, k1, k2), (v0, v1, v2)
        i = pl.program_id(0)

        @pl.when(i == 0)
        def _():
            dk_acc[...] = jnp.zeros_like(dk_acc)
            dv_acc[...] = jnp.zeros_like(dv_acc)
            db_ref[...] = jnp.zeros_like(db_ref)

        first_key = _att_first_key(i)

        def scores(hd):
            q_h = q_ref[:, _head_tile(hd)] * A_SCALE
            do_h = do_ref[:, _head_tile(hd)]
            for d in range(WIN_BLKS):
                s_scr[hd % 2, :, _win_cols(d)] = _dot_nt(q_h, k_refs[d][:, _head_tile(hd)])
                dp_scr[hd % 2, :, _win_cols(d)] = _dot_nt(do_h, v_refs[d][:, _head_tile(hd)])

        scores(0)
        for hd in range(ATT_HEADS):
            if hd + 1 < ATT_HEADS:
                scores(hd + 1)
            s_h, dp_h = s_scr.at[hd % 2], dp_scr.at[hd % 2]
            for t in range(AQ_BLK // ATT_SLAB):
                rows = _slab_rows(t)
                p = _att_softmax_slab(s_h, b_ref, hd, rows, first_key)
                dp = dp_h[rows, :]
                ds = p * (dp - jnp.sum(p * dp, axis=-1, keepdims=True))
                db_ref[hd, rows, :] += ds
                s_h[rows, :] = p
                dp_h[rows, :] = ds
            q_h = q_ref[:, _head_tile(hd)] * A_SCALE
            do_h = do_ref[:, _head_tile(hd)]
            ls = _head_lanes(hd)
            dq_h = jnp.zeros((AQ_BLK, HEAD_PAD), F32)
            for d in range(WIN_BLKS):
                pb = s_h[:, _win_cols(d)].astype(BF16)
                dsb = dp_h[:, _win_cols(d)].astype(BF16)
                rows = pl.ds(pl.multiple_of(_win_block(i, d) * AQ_BLK, AQ_BLK), AQ_BLK)
                dv_acc[rows, ls] += _dot_tn(pb, do_h)[:, :64]
                dk_acc[rows, ls] += _dot_tn(dsb, q_h)[:, :64]
                dq_h = dq_h + _dot(dsb, k_refs[d][:, _head_tile(hd)])
            dq_ref[:, ls] = (dq_h[:, :64] * A_SCALE).astype(BF16)

        @pl.when(i == nb - 1)
        def _():
            dk_ref[...] = dk_acc[...].astype(BF16)
            dv_ref[...] = dv_acc[...].astype(BF16)

    return _pcall(
        body, name="att_bwd", grid=(nb,),
        in_specs=[_row_spec(AQ_BLK, ATT_WIDE)] + _win_specs() + _win_specs()
        + [_full_spec((6, AQ_BLK, AK_WIN)), _row_spec(AQ_BLK, ATT_WIDE)],
        out_specs=[_row_spec(AQ_BLK, 384), _full_spec((T, 384)), _full_spec((T, 384)),
                   _full_spec((6, AQ_BLK, AK_WIN))],
        out_shape=[jax.ShapeDtypeStruct((T, 384), BF16), jax.ShapeDtypeStruct((T, 384), BF16),
                   jax.ShapeDtypeStruct((T, 384), BF16), jax.ShapeDtypeStruct((6, AQ_BLK, AK_WIN), F32)],
        scratch_shapes=[pltpu.VMEM((T, 384), F32), pltpu.VMEM((T, 384), F32),
                        pltpu.VMEM((2, AQ_BLK, AK_WIN), F32), pltpu.VMEM((2, AQ_BLK, AK_WIN), F32)],
        sem=("arbitrary",), operands=(q, k, k, k, v, v, v, bias, do), jobs=jobs)


FF_BLK = 512
N_FF = 4096 // FF_BLK


def _outproj_mlp_fwd(h, o_gla, o_conv, o_att, w_out, gamma, w_up, w_down, jobs=()):
    T = h.shape[0]
    tm = 512

    def body(h_ref, og_ref, oc_ref, oa_ref, wo_ref, g_ref, wu_ref, wd_ref, h1_ref, xn_ref, h2_ref, a_ref, acc):
        j = pl.program_id(1)

        @pl.when(j == 0)
        def _():
            wo = wo_ref[...]
            h1 = (h_ref[...] + _dot(og_ref[...], wo[0:384]) + _dot(oc_ref[...], wo[384:640])
                  + _dot(oa_ref[...], wo[640:1024]))
            h1_ref[...] = h1
            r = lax.rsqrt(jnp.mean(h1 * h1, axis=-1, keepdims=True) + EPS)
            xn_ref[...] = (h1 * r * g_ref[...]).astype(BF16)
            acc[...] = h1

        a = jnp.maximum(_dot(xn_ref[...], wu_ref[0]), 0.0)
        a_ref[...] = a.astype(BF16)
        acc[...] += _dot((a * a).astype(BF16), wd_ref[0])

        @pl.when(j == N_FF - 1)
        def _():
            h2_ref[...] = acc[...]

    row = lambda n: pl.BlockSpec((tm, n), lambda i, j: (i, 0))
    return _pcall(
        body, name="outproj_mlp_fwd", grid=(T // tm, N_FF),
        in_specs=[row(D), row(384), row(256), row(384),
                  pl.BlockSpec((D, D), lambda i, j: (0, 0)), pl.BlockSpec((1, D), lambda i, j: (0, 0)),
                  pl.BlockSpec((1, D, FF_BLK), lambda i, j: (j, 0, 0)),
                  pl.BlockSpec((1, FF_BLK, D), lambda i, j: (j, 0, 0))],
        out_specs=[row(D), row(D), row(D), pl.BlockSpec((tm, FF_BLK), lambda i, j: (i, j))],
        out_shape=[jax.ShapeDtypeStruct((T, D), F32), jax.ShapeDtypeStruct((T, D), BF16),
                   jax.ShapeDtypeStruct((T, D), F32), jax.ShapeDtypeStruct((T, N_FF * FF_BLK), BF16)],
        scratch_shapes=[pltpu.VMEM((tm, D), F32)],
        sem=("arbitrary", "arbitrary"), operands=(h, o_gla, o_conv, o_att, w_out, gamma, w_up, w_down), jobs=jobs)


def _mlp_bwd(xn2, act, h1, dh2, gamma, w_up, w_down, jobs=()):
    T = xn2.shape[0]
    tm = 512
    nt = T // tm
    last = N_FF - 1

    def body(xn_ref, a_ref, h1_ref, dy_ref, g_ref, wu_ref, wd_ref, dh1_ref, dwu_ref, dwd_ref, dg_ref,
             dxn_acc, acc_u, acc_d):
        j = pl.program_id(0)
        i = pl.program_id(1)
        x = xn_ref[...]
        dy = dy_ref[...]
        dyb = dy.astype(BF16)
        wu = wu_ref[0]
        wd = wd_ref[0]
        a = a_ref[...].astype(F32)
        hh = (a * a).astype(BF16)
        du = (_dot_nt(dyb, wd) * (2.0 * a)).astype(BF16)
        cu_ = _dot_tn(x, du)
        cd_ = _dot_tn(hh, dyb)

        @pl.when(i == 0)
        def _():
            acc_u[...] = cu_
            acc_d[...] = cd_

        @pl.when(i > 0)
        def _():
            acc_u[...] += cu_
            acc_d[...] += cd_

        @pl.when(i == nt - 1)
        def _():
            dwu_ref[0, 0] = acc_u[...].astype(BF16)
            dwd_ref[0, 0] = acc_d[...].astype(BF16)

        rows = pl.ds(pl.multiple_of(i * tm, tm), tm)
        dxn = _dot_nt(du, wu)

        @pl.when(j == 0)
        def _():
            dxn_acc[rows, :] = dxn

        @pl.when(j > 0)
        def _():
            dxn_acc[rows, :] += dxn

        @pl.when(j == last)
        def _():
            @pl.when(i == 0)
            def _():
                dg_ref[...] = jnp.zeros_like(dg_ref)

            h1 = h1_ref[...]
            r = lax.rsqrt(jnp.mean(h1 * h1, axis=-1, keepdims=True) + EPS)
            dx, dgam = _rms_bwd(dxn_acc[rows, :], h1, r, g_ref[...])
            dh1_ref[...] = dy + dx
            dg_ref[...] += dgam

    late = lambda j, i: (jnp.where(j == last, i, 0), 0)
    return _pcall(
        body, name="mlp_bwd", grid=(N_FF, nt),
        in_specs=[pl.BlockSpec((tm, D), lambda j, i: (i, 0)), pl.BlockSpec((tm, FF_BLK), lambda j, i: (i, j)),
                  pl.BlockSpec((tm, D), late),
                  pl.BlockSpec((tm, D), lambda j, i: (i, 0)), pl.BlockSpec((1, D), lambda j, i: (0, 0)),
                  pl.BlockSpec((1, D, FF_BLK), lambda j, i: (j, 0, 0)),
                  pl.BlockSpec((1, FF_BLK, D), lambda j, i: (j, 0, 0))],
        out_specs=[pl.BlockSpec((tm, D), late),
                   pl.BlockSpec((1, 1, D, FF_BLK), lambda j, i: (j % 2, j // 2, 0, 0)),
                   pl.BlockSpec((1, 1, FF_BLK, D), lambda j, i: (j % 2, j // 2, 0, 0)),
                   pl.BlockSpec((1, D), lambda j, i: (0, 0))],
        out_shape=[jax.ShapeDtypeStruct((T, D), F32), jax.ShapeDtypeStruct((2, 4, D, FF_BLK), BF16),
                   jax.ShapeDtypeStruct((2, 4, FF_BLK, D), BF16), jax.ShapeDtypeStruct((1, D), F32)],
        scratch_shapes=[pltpu.VMEM((T, D), F32), pltpu.VMEM((D, FF_BLK), F32), pltpu.VMEM((FF_BLK, D), F32)],
        sem=("arbitrary", "arbitrary"), operands=(xn2, act, h1, dh2, gamma, w_up, w_down), jobs=jobs)


def _outproj_bwd(dh1, o_gla, o_conv, o_att, w_out, jobs=()):
    T = dh1.shape[0]
    tm = 512
    nt = T // tm

    def body(dy_ref, og_ref, oc_ref, oa_ref, wo_ref, dg_ref, dc_ref, da_ref, dw_ref, acc):
        i = pl.program_id(0)
        dyb = dy_ref[...].astype(BF16)
        dm = _dot_nt(dyb, wo_ref[...])
        dg_ref[...] = dm[:, 0:384].astype(BF16)
        dc_ref[...] = dm[:, 384:640].astype(BF16)
        _store_head_padded(da_ref, dm[:, 640:1024].astype(BF16))
        mixed = jnp.concatenate([og_ref[...], oc_ref[...], oa_ref[...]], axis=1)
        contrib = _dot_tn(mixed, dyb)

        @pl.when(i == 0)
        def _():
            acc[...] = contrib

        @pl.when(i > 0)
        def _():
            acc[...] += contrib

        @pl.when(i == nt - 1)
        def _():
            for j in range(N_DEV):
                dw_ref[j % 2, j // 2] = acc[j * 128:(j + 1) * 128, :].astype(BF16)

    return _pcall(
        body, name="outproj_bwd", grid=(nt,),
        in_specs=[_row_spec(tm, D), _row_spec(tm, 384), _row_spec(tm, 256), _row_spec(tm, 384),
                  _full_spec((D, D))],
        out_specs=[_row_spec(tm, 384), _row_spec(tm, 256), _row_spec(tm, ATT_WIDE), _full_spec((2, 4, 128, D))],
        out_shape=[jax.ShapeDtypeStruct((T, 384), BF16), jax.ShapeDtypeStruct((T, 256), BF16),
                   jax.ShapeDtypeStruct((T, ATT_WIDE), BF16), jax.ShapeDtypeStruct((2, 4, 128, D), BF16)],
        scratch_shapes=[pltpu.VMEM((D, D), F32)],
        sem=("arbitrary",), operands=(dh1, o_gla, o_conv, o_att, w_out), jobs=jobs)


def _loss_fwd_bwd(h, gamma, target):
    T = h.shape[0]
    tm = 512

    def body(h_ref, g_ref, t_ref, loss_ref, dh_ref, dg_ref):
        @pl.when(pl.program_id(0) == 0)
        def _():
            loss_ref[...] = jnp.zeros_like(loss_ref)
            dg_ref[...] = jnp.zeros_like(dg_ref)

        x = h_ref[...]
        r = lax.rsqrt(jnp.mean(x * x, axis=-1, keepdims=True) + EPS)
        gamma_ = g_ref[...]
        e = x * r * gamma_ - t_ref[...]
        loss_ref[...] += 0.5 * jnp.sum(jnp.mean(e * e, axis=-1, keepdims=True), axis=0, keepdims=True)
        dx, dgam = _rms_bwd(e * (1.0 / D), x, r, gamma_)
        dh_ref[...] = dx
        dg_ref[...] += dgam

    return pl.pallas_call(
        body, name="loss_fwd_bwd", grid=(T // tm,),
        in_specs=[_row_spec(tm, D), _full_spec((1, D)), _row_spec(tm, D)],
        out_specs=[_full_spec((8, 128)), _row_spec(tm, D), _full_spec((1, D))],
        out_shape=[jax.ShapeDtypeStruct((8, 128), F32), jax.ShapeDtypeStruct((T, D), F32),
                   jax.ShapeDtypeStruct((1, D), F32)],
        compiler_params=_cp(("arbitrary",)),
    )(h, gamma, target)


def _adamw_math(w, g, m, v):
    m = ADAM_B1 * m + (1.0 - ADAM_B1) * g
    v = ADAM_B2 * v + (1.0 - ADAM_B2) * (g * g)
    m_hat = m / (1.0 - ADAM_B1 ** ADAM_STEP)
    v_hat = v / (1.0 - ADAM_B2 ** ADAM_STEP)
    delta = -ADAM_LR * (m_hat / (jnp.sqrt(v_hat) + ADAM_EPS) + ADAM_WD * w)
    return delta, m, v


def _rs_adamw(a_own, r2, w, m, v, layer, chip_idx, rows_blk, prev=None):
    _, R, C = w.shape
    nblk = R // rows_blk

    def body(chip_ref, a_ref, r_ref, w_ref, m_ref, v_ref, *rest):
        g_out, d_out, m_out, v_out = rest[-4:]
        g = (a_ref[0].astype(F32) + r_ref[0].astype(F32)) + (r_ref[1].astype(F32) + r_ref[2].astype(F32))
        delta, m_new, v_new = _adamw_math(w_ref[0], g, m_ref[0], v_ref[0])
        g_out[0] = g
        d_out[0] = delta
        m_out[0] = m_new
        v_out[0] = v_new

    blk = pl.BlockSpec((1, rows_blk, C), lambda i, chip: (layer, i, 0))
    n_prev = 0 if prev is None else 4
    grid_spec = pltpu.PrefetchScalarGridSpec(
        num_scalar_prefetch=1, grid=(nblk,),
        in_specs=[pl.BlockSpec((1, rows_blk, C), lambda i, chip: (chip[0], i, 0)),
                  pl.BlockSpec((3, rows_blk, C), lambda i, chip: (0, i, 0)), blk, blk, blk]
        + [_any_spec()] * n_prev,
        out_specs=[blk, blk, blk, blk])
    return pl.pallas_call(
        body, name="rs_adamw", grid_spec=grid_spec,
        out_shape=[jax.ShapeDtypeStruct((DEPTH, R, C), F32)] * 4,
        input_output_aliases={6 + t: t for t in range(n_prev)},
        compiler_params=_cp(("arbitrary",)),
    )(chip_idx, a_own, r2, w, m, v, *(prev or ()))


def _pair_sum(g, r1, core_idx, rows_blk):
    _, _, R, C = g.shape
    nblk = R // rows_blk

    def body(core_ref, g_ref, r_ref, o_ref):
        o_ref[...] = (g_ref[0].astype(F32) + r_ref[...].astype(F32)).astype(BF16)

    grid_spec = pltpu.PrefetchScalarGridSpec(
        num_scalar_prefetch=1, grid=(4, nblk),
        in_specs=[pl.BlockSpec((1, 1, rows_blk, C), lambda k, i, core: (core[0], k, i, 0)),
                  pl.BlockSpec((1, rows_blk, C), lambda k, i, core: (k, i, 0))],
        out_specs=pl.BlockSpec((1, rows_blk, C), lambda k, i, core: (k, i, 0)))
    return pl.pallas_call(
        body, name="rs_pair_sum", grid_spec=grid_spec,
        out_shape=jax.ShapeDtypeStruct((4, R, C), BF16),
        compiler_params=_cp(("arbitrary", "arbitrary")),
    )(core_idx, g, r1)


def _small_sum(gathered):
    def body(g_ref, o_ref):
        acc = g_ref[0]
        for d in range(1, N_DEV):
            acc = acc + g_ref[d]
        o_ref[...] = acc

    return pl.pallas_call(
        body, name="small_sum",
        out_shape=jax.ShapeDtypeStruct(gathered.shape[1:], F32),
        compiler_params=_cp(),
    )(gathered)


def _adamw_small(ws, gs, ms, vs):
    n = len(ws)

    def body(*refs):
        w_r, g_r, m_r, v_r = refs[0:n], refs[n:2 * n], refs[2 * n:3 * n], refs[3 * n:4 * n]
        d_o, m_o, v_o = refs[4 * n:5 * n], refs[5 * n:6 * n], refs[6 * n:7 * n]
        for t in range(n):
            delta, m_new, v_new = _adamw_math(w_r[t][...], g_r[t][...], m_r[t][...], v_r[t][...])
            d_o[t][...] = delta
            m_o[t][...] = m_new
            v_o[t][...] = v_new

    shapes = [jax.ShapeDtypeStruct(w.shape, F32) for w in ws]
    outs = pl.pallas_call(
        body, name="adamw_small", out_shape=shapes * 3, compiler_params=_cp(),
    )(*ws, *gs, *ms, *vs)
    return outs[0:n], outs[n:2 * n], outs[2 * n:3 * n]


def _mesh_pos():
    return lax.axis_index("x"), lax.axis_index("y"), lax.axis_index("c")


def _peers():
    x, y, c = _mesh_pos()
    return (x, y, c), (x, y, 1 - c), [(1 - x, y), (x, 1 - y), (1 - x, 1 - y)]


def _slot(ref, pos):
    return ref.at[4 * pos[0] + 2 * pos[1] + pos[2]]


def _remote(src, dst, send_sem, recv_sem, to):
    return pltpu.make_async_remote_copy(src_ref=src, dst_ref=dst, send_sem=send_sem, recv_sem=recv_sem,
                                        device_id=to, device_id_type=MESH)


def _ag_spread(shards):
    n = len(shards)

    def copies(ins, outs, sems):
        send, recv, loc = sems
        me, sibling, chips = _peers()
        peers = [sibling] + [(*chip, me[2]) for chip in chips]
        local = [pltpu.make_async_copy(ins[a], _slot(outs[a], me), loc.at[a]) for a in range(n)]
        sends = [_remote(ins[a], _slot(outs[a], me), send.at[a, k], recv.at[a, k], p)
                 for a in range(n) for k, p in enumerate(peers)]
        recvs = [_remote(ins[a], _slot(outs[a], p), send.at[a, k], recv.at[a, k], p)
                 for a in range(n) for k, p in enumerate(peers)]
        return local, sends, recvs

    def start(ins, outs, sems):
        local, sends, _ = copies(ins, outs, sems)
        for cp in local + sends:
            cp.start()

    def finish(ins, outs, sems):
        local, sends, recvs = copies(ins, outs, sems)
        for cp in sends:
            cp.wait_send()
        for cp in recvs:
            cp.wait_recv()
        for cp in local:
            cp.wait()

    return _Job(shards, [jax.ShapeDtypeStruct((N_DEV,) + a.shape, a.dtype) for a in shards],
                [pltpu.SemaphoreType.DMA((n, 4)), pltpu.SemaphoreType.DMA((n, 4)), pltpu.SemaphoreType.DMA((n,))],
                start, finish)


def _ag_pass(stacks):
    n = len(stacks)

    def copies(ins, outs, sems):
        send, recv = sems
        me, sibling, chips = _peers()
        sends = [_remote(_slot(ins[a], (*chip, me[2])), _slot(outs[a], (*chip, me[2])), send.at[a, j], recv.at[a, j],
                         sibling) for a in range(n) for j, chip in enumerate(chips)]
        recvs = [_remote(_slot(ins[a], (*chip, me[2])), _slot(outs[a], (*chip, 1 - me[2])), send.at[a, j],
                         recv.at[a, j], sibling) for a in range(n) for j, chip in enumerate(chips)]
        return sends, recvs

    def start(ins, outs, sems):
        for cp in copies(ins, outs, sems)[0]:
            cp.start()

    def finish(ins, outs, sems):
        sends, recvs = copies(ins, outs, sems)
        for cp in sends:
            cp.wait_send()
        for cp in recvs:
            cp.wait_recv()

    return _Job(stacks, [jax.ShapeDtypeStruct(a.shape, a.dtype) for a in stacks],
                [pltpu.SemaphoreType.DMA((n, 3)), pltpu.SemaphoreType.DMA((n, 3))],
                start, finish, aliases={a: a for a in range(n)})


def _rs_swap(parts):
    n = len(parts)

    def copies(ins, outs, sems):
        send, recv = sems
        me, sibling, _ = _peers()
        return [_remote(ins[a].at[1 - me[2]], outs[a], send.at[a], recv.at[a], sibling) for a in range(n)]

    def start(ins, outs, sems):
        for cp in copies(ins, outs, sems):
            cp.start()

    def finish(ins, outs, sems):
        for cp in copies(ins, outs, sems):
            cp.wait()

    return _Job(parts, [jax.ShapeDtypeStruct(a.shape[1:], a.dtype) for a in parts],
                [pltpu.SemaphoreType.DMA((n,)), pltpu.SemaphoreType.DMA((n,))], start, finish)


def _rs_ici(pairs):
    n = len(pairs)

    def copies(ins, outs, sems):
        send, recv = sems
        me, _, chips = _peers()
        return [_remote(ins[a].at[2 * chip[0] + chip[1]], outs[a].at[j], send.at[a, j], recv.at[a, j],
                        (*chip, me[2])) for a in range(n) for j, chip in enumerate(chips)]

    def start(ins, outs, sems):
        for cp in copies(ins, outs, sems):
            cp.start()

    def finish(ins, outs, sems):
        for cp in copies(ins, outs, sems):
            cp.wait()

    return _Job(pairs, [jax.ShapeDtypeStruct((3,) + a.shape[1:], a.dtype) for a in pairs],
                [pltpu.SemaphoreType.DMA((n, 3)), pltpu.SemaphoreType.DMA((n, 3))], start, finish)


def _comm_call(jobs, name):
    def body():
        pass

    return _pcall(body, name=name, grid=(), in_specs=[], out_specs=[], out_shape=[], operands=(), jobs=jobs)[1]


def _allgather(arrs, name):
    n = len(arrs)

    def body(*refs):
        ins, outs = refs[:n], refs[n:2 * n]
        send_sems, recv_sems, local_sems = refs[2 * n:]
        x, y, c = _mesh_pos()
        me, sibling = (x, y, c), (x, y, 1 - c)
        chips = [(1 - x, y), (x, 1 - y), (1 - x, 1 - y)]

        def slot(a, pos):
            return outs[a].at[4 * pos[0] + 2 * pos[1] + pos[2]]

        def copy(a, k, block, to, src=None):
            return pltpu.make_async_remote_copy(
                src_ref=slot(a, block) if src is None else src, dst_ref=slot(a, block),
                send_sem=send_sems.at[a, k], recv_sem=recv_sems.at[a, k],
                device_id=to, device_id_type=MESH)

        mine = [pltpu.make_async_copy(ins[a], slot(a, me), local_sems.at[a]) for a in range(n)]
        for cp in mine:
            cp.start()
        first = []
        for a in range(n):
            first.append(copy(a, 0, me, sibling, src=ins[a]))
            first += [copy(a, 1 + j, me, (*chip, c), src=ins[a]) for j, chip in enumerate(chips)]
        for cp in first:
            cp.start()
        passed = []
        for j, chip in enumerate(chips):
            for a in range(n):
                copy(a, 1 + j, (*chip, c), me).wait_recv()
                fwd = copy(a, 4 + j, (*chip, c), sibling)
                fwd.start()
                passed.append(fwd)
        for a in range(n):
            copy(a, 0, sibling, me).wait_recv()
            for j, chip in enumerate(chips):
                copy(a, 4 + j, (*chip, 1 - c), me).wait_recv()
        for cp in first + passed:
            cp.wait_send()
        for cp in mine:
            cp.wait()

    return pl.pallas_call(
        body, name=name,
        in_specs=[_any_spec()] * n, out_specs=[_any_spec()] * n,
        out_shape=[jax.ShapeDtypeStruct((N_DEV,) + a.shape, a.dtype) for a in arrs],
        scratch_shapes=[pltpu.SemaphoreType.DMA((n, 7)), pltpu.SemaphoreType.DMA((n, 7)),
                        pltpu.SemaphoreType.DMA((n,))],
        compiler_params=_cp(),
    )(*arrs)


W_IN_SHARD = 354
W_IN_COLS = ((0, 192, OQ), (192, 192, OKK), (384, 384, OV), (768, 384, OG), (1152, 16, OLR), (1168, 512, OCU),
             (1680, 384, OAQ), (2064, 384, OAK), (2448, 384, OAV))


def _w_in_padded(stack):
    new_to_ref = {new: (start, width) for start, width, new in W_IN_COLS}
    cols = []
    for new, padded in IN_GROUPS:
        start, width = new_to_ref[new]
        a = start
        while a < start + width:
            j = a // W_IN_SHARD
            b = min(start + width, (j + 1) * W_IN_SHARD)
            cols.append(stack[j][:, a - j * W_IN_SHARD:b - j * W_IN_SHARD])
            a = b
        if padded > width:
            cols.append(jnp.zeros((stack.shape[1], padded - width), stack.dtype))
    return jnp.concatenate(cols, axis=1)


def _dw_in_shards(dw):
    shards = []
    for j in range(N_DEV):
        lo, hi = j * W_IN_SHARD, (j + 1) * W_IN_SHARD
        segs = []
        for start, width, new in W_IN_COLS:
            a, b = max(lo, start), min(hi, start + width)
            if a < b:
                segs.append(dw[:, new + a - start:new + b - start])
        shards.append(jnp.concatenate(segs, axis=1))
    return jnp.stack([jnp.stack([shards[2 * chip + core] for chip in range(4)]) for core in range(2)])


def _pad_to(a, shape):
    return jnp.pad(a, [(0, s - d) for d, s in zip(a.shape, shape)])


SMALL_LAYOUT = (
    ("norm_mix", 2, 1024), ("norm_ffn", 2, 1024), ("norm_final", 1, 1024), ("gla_norm", 2, 384),
    ("b_gla_gate", 2, 192), ("b_dw", 2, 256), ("conv_ln_g", 2, 256), ("conv_ln_b", 2, 256),
    ("rel_bias", 12, 257), ("w_gla_gate", 32, 192), ("w_dw", 62, 256),
)
SMALL_LANES = 128
SMALL_TILE = 8 * SMALL_LANES


def _small_tile_rows(r, lanes):
    return -(-(r * lanes) // SMALL_TILE) * 8


SMALL_ROWS = sum(_small_tile_rows(r, lanes) for _, r, lanes in SMALL_LAYOUT)


def _pack_small(parts):
    tiles = []
    for name, r, lanes in SMALL_LAYOUT:
        rows = _small_tile_rows(r, lanes)
        flat = _pad_to(parts[name].reshape(r * lanes), (rows * SMALL_LANES,))
        tiles.append(flat.reshape(rows, SMALL_LANES))
    return jnp.concatenate(tiles, axis=0)


def _unpack_small(packed):
    out, r0 = {}, 0
    for name, r, lanes in SMALL_LAYOUT:
        rows = _small_tile_rows(r, lanes)
        out[name] = packed[r0:r0 + rows].reshape(rows * SMALL_LANES)[:r * lanes].reshape(r, lanes)
        r0 += rows
    return out


def _mixers_fwd(h, wl, w_in_p, plan=None):
    plan, res = plan or {}, {}

    def jobs(host):
        return plan[host](res) if host in plan else ()

    (q, k, v, g, cu, aq, ak, av, lr), res["inproj"] = _inproj_fwd(h, wl["norm_mix"], w_in_p, jobs=jobs("inproj"))
    bias = _relbias_expand(wl["rb"])
    (o_att,), res["att"] = _att_fwd(aq, ak, av, bias, jobs=jobs("att"))
    (o_gla, states), res["gla"] = _gla_fwd(q, k, v, g, lr, wl["wg"], wl["bg"], wl["gn"], jobs=jobs("gla"))
    (o_conv,), res["conv"] = _conv_fwd(cu, wl["w_dw"], wl["b_dw"], wl["ln_g"], wl["ln_b"], jobs=jobs("conv"))
    sv = dict(h=h, w_in=w_in_p, q=q, k=k, v=v, g=g, cu=cu, aq=aq, ak=ak, av=av, lr=lr,
              o_gla=o_gla, o_conv=o_conv, o_att=o_att, states=states, bias=bias)
    return sv, res


def _mixers_bwd(sv, wl, dh1, d_ogla, d_oconv, att_grads, conv_jobs=()):
    daq, dak, dav, dbias = att_grads
    d_rb = _relbias_grad(dbias)
    (dcu, dw_dw, db_dw, dln_g, dln_b), conv_res = _conv_bwd(
        sv["cu"], d_oconv, wl["w_dw"], wl["b_dw"], wl["ln_g"], wl["ln_b"], jobs=conv_jobs)
    dq, dk, dv, dg, dlr, dwg, dbg, dgn = _gla_bwd(sv["q"], sv["k"], sv["v"], sv["g"], sv["lr"], sv["states"],
                                                  d_ogla, wl["wg"], wl["bg"], wl["gn"])
    dh, dw_in, d_nmix = _inproj_bwd(sv["h"], dh1, wl["norm_mix"], sv["w_in"],
                                    (dq, dk, dv, dg, dcu, daq, dak, dav, dlr))
    small = dict(norm_mix=d_nmix, wg=dwg, bg=dbg, gn=dgn, w_dw=dw_dw, b_dw=db_dw, ln_g=dln_g, ln_b=dln_b, rb=d_rb)
    return dh, dw_in, small, conv_res


def _layer_small(l, w_dw_full, norm_mix, w_gla_gate, b_gla_gate, gla_norm, b_dw, conv_ln_g, conv_ln_b, rel_bias,
                 norm_ffn):
    return dict(
        norm_mix=norm_mix[l][None, :], norm_ffn=norm_ffn[l][None, :],
        wg=_pad_to(w_gla_gate[l], (128, 256)).astype(BF16), bg=_pad_to(b_gla_gate[l][None, :], (1, 256)),
        gn=gla_norm[l][None, :], w_dw=_pad_to(w_dw_full, (32, 256)), b_dw=b_dw[l][None, :],
        ln_g=conv_ln_g[l][None, :], ln_b=conv_ln_b[l][None, :], rb=_pad_to(rel_bias[l], (8, 384)))


RS_ROWS = dict(w_in=512, w_out=128, w_up=512, w_down=256)


def kernel(x, norm_mix, w_in, w_gla_gate, b_gla_gate, gla_norm, w_dw, b_dw, conv_ln_g, conv_ln_b, rel_bias, w_out, norm_ffn, w_up, w_down, norm_final, loss_target, m_norm_mix, m_w_in, m_w_gla_gate, m_b_gla_gate, m_gla_norm, m_w_dw, m_b_dw, m_conv_ln_g, m_conv_ln_b, m_rel_bias, m_w_out, m_norm_ffn, m_w_up, m_w_down, m_norm_final, v_norm_mix, v_w_in, v_w_gla_gate, v_b_gla_gate, v_gla_norm, v_w_dw, v_b_dw, v_conv_ln_g, v_conv_ln_b, v_rel_bias, v_w_out, v_norm_ffn, v_w_up, v_w_down, v_norm_final):
    mx, my, mc = _mesh_pos()
    me = 4 * mx + 2 * my + mc
    chip_idx = (2 * mx + my).astype(jnp.int32).reshape(1)
    core_idx = mc.astype(jnp.int32).reshape(1)
    x0, target = x[0], loss_target[0]

    def pair_sums(parts, r1):
        return [_pair_sum(p, r, core_idx, p.shape[2]) for p, r in zip(parts, r1)]

    sh = [dict(w_in=w_in[l].astype(BF16), w_out=w_out[l].astype(BF16), w_up=w_up[l].astype(BF16),
               w_down=w_down[l].astype(BF16)) for l in range(DEPTH)]
    dw_flat = _pad_to(w_dw, (DEPTH, 32, 32)).reshape(16, 128)
    st_in0, st_dw = _allgather([sh[0]["w_in"], dw_flat], "allgather_first")
    dw_all = st_dw.reshape(N_DEV, DEPTH, 32, 32)[:, :, :KCONV, :]
    dw_all = jnp.transpose(dw_all, (1, 2, 0, 3)).reshape(DEPTH, KCONV, 256)
    wl = [_layer_small(l, dw_all[l], norm_mix, w_gla_gate, b_gla_gate, gla_norm, b_dw, conv_ln_g, conv_ln_b,
                       rel_bias, norm_ffn) for l in range(DEPTH)]

    s0, s1 = sh[0], sh[1]
    sv0, g0 = _mixers_fwd(x0, wl[0], _w_in_padded(st_in0), plan=dict(
        inproj=lambda r: [_ag_spread([s0["w_out"], s1["w_in"]])],
        att=lambda r: [_ag_spread([s0["w_up"]]), _ag_pass(r["inproj"])],
        gla=lambda r: [_ag_spread([s0["w_down"]]), _ag_pass(r["att"][:1])],
        conv=lambda r: [_ag_spread([s1["w_out"]]), _ag_pass(r["gla"][:1])]))
    st_out0, st_in1 = g0["att"][1:]
    st_up0, st_down0 = g0["gla"][1], g0["conv"][1]
    wo0 = st_out0.reshape(D, D)
    (h1_0, xn2_0, h2_0, act_0), (up1_half, down1_half, st_out1) = _outproj_mlp_fwd(
        x0, sv0["o_gla"], sv0["o_conv"], sv0["o_att"], wo0, wl[0]["norm_ffn"], st_up0, st_down0,
        jobs=[_ag_spread([s1["w_up"], s1["w_down"]]), _ag_pass(g0["conv"][:1])])

    sv1, g1 = _mixers_fwd(h2_0, wl[1], _w_in_padded(st_in1), plan=dict(
        att=lambda r: [_ag_pass([up1_half, down1_half])]))
    st_up1, st_down1 = g1["att"]
    wo1 = st_out1.reshape(D, D)
    (h1_1, xn2_1, h2_1, act_1), _ = _outproj_mlp_fwd(
        h2_0, sv1["o_gla"], sv1["o_conv"], sv1["o_att"], wo1, wl[1]["norm_ffn"], st_up1, st_down1)

    loss8, dh, d_nf = _loss_fwd_bwd(h2_1, norm_final[None, :], target)
    loss = lax.psum(loss8[0, 0], ("x", "y", "c"))

    def layer_bwd(dh, sv, wl_l, xn2, act, h1, wo, st_up, st_down, mlp_jobs):
        (dh1, dw_up, dw_down, d_nffn), mlp_res = _mlp_bwd(xn2, act, h1, dh, wl_l["norm_ffn"], st_up, st_down,
                                                           jobs=mlp_jobs)
        ud = [dw_up, dw_down]
        (d_ogla, d_oconv, d_oatt, dw_out), r1 = _outproj_bwd(
            dh1, sv["o_gla"], sv["o_conv"], sv["o_att"], wo, jobs=[_rs_swap(ud)])
        pair_ud = pair_sums(ud, r1)
        att_grads, r = _att_bwd(sv["aq"], sv["ak"], sv["av"], sv["bias"], d_oatt,
                                jobs=[_rs_ici(pair_ud), _rs_swap([dw_out])])
        r2_ud, r1_out = r[:2], r[2:]
        pair_out = pair_sums([dw_out], r1_out)
        dh_in, dw_in, small, r2_out = _mixers_bwd(sv, wl_l, dh1, d_ogla, d_oconv, att_grads,
                                                   conv_jobs=[_rs_ici(pair_out)])
        small["norm_ffn"] = d_nffn
        sums = dict(w_out=(pair_out[0], r2_out[0]), w_up=(pair_ud[0], r2_ud[0]), w_down=(pair_ud[1], r2_ud[1]))
        return dh_in, dw_in, small, sums, mlp_res

    dh, dw_in1, small1, sums1, _ = layer_bwd(dh, sv1, wl[1], xn2_1, act_1, h1_1, wo1, st_up1, st_down1, ())
    in1 = [_dw_in_shards(dw_in1)]
    pair_in1 = pair_sums(in1, _comm_call([_rs_swap(in1)], "rs_swap_w_in_1"))
    dx, dw_in0, small0, sums0, r2_in1 = layer_bwd(dh, sv0, wl[0], xn2_0, act_0, h1_0, wo0, st_up0, st_down0,
                                                  [_rs_ici(pair_in1)])
    sums1["w_in"] = (pair_in1[0], r2_in1[0])

    in0 = [_dw_in_shards(dw_in0)]
    pair_in0 = pair_sums(in0, _comm_call([_rs_swap(in0)], "rs_swap_w_in_0"))
    sums0["w_in"] = (pair_in0[0], _comm_call([_rs_ici(pair_in0)], "rs_ici_w_in_0")[0])

    big_w = dict(w_in=(w_in, m_w_in, v_w_in), w_out=(w_out, m_w_out, v_w_out), w_up=(w_up, m_w_up, v_w_up),
                 w_down=(w_down, m_w_down, v_w_down))
    pairs = {1: sums1, 0: sums0}
    big_out = {}
    for name, (w_, m_, v_) in big_w.items():
        res = None
        for l in (1, 0):
            a_own, r2_ = pairs[l][name]
            res = _rs_adamw(a_own, r2_, w_, m_, v_, l, chip_idx, RS_ROWS[name], prev=res)
        big_out[name] = res

    grads = (small0, small1)
    parts = dict(
        norm_mix=jnp.concatenate([grads[l]["norm_mix"] for l in range(DEPTH)], axis=0),
        norm_ffn=jnp.concatenate([grads[l]["norm_ffn"] for l in range(DEPTH)], axis=0),
        norm_final=d_nf,
        gla_norm=jnp.concatenate([grads[l]["gn"] for l in range(DEPTH)], axis=0),
        b_gla_gate=jnp.concatenate([grads[l]["bg"][:, :192] for l in range(DEPTH)], axis=0),
        b_dw=jnp.concatenate([grads[l]["b_dw"] for l in range(DEPTH)], axis=0),
        conv_ln_g=jnp.concatenate([grads[l]["ln_g"] for l in range(DEPTH)], axis=0),
        conv_ln_b=jnp.concatenate([grads[l]["ln_b"] for l in range(DEPTH)], axis=0),
        rel_bias=jnp.concatenate([grads[l]["rb"][:6, :N_REL] for l in range(DEPTH)], axis=0),
        w_gla_gate=jnp.concatenate([grads[l]["wg"][:16, :192] for l in range(DEPTH)], axis=0),
        w_dw=jnp.concatenate([grads[l]["w_dw"][:KCONV] for l in range(DEPTH)], axis=0),
    )
    small_all = _allgather([_pack_small(parts)], "allgather_small")[0]
    sg = _unpack_small(_small_sum(small_all))
    dw_grad = lax.dynamic_slice_in_dim(sg["w_dw"].reshape(DEPTH, KCONV, 256), me * 32, 32, axis=2)
    small_g = dict(
        norm_mix=sg["norm_mix"], w_gla_gate=sg["w_gla_gate"].reshape(DEPTH, 16, 192), b_gla_gate=sg["b_gla_gate"],
        gla_norm=sg["gla_norm"], w_dw=dw_grad, b_dw=sg["b_dw"], conv_ln_g=sg["conv_ln_g"],
        conv_ln_b=sg["conv_ln_b"], rel_bias=sg["rel_bias"].reshape(DEPTH, 6, N_REL), norm_ffn=sg["norm_ffn"],
        norm_final=sg["norm_final"].reshape(D))
    small_names = ("norm_mix", "w_gla_gate", "b_gla_gate", "gla_norm", "w_dw", "b_dw", "conv_ln_g", "conv_ln_b",
                   "rel_bias", "norm_ffn", "norm_final")
    small_w = dict(norm_mix=norm_mix, w_gla_gate=w_gla_gate, b_gla_gate=b_gla_gate, gla_norm=gla_norm, w_dw=w_dw,
                   b_dw=b_dw, conv_ln_g=conv_ln_g, conv_ln_b=conv_ln_b, rel_bias=rel_bias, norm_ffn=norm_ffn,
                   norm_final=norm_final)
    small_m = dict(norm_mix=m_norm_mix, w_gla_gate=m_w_gla_gate, b_gla_gate=m_b_gla_gate, gla_norm=m_gla_norm,
                   w_dw=m_w_dw, b_dw=m_b_dw, conv_ln_g=m_conv_ln_g, conv_ln_b=m_conv_ln_b, rel_bias=m_rel_bias,
                   norm_ffn=m_norm_ffn, norm_final=m_norm_final)
    small_v = dict(norm_mix=v_norm_mix, w_gla_gate=v_w_gla_gate, b_gla_gate=v_b_gla_gate, gla_norm=v_gla_norm,
                   w_dw=v_w_dw, b_dw=v_b_dw, conv_ln_g=v_conv_ln_g, conv_ln_b=v_conv_ln_b, rel_bias=v_rel_bias,
                   norm_ffn=v_norm_ffn, norm_final=v_norm_final)
    s_delta, s_m, s_v = _adamw_small([small_w[n] for n in small_names], [small_g[n] for n in small_names],
                                     [small_m[n] for n in small_names], [small_v[n] for n in small_names])
    s_idx = {n: t for t, n in enumerate(small_names)}

    order = ("norm_mix", "w_in", "w_gla_gate", "b_gla_gate", "gla_norm", "w_dw", "b_dw", "conv_ln_g", "conv_ln_b",
             "rel_bias", "w_out", "norm_ffn", "w_up", "w_down", "norm_final")

    def pick(kind, name):
        if name in big_out:
            return big_out[name][kind]
        t = s_idx[name]
        return (small_g[name], s_delta[t], s_m[t], s_v[t])[kind]

    outs = [loss, dx[None]]
    for kind in range(4):
        outs += [pick(kind, n) for n in order]
    return tuple(outs)
```

```python
import functools

import jax
import jax.numpy as jnp
from jax import lax
from jax.experimental import pallas as pl
from jax.experimental.pallas import tpu as pltpu

F32 = jnp.float32
BF16 = jnp.bfloat16
MESH = pl.DeviceIdType.MESH

D = 1024
DEPTH = 2
CH = 64
EPS = 1e-6
NEG = -1e30
N_DEV = 8
N_REL = 257
Q_SCALE = 48.0 ** -0.5
A_SCALE = 64.0 ** -0.5
GATE_TAU = 16.0
KCONV = 31

OQ, OKK, OV, OG, OCU, OAQ, OAK, OAV, OLR, DINP = 0, 256, 512, 896, 1280, 1792, 2176, 2560, 2944, 3072
IN_GROUPS = ((OQ, 256), (OKK, 256), (OV, 384), (OG, 384), (OCU, 512), (OAQ, 384), (OAK, 384), (OAV, 384), (OLR, 128))

AQ_BLK = 256
AK_WIN = 768
WIN_LEFT = 2
RB_W = 1536

ADAM_LR, ADAM_B1, ADAM_B2, ADAM_EPS, ADAM_WD, ADAM_STEP = 0.001, 0.9, 0.999, 1e-08, 0.01, 10


V7X_VMEM_MIB = 64
VMEM_LIMIT_MIB = V7X_VMEM_MIB - 1


def _cp(sem=None):
    kw = {"vmem_limit_bytes": VMEM_LIMIT_MIB * 1024 * 1024}
    if sem is not None:
        kw["dimension_semantics"] = sem
    return pltpu.CompilerParams(**kw)


def _dot(a, b):
    return jnp.dot(a, b, preferred_element_type=F32)


def _dot_nt(a, b):
    return lax.dot_general(a, b, (((1,), (1,)), ((), ())), preferred_element_type=F32)


def _dot_tn(a, b):
    return lax.dot_general(a, b, (((0,), (0,)), ((), ())), preferred_element_type=F32)


def _split2(a):
    hi = a.astype(BF16)
    lo = (a - hi.astype(F32)).astype(BF16)
    return hi, lo


def _split3(a):
    hi = a.astype(BF16)
    r1 = a - hi.astype(F32)
    mid = r1.astype(BF16)
    lo = (r1 - mid.astype(F32)).astype(BF16)
    return hi, mid, lo


def _sigmoid(x):
    return 1.0 / (1.0 + jnp.exp(-x))


def _group(idx, size, n):
    g = jnp.zeros_like(idx)
    for t in range(1, n):
        g = g + (idx >= t * size).astype(jnp.int32)
    return g


def _rms_bwd(dy, x, r, gamma):
    xh = x * r
    dxh = dy * gamma
    dx = r * (dxh - xh * jnp.mean(dxh * xh, axis=-1, keepdims=True))
    return dx, jnp.sum(dy * xh, axis=0, keepdims=True)


def _row_spec(tm, n):
    return pl.BlockSpec((tm, n), lambda i: (i, 0))


def _full_spec(shape):
    nd = len(shape)
    return pl.BlockSpec(shape, lambda *_: (0,) * nd)


def _any_spec():
    return pl.BlockSpec(memory_space=pl.ANY)


class _Job:
    def __init__(self, operands, out_shapes, sems, start, finish, aliases=None):
        self.operands, self.out_shapes, self.sems = list(operands), list(out_shapes), list(sems)
        self.start, self.finish, self.aliases = start, finish, dict(aliases or {})


def _pcall(body, *, name, grid, in_specs, out_specs, out_shape, operands, scratch_shapes=(), sem=None, jobs=()):
    jobs = list(jobs)
    in_specs, out_specs, out_shape = list(in_specs), list(out_specs), list(out_shape)
    scratch_shapes = list(scratch_shapes)
    n_in, n_out, n_scr = len(in_specs), len(out_specs), len(scratch_shapes)
    j_in = [a for j in jobs for a in j.operands]
    j_out = [s for j in jobs for s in j.out_shapes]
    j_sem = [s for j in jobs for s in j.sems]
    aliases, io, oo = {}, n_in, n_out
    for j in jobs:
        for a, b in j.aliases.items():
            aliases[io + a] = oo + b
        io += len(j.operands)
        oo += len(j.out_shapes)

    def wrapped(*refs):
        own_in, ji = refs[:n_in], refs[n_in:n_in + len(j_in)]
        o0 = n_in + len(j_in)
        own_out, jo = refs[o0:o0 + n_out], refs[o0 + n_out:o0 + n_out + len(j_out)]
        s0 = o0 + n_out + len(j_out)
        own_scr, js = refs[s0:s0 + n_scr], refs[s0 + n_scr:]

        def each_job(fn_name):
            a = b = c = 0
            for j in jobs:
                na, nb, nc = len(j.operands), len(j.out_shapes), len(j.sems)
                getattr(j, fn_name)(ji[a:a + na], jo[b:b + nb], js[c:c + nc])
                a, b, c = a + na, b + nb, c + nc

        if jobs and grid:
            pids = [pl.program_id(d) for d in range(len(grid))]
            first = functools.reduce(jnp.logical_and, [p == 0 for p in pids])
            last = functools.reduce(jnp.logical_and, [p == g - 1 for p, g in zip(pids, grid)])
            pl.when(first)(lambda: each_job("start"))
        elif jobs:
            each_job("start")

        body(*own_in, *own_out, *own_scr)

        if jobs and grid:
            pl.when(last)(lambda: each_job("finish"))
        elif jobs:
            each_job("finish")

    res = pl.pallas_call(
        wrapped, name=name, grid=grid,
        in_specs=in_specs + [_any_spec()] * len(j_in), out_specs=out_specs + [_any_spec()] * len(j_out),
        out_shape=out_shape + j_out, scratch_shapes=scratch_shapes + j_sem,
        input_output_aliases=aliases, compiler_params=_cp(sem),
    )(*operands, *j_in)
    return res[:n_out], res[n_out:]


ATT_HEADS = 6
HEAD_PAD = 128
ATT_WIDE = ATT_HEADS * HEAD_PAD
ATT_GROUP_OFFS = (OAQ, OAK, OAV)


def _store_head_padded(o_ref, part):
    o_ref[...] = jnp.zeros_like(o_ref)
    for hd in range(ATT_HEADS):
        o_ref[:, hd * HEAD_PAD:hd * HEAD_PAD + 64] = part[:, hd * 64:(hd + 1) * 64]


def _inproj_fwd(h, gamma, w, jobs=()):
    T = h.shape[0]
    tm = 512

    def body(h_ref, g_ref, w_ref, *outs):
        x = h_ref[...]
        r = lax.rsqrt(jnp.mean(x * x, axis=-1, keepdims=True) + EPS)
        xn = (x * r * g_ref[...]).astype(BF16)
        p = _dot(xn, w_ref[...])
        for o_ref, (off, n) in zip(outs, IN_GROUPS):
            part = p[:, off:off + n].astype(BF16)
            if off in ATT_GROUP_OFFS:
                _store_head_padded(o_ref, part)
            else:
                o_ref[...] = part

    widths = [ATT_WIDE if off in ATT_GROUP_OFFS else n for off, n in IN_GROUPS]
    return _pcall(
        body, name="inproj_fwd", grid=(T // tm,),
        in_specs=[_row_spec(tm, D), _full_spec((1, D)), _full_spec((D, DINP))],
        out_specs=[_row_spec(tm, n) for n in widths],
        out_shape=[jax.ShapeDtypeStruct((T, n), BF16) for n in widths],
        sem=("arbitrary",), operands=(h, gamma, w), jobs=jobs)


def _inproj_bwd(h, dh_in, gamma, w, dparts):
    T = h.shape[0]
    tm = 512
    nt = T // tm

    def body(h_ref, dhin_ref, g_ref, w_ref, *rest):
        dp_refs = rest[:9]
        dh_ref, dhb_ref, dw_ref, dg_ref, acc = rest[9:]
        i = pl.program_id(0)

        @pl.when(i == 0)
        def _():
            acc[...] = jnp.zeros_like(acc)
            dg_ref[...] = jnp.zeros_like(dg_ref)

        x = h_ref[...]
        r = lax.rsqrt(jnp.mean(x * x, axis=-1, keepdims=True) + EPS)
        gamma_ = g_ref[...]
        xnt = jnp.transpose((x * r * gamma_).astype(BF16))
        dxn = jnp.zeros((tm, D), F32)
        for d_ref, (off, n) in zip(dp_refs, IN_GROUPS):
            d = d_ref[...]
            acc[:, off:off + n] += _dot(xnt, d)
            dxn = dxn + _dot_nt(d, w_ref[:, off:off + n])
        dx, dgam = _rms_bwd(dxn, x, r, gamma_)
        dh = dhin_ref[...] + dx
        dh_ref[...] = dh
        dhb_ref[...] = dh.astype(BF16)
        dg_ref[...] += dgam

        @pl.when(i == nt - 1)
        def _():
            dw_ref[...] = acc[...].astype(BF16)

    return pl.pallas_call(
        body, name="inproj_bwd", grid=(nt,),
        in_specs=[_row_spec(tm, D), _row_spec(tm, D), _full_spec((1, D)),
                  pl.BlockSpec((D, DINP), lambda i: (0, 0), pipeline_mode=pl.Buffered(1))]
        + [_row_spec(tm, n) for _, n in IN_GROUPS],
        out_specs=[_row_spec(tm, D), _row_spec(tm, D), _full_spec((D, DINP)), _full_spec((1, D))],
        out_shape=[jax.ShapeDtypeStruct((T, D), F32), jax.ShapeDtypeStruct((T, D), BF16),
                   jax.ShapeDtypeStruct((D, DINP), BF16), jax.ShapeDtypeStruct((1, D), F32)],
        scratch_shapes=[pltpu.VMEM((D, DINP), F32)],
        compiler_params=_cp(("arbitrary",)),
    )(h, dh_in, gamma, w, *dparts)


GLA_ROWS = 512
GLA_NC = GLA_ROWS // CH


def _gla_consts():
    ri = lax.broadcasted_iota(jnp.int32, (CH, CH), 0)
    ci = lax.broadcasted_iota(jnp.int32, (CH, CH), 1)
    upper = (ci > ri).astype(BF16)
    vv = lax.broadcasted_iota(jnp.int32, (384, 256), 0)
    kk = lax.broadcasted_iota(jnp.int32, (384, 256), 1)
    mask_t = ((_group(vv, 96, 4) == _group(kk, 48, 4)) & (kk < 192)).astype(F32)
    pi = lax.broadcasted_iota(jnp.int32, (384, 384), 0)
    pj = lax.broadcasted_iota(jnp.int32, (384, 384), 1)
    same_head = (_group(pi, 96, 4) == _group(pj, 96, 4)).astype(BF16)
    return upper, mask_t, same_head


def _gla_gate(lr_ref, wg_ref, bg_ref):
    z = _dot(lr_ref[...], wg_ref[...]) + bg_ref[...]
    la = (jnp.minimum(z, 0.0) - jnp.log(1.0 + jnp.exp(-jnp.abs(z)))) * (1.0 / GATE_TAU)
    return z, la


def _gla_chunk_decay(la_c, upper):
    hi, lo = _split2(la_c)
    dec = _dot(upper, hi) + _dot(upper, lo)
    end = jnp.sum(la_c, axis=0, keepdims=True)
    return jnp.exp(dec), jnp.exp(end)


def _head_mean(x, same_head):
    hi, lo = _split2(x)
    return (_dot(hi, same_head) + _dot(lo, same_head)) * (1.0 / 96.0)


def _gla_fwd(q, k, v, g, lr, wg, bg, gn, jobs=()):
    T = q.shape[0]
    nb = T // GLA_ROWS

    def body(q_ref, k_ref, v_ref, g_ref, lr_ref, wg_ref, bg_ref, gn_ref, y_ref, st_ref, s_scr, o_scr):
        upper, mask_t, same_head = _gla_consts()

        @pl.when(pl.program_id(0) == 0)
        def _():
            s_scr[...] = jnp.zeros_like(s_scr)

        _, la = _gla_gate(lr_ref, wg_ref, bg_ref)
        for c in range(GLA_NC):
            rs = slice(c * CH, (c + 1) * CH)
            w, a = _gla_chunk_decay(la[rs], upper)
            kd = (k_ref[rs, :].astype(F32) * w).astype(BF16)
            kv_t = _dot_tn(v_ref[rs, :], kd)
            s_new = s_scr[...] * a + kv_t * mask_t
            s_scr[...] = s_new
            sb = s_new.astype(BF16)
            st_ref[c] = sb
            qs = (q_ref[rs, :].astype(F32) * Q_SCALE).astype(BF16)
            o_scr[rs, :] = _dot_nt(qs, sb)
        o = o_scr[...]
        r = lax.rsqrt(_head_mean(o * o, same_head) + EPS)
        gf = g_ref[...].astype(F32)
        y_ref[...] = (o * r * gn_ref[...] * (gf * _sigmoid(gf))).astype(BF16)

    return _pcall(
        body, name="gla_fwd", grid=(nb,),
        in_specs=[_row_spec(GLA_ROWS, 256), _row_spec(GLA_ROWS, 256), _row_spec(GLA_ROWS, 384),
                  _row_spec(GLA_ROWS, 384), _row_spec(GLA_ROWS, 128),
                  _full_spec((128, 256)), _full_spec((1, 256)), _full_spec((1, 384))],
        out_specs=[_row_spec(GLA_ROWS, 384), pl.BlockSpec((GLA_NC, 384, 256), lambda i: (i, 0, 0))],
        out_shape=[jax.ShapeDtypeStruct((T, 384), BF16), jax.ShapeDtypeStruct((T // CH, 384, 256), BF16)],
        scratch_shapes=[pltpu.VMEM((384, 256), F32), pltpu.VMEM((GLA_ROWS, 384), F32)],
        sem=("arbitrary",), operands=(q, k, v, g, lr, wg, bg, gn), jobs=jobs)


def _gla_bwd(q, k, v, g, lr, states, dy, wg, bg, gn):
    T = q.shape[0]
    nb = T // GLA_ROWS

    def rev(s):
        return nb - 1 - s

    def body(q_ref, k_ref, v_ref, g_ref, lr_ref, st_ref, stp_ref, dy_ref, wg_ref, bg_ref, gn_ref,
             dq_ref, dk_ref, dv_ref, dg_ref, dlr_ref, dwg_ref, dbg_ref, dgn_ref,
             d_scr, an_scr, o_scr, do_scr, dla_scr):
        upper, mask_t, same_head = _gla_consts()
        s = pl.program_id(0)
        blk = rev(s)

        @pl.when(s == 0)
        def _():
            d_scr[...] = jnp.zeros_like(d_scr)
            an_scr[...] = jnp.zeros_like(an_scr)
            dwg_ref[...] = jnp.zeros_like(dwg_ref)
            dbg_ref[...] = jnp.zeros_like(dbg_ref)
            dgn_ref[...] = jnp.zeros_like(dgn_ref)

        z, la = _gla_gate(lr_ref, wg_ref, bg_ref)
        ws, as_, qss, kds = [], [], [], []
        for c in range(GLA_NC):
            rs = slice(c * CH, (c + 1) * CH)
            w, a = _gla_chunk_decay(la[rs], upper)
            ws.append(w)
            as_.append(a)
            qs = (q_ref[rs, :].astype(F32) * Q_SCALE).astype(BF16)
            qss.append(qs)
            kds.append((k_ref[rs, :].astype(F32) * w).astype(BF16))
            o_scr[rs, :] = _dot_nt(qs, st_ref[c])
        o = o_scr[...]
        r = lax.rsqrt(_head_mean(o * o, same_head) + EPS)
        on = o * r
        gf = g_ref[...].astype(F32)
        sg = _sigmoid(gf)
        si = gf * sg
        dyf = dy_ref[...].astype(F32)
        gn_ = gn_ref[...]
        dgn_ref[...] += jnp.sum(dyf * si * on, axis=0, keepdims=True)
        dg_ref[...] = (dyf * on * gn_ * (sg * (1.0 + gf * (1.0 - sg)))).astype(BF16)
        d_on = dyf * si * gn_
        do_scr[...] = r * (d_on - on * _head_mean(d_on * on, same_head))

        first = (blk > 0).astype(F32)
        for c in reversed(range(GLA_NC)):
            rs = slice(c * CH, (c + 1) * CH)
            dob = do_scr[rs, :].astype(BF16)
            sb = st_ref[c]
            if c > 0:
                s_prev = st_ref[c - 1].astype(F32)
            else:
                s_prev = stp_ref[0].astype(F32) * first
            dq_ref[rs, :] = (_dot(dob, sb) * Q_SCALE).astype(BF16)
            dt = d_scr[...] * an_scr[...] + _dot_tn(dob, qss[c]) * mask_t
            d_scr[...] = dt
            da = jnp.sum(dt * s_prev, axis=0, keepdims=True)
            db = dt.astype(BF16)
            dkd = _dot(v_ref[rs, :], db)
            dv_ref[rs, :] = _dot_nt(kds[c], db).astype(BF16)
            dk_ref[rs, :] = (dkd * ws[c]).astype(BF16)
            ddec = dkd * k_ref[rs, :].astype(F32) * ws[c]
            hi, lo = _split2(ddec)
            dla_scr[rs, :] = _dot_tn(upper, hi) + _dot_tn(upper, lo) + as_[c] * da
            an_scr[...] = as_[c]

        dz = dla_scr[...] * (1.0 - _sigmoid(z)) * (1.0 / GATE_TAU)
        dzb = dz.astype(BF16)
        dlr_ref[...] = _dot_nt(dzb, wg_ref[...]).astype(BF16)
        dwg_ref[...] += _dot_tn(lr_ref[...], dzb)
        dbg_ref[...] += jnp.sum(dz, axis=0, keepdims=True)

    def rspec(n):
        return pl.BlockSpec((GLA_ROWS, n), lambda s: (rev(s), 0))

    return pl.pallas_call(
        body, name="gla_bwd", grid=(nb,),
        in_specs=[rspec(256), rspec(256), rspec(384), rspec(384), rspec(128),
                  pl.BlockSpec((GLA_NC, 384, 256), lambda s: (rev(s), 0, 0)),
                  pl.BlockSpec((1, 384, 256), lambda s: (jnp.maximum(rev(s) * GLA_NC - 1, 0), 0, 0)),
                  rspec(384), _full_spec((128, 256)), _full_spec((1, 256)), _full_spec((1, 384))],
        out_specs=[rspec(256), rspec(256), rspec(384), rspec(384), rspec(128),
                   _full_spec((128, 256)), _full_spec((1, 256)), _full_spec((1, 384))],
        out_shape=[jax.ShapeDtypeStruct((T, 256), BF16), jax.ShapeDtypeStruct((T, 256), BF16),
                   jax.ShapeDtypeStruct((T, 384), BF16), jax.ShapeDtypeStruct((T, 384), BF16),
                   jax.ShapeDtypeStruct((T, 128), BF16),
                   jax.ShapeDtypeStruct((128, 256), F32), jax.ShapeDtypeStruct((1, 256), F32),
                   jax.ShapeDtypeStruct((1, 384), F32)],
        scratch_shapes=[pltpu.VMEM((384, 256), F32), pltpu.VMEM((1, 256), F32),
                        pltpu.VMEM((GLA_ROWS, 384), F32), pltpu.VMEM((GLA_ROWS, 384), F32),
                        pltpu.VMEM((GLA_ROWS, 256), F32)],
        compiler_params=_cp(("arbitrary",)),
    )(q, k, v, g, lr, states, states, dy, wg, bg, gn)


CONV_ROWS = 512
HALO = 32
SUBL = 8
CONV_SLAB = 32
PHASE_ROWS = CONV_ROWS + HALO - SUBL
FWD_SHIFT = tuple(HALO - (KCONV - 1) + j for j in range(KCONV))
BWD_SHIFT = tuple(KCONV - 1 - j for j in range(KCONV))


def _fill_phases(buf, ph):
    for f in range(1, SUBL):
        ph[f, 0:PHASE_ROWS, :] = buf[pl.ds(f, PHASE_ROWS), :]


def _tap(buf, ph, shift, r, n):
    f, base = shift % SUBL, shift - shift % SUBL
    src = buf if f == 0 else ph.at[f]
    return src[pl.ds(base + r, n), :]


def _taps_apply(w_ref, buf, ph, shifts, out):
    for r in range(0, CONV_ROWS, CONV_SLAB):
        acc = jnp.zeros((CONV_SLAB, 256), F32)
        for j in range(KCONV):
            acc = acc + w_ref[j:j + 1, :] * _tap(buf, ph, shifts[j], r, CONV_SLAB)
        out[r:r + CONV_SLAB, :] = acc


def _conv_scratch():
    return [pltpu.VMEM((CONV_ROWS + HALO, 256), F32), pltpu.VMEM((SUBL, CONV_ROWS + HALO, 256), F32),
            pltpu.VMEM((CONV_ROWS, 256), F32)]


def _conv_common(cu_ref, halo_ref, w_ref, b_ref, lg_ref, lb_ref, buf, ph, cbuf, blk):
    u = cu_ref[...].astype(F32)
    a = u[:, :256]
    sb = _sigmoid(u[:, 256:])
    uh = halo_ref[...].astype(F32)
    hh = uh[:, :256] * _sigmoid(uh[:, 256:]) * (blk > 0).astype(F32)
    buf[0:HALO, :] = hh
    buf[HALO:HALO + CONV_ROWS, :] = a * sb
    _fill_phases(buf, ph)
    _taps_apply(w_ref, buf, ph, FWD_SHIFT, cbuf)
    cc = cbuf[...] + b_ref[...]
    mu = jnp.mean(cc, axis=-1, keepdims=True)
    xc = cc - mu
    rstd = lax.rsqrt(jnp.mean(xc * xc, axis=-1, keepdims=True) + EPS)
    n = xc * rstd
    yln = n * lg_ref[...] + lb_ref[...]
    return a, sb, n, rstd, yln


def _conv_fwd(cu, w, b, lg, lb, jobs=()):
    T = cu.shape[0]
    nb = T // CONV_ROWS
    per = CONV_ROWS // HALO

    def body(cu_ref, halo_ref, w_ref, b_ref, lg_ref, lb_ref, y_ref, buf, ph, cbuf):
        _, _, _, _, yln = _conv_common(cu_ref, halo_ref, w_ref, b_ref, lg_ref, lb_ref, buf, ph, cbuf,
                                       pl.program_id(0))
        y_ref[...] = (yln * _sigmoid(yln)).astype(BF16)

    return _pcall(
        body, name="conv_fwd", grid=(nb,),
        in_specs=[_row_spec(CONV_ROWS, 512),
                  pl.BlockSpec((HALO, 512), lambda i: (jnp.maximum(i * per - 1, 0), 0)),
                  _full_spec((32, 256)), _full_spec((1, 256)), _full_spec((1, 256)), _full_spec((1, 256))],
        out_specs=[_row_spec(CONV_ROWS, 256)],
        out_shape=[jax.ShapeDtypeStruct((T, 256), BF16)],
        scratch_shapes=_conv_scratch(),
        sem=("arbitrary",), operands=(cu, cu, w, b, lg, lb), jobs=jobs)


def _conv_bwd(cu, dy, w, b, lg, lb, jobs=()):
    T = cu.shape[0]
    nb = T // CONV_ROWS
    per = CONV_ROWS // HALO

    def rev(s):
        return nb - 1 - s

    def body(cu_ref, halo_ref, dy_ref, w_ref, b_ref, lg_ref, lb_ref,
             dcu_ref, dw_ref, db_ref, dlg_ref, dlb_ref, buf, ph, cbuf, dcbuf, dph, carry):
        s = pl.program_id(0)

        @pl.when(s == 0)
        def _():
            carry[...] = jnp.zeros_like(carry)
            dw_ref[...] = jnp.zeros_like(dw_ref)
            db_ref[...] = jnp.zeros_like(db_ref)
            dlg_ref[...] = jnp.zeros_like(dlg_ref)
            dlb_ref[...] = jnp.zeros_like(dlb_ref)

        a, sb, n, rstd, yln = _conv_common(cu_ref, halo_ref, w_ref, b_ref, lg_ref, lb_ref, buf, ph, cbuf, rev(s))
        sg = _sigmoid(yln)
        dyln = dy_ref[...].astype(F32) * (sg * (1.0 + yln * (1.0 - sg)))
        dlg_ref[...] += jnp.sum(dyln * n, axis=0, keepdims=True)
        dlb_ref[...] += jnp.sum(dyln, axis=0, keepdims=True)
        dn = dyln * lg_ref[...]
        dc = rstd * (dn - jnp.mean(dn, axis=-1, keepdims=True) - n * jnp.mean(dn * n, axis=-1, keepdims=True))
        db_ref[...] += jnp.sum(dc, axis=0, keepdims=True)
        dcbuf[0:CONV_ROWS, :] = dc
        dcbuf[CONV_ROWS:CONV_ROWS + HALO, :] = carry[...]
        carry[...] = dc[0:HALO, :]
        _fill_phases(dcbuf, dph)
        for j in range(KCONV):
            acc = jnp.zeros((SUBL, 256), F32)
            for r in range(0, CONV_ROWS, 2 * CONV_SLAB):
                prod = dcbuf[r:r + 2 * CONV_SLAB, :] * _tap(buf, ph, FWD_SHIFT[j], r, 2 * CONV_SLAB)
                acc = acc + jnp.sum(prod.reshape(2 * CONV_SLAB // SUBL, SUBL, 256), axis=0)
            dw_ref[j:j + 1, :] += jnp.sum(acc, axis=0, keepdims=True)
        _taps_apply(w_ref, dcbuf, dph, BWD_SHIFT, cbuf)
        dhg = cbuf[...]
        dcu_ref[...] = jnp.concatenate([dhg * sb, dhg * a * sb * (1.0 - sb)], axis=1).astype(BF16)

    def rspec(n):
        return pl.BlockSpec((CONV_ROWS, n), lambda s: (rev(s), 0))

    return _pcall(
        body, name="conv_bwd", grid=(nb,),
        in_specs=[rspec(512),
                  pl.BlockSpec((HALO, 512), lambda s: (jnp.maximum(rev(s) * per - 1, 0), 0)),
                  rspec(256),
                  _full_spec((32, 256)), _full_spec((1, 256)), _full_spec((1, 256)), _full_spec((1, 256))],
        out_specs=[rspec(512), _full_spec((32, 256)), _full_spec((1, 256)), _full_spec((1, 256)),
                   _full_spec((1, 256))],
        out_shape=[jax.ShapeDtypeStruct((T, 512), BF16), jax.ShapeDtypeStruct((32, 256), F32),
                   jax.ShapeDtypeStruct((1, 256), F32), jax.ShapeDtypeStruct((1, 256), F32),
                   jax.ShapeDtypeStruct((1, 256), F32)],
        scratch_shapes=_conv_scratch() + [pltpu.VMEM((CONV_ROWS + HALO, 256), F32),
                                          pltpu.VMEM((SUBL, CONV_ROWS + HALO, 256), F32),
                                          pltpu.VMEM((HALO, 256), F32)],
        sem=("arbitrary",), operands=(cu, cu, dy, w, b, lg, lb), jobs=jobs)


def _rel_onehot_t(shift=0):
    r = lax.broadcasted_iota(jnp.int32, (384, RB_W), 0)
    n = lax.broadcasted_iota(jnp.int32, (384, RB_W), 1) - shift
    idx = jnp.clip(1024 - n, -128, 128) + 128
    return (idx == r).astype(BF16)


def _relbias_expand(rb):
    def body(rb_ref, out_ref):
        oh = _rel_onehot_t()
        hi, mid, lo = _split3(rb_ref[...])
        strip = _dot(hi, oh) + _dot(mid, oh) + _dot(lo, oh)
        qi = _group(lax.broadcasted_iota(jnp.int32, (AQ_BLK, AK_WIN), 0), CH, 4)
        kj = _group(lax.broadcasted_iota(jnp.int32, (AQ_BLK, AK_WIN), 1), CH, 12)
        valid = (kj >= qi) & (kj <= qi + 8)
        for hd in range(6):
            x = jnp.broadcast_to(strip[hd:hd + 1, :], (AQ_BLK, RB_W))
            xr = pltpu.roll(x, 0, 1, stride=1, stride_axis=0)
            out_ref[hd] = jnp.where(valid, xr[:, 512:512 + AK_WIN], NEG)

    return pl.pallas_call(
        body, name="relbias_expand",
        out_shape=jax.ShapeDtypeStruct((6, AQ_BLK, AK_WIN), F32),
        compiler_params=_cp(),
    )(rb)


def _relbias_grad(dbias):
    def body(db_ref, out_ref):
        oh = _rel_onehot_t(AQ_BLK - 1)
        ri = lax.broadcasted_iota(jnp.int32, (AQ_BLK, AQ_BLK), 0)
        ci = lax.broadcasted_iota(jnp.int32, (AQ_BLK, AQ_BLK), 1)
        flip = (ri + ci == AQ_BLK - 1).astype(BF16)
        rows = []
        for hd in range(6):
            hi, mid, lo = _split3(db_ref[hd])
            rev = _dot(flip, hi) + _dot(flip, mid) + _dot(flip, lo)
            x = jnp.concatenate([jnp.zeros((AQ_BLK, 512), F32), rev,
                                 jnp.zeros((AQ_BLK, RB_W - 512 - AK_WIN), F32)], axis=1)
            xr = pltpu.roll(x, 0, 1, stride=1, stride_axis=0)
            rows.append(jnp.sum(xr, axis=0, keepdims=True))
        rows.append(jnp.zeros((2, RB_W), F32))
        dstrip = jnp.concatenate(rows, axis=0)
        hi, mid, lo = _split3(dstrip)
        out_ref[...] = _dot_nt(hi, oh) + _dot_nt(mid, oh) + _dot_nt(lo, oh)

    return pl.pallas_call(
        body, name="relbias_grad",
        out_shape=jax.ShapeDtypeStruct((8, 384), F32),
        compiler_params=_cp(),
    )(dbias)


ATT_SLAB = 8


def _att_softmax_slab(s_scr, b_ref, hd, rows, first_key):
    kvalid = lax.broadcasted_iota(jnp.int32, (ATT_SLAB, AK_WIN), 1) >= first_key
    s = jnp.where(kvalid, s_scr[rows, :] + b_ref[hd, rows, :], NEG)
    m = jnp.max(s, axis=-1, keepdims=True)
    p = jnp.exp(s - m)
    return p * (1.0 / jnp.sum(p, axis=-1, keepdims=True))


def _att_first_key(i):
    return (8 - 4 * i) * CH


def _slab_rows(t):
    return pl.ds(t * ATT_SLAB, ATT_SLAB)


def _head_lanes(hd):
    return slice(hd * 64, (hd + 1) * 64)


WIN_BLKS = AK_WIN // AQ_BLK


def _head_tile(hd):
    return slice(hd * HEAD_PAD, (hd + 1) * HEAD_PAD)


def _win_cols(d):
    return slice(d * AQ_BLK, (d + 1) * AQ_BLK)


def _win_block(i, d):
    return jnp.maximum(i + d - WIN_LEFT, 0)


def _win_specs():
    return [pl.BlockSpec((AQ_BLK, ATT_WIDE), lambda i, d=d: (_win_block(i, d), 0)) for d in range(WIN_BLKS)]


def _att_fwd(q, k, v, bias, jobs=()):
    T = q.shape[0]
    nb = T // AQ_BLK

    def body(q_ref, k0, k1, k2, v0, v1, v2, b_ref, o_ref, s_scr):
        k_refs, v_refs = (k0, k1, k2), (v0, v1, v2)
        first_key = _att_first_key(pl.program_id(0))

        def scores(hd):
            q_h = q_ref[:, _head_tile(hd)] * A_SCALE
            for d in range(WIN_BLKS):
                s_scr[hd % 2, :, _win_cols(d)] = _dot_nt(q_h, k_refs[d][:, _head_tile(hd)])

        scores(0)
        for hd in range(ATT_HEADS):
            if hd + 1 < ATT_HEADS:
                scores(hd + 1)
            s_h = s_scr.at[hd % 2]
            for t in range(AQ_BLK // ATT_SLAB):
                rows = _slab_rows(t)
                s_h[rows, :] = _att_softmax_slab(s_h, b_ref, hd, rows, first_key)
            o_h = _dot(s_h[:, _win_cols(0)].astype(BF16), v_refs[0][:, _head_tile(hd)])
            for d in range(1, WIN_BLKS):
                o_h = o_h + _dot(s_h[:, _win_cols(d)].astype(BF16), v_refs[d][:, _head_tile(hd)])
            o_ref[:, _head_lanes(hd)] = o_h[:, :64].astype(BF16)

    return _pcall(
        body, name="att_fwd", grid=(nb,),
        in_specs=[_row_spec(AQ_BLK, ATT_WIDE)] + _win_specs() + _win_specs() + [_full_spec((6, AQ_BLK, AK_WIN))],
        out_specs=[_row_spec(AQ_BLK, 384)],
        out_shape=[jax.ShapeDtypeStruct((T, 384), BF16)],
        scratch_shapes=[pltpu.VMEM((2, AQ_BLK, AK_WIN), F32)],
        sem=("arbitrary",), operands=(q, k, k, k, v, v, v, bias), jobs=jobs)


def _att_bwd(q, k, v, bias, do, jobs=()):
    T = q.shape[0]
    nb = T // AQ_BLK

    def body(q_ref, k0, k1, k2, v0, v1, v2, b_ref, do_ref, dq_ref, dk_ref, dv_ref, db_ref, dk_acc, dv_acc,
             s_scr, dp_scr):
        k_refs, v_refs = (k0, k1, k2), (v0, v1, v2)
        i = pl.program_id(0)

        @pl.when(i == 0)
        def _():
            dk_acc[...] = jnp.zeros_like(dk_acc)
            dv_acc[...] = jnp.zeros_like(dv_acc)
            db_ref[...] = jnp.zeros_like(db_ref)

        first_key = _att_first_key(i)

        def scores(hd):
            q_h = q_ref[:, _head_tile(hd)] * A_SCALE
            do_h = do_ref[:, _head_tile(hd)]
            for d in range(WIN_BLKS):
                s_scr[hd % 2, :, _win_cols(d)] = _dot_nt(q_h, k_refs[d][:, _head_tile(hd)])
                dp_scr[hd % 2, :, _win_cols(d)] = _dot_nt(do_h, v_refs[d][:, _head_tile(hd)])

        scores(0)
        for hd in range(ATT_HEADS):
            if hd + 1 < ATT_HEADS:
                scores(hd + 1)
            s_h, dp_h = s_scr.at[hd % 2], dp_scr.at[hd % 2]
            for t in range(AQ_BLK // ATT_SLAB):
                rows = _slab_rows(t)
                p = _att_softmax_slab(s_h, b_ref, hd, rows, first_key)
                dp = dp_h[rows, :]
                ds = p * (dp - jnp.sum(p * dp, axis=-1, keepdims=True))
                db_ref[hd, rows, :] += ds
                s_h[rows, :] = p
                dp_h[rows, :] = ds
            q_h = q_ref[:, _head_tile(hd)] * A_SCALE
            do_h = do_ref[:, _head_tile(hd)]
            ls = _head_lanes(hd)
            dq_h = jnp.zeros((AQ_BLK, HEAD_PAD), F32)
            for d in range(WIN_BLKS):
                pb = s_h[:, _win_cols(d)].astype(BF16)
                dsb = dp_h[:, _win_cols(d)].astype(BF16)
                rows = pl.ds(pl.multiple_of(_win_block(i, d) * AQ_BLK, AQ_BLK), AQ_BLK)
                dv_acc[rows, ls] += _dot_tn(pb, do_h)[:, :64]
                dk_acc[rows, ls] += _dot_tn(dsb, q_h)[:, :64]
                dq_h = dq_h + _dot(dsb, k_refs[d][:, _head_tile(hd)])
            dq_ref[:, ls] = (dq_h[:, :64] * A_SCALE).astype(BF16)

        @pl.when(i == nb - 1)
        def _():
            dk_ref[...] = dk_acc[...].astype(BF16)
            dv_ref[...] = dv_acc[...].astype(BF16)

    return _pcall(
        body, name="att_bwd", grid=(nb,),
        in_specs=[_row_spec(AQ_BLK, ATT_WIDE)] + _win_specs() + _win_specs()
        + [_full_spec((6, AQ_BLK, AK_WIN)), _row_spec(AQ_BLK, ATT_WIDE)],
        out_specs=[_row_spec(AQ_BLK, 384), _full_spec((T, 384)), _full_spec((T, 384)),
                   _full_spec((6, AQ_BLK, AK_WIN))],
        out_shape=[jax.ShapeDtypeStruct((T, 384), BF16), jax.ShapeDtypeStruct((T, 384), BF16),
                   jax.ShapeDtypeStruct((T, 384), BF16), jax.ShapeDtypeStruct((6, AQ_BLK, AK_WIN), F32)],
        scratch_shapes=[pltpu.VMEM((T, 384), F32), pltpu.VMEM((T, 384), F32),
                        pltpu.VMEM((2, AQ_BLK, AK_WIN), F32), pltpu.VMEM((2, AQ_BLK, AK_WIN), F32)],
        sem=("arbitrary",), operands=(q, k, k, k, v, v, v, bias, do), jobs=jobs)


FF_BLK = 512
N_FF = 4096 // FF_BLK
MLP_SHARDS = 2


def _outproj_mlp_fwd(h, o_gla, o_conv, o_att, w_out, gamma, w_up, w_down, jobs=()):
    T = h.shape[0]
    tm = 512

    def body(h_ref, og_ref, oc_ref, oa_ref, wo_ref, g_ref, wu_ref, wd_ref, h1_ref, xt_ref, h2_ref, a_ref, acc, xn_ref):
        j = pl.program_id(1)

        @pl.when(j == 0)
        def _():
            wo = wo_ref[...]
            h1 = (h_ref[...] + _dot(og_ref[...], wo[0:384]) + _dot(oc_ref[...], wo[384:640])
                  + _dot(oa_ref[...], wo[640:1024]))
            h1_ref[...] = h1
            r = lax.rsqrt(jnp.mean(h1 * h1, axis=-1, keepdims=True) + EPS)
            xn = (h1 * r * g_ref[...]).astype(BF16)
            xn_ref[...] = xn
            xt_ref[...] = jnp.transpose(xn)
            acc[...] = h1

        xn_ = xn_ref[...]
        for s in range(MLP_SHARDS):
            a = jnp.maximum(_dot(xn_, wu_ref[s]), 0.0)
            a_ref[:, s * FF_BLK:(s + 1) * FF_BLK] = a.astype(BF16)
            acc[...] += _dot((a * a).astype(BF16), wd_ref[s])

        @pl.when(j == N_FF // MLP_SHARDS - 1)
        def _():
            h2_ref[...] = acc[...]

    row = lambda n: pl.BlockSpec((tm, n), lambda i, j: (i, 0))
    return _pcall(
        body, name="outproj_mlp_fwd", grid=(T // tm, N_FF // MLP_SHARDS),
        in_specs=[row(D), row(384), row(256), row(384),
                  pl.BlockSpec((D, D), lambda i, j: (0, 0)), pl.BlockSpec((1, D), lambda i, j: (0, 0)),
                  pl.BlockSpec((MLP_SHARDS, D, FF_BLK), lambda i, j: (j, 0, 0)),
                  pl.BlockSpec((MLP_SHARDS, FF_BLK, D), lambda i, j: (j, 0, 0))],
        out_specs=[row(D), pl.BlockSpec((D, tm), lambda i, j: (0, i)), row(D),
                   pl.BlockSpec((tm, MLP_SHARDS * FF_BLK), lambda i, j: (i, j))],
        out_shape=[jax.ShapeDtypeStruct((T, D), F32), jax.ShapeDtypeStruct((D, T), BF16),
                   jax.ShapeDtypeStruct((T, D), F32), jax.ShapeDtypeStruct((T, N_FF * FF_BLK), BF16)],
        scratch_shapes=[pltpu.VMEM((tm, D), F32), pltpu.VMEM((tm, D), BF16)],
        sem=("arbitrary", "arbitrary"), operands=(h, o_gla, o_conv, o_att, w_out, gamma, w_up, w_down), jobs=jobs)


def _mlp_bwd(xn2t, act, h1, dh2, dh2b, gamma, w_up, w_down, jobs=()):
    T = act.shape[0]
    tm = 512
    nt = T // tm
    ns = MLP_SHARDS
    nj = N_FF // ns
    last = nj - 1

    def body(xt_ref, a_ref, h1_ref, dy_ref, dyb_ref, g_ref, wu_ref, wd_ref, dh1_ref, dwu_ref, dwd_ref, dg_ref,
             dxn_acc, acc_u, acc_d):
        j = pl.program_id(0)
        i = pl.program_id(1)
        xt = xt_ref[...]
        dyb = dyb_ref[...]
        rows = pl.ds(pl.multiple_of(i * tm, tm), tm)

        @pl.when(i == 0)
        def _():
            acc_u[...] = jnp.zeros_like(acc_u)
            acc_d[...] = jnp.zeros_like(acc_d)

        @pl.when(j == 0)
        def _():
            dxn_acc[rows, :] = jnp.zeros((tm, D), F32)

        for s in range(ns):
            a = a_ref[:, s * FF_BLK:(s + 1) * FF_BLK].astype(F32)
            hh = (a * a).astype(BF16)
            du = (_dot_nt(dyb, wd_ref[s]) * (2.0 * a)).astype(BF16)
            acc_d[s] += _dot_tn(hh, dyb)
            acc_u[s] += _dot(xt, du)
            dxn_acc[rows, :] += _dot_nt(du, wu_ref[s])

        @pl.when(i == nt - 1)
        def _():
            for s in range(ns):
                dwu_ref[s, 0] = acc_u[s].astype(BF16)
                dwd_ref[s, 0] = acc_d[s].astype(BF16)

        @pl.when(j == last)
        def _():
            @pl.when(i == 0)
            def _():
                dg_ref[...] = jnp.zeros_like(dg_ref)

            h1 = h1_ref[...]
            r = lax.rsqrt(jnp.mean(h1 * h1, axis=-1, keepdims=True) + EPS)
            dx, dgam = _rms_bwd(dxn_acc[rows, :], h1, r, g_ref[...])
            dh1_ref[...] = dy_ref[...] + dx
            dg_ref[...] += dgam

    assert ns == 2
    late = lambda j, i: (jnp.where(j == last, i, 0), 0)
    return _pcall(
        body, name="mlp_bwd", grid=(nj, nt),
        in_specs=[pl.BlockSpec((D, tm), lambda j, i: (0, i)), pl.BlockSpec((tm, ns * FF_BLK), lambda j, i: (i, j)),
                  pl.BlockSpec((tm, D), late), pl.BlockSpec((tm, D), late),
                  pl.BlockSpec((tm, D), lambda j, i: (i, 0)), pl.BlockSpec((1, D), lambda j, i: (0, 0)),
                  pl.BlockSpec((ns, D, FF_BLK), lambda j, i: (j, 0, 0), pipeline_mode=pl.Buffered(1)),
                  pl.BlockSpec((ns, FF_BLK, D), lambda j, i: (j, 0, 0), pipeline_mode=pl.Buffered(1))],
        out_specs=[pl.BlockSpec((tm, D), late),
                   pl.BlockSpec((ns, 1, D, FF_BLK), lambda j, i: (0, j, 0, 0)),
                   pl.BlockSpec((ns, 1, FF_BLK, D), lambda j, i: (0, j, 0, 0)),
                   pl.BlockSpec((1, D), lambda j, i: (0, 0))],
        out_shape=[jax.ShapeDtypeStruct((T, D), F32), jax.ShapeDtypeStruct((2, 4, D, FF_BLK), BF16),
                   jax.ShapeDtypeStruct((2, 4, FF_BLK, D), BF16), jax.ShapeDtypeStruct((1, D), F32)],
        scratch_shapes=[pltpu.VMEM((T, D), F32), pltpu.VMEM((ns, D, FF_BLK), F32), pltpu.VMEM((ns, FF_BLK, D), F32)],
        sem=("arbitrary", "arbitrary"), operands=(xn2t, act, h1, dh2, dh2b, gamma, w_up, w_down), jobs=jobs)


def _outproj_bwd(dh1, o_gla, o_conv, o_att, w_out, jobs=()):
    T = dh1.shape[0]
    tm = 512
    nt = T // tm

    def body(dy_ref, og_ref, oc_ref, oa_ref, wo_ref, dg_ref, dc_ref, da_ref, dw_ref, acc):
        i = pl.program_id(0)

        @pl.when(i == 0)
        def _():
            acc[...] = jnp.zeros_like(acc)

        dyb = dy_ref[...].astype(BF16)
        dm = _dot_nt(dyb, wo_ref[...])
        dg_ref[...] = dm[:, 0:384].astype(BF16)
        dc_ref[...] = dm[:, 384:640].astype(BF16)
        _store_head_padded(da_ref, dm[:, 640:1024].astype(BF16))
        mixed = jnp.concatenate([og_ref[...], oc_ref[...], oa_ref[...]], axis=1)
        acc[...] += _dot_tn(mixed, dyb)

        @pl.when(i == nt - 1)
        def _():
            for j in range(N_DEV):
                dw_ref[j % 2, j // 2] = acc[j * 128:(j + 1) * 128, :].astype(BF16)

    return _pcall(
        body, name="outproj_bwd", grid=(nt,),
        in_specs=[_row_spec(tm, D), _row_spec(tm, 384), _row_spec(tm, 256), _row_spec(tm, 384),
                  _full_spec((D, D))],
        out_specs=[_row_spec(tm, 384), _row_spec(tm, 256), _row_spec(tm, ATT_WIDE), _full_spec((2, 4, 128, D))],
        out_shape=[jax.ShapeDtypeStruct((T, 384), BF16), jax.ShapeDtypeStruct((T, 256), BF16),
                   jax.ShapeDtypeStruct((T, ATT_WIDE), BF16), jax.ShapeDtypeStruct((2, 4, 128, D), BF16)],
        scratch_shapes=[pltpu.VMEM((D, D), F32)],
        sem=("arbitrary",), operands=(dh1, o_gla, o_conv, o_att, w_out), jobs=jobs)


def _loss_fwd_bwd(h, gamma, target):
    T = h.shape[0]
    tm = 512

    def body(h_ref, g_ref, t_ref, loss_ref, dh_ref, dhb_ref, dg_ref):
        @pl.when(pl.program_id(0) == 0)
        def _():
            loss_ref[...] = jnp.zeros_like(loss_ref)
            dg_ref[...] = jnp.zeros_like(dg_ref)

        x = h_ref[...]
        r = lax.rsqrt(jnp.mean(x * x, axis=-1, keepdims=True) + EPS)
        gamma_ = g_ref[...]
        e = x * r * gamma_ - t_ref[...]
        loss_ref[...] += 0.5 * jnp.sum(jnp.mean(e * e, axis=-1, keepdims=True), axis=0, keepdims=True)
        dx, dgam = _rms_bwd(e * (1.0 / D), x, r, gamma_)
        dh_ref[...] = dx
        dhb_ref[...] = dx.astype(BF16)
        dg_ref[...] += dgam

    return pl.pallas_call(
        body, name="loss_fwd_bwd", grid=(T // tm,),
        in_specs=[_row_spec(tm, D), _full_spec((1, D)), _row_spec(tm, D)],
        out_specs=[_full_spec((8, 128)), _row_spec(tm, D), _row_spec(tm, D), _full_spec((1, D))],
        out_shape=[jax.ShapeDtypeStruct((8, 128), F32), jax.ShapeDtypeStruct((T, D), F32),
                   jax.ShapeDtypeStruct((T, D), BF16), jax.ShapeDtypeStruct((1, D), F32)],
        compiler_params=_cp(("arbitrary",)),
    )(h, gamma, target)


def _adamw_math(w, g, m, v):
    m = ADAM_B1 * m + (1.0 - ADAM_B1) * g
    v = ADAM_B2 * v + (1.0 - ADAM_B2) * (g * g)
    m_hat = m / (1.0 - ADAM_B1 ** ADAM_STEP)
    v_hat = v / (1.0 - ADAM_B2 ** ADAM_STEP)
    delta = -ADAM_LR * (m_hat / (jnp.sqrt(v_hat) + ADAM_EPS) + ADAM_WD * w)
    return delta, m, v


def _rs_adamw(a_own, r2, w, m, v, layer, chip_idx, rows_blk, prev=None):
    _, R, C = w.shape
    nblk = R // rows_blk

    def body(chip_ref, a_ref, r_ref, w_ref, m_ref, v_ref, *rest):
        g_out, d_out, m_out, v_out = rest[-4:]
        g = (a_ref[0].astype(F32) + r_ref[0].astype(F32)) + (r_ref[1].astype(F32) + r_ref[2].astype(F32))
        delta, m_new, v_new = _adamw_math(w_ref[0], g, m_ref[0], v_ref[0])
        g_out[0] = g
        d_out[0] = delta
        m_out[0] = m_new
        v_out[0] = v_new

    blk = pl.BlockSpec((1, rows_blk, C), lambda i, chip: (layer, i, 0))
    n_prev = 0 if prev is None else 4
    grid_spec = pltpu.PrefetchScalarGridSpec(
        num_scalar_prefetch=1, grid=(nblk,),
        in_specs=[pl.BlockSpec((1, rows_blk, C), lambda i, chip: (chip[0], i, 0)),
                  pl.BlockSpec((3, rows_blk, C), lambda i, chip: (0, i, 0)), blk, blk, blk]
        + [_any_spec()] * n_prev,
        out_specs=[blk, blk, blk, blk])
    return pl.pallas_call(
        body, name="rs_adamw", grid_spec=grid_spec,
        out_shape=[jax.ShapeDtypeStruct((DEPTH, R, C), F32)] * 4,
        input_output_aliases={6 + t: t for t in range(n_prev)},
        compiler_params=_cp(("arbitrary",)),
    )(chip_idx, a_own, r2, w, m, v, *(prev or ()))


def _pair_sum(g, r1, core_idx, rows_blk):
    _, _, R, C = g.shape
    nblk = R // rows_blk

    def body(core_ref, g_ref, r_ref, o_ref):
        o_ref[...] = (g_ref[0].astype(F32) + r_ref[...].astype(F32)).astype(BF16)

    grid_spec = pltpu.PrefetchScalarGridSpec(
        num_scalar_prefetch=1, grid=(4, nblk),
        in_specs=[pl.BlockSpec((1, 1, rows_blk, C), lambda k, i, core: (core[0], k, i, 0)),
                  pl.BlockSpec((1, rows_blk, C), lambda k, i, core: (k, i, 0))],
        out_specs=pl.BlockSpec((1, rows_blk, C), lambda k, i, core: (k, i, 0)))
    return pl.pallas_call(
        body, name="rs_pair_sum", grid_spec=grid_spec,
        out_shape=jax.ShapeDtypeStruct((4, R, C), BF16),
        compiler_params=_cp(("arbitrary", "arbitrary")),
    )(core_idx, g, r1)


def _small_sum(gathered):
    def body(g_ref, o_ref):
        acc = g_ref[0]
        for d in range(1, N_DEV):
            acc = acc + g_ref[d]
        o_ref[...] = acc

    return pl.pallas_call(
        body, name="small_sum",
        out_shape=jax.ShapeDtypeStruct(gathered.shape[1:], F32),
        compiler_params=_cp(),
    )(gathered)


def _adamw_small(ws, gs, ms, vs):
    n = len(ws)

    def body(*refs):
        w_r, g_r, m_r, v_r = refs[0:n], refs[n:2 * n], refs[2 * n:3 * n], refs[3 * n:4 * n]
        d_o, m_o, v_o = refs[4 * n:5 * n], refs[5 * n:6 * n], refs[6 * n:7 * n]
        for t in range(n):
            delta, m_new, v_new = _adamw_math(w_r[t][...], g_r[t][...], m_r[t][...], v_r[t][...])
            d_o[t][...] = delta
            m_o[t][...] = m_new
            v_o[t][...] = v_new

    shapes = [jax.ShapeDtypeStruct(w.shape, F32) for w in ws]
    outs = pl.pallas_call(
        body, name="adamw_small", out_shape=shapes * 3, compiler_params=_cp(),
    )(*ws, *gs, *ms, *vs)
    return outs[0:n], outs[n:2 * n], outs[2 * n:3 * n]


def _mesh_pos():
    return lax.axis_index("x"), lax.axis_index("y"), lax.axis_index("c")


def _peers():
    x, y, c = _mesh_pos()
    return (x, y, c), (x, y, 1 - c), [(1 - x, y), (x, 1 - y), (1 - x, 1 - y)]


def _slot(ref, pos):
    return ref.at[4 * pos[0] + 2 * pos[1] + pos[2]]


def _remote(src, dst, send_sem, recv_sem, to):
    return pltpu.make_async_remote_copy(src_ref=src, dst_ref=dst, send_sem=send_sem, recv_sem=recv_sem,
                                        device_id=to, device_id_type=MESH)


def _ag_spread(shards):
    n = len(shards)

    def copies(ins, outs, sems):
        send, recv, loc = sems
        me, sibling, chips = _peers()
        peers = [sibling] + [(*chip, me[2]) for chip in chips]
        local = [pltpu.make_async_copy(ins[a], _slot(outs[a], me), loc.at[a]) for a in range(n)]
        sends = [_remote(ins[a], _slot(outs[a], me), send.at[a, k], recv.at[a, k], p)
                 for a in range(n) for k, p in enumerate(peers)]
        recvs = [_remote(ins[a], _slot(outs[a], p), send.at[a, k], recv.at[a, k], p)
                 for a in range(n) for k, p in enumerate(peers)]
        return local, sends, recvs

    def start(ins, outs, sems):
        local, sends, _ = copies(ins, outs, sems)
        for cp in local + sends:
            cp.start()

    def finish(ins, outs, sems):
        local, sends, recvs = copies(ins, outs, sems)
        for cp in sends:
            cp.wait_send()
        for cp in recvs:
            cp.wait_recv()
        for cp in local:
            cp.wait()

    return _Job(shards, [jax.ShapeDtypeStruct((N_DEV,) + a.shape, a.dtype) for a in shards],
                [pltpu.SemaphoreType.DMA((n, 4)), pltpu.SemaphoreType.DMA((n, 4)), pltpu.SemaphoreType.DMA((n,))],
                start, finish)


def _ag_pass(stacks):
    n = len(stacks)

    def copies(ins, outs, sems):
        send, recv = sems
        me, sibling, chips = _peers()
        sends = [_remote(_slot(ins[a], (*chip, me[2])), _slot(outs[a], (*chip, me[2])), send.at[a, j], recv.at[a, j],
                         sibling) for a in range(n) for j, chip in enumerate(chips)]
        recvs = [_remote(_slot(ins[a], (*chip, me[2])), _slot(outs[a], (*chip, 1 - me[2])), send.at[a, j],
                         recv.at[a, j], sibling) for a in range(n) for j, chip in enumerate(chips)]
        return sends, recvs

    def start(ins, outs, sems):
        for cp in copies(ins, outs, sems)[0]:
            cp.start()

    def finish(ins, outs, sems):
        sends, recvs = copies(ins, outs, sems)
        for cp in sends:
            cp.wait_send()
        for cp in recvs:
            cp.wait_recv()

    return _Job(stacks, [jax.ShapeDtypeStruct(a.shape, a.dtype) for a in stacks],
                [pltpu.SemaphoreType.DMA((n, 3)), pltpu.SemaphoreType.DMA((n, 3))],
                start, finish, aliases={a: a for a in range(n)})


def _rs_swap(parts):
    n = len(parts)

    def copies(ins, outs, sems):
        send, recv = sems
        me, sibling, _ = _peers()
        return [_remote(ins[a].at[1 - me[2]], outs[a], send.at[a], recv.at[a], sibling) for a in range(n)]

    def start(ins, outs, sems):
        for cp in copies(ins, outs, sems):
            cp.start()

    def finish(ins, outs, sems):
        for cp in copies(ins, outs, sems):
            cp.wait()

    return _Job(parts, [jax.ShapeDtypeStruct(a.shape[1:], a.dtype) for a in parts],
                [pltpu.SemaphoreType.DMA((n,)), pltpu.SemaphoreType.DMA((n,))], start, finish)


def _rs_ici(pairs):
    n = len(pairs)

    def copies(ins, outs, sems):
        send, recv = sems
        me, _, chips = _peers()
        return [_remote(ins[a].at[2 * chip[0] + chip[1]], outs[a].at[j], send.at[a, j], recv.at[a, j],
                        (*chip, me[2])) for a in range(n) for j, chip in enumerate(chips)]

    def start(ins, outs, sems):
        for cp in copies(ins, outs, sems):
            cp.start()

    def finish(ins, outs, sems):
        for cp in copies(ins, outs, sems):
            cp.wait()

    return _Job(pairs, [jax.ShapeDtypeStruct((3,) + a.shape[1:], a.dtype) for a in pairs],
                [pltpu.SemaphoreType.DMA((n, 3)), pltpu.SemaphoreType.DMA((n, 3))], start, finish)


def _comm_call(jobs, name):
    def body():
        pass

    return _pcall(body, name=name, grid=(), in_specs=[], out_specs=[], out_shape=[], operands=(), jobs=jobs)[1]


def _allgather(arrs, name):
    n = len(arrs)

    def body(*refs):
        ins, outs = refs[:n], refs[n:2 * n]
        send_sems, recv_sems, local_sems = refs[2 * n:]
        x, y, c = _mesh_pos()
        me, sibling = (x, y, c), (x, y, 1 - c)
        chips = [(1 - x, y), (x, 1 - y), (1 - x, 1 - y)]

        def slot(a, pos):
            return outs[a].at[4 * pos[0] + 2 * pos[1] + pos[2]]

        def copy(a, k, block, to, src=None):
            return pltpu.make_async_remote_copy(
                src_ref=slot(a, block) if src is None else src, dst_ref=slot(a, block),
                send_sem=send_sems.at[a, k], recv_sem=recv_sems.at[a, k],
                device_id=to, device_id_type=MESH)

        mine = [pltpu.make_async_copy(ins[a], slot(a, me), local_sems.at[a]) for a in range(n)]
        for cp in mine:
            cp.start()
        first = []
        for a in range(n):
            first.append(copy(a, 0, me, sibling, src=ins[a]))
            first += [copy(a, 1 + j, me, (*chip, c), src=ins[a]) for j, chip in enumerate(chips)]
        for cp in first:
            cp.start()
        passed = []
        for j, chip in enumerate(chips):
            for a in range(n):
                copy(a, 1 + j, (*chip, c), me).wait_recv()
                fwd = copy(a, 4 + j, (*chip, c), sibling)
                fwd.start()
                passed.append(fwd)
        for a in range(n):
            copy(a, 0, sibling, me).wait_recv()
            for j, chip in enumerate(chips):
                copy(a, 4 + j, (*chip, 1 - c), me).wait_recv()
        for cp in first + passed:
            cp.wait_send()
        for cp in mine:
            cp.wait()

    return pl.pallas_call(
        body, name=name,
        in_specs=[_any_spec()] * n, out_specs=[_any_spec()] * n,
        out_shape=[jax.ShapeDtypeStruct((N_DEV,) + a.shape, a.dtype) for a in arrs],
        scratch_shapes=[pltpu.SemaphoreType.DMA((n, 7)), pltpu.SemaphoreType.DMA((n, 7)),
                        pltpu.SemaphoreType.DMA((n,))],
        compiler_params=_cp(),
    )(*arrs)


W_IN_SHARD = 354
W_IN_COLS = ((0, 192, OQ), (192, 192, OKK), (384, 384, OV), (768, 384, OG), (1152, 16, OLR), (1168, 512, OCU),
             (1680, 384, OAQ), (2064, 384, OAK), (2448, 384, OAV))


def _w_in_padded(stack):
    new_to_ref = {new: (start, width) for start, width, new in W_IN_COLS}
    cols = []
    for new, padded in IN_GROUPS:
        start, width = new_to_ref[new]
        a = start
        while a < start + width:
            j = a // W_IN_SHARD
            b = min(start + width, (j + 1) * W_IN_SHARD)
            cols.append(stack[j][:, a - j * W_IN_SHARD:b - j * W_IN_SHARD])
            a = b
        if padded > width:
            cols.append(jnp.zeros((stack.shape[1], padded - width), stack.dtype))
    return jnp.concatenate(cols, axis=1)


def _dw_in_shards(dw):
    shards = []
    for j in range(N_DEV):
        lo, hi = j * W_IN_SHARD, (j + 1) * W_IN_SHARD
        segs = []
        for start, width, new in W_IN_COLS:
            a, b = max(lo, start), min(hi, start + width)
            if a < b:
                segs.append(dw[:, new + a - start:new + b - start])
        shards.append(jnp.concatenate(segs, axis=1))
    return jnp.stack([jnp.stack([shards[2 * chip + core] for chip in range(4)]) for core in range(2)])


def _pad_to(a, shape):
    return jnp.pad(a, [(0, s - d) for d, s in zip(a.shape, shape)])


SMALL_LAYOUT = (
    ("norm_mix", 2, 1024), ("norm_ffn", 2, 1024), ("norm_final", 1, 1024), ("gla_norm", 2, 384),
    ("b_gla_gate", 2, 192), ("b_dw", 2, 256), ("conv_ln_g", 2, 256), ("conv_ln_b", 2, 256),
    ("rel_bias", 12, 257), ("w_gla_gate", 32, 192), ("w_dw", 62, 256),
)
SMALL_LANES = 128
SMALL_TILE = 8 * SMALL_LANES


def _small_tile_rows(r, lanes):
    return -(-(r * lanes) // SMALL_TILE) * 8


SMALL_ROWS = sum(_small_tile_rows(r, lanes) for _, r, lanes in SMALL_LAYOUT)


def _pack_small(parts):
    tiles = []
    for name, r, lanes in SMALL_LAYOUT:
        rows = _small_tile_rows(r, lanes)
        flat = _pad_to(parts[name].reshape(r * lanes), (rows * SMALL_LANES,))
        tiles.append(flat.reshape(rows, SMALL_LANES))
    return jnp.concatenate(tiles, axis=0)


def _unpack_small(packed):
    out, r0 = {}, 0
    for name, r, lanes in SMALL_LAYOUT:
        rows = _small_tile_rows(r, lanes)
        out[name] = packed[r0:r0 + rows].reshape(rows * SMALL_LANES)[:r * lanes].reshape(r, lanes)
        r0 += rows
    return out


def _mixers_fwd(h, wl, w_in_p, plan=None):
    plan, res = plan or {}, {}

    def jobs(host):
        return plan[host](res) if host in plan else ()

    (q, k, v, g, cu, aq, ak, av, lr), res["inproj"] = _inproj_fwd(h, wl["norm_mix"], w_in_p, jobs=jobs("inproj"))
    bias = _relbias_expand(wl["rb"])
    (o_att,), res["att"] = _att_fwd(aq, ak, av, bias, jobs=jobs("att"))
    (o_gla, states), res["gla"] = _gla_fwd(q, k, v, g, lr, wl["wg"], wl["bg"], wl["gn"], jobs=jobs("gla"))
    (o_conv,), res["conv"] = _conv_fwd(cu, wl["w_dw"], wl["b_dw"], wl["ln_g"], wl["ln_b"], jobs=jobs("conv"))
    sv = dict(h=h, w_in=w_in_p, q=q, k=k, v=v, g=g, cu=cu, aq=aq, ak=ak, av=av, lr=lr,
              o_gla=o_gla, o_conv=o_conv, o_att=o_att, states=states, bias=bias)
    return sv, res


def _mixers_bwd(sv, wl, dh1, d_ogla, d_oconv, att_grads, conv_jobs=()):
    daq, dak, dav, dbias = att_grads
    d_rb = _relbias_grad(dbias)
    (dcu, dw_dw, db_dw, dln_g, dln_b), conv_res = _conv_bwd(
        sv["cu"], d_oconv, wl["w_dw"], wl["b_dw"], wl["ln_g"], wl["ln_b"], jobs=conv_jobs)
    dq, dk, dv, dg, dlr, dwg, dbg, dgn = _gla_bwd(sv["q"], sv["k"], sv["v"], sv["g"], sv["lr"], sv["states"],
                                                  d_ogla, wl["wg"], wl["bg"], wl["gn"])
    dh, dhb, dw_in, d_nmix = _inproj_bwd(sv["h"], dh1, wl["norm_mix"], sv["w_in"],
                                         (dq, dk, dv, dg, dcu, daq, dak, dav, dlr))
    small = dict(norm_mix=d_nmix, wg=dwg, bg=dbg, gn=dgn, w_dw=dw_dw, b_dw=db_dw, ln_g=dln_g, ln_b=dln_b, rb=d_rb)
    return (dh, dhb), dw_in, small, conv_res


def _layer_small(l, w_dw_full, norm_mix, w_gla_gate, b_gla_gate, gla_norm, b_dw, conv_ln_g, conv_ln_b, rel_bias,
                 norm_ffn):
    return dict(
        norm_mix=norm_mix[l][None, :], norm_ffn=norm_ffn[l][None, :],
        wg=_pad_to(w_gla_gate[l], (128, 256)).astype(BF16), bg=_pad_to(b_gla_gate[l][None, :], (1, 256)),
        gn=gla_norm[l][None, :], w_dw=_pad_to(w_dw_full, (32, 256)), b_dw=b_dw[l][None, :],
        ln_g=conv_ln_g[l][None, :], ln_b=conv_ln_b[l][None, :], rb=_pad_to(rel_bias[l], (8, 384)))


RS_ROWS = dict(w_in=512, w_out=128, w_up=512, w_down=256)


def kernel(x, norm_mix, w_in, w_gla_gate, b_gla_gate, gla_norm, w_dw, b_dw, conv_ln_g, conv_ln_b, rel_bias, w_out, norm_ffn, w_up, w_down, norm_final, loss_target, m_norm_mix, m_w_in, m_w_gla_gate, m_b_gla_gate, m_gla_norm, m_w_dw, m_b_dw, m_conv_ln_g, m_conv_ln_b, m_rel_bias, m_w_out, m_norm_ffn, m_w_up, m_w_down, m_norm_final, v_norm_mix, v_w_in, v_w_gla_gate, v_b_gla_gate, v_gla_norm, v_w_dw, v_b_dw, v_conv_ln_g, v_conv_ln_b, v_rel_bias, v_w_out, v_norm_ffn, v_w_up, v_w_down, v_norm_final):
    mx, my, mc = _mesh_pos()
    me = 4 * mx + 2 * my + mc
    chip_idx = (2 * mx + my).astype(jnp.int32).reshape(1)
    core_idx = mc.astype(jnp.int32).reshape(1)
    x0, target = x[0], loss_target[0]

    def pair_sums(parts, r1):
        return [_pair_sum(p, r, core_idx, p.shape[2]) for p, r in zip(parts, r1)]

    sh = [dict(w_in=w_in[l].astype(BF16), w_out=w_out[l].astype(BF16), w_up=w_up[l].astype(BF16),
               w_down=w_down[l].astype(BF16)) for l in range(DEPTH)]
    dw_flat = _pad_to(w_dw, (DEPTH, 32, 32)).reshape(16, 128)
    st_in0, st_dw = _allgather([sh[0]["w_in"], dw_flat], "allgather_first")
    dw_all = st_dw.reshape(N_DEV, DEPTH, 32, 32)[:, :, :KCONV, :]
    dw_all = jnp.transpose(dw_all, (1, 2, 0, 3)).reshape(DEPTH, KCONV, 256)
    wl = [_layer_small(l, dw_all[l], norm_mix, w_gla_gate, b_gla_gate, gla_norm, b_dw, conv_ln_g, conv_ln_b,
                       rel_bias, norm_ffn) for l in range(DEPTH)]

    s0, s1 = sh[0], sh[1]
    sv0, g0 = _mixers_fwd(x0, wl[0], _w_in_padded(st_in0), plan=dict(
        inproj=lambda r: [_ag_spread([s0["w_out"], s1["w_in"]])],
        att=lambda r: [_ag_spread([s0["w_up"]]), _ag_pass(r["inproj"])],
        gla=lambda r: [_ag_spread([s0["w_down"]]), _ag_pass(r["att"][:1])],
        conv=lambda r: [_ag_spread([s1["w_out"]]), _ag_pass(r["gla"][:1])]))
    st_out0, st_in1 = g0["att"][1:]
    st_up0, st_down0 = g0["gla"][1], g0["conv"][1]
    wo0 = st_out0.reshape(D, D)
    (h1_0, xn2t_0, h2_0, act_0), (up1_half, down1_half, st_out1) = _outproj_mlp_fwd(
        x0, sv0["o_gla"], sv0["o_conv"], sv0["o_att"], wo0, wl[0]["norm_ffn"], st_up0, st_down0,
        jobs=[_ag_spread([s1["w_up"], s1["w_down"]]), _ag_pass(g0["conv"][:1])])

    sv1, g1 = _mixers_fwd(h2_0, wl[1], _w_in_padded(st_in1), plan=dict(
        att=lambda r: [_ag_pass([up1_half, down1_half])]))
    st_up1, st_down1 = g1["att"]
    wo1 = st_out1.reshape(D, D)
    (h1_1, xn2t_1, h2_1, act_1), _ = _outproj_mlp_fwd(
        h2_0, sv1["o_gla"], sv1["o_conv"], sv1["o_att"], wo1, wl[1]["norm_ffn"], st_up1, st_down1)

    loss8, dh, dhb, d_nf = _loss_fwd_bwd(h2_1, norm_final[None, :], target)
    loss = lax.psum(loss8[0, 0], ("x", "y", "c"))

    def layer_bwd(dh_pair, sv, wl_l, xn2t, act, h1, wo, st_up, st_down, mlp_jobs):
        (dh1, dw_up, dw_down, d_nffn), mlp_res = _mlp_bwd(xn2t, act, h1, dh_pair[0], dh_pair[1], wl_l["norm_ffn"],
                                                           st_up, st_down, jobs=mlp_jobs)
        ud = [dw_up, dw_down]
        (d_ogla, d_oconv, d_oatt, dw_out), r1 = _outproj_bwd(
            dh1, sv["o_gla"], sv["o_conv"], sv["o_att"], wo, jobs=[_rs_swap(ud)])
        pair_ud = pair_sums(ud, r1)
        att_grads, r = _att_bwd(sv["aq"], sv["ak"], sv["av"], sv["bias"], d_oatt,
                                jobs=[_rs_ici(pair_ud), _rs_swap([dw_out])])
        r2_ud, r1_out = r[:2], r[2:]
        pair_out = pair_sums([dw_out], r1_out)
        dh_in, dw_in, small, r2_out = _mixers_bwd(sv, wl_l, dh1, d_ogla, d_oconv, att_grads,
                                                   conv_jobs=[_rs_ici(pair_out)])
        small["norm_ffn"] = d_nffn
        sums = dict(w_out=(pair_out[0], r2_out[0]), w_up=(pair_ud[0], r2_ud[0]), w_down=(pair_ud[1], r2_ud[1]))
        return dh_in, dw_in, small, sums, mlp_res

    dh_pair, dw_in1, small1, sums1, _ = layer_bwd((dh, dhb), sv1, wl[1], xn2t_1, act_1, h1_1, wo1, st_up1, st_down1,
                                                  ())
    in1 = [_dw_in_shards(dw_in1)]
    pair_in1 = pair_sums(in1, _comm_call([_rs_swap(in1)], "rs_swap_w_in_1"))
    (dx, _), dw_in0, small0, sums0, r2_in1 = layer_bwd(dh_pair, sv0, wl[0], xn2t_0, act_0, h1_0, wo0, st_up0, st_down0,
                                                       [_rs_ici(pair_in1)])
    sums1["w_in"] = (pair_in1[0], r2_in1[0])

    in0 = [_dw_in_shards(dw_in0)]
    pair_in0 = pair_sums(in0, _comm_call([_rs_swap(in0)], "rs_swap_w_in_0"))
    sums0["w_in"] = (pair_in0[0], _comm_call([_rs_ici(pair_in0)], "rs_ici_w_in_0")[0])

    big_w = dict(w_in=(w_in, m_w_in, v_w_in), w_out=(w_out, m_w_out, v_w_out), w_up=(w_up, m_w_up, v_w_up),
                 w_down=(w_down, m_w_down, v_w_down))
    pairs = {1: sums1, 0: sums0}
    big_out = {}
    for name, (w_, m_, v_) in big_w.items():
        res = None
        for l in (1, 0):
            a_own, r2_ = pairs[l][name]
            res = _rs_adamw(a_own, r2_, w_, m_, v_, l, chip_idx, RS_ROWS[name], prev=res)
        big_out[name] = res

    grads = (small0, small1)
    parts = dict(
        norm_mix=jnp.concatenate([grads[l]["norm_mix"] for l in range(DEPTH)], axis=0),
        norm_ffn=jnp.concatenate([grads[l]["norm_ffn"] for l in range(DEPTH)], axis=0),
        norm_final=d_nf,
        gla_norm=jnp.concatenate([grads[l]["gn"] for l in range(DEPTH)], axis=0),
        b_gla_gate=jnp.concatenate([grads[l]["bg"][:, :192] for l in range(DEPTH)], axis=0),
        b_dw=jnp.concatenate([grads[l]["b_dw"] for l in range(DEPTH)], axis=0),
        conv_ln_g=jnp.concatenate([grads[l]["ln_g"] for l in range(DEPTH)], axis=0),
        conv_ln_b=jnp.concatenate([grads[l]["ln_b"] for l in range(DEPTH)], axis=0),
        rel_bias=jnp.concatenate([grads[l]["rb"][:6, :N_REL] for l in range(DEPTH)], axis=0),
        w_gla_gate=jnp.concatenate([grads[l]["wg"][:16, :192] for l in range(DEPTH)], axis=0),
        w_dw=jnp.concatenate([grads[l]["w_dw"][:KCONV] for l in range(DEPTH)], axis=0),
    )
    small_all = _allgather([_pack_small(parts)], "allgather_small")[0]
    sg = _unpack_small(_small_sum(small_all))
    dw_grad = lax.dynamic_slice_in_dim(sg["w_dw"].reshape(DEPTH, KCONV, 256), me * 32, 32, axis=2)
    small_g = dict(
        norm_mix=sg["norm_mix"], w_gla_gate=sg["w_gla_gate"].reshape(DEPTH, 16, 192), b_gla_gate=sg["b_gla_gate"],
        gla_norm=sg["gla_norm"], w_dw=dw_grad, b_dw=sg["b_dw"], conv_ln_g=sg["conv_ln_g"],
        conv_ln_b=sg["conv_ln_b"], rel_bias=sg["rel_bias"].reshape(DEPTH, 6, N_REL), norm_ffn=sg["norm_ffn"],
        norm_final=sg["norm_final"].reshape(D))
    small_names = ("norm_mix", "w_gla_gate", "b_gla_gate", "gla_norm", "w_dw", "b_dw", "conv_ln_g", "conv_ln_b",
                   "rel_bias", "norm_ffn", "norm_final")
    small_w = dict(norm_mix=norm_mix, w_gla_gate=w_gla_gate, b_gla_gate=b_gla_gate, gla_norm=gla_norm, w_dw=w_dw,
                   b_dw=b_dw, conv_ln_g=conv_ln_g, conv_ln_b=conv_ln_b, rel_bias=rel_bias, norm_ffn=norm_ffn,
                   norm_final=norm_final)
    small_m = dict(norm_mix=m_norm_mix, w_gla_gate=m_w_gla_gate, b_gla_gate=m_b_gla_gate, gla_norm=m_gla_norm,
                   w_dw=m_w_dw, b_dw=m_b_dw, conv_ln_g=m_conv_ln_g, conv_ln_b=m_conv_ln_b, rel_bias=m_rel_bias,
                   norm_ffn=m_norm_ffn, norm_final=m_norm_final)
    small_v = dict(norm_mix=v_norm_mix, w_gla_gate=v_w_gla_gate, b_gla_gate=v_b_gla_gate, gla_norm=v_gla_norm,
                   w_dw=v_w_dw, b_dw=v_b_dw, conv_ln_g=v_conv_ln_g, conv_ln_b=v_conv_ln_b, rel_bias=v_rel_bias,
                   norm_ffn=v_norm_ffn, norm_final=v_norm_final)
    s_delta, s_m, s_v = _adamw_small([small_w[n] for n in small_names], [small_g[n] for n in small_names],
                                     [small_m[n] for n in small_names], [small_v[n] for n in small_names])
    s_idx = {n: t for t, n in enumerate(small_names)}

    order = ("norm_mix", "w_in", "w_gla_gate", "b_gla_gate", "gla_norm", "w_dw", "b_dw", "conv_ln_g", "conv_ln_b",
             "rel_bias", "w_out", "norm_ffn", "w_up", "w_down", "norm_final")

    def pick(kind, name):
        if name in big_out:
            return big_out[name][kind]
        t = s_idx[name]
        return (small_g[name], s_delta[t], s_m[t], s_v[t])[kind]

    outs = [loss, dx[None]]
    for kind in range(4):
        outs += [pick(kind, n) for n in order]
    return tuple(outs)
```

```python
import functools

import jax
import jax.numpy as jnp
from jax import lax
from jax.experimental import pallas as pl
from jax.experimental.pallas import tpu as pltpu

F32 = jnp.float32
BF16 = jnp.bfloat16
MESH = pl.DeviceIdType.MESH

D = 1024
DEPTH = 2
CH = 64
EPS = 1e-6
NEG = -1e30
N_DEV = 8
N_REL = 257
Q_SCALE = 48.0 ** -0.5
A_SCALE = 64.0 ** -0.5
GATE_TAU = 16.0
KCONV = 31

OQ, OKK, OV, OG, OCU, OAQ, OAK, OAV, OLR, DINP = 0, 256, 512, 896, 1280, 1792, 2176, 2560, 2944, 3072
IN_GROUPS = ((OQ, 256), (OKK, 256), (OV, 384), (OG, 384), (OCU, 512), (OAQ, 384), (OAK, 384), (OAV, 384), (OLR, 128))

AQ_BLK = 256
AK_WIN = 768
WIN_LEFT = 2
RB_W = 1536

ADAM_LR, ADAM_B1, ADAM_B2, ADAM_EPS, ADAM_WD, ADAM_STEP = 0.001, 0.9, 0.999, 1e-08, 0.01, 10


V7X_VMEM_MIB = 64
VMEM_LIMIT_MIB = V7X_VMEM_MIB - 1


def _cp(sem=None):
    kw = {"vmem_limit_bytes": VMEM_LIMIT_MIB * 1024 * 1024}
    if sem is not None:
        kw["dimension_semantics"] = sem
    return pltpu.CompilerParams(**kw)


def _dot(a, b):
    return jnp.dot(a, b, preferred_element_type=F32)


def _dot_nt(a, b):
    return lax.dot_general(a, b, (((1,), (1,)), ((), ())), preferred_element_type=F32)


def _dot_tn(a, b):
    return lax.dot_general(a, b, (((0,), (0,)), ((), ())), preferred_element_type=F32)


def _split2(a):
    hi = a.astype(BF16)
    lo = (a - hi.astype(F32)).astype(BF16)
    return hi, lo


def _split3(a):
    hi = a.astype(BF16)
    r1 = a - hi.astype(F32)
    mid = r1.astype(BF16)
    lo = (r1 - mid.astype(F32)).astype(BF16)
    return hi, mid, lo


def _sigmoid(x):
    return 1.0 / (1.0 + jnp.exp(-x))


def _group(idx, size, n):
    g = jnp.zeros_like(idx)
    for t in range(1, n):
        g = g + (idx >= t * size).astype(jnp.int32)
    return g


def _rms_bwd(dy, x, r, gamma):
    xh = x * r
    dxh = dy * gamma
    dx = r * (dxh - xh * jnp.mean(dxh * xh, axis=-1, keepdims=True))
    return dx, jnp.sum(dy * xh, axis=0, keepdims=True)


def _row_spec(tm, n):
    return pl.BlockSpec((tm, n), lambda i: (i, 0))


def _full_spec(shape):
    nd = len(shape)
    return pl.BlockSpec(shape, lambda *_: (0,) * nd)


def _any_spec():
    return pl.BlockSpec(memory_space=pl.ANY)


class _Job:
    def __init__(self, operands, out_shapes, sems, start, finish, aliases=None):
        self.operands, self.out_shapes, self.sems = list(operands), list(out_shapes), list(sems)
        self.start, self.finish, self.aliases = start, finish, dict(aliases or {})


def _pcall(body, *, name, grid, in_specs, out_specs, out_shape, operands, scratch_shapes=(), sem=None, jobs=()):
    jobs = list(jobs)
    in_specs, out_specs, out_shape = list(in_specs), list(out_specs), list(out_shape)
    scratch_shapes = list(scratch_shapes)
    n_in, n_out, n_scr = len(in_specs), len(out_specs), len(scratch_shapes)
    j_in = [a for j in jobs for a in j.operands]
    j_out = [s for j in jobs for s in j.out_shapes]
    j_sem = [s for j in jobs for s in j.sems]
    aliases, io, oo = {}, n_in, n_out
    for j in jobs:
        for a, b in j.aliases.items():
            aliases[io + a] = oo + b
        io += len(j.operands)
        oo += len(j.out_shapes)

    def wrapped(*refs):
        own_in, ji = refs[:n_in], refs[n_in:n_in + len(j_in)]
        o0 = n_in + len(j_in)
        own_out, jo = refs[o0:o0 + n_out], refs[o0 + n_out:o0 + n_out + len(j_out)]
        s0 = o0 + n_out + len(j_out)
        own_scr, js = refs[s0:s0 + n_scr], refs[s0 + n_scr:]

        def each_job(fn_name):
            a = b = c = 0
            for j in jobs:
                na, nb, nc = len(j.operands), len(j.out_shapes), len(j.sems)
                getattr(j, fn_name)(ji[a:a + na], jo[b:b + nb], js[c:c + nc])
                a, b, c = a + na, b + nb, c + nc

        if jobs and grid:
            pids = [pl.program_id(d) for d in range(len(grid))]
            first = functools.reduce(jnp.logical_and, [p == 0 for p in pids])
            last = functools.reduce(jnp.logical_and, [p == g - 1 for p, g in zip(pids, grid)])
            pl.when(first)(lambda: each_job("start"))
        elif jobs:
            each_job("start")

        body(*own_in, *own_out, *own_scr)

        if jobs and grid:
            pl.when(last)(lambda: each_job("finish"))
        elif jobs:
            each_job("finish")

    res = pl.pallas_call(
        wrapped, name=name, grid=grid,
        in_specs=in_specs + [_any_spec()] * len(j_in), out_specs=out_specs + [_any_spec()] * len(j_out),
        out_shape=out_shape + j_out, scratch_shapes=scratch_shapes + j_sem,
        input_output_aliases=aliases, compiler_params=_cp(sem),
    )(*operands, *j_in)
    return res[:n_out], res[n_out:]


ATT_HEADS = 6
HEAD_PAD = 128
ATT_WIDE = ATT_HEADS * HEAD_PAD
ATT_GROUP_OFFS = (OAQ, OAK, OAV)


def _store_head_padded(o_ref, part):
    o_ref[...] = jnp.zeros_like(o_ref)
    for hd in range(ATT_HEADS):
        o_ref[:, hd * HEAD_PAD:hd * HEAD_PAD + 64] = part[:, hd * 64:(hd + 1) * 64]


def _inproj_fwd(h, gamma, w, jobs=()):
    T = h.shape[0]
    tm = 512

    def body(h_ref, g_ref, w_ref, *outs):
        x = h_ref[...]
        r = lax.rsqrt(jnp.mean(x * x, axis=-1, keepdims=True) + EPS)
        xn = (x * r * g_ref[...]).astype(BF16)
        p = _dot(xn, w_ref[...])
        for o_ref, (off, n) in zip(outs, IN_GROUPS):
            part = p[:, off:off + n].astype(BF16)
            if off in ATT_GROUP_OFFS:
                _store_head_padded(o_ref, part)
            else:
                o_ref[...] = part

    widths = [ATT_WIDE if off in ATT_GROUP_OFFS else n for off, n in IN_GROUPS]
    return _pcall(
        body, name="inproj_fwd", grid=(T // tm,),
        in_specs=[_row_spec(tm, D), _full_spec((1, D)), _full_spec((D, DINP))],
        out_specs=[_row_spec(tm, n) for n in widths],
        out_shape=[jax.ShapeDtypeStruct((T, n), BF16) for n in widths],
        sem=("arbitrary",), operands=(h, gamma, w), jobs=jobs)


def _inproj_norm(h_ref, g_ref):
    x = h_ref[...]
    r = lax.rsqrt(jnp.mean(x * x, axis=-1, keepdims=True) + EPS)
    return x, r, g_ref[...]


def _inproj_bwd_w(h, gamma, dparts):
    T = h.shape[0]
    tm = 512
    nt = T // tm

    def body(h_ref, g_ref, *rest):
        dp_refs = rest[:9]
        dw_ref, acc = rest[9:]
        i = pl.program_id(0)

        @pl.when(i == 0)
        def _():
            acc[...] = jnp.zeros_like(acc)

        x, r, gamma_ = _inproj_norm(h_ref, g_ref)
        xnt = jnp.transpose((x * r * gamma_).astype(BF16))
        for d_ref, (off, n) in zip(dp_refs, IN_GROUPS):
            acc[:, off:off + n] += _dot(xnt, d_ref[...])

        @pl.when(i == nt - 1)
        def _():
            dw_ref[...] = acc[...].astype(BF16)

    return pl.pallas_call(
        body, name="inproj_bwd_w", grid=(nt,),
        in_specs=[_row_spec(tm, D), _full_spec((1, D))] + [_row_spec(tm, n) for _, n in IN_GROUPS],
        out_specs=_full_spec((D, DINP)),
        out_shape=jax.ShapeDtypeStruct((D, DINP), BF16),
        scratch_shapes=[pltpu.VMEM((D, DINP), F32)],
        compiler_params=_cp(("arbitrary",)),
    )(h, gamma, *dparts)


def _inproj_bwd_x(h, dh_in, gamma, w, dparts, jobs=()):
    T = h.shape[0]
    tm = 512

    def body(h_ref, dhin_ref, g_ref, w_ref, *rest):
        dp_refs = rest[:9]
        dh_ref, dhb_ref, dg_ref = rest[9:]

        @pl.when(pl.program_id(0) == 0)
        def _():
            dg_ref[...] = jnp.zeros_like(dg_ref)

        x, r, gamma_ = _inproj_norm(h_ref, g_ref)
        dxn = None
        for d_ref, (off, n) in zip(dp_refs, IN_GROUPS):
            part = _dot_nt(d_ref[...], w_ref[:, off:off + n])
            dxn = part if dxn is None else dxn + part
        dx, dgam = _rms_bwd(dxn, x, r, gamma_)
        dh = dhin_ref[...] + dx
        dh_ref[...] = dh
        dhb_ref[...] = dh.astype(BF16)
        dg_ref[...] += dgam

    return _pcall(
        body, name="inproj_bwd_x", grid=(T // tm,),
        in_specs=[_row_spec(tm, D), _row_spec(tm, D), _full_spec((1, D)), _full_spec((D, DINP))]
        + [_row_spec(tm, n) for _, n in IN_GROUPS],
        out_specs=[_row_spec(tm, D), _row_spec(tm, D), _full_spec((1, D))],
        out_shape=[jax.ShapeDtypeStruct((T, D), F32), jax.ShapeDtypeStruct((T, D), BF16),
                   jax.ShapeDtypeStruct((1, D), F32)],
        sem=("arbitrary",), operands=(h, dh_in, gamma, w, *dparts), jobs=jobs)


GLA_ROWS = 512
GLA_NC = GLA_ROWS // CH


def _gla_consts():
    ri = lax.broadcasted_iota(jnp.int32, (CH, CH), 0)
    ci = lax.broadcasted_iota(jnp.int32, (CH, CH), 1)
    upper = (ci > ri).astype(BF16)
    vv = lax.broadcasted_iota(jnp.int32, (384, 256), 0)
    kk = lax.broadcasted_iota(jnp.int32, (384, 256), 1)
    mask_t = ((_group(vv, 96, 4) == _group(kk, 48, 4)) & (kk < 192)).astype(F32)
    pi = lax.broadcasted_iota(jnp.int32, (384, 384), 0)
    pj = lax.broadcasted_iota(jnp.int32, (384, 384), 1)
    same_head = (_group(pi, 96, 4) == _group(pj, 96, 4)).astype(BF16)
    return upper, mask_t, same_head


def _gla_gate(lr_ref, wg_ref, bg_ref):
    z = _dot(lr_ref[...], wg_ref[...]) + bg_ref[...]
    la = (jnp.minimum(z, 0.0) - jnp.log(1.0 + jnp.exp(-jnp.abs(z)))) * (1.0 / GATE_TAU)
    return z, la


def _gla_chunk_decay(la_c, upper):
    hi, lo = _split2(la_c)
    dec = _dot(upper, hi) + _dot(upper, lo)
    end = jnp.sum(la_c, axis=0, keepdims=True)
    return jnp.exp(dec), jnp.exp(end)


def _head_mean(x, same_head):
    hi, lo = _split2(x)
    return (_dot(hi, same_head) + _dot(lo, same_head)) * (1.0 / 96.0)


def _gla_fwd(q, k, v, g, lr, wg, bg, gn, jobs=()):
    T = q.shape[0]
    nb = T // GLA_ROWS

    def body(q_ref, k_ref, v_ref, g_ref, lr_ref, wg_ref, bg_ref, gn_ref, y_ref, st_ref, s_scr, o_scr, kv_scr):
        upper, mask_t, same_head = _gla_consts()

        @pl.when(pl.program_id(0) == 0)
        def _():
            s_scr[...] = jnp.zeros_like(s_scr)

        _, la = _gla_gate(lr_ref, wg_ref, bg_ref)
        decays = []
        for c in range(GLA_NC):
            rs = slice(c * CH, (c + 1) * CH)
            w, a = _gla_chunk_decay(la[rs], upper)
            decays.append(a)
            kd = (k_ref[rs, :].astype(F32) * w).astype(BF16)
            kv_scr[c] = _dot_tn(v_ref[rs, :], kd) * mask_t
        for c in range(GLA_NC):
            s_new = s_scr[...] * decays[c] + kv_scr[c]
            s_scr[...] = s_new
            st_ref[c] = s_new.astype(BF16)
        for c in range(GLA_NC):
            rs = slice(c * CH, (c + 1) * CH)
            qs = (q_ref[rs, :].astype(F32) * Q_SCALE).astype(BF16)
            o_scr[rs, :] = _dot_nt(qs, st_ref[c])
        o = o_scr[...]
        r = lax.rsqrt(_head_mean(o * o, same_head) + EPS)
        gf = g_ref[...].astype(F32)
        y_ref[...] = (o * r * gn_ref[...] * (gf * _sigmoid(gf))).astype(BF16)

    return _pcall(
        body, name="gla_fwd", grid=(nb,),
        in_specs=[_row_spec(GLA_ROWS, 256), _row_spec(GLA_ROWS, 256), _row_spec(GLA_ROWS, 384),
                  _row_spec(GLA_ROWS, 384), _row_spec(GLA_ROWS, 128),
                  _full_spec((128, 256)), _full_spec((1, 256)), _full_spec((1, 384))],
        out_specs=[_row_spec(GLA_ROWS, 384), pl.BlockSpec((GLA_NC, 384, 256), lambda i: (i, 0, 0))],
        out_shape=[jax.ShapeDtypeStruct((T, 384), BF16), jax.ShapeDtypeStruct((T // CH, 384, 256), BF16)],
        scratch_shapes=[pltpu.VMEM((384, 256), F32), pltpu.VMEM((GLA_ROWS, 384), F32),
                        pltpu.VMEM((GLA_NC, 384, 256), F32)],
        sem=("arbitrary",), operands=(q, k, v, g, lr, wg, bg, gn), jobs=jobs)


def _gla_bwd(q, k, v, g, lr, states, dy, wg, bg, gn):
    T = q.shape[0]
    nb = T // GLA_ROWS

    def rev(s):
        return nb - 1 - s

    def body(q_ref, k_ref, v_ref, g_ref, lr_ref, st_ref, stp_ref, dy_ref, wg_ref, bg_ref, gn_ref,
             dq_ref, dk_ref, dv_ref, dg_ref, dlr_ref, dwg_ref, dbg_ref, dgn_ref,
             d_scr, an_scr, o_scr, do_scr, dla_scr, dst_scr):
        upper, mask_t, same_head = _gla_consts()
        s = pl.program_id(0)
        blk = rev(s)

        @pl.when(s == 0)
        def _():
            d_scr[...] = jnp.zeros_like(d_scr)
            an_scr[...] = jnp.zeros_like(an_scr)
            dwg_ref[...] = jnp.zeros_like(dwg_ref)
            dbg_ref[...] = jnp.zeros_like(dbg_ref)
            dgn_ref[...] = jnp.zeros_like(dgn_ref)

        z, la = _gla_gate(lr_ref, wg_ref, bg_ref)
        ws, as_, qss, kds = [], [], [], []
        for c in range(GLA_NC):
            rs = slice(c * CH, (c + 1) * CH)
            w, a = _gla_chunk_decay(la[rs], upper)
            ws.append(w)
            as_.append(a)
            qs = (q_ref[rs, :].astype(F32) * Q_SCALE).astype(BF16)
            qss.append(qs)
            kds.append((k_ref[rs, :].astype(F32) * w).astype(BF16))
            o_scr[rs, :] = _dot_nt(qs, st_ref[c])
        o = o_scr[...]
        r = lax.rsqrt(_head_mean(o * o, same_head) + EPS)
        on = o * r
        gf = g_ref[...].astype(F32)
        sg = _sigmoid(gf)
        si = gf * sg
        dyf = dy_ref[...].astype(F32)
        gn_ = gn_ref[...]
        dgn_ref[...] += jnp.sum(dyf * si * on, axis=0, keepdims=True)
        dg_ref[...] = (dyf * on * gn_ * (sg * (1.0 + gf * (1.0 - sg)))).astype(BF16)
        d_on = dyf * si * gn_
        do_scr[...] = r * (d_on - on * _head_mean(d_on * on, same_head))

        for c in range(GLA_NC):
            rs = slice(c * CH, (c + 1) * CH)
            dst_scr[c] = _dot_tn(do_scr[rs, :].astype(BF16), qss[c]) * mask_t
        for c in reversed(range(GLA_NC)):
            dt = d_scr[...] * an_scr[...] + dst_scr[c]
            d_scr[...] = dt
            dst_scr[c] = dt
            an_scr[...] = as_[c]
        first = (blk > 0).astype(F32)
        for c in range(GLA_NC):
            rs = slice(c * CH, (c + 1) * CH)
            dob = do_scr[rs, :].astype(BF16)
            if c > 0:
                s_prev = st_ref[c - 1].astype(F32)
            else:
                s_prev = stp_ref[0].astype(F32) * first
            dq_ref[rs, :] = (_dot(dob, st_ref[c]) * Q_SCALE).astype(BF16)
            dt = dst_scr[c]
            da = jnp.sum(dt * s_prev, axis=0, keepdims=True)
            db = dt.astype(BF16)
            dkd = _dot(v_ref[rs, :], db)
            dv_ref[rs, :] = _dot_nt(kds[c], db).astype(BF16)
            dk_ref[rs, :] = (dkd * ws[c]).astype(BF16)
            ddec = dkd * k_ref[rs, :].astype(F32) * ws[c]
            hi, lo = _split2(ddec)
            dla_scr[rs, :] = _dot_tn(upper, hi) + _dot_tn(upper, lo) + as_[c] * da

        dz = dla_scr[...] * (1.0 - _sigmoid(z)) * (1.0 / GATE_TAU)
        dzb = dz.astype(BF16)
        dlr_ref[...] = _dot_nt(dzb, wg_ref[...]).astype(BF16)
        dwg_ref[...] += _dot_tn(lr_ref[...], dzb)
        dbg_ref[...] += jnp.sum(dz, axis=0, keepdims=True)

    def rspec(n):
        return pl.BlockSpec((GLA_ROWS, n), lambda s: (rev(s), 0))

    return pl.pallas_call(
        body, name="gla_bwd", grid=(nb,),
        in_specs=[rspec(256), rspec(256), rspec(384), rspec(384), rspec(128),
                  pl.BlockSpec((GLA_NC, 384, 256), lambda s: (rev(s), 0, 0)),
                  pl.BlockSpec((1, 384, 256), lambda s: (jnp.maximum(rev(s) * GLA_NC - 1, 0), 0, 0)),
                  rspec(384), _full_spec((128, 256)), _full_spec((1, 256)), _full_spec((1, 384))],
        out_specs=[rspec(256), rspec(256), rspec(384), rspec(384), rspec(128),
                   _full_spec((128, 256)), _full_spec((1, 256)), _full_spec((1, 384))],
        out_shape=[jax.ShapeDtypeStruct((T, 256), BF16), jax.ShapeDtypeStruct((T, 256), BF16),
                   jax.ShapeDtypeStruct((T, 384), BF16), jax.ShapeDtypeStruct((T, 384), BF16),
                   jax.ShapeDtypeStruct((T, 128), BF16),
                   jax.ShapeDtypeStruct((128, 256), F32), jax.ShapeDtypeStruct((1, 256), F32),
                   jax.ShapeDtypeStruct((1, 384), F32)],
        scratch_shapes=[pltpu.VMEM((384, 256), F32), pltpu.VMEM((1, 256), F32),
                        pltpu.VMEM((GLA_ROWS, 384), F32), pltpu.VMEM((GLA_ROWS, 384), F32),
                        pltpu.VMEM((GLA_ROWS, 256), F32), pltpu.VMEM((GLA_NC, 384, 256), F32)],
        compiler_params=_cp(("arbitrary",)),
    )(q, k, v, g, lr, states, states, dy, wg, bg, gn)


CONV_ROWS = 512
HALO = 32
SUBL = 8
CONV_SLAB = 32
PHASE_ROWS = CONV_ROWS + HALO - SUBL
FWD_SHIFT = tuple(HALO - (KCONV - 1) + j for j in range(KCONV))
BWD_SHIFT = tuple(KCONV - 1 - j for j in range(KCONV))


def _fill_phases(buf, ph):
    for f in range(1, SUBL):
        ph[f, 0:PHASE_ROWS, :] = buf[pl.ds(f, PHASE_ROWS), :]


def _tap(buf, ph, shift, r, n):
    f, base = shift % SUBL, shift - shift % SUBL
    src = buf if f == 0 else ph.at[f]
    return src[pl.ds(base + r, n), :]


def _taps_apply(w_ref, buf, ph, shifts, out):
    for r in range(0, CONV_ROWS, CONV_SLAB):
        acc = jnp.zeros((CONV_SLAB, 256), F32)
        for j in range(KCONV):
            acc = acc + w_ref[j:j + 1, :] * _tap(buf, ph, shifts[j], r, CONV_SLAB)
        out[r:r + CONV_SLAB, :] = acc


def _conv_scratch():
    return [pltpu.VMEM((CONV_ROWS + HALO, 256), F32), pltpu.VMEM((SUBL, CONV_ROWS + HALO, 256), F32),
            pltpu.VMEM((CONV_ROWS, 256), F32)]


def _conv_common(cu_ref, halo_ref, w_ref, b_ref, lg_ref, lb_ref, buf, ph, cbuf, blk):
    u = cu_ref[...].astype(F32)
    a = u[:, :256]
    sb = _sigmoid(u[:, 256:])
    uh = halo_ref[...].astype(F32)
    hh = uh[:, :256] * _sigmoid(uh[:, 256:]) * (blk > 0).astype(F32)
    buf[0:HALO, :] = hh
    buf[HALO:HALO + CONV_ROWS, :] = a * sb
    _fill_phases(buf, ph)
    _taps_apply(w_ref, buf, ph, FWD_SHIFT, cbuf)
    cc = cbuf[...] + b_ref[...]
    mu = jnp.mean(cc, axis=-1, keepdims=True)
    xc = cc - mu
    rstd = lax.rsqrt(jnp.mean(xc * xc, axis=-1, keepdims=True) + EPS)
    n = xc * rstd
    yln = n * lg_ref[...] + lb_ref[...]
    return a, sb, n, rstd, yln


def _conv_fwd(cu, w, b, lg, lb, jobs=()):
    T = cu.shape[0]
    nb = T // CONV_ROWS
    per = CONV_ROWS // HALO

    def body(cu_ref, halo_ref, w_ref, b_ref, lg_ref, lb_ref, y_ref, buf, ph, cbuf):
        _, _, _, _, yln = _conv_common(cu_ref, halo_ref, w_ref, b_ref, lg_ref, lb_ref, buf, ph, cbuf,
                                       pl.program_id(0))
        y_ref[...] = (yln * _sigmoid(yln)).astype(BF16)

    return _pcall(
        body, name="conv_fwd", grid=(nb,),
        in_specs=[_row_spec(CONV_ROWS, 512),
                  pl.BlockSpec((HALO, 512), lambda i: (jnp.maximum(i * per - 1, 0), 0)),
                  _full_spec((32, 256)), _full_spec((1, 256)), _full_spec((1, 256)), _full_spec((1, 256))],
        out_specs=[_row_spec(CONV_ROWS, 256)],
        out_shape=[jax.ShapeDtypeStruct((T, 256), BF16)],
        scratch_shapes=_conv_scratch(),
        sem=("arbitrary",), operands=(cu, cu, w, b, lg, lb), jobs=jobs)


def _conv_bwd(cu, dy, w, b, lg, lb, jobs=()):
    T = cu.shape[0]
    nb = T // CONV_ROWS
    per = CONV_ROWS // HALO

    def rev(s):
        return nb - 1 - s

    def body(cu_ref, halo_ref, dy_ref, w_ref, b_ref, lg_ref, lb_ref,
             dcu_ref, dw_ref, db_ref, dlg_ref, dlb_ref, buf, ph, cbuf, dcbuf, dph, carry):
        s = pl.program_id(0)

        @pl.when(s == 0)
        def _():
            carry[...] = jnp.zeros_like(carry)
            dw_ref[...] = jnp.zeros_like(dw_ref)
            db_ref[...] = jnp.zeros_like(db_ref)
            dlg_ref[...] = jnp.zeros_like(dlg_ref)
            dlb_ref[...] = jnp.zeros_like(dlb_ref)

        a, sb, n, rstd, yln = _conv_common(cu_ref, halo_ref, w_ref, b_ref, lg_ref, lb_ref, buf, ph, cbuf, rev(s))
        sg = _sigmoid(yln)
        dyln = dy_ref[...].astype(F32) * (sg * (1.0 + yln * (1.0 - sg)))
        dlg_ref[...] += jnp.sum(dyln * n, axis=0, keepdims=True)
        dlb_ref[...] += jnp.sum(dyln, axis=0, keepdims=True)
        dn = dyln * lg_ref[...]
        dc = rstd * (dn - jnp.mean(dn, axis=-1, keepdims=True) - n * jnp.mean(dn * n, axis=-1, keepdims=True))
        db_ref[...] += jnp.sum(dc, axis=0, keepdims=True)
        dcbuf[0:CONV_ROWS, :] = dc
        dcbuf[CONV_ROWS:CONV_ROWS + HALO, :] = carry[...]
        carry[...] = dc[0:HALO, :]
        _fill_phases(dcbuf, dph)
        for j in range(KCONV):
            acc = jnp.zeros((SUBL, 256), F32)
            for r in range(0, CONV_ROWS, 2 * CONV_SLAB):
                prod = dcbuf[r:r + 2 * CONV_SLAB, :] * _tap(buf, ph, FWD_SHIFT[j], r, 2 * CONV_SLAB)
                acc = acc + jnp.sum(prod.reshape(2 * CONV_SLAB // SUBL, SUBL, 256), axis=0)
            dw_ref[j:j + 1, :] += jnp.sum(acc, axis=0, keepdims=True)
        _taps_apply(w_ref, dcbuf, dph, BWD_SHIFT, cbuf)
        dhg = cbuf[...]
        dcu_ref[...] = jnp.concatenate([dhg * sb, dhg * a * sb * (1.0 - sb)], axis=1).astype(BF16)

    def rspec(n):
        return pl.BlockSpec((CONV_ROWS, n), lambda s: (rev(s), 0))

    return _pcall(
        body, name="conv_bwd", grid=(nb,),
        in_specs=[rspec(512),
                  pl.BlockSpec((HALO, 512), lambda s: (jnp.maximum(rev(s) * per - 1, 0), 0)),
                  rspec(256),
                  _full_spec((32, 256)), _full_spec((1, 256)), _full_spec((1, 256)), _full_spec((1, 256))],
        out_specs=[rspec(512), _full_spec((32, 256)), _full_spec((1, 256)), _full_spec((1, 256)),
                   _full_spec((1, 256))],
        out_shape=[jax.ShapeDtypeStruct((T, 512), BF16), jax.ShapeDtypeStruct((32, 256), F32),
                   jax.ShapeDtypeStruct((1, 256), F32), jax.ShapeDtypeStruct((1, 256), F32),
                   jax.ShapeDtypeStruct((1, 256), F32)],
        scratch_shapes=_conv_scratch() + [pltpu.VMEM((CONV_ROWS + HALO, 256), F32),
                                          pltpu.VMEM((SUBL, CONV_ROWS + HALO, 256), F32),
                                          pltpu.VMEM((HALO, 256), F32)],
        sem=("arbitrary",), operands=(cu, cu, dy, w, b, lg, lb), jobs=jobs)


def _rel_onehot_t(shift=0):
    r = lax.broadcasted_iota(jnp.int32, (384, RB_W), 0)
    n = lax.broadcasted_iota(jnp.int32, (384, RB_W), 1) - shift
    idx = jnp.clip(1024 - n, -128, 128) + 128
    return (idx == r).astype(BF16)


def _relbias_expand(rb):
    def body(rb_ref, out_ref):
        oh = _rel_onehot_t()
        hi, mid, lo = _split3(rb_ref[...])
        strip = _dot(hi, oh) + _dot(mid, oh) + _dot(lo, oh)
        qi = _group(lax.broadcasted_iota(jnp.int32, (AQ_BLK, AK_WIN), 0), CH, 4)
        kj = _group(lax.broadcasted_iota(jnp.int32, (AQ_BLK, AK_WIN), 1), CH, 12)
        valid = (kj >= qi) & (kj <= qi + 8)
        for hd in range(6):
            x = jnp.broadcast_to(strip[hd:hd + 1, :], (AQ_BLK, RB_W))
            xr = pltpu.roll(x, 0, 1, stride=1, stride_axis=0)
            out_ref[hd] = jnp.where(valid, xr[:, 512:512 + AK_WIN], NEG)

    return pl.pallas_call(
        body, name="relbias_expand",
        out_shape=jax.ShapeDtypeStruct((6, AQ_BLK, AK_WIN), F32),
        compiler_params=_cp(),
    )(rb)


def _relbias_grad(dbias):
    def body(db_ref, out_ref):
        oh = _rel_onehot_t(AQ_BLK - 1)
        ri = lax.broadcasted_iota(jnp.int32, (AQ_BLK, AQ_BLK), 0)
        ci = lax.broadcasted_iota(jnp.int32, (AQ_BLK, AQ_BLK), 1)
        flip = (ri + ci == AQ_BLK - 1).astype(BF16)
        rows = []
        for hd in range(6):
            hi, mid, lo = _split3(db_ref[hd])
            rev = _dot(flip, hi) + _dot(flip, mid) + _dot(flip, lo)
            x = jnp.concatenate([jnp.zeros((AQ_BLK, 512), F32), rev,
                                 jnp.zeros((AQ_BLK, RB_W - 512 - AK_WIN), F32)], axis=1)
            xr = pltpu.roll(x, 0, 1, stride=1, stride_axis=0)
            rows.append(jnp.sum(xr, axis=0, keepdims=True))
        rows.append(jnp.zeros((2, RB_W), F32))
        dstrip = jnp.concatenate(rows, axis=0)
        hi, mid, lo = _split3(dstrip)
        out_ref[...] = _dot_nt(hi, oh) + _dot_nt(mid, oh) + _dot_nt(lo, oh)

    return pl.pallas_call(
        body, name="relbias_grad",
        out_shape=jax.ShapeDtypeStruct((8, 384), F32),
        compiler_params=_cp(),
    )(dbias)


ATT_SLAB = 8


def _att_softmax_slab(s_scr, b_ref, hd, rows, first_key):
    kvalid = lax.broadcasted_iota(jnp.int32, (ATT_SLAB, AK_WIN), 1) >= first_key
    s = jnp.where(kvalid, s_scr[rows, :] + b_ref[hd, rows, :], NEG)
    m = jnp.max(s, axis=-1, keepdims=True)
    p = jnp.exp(s - m)
    return p * (1.0 / jnp.sum(p, axis=-1, keepdims=True))


def _att_first_key(i):
    return (8 - 4 * i) * CH


def _slab_rows(t):
    return pl.ds(t * ATT_SLAB, ATT_SLAB)


def _head_lanes(hd):
    return slice(hd * 64, (hd + 1) * 64)


WIN_BLKS = AK_WIN // AQ_BLK


def _head_tile(hd):
    return slice(hd * HEAD_PAD, (hd + 1) * HEAD_PAD)


def _win_cols(d):
    return slice(d * AQ_BLK, (d + 1) * AQ_BLK)


def _win_block(i, d):
    return jnp.maximum(i + d - WIN_LEFT, 0)


def _win_specs():
    return [pl.BlockSpec((AQ_BLK, ATT_WIDE), lambda i, d=d: (_win_block(i, d), 0)) for d in range(WIN_BLKS)]


def _att_fwd(q, k, v, bias, jobs=()):
    T = q.shape[0]
    nb = T // AQ_BLK

    def body(q_ref, k0, k1, k2, v0, v1, v2, b_ref, o_ref, s_scr):
        k_refs, v_refs = (k0, k1, k2), (v0, v1, v2)
        first_key = _att_first_key(pl.program_id(0))

        def scores(hd):
            q_h = q_ref[:, _head_tile(hd)] * A_SCALE
            for d in range(WIN_BLKS):
                s_scr[hd % 2, :, _win_cols(d)] = _dot_nt(q_h, k_refs[d][:, _head_tile(hd)])

        scores(0)
        for hd in range(ATT_HEADS):
            if hd + 1 < ATT_HEADS:
                scores(hd + 1)
            s_h = s_scr.at[hd % 2]
            for t in range(AQ_BLK // ATT_SLAB):
                rows = _slab_rows(t)
                s_h[rows, :] = _att_softmax_slab(s_h, b_ref, hd, rows, first_key)
            o_h = _dot(s_h[:, _win_cols(0)].astype(BF16), v_refs[0][:, _head_tile(hd)])
            for d in range(1, WIN_BLKS):
                o_h = o_h + _dot(s_h[:, _win_cols(d)].astype(BF16), v_refs[d][:, _head_tile(hd)])
            o_ref[:, _head_lanes(hd)] = o_h[:, :64].astype(BF16)

    return _pcall(
        body, name="att_fwd", grid=(nb,),
        in_specs=[_row_spec(AQ_BLK, ATT_WIDE)] + _win_specs() + _win_specs() + [_full_spec((6, AQ_BLK, AK_WIN))],
        out_specs=[_row_spec(AQ_BLK, 384)],
        out_shape=[jax.ShapeDtypeStruct((T, 384), BF16)],
        scratch_shapes=[pltpu.VMEM((2, AQ_BLK, AK_WIN), F32)],
        sem=("arbitrary",), operands=(q, k, k, k, v, v, v, bias), jobs=jobs)


def _att_bwd(q, k, v, bias, do, jobs=()):
    T = q.shape[0]
    nb = T // AQ_BLK

    def body(q_ref, k0, k1, k2, v0, v1, v2, b_ref, do_ref, dq_ref, dk_ref, dv_ref, db_ref, dk_acc, dv_acc,
             s_scr, dp_scr):
        k_refs, v_refs = (k0, k1, k2), (v0, v1, v2)
        i = pl.program_id(0)

        @pl.when(i == 0)
        def _():
            dk_acc[...] = jnp.zeros_like(dk_acc)
            dv_acc[...] = jnp.zeros_like(dv_acc)
            db_ref[...] = jnp.zeros_like(db_ref)

        first_key = _att_first_key(i)

        def scores(hd):
            q_h = q_ref[:, _head_tile(hd)] * A_SCALE
            do_h = do_ref[:, _head_tile(hd)]
            for d in range(WIN_BLKS):
                s_scr[hd % 2, :, _win_cols(d)] = _dot_nt(q_h, k_refs[d][:, _head_tile(hd)])
                dp_scr[hd % 2, :, _win_cols(d)] = _dot_nt(do_h, v_refs[d][:, _head_tile(hd)])

        scores(0)
        for hd in range(ATT_HEADS):
            if hd + 1 < ATT_HEADS:
                scores(hd + 1)
            s_h, dp_h = s_scr.at[hd % 2], dp_scr.at[hd % 2]
            for t in range(AQ_BLK // ATT_SLAB):
                rows = _slab_rows(t)
                p = _att_softmax_slab(s_h, b_ref, hd, rows, first_key)
                dp = dp_h[rows, :]
                ds = p * (dp - jnp.sum(p * dp, axis=-1, keepdims=True))
                db_ref[hd, rows, :] += ds
                s_h[rows, :] = p
                dp_h[rows, :] = ds
            q_h = q_ref[:, _head_tile(hd)] * A_SCALE
            do_h = do_ref[:, _head_tile(hd)]
            ls = _head_lanes(hd)
            dq_h = jnp.zeros((AQ_BLK, HEAD_PAD), F32)
            for d in range(WIN_BLKS):
                pb = s_h[:, _win_cols(d)].astype(BF16)
                dsb = dp_h[:, _win_cols(d)].astype(BF16)
                rows = pl.ds(pl.multiple_of(_win_block(i, d) * AQ_BLK, AQ_BLK), AQ_BLK)
                dv_acc[rows, ls] += _dot_tn(pb, do_h)[:, :64]
                dk_acc[rows, ls] += _dot_tn(dsb, q_h)[:, :64]
                dq_h = dq_h + _dot(dsb, k_refs[d][:, _head_tile(hd)])
            dq_ref[:, ls] = (dq_h[:, :64] * A_SCALE).astype(BF16)

        @pl.when(i == nb - 1)
        def _():
            dk_ref[...] = dk_acc[...].astype(BF16)
            dv_ref[...] = dv_acc[...].astype(BF16)

    return _pcall(
        body, name="att_bwd", grid=(nb,),
        in_specs=[_row_spec(AQ_BLK, ATT_WIDE)] + _win_specs() + _win_specs()
        + [_full_spec((6, AQ_BLK, AK_WIN)), _row_spec(AQ_BLK, ATT_WIDE)],
        out_specs=[_row_spec(AQ_BLK, 384), _full_spec((T, 384)), _full_spec((T, 384)),
                   _full_spec((6, AQ_BLK, AK_WIN))],
        out_shape=[jax.ShapeDtypeStruct((T, 384), BF16), jax.ShapeDtypeStruct((T, 384), BF16),
                   jax.ShapeDtypeStruct((T, 384), BF16), jax.ShapeDtypeStruct((6, AQ_BLK, AK_WIN), F32)],
        scratch_shapes=[pltpu.VMEM((T, 384), F32), pltpu.VMEM((T, 384), F32),
                        pltpu.VMEM((2, AQ_BLK, AK_WIN), F32), pltpu.VMEM((2, AQ_BLK, AK_WIN), F32)],
        sem=("arbitrary",), operands=(q, k, k, k, v, v, v, bias, do), jobs=jobs)


FF_BLK = 512
N_FF = 4096 // FF_BLK
MLP_SHARDS = 2


def _outproj_mlp_fwd(h, o_gla, o_conv, o_att, w_out, gamma, w_up, w_down, jobs=()):
    T = h.shape[0]
    tm = 512

    def body(h_ref, og_ref, oc_ref, oa_ref, wo_ref, g_ref, wu_ref, wd_ref, h1_ref, xt_ref, h2_ref, a_ref, acc, xn_ref):
        j = pl.program_id(1)

        @pl.when(j == 0)
        def _():
            wo = wo_ref[...]
            h1 = (h_ref[...] + _dot(og_ref[...], wo[0:384]) + _dot(oc_ref[...], wo[384:640])
                  + _dot(oa_ref[...], wo[640:1024]))
            h1_ref[...] = h1
            r = lax.rsqrt(jnp.mean(h1 * h1, axis=-1, keepdims=True) + EPS)
            xn = (h1 * r * g_ref[...]).astype(BF16)
            xn_ref[...] = xn
            xt_ref[...] = jnp.transpose(xn)
            acc[...] = h1

        xn_ = xn_ref[...]
        down = None
        for s in range(MLP_SHARDS):
            a = jnp.maximum(_dot(xn_, wu_ref[s]), 0.0)
            a_ref[:, s * FF_BLK:(s + 1) * FF_BLK] = a.astype(BF16)
            part = _dot((a * a).astype(BF16), wd_ref[s])
            down = part if down is None else down + part
        acc[...] += down

        @pl.when(j == N_FF // MLP_SHARDS - 1)
        def _():
            h2_ref[...] = acc[...]

    row = lambda n: pl.BlockSpec((tm, n), lambda i, j: (i, 0))
    return _pcall(
        body, name="outproj_mlp_fwd", grid=(T // tm, N_FF // MLP_SHARDS),
        in_specs=[row(D), row(384), row(256), row(384),
                  pl.BlockSpec((D, D), lambda i, j: (0, 0)), pl.BlockSpec((1, D), lambda i, j: (0, 0)),
                  pl.BlockSpec((MLP_SHARDS, D, FF_BLK), lambda i, j: (j, 0, 0)),
                  pl.BlockSpec((MLP_SHARDS, FF_BLK, D), lambda i, j: (j, 0, 0))],
        out_specs=[row(D), pl.BlockSpec((D, tm), lambda i, j: (0, i)), row(D),
                   pl.BlockSpec((tm, MLP_SHARDS * FF_BLK), lambda i, j: (i, j))],
        out_shape=[jax.ShapeDtypeStruct((T, D), F32), jax.ShapeDtypeStruct((D, T), BF16),
                   jax.ShapeDtypeStruct((T, D), F32), jax.ShapeDtypeStruct((T, N_FF * FF_BLK), BF16)],
        scratch_shapes=[pltpu.VMEM((tm, D), F32), pltpu.VMEM((tm, D), BF16)],
        sem=("arbitrary", "arbitrary"), operands=(h, o_gla, o_conv, o_att, w_out, gamma, w_up, w_down), jobs=jobs)


def _mlp_bwd(xn2t, act, h1, dh2, dh2b, gamma, w_up, w_down, jobs=()):
    T = act.shape[0]
    tm = 512
    nt = T // tm
    ns = MLP_SHARDS
    nj = N_FF // ns
    last = nj - 1

    def body(xt_ref, a_ref, h1_ref, dy_ref, dyb_ref, g_ref, wu_ref, wd_ref, dh1_ref, dwu_ref, dwd_ref, dg_ref,
             dxn_acc, acc_u, acc_d):
        j = pl.program_id(0)
        i = pl.program_id(1)
        xt = xt_ref[...]
        dyb = dyb_ref[...]
        rows = pl.ds(pl.multiple_of(i * tm, tm), tm)

        @pl.when(i == 0)
        def _():
            acc_u[...] = jnp.zeros_like(acc_u)
            acc_d[...] = jnp.zeros_like(acc_d)

        @pl.when(j == 0)
        def _():
            dxn_acc[rows, :] = jnp.zeros((tm, D), F32)

        dxn = None
        for s in range(ns):
            a = a_ref[:, s * FF_BLK:(s + 1) * FF_BLK].astype(F32)
            hh = (a * a).astype(BF16)
            du = (_dot_nt(dyb, wd_ref[s]) * (2.0 * a)).astype(BF16)
            acc_d[s] += _dot_tn(hh, dyb)
            acc_u[s] += _dot(xt, du)
            part = _dot_nt(du, wu_ref[s])
            dxn = part if dxn is None else dxn + part
        dxn_acc[rows, :] += dxn

        @pl.when(i == nt - 1)
        def _():
            for s in range(ns):
                dwu_ref[s, 0] = acc_u[s].astype(BF16)
                dwd_ref[s, 0] = acc_d[s].astype(BF16)

        @pl.when(j == last)
        def _():
            @pl.when(i == 0)
            def _():
                dg_ref[...] = jnp.zeros_like(dg_ref)

            h1 = h1_ref[...]
            r = lax.rsqrt(jnp.mean(h1 * h1, axis=-1, keepdims=True) + EPS)
            dx, dgam = _rms_bwd(dxn_acc[rows, :], h1, r, g_ref[...])
            dh1_ref[...] = dy_ref[...] + dx
            dg_ref[...] += dgam

    assert ns == 2
    late = lambda j, i: (jnp.where(j == last, i, 0), 0)
    return _pcall(
        body, name="mlp_bwd", grid=(nj, nt),
        in_specs=[pl.BlockSpec((D, tm), lambda j, i: (0, i)), pl.BlockSpec((tm, ns * FF_BLK), lambda j, i: (i, j)),
                  pl.BlockSpec((tm, D), late), pl.BlockSpec((tm, D), late),
                  pl.BlockSpec((tm, D), lambda j, i: (i, 0)), pl.BlockSpec((1, D), lambda j, i: (0, 0)),
                  pl.BlockSpec((ns, D, FF_BLK), lambda j, i: (j, 0, 0), pipeline_mode=pl.Buffered(1)),
                  pl.BlockSpec((ns, FF_BLK, D), lambda j, i: (j, 0, 0), pipeline_mode=pl.Buffered(1))],
        out_specs=[pl.BlockSpec((tm, D), late),
                   pl.BlockSpec((ns, 1, D, FF_BLK), lambda j, i: (0, j, 0, 0)),
                   pl.BlockSpec((ns, 1, FF_BLK, D), lambda j, i: (0, j, 0, 0)),
                   pl.BlockSpec((1, D), lambda j, i: (0, 0))],
        out_shape=[jax.ShapeDtypeStruct((T, D), F32), jax.ShapeDtypeStruct((2, 4, D, FF_BLK), BF16),
                   jax.ShapeDtypeStruct((2, 4, FF_BLK, D), BF16), jax.ShapeDtypeStruct((1, D), F32)],
        scratch_shapes=[pltpu.VMEM((T, D), F32), pltpu.VMEM((ns, D, FF_BLK), F32), pltpu.VMEM((ns, FF_BLK, D), F32)],
        sem=("arbitrary", "arbitrary"), operands=(xn2t, act, h1, dh2, dh2b, gamma, w_up, w_down), jobs=jobs)


def _outproj_bwd(dh1, o_gla, o_conv, o_att, w_out, jobs=()):
    T = dh1.shape[0]
    tm = 512
    nt = T // tm

    def body(dy_ref, og_ref, oc_ref, oa_ref, wo_ref, dg_ref, dc_ref, da_ref, dw_ref, acc):
        i = pl.program_id(0)

        @pl.when(i == 0)
        def _():
            acc[...] = jnp.zeros_like(acc)

        dyb = dy_ref[...].astype(BF16)
        dm = _dot_nt(dyb, wo_ref[...])
        dg_ref[...] = dm[:, 0:384].astype(BF16)
        dc_ref[...] = dm[:, 384:640].astype(BF16)
        _store_head_padded(da_ref, dm[:, 640:1024].astype(BF16))
        mixed = jnp.concatenate([og_ref[...], oc_ref[...], oa_ref[...]], axis=1)
        acc[...] += _dot_tn(mixed, dyb)

        @pl.when(i == nt - 1)
        def _():
            for j in range(N_DEV):
                dw_ref[j % 2, j // 2] = acc[j * 128:(j + 1) * 128, :].astype(BF16)

    return _pcall(
        body, name="outproj_bwd", grid=(nt,),
        in_specs=[_row_spec(tm, D), _row_spec(tm, 384), _row_spec(tm, 256), _row_spec(tm, 384),
                  _full_spec((D, D))],
        out_specs=[_row_spec(tm, 384), _row_spec(tm, 256), _row_spec(tm, ATT_WIDE), _full_spec((2, 4, 128, D))],
        out_shape=[jax.ShapeDtypeStruct((T, 384), BF16), jax.ShapeDtypeStruct((T, 256), BF16),
                   jax.ShapeDtypeStruct((T, ATT_WIDE), BF16), jax.ShapeDtypeStruct((2, 4, 128, D), BF16)],
        scratch_shapes=[pltpu.VMEM((D, D), F32)],
        sem=("arbitrary",), operands=(dh1, o_gla, o_conv, o_att, w_out), jobs=jobs)


def _loss_fwd_bwd(h, gamma, target):
    T = h.shape[0]
    tm = 512

    def body(h_ref, g_ref, t_ref, loss_ref, dh_ref, dhb_ref, dg_ref):
        @pl.when(pl.program_id(0) == 0)
        def _():
            loss_ref[...] = jnp.zeros_like(loss_ref)
            dg_ref[...] = jnp.zeros_like(dg_ref)

        x = h_ref[...]
        r = lax.rsqrt(jnp.mean(x * x, axis=-1, keepdims=True) + EPS)
        gamma_ = g_ref[...]
        e = x * r * gamma_ - t_ref[...]
        loss_ref[...] += 0.5 * jnp.sum(jnp.mean(e * e, axis=-1, keepdims=True), axis=0, keepdims=True)
        dx, dgam = _rms_bwd(e * (1.0 / D), x, r, gamma_)
        dh_ref[...] = dx
        dhb_ref[...] = dx.astype(BF16)
        dg_ref[...] += dgam

    return pl.pallas_call(
        body, name="loss_fwd_bwd", grid=(T // tm,),
        in_specs=[_row_spec(tm, D), _full_spec((1, D)), _row_spec(tm, D)],
        out_specs=[_full_spec((8, 128)), _row_spec(tm, D), _row_spec(tm, D), _full_spec((1, D))],
        out_shape=[jax.ShapeDtypeStruct((8, 128), F32), jax.ShapeDtypeStruct((T, D), F32),
                   jax.ShapeDtypeStruct((T, D), BF16), jax.ShapeDtypeStruct((1, D), F32)],
        compiler_params=_cp(("arbitrary",)),
    )(h, gamma, target)


def _adamw_math(w, g, m, v):
    m = ADAM_B1 * m + (1.0 - ADAM_B1) * g
    v = ADAM_B2 * v + (1.0 - ADAM_B2) * (g * g)
    m_hat = m / (1.0 - ADAM_B1 ** ADAM_STEP)
    v_hat = v / (1.0 - ADAM_B2 ** ADAM_STEP)
    delta = -ADAM_LR * (m_hat / (jnp.sqrt(v_hat) + ADAM_EPS) + ADAM_WD * w)
    return delta, m, v


def _rs_adamw(a_own, r2, w, m, v, layer, chip_idx, rows_blk, prev=None):
    _, R, C = w.shape
    nblk = R // rows_blk

    def body(chip_ref, a_ref, r_ref, w_ref, m_ref, v_ref, *rest):
        g_out, d_out, m_out, v_out = rest[-4:]
        g = (a_ref[0].astype(F32) + r_ref[0].astype(F32)) + (r_ref[1].astype(F32) + r_ref[2].astype(F32))
        delta, m_new, v_new = _adamw_math(w_ref[0], g, m_ref[0], v_ref[0])
        g_out[0] = g
        d_out[0] = delta
        m_out[0] = m_new
        v_out[0] = v_new

    blk = pl.BlockSpec((1, rows_blk, C), lambda i, chip: (layer, i, 0))
    n_prev = 0 if prev is None else 4
    grid_spec = pltpu.PrefetchScalarGridSpec(
        num_scalar_prefetch=1, grid=(nblk,),
        in_specs=[pl.BlockSpec((1, rows_blk, C), lambda i, chip: (chip[0], i, 0)),
                  pl.BlockSpec((3, rows_blk, C), lambda i, chip: (0, i, 0)), blk, blk, blk]
        + [_any_spec()] * n_prev,
        out_specs=[blk, blk, blk, blk])
    return pl.pallas_call(
        body, name="rs_adamw", grid_spec=grid_spec,
        out_shape=[jax.ShapeDtypeStruct((DEPTH, R, C), F32)] * 4,
        input_output_aliases={6 + t: t for t in range(n_prev)},
        compiler_params=_cp(("arbitrary",)),
    )(chip_idx, a_own, r2, w, m, v, *(prev or ()))


def _pair_sum(g, r1, core_idx, rows_blk):
    _, _, R, C = g.shape
    nblk = R // rows_blk

    def body(core_ref, g_ref, r_ref, o_ref):
        o_ref[...] = (g_ref[0].astype(F32) + r_ref[...].astype(F32)).astype(BF16)

    grid_spec = pltpu.PrefetchScalarGridSpec(
        num_scalar_prefetch=1, grid=(4, nblk),
        in_specs=[pl.BlockSpec((1, 1, rows_blk, C), lambda k, i, core: (core[0], k, i, 0)),
                  pl.BlockSpec((1, rows_blk, C), lambda k, i, core: (k, i, 0))],
        out_specs=pl.BlockSpec((1, rows_blk, C), lambda k, i, core: (k, i, 0)))
    return pl.pallas_call(
        body, name="rs_pair_sum", grid_spec=grid_spec,
        out_shape=jax.ShapeDtypeStruct((4, R, C), BF16),
        compiler_params=_cp(("arbitrary", "arbitrary")),
    )(core_idx, g, r1)


def _small_sum(gathered):
    def body(g_ref, o_ref):
        acc = g_ref[0]
        for d in range(1, N_DEV):
            acc = acc + g_ref[d]
        o_ref[...] = acc

    return pl.pallas_call(
        body, name="small_sum",
        out_shape=jax.ShapeDtypeStruct(gathered.shape[1:], F32),
        compiler_params=_cp(),
    )(gathered)


def _adamw_small(ws, gs, ms, vs):
    n = len(ws)

    def body(*refs):
        w_r, g_r, m_r, v_r = refs[0:n], refs[n:2 * n], refs[2 * n:3 * n], refs[3 * n:4 * n]
        d_o, m_o, v_o = refs[4 * n:5 * n], refs[5 * n:6 * n], refs[6 * n:7 * n]
        for t in range(n):
            delta, m_new, v_new = _adamw_math(w_r[t][...], g_r[t][...], m_r[t][...], v_r[t][...])
            d_o[t][...] = delta
            m_o[t][...] = m_new
            v_o[t][...] = v_new

    shapes = [jax.ShapeDtypeStruct(w.shape, F32) for w in ws]
    outs = pl.pallas_call(
        body, name="adamw_small", out_shape=shapes * 3, compiler_params=_cp(),
    )(*ws, *gs, *ms, *vs)
    return outs[0:n], outs[n:2 * n], outs[2 * n:3 * n]


def _mesh_pos():
    return lax.axis_index("x"), lax.axis_index("y"), lax.axis_index("c")


def _peers():
    x, y, c = _mesh_pos()
    return (x, y, c), (x, y, 1 - c), [(1 - x, y), (x, 1 - y), (1 - x, 1 - y)]


def _slot(ref, pos):
    return ref.at[4 * pos[0] + 2 * pos[1] + pos[2]]


def _remote(src, dst, send_sem, recv_sem, to):
    return pltpu.make_async_remote_copy(src_ref=src, dst_ref=dst, send_sem=send_sem, recv_sem=recv_sem,
                                        device_id=to, device_id_type=MESH)


def _ag_spread(shards):
    n = len(shards)

    def copies(ins, outs, sems):
        send, recv, loc = sems
        me, sibling, chips = _peers()
        peers = [sibling] + [(*chip, me[2]) for chip in chips]
        local = [pltpu.make_async_copy(ins[a], _slot(outs[a], me), loc.at[a]) for a in range(n)]
        sends = [_remote(ins[a], _slot(outs[a], me), send.at[a, k], recv.at[a, k], p)
                 for a in range(n) for k, p in enumerate(peers)]
        recvs = [_remote(ins[a], _slot(outs[a], p), send.at[a, k], recv.at[a, k], p)
                 for a in range(n) for k, p in enumerate(peers)]
        return local, sends, recvs

    def start(ins, outs, sems):
        local, sends, _ = copies(ins, outs, sems)
        for cp in local + sends:
            cp.start()

    def finish(ins, outs, sems):
        local, sends, recvs = copies(ins, outs, sems)
        for cp in sends:
            cp.wait_send()
        for cp in recvs:
            cp.wait_recv()
        for cp in local:
            cp.wait()

    return _Job(shards, [jax.ShapeDtypeStruct((N_DEV,) + a.shape, a.dtype) for a in shards],
                [pltpu.SemaphoreType.DMA((n, 4)), pltpu.SemaphoreType.DMA((n, 4)), pltpu.SemaphoreType.DMA((n,))],
                start, finish)


def _ag_pass(stacks):
    n = len(stacks)

    def copies(ins, outs, sems):
        send, recv = sems
        me, sibling, chips = _peers()
        sends = [_remote(_slot(ins[a], (*chip, me[2])), _slot(outs[a], (*chip, me[2])), send.at[a, j], recv.at[a, j],
                         sibling) for a in range(n) for j, chip in enumerate(chips)]
        recvs = [_remote(_slot(ins[a], (*chip, me[2])), _slot(outs[a], (*chip, 1 - me[2])), send.at[a, j],
                         recv.at[a, j], sibling) for a in range(n) for j, chip in enumerate(chips)]
        return sends, recvs

    def start(ins, outs, sems):
        for cp in copies(ins, outs, sems)[0]:
            cp.start()

    def finish(ins, outs, sems):
        sends, recvs = copies(ins, outs, sems)
        for cp in sends:
            cp.wait_send()
        for cp in recvs:
            cp.wait_recv()

    return _Job(stacks, [jax.ShapeDtypeStruct(a.shape, a.dtype) for a in stacks],
                [pltpu.SemaphoreType.DMA((n, 3)), pltpu.SemaphoreType.DMA((n, 3))],
                start, finish, aliases={a: a for a in range(n)})


def _rs_swap(parts):
    n = len(parts)

    def copies(ins, outs, sems):
        send, recv = sems
        me, sibling, _ = _peers()
        return [_remote(ins[a].at[1 - me[2]], outs[a], send.at[a], recv.at[a], sibling) for a in range(n)]

    def start(ins, outs, sems):
        for cp in copies(ins, outs, sems):
            cp.start()

    def finish(ins, outs, sems):
        for cp in copies(ins, outs, sems):
            cp.wait()

    return _Job(parts, [jax.ShapeDtypeStruct(a.shape[1:], a.dtype) for a in parts],
                [pltpu.SemaphoreType.DMA((n,)), pltpu.SemaphoreType.DMA((n,))], start, finish)


def _rs_ici(pairs):
    n = len(pairs)

    def copies(ins, outs, sems):
        send, recv = sems
        me, _, chips = _peers()
        return [_remote(ins[a].at[2 * chip[0] + chip[1]], outs[a].at[j], send.at[a, j], recv.at[a, j],
                        (*chip, me[2])) for a in range(n) for j, chip in enumerate(chips)]

    def start(ins, outs, sems):
        for cp in copies(ins, outs, sems):
            cp.start()

    def finish(ins, outs, sems):
        for cp in copies(ins, outs, sems):
            cp.wait()

    return _Job(pairs, [jax.ShapeDtypeStruct((3,) + a.shape[1:], a.dtype) for a in pairs],
                [pltpu.SemaphoreType.DMA((n, 3)), pltpu.SemaphoreType.DMA((n, 3))], start, finish)


def _comm_call(jobs, name):
    def body():
        pass

    return _pcall(body, name=name, grid=(), in_specs=[], out_specs=[], out_shape=[], operands=(), jobs=jobs)[1]


def _allgather(arrs, name):
    n = len(arrs)

    def body(*refs):
        ins, outs = refs[:n], refs[n:2 * n]
        send_sems, recv_sems, local_sems = refs[2 * n:]
        x, y, c = _mesh_pos()
        me, sibling = (x, y, c), (x, y, 1 - c)
        chips = [(1 - x, y), (x, 1 - y), (1 - x, 1 - y)]

        def slot(a, pos):
            return outs[a].at[4 * pos[0] + 2 * pos[1] + pos[2]]

        def copy(a, k, block, to, src=None):
            return pltpu.make_async_remote_copy(
                src_ref=slot(a, block) if src is None else src, dst_ref=slot(a, block),
                send_sem=send_sems.at[a, k], recv_sem=recv_sems.at[a, k],
                device_id=to, device_id_type=MESH)

        mine = [pltpu.make_async_copy(ins[a], slot(a, me), local_sems.at[a]) for a in range(n)]
        for cp in mine:
            cp.start()
        first = []
        for a in range(n):
            first.append(copy(a, 0, me, sibling, src=ins[a]))
            first += [copy(a, 1 + j, me, (*chip, c), src=ins[a]) for j, chip in enumerate(chips)]
        for cp in first:
            cp.start()
        passed = []
        for j, chip in enumerate(chips):
            for a in range(n):
                copy(a, 1 + j, (*chip, c), me).wait_recv()
                fwd = copy(a, 4 + j, (*chip, c), sibling)
                fwd.start()
                passed.append(fwd)
        for a in range(n):
            copy(a, 0, sibling, me).wait_recv()
            for j, chip in enumerate(chips):
                copy(a, 4 + j, (*chip, 1 - c), me).wait_recv()
        for cp in first + passed:
            cp.wait_send()
        for cp in mine:
            cp.wait()

    return pl.pallas_call(
        body, name=name,
        in_specs=[_any_spec()] * n, out_specs=[_any_spec()] * n,
        out_shape=[jax.ShapeDtypeStruct((N_DEV,) + a.shape, a.dtype) for a in arrs],
        scratch_shapes=[pltpu.SemaphoreType.DMA((n, 7)), pltpu.SemaphoreType.DMA((n, 7)),
                        pltpu.SemaphoreType.DMA((n,))],
        compiler_params=_cp(),
    )(*arrs)


W_IN_SHARD = 354
W_IN_COLS = ((0, 192, OQ), (192, 192, OKK), (384, 384, OV), (768, 384, OG), (1152, 16, OLR), (1168, 512, OCU),
             (1680, 384, OAQ), (2064, 384, OAK), (2448, 384, OAV))


def _w_in_padded(stack):
    new_to_ref = {new: (start, width) for start, width, new in W_IN_COLS}
    cols = []
    for new, padded in IN_GROUPS:
        start, width = new_to_ref[new]
        a = start
        while a < start + width:
            j = a // W_IN_SHARD
            b = min(start + width, (j + 1) * W_IN_SHARD)
            cols.append(stack[j][:, a - j * W_IN_SHARD:b - j * W_IN_SHARD])
            a = b
        if padded > width:
            cols.append(jnp.zeros((stack.shape[1], padded - width), stack.dtype))
    return jnp.concatenate(cols, axis=1)


def _dw_in_shards(dw):
    shards = []
    for j in range(N_DEV):
        lo, hi = j * W_IN_SHARD, (j + 1) * W_IN_SHARD
        segs = []
        for start, width, new in W_IN_COLS:
            a, b = max(lo, start), min(hi, start + width)
            if a < b:
                segs.append(dw[:, new + a - start:new + b - start])
        shards.append(jnp.concatenate(segs, axis=1))
    return jnp.stack([jnp.stack([shards[2 * chip + core] for chip in range(4)]) for core in range(2)])


def _pad_to(a, shape):
    return jnp.pad(a, [(0, s - d) for d, s in zip(a.shape, shape)])


SMALL_LAYOUT = (
    ("norm_mix", 2, 1024), ("norm_ffn", 2, 1024), ("norm_final", 1, 1024), ("gla_norm", 2, 384),
    ("b_gla_gate", 2, 192), ("b_dw", 2, 256), ("conv_ln_g", 2, 256), ("conv_ln_b", 2, 256),
    ("rel_bias", 12, 257), ("w_gla_gate", 32, 192), ("w_dw", 62, 256),
)
SMALL_LANES = 128
SMALL_TILE = 8 * SMALL_LANES


def _small_tile_rows(r, lanes):
    return -(-(r * lanes) // SMALL_TILE) * 8


SMALL_ROWS = sum(_small_tile_rows(r, lanes) for _, r, lanes in SMALL_LAYOUT)


def _pack_small(parts):
    tiles = []
    for name, r, lanes in SMALL_LAYOUT:
        rows = _small_tile_rows(r, lanes)
        flat = _pad_to(parts[name].reshape(r * lanes), (rows * SMALL_LANES,))
        tiles.append(flat.reshape(rows, SMALL_LANES))
    return jnp.concatenate(tiles, axis=0)


def _unpack_small(packed):
    out, r0 = {}, 0
    for name, r, lanes in SMALL_LAYOUT:
        rows = _small_tile_rows(r, lanes)
        out[name] = packed[r0:r0 + rows].reshape(rows * SMALL_LANES)[:r * lanes].reshape(r, lanes)
        r0 += rows
    return out


def _mixers_fwd(h, wl, w_in_p, plan=None):
    plan, res = plan or {}, {}

    def jobs(host):
        return plan[host](res) if host in plan else ()

    (q, k, v, g, cu, aq, ak, av, lr), res["inproj"] = _inproj_fwd(h, wl["norm_mix"], w_in_p, jobs=jobs("inproj"))
    bias = _relbias_expand(wl["rb"])
    (o_att,), res["att"] = _att_fwd(aq, ak, av, bias, jobs=jobs("att"))
    (o_gla, states), res["gla"] = _gla_fwd(q, k, v, g, lr, wl["wg"], wl["bg"], wl["gn"], jobs=jobs("gla"))
    (o_conv,), res["conv"] = _conv_fwd(cu, wl["w_dw"], wl["b_dw"], wl["ln_g"], wl["ln_b"], jobs=jobs("conv"))
    sv = dict(h=h, w_in=w_in_p, q=q, k=k, v=v, g=g, cu=cu, aq=aq, ak=ak, av=av, lr=lr,
              o_gla=o_gla, o_conv=o_conv, o_att=o_att, states=states, bias=bias)
    return sv, res


def _mixers_bwd(sv, wl, dh1, d_ogla, d_oconv, att_grads, conv_jobs=(), x_jobs_fn=None):
    daq, dak, dav, dbias = att_grads
    d_rb = _relbias_grad(dbias)
    (dcu, dw_dw, db_dw, dln_g, dln_b), conv_res = _conv_bwd(
        sv["cu"], d_oconv, wl["w_dw"], wl["b_dw"], wl["ln_g"], wl["ln_b"], jobs=conv_jobs)
    dq, dk, dv, dg, dlr, dwg, dbg, dgn = _gla_bwd(sv["q"], sv["k"], sv["v"], sv["g"], sv["lr"], sv["states"],
                                                  d_ogla, wl["wg"], wl["bg"], wl["gn"])
    dparts = (dq, dk, dv, dg, dcu, daq, dak, dav, dlr)
    dw_in = _inproj_bwd_w(sv["h"], wl["norm_mix"], dparts)
    x_jobs = x_jobs_fn(dw_in) if x_jobs_fn is not None else ()
    (dh, dhb, d_nmix), x_res = _inproj_bwd_x(sv["h"], dh1, wl["norm_mix"], sv["w_in"], dparts, jobs=x_jobs)
    small = dict(norm_mix=d_nmix, wg=dwg, bg=dbg, gn=dgn, w_dw=dw_dw, b_dw=db_dw, ln_g=dln_g, ln_b=dln_b, rb=d_rb)
    return (dh, dhb), dw_in, small, conv_res, x_res


def _layer_small(l, w_dw_full, norm_mix, w_gla_gate, b_gla_gate, gla_norm, b_dw, conv_ln_g, conv_ln_b, rel_bias,
                 norm_ffn):
    return dict(
        norm_mix=norm_mix[l][None, :], norm_ffn=norm_ffn[l][None, :],
        wg=_pad_to(w_gla_gate[l], (128, 256)).astype(BF16), bg=_pad_to(b_gla_gate[l][None, :], (1, 256)),
        gn=gla_norm[l][None, :], w_dw=_pad_to(w_dw_full, (32, 256)), b_dw=b_dw[l][None, :],
        ln_g=conv_ln_g[l][None, :], ln_b=conv_ln_b[l][None, :], rb=_pad_to(rel_bias[l], (8, 384)))


RS_ROWS = dict(w_in=512, w_out=128, w_up=512, w_down=256)


def kernel(x, norm_mix, w_in, w_gla_gate, b_gla_gate, gla_norm, w_dw, b_dw, conv_ln_g, conv_ln_b, rel_bias, w_out, norm_ffn, w_up, w_down, norm_final, loss_target, m_norm_mix, m_w_in, m_w_gla_gate, m_b_gla_gate, m_gla_norm, m_w_dw, m_b_dw, m_conv_ln_g, m_conv_ln_b, m_rel_bias, m_w_out, m_norm_ffn, m_w_up, m_w_down, m_norm_final, v_norm_mix, v_w_in, v_w_gla_gate, v_b_gla_gate, v_gla_norm, v_w_dw, v_b_dw, v_conv_ln_g, v_conv_ln_b, v_rel_bias, v_w_out, v_norm_ffn, v_w_up, v_w_down, v_norm_final):
    mx, my, mc = _mesh_pos()
    me = 4 * mx + 2 * my + mc
    chip_idx = (2 * mx + my).astype(jnp.int32).reshape(1)
    core_idx = mc.astype(jnp.int32).reshape(1)
    x0, target = x[0], loss_target[0]

    def pair_sums(parts, r1):
        return [_pair_sum(p, r, core_idx, p.shape[2]) for p, r in zip(parts, r1)]

    sh = [dict(w_in=w_in[l].astype(BF16), w_out=w_out[l].astype(BF16), w_up=w_up[l].astype(BF16),
               w_down=w_down[l].astype(BF16)) for l in range(DEPTH)]
    dw_flat = _pad_to(w_dw, (DEPTH, 32, 32)).reshape(16, 128)
    st_in0, st_dw = _allgather([sh[0]["w_in"], dw_flat], "allgather_first")
    dw_all = st_dw.reshape(N_DEV, DEPTH, 32, 32)[:, :, :KCONV, :]
    dw_all = jnp.transpose(dw_all, (1, 2, 0, 3)).reshape(DEPTH, KCONV, 256)
    wl = [_layer_small(l, dw_all[l], norm_mix, w_gla_gate, b_gla_gate, gla_norm, b_dw, conv_ln_g, conv_ln_b,
                       rel_bias, norm_ffn) for l in range(DEPTH)]

    s0, s1 = sh[0], sh[1]
    sv0, g0 = _mixers_fwd(x0, wl[0], _w_in_padded(st_in0), plan=dict(
        inproj=lambda r: [_ag_spread([s0["w_out"], s1["w_in"]])],
        att=lambda r: [_ag_spread([s0["w_up"]]), _ag_pass(r["inproj"])],
        gla=lambda r: [_ag_spread([s0["w_down"]]), _ag_pass(r["att"][:1])],
        conv=lambda r: [_ag_spread([s1["w_out"]]), _ag_pass(r["gla"][:1])]))
    st_out0, st_in1 = g0["att"][1:]
    st_up0, st_down0 = g0["gla"][1], g0["conv"][1]
    wo0 = st_out0.reshape(D, D)
    (h1_0, xn2t_0, h2_0, act_0), (up1_half, down1_half, st_out1) = _outproj_mlp_fwd(
        x0, sv0["o_gla"], sv0["o_conv"], sv0["o_att"], wo0, wl[0]["norm_ffn"], st_up0, st_down0,
        jobs=[_ag_spread([s1["w_up"], s1["w_down"]]), _ag_pass(g0["conv"][:1])])

    sv1, g1 = _mixers_fwd(h2_0, wl[1], _w_in_padded(st_in1), plan=dict(
        att=lambda r: [_ag_pass([up1_half, down1_half])]))
    st_up1, st_down1 = g1["att"]
    wo1 = st_out1.reshape(D, D)
    (h1_1, xn2t_1, h2_1, act_1), _ = _outproj_mlp_fwd(
        h2_0, sv1["o_gla"], sv1["o_conv"], sv1["o_att"], wo1, wl[1]["norm_ffn"], st_up1, st_down1)

    loss8, dh, dhb, d_nf = _loss_fwd_bwd(h2_1, norm_final[None, :], target)
    loss = lax.psum(loss8[0, 0], ("x", "y", "c"))

    def layer_bwd(dh_pair, sv, wl_l, xn2t, act, h1, wo, st_up, st_down, mlp_jobs, x_jobs_fn):
        (dh1, dw_up, dw_down, d_nffn), mlp_res = _mlp_bwd(xn2t, act, h1, dh_pair[0], dh_pair[1], wl_l["norm_ffn"],
                                                           st_up, st_down, jobs=mlp_jobs)
        ud = [dw_up, dw_down]
        (d_ogla, d_oconv, d_oatt, dw_out), r1 = _outproj_bwd(
            dh1, sv["o_gla"], sv["o_conv"], sv["o_att"], wo, jobs=[_rs_swap(ud)])
        pair_ud = pair_sums(ud, r1)
        att_grads, r = _att_bwd(sv["aq"], sv["ak"], sv["av"], sv["bias"], d_oatt,
                                jobs=[_rs_ici(pair_ud), _rs_swap([dw_out])])
        r2_ud, r1_out = r[:2], r[2:]
        pair_out = pair_sums([dw_out], r1_out)
        dh_in, _, small, r2_out, x_res = _mixers_bwd(sv, wl_l, dh1, d_ogla, d_oconv, att_grads,
                                                     conv_jobs=[_rs_ici(pair_out)], x_jobs_fn=x_jobs_fn)
        small["norm_ffn"] = d_nffn
        sums = dict(w_out=(pair_out[0], r2_out[0]), w_up=(pair_ud[0], r2_ud[0]), w_down=(pair_ud[1], r2_ud[1]))
        return dh_in, small, sums, mlp_res, x_res

    stash = {}

    def swap_w_in(dw_in):
        stash["in1"] = [_dw_in_shards(dw_in)]
        return [_rs_swap(stash["in1"])]

    dh_pair, small1, sums1, _, r1_in1 = layer_bwd((dh, dhb), sv1, wl[1], xn2t_1, act_1, h1_1, wo1, st_up1, st_down1,
                                                  (), swap_w_in)
    pair_in1 = pair_sums(stash["in1"], r1_in1)

    def send_w_in(dw_in):
        in0 = [_dw_in_shards(dw_in)]
        stash["pair_in0"] = pair_sums(in0, _comm_call([_rs_swap(in0)], "rs_swap_w_in_0"))
        return [_rs_ici(stash["pair_in0"])]

    (dx, _), small0, sums0, r2_in1, r2_in0 = layer_bwd(dh_pair, sv0, wl[0], xn2t_0, act_0, h1_0, wo0, st_up0, st_down0,
                                                       [_rs_ici(pair_in1)], send_w_in)
    sums1["w_in"] = (pair_in1[0], r2_in1[0])
    sums0["w_in"] = (stash["pair_in0"][0], r2_in0[0])

    big_w = dict(w_in=(w_in, m_w_in, v_w_in), w_out=(w_out, m_w_out, v_w_out), w_up=(w_up, m_w_up, v_w_up),
                 w_down=(w_down, m_w_down, v_w_down))
    pairs = {1: sums1, 0: sums0}
    big_out = {}
    for name, (w_, m_, v_) in big_w.items():
        res = None
        for l in (1, 0):
            a_own, r2_ = pairs[l][name]
            res = _rs_adamw(a_own, r2_, w_, m_, v_, l, chip_idx, RS_ROWS[name], prev=res)
        big_out[name] = res

    grads = (small0, small1)
    parts = dict(
        norm_mix=jnp.concatenate([grads[l]["norm_mix"] for l in range(DEPTH)], axis=0),
        norm_ffn=jnp.concatenate([grads[l]["norm_ffn"] for l in range(DEPTH)], axis=0),
        norm_final=d_nf,
        gla_norm=jnp.concatenate([grads[l]["gn"] for l in range(DEPTH)], axis=0),
        b_gla_gate=jnp.concatenate([grads[l]["bg"][:, :192] for l in range(DEPTH)], axis=0),
        b_dw=jnp.concatenate([grads[l]["b_dw"] for l in range(DEPTH)], axis=0),
        conv_ln_g=jnp.concatenate([grads[l]["ln_g"] for l in range(DEPTH)], axis=0),
        conv_ln_b=jnp.concatenate([grads[l]["ln_b"] for l in range(DEPTH)], axis=0),
        rel_bias=jnp.concatenate([grads[l]["rb"][:6, :N_REL] for l in range(DEPTH)], axis=0),
        w_gla_gate=jnp.concatenate([grads[l]["wg"][:16, :192] for l in range(DEPTH)], axis=0),
        w_dw=jnp.concatenate([grads[l]["w_dw"][:KCONV] for l in range(DEPTH)], axis=0),
    )
    small_all = _allgather([_pack_small(parts)], "allgather_small")[0]
    sg = _unpack_small(_small_sum(small_all))
    dw_grad = lax.dynamic_slice_in_dim(sg["w_dw"].reshape(DEPTH, KCONV, 256), me * 32, 32, axis=2)
    small_g = dict(
        norm_mix=sg["norm_mix"], w_gla_gate=sg["w_gla_gate"].reshape(DEPTH, 16, 192), b_gla_gate=sg["b_gla_gate"],
        gla_norm=sg["gla_norm"], w_dw=dw_grad, b_dw=sg["b_dw"], conv_ln_g=sg["conv_ln_g"],
        conv_ln_b=sg["conv_ln_b"], rel_bias=sg["rel_bias"].reshape(DEPTH, 6, N_REL), norm_ffn=sg["norm_ffn"],
        norm_final=sg["norm_final"].reshape(D))
    small_names = ("norm_mix", "w_gla_gate", "b_gla_gate", "gla_norm", "w_dw", "b_dw", "conv_ln_g", "conv_ln_b",
                   "rel_bias", "norm_ffn", "norm_final")
    small_w = dict(norm_mix=norm_mix, w_gla_gate=w_gla_gate, b_gla_gate=b_gla_gate, gla_norm=gla_norm, w_dw=w_dw,
                   b_dw=b_dw, conv_ln_g=conv_ln_g, conv_ln_b=conv_ln_b, rel_bias=rel_bias, norm_ffn=norm_ffn,
                   norm_final=norm_final)
    small_m = dict(norm_mix=m_norm_mix, w_gla_gate=m_w_gla_gate, b_gla_gate=m_b_gla_gate, gla_norm=m_gla_norm,
                   w_dw=m_w_dw, b_dw=m_b_dw, conv_ln_g=m_conv_ln_g, conv_ln_b=m_conv_ln_b, rel_bias=m_rel_bias,
                   norm_ffn=m_norm_ffn, norm_final=m_norm_final)
    small_v = dict(norm_mix=v_norm_mix, w_gla_gate=v_w_gla_gate, b_gla_gate=v_b_gla_gate, gla_norm=v_gla_norm,
                   w_dw=v_w_dw, b_dw=v_b_dw, conv_ln_g=v_conv_ln_g, conv_ln_b=v_conv_ln_b, rel_bias=v_rel_bias,
                   norm_ffn=v_norm_ffn, norm_final=v_norm_final)
    s_delta, s_m, s_v = _adamw_small([small_w[n] for n in small_names], [small_g[n] for n in small_names],
                                     [small_m[n] for n in small_names], [small_v[n] for n in small_names])
    s_idx = {n: t for t, n in enumerate(small_names)}

    order = ("norm_mix", "w_in", "w_gla_gate", "b_gla_gate", "gla_norm", "w_dw", "b_dw", "conv_ln_g", "conv_ln_b",
             "rel_bias", "w_out", "norm_ffn", "w_up", "w_down", "norm_final")

    def pick(kind, name):
        if name in big_out:
            return big_out[name][kind]
        t = s_idx[name]
        return (small_g[name], s_delta[t], s_m[t], s_v[t])[kind]

    outs = [loss, dx[None]]
    for kind in range(4):
        outs += [pick(kind, n) for n in order]
    return tuple(outs)
```

```python
import functools

import jax
import jax.numpy as jnp
from jax import lax
from jax.experimental import pallas as pl
from jax.experimental.pallas import tpu as pltpu

F32 = jnp.float32
BF16 = jnp.bfloat16
MESH = pl.DeviceIdType.MESH

D = 1024
DEPTH = 2
CH = 64
EPS = 1e-6
NEG = -1e30
N_DEV = 8
N_REL = 257
Q_SCALE = 48.0 ** -0.5
A_SCALE = 64.0 ** -0.5
GATE_TAU = 16.0
KCONV = 31

OQ, OKK, OV, OG, OCU, OAQ, OAK, OAV, OLR, DINP = 0, 256, 512, 896, 1280, 1792, 2176, 2560, 2944, 3072
IN_GROUPS = ((OQ, 256), (OKK, 256), (OV, 384), (OG, 384), (OCU, 512), (OAQ, 384), (OAK, 384), (OAV, 384), (OLR, 128))

AQ_BLK = 256
AK_WIN = 768
WIN_LEFT = 2
RB_W = 1536

ADAM_LR, ADAM_B1, ADAM_B2, ADAM_EPS, ADAM_WD, ADAM_STEP = 0.001, 0.9, 0.999, 1e-08, 0.01, 10


V7X_VMEM_MIB = 64
VMEM_LIMIT_MIB = V7X_VMEM_MIB - 1


def _cp(sem=None):
    kw = {"vmem_limit_bytes": VMEM_LIMIT_MIB * 1024 * 1024}
    if sem is not None:
        kw["dimension_semantics"] = sem
    return pltpu.CompilerParams(**kw)


def _dot(a, b):
    return jnp.dot(a, b, preferred_element_type=F32)


def _dot_nt(a, b):
    return lax.dot_general(a, b, (((1,), (1,)), ((), ())), preferred_element_type=F32)


def _dot_tn(a, b):
    return lax.dot_general(a, b, (((0,), (0,)), ((), ())), preferred_element_type=F32)


def _split2(a):
    hi = a.astype(BF16)
    lo = (a - hi.astype(F32)).astype(BF16)
    return hi, lo


def _split3(a):
    hi = a.astype(BF16)
    r1 = a - hi.astype(F32)
    mid = r1.astype(BF16)
    lo = (r1 - mid.astype(F32)).astype(BF16)
    return hi, mid, lo


def _sigmoid(x):
    return 1.0 / (1.0 + jnp.exp(-x))


def _group(idx, size, n):
    g = jnp.zeros_like(idx)
    for t in range(1, n):
        g = g + (idx >= t * size).astype(jnp.int32)
    return g


def _rms_bwd(dy, x, r, gamma):
    xh = x * r
    dxh = dy * gamma
    dx = r * (dxh - xh * jnp.mean(dxh * xh, axis=-1, keepdims=True))
    return dx, jnp.sum(dy * xh, axis=0, keepdims=True)


def _row_spec(tm, n):
    return pl.BlockSpec((tm, n), lambda i: (i, 0))


def _full_spec(shape):
    nd = len(shape)
    return pl.BlockSpec(shape, lambda *_: (0,) * nd)


def _any_spec():
    return pl.BlockSpec(memory_space=pl.ANY)


class _Job:
    def __init__(self, operands, out_shapes, sems, start, finish, aliases=None):
        self.operands, self.out_shapes, self.sems = list(operands), list(out_shapes), list(sems)
        self.start, self.finish, self.aliases = start, finish, dict(aliases or {})


def _pcall(body, *, name, grid, in_specs, out_specs, out_shape, operands, scratch_shapes=(), sem=None, jobs=()):
    jobs = list(jobs)
    in_specs, out_specs, out_shape = list(in_specs), list(out_specs), list(out_shape)
    scratch_shapes = list(scratch_shapes)
    n_in, n_out, n_scr = len(in_specs), len(out_specs), len(scratch_shapes)
    j_in = [a for j in jobs for a in j.operands]
    j_out = [s for j in jobs for s in j.out_shapes]
    j_sem = [s for j in jobs for s in j.sems]
    aliases, io, oo = {}, n_in, n_out
    for j in jobs:
        for a, b in j.aliases.items():
            aliases[io + a] = oo + b
        io += len(j.operands)
        oo += len(j.out_shapes)

    def wrapped(*refs):
        own_in, ji = refs[:n_in], refs[n_in:n_in + len(j_in)]
        o0 = n_in + len(j_in)
        own_out, jo = refs[o0:o0 + n_out], refs[o0 + n_out:o0 + n_out + len(j_out)]
        s0 = o0 + n_out + len(j_out)
        own_scr, js = refs[s0:s0 + n_scr], refs[s0 + n_scr:]

        def each_job(fn_name):
            a = b = c = 0
            for j in jobs:
                na, nb, nc = len(j.operands), len(j.out_shapes), len(j.sems)
                getattr(j, fn_name)(ji[a:a + na], jo[b:b + nb], js[c:c + nc])
                a, b, c = a + na, b + nb, c + nc

        if jobs and grid:
            pids = [pl.program_id(d) for d in range(len(grid))]
            first = functools.reduce(jnp.logical_and, [p == 0 for p in pids])
            last = functools.reduce(jnp.logical_and, [p == g - 1 for p, g in zip(pids, grid)])
            pl.when(first)(lambda: each_job("start"))
        elif jobs:
            each_job("start")

        body(*own_in, *own_out, *own_scr)

        if jobs and grid:
            pl.when(last)(lambda: each_job("finish"))
        elif jobs:
            each_job("finish")

    res = pl.pallas_call(
        wrapped, name=name, grid=grid,
        in_specs=in_specs + [_any_spec()] * len(j_in), out_specs=out_specs + [_any_spec()] * len(j_out),
        out_shape=out_shape + j_out, scratch_shapes=scratch_shapes + j_sem,
        input_output_aliases=aliases, compiler_params=_cp(sem),
    )(*operands, *j_in)
    return res[:n_out], res[n_out:]


ATT_HEADS = 6
HEAD_PAD = 128
ATT_WIDE = ATT_HEADS * HEAD_PAD
ATT_GROUP_OFFS = (OAQ, OAK, OAV)


def _store_head_padded(o_ref, part):
    o_ref[...] = jnp.zeros_like(o_ref)
    for hd in range(ATT_HEADS):
        o_ref[:, hd * HEAD_PAD:hd * HEAD_PAD + 64] = part[:, hd * 64:(hd + 1) * 64]


def _inproj_fwd(h, gamma, w, jobs=()):
    T = h.shape[0]
    tm = 512

    def body(h_ref, g_ref, w_ref, *outs):
        x = h_ref[...]
        r = lax.rsqrt(jnp.mean(x * x, axis=-1, keepdims=True) + EPS)
        xn = (x * r * g_ref[...]).astype(BF16)
        p = _dot(xn, w_ref[...])
        for o_ref, (off, n) in zip(outs, IN_GROUPS):
            part = p[:, off:off + n].astype(BF16)
            if off in ATT_GROUP_OFFS:
                _store_head_padded(o_ref, part)
            else:
                o_ref[...] = part

    widths = [ATT_WIDE if off in ATT_GROUP_OFFS else n for off, n in IN_GROUPS]
    return _pcall(
        body, name="inproj_fwd", grid=(T // tm,),
        in_specs=[_row_spec(tm, D), _full_spec((1, D)), _full_spec((D, DINP))],
        out_specs=[_row_spec(tm, n) for n in widths],
        out_shape=[jax.ShapeDtypeStruct((T, n), BF16) for n in widths],
        sem=("arbitrary",), operands=(h, gamma, w), jobs=jobs)


def _inproj_norm(h_ref, g_ref):
    x = h_ref[...]
    r = lax.rsqrt(jnp.mean(x * x, axis=-1, keepdims=True) + EPS)
    return x, r, g_ref[...]


def _inproj_bwd_w(h, gamma, dparts):
    T = h.shape[0]
    tm = 512
    nt = T // tm

    def body(h_ref, g_ref, *rest):
        dp_refs = rest[:9]
        dw_ref, acc = rest[9:]
        i = pl.program_id(0)

        @pl.when(i == 0)
        def _():
            acc[...] = jnp.zeros_like(acc)

        x, r, gamma_ = _inproj_norm(h_ref, g_ref)
        xnt = jnp.transpose((x * r * gamma_).astype(BF16))
        for d_ref, (off, n) in zip(dp_refs, IN_GROUPS):
            acc[:, off:off + n] += _dot(xnt, d_ref[...])

        @pl.when(i == nt - 1)
        def _():
            dw_ref[...] = acc[...].astype(BF16)

    return pl.pallas_call(
        body, name="inproj_bwd_w", grid=(nt,),
        in_specs=[_row_spec(tm, D), _full_spec((1, D))] + [_row_spec(tm, n) for _, n in IN_GROUPS],
        out_specs=_full_spec((D, DINP)),
        out_shape=jax.ShapeDtypeStruct((D, DINP), BF16),
        scratch_shapes=[pltpu.VMEM((D, DINP), F32)],
        compiler_params=_cp(("arbitrary",)),
    )(h, gamma, *dparts)


def _inproj_bwd_x(h, dh_in, gamma, w, dparts, jobs=()):
    T = h.shape[0]
    tm = 512

    def body(h_ref, dhin_ref, g_ref, w_ref, *rest):
        dp_refs = rest[:9]
        dh_ref, dhb_ref, dg_ref = rest[9:]

        @pl.when(pl.program_id(0) == 0)
        def _():
            dg_ref[...] = jnp.zeros_like(dg_ref)

        x, r, gamma_ = _inproj_norm(h_ref, g_ref)
        dxn = None
        for d_ref, (off, n) in zip(dp_refs, IN_GROUPS):
            part = _dot_nt(d_ref[...], w_ref[:, off:off + n])
            dxn = part if dxn is None else dxn + part
        dx, dgam = _rms_bwd(dxn, x, r, gamma_)
        dh = dhin_ref[...] + dx
        dh_ref[...] = dh
        dhb_ref[...] = dh.astype(BF16)
        dg_ref[...] += dgam

    return _pcall(
        body, name="inproj_bwd_x", grid=(T // tm,),
        in_specs=[_row_spec(tm, D), _row_spec(tm, D), _full_spec((1, D)), _full_spec((D, DINP))]
        + [_row_spec(tm, n) for _, n in IN_GROUPS],
        out_specs=[_row_spec(tm, D), _row_spec(tm, D), _full_spec((1, D))],
        out_shape=[jax.ShapeDtypeStruct((T, D), F32), jax.ShapeDtypeStruct((T, D), BF16),
                   jax.ShapeDtypeStruct((1, D), F32)],
        sem=("arbitrary",), operands=(h, dh_in, gamma, w, *dparts), jobs=jobs)


GLA_ROWS = 512
GLA_NC = GLA_ROWS // CH


def _gla_consts():
    ri = lax.broadcasted_iota(jnp.int32, (CH, CH), 0)
    ci = lax.broadcasted_iota(jnp.int32, (CH, CH), 1)
    upper = (ci > ri).astype(BF16)
    vv = lax.broadcasted_iota(jnp.int32, (384, 256), 0)
    kk = lax.broadcasted_iota(jnp.int32, (384, 256), 1)
    mask_t = ((_group(vv, 96, 4) == _group(kk, 48, 4)) & (kk < 192)).astype(F32)
    pi = lax.broadcasted_iota(jnp.int32, (384, 384), 0)
    pj = lax.broadcasted_iota(jnp.int32, (384, 384), 1)
    same_head = (_group(pi, 96, 4) == _group(pj, 96, 4)).astype(BF16)
    return upper, mask_t, same_head


def _gla_gate(lr_ref, wg_ref, bg_ref):
    z = _dot(lr_ref[...], wg_ref[...]) + bg_ref[...]
    la = (jnp.minimum(z, 0.0) - jnp.log(1.0 + jnp.exp(-jnp.abs(z)))) * (1.0 / GATE_TAU)
    return z, la


def _gla_chunk_decay(la_c, upper):
    hi, lo = _split2(la_c)
    dec = _dot(upper, hi) + _dot(upper, lo)
    end = jnp.sum(la_c, axis=0, keepdims=True)
    return jnp.exp(dec), jnp.exp(end)


def _head_mean(x, same_head):
    hi, lo = _split2(x)
    return (_dot(hi, same_head) + _dot(lo, same_head)) * (1.0 / 96.0)


def _gla_fwd(q, k, v, g, lr, wg, bg, gn, jobs=()):
    T = q.shape[0]
    nb = T // GLA_ROWS

    def body(q_ref, k_ref, v_ref, g_ref, lr_ref, wg_ref, bg_ref, gn_ref, y_ref, st_ref, s_scr, o_scr, kv_scr):
        upper, mask_t, same_head = _gla_consts()

        @pl.when(pl.program_id(0) == 0)
        def _():
            s_scr[...] = jnp.zeros_like(s_scr)

        _, la = _gla_gate(lr_ref, wg_ref, bg_ref)
        decays = []
        for c in range(GLA_NC):
            rs = slice(c * CH, (c + 1) * CH)
            w, a = _gla_chunk_decay(la[rs], upper)
            decays.append(a)
            kd = (k_ref[rs, :].astype(F32) * w).astype(BF16)
            kv_scr[c] = _dot_tn(v_ref[rs, :], kd) * mask_t
        for c in range(GLA_NC):
            s_new = s_scr[...] * decays[c] + kv_scr[c]
            s_scr[...] = s_new
            st_ref[c] = s_new.astype(BF16)
        for c in range(GLA_NC):
            rs = slice(c * CH, (c + 1) * CH)
            qs = (q_ref[rs, :].astype(F32) * Q_SCALE).astype(BF16)
            o_scr[rs, :] = _dot_nt(qs, st_ref[c])
        o = o_scr[...]
        r = lax.rsqrt(_head_mean(o * o, same_head) + EPS)
        gf = g_ref[...].astype(F32)
        y_ref[...] = (o * r * gn_ref[...] * (gf * _sigmoid(gf))).astype(BF16)

    return _pcall(
        body, name="gla_fwd", grid=(nb,),
        in_specs=[_row_spec(GLA_ROWS, 256), _row_spec(GLA_ROWS, 256), _row_spec(GLA_ROWS, 384),
                  _row_spec(GLA_ROWS, 384), _row_spec(GLA_ROWS, 128),
                  _full_spec((128, 256)), _full_spec((1, 256)), _full_spec((1, 384))],
        out_specs=[_row_spec(GLA_ROWS, 384), pl.BlockSpec((GLA_NC, 384, 256), lambda i: (i, 0, 0))],
        out_shape=[jax.ShapeDtypeStruct((T, 384), BF16), jax.ShapeDtypeStruct((T // CH, 384, 256), BF16)],
        scratch_shapes=[pltpu.VMEM((384, 256), F32), pltpu.VMEM((GLA_ROWS, 384), F32),
                        pltpu.VMEM((GLA_NC, 384, 256), F32)],
        sem=("arbitrary",), operands=(q, k, v, g, lr, wg, bg, gn), jobs=jobs)


def _gla_bwd(q, k, v, g, lr, states, dy, wg, bg, gn):
    T = q.shape[0]
    nb = T // GLA_ROWS

    def rev(s):
        return nb - 1 - s

    def body(q_ref, k_ref, v_ref, g_ref, lr_ref, st_ref, stp_ref, dy_ref, wg_ref, bg_ref, gn_ref,
             dq_ref, dk_ref, dv_ref, dg_ref, dlr_ref, dwg_ref, dbg_ref, dgn_ref,
             d_scr, an_scr, o_scr, do_scr, dla_scr, dst_scr):
        upper, mask_t, same_head = _gla_consts()
        s = pl.program_id(0)
        blk = rev(s)

        @pl.when(s == 0)
        def _():
            d_scr[...] = jnp.zeros_like(d_scr)
            an_scr[...] = jnp.zeros_like(an_scr)
            dwg_ref[...] = jnp.zeros_like(dwg_ref)
            dbg_ref[...] = jnp.zeros_like(dbg_ref)
            dgn_ref[...] = jnp.zeros_like(dgn_ref)

        z, la = _gla_gate(lr_ref, wg_ref, bg_ref)
        ws, as_, qss, kds = [], [], [], []
        for c in range(GLA_NC):
            rs = slice(c * CH, (c + 1) * CH)
            w, a = _gla_chunk_decay(la[rs], upper)
            ws.append(w)
            as_.append(a)
            qs = (q_ref[rs, :].astype(F32) * Q_SCALE).astype(BF16)
            qss.append(qs)
            kds.append((k_ref[rs, :].astype(F32) * w).astype(BF16))
            o_scr[rs, :] = _dot_nt(qs, st_ref[c])
        o = o_scr[...]
        r = lax.rsqrt(_head_mean(o * o, same_head) + EPS)
        on = o * r
        gf = g_ref[...].astype(F32)
        sg = _sigmoid(gf)
        si = gf * sg
        dyf = dy_ref[...].astype(F32)
        gn_ = gn_ref[...]
        dgn_ref[...] += jnp.sum(dyf * si * on, axis=0, keepdims=True)
        dg_ref[...] = (dyf * on * gn_ * (sg * (1.0 + gf * (1.0 - sg)))).astype(BF16)
        d_on = dyf * si * gn_
        do_scr[...] = r * (d_on - on * _head_mean(d_on * on, same_head))

        for c in range(GLA_NC):
            rs = slice(c * CH, (c + 1) * CH)
            dst_scr[c] = _dot_tn(do_scr[rs, :].astype(BF16), qss[c]) * mask_t
        for c in reversed(range(GLA_NC)):
            dt = d_scr[...] * an_scr[...] + dst_scr[c]
            d_scr[...] = dt
            dst_scr[c] = dt
            an_scr[...] = as_[c]
        first = (blk > 0).astype(F32)
        for c in range(GLA_NC):
            rs = slice(c * CH, (c + 1) * CH)
            dob = do_scr[rs, :].astype(BF16)
            if c > 0:
                s_prev = st_ref[c - 1].astype(F32)
            else:
                s_prev = stp_ref[0].astype(F32) * first
            dq_ref[rs, :] = (_dot(dob, st_ref[c]) * Q_SCALE).astype(BF16)
            dt = dst_scr[c]
            da = jnp.sum(dt * s_prev, axis=0, keepdims=True)
            db = dt.astype(BF16)
            dkd = _dot(v_ref[rs, :], db)
            dv_ref[rs, :] = _dot_nt(kds[c], db).astype(BF16)
            dk_ref[rs, :] = (dkd * ws[c]).astype(BF16)
            ddec = dkd * k_ref[rs, :].astype(F32) * ws[c]
            hi, lo = _split2(ddec)
            dla_scr[rs, :] = _dot_tn(upper, hi) + _dot_tn(upper, lo) + as_[c] * da

        dz = dla_scr[...] * (1.0 - _sigmoid(z)) * (1.0 / GATE_TAU)
        dzb = dz.astype(BF16)
        dlr_ref[...] = _dot_nt(dzb, wg_ref[...]).astype(BF16)
        dwg_ref[...] += _dot_tn(lr_ref[...], dzb)
        dbg_ref[...] += jnp.sum(dz, axis=0, keepdims=True)

    def rspec(n):
        return pl.BlockSpec((GLA_ROWS, n), lambda s: (rev(s), 0))

    return pl.pallas_call(
        body, name="gla_bwd", grid=(nb,),
        in_specs=[rspec(256), rspec(256), rspec(384), rspec(384), rspec(128),
                  pl.BlockSpec((GLA_NC, 384, 256), lambda s: (rev(s), 0, 0)),
                  pl.BlockSpec((1, 384, 256), lambda s: (jnp.maximum(rev(s) * GLA_NC - 1, 0), 0, 0)),
                  rspec(384), _full_spec((128, 256)), _full_spec((1, 256)), _full_spec((1, 384))],
        out_specs=[rspec(256), rspec(256), rspec(384), rspec(384), rspec(128),
                   _full_spec((128, 256)), _full_spec((1, 256)), _full_spec((1, 384))],
        out_shape=[jax.ShapeDtypeStruct((T, 256), BF16), jax.ShapeDtypeStruct((T, 256), BF16),
                   jax.ShapeDtypeStruct((T, 384), BF16), jax.ShapeDtypeStruct((T, 384), BF16),
                   jax.ShapeDtypeStruct((T, 128), BF16),
                   jax.ShapeDtypeStruct((128, 256), F32), jax.ShapeDtypeStruct((1, 256), F32),
                   jax.ShapeDtypeStruct((1, 384), F32)],
        scratch_shapes=[pltpu.VMEM((384, 256), F32), pltpu.VMEM((1, 256), F32),
                        pltpu.VMEM((GLA_ROWS, 384), F32), pltpu.VMEM((GLA_ROWS, 384), F32),
                        pltpu.VMEM((GLA_ROWS, 256), F32), pltpu.VMEM((GLA_NC, 384, 256), F32)],
        compiler_params=_cp(("arbitrary",)),
    )(q, k, v, g, lr, states, states, dy, wg, bg, gn)


CONV_ROWS = 512
HALO = 32
SUBL = 8
CONV_SLAB = 32
PHASE_ROWS = CONV_ROWS + HALO - SUBL
FWD_SHIFT = tuple(HALO - (KCONV - 1) + j for j in range(KCONV))
BWD_SHIFT = tuple(KCONV - 1 - j for j in range(KCONV))


def _fill_phases(buf, ph):
    for f in range(1, SUBL):
        ph[f, 0:PHASE_ROWS, :] = buf[pl.ds(f, PHASE_ROWS), :]


def _tap(buf, ph, shift, r, n):
    f, base = shift % SUBL, shift - shift % SUBL
    src = buf if f == 0 else ph.at[f]
    return src[pl.ds(base + r, n), :]


def _taps_apply(w_ref, buf, ph, shifts, out):
    for r in range(0, CONV_ROWS, CONV_SLAB):
        acc = jnp.zeros((CONV_SLAB, 256), F32)
        for j in range(KCONV):
            acc = acc + w_ref[j:j + 1, :] * _tap(buf, ph, shifts[j], r, CONV_SLAB)
        out[r:r + CONV_SLAB, :] = acc


def _conv_scratch():
    return [pltpu.VMEM((CONV_ROWS + HALO, 256), F32), pltpu.VMEM((SUBL, CONV_ROWS + HALO, 256), F32),
            pltpu.VMEM((CONV_ROWS, 256), F32)]


def _conv_common(cu_ref, halo_ref, w_ref, b_ref, lg_ref, lb_ref, buf, ph, cbuf, blk):
    u = cu_ref[...].astype(F32)
    a = u[:, :256]
    sb = _sigmoid(u[:, 256:])
    uh = halo_ref[...].astype(F32)
    hh = uh[:, :256] * _sigmoid(uh[:, 256:]) * (blk > 0).astype(F32)
    buf[0:HALO, :] = hh
    buf[HALO:HALO + CONV_ROWS, :] = a * sb
    _fill_phases(buf, ph)
    _taps_apply(w_ref, buf, ph, FWD_SHIFT, cbuf)
    cc = cbuf[...] + b_ref[...]
    mu = jnp.mean(cc, axis=-1, keepdims=True)
    xc = cc - mu
    rstd = lax.rsqrt(jnp.mean(xc * xc, axis=-1, keepdims=True) + EPS)
    n = xc * rstd
    yln = n * lg_ref[...] + lb_ref[...]
    return a, sb, n, rstd, yln


def _conv_fwd(cu, w, b, lg, lb, jobs=()):
    T = cu.shape[0]
    nb = T // CONV_ROWS
    per = CONV_ROWS // HALO

    def body(cu_ref, halo_ref, w_ref, b_ref, lg_ref, lb_ref, y_ref, buf, ph, cbuf):
        _, _, _, _, yln = _conv_common(cu_ref, halo_ref, w_ref, b_ref, lg_ref, lb_ref, buf, ph, cbuf,
                                       pl.program_id(0))
        y_ref[...] = (yln * _sigmoid(yln)).astype(BF16)

    return _pcall(
        body, name="conv_fwd", grid=(nb,),
        in_specs=[_row_spec(CONV_ROWS, 512),
                  pl.BlockSpec((HALO, 512), lambda i: (jnp.maximum(i * per - 1, 0), 0)),
                  _full_spec((32, 256)), _full_spec((1, 256)), _full_spec((1, 256)), _full_spec((1, 256))],
        out_specs=[_row_spec(CONV_ROWS, 256)],
        out_shape=[jax.ShapeDtypeStruct((T, 256), BF16)],
        scratch_shapes=_conv_scratch(),
        sem=("arbitrary",), operands=(cu, cu, w, b, lg, lb), jobs=jobs)


def _conv_bwd(cu, dy, w, b, lg, lb, jobs=()):
    T = cu.shape[0]
    nb = T // CONV_ROWS
    per = CONV_ROWS // HALO

    def rev(s):
        return nb - 1 - s

    def body(cu_ref, halo_ref, dy_ref, w_ref, b_ref, lg_ref, lb_ref,
             dcu_ref, dw_ref, db_ref, dlg_ref, dlb_ref, buf, ph, cbuf, dcbuf, dph, carry):
        s = pl.program_id(0)

        @pl.when(s == 0)
        def _():
            carry[...] = jnp.zeros_like(carry)
            dw_ref[...] = jnp.zeros_like(dw_ref)
            db_ref[...] = jnp.zeros_like(db_ref)
            dlg_ref[...] = jnp.zeros_like(dlg_ref)
            dlb_ref[...] = jnp.zeros_like(dlb_ref)

        a, sb, n, rstd, yln = _conv_common(cu_ref, halo_ref, w_ref, b_ref, lg_ref, lb_ref, buf, ph, cbuf, rev(s))
        sg = _sigmoid(yln)
        dyln = dy_ref[...].astype(F32) * (sg * (1.0 + yln * (1.0 - sg)))
        dlg_ref[...] += jnp.sum(dyln * n, axis=0, keepdims=True)
        dlb_ref[...] += jnp.sum(dyln, axis=0, keepdims=True)
        dn = dyln * lg_ref[...]
        dc = rstd * (dn - jnp.mean(dn, axis=-1, keepdims=True) - n * jnp.mean(dn * n, axis=-1, keepdims=True))
        db_ref[...] += jnp.sum(dc, axis=0, keepdims=True)
        dcbuf[0:CONV_ROWS, :] = dc
        dcbuf[CONV_ROWS:CONV_ROWS + HALO, :] = carry[...]
        carry[...] = dc[0:HALO, :]
        _fill_phases(dcbuf, dph)
        for j in range(KCONV):
            acc = jnp.zeros((SUBL, 256), F32)
            for r in range(0, CONV_ROWS, 2 * CONV_SLAB):
                prod = dcbuf[r:r + 2 * CONV_SLAB, :] * _tap(buf, ph, FWD_SHIFT[j], r, 2 * CONV_SLAB)
                acc = acc + jnp.sum(prod.reshape(2 * CONV_SLAB // SUBL, SUBL, 256), axis=0)
            dw_ref[j:j + 1, :] += jnp.sum(acc, axis=0, keepdims=True)
        _taps_apply(w_ref, dcbuf, dph, BWD_SHIFT, cbuf)
        dhg = cbuf[...]
        dcu_ref[...] = jnp.concatenate([dhg * sb, dhg * a * sb * (1.0 - sb)], axis=1).astype(BF16)

    def rspec(n):
        return pl.BlockSpec((CONV_ROWS, n), lambda s: (rev(s), 0))

    return _pcall(
        body, name="conv_bwd", grid=(nb,),
        in_specs=[rspec(512),
                  pl.BlockSpec((HALO, 512), lambda s: (jnp.maximum(rev(s) * per - 1, 0), 0)),
                  rspec(256),
                  _full_spec((32, 256)), _full_spec((1, 256)), _full_spec((1, 256)), _full_spec((1, 256))],
        out_specs=[rspec(512), _full_spec((32, 256)), _full_spec((1, 256)), _full_spec((1, 256)),
                   _full_spec((1, 256))],
        out_shape=[jax.ShapeDtypeStruct((T, 512), BF16), jax.ShapeDtypeStruct((32, 256), F32),
                   jax.ShapeDtypeStruct((1, 256), F32), jax.ShapeDtypeStruct((1, 256), F32),
                   jax.ShapeDtypeStruct((1, 256), F32)],
        scratch_shapes=_conv_scratch() + [pltpu.VMEM((CONV_ROWS + HALO, 256), F32),
                                          pltpu.VMEM((SUBL, CONV_ROWS + HALO, 256), F32),
                                          pltpu.VMEM((HALO, 256), F32)],
        sem=("arbitrary",), operands=(cu, cu, dy, w, b, lg, lb), jobs=jobs)


def _rel_onehot_t(shift=0):
    r = lax.broadcasted_iota(jnp.int32, (384, RB_W), 0)
    n = lax.broadcasted_iota(jnp.int32, (384, RB_W), 1) - shift
    idx = jnp.clip(1024 - n, -128, 128) + 128
    return (idx == r).astype(BF16)


def _relbias_expand(rb):
    def body(rb_ref, out_ref):
        oh = _rel_onehot_t()
        hi, mid, lo = _split3(rb_ref[...])
        strip = _dot(hi, oh) + _dot(mid, oh) + _dot(lo, oh)
        qi = _group(lax.broadcasted_iota(jnp.int32, (AQ_BLK, AK_WIN), 0), CH, 4)
        kj = _group(lax.broadcasted_iota(jnp.int32, (AQ_BLK, AK_WIN), 1), CH, 12)
        valid = (kj >= qi) & (kj <= qi + 8)
        for hd in range(6):
            x = jnp.broadcast_to(strip[hd:hd + 1, :], (AQ_BLK, RB_W))
            xr = pltpu.roll(x, 0, 1, stride=1, stride_axis=0)
            out_ref[hd] = jnp.where(valid, xr[:, 512:512 + AK_WIN], NEG)

    return pl.pallas_call(
        body, name="relbias_expand",
        out_shape=jax.ShapeDtypeStruct((6, AQ_BLK, AK_WIN), F32),
        compiler_params=_cp(),
    )(rb)


def _relbias_grad(dbias):
    def body(db_ref, out_ref):
        oh = _rel_onehot_t(AQ_BLK - 1)
        ri = lax.broadcasted_iota(jnp.int32, (AQ_BLK, AQ_BLK), 0)
        ci = lax.broadcasted_iota(jnp.int32, (AQ_BLK, AQ_BLK), 1)
        flip = (ri + ci == AQ_BLK - 1).astype(BF16)
        rows = []
        for hd in range(6):
            hi, mid, lo = _split3(db_ref[hd])
            rev = _dot(flip, hi) + _dot(flip, mid) + _dot(flip, lo)
            x = jnp.concatenate([jnp.zeros((AQ_BLK, 512), F32), rev,
                                 jnp.zeros((AQ_BLK, RB_W - 512 - AK_WIN), F32)], axis=1)
            xr = pltpu.roll(x, 0, 1, stride=1, stride_axis=0)
            rows.append(jnp.sum(xr, axis=0, keepdims=True))
        rows.append(jnp.zeros((2, RB_W), F32))
        dstrip = jnp.concatenate(rows, axis=0)
        hi, mid, lo = _split3(dstrip)
        out_ref[...] = _dot_nt(hi, oh) + _dot_nt(mid, oh) + _dot_nt(lo, oh)

    return pl.pallas_call(
        body, name="relbias_grad",
        out_shape=jax.ShapeDtypeStruct((8, 384), F32),
        compiler_params=_cp(),
    )(dbias)


ATT_SLAB = 8


def _att_softmax_slab(s_scr, b_ref, hd, rows, first_key):
    kvalid = lax.broadcasted_iota(jnp.int32, (ATT_SLAB, AK_WIN), 1) >= first_key
    s = jnp.where(kvalid, s_scr[rows, :] + b_ref[hd, rows, :], NEG)
    m = jnp.max(s, axis=-1, keepdims=True)
    p = jnp.exp(s - m)
    return p * (1.0 / jnp.sum(p, axis=-1, keepdims=True))


def _att_first_key(i):
    return (8 - 4 * i) * CH


def _slab_rows(t):
    return pl.ds(t * ATT_SLAB, ATT_SLAB)


def _head_lanes(hd):
    return slice(hd * 64, (hd + 1) * 64)


WIN_BLKS = AK_WIN // AQ_BLK


def _head_tile(hd):
    return slice(hd * HEAD_PAD, (hd + 1) * HEAD_PAD)


def _win_cols(d):
    return slice(d * AQ_BLK, (d + 1) * AQ_BLK)


def _win_block(i, d):
    return jnp.maximum(i + d - WIN_LEFT, 0)


def _win_specs():
    return [pl.BlockSpec((AQ_BLK, ATT_WIDE), lambda i, d=d: (_win_block(i, d), 0)) for d in range(WIN_BLKS)]


def _att_fwd(q, k, v, bias, jobs=()):
    T = q.shape[0]
    nb = T // AQ_BLK

    def body(q_ref, k0, k1, k2, v0, v1, v2, b_ref, o_ref, s_scr):
        k_refs, v_refs = (k0, k1, k2), (v0, v1, v2)
        first_key = _att_first_key(pl.program_id(0))

        def scores(hd):
            q_h = q_ref[:, _head_tile(hd)] * A_SCALE
            for d in range(WIN_BLKS):
                s_scr[hd % 2, :, _win_cols(d)] = _dot_nt(q_h, k_refs[d][:, _head_tile(hd)])

        scores(0)
        for hd in range(ATT_HEADS):
            if hd + 1 < ATT_HEADS:
                scores(hd + 1)
            s_h = s_scr.at[hd % 2]
            for t in range(AQ_BLK // ATT_SLAB):
                rows = _slab_rows(t)
                s_h[rows, :] = _att_softmax_slab(s_h, b_ref, hd, rows, first_key)
            o_h = _dot(s_h[:, _win_cols(0)].astype(BF16), v_refs[0][:, _head_tile(hd)])
            for d in range(1, WIN_BLKS):
                o_h = o_h + _dot(s_h[:, _win_cols(d)].astype(BF16), v_refs[d][:, _head_tile(hd)])
            o_ref[:, _head_lanes(hd)] = o_h[:, :64].astype(BF16)

    return _pcall(
        body, name="att_fwd", grid=(nb,),
        in_specs=[_row_spec(AQ_BLK, ATT_WIDE)] + _win_specs() + _win_specs() + [_full_spec((6, AQ_BLK, AK_WIN))],
        out_specs=[_row_spec(AQ_BLK, 384)],
        out_shape=[jax.ShapeDtypeStruct((T, 384), BF16)],
        scratch_shapes=[pltpu.VMEM((2, AQ_BLK, AK_WIN), F32)],
        sem=("arbitrary",), operands=(q, k, k, k, v, v, v, bias), jobs=jobs)


def _att_bwd(q, k, v, bias, do, jobs=()):
    T = q.shape[0]
    nb = T // AQ_BLK

    def body(q_ref, k0, k1, k2, v0, v1, v2, b_ref, do_ref, dq_ref, dk_ref, dv_ref, db_ref, dk_acc, dv_acc,
             s_scr, dp_scr):
        k_refs, v_refs = (k0, k1, k2), (v0, v1, v2)
        i = pl.program_id(0)

        @pl.when(i == 0)
        def _():
            dk_acc[...] = jnp.zeros_like(dk_acc)
            dv_acc[...] = jnp.zeros_like(dv_acc)
            db_ref[...] = jnp.zeros_like(db_ref)

        first_key = _att_first_key(i)

        def scores(hd):
            q_h = q_ref[:, _head_tile(hd)] * A_SCALE
            do_h = do_ref[:, _head_tile(hd)]
            for d in range(WIN_BLKS):
                s_scr[hd % 2, :, _win_cols(d)] = _dot_nt(q_h, k_refs[d][:, _head_tile(hd)])
                dp_scr[hd % 2, :, _win_cols(d)] = _dot_nt(do_h, v_refs[d][:, _head_tile(hd)])

        scores(0)
        for hd in range(ATT_HEADS):
            if hd + 1 < ATT_HEADS:
                scores(hd + 1)
            s_h, dp_h = s_scr.at[hd % 2], dp_scr.at[hd % 2]
            for t in range(AQ_BLK // ATT_SLAB):
                rows = _slab_rows(t)
                p = _att_softmax_slab(s_h, b_ref, hd, rows, first_key)
                dp = dp_h[rows, :]
                ds = p * (dp - jnp.sum(p * dp, axis=-1, keepdims=True))
                db_ref[hd, rows, :] += ds
                s_h[rows, :] = p
                dp_h[rows, :] = ds
            q_h = q_ref[:, _head_tile(hd)] * A_SCALE
            do_h = do_ref[:, _head_tile(hd)]
            ls = _head_lanes(hd)
            dq_h = jnp.zeros((AQ_BLK, HEAD_PAD), F32)
            for d in range(WIN_BLKS):
                pb = s_h[:, _win_cols(d)].astype(BF16)
                dsb = dp_h[:, _win_cols(d)].astype(BF16)
                rows = pl.ds(pl.multiple_of(_win_block(i, d) * AQ_BLK, AQ_BLK), AQ_BLK)
                dv_acc[rows, ls] += _dot_tn(pb, do_h)[:, :64]
                dk_acc[rows, ls] += _dot_tn(dsb, q_h)[:, :64]
                dq_h = dq_h + _dot(dsb, k_refs[d][:, _head_tile(hd)])
            dq_ref[:, ls] = (dq_h[:, :64] * A_SCALE).astype(BF16)

        @pl.when(i == nb - 1)
        def _():
            dk_ref[...] = dk_acc[...].astype(BF16)
            dv_ref[...] = dv_acc[...].astype(BF16)

    return _pcall(
        body, name="att_bwd", grid=(nb,),
        in_specs=[_row_spec(AQ_BLK, ATT_WIDE)] + _win_specs() + _win_specs()
        + [_full_spec((6, AQ_BLK, AK_WIN)), _row_spec(AQ_BLK, ATT_WIDE)],
        out_specs=[_row_spec(AQ_BLK, 384), _full_spec((T, 384)), _full_spec((T, 384)),
                   _full_spec((6, AQ_BLK, AK_WIN))],
        out_shape=[jax.ShapeDtypeStruct((T, 384), BF16), jax.ShapeDtypeStruct((T, 384), BF16),
                   jax.ShapeDtypeStruct((T, 384), BF16), jax.ShapeDtypeStruct((6, AQ_BLK, AK_WIN), F32)],
        scratch_shapes=[pltpu.VMEM((T, 384), F32), pltpu.VMEM((T, 384), F32),
                        pltpu.VMEM((2, AQ_BLK, AK_WIN), F32), pltpu.VMEM((2, AQ_BLK, AK_WIN), F32)],
        sem=("arbitrary",), operands=(q, k, k, k, v, v, v, bias, do), jobs=jobs)


FF_BLK = 512
N_FF = 4096 // FF_BLK
MLP_SHARDS = 2


def _outproj_mlp_fwd(h, o_gla, o_conv, o_att, w_out, gamma, w_up, w_down, jobs=()):
    T = h.shape[0]
    tm = 512

    def body(h_ref, og_ref, oc_ref, oa_ref, wo_ref, g_ref, wu_ref, wd_ref, h1_ref, xt_ref, h2_ref, a_ref, acc, xn_ref):
        j = pl.program_id(1)

        @pl.when(j == 0)
        def _():
            wo = wo_ref[...]
            h1 = (h_ref[...] + _dot(og_ref[...], wo[0:384]) + _dot(oc_ref[...], wo[384:640])
                  + _dot(oa_ref[...], wo[640:1024]))
            h1_ref[...] = h1
            r = lax.rsqrt(jnp.mean(h1 * h1, axis=-1, keepdims=True) + EPS)
            xn = (h1 * r * g_ref[...]).astype(BF16)
            xn_ref[...] = xn
            xt_ref[...] = jnp.transpose(xn)
            acc[...] = h1

        xn_ = xn_ref[...]
        down = None
        for s in range(MLP_SHARDS):
            a = jnp.maximum(_dot(xn_, wu_ref[s]), 0.0)
            a_ref[:, s * FF_BLK:(s + 1) * FF_BLK] = a.astype(BF16)
            part = _dot((a * a).astype(BF16), wd_ref[s])
            down = part if down is None else down + part
        acc[...] += down

        @pl.when(j == N_FF // MLP_SHARDS - 1)
        def _():
            h2_ref[...] = acc[...]

    row = lambda n: pl.BlockSpec((tm, n), lambda i, j: (i, 0))
    return _pcall(
        body, name="outproj_mlp_fwd", grid=(T // tm, N_FF // MLP_SHARDS),
        in_specs=[row(D), row(384), row(256), row(384),
                  pl.BlockSpec((D, D), lambda i, j: (0, 0)), pl.BlockSpec((1, D), lambda i, j: (0, 0)),
                  pl.BlockSpec((MLP_SHARDS, D, FF_BLK), lambda i, j: (j, 0, 0)),
                  pl.BlockSpec((MLP_SHARDS, FF_BLK, D), lambda i, j: (j, 0, 0))],
        out_specs=[row(D), pl.BlockSpec((D, tm), lambda i, j: (0, i)), row(D),
                   pl.BlockSpec((tm, MLP_SHARDS * FF_BLK), lambda i, j: (i, j))],
        out_shape=[jax.ShapeDtypeStruct((T, D), F32), jax.ShapeDtypeStruct((D, T), BF16),
                   jax.ShapeDtypeStruct((T, D), F32), jax.ShapeDtypeStruct((T, N_FF * FF_BLK), BF16)],
        scratch_shapes=[pltpu.VMEM((tm, D), F32), pltpu.VMEM((tm, D), BF16)],
        sem=("arbitrary", "arbitrary"), operands=(h, o_gla, o_conv, o_att, w_out, gamma, w_up, w_down), jobs=jobs)


def _mlp_bwd(xn2t, act, h1, dh2, dh2b, gamma, w_up, w_down, jobs=()):
    T = act.shape[0]
    tm = 512
    nt = T // tm
    ns = MLP_SHARDS
    nj = N_FF // ns
    last = nj - 1

    def body(xt_ref, a_ref, h1_ref, dy_ref, dyb_ref, g_ref, wu_ref, wd_ref, dh1_ref, dwu_ref, dwd_ref, dg_ref,
             dxn_acc, acc_u, acc_d):
        j = pl.program_id(0)
        i = pl.program_id(1)
        xt = xt_ref[...]
        dyb = dyb_ref[...]
        rows = pl.ds(pl.multiple_of(i * tm, tm), tm)

        @pl.when(i == 0)
        def _():
            acc_u[...] = jnp.zeros_like(acc_u)
            acc_d[...] = jnp.zeros_like(acc_d)

        @pl.when(j == 0)
        def _():
            dxn_acc[rows, :] = jnp.zeros((tm, D), F32)

        dxn = None
        for s in range(ns):
            a = a_ref[:, s * FF_BLK:(s + 1) * FF_BLK].astype(F32)
            hh = (a * a).astype(BF16)
            du = (_dot_nt(dyb, wd_ref[s]) * (2.0 * a)).astype(BF16)
            acc_d[s] += _dot_tn(hh, dyb)
            acc_u[s] += _dot(xt, du)
            part = _dot_nt(du, wu_ref[s])
            dxn = part if dxn is None else dxn + part
        dxn_acc[rows, :] += dxn

        @pl.when(i == nt - 1)
        def _():
            for s in range(ns):
                dwu_ref[s, 0] = acc_u[s].astype(BF16)
                dwd_ref[s, 0] = acc_d[s].astype(BF16)

        @pl.when(j == last)
        def _():
            @pl.when(i == 0)
            def _():
                dg_ref[...] = jnp.zeros_like(dg_ref)

            h1 = h1_ref[...]
            r = lax.rsqrt(jnp.mean(h1 * h1, axis=-1, keepdims=True) + EPS)
            dx, dgam = _rms_bwd(dxn_acc[rows, :], h1, r, g_ref[...])
            dh1_ref[...] = dy_ref[...] + dx
            dg_ref[...] += dgam

    assert ns == 2
    late = lambda j, i: (jnp.where(j == last, i, 0), 0)
    return _pcall(
        body, name="mlp_bwd", grid=(nj, nt),
        in_specs=[pl.BlockSpec((D, tm), lambda j, i: (0, i)), pl.BlockSpec((tm, ns * FF_BLK), lambda j, i: (i, j)),
                  pl.BlockSpec((tm, D), late), pl.BlockSpec((tm, D), late),
                  pl.BlockSpec((tm, D), lambda j, i: (i, 0)), pl.BlockSpec((1, D), lambda j, i: (0, 0)),
                  pl.BlockSpec((ns, D, FF_BLK), lambda j, i: (j, 0, 0), pipeline_mode=pl.Buffered(1)),
                  pl.BlockSpec((ns, FF_BLK, D), lambda j, i: (j, 0, 0), pipeline_mode=pl.Buffered(1))],
        out_specs=[pl.BlockSpec((tm, D), late),
                   pl.BlockSpec((ns, 1, D, FF_BLK), lambda j, i: (0, j, 0, 0)),
                   pl.BlockSpec((ns, 1, FF_BLK, D), lambda j, i: (0, j, 0, 0)),
                   pl.BlockSpec((1, D), lambda j, i: (0, 0))],
        out_shape=[jax.ShapeDtypeStruct((T, D), F32), jax.ShapeDtypeStruct((2, 4, D, FF_BLK), BF16),
                   jax.ShapeDtypeStruct((2, 4, FF_BLK, D), BF16), jax.ShapeDtypeStruct((1, D), F32)],
        scratch_shapes=[pltpu.VMEM((T, D), F32), pltpu.VMEM((ns, D, FF_BLK), F32), pltpu.VMEM((ns, FF_BLK, D), F32)],
        sem=("arbitrary", "arbitrary"), operands=(xn2t, act, h1, dh2, dh2b, gamma, w_up, w_down), jobs=jobs)


def _outproj_bwd(dh1, o_gla, o_conv, o_att, w_out, jobs=()):
    T = dh1.shape[0]
    tm = 512
    nt = T // tm

    def body(dy_ref, og_ref, oc_ref, oa_ref, wo_ref, dg_ref, dc_ref, da_ref, dw_ref, acc):
        i = pl.program_id(0)

        @pl.when(i == 0)
        def _():
            acc[...] = jnp.zeros_like(acc)

        dyb = dy_ref[...].astype(BF16)
        dm = _dot_nt(dyb, wo_ref[...])
        dg_ref[...] = dm[:, 0:384].astype(BF16)
        dc_ref[...] = dm[:, 384:640].astype(BF16)
        _store_head_padded(da_ref, dm[:, 640:1024].astype(BF16))
        mixed = jnp.concatenate([og_ref[...], oc_ref[...], oa_ref[...]], axis=1)
        acc[...] += _dot_tn(mixed, dyb)

        @pl.when(i == nt - 1)
        def _():
            for j in range(N_DEV):
                dw_ref[j % 2, j // 2] = acc[j * 128:(j + 1) * 128, :].astype(BF16)

    return _pcall(
        body, name="outproj_bwd", grid=(nt,),
        in_specs=[_row_spec(tm, D), _row_spec(tm, 384), _row_spec(tm, 256), _row_spec(tm, 384),
                  _full_spec((D, D))],
        out_specs=[_row_spec(tm, 384), _row_spec(tm, 256), _row_spec(tm, ATT_WIDE), _full_spec((2, 4, 128, D))],
        out_shape=[jax.ShapeDtypeStruct((T, 384), BF16), jax.ShapeDtypeStruct((T, 256), BF16),
                   jax.ShapeDtypeStruct((T, ATT_WIDE), BF16), jax.ShapeDtypeStruct((2, 4, 128, D), BF16)],
        scratch_shapes=[pltpu.VMEM((D, D), F32)],
        sem=("arbitrary",), operands=(dh1, o_gla, o_conv, o_att, w_out), jobs=jobs)


def _loss_fwd_bwd(h, gamma, target):
    T = h.shape[0]
    tm = 512

    def body(h_ref, g_ref, t_ref, loss_ref, dh_ref, dhb_ref, dg_ref):
        @pl.when(pl.program_id(0) == 0)
        def _():
            loss_ref[...] = jnp.zeros_like(loss_ref)
            dg_ref[...] = jnp.zeros_like(dg_ref)

        x = h_ref[...]
        r = lax.rsqrt(jnp.mean(x * x, axis=-1, keepdims=True) + EPS)
        gamma_ = g_ref[...]
        e = x * r * gamma_ - t_ref[...]
        loss_ref[...] += 0.5 * jnp.sum(jnp.mean(e * e, axis=-1, keepdims=True), axis=0, keepdims=True)
        dx, dgam = _rms_bwd(e * (1.0 / D), x, r, gamma_)
        dh_ref[...] = dx
        dhb_ref[...] = dx.astype(BF16)
        dg_ref[...] += dgam

    return pl.pallas_call(
        body, name="loss_fwd_bwd", grid=(T // tm,),
        in_specs=[_row_spec(tm, D), _full_spec((1, D)), _row_spec(tm, D)],
        out_specs=[_full_spec((8, 128)), _row_spec(tm, D), _row_spec(tm, D), _full_spec((1, D))],
        out_shape=[jax.ShapeDtypeStruct((8, 128), F32), jax.ShapeDtypeStruct((T, D), F32),
                   jax.ShapeDtypeStruct((T, D), BF16), jax.ShapeDtypeStruct((1, D), F32)],
        compiler_params=_cp(("arbitrary",)),
    )(h, gamma, target)


def _adamw_math(w, g, m, v):
    m = ADAM_B1 * m + (1.0 - ADAM_B1) * g
    v = ADAM_B2 * v + (1.0 - ADAM_B2) * (g * g)
    m_hat = m / (1.0 - ADAM_B1 ** ADAM_STEP)
    v_hat = v / (1.0 - ADAM_B2 ** ADAM_STEP)
    delta = -ADAM_LR * (m_hat / (jnp.sqrt(v_hat) + ADAM_EPS) + ADAM_WD * w)
    return delta, m, v


def _rs_adamw(a_own, r2, w, m, v, layer, chip_idx, rows_blk, prev=None):
    _, R, C = w.shape
    nblk = R // rows_blk

    def body(chip_ref, a_ref, r_ref, w_ref, m_ref, v_ref, *rest):
        g_out, d_out, m_out, v_out = rest[-4:]
        g = (a_ref[0].astype(F32) + r_ref[0].astype(F32)) + (r_ref[1].astype(F32) + r_ref[2].astype(F32))
        delta, m_new, v_new = _adamw_math(w_ref[0], g, m_ref[0], v_ref[0])
        g_out[0] = g
        d_out[0] = delta
        m_out[0] = m_new
        v_out[0] = v_new

    blk = pl.BlockSpec((1, rows_blk, C), lambda i, chip: (layer, i, 0))
    n_prev = 0 if prev is None else 4
    grid_spec = pltpu.PrefetchScalarGridSpec(
        num_scalar_prefetch=1, grid=(nblk,),
        in_specs=[pl.BlockSpec((1, rows_blk, C), lambda i, chip: (chip[0], i, 0)),
                  pl.BlockSpec((3, rows_blk, C), lambda i, chip: (0, i, 0)), blk, blk, blk]
        + [_any_spec()] * n_prev,
        out_specs=[blk, blk, blk, blk])
    return pl.pallas_call(
        body, name="rs_adamw", grid_spec=grid_spec,
        out_shape=[jax.ShapeDtypeStruct((DEPTH, R, C), F32)] * 4,
        input_output_aliases={6 + t: t for t in range(n_prev)},
        compiler_params=_cp(("arbitrary",)),
    )(chip_idx, a_own, r2, w, m, v, *(prev or ()))


def _pair_sum(g, r1, core_idx, rows_blk):
    _, _, R, C = g.shape
    nblk = R // rows_blk

    def body(core_ref, g_ref, r_ref, o_ref):
        o_ref[...] = (g_ref[0].astype(F32) + r_ref[...].astype(F32)).astype(BF16)

    grid_spec = pltpu.PrefetchScalarGridSpec(
        num_scalar_prefetch=1, grid=(4, nblk),
        in_specs=[pl.BlockSpec((1, 1, rows_blk, C), lambda k, i, core: (core[0], k, i, 0)),
                  pl.BlockSpec((1, rows_blk, C), lambda k, i, core: (k, i, 0))],
        out_specs=pl.BlockSpec((1, rows_blk, C), lambda k, i, core: (k, i, 0)))
    return pl.pallas_call(
        body, name="rs_pair_sum", grid_spec=grid_spec,
        out_shape=jax.ShapeDtypeStruct((4, R, C), BF16),
        compiler_params=_cp(("arbitrary", "arbitrary")),
    )(core_idx, g, r1)


def _small_sum(gathered):
    def body(g_ref, o_ref):
        acc = g_ref[0]
        for d in range(1, N_DEV):
            acc = acc + g_ref[d]
        o_ref[...] = acc

    return pl.pallas_call(
        body, name="small_sum",
        out_shape=jax.ShapeDtypeStruct(gathered.shape[1:], F32),
        compiler_params=_cp(),
    )(gathered)


def _adamw_small(ws, gs, ms, vs):
    n = len(ws)

    def body(*refs):
        w_r, g_r, m_r, v_r = refs[0:n], refs[n:2 * n], refs[2 * n:3 * n], refs[3 * n:4 * n]
        d_o, m_o, v_o = refs[4 * n:5 * n], refs[5 * n:6 * n], refs[6 * n:7 * n]
        for t in range(n):
            delta, m_new, v_new = _adamw_math(w_r[t][...], g_r[t][...], m_r[t][...], v_r[t][...])
            d_o[t][...] = delta
            m_o[t][...] = m_new
            v_o[t][...] = v_new

    shapes = [jax.ShapeDtypeStruct(w.shape, F32) for w in ws]
    outs = pl.pallas_call(
        body, name="adamw_small", out_shape=shapes * 3, compiler_params=_cp(),
    )(*ws, *gs, *ms, *vs)
    return outs[0:n], outs[n:2 * n], outs[2 * n:3 * n]


def _mesh_pos():
    return lax.axis_index("x"), lax.axis_index("y"), lax.axis_index("c")


def _peers():
    x, y, c = _mesh_pos()
    return (x, y, c), (x, y, 1 - c), [(1 - x, y), (x, 1 - y), (1 - x, 1 - y)]


def _slot(ref, pos):
    return ref.at[4 * pos[0] + 2 * pos[1] + pos[2]]


def _remote(src, dst, send_sem, recv_sem, to):
    return pltpu.make_async_remote_copy(src_ref=src, dst_ref=dst, send_sem=send_sem, recv_sem=recv_sem,
                                        device_id=to, device_id_type=MESH)


def _ag_spread(shards):
    n = len(shards)

    def copies(ins, outs, sems):
        send, recv, loc = sems
        me, sibling, chips = _peers()
        peers = [sibling] + [(*chip, me[2]) for chip in chips]
        local = [pltpu.make_async_copy(ins[a], _slot(outs[a], me), loc.at[a]) for a in range(n)]
        sends = [_remote(ins[a], _slot(outs[a], me), send.at[a, k], recv.at[a, k], p)
                 for a in range(n) for k, p in enumerate(peers)]
        recvs = [_remote(ins[a], _slot(outs[a], p), send.at[a, k], recv.at[a, k], p)
                 for a in range(n) for k, p in enumerate(peers)]
        return local, sends, recvs

    def start(ins, outs, sems):
        local, sends, _ = copies(ins, outs, sems)
        for cp in local + sends:
            cp.start()

    def finish(ins, outs, sems):
        local, sends, recvs = copies(ins, outs, sems)
        for cp in sends:
            cp.wait_send()
        for cp in recvs:
            cp.wait_recv()
        for cp in local:
            cp.wait()

    return _Job(shards, [jax.ShapeDtypeStruct((N_DEV,) + a.shape, a.dtype) for a in shards],
                [pltpu.SemaphoreType.DMA((n, 4)), pltpu.SemaphoreType.DMA((n, 4)), pltpu.SemaphoreType.DMA((n,))],
                start, finish)


def _ag_pass(stacks):
    n = len(stacks)

    def copies(ins, outs, sems):
        send, recv = sems
        me, sibling, chips = _peers()
        sends = [_remote(_slot(ins[a], (*chip, me[2])), _slot(outs[a], (*chip, me[2])), send.at[a, j], recv.at[a, j],
                         sibling) for a in range(n) for j, chip in enumerate(chips)]
        recvs = [_remote(_slot(ins[a], (*chip, me[2])), _slot(outs[a], (*chip, 1 - me[2])), send.at[a, j],
                         recv.at[a, j], sibling) for a in range(n) for j, chip in enumerate(chips)]
        return sends, recvs

    def start(ins, outs, sems):
        for cp in copies(ins, outs, sems)[0]:
            cp.start()

    def finish(ins, outs, sems):
        sends, recvs = copies(ins, outs, sems)
        for cp in sends:
            cp.wait_send()
        for cp in recvs:
            cp.wait_recv()

    return _Job(stacks, [jax.ShapeDtypeStruct(a.shape, a.dtype) for a in stacks],
                [pltpu.SemaphoreType.DMA((n, 3)), pltpu.SemaphoreType.DMA((n, 3))],
                start, finish, aliases={a: a for a in range(n)})


def _ag_both(shards):
    spread = _ag_spread(shards)
    fake = [jax.ShapeDtypeStruct((N_DEV,) + a.shape, a.dtype) for a in shards]
    onward = _ag_pass(fake)
    n_sp = len(spread.sems)

    def start(ins, outs, sems):
        spread.start(ins, outs, sems[:n_sp])

    def finish(ins, outs, sems):
        spread.finish(ins, outs, sems[:n_sp])
        onward.start(outs, outs, sems[n_sp:])
        onward.finish(outs, outs, sems[n_sp:])

    return _Job(shards, spread.out_shapes, spread.sems + onward.sems, start, finish)


def _rs_swap(parts):
    n = len(parts)

    def copies(ins, outs, sems):
        send, recv = sems
        me, sibling, _ = _peers()
        return [_remote(ins[a].at[1 - me[2]], outs[a], send.at[a], recv.at[a], sibling) for a in range(n)]

    def start(ins, outs, sems):
        for cp in copies(ins, outs, sems):
            cp.start()

    def finish(ins, outs, sems):
        for cp in copies(ins, outs, sems):
            cp.wait()

    return _Job(parts, [jax.ShapeDtypeStruct(a.shape[1:], a.dtype) for a in parts],
                [pltpu.SemaphoreType.DMA((n,)), pltpu.SemaphoreType.DMA((n,))], start, finish)


def _rs_ici(pairs):
    n = len(pairs)

    def copies(ins, outs, sems):
        send, recv = sems
        me, _, chips = _peers()
        return [_remote(ins[a].at[2 * chip[0] + chip[1]], outs[a].at[j], send.at[a, j], recv.at[a, j],
                        (*chip, me[2])) for a in range(n) for j, chip in enumerate(chips)]

    def start(ins, outs, sems):
        for cp in copies(ins, outs, sems):
            cp.start()

    def finish(ins, outs, sems):
        for cp in copies(ins, outs, sems):
            cp.wait()

    return _Job(pairs, [jax.ShapeDtypeStruct((3,) + a.shape[1:], a.dtype) for a in pairs],
                [pltpu.SemaphoreType.DMA((n, 3)), pltpu.SemaphoreType.DMA((n, 3))], start, finish)


def _comm_call(jobs, name):
    def body():
        pass

    return _pcall(body, name=name, grid=(), in_specs=[], out_specs=[], out_shape=[], operands=(), jobs=jobs)[1]


def _allgather(arrs, name):
    n = len(arrs)

    def body(*refs):
        ins, outs = refs[:n], refs[n:2 * n]
        send_sems, recv_sems, local_sems = refs[2 * n:]
        x, y, c = _mesh_pos()
        me, sibling = (x, y, c), (x, y, 1 - c)
        chips = [(1 - x, y), (x, 1 - y), (1 - x, 1 - y)]

        def slot(a, pos):
            return outs[a].at[4 * pos[0] + 2 * pos[1] + pos[2]]

        def copy(a, k, block, to, src=None):
            return pltpu.make_async_remote_copy(
                src_ref=slot(a, block) if src is None else src, dst_ref=slot(a, block),
                send_sem=send_sems.at[a, k], recv_sem=recv_sems.at[a, k],
                device_id=to, device_id_type=MESH)

        mine = [pltpu.make_async_copy(ins[a], slot(a, me), local_sems.at[a]) for a in range(n)]
        for cp in mine:
            cp.start()
        first = []
        for a in range(n):
            first.append(copy(a, 0, me, sibling, src=ins[a]))
            first += [copy(a, 1 + j, me, (*chip, c), src=ins[a]) for j, chip in enumerate(chips)]
        for cp in first:
            cp.start()
        passed = []
        for j, chip in enumerate(chips):
            for a in range(n):
                copy(a, 1 + j, (*chip, c), me).wait_recv()
                fwd = copy(a, 4 + j, (*chip, c), sibling)
                fwd.start()
                passed.append(fwd)
        for a in range(n):
            copy(a, 0, sibling, me).wait_recv()
            for j, chip in enumerate(chips):
                copy(a, 4 + j, (*chip, 1 - c), me).wait_recv()
        for cp in first + passed:
            cp.wait_send()
        for cp in mine:
            cp.wait()

    return pl.pallas_call(
        body, name=name,
        in_specs=[_any_spec()] * n, out_specs=[_any_spec()] * n,
        out_shape=[jax.ShapeDtypeStruct((N_DEV,) + a.shape, a.dtype) for a in arrs],
        scratch_shapes=[pltpu.SemaphoreType.DMA((n, 7)), pltpu.SemaphoreType.DMA((n, 7)),
                        pltpu.SemaphoreType.DMA((n,))],
        compiler_params=_cp(),
    )(*arrs)


W_IN_SHARD = 354
W_IN_COLS = ((0, 192, OQ), (192, 192, OKK), (384, 384, OV), (768, 384, OG), (1152, 16, OLR), (1168, 512, OCU),
             (1680, 384, OAQ), (2064, 384, OAK), (2448, 384, OAV))


def _w_in_padded(stack):
    new_to_ref = {new: (start, width) for start, width, new in W_IN_COLS}
    cols = []
    for new, padded in IN_GROUPS:
        start, width = new_to_ref[new]
        a = start
        while a < start + width:
            j = a // W_IN_SHARD
            b = min(start + width, (j + 1) * W_IN_SHARD)
            cols.append(stack[j][:, a - j * W_IN_SHARD:b - j * W_IN_SHARD])
            a = b
        if padded > width:
            cols.append(jnp.zeros((stack.shape[1], padded - width), stack.dtype))
    return jnp.concatenate(cols, axis=1)


def _dw_in_shards(dw):
    shards = []
    for j in range(N_DEV):
        lo, hi = j * W_IN_SHARD, (j + 1) * W_IN_SHARD
        segs = []
        for start, width, new in W_IN_COLS:
            a, b = max(lo, start), min(hi, start + width)
            if a < b:
                segs.append(dw[:, new + a - start:new + b - start])
        shards.append(jnp.concatenate(segs, axis=1))
    return jnp.stack([jnp.stack([shards[2 * chip + core] for chip in range(4)]) for core in range(2)])


def _pad_to(a, shape):
    return jnp.pad(a, [(0, s - d) for d, s in zip(a.shape, shape)])


SMALL_LAYOUT = (
    ("norm_mix", 2, 1024), ("norm_ffn", 2, 1024), ("norm_final", 1, 1024), ("gla_norm", 2, 384),
    ("b_gla_gate", 2, 192), ("b_dw", 2, 256), ("conv_ln_g", 2, 256), ("conv_ln_b", 2, 256),
    ("rel_bias", 12, 257), ("w_gla_gate", 32, 192), ("w_dw", 62, 256),
)
SMALL_LANES = 128
SMALL_TILE = 8 * SMALL_LANES


def _small_tile_rows(r, lanes):
    return -(-(r * lanes) // SMALL_TILE) * 8


SMALL_ROWS = sum(_small_tile_rows(r, lanes) for _, r, lanes in SMALL_LAYOUT)


def _pack_small(parts):
    tiles = []
    for name, r, lanes in SMALL_LAYOUT:
        rows = _small_tile_rows(r, lanes)
        flat = _pad_to(parts[name].reshape(r * lanes), (rows * SMALL_LANES,))
        tiles.append(flat.reshape(rows, SMALL_LANES))
    return jnp.concatenate(tiles, axis=0)


def _unpack_small(packed):
    out, r0 = {}, 0
    for name, r, lanes in SMALL_LAYOUT:
        rows = _small_tile_rows(r, lanes)
        out[name] = packed[r0:r0 + rows].reshape(rows * SMALL_LANES)[:r * lanes].reshape(r, lanes)
        r0 += rows
    return out


def _mixers_fwd(h, wl, w_in_p, plan=None):
    plan, res = plan or {}, {}

    def jobs(host):
        return plan[host](res) if host in plan else ()

    (q, k, v, g, cu, aq, ak, av, lr), res["inproj"] = _inproj_fwd(h, wl["norm_mix"], w_in_p, jobs=jobs("inproj"))
    bias = _relbias_expand(wl["rb"])
    (o_att,), res["att"] = _att_fwd(aq, ak, av, bias, jobs=jobs("att"))
    (o_gla, states), res["gla"] = _gla_fwd(q, k, v, g, lr, wl["wg"], wl["bg"], wl["gn"], jobs=jobs("gla"))
    (o_conv,), res["conv"] = _conv_fwd(cu, wl["w_dw"], wl["b_dw"], wl["ln_g"], wl["ln_b"], jobs=jobs("conv"))
    sv = dict(h=h, w_in=w_in_p, q=q, k=k, v=v, g=g, cu=cu, aq=aq, ak=ak, av=av, lr=lr,
              o_gla=o_gla, o_conv=o_conv, o_att=o_att, states=states, bias=bias)
    return sv, res


def _mixers_bwd(sv, wl, dh1, d_ogla, d_oconv, att_grads, conv_jobs=(), x_jobs_fn=None):
    daq, dak, dav, dbias = att_grads
    d_rb = _relbias_grad(dbias)
    (dcu, dw_dw, db_dw, dln_g, dln_b), conv_res = _conv_bwd(
        sv["cu"], d_oconv, wl["w_dw"], wl["b_dw"], wl["ln_g"], wl["ln_b"], jobs=conv_jobs)
    dq, dk, dv, dg, dlr, dwg, dbg, dgn = _gla_bwd(sv["q"], sv["k"], sv["v"], sv["g"], sv["lr"], sv["states"],
                                                  d_ogla, wl["wg"], wl["bg"], wl["gn"])
    dparts = (dq, dk, dv, dg, dcu, daq, dak, dav, dlr)
    dw_in = _inproj_bwd_w(sv["h"], wl["norm_mix"], dparts)
    x_jobs = x_jobs_fn(dw_in) if x_jobs_fn is not None else ()
    (dh, dhb, d_nmix), x_res = _inproj_bwd_x(sv["h"], dh1, wl["norm_mix"], sv["w_in"], dparts, jobs=x_jobs)
    small = dict(norm_mix=d_nmix, wg=dwg, bg=dbg, gn=dgn, w_dw=dw_dw, b_dw=db_dw, ln_g=dln_g, ln_b=dln_b, rb=d_rb)
    return (dh, dhb), dw_in, small, conv_res, x_res


def _layer_small(l, w_dw_full, norm_mix, w_gla_gate, b_gla_gate, gla_norm, b_dw, conv_ln_g, conv_ln_b, rel_bias,
                 norm_ffn):
    return dict(
        norm_mix=norm_mix[l][None, :], norm_ffn=norm_ffn[l][None, :],
        wg=_pad_to(w_gla_gate[l], (128, 256)).astype(BF16), bg=_pad_to(b_gla_gate[l][None, :], (1, 256)),
        gn=gla_norm[l][None, :], w_dw=_pad_to(w_dw_full, (32, 256)), b_dw=b_dw[l][None, :],
        ln_g=conv_ln_g[l][None, :], ln_b=conv_ln_b[l][None, :], rb=_pad_to(rel_bias[l], (8, 384)))


RS_ROWS = dict(w_in=512, w_out=128, w_up=512, w_down=256)


def kernel(x, norm_mix, w_in, w_gla_gate, b_gla_gate, gla_norm, w_dw, b_dw, conv_ln_g, conv_ln_b, rel_bias, w_out, norm_ffn, w_up, w_down, norm_final, loss_target, m_norm_mix, m_w_in, m_w_gla_gate, m_b_gla_gate, m_gla_norm, m_w_dw, m_b_dw, m_conv_ln_g, m_conv_ln_b, m_rel_bias, m_w_out, m_norm_ffn, m_w_up, m_w_down, m_norm_final, v_norm_mix, v_w_in, v_w_gla_gate, v_b_gla_gate, v_gla_norm, v_w_dw, v_b_dw, v_conv_ln_g, v_conv_ln_b, v_rel_bias, v_w_out, v_norm_ffn, v_w_up, v_w_down, v_norm_final):
    mx, my, mc = _mesh_pos()
    me = 4 * mx + 2 * my + mc
    chip_idx = (2 * mx + my).astype(jnp.int32).reshape(1)
    core_idx = mc.astype(jnp.int32).reshape(1)
    x0, target = x[0], loss_target[0]

    def pair_sums(parts, r1):
        return [_pair_sum(p, r, core_idx, p.shape[2]) for p, r in zip(parts, r1)]

    sh = [dict(w_in=w_in[l].astype(BF16), w_out=w_out[l].astype(BF16), w_up=w_up[l].astype(BF16),
               w_down=w_down[l].astype(BF16)) for l in range(DEPTH)]
    dw_flat = _pad_to(w_dw, (DEPTH, 32, 32)).reshape(16, 128)
    st_in0, st_dw = _allgather([sh[0]["w_in"], dw_flat], "allgather_first")
    dw_all = st_dw.reshape(N_DEV, DEPTH, 32, 32)[:, :, :KCONV, :]
    dw_all = jnp.transpose(dw_all, (1, 2, 0, 3)).reshape(DEPTH, KCONV, 256)
    wl = [_layer_small(l, dw_all[l], norm_mix, w_gla_gate, b_gla_gate, gla_norm, b_dw, conv_ln_g, conv_ln_b,
                       rel_bias, norm_ffn) for l in range(DEPTH)]

    s0, s1 = sh[0], sh[1]
    half = s0["w_down"].shape[0] // 2
    down0_a, down0_b = s0["w_down"][:half], s0["w_down"][half:]
    sv0, g0 = _mixers_fwd(x0, wl[0], _w_in_padded(st_in0), plan=dict(
        inproj=lambda r: [_ag_spread([s0["w_out"], down0_a])],
        att=lambda r: [_ag_spread([s0["w_up"]]), _ag_pass(r["inproj"])],
        gla=lambda r: [_ag_spread([down0_b]), _ag_pass(r["att"][:1])],
        conv=lambda r: [_ag_pass(r["gla"][:1])]))
    st_out0, st_down0_a = g0["att"][1:]
    st_up0, st_down0_b = g0["gla"][1], g0["conv"][0]
    st_down0 = jnp.concatenate([st_down0_a, st_down0_b], axis=1)
    wo0 = st_out0.reshape(D, D)
    (h1_0, xn2t_0, h2_0, act_0), (st_in1, out1_half) = _outproj_mlp_fwd(
        x0, sv0["o_gla"], sv0["o_conv"], sv0["o_att"], wo0, wl[0]["norm_ffn"], st_up0, st_down0,
        jobs=[_ag_both([s1["w_in"]]), _ag_spread([s1["w_out"]])])

    sv1, g1 = _mixers_fwd(h2_0, wl[1], _w_in_padded(st_in1), plan=dict(
        inproj=lambda r: [_ag_spread([s1["w_up"]]), _ag_pass([out1_half])],
        att=lambda r: [_ag_spread([s1["w_down"]]), _ag_pass(r["inproj"][:1])],
        gla=lambda r: [_ag_pass(r["att"][:1])]))
    st_out1, st_up1, st_down1 = g1["inproj"][1], g1["att"][1], g1["gla"][0]
    wo1 = st_out1.reshape(D, D)
    (h1_1, xn2t_1, h2_1, act_1), _ = _outproj_mlp_fwd(
        h2_0, sv1["o_gla"], sv1["o_conv"], sv1["o_att"], wo1, wl[1]["norm_ffn"], st_up1, st_down1)

    loss8, dh, dhb, d_nf = _loss_fwd_bwd(h2_1, norm_final[None, :], target)
    loss = lax.psum(loss8[0, 0], ("x", "y", "c"))

    def layer_bwd(dh_pair, sv, wl_l, xn2t, act, h1, wo, st_up, st_down, mlp_jobs, x_jobs_fn):
        (dh1, dw_up, dw_down, d_nffn), mlp_res = _mlp_bwd(xn2t, act, h1, dh_pair[0], dh_pair[1], wl_l["norm_ffn"],
                                                           st_up, st_down, jobs=mlp_jobs)
        ud = [dw_up, dw_down]
        (d_ogla, d_oconv, d_oatt, dw_out), r1 = _outproj_bwd(
            dh1, sv["o_gla"], sv["o_conv"], sv["o_att"], wo, jobs=[_rs_swap(ud)])
        pair_ud = pair_sums(ud, r1)
        att_grads, r = _att_bwd(sv["aq"], sv["ak"], sv["av"], sv["bias"], d_oatt,
                                jobs=[_rs_ici(pair_ud), _rs_swap([dw_out])])
        r2_ud, r1_out = r[:2], r[2:]
        pair_out = pair_sums([dw_out], r1_out)
        dh_in, _, small, r2_out, x_res = _mixers_bwd(sv, wl_l, dh1, d_ogla, d_oconv, att_grads,
                                                     conv_jobs=[_rs_ici(pair_out)], x_jobs_fn=x_jobs_fn)
        small["norm_ffn"] = d_nffn
        sums = dict(w_out=(pair_out[0], r2_out[0]), w_up=(pair_ud[0], r2_ud[0]), w_down=(pair_ud[1], r2_ud[1]))
        return dh_in, small, sums, mlp_res, x_res

    stash = {}

    def swap_w_in(dw_in):
        stash["in1"] = [_dw_in_shards(dw_in)]
        return [_rs_swap(stash["in1"])]

    dh_pair, small1, sums1, _, r1_in1 = layer_bwd((dh, dhb), sv1, wl[1], xn2t_1, act_1, h1_1, wo1, st_up1, st_down1,
                                                  (), swap_w_in)
    pair_in1 = pair_sums(stash["in1"], r1_in1)

    def send_w_in(dw_in):
        in0 = [_dw_in_shards(dw_in)]
        stash["pair_in0"] = pair_sums(in0, _comm_call([_rs_swap(in0)], "rs_swap_w_in_0"))
        return [_rs_ici(stash["pair_in0"])]

    (dx, _), small0, sums0, r2_in1, r2_in0 = layer_bwd(dh_pair, sv0, wl[0], xn2t_0, act_0, h1_0, wo0, st_up0, st_down0,
                                                       [_rs_ici(pair_in1)], send_w_in)
    sums1["w_in"] = (pair_in1[0], r2_in1[0])
    sums0["w_in"] = (stash["pair_in0"][0], r2_in0[0])

    big_w = dict(w_in=(w_in, m_w_in, v_w_in), w_out=(w_out, m_w_out, v_w_out), w_up=(w_up, m_w_up, v_w_up),
                 w_down=(w_down, m_w_down, v_w_down))
    pairs = {1: sums1, 0: sums0}
    big_out = {}
    for name, (w_, m_, v_) in big_w.items():
        res = None
        for l in (1, 0):
            a_own, r2_ = pairs[l][name]
            res = _rs_adamw(a_own, r2_, w_, m_, v_, l, chip_idx, RS_ROWS[name], prev=res)
        big_out[name] = res

    grads = (small0, small1)
    parts = dict(
        norm_mix=jnp.concatenate([grads[l]["norm_mix"] for l in range(DEPTH)], axis=0),
        norm_ffn=jnp.concatenate([grads[l]["norm_ffn"] for l in range(DEPTH)], axis=0),
        norm_final=d_nf,
        gla_norm=jnp.concatenate([grads[l]["gn"] for l in range(DEPTH)], axis=0),
        b_gla_gate=jnp.concatenate([grads[l]["bg"][:, :192] for l in range(DEPTH)], axis=0),
        b_dw=jnp.concatenate([grads[l]["b_dw"] for l in range(DEPTH)], axis=0),
        conv_ln_g=jnp.concatenate([grads[l]["ln_g"] for l in range(DEPTH)], axis=0),
        conv_ln_b=jnp.concatenate([grads[l]["ln_b"] for l in range(DEPTH)], axis=0),
        rel_bias=jnp.concatenate([grads[l]["rb"][:6, :N_REL] for l in range(DEPTH)], axis=0),
        w_gla_gate=jnp.concatenate([grads[l]["wg"][:16, :192] for l in range(DEPTH)], axis=0),
        w_dw=jnp.concatenate([grads[l]["w_dw"][:KCONV] for l in range(DEPTH)], axis=0),
    )
    small_all = _allgather([_pack_small(parts)], "allgather_small")[0]
    sg = _unpack_small(_small_sum(small_all))
    dw_grad = lax.dynamic_slice_in_dim(sg["w_dw"].reshape(DEPTH, KCONV, 256), me * 32, 32, axis=2)
    small_g = dict(
        norm_mix=sg["norm_mix"], w_gla_gate=sg["w_gla_gate"].reshape(DEPTH, 16, 192), b_gla_gate=sg["b_gla_gate"],
        gla_norm=sg["gla_norm"], w_dw=dw_grad, b_dw=sg["b_dw"], conv_ln_g=sg["conv_ln_g"],
        conv_ln_b=sg["conv_ln_b"], rel_bias=sg["rel_bias"].reshape(DEPTH, 6, N_REL), norm_ffn=sg["norm_ffn"],
        norm_final=sg["norm_final"].reshape(D))
    small_names = ("norm_mix", "w_gla_gate", "b_gla_gate", "gla_norm", "w_dw", "b_dw", "conv_ln_g", "conv_ln_b",
                   "rel_bias", "norm_ffn", "norm_final")
    small_w = dict(norm_mix=norm_mix, w_gla_gate=w_gla_gate, b_gla_gate=b_gla_gate, gla_norm=gla_norm, w_dw=w_dw,
                   b_dw=b_dw, conv_ln_g=conv_ln_g, conv_ln_b=conv_ln_b, rel_bias=rel_bias, norm_ffn=norm_ffn,
                   norm_final=norm_final)
    small_m = dict(norm_mix=m_norm_mix, w_gla_gate=m_w_gla_gate, b_gla_gate=m_b_gla_gate, gla_norm=m_gla_norm,
                   w_dw=m_w_dw, b_dw=m_b_dw, conv_ln_g=m_conv_ln_g, conv_ln_b=m_conv_ln_b, rel_bias=m_rel_bias,
                   norm_ffn=m_norm_ffn, norm_final=m_norm_final)
    small_v = dict(norm_mix=v_norm_mix, w_gla_gate=v_w_gla_gate, b_gla_gate=v_b_gla_gate, gla_norm=v_gla_norm,
                   w_dw=v_w_dw, b_dw=v_b_dw, conv_ln_g=v_conv_ln_g, conv_ln_b=v_conv_ln_b, rel_bias=v_rel_bias,
                   norm_ffn=v_norm_ffn, norm_final=v_norm_final)
    s_delta, s_m, s_v = _adamw_small([small_w[n] for n in small_names], [small_g[n] for n in small_names],
                                     [small_m[n] for n in small_names], [small_v[n] for n in small_names])
    s_idx = {n: t for t, n in enumerate(small_names)}

    order = ("norm_mix", "w_in", "w_gla_gate", "b_gla_gate", "gla_norm", "w_dw", "b_dw", "conv_ln_g", "conv_ln_b",
             "rel_bias", "w_out", "norm_ffn", "w_up", "w_down", "norm_final")

    def pick(kind, name):
        if name in big_out:
            return big_out[name][kind]
        t = s_idx[name]
        return (small_g[name], s_delta[t], s_m[t], s_v[t])[kind]

    outs = [loss, dx[None]]
    for kind in range(4):
        outs += [pick(kind, n) for n in order]
    return tuple(outs)
```

```python
import functools

import jax
import jax.numpy as jnp
from jax import lax
from jax.experimental import pallas as pl
from jax.experimental.pallas import tpu as pltpu

F32 = jnp.float32
BF16 = jnp.bfloat16
MESH = pl.DeviceIdType.MESH

D = 1024
DEPTH = 2
CH = 64
EPS = 1e-6
NEG = -1e30
N_DEV = 8
N_REL = 257
Q_SCALE = 48.0 ** -0.5
A_SCALE = 64.0 ** -0.5
GATE_TAU = 16.0
KCONV = 31

OQ, OKK, OV, OG, OCU, OAQ, OAK, OAV, OLR, DINP = 0, 256, 512, 896, 1280, 1792, 2176, 2560, 2944, 3072
IN_GROUPS = ((OQ, 256), (OKK, 256), (OV, 384), (OG, 384), (OCU, 512), (OAQ, 384), (OAK, 384), (OAV, 384), (OLR, 128))

AQ_BLK = 256
AK_WIN = 768
WIN_LEFT = 2
RB_W = 1536

ADAM_LR, ADAM_B1, ADAM_B2, ADAM_EPS, ADAM_WD, ADAM_STEP = 0.001, 0.9, 0.999, 1e-08, 0.01, 10


V7X_VMEM_MIB = 64
VMEM_LIMIT_MIB = V7X_VMEM_MIB - 1


def _cp(sem=None):
    kw = {"vmem_limit_bytes": VMEM_LIMIT_MIB * 1024 * 1024}
    if sem is not None:
        kw["dimension_semantics"] = sem
    return pltpu.CompilerParams(**kw)


def _dot(a, b):
    return jnp.dot(a, b, preferred_element_type=F32)


def _dot_nt(a, b):
    return lax.dot_general(a, b, (((1,), (1,)), ((), ())), preferred_element_type=F32)


def _dot_tn(a, b):
    return lax.dot_general(a, b, (((0,), (0,)), ((), ())), preferred_element_type=F32)


def _split2(a):
    hi = a.astype(BF16)
    lo = (a - hi.astype(F32)).astype(BF16)
    return hi, lo


def _split3(a):
    hi = a.astype(BF16)
    r1 = a - hi.astype(F32)
    mid = r1.astype(BF16)
    lo = (r1 - mid.astype(F32)).astype(BF16)
    return hi, mid, lo


def _sigmoid(x):
    return 1.0 / (1.0 + jnp.exp(-x))


def _group(idx, size, n):
    g = jnp.zeros_like(idx)
    for t in range(1, n):
        g = g + (idx >= t * size).astype(jnp.int32)
    return g


def _rms_bwd(dy, x, r, gamma):
    xh = x * r
    dxh = dy * gamma
    dx = r * (dxh - xh * jnp.mean(dxh * xh, axis=-1, keepdims=True))
    return dx, jnp.sum(dy * xh, axis=0, keepdims=True)


def _row_spec(tm, n):
    return pl.BlockSpec((tm, n), lambda i: (i, 0))


def _full_spec(shape):
    nd = len(shape)
    return pl.BlockSpec(shape, lambda *_: (0,) * nd)


def _any_spec():
    return pl.BlockSpec(memory_space=pl.ANY)


class _Job:
    def __init__(self, operands, out_shapes, sems, start, finish, aliases=None):
        self.operands, self.out_shapes, self.sems = list(operands), list(out_shapes), list(sems)
        self.start, self.finish, self.aliases = start, finish, dict(aliases or {})


def _pcall(body, *, name, grid, in_specs, out_specs, out_shape, operands, scratch_shapes=(), sem=None, jobs=()):
    jobs = list(jobs)
    in_specs, out_specs, out_shape = list(in_specs), list(out_specs), list(out_shape)
    scratch_shapes = list(scratch_shapes)
    n_in, n_out, n_scr = len(in_specs), len(out_specs), len(scratch_shapes)
    j_in = [a for j in jobs for a in j.operands]
    j_out = [s for j in jobs for s in j.out_shapes]
    j_sem = [s for j in jobs for s in j.sems]
    aliases, io, oo = {}, n_in, n_out
    for j in jobs:
        for a, b in j.aliases.items():
            aliases[io + a] = oo + b
        io += len(j.operands)
        oo += len(j.out_shapes)

    def wrapped(*refs):
        own_in, ji = refs[:n_in], refs[n_in:n_in + len(j_in)]
        o0 = n_in + len(j_in)
        own_out, jo = refs[o0:o0 + n_out], refs[o0 + n_out:o0 + n_out + len(j_out)]
        s0 = o0 + n_out + len(j_out)
        own_scr, js = refs[s0:s0 + n_scr], refs[s0 + n_scr:]

        def each_job(fn_name):
            a = b = c = 0
            for j in jobs:
                na, nb, nc = len(j.operands), len(j.out_shapes), len(j.sems)
                getattr(j, fn_name)(ji[a:a + na], jo[b:b + nb], js[c:c + nc])
                a, b, c = a + na, b + nb, c + nc

        if jobs and grid:
            pids = [pl.program_id(d) for d in range(len(grid))]
            first = functools.reduce(jnp.logical_and, [p == 0 for p in pids])
            last = functools.reduce(jnp.logical_and, [p == g - 1 for p, g in zip(pids, grid)])
            pl.when(first)(lambda: each_job("start"))
        elif jobs:
            each_job("start")

        body(*own_in, *own_out, *own_scr)

        if jobs and grid:
            pl.when(last)(lambda: each_job("finish"))
        elif jobs:
            each_job("finish")

    res = pl.pallas_call(
        wrapped, name=name, grid=grid,
        in_specs=in_specs + [_any_spec()] * len(j_in), out_specs=out_specs + [_any_spec()] * len(j_out),
        out_shape=out_shape + j_out, scratch_shapes=scratch_shapes + j_sem,
        input_output_aliases=aliases, compiler_params=_cp(sem),
    )(*operands, *j_in)
    return res[:n_out], res[n_out:]


ATT_HEADS = 6
HEAD_PAD = 128
ATT_WIDE = ATT_HEADS * HEAD_PAD
ATT_GROUP_OFFS = (OAQ, OAK, OAV)


def _store_head_padded(o_ref, part):
    o_ref[...] = jnp.zeros_like(o_ref)
    for hd in range(ATT_HEADS):
        o_ref[:, hd * HEAD_PAD:hd * HEAD_PAD + 64] = part[:, hd * 64:(hd + 1) * 64]


def _inproj_fwd(h, gamma, w, jobs=()):
    T = h.shape[0]
    tm = 512

    def body(h_ref, g_ref, w_ref, *outs):
        x = h_ref[...]
        r = lax.rsqrt(jnp.mean(x * x, axis=-1, keepdims=True) + EPS)
        xn = (x * r * g_ref[...]).astype(BF16)
        p = _dot(xn, w_ref[...])
        for o_ref, (off, n) in zip(outs, IN_GROUPS):
            part = p[:, off:off + n].astype(BF16)
            if off in ATT_GROUP_OFFS:
                _store_head_padded(o_ref, part)
            else:
                o_ref[...] = part

    widths = [ATT_WIDE if off in ATT_GROUP_OFFS else n for off, n in IN_GROUPS]
    return _pcall(
        body, name="inproj_fwd", grid=(T // tm,),
        in_specs=[_row_spec(tm, D), _full_spec((1, D)), _full_spec((D, DINP))],
        out_specs=[_row_spec(tm, n) for n in widths],
        out_shape=[jax.ShapeDtypeStruct((T, n), BF16) for n in widths],
        sem=("arbitrary",), operands=(h, gamma, w), jobs=jobs)


def _inproj_norm(h_ref, g_ref):
    x = h_ref[...]
    r = lax.rsqrt(jnp.mean(x * x, axis=-1, keepdims=True) + EPS)
    return x, r, g_ref[...]


def _inproj_bwd_w(h, gamma, dparts):
    T = h.shape[0]
    tm = 512
    nt = T // tm

    def body(h_ref, g_ref, *rest):
        dp_refs = rest[:9]
        dw_ref, acc = rest[9:]
        i = pl.program_id(0)

        @pl.when(i == 0)
        def _():
            acc[...] = jnp.zeros_like(acc)

        x, r, gamma_ = _inproj_norm(h_ref, g_ref)
        xnt = jnp.transpose((x * r * gamma_).astype(BF16))
        for d_ref, (off, n) in zip(dp_refs, IN_GROUPS):
            acc[:, off:off + n] += _dot(xnt, d_ref[...])

        @pl.when(i == nt - 1)
        def _():
            dw_ref[...] = acc[...].astype(BF16)

    return pl.pallas_call(
        body, name="inproj_bwd_w", grid=(nt,),
        in_specs=[_row_spec(tm, D), _full_spec((1, D))] + [_row_spec(tm, n) for _, n in IN_GROUPS],
        out_specs=_full_spec((D, DINP)),
        out_shape=jax.ShapeDtypeStruct((D, DINP), BF16),
        scratch_shapes=[pltpu.VMEM((D, DINP), F32)],
        compiler_params=_cp(("arbitrary",)),
    )(h, gamma, *dparts)


def _inproj_bwd_x(h, dh_in, gamma, w, dparts, jobs=()):
    T = h.shape[0]
    tm = 512

    def body(h_ref, dhin_ref, g_ref, w_ref, *rest):
        dp_refs = rest[:9]
        dh_ref, dhb_ref, dg_ref = rest[9:]

        @pl.when(pl.program_id(0) == 0)
        def _():
            dg_ref[...] = jnp.zeros_like(dg_ref)

        x, r, gamma_ = _inproj_norm(h_ref, g_ref)
        dxn = None
        for d_ref, (off, n) in zip(dp_refs, IN_GROUPS):
            part = _dot_nt(d_ref[...], w_ref[:, off:off + n])
            dxn = part if dxn is None else dxn + part
        dx, dgam = _rms_bwd(dxn, x, r, gamma_)
        dh = dhin_ref[...] + dx
        dh_ref[...] = dh
        dhb_ref[...] = dh.astype(BF16)
        dg_ref[...] += dgam

    return _pcall(
        body, name="inproj_bwd_x", grid=(T // tm,),
        in_specs=[_row_spec(tm, D), _row_spec(tm, D), _full_spec((1, D)), _full_spec((D, DINP))]
        + [_row_spec(tm, n) for _, n in IN_GROUPS],
        out_specs=[_row_spec(tm, D), _row_spec(tm, D), _full_spec((1, D))],
        out_shape=[jax.ShapeDtypeStruct((T, D), F32), jax.ShapeDtypeStruct((T, D), BF16),
                   jax.ShapeDtypeStruct((1, D), F32)],
        sem=("arbitrary",), operands=(h, dh_in, gamma, w, *dparts), jobs=jobs)


GLA_ROWS = 512
GLA_NC = GLA_ROWS // CH


def _gla_consts():
    ri = lax.broadcasted_iota(jnp.int32, (CH, CH), 0)
    ci = lax.broadcasted_iota(jnp.int32, (CH, CH), 1)
    upper = (ci > ri).astype(BF16)
    vv = lax.broadcasted_iota(jnp.int32, (384, 256), 0)
    kk = lax.broadcasted_iota(jnp.int32, (384, 256), 1)
    mask_t = ((_group(vv, 96, 4) == _group(kk, 48, 4)) & (kk < 192)).astype(F32)
    pi = lax.broadcasted_iota(jnp.int32, (384, 384), 0)
    pj = lax.broadcasted_iota(jnp.int32, (384, 384), 1)
    same_head = (_group(pi, 96, 4) == _group(pj, 96, 4)).astype(BF16)
    return upper, mask_t, same_head


def _gla_gate(lr_ref, wg_ref, bg_ref):
    z = _dot(lr_ref[...], wg_ref[...]) + bg_ref[...]
    la = (jnp.minimum(z, 0.0) - jnp.log(1.0 + jnp.exp(-jnp.abs(z)))) * (1.0 / GATE_TAU)
    return z, la


def _gla_chunk_decay(la_c, upper):
    hi, lo = _split2(la_c)
    dec = _dot(upper, hi) + _dot(upper, lo)
    end = jnp.sum(la_c, axis=0, keepdims=True)
    return jnp.exp(dec), jnp.exp(end)


def _head_mean(x, same_head):
    hi, lo = _split2(x)
    return (_dot(hi, same_head) + _dot(lo, same_head)) * (1.0 / 96.0)


def _gla_fwd(q, k, v, g, lr, wg, bg, gn, jobs=()):
    T = q.shape[0]
    nb = T // GLA_ROWS

    def body(q_ref, k_ref, v_ref, g_ref, lr_ref, wg_ref, bg_ref, gn_ref, y_ref, st_ref, s_scr, o_scr, kv_scr):
        upper, mask_t, same_head = _gla_consts()

        @pl.when(pl.program_id(0) == 0)
        def _():
            s_scr[...] = jnp.zeros_like(s_scr)

        _, la = _gla_gate(lr_ref, wg_ref, bg_ref)
        decays = []
        for c in range(GLA_NC):
            rs = slice(c * CH, (c + 1) * CH)
            w, a = _gla_chunk_decay(la[rs], upper)
            decays.append(a)
            kd = (k_ref[rs, :].astype(F32) * w).astype(BF16)
            kv_scr[c] = _dot_tn(v_ref[rs, :], kd) * mask_t
        for c in range(GLA_NC):
            s_new = s_scr[...] * decays[c] + kv_scr[c]
            s_scr[...] = s_new
            st_ref[c] = s_new.astype(BF16)
        for c in range(GLA_NC):
            rs = slice(c * CH, (c + 1) * CH)
            qs = (q_ref[rs, :].astype(F32) * Q_SCALE).astype(BF16)
            o_scr[rs, :] = _dot_nt(qs, st_ref[c])
        o = o_scr[...]
        r = lax.rsqrt(_head_mean(o * o, same_head) + EPS)
        gf = g_ref[...].astype(F32)
        y_ref[...] = (o * r * gn_ref[...] * (gf * _sigmoid(gf))).astype(BF16)

    return _pcall(
        body, name="gla_fwd", grid=(nb,),
        in_specs=[_row_spec(GLA_ROWS, 256), _row_spec(GLA_ROWS, 256), _row_spec(GLA_ROWS, 384),
                  _row_spec(GLA_ROWS, 384), _row_spec(GLA_ROWS, 128),
                  _full_spec((128, 256)), _full_spec((1, 256)), _full_spec((1, 384))],
        out_specs=[_row_spec(GLA_ROWS, 384), pl.BlockSpec((GLA_NC, 384, 256), lambda i: (i, 0, 0))],
        out_shape=[jax.ShapeDtypeStruct((T, 384), BF16), jax.ShapeDtypeStruct((T // CH, 384, 256), BF16)],
        scratch_shapes=[pltpu.VMEM((384, 256), F32), pltpu.VMEM((GLA_ROWS, 384), F32),
                        pltpu.VMEM((GLA_NC, 384, 256), F32)],
        sem=("arbitrary",), operands=(q, k, v, g, lr, wg, bg, gn), jobs=jobs)


def _gla_bwd(q, k, v, g, lr, states, dy, wg, bg, gn):
    T = q.shape[0]
    nb = T // GLA_ROWS

    def rev(s):
        return nb - 1 - s

    def body(q_ref, k_ref, v_ref, g_ref, lr_ref, st_ref, stp_ref, dy_ref, wg_ref, bg_ref, gn_ref,
             dq_ref, dk_ref, dv_ref, dg_ref, dlr_ref, dwg_ref, dbg_ref, dgn_ref,
             d_scr, an_scr, o_scr, do_scr, dla_scr, dst_scr):
        upper, mask_t, same_head = _gla_consts()
        s = pl.program_id(0)
        blk = rev(s)

        @pl.when(s == 0)
        def _():
            d_scr[...] = jnp.zeros_like(d_scr)
            an_scr[...] = jnp.zeros_like(an_scr)
            dwg_ref[...] = jnp.zeros_like(dwg_ref)
            dbg_ref[...] = jnp.zeros_like(dbg_ref)
            dgn_ref[...] = jnp.zeros_like(dgn_ref)

        z, la = _gla_gate(lr_ref, wg_ref, bg_ref)
        ws, as_, qss, kds = [], [], [], []
        for c in range(GLA_NC):
            rs = slice(c * CH, (c + 1) * CH)
            w, a = _gla_chunk_decay(la[rs], upper)
            ws.append(w)
            as_.append(a)
            qs = (q_ref[rs, :].astype(F32) * Q_SCALE).astype(BF16)
            qss.append(qs)
            kds.append((k_ref[rs, :].astype(F32) * w).astype(BF16))
            o_scr[rs, :] = _dot_nt(qs, st_ref[c])
        o = o_scr[...]
        r = lax.rsqrt(_head_mean(o * o, same_head) + EPS)
        on = o * r
        gf = g_ref[...].astype(F32)
        sg = _sigmoid(gf)
        si = gf * sg
        dyf = dy_ref[...].astype(F32)
        gn_ = gn_ref[...]
        dgn_ref[...] += jnp.sum(dyf * si * on, axis=0, keepdims=True)
        dg_ref[...] = (dyf * on * gn_ * (sg * (1.0 + gf * (1.0 - sg)))).astype(BF16)
        d_on = dyf * si * gn_
        do_scr[...] = r * (d_on - on * _head_mean(d_on * on, same_head))

        for c in range(GLA_NC):
            rs = slice(c * CH, (c + 1) * CH)
            dst_scr[c] = _dot_tn(do_scr[rs, :].astype(BF16), qss[c]) * mask_t
        for c in reversed(range(GLA_NC)):
            dt = d_scr[...] * an_scr[...] + dst_scr[c]
            d_scr[...] = dt
            dst_scr[c] = dt
            an_scr[...] = as_[c]
        first = (blk > 0).astype(F32)
        for c in range(GLA_NC):
            rs = slice(c * CH, (c + 1) * CH)
            dob = do_scr[rs, :].astype(BF16)
            if c > 0:
                s_prev = st_ref[c - 1].astype(F32)
            else:
                s_prev = stp_ref[0].astype(F32) * first
            dq_ref[rs, :] = (_dot(dob, st_ref[c]) * Q_SCALE).astype(BF16)
            dt = dst_scr[c]
            da = jnp.sum(dt * s_prev, axis=0, keepdims=True)
            db = dt.astype(BF16)
            dkd = _dot(v_ref[rs, :], db)
            dv_ref[rs, :] = _dot_nt(kds[c], db).astype(BF16)
            dk_ref[rs, :] = (dkd * ws[c]).astype(BF16)
            ddec = dkd * k_ref[rs, :].astype(F32) * ws[c]
            hi, lo = _split2(ddec)
            dla_scr[rs, :] = _dot_tn(upper, hi) + _dot_tn(upper, lo) + as_[c] * da

        dz = dla_scr[...] * (1.0 - _sigmoid(z)) * (1.0 / GATE_TAU)
        dzb = dz.astype(BF16)
        dlr_ref[...] = _dot_nt(dzb, wg_ref[...]).astype(BF16)
        dwg_ref[...] += _dot_tn(lr_ref[...], dzb)
        dbg_ref[...] += jnp.sum(dz, axis=0, keepdims=True)

    def rspec(n):
        return pl.BlockSpec((GLA_ROWS, n), lambda s: (rev(s), 0))

    return pl.pallas_call(
        body, name="gla_bwd", grid=(nb,),
        in_specs=[rspec(256), rspec(256), rspec(384), rspec(384), rspec(128),
                  pl.BlockSpec((GLA_NC, 384, 256), lambda s: (rev(s), 0, 0)),
                  pl.BlockSpec((1, 384, 256), lambda s: (jnp.maximum(rev(s) * GLA_NC - 1, 0), 0, 0)),
                  rspec(384), _full_spec((128, 256)), _full_spec((1, 256)), _full_spec((1, 384))],
        out_specs=[rspec(256), rspec(256), rspec(384), rspec(384), rspec(128),
                   _full_spec((128, 256)), _full_spec((1, 256)), _full_spec((1, 384))],
        out_shape=[jax.ShapeDtypeStruct((T, 256), BF16), jax.ShapeDtypeStruct((T, 256), BF16),
                   jax.ShapeDtypeStruct((T, 384), BF16), jax.ShapeDtypeStruct((T, 384), BF16),
                   jax.ShapeDtypeStruct((T, 128), BF16),
                   jax.ShapeDtypeStruct((128, 256), F32), jax.ShapeDtypeStruct((1, 256), F32),
                   jax.ShapeDtypeStruct((1, 384), F32)],
        scratch_shapes=[pltpu.VMEM((384, 256), F32), pltpu.VMEM((1, 256), F32),
                        pltpu.VMEM((GLA_ROWS, 384), F32), pltpu.VMEM((GLA_ROWS, 384), F32),
                        pltpu.VMEM((GLA_ROWS, 256), F32), pltpu.VMEM((GLA_NC, 384, 256), F32)],
        compiler_params=_cp(("arbitrary",)),
    )(q, k, v, g, lr, states, states, dy, wg, bg, gn)


CONV_ROWS = 512
HALO = 32
SUBL = 8
CONV_SLAB = 32
PHASE_ROWS = CONV_ROWS + HALO - SUBL
FWD_SHIFT = tuple(HALO - (KCONV - 1) + j for j in range(KCONV))
BWD_SHIFT = tuple(KCONV - 1 - j for j in range(KCONV))


def _fill_phases(buf, ph):
    for f in range(1, SUBL):
        ph[f, 0:PHASE_ROWS, :] = buf[pl.ds(f, PHASE_ROWS), :]


def _tap(buf, ph, shift, r, n):
    f, base = shift % SUBL, shift - shift % SUBL
    src = buf if f == 0 else ph.at[f]
    return src[pl.ds(base + r, n), :]


def _taps_apply(w_ref, buf, ph, shifts, out):
    for r in range(0, CONV_ROWS, CONV_SLAB):
        acc = jnp.zeros((CONV_SLAB, 256), F32)
        for j in range(KCONV):
            acc = acc + w_ref[j:j + 1, :] * _tap(buf, ph, shifts[j], r, CONV_SLAB)
        out[r:r + CONV_SLAB, :] = acc


def _conv_scratch():
    return [pltpu.VMEM((CONV_ROWS + HALO, 256), F32), pltpu.VMEM((SUBL, CONV_ROWS + HALO, 256), F32),
            pltpu.VMEM((CONV_ROWS, 256), F32)]


def _conv_common(cu_ref, halo_ref, w_ref, b_ref, lg_ref, lb_ref, buf, ph, cbuf, blk):
    u = cu_ref[...].astype(F32)
    a = u[:, :256]
    sb = _sigmoid(u[:, 256:])
    uh = halo_ref[...].astype(F32)
    hh = uh[:, :256] * _sigmoid(uh[:, 256:]) * (blk > 0).astype(F32)
    buf[0:HALO, :] = hh
    buf[HALO:HALO + CONV_ROWS, :] = a * sb
    _fill_phases(buf, ph)
    _taps_apply(w_ref, buf, ph, FWD_SHIFT, cbuf)
    cc = cbuf[...] + b_ref[...]
    mu = jnp.mean(cc, axis=-1, keepdims=True)
    xc = cc - mu
    rstd = lax.rsqrt(jnp.mean(xc * xc, axis=-1, keepdims=True) + EPS)
    n = xc * rstd
    yln = n * lg_ref[...] + lb_ref[...]
    return a, sb, n, rstd, yln


def _conv_fwd(cu, w, b, lg, lb, jobs=()):
    T = cu.shape[0]
    nb = T // CONV_ROWS
    per = CONV_ROWS // HALO

    def body(cu_ref, halo_ref, w_ref, b_ref, lg_ref, lb_ref, y_ref, buf, ph, cbuf):
        _, _, _, _, yln = _conv_common(cu_ref, halo_ref, w_ref, b_ref, lg_ref, lb_ref, buf, ph, cbuf,
                                       pl.program_id(0))
        y_ref[...] = (yln * _sigmoid(yln)).astype(BF16)

    return _pcall(
        body, name="conv_fwd", grid=(nb,),
        in_specs=[_row_spec(CONV_ROWS, 512),
                  pl.BlockSpec((HALO, 512), lambda i: (jnp.maximum(i * per - 1, 0), 0)),
                  _full_spec((32, 256)), _full_spec((1, 256)), _full_spec((1, 256)), _full_spec((1, 256))],
        out_specs=[_row_spec(CONV_ROWS, 256)],
        out_shape=[jax.ShapeDtypeStruct((T, 256), BF16)],
        scratch_shapes=_conv_scratch(),
        sem=("arbitrary",), operands=(cu, cu, w, b, lg, lb), jobs=jobs)


def _conv_bwd(cu, dy, w, b, lg, lb, jobs=()):
    T = cu.shape[0]
    nb = T // CONV_ROWS
    per = CONV_ROWS // HALO

    def rev(s):
        return nb - 1 - s

    def body(cu_ref, halo_ref, dy_ref, w_ref, b_ref, lg_ref, lb_ref,
             dcu_ref, dw_ref, db_ref, dlg_ref, dlb_ref, buf, ph, cbuf, dcbuf, dph, carry):
        s = pl.program_id(0)

        @pl.when(s == 0)
        def _():
            carry[...] = jnp.zeros_like(carry)
            dw_ref[...] = jnp.zeros_like(dw_ref)
            db_ref[...] = jnp.zeros_like(db_ref)
            dlg_ref[...] = jnp.zeros_like(dlg_ref)
            dlb_ref[...] = jnp.zeros_like(dlb_ref)

        a, sb, n, rstd, yln = _conv_common(cu_ref, halo_ref, w_ref, b_ref, lg_ref, lb_ref, buf, ph, cbuf, rev(s))
        sg = _sigmoid(yln)
        dyln = dy_ref[...].astype(F32) * (sg * (1.0 + yln * (1.0 - sg)))
        dlg_ref[...] += jnp.sum(dyln * n, axis=0, keepdims=True)
        dlb_ref[...] += jnp.sum(dyln, axis=0, keepdims=True)
        dn = dyln * lg_ref[...]
        dc = rstd * (dn - jnp.mean(dn, axis=-1, keepdims=True) - n * jnp.mean(dn * n, axis=-1, keepdims=True))
        db_ref[...] += jnp.sum(dc, axis=0, keepdims=True)
        dcbuf[0:CONV_ROWS, :] = dc
        dcbuf[CONV_ROWS:CONV_ROWS + HALO, :] = carry[...]
        carry[...] = dc[0:HALO, :]
        _fill_phases(dcbuf, dph)
        for j in range(KCONV):
            acc = jnp.zeros((SUBL, 256), F32)
            for r in range(0, CONV_ROWS, 2 * CONV_SLAB):
                prod = dcbuf[r:r + 2 * CONV_SLAB, :] * _tap(buf, ph, FWD_SHIFT[j], r, 2 * CONV_SLAB)
                acc = acc + jnp.sum(prod.reshape(2 * CONV_SLAB // SUBL, SUBL, 256), axis=0)
            dw_ref[j:j + 1, :] += jnp.sum(acc, axis=0, keepdims=True)
        _taps_apply(w_ref, dcbuf, dph, BWD_SHIFT, cbuf)
        dhg = cbuf[...]
        dcu_ref[...] = jnp.concatenate([dhg * sb, dhg * a * sb * (1.0 - sb)], axis=1).astype(BF16)

    def rspec(n):
        return pl.BlockSpec((CONV_ROWS, n), lambda s: (rev(s), 0))

    return _pcall(
        body, name="conv_bwd", grid=(nb,),
        in_specs=[rspec(512),
                  pl.BlockSpec((HALO, 512), lambda s: (jnp.maximum(rev(s) * per - 1, 0), 0)),
                  rspec(256),
                  _full_spec((32, 256)), _full_spec((1, 256)), _full_spec((1, 256)), _full_spec((1, 256))],
        out_specs=[rspec(512), _full_spec((32, 256)), _full_spec((1, 256)), _full_spec((1, 256)),
                   _full_spec((1, 256))],
        out_shape=[jax.ShapeDtypeStruct((T, 512), BF16), jax.ShapeDtypeStruct((32, 256), F32),
                   jax.ShapeDtypeStruct((1, 256), F32), jax.ShapeDtypeStruct((1, 256), F32),
                   jax.ShapeDtypeStruct((1, 256), F32)],
        scratch_shapes=_conv_scratch() + [pltpu.VMEM((CONV_ROWS + HALO, 256), F32),
                                          pltpu.VMEM((SUBL, CONV_ROWS + HALO, 256), F32),
                                          pltpu.VMEM((HALO, 256), F32)],
        sem=("arbitrary",), operands=(cu, cu, dy, w, b, lg, lb), jobs=jobs)


def _rel_onehot_t(shift=0):
    r = lax.broadcasted_iota(jnp.int32, (384, RB_W), 0)
    n = lax.broadcasted_iota(jnp.int32, (384, RB_W), 1) - shift
    idx = jnp.clip(1024 - n, -128, 128) + 128
    return (idx == r).astype(BF16)


def _relbias_expand(rb):
    def body(rb_ref, out_ref):
        oh = _rel_onehot_t()
        hi, mid, lo = _split3(rb_ref[...])
        strip = _dot(hi, oh) + _dot(mid, oh) + _dot(lo, oh)
        qi = _group(lax.broadcasted_iota(jnp.int32, (AQ_BLK, AK_WIN), 0), CH, 4)
        kj = _group(lax.broadcasted_iota(jnp.int32, (AQ_BLK, AK_WIN), 1), CH, 12)
        valid = (kj >= qi) & (kj <= qi + 8)
        for hd in range(6):
            x = jnp.broadcast_to(strip[hd:hd + 1, :], (AQ_BLK, RB_W))
            xr = pltpu.roll(x, 0, 1, stride=1, stride_axis=0)
            out_ref[hd] = jnp.where(valid, xr[:, 512:512 + AK_WIN], NEG)

    return pl.pallas_call(
        body, name="relbias_expand",
        out_shape=jax.ShapeDtypeStruct((6, AQ_BLK, AK_WIN), F32),
        compiler_params=_cp(),
    )(rb)


def _relbias_grad(dbias):
    def body(db_ref, out_ref):
        oh = _rel_onehot_t(AQ_BLK - 1)
        ri = lax.broadcasted_iota(jnp.int32, (AQ_BLK, AQ_BLK), 0)
        ci = lax.broadcasted_iota(jnp.int32, (AQ_BLK, AQ_BLK), 1)
        flip = (ri + ci == AQ_BLK - 1).astype(BF16)
        rows = []
        for hd in range(6):
            hi, mid, lo = _split3(db_ref[hd])
            rev = _dot(flip, hi) + _dot(flip, mid) + _dot(flip, lo)
            x = jnp.concatenate([jnp.zeros((AQ_BLK, 512), F32), rev,
                                 jnp.zeros((AQ_BLK, RB_W - 512 - AK_WIN), F32)], axis=1)
            xr = pltpu.roll(x, 0, 1, stride=1, stride_axis=0)
            rows.append(jnp.sum(xr, axis=0, keepdims=True))
        rows.append(jnp.zeros((2, RB_W), F32))
        dstrip = jnp.concatenate(rows, axis=0)
        hi, mid, lo = _split3(dstrip)
        out_ref[...] = _dot_nt(hi, oh) + _dot_nt(mid, oh) + _dot_nt(lo, oh)

    return pl.pallas_call(
        body, name="relbias_grad",
        out_shape=jax.ShapeDtypeStruct((8, 384), F32),
        compiler_params=_cp(),
    )(dbias)


ATT_SLAB = 8


def _att_logits_slab(s_scr, b_ref, hd, rows, first_key):
    kvalid = lax.broadcasted_iota(jnp.int32, (ATT_SLAB, AK_WIN), 1) >= first_key
    return jnp.where(kvalid, s_scr[rows, :] + b_ref[hd, rows, :], NEG)


def _att_softmax_slab(s_scr, b_ref, hd, rows, first_key):
    s = _att_logits_slab(s_scr, b_ref, hd, rows, first_key)
    m = jnp.max(s, axis=-1, keepdims=True)
    p = jnp.exp(s - m)
    total = jnp.sum(p, axis=-1, keepdims=True)
    return p * (1.0 / total), m + jnp.log(total)


def _att_first_key(i):
    return (8 - 4 * i) * CH


def _slab_rows(t):
    return pl.ds(t * ATT_SLAB, ATT_SLAB)


def _head_lanes(hd):
    return slice(hd * 64, (hd + 1) * 64)


WIN_BLKS = AK_WIN // AQ_BLK


def _head_tile(hd):
    return slice(hd * HEAD_PAD, (hd + 1) * HEAD_PAD)


def _win_cols(d):
    return slice(d * AQ_BLK, (d + 1) * AQ_BLK)


def _win_block(i, d):
    return jnp.maximum(i + d - WIN_LEFT, 0)


def _win_specs():
    return [pl.BlockSpec((AQ_BLK, ATT_WIDE), lambda i, d=d: (_win_block(i, d), 0)) for d in range(WIN_BLKS)]


def _att_fwd(q, k, v, bias, jobs=()):
    T = q.shape[0]
    nb = T // AQ_BLK

    def body(q_ref, k0, k1, k2, v0, v1, v2, b_ref, o_ref, lse_ref, s_scr):
        k_refs, v_refs = (k0, k1, k2), (v0, v1, v2)
        first_key = _att_first_key(pl.program_id(0))
        lse_ref[...] = jnp.zeros_like(lse_ref)

        def scores(hd):
            q_h = q_ref[:, _head_tile(hd)] * A_SCALE
            for d in range(WIN_BLKS):
                s_scr[hd % 2, :, _win_cols(d)] = _dot_nt(q_h, k_refs[d][:, _head_tile(hd)])

        scores(0)
        for hd in range(ATT_HEADS):
            if hd + 1 < ATT_HEADS:
                scores(hd + 1)
            s_h = s_scr.at[hd % 2]
            for t in range(AQ_BLK // ATT_SLAB):
                rows = _slab_rows(t)
                s_h[rows, :], lse_ref[rows, hd:hd + 1] = _att_softmax_slab(s_h, b_ref, hd, rows, first_key)
            o_h = _dot(s_h[:, _win_cols(0)].astype(BF16), v_refs[0][:, _head_tile(hd)])
            for d in range(1, WIN_BLKS):
                o_h = o_h + _dot(s_h[:, _win_cols(d)].astype(BF16), v_refs[d][:, _head_tile(hd)])
            o_ref[:, _head_lanes(hd)] = o_h[:, :64].astype(BF16)

    return _pcall(
        body, name="att_fwd", grid=(nb,),
        in_specs=[_row_spec(AQ_BLK, ATT_WIDE)] + _win_specs() + _win_specs() + [_full_spec((6, AQ_BLK, AK_WIN))],
        out_specs=[_row_spec(AQ_BLK, 384), _row_spec(AQ_BLK, 128)],
        out_shape=[jax.ShapeDtypeStruct((T, 384), BF16), jax.ShapeDtypeStruct((T, 128), F32)],
        scratch_shapes=[pltpu.VMEM((2, AQ_BLK, AK_WIN), F32)],
        sem=("arbitrary",), operands=(q, k, k, k, v, v, v, bias), jobs=jobs)


def _att_bwd(q, k, v, bias, o, lse, do, jobs=()):
    T = q.shape[0]
    nb = T // AQ_BLK

    def body(q_ref, k0, k1, k2, v0, v1, v2, b_ref, o_ref, lse_ref, do_ref, dq_ref, dk_ref, dv_ref, db_ref,
             dk_acc, dv_acc, s_scr, dp_scr, delta_scr):
        k_refs, v_refs = (k0, k1, k2), (v0, v1, v2)
        i = pl.program_id(0)

        @pl.when(i == 0)
        def _():
            dk_acc[...] = jnp.zeros_like(dk_acc)
            dv_acc[...] = jnp.zeros_like(dv_acc)
            db_ref[...] = jnp.zeros_like(db_ref)

        first_key = _att_first_key(i)

        def scores(hd):
            q_h = q_ref[:, _head_tile(hd)] * A_SCALE
            do_h = do_ref[:, _head_tile(hd)]
            delta_scr[:, hd:hd + 1] = jnp.sum(do_h[:, :64].astype(F32) * o_ref[:, _head_lanes(hd)].astype(F32),
                                              axis=-1, keepdims=True)
            for d in range(WIN_BLKS):
                s_scr[hd % 2, :, _win_cols(d)] = _dot_nt(q_h, k_refs[d][:, _head_tile(hd)])
                dp_scr[hd % 2, :, _win_cols(d)] = _dot_nt(do_h, v_refs[d][:, _head_tile(hd)])

        scores(0)
        for hd in range(ATT_HEADS):
            if hd + 1 < ATT_HEADS:
                scores(hd + 1)
            s_h, dp_h = s_scr.at[hd % 2], dp_scr.at[hd % 2]
            for t in range(AQ_BLK // ATT_SLAB):
                rows = _slab_rows(t)
                p = jnp.exp(_att_logits_slab(s_h, b_ref, hd, rows, first_key) - lse_ref[rows, hd:hd + 1])
                ds = p * (dp_h[rows, :] - delta_scr[rows, hd:hd + 1])
                db_ref[hd, rows, :] += ds
                s_h[rows, :] = p
                dp_h[rows, :] = ds
            q_h = q_ref[:, _head_tile(hd)] * A_SCALE
            do_h = do_ref[:, _head_tile(hd)]
            ls = _head_lanes(hd)
            dq_h = jnp.zeros((AQ_BLK, HEAD_PAD), F32)
            for d in range(WIN_BLKS):
                pb = s_h[:, _win_cols(d)].astype(BF16)
                dsb = dp_h[:, _win_cols(d)].astype(BF16)
                rows = pl.ds(pl.multiple_of(_win_block(i, d) * AQ_BLK, AQ_BLK), AQ_BLK)
                dv_acc[rows, ls] += _dot_tn(pb, do_h)[:, :64]
                dk_acc[rows, ls] += _dot_tn(dsb, q_h)[:, :64]
                dq_h = dq_h + _dot(dsb, k_refs[d][:, _head_tile(hd)])
            dq_ref[:, ls] = (dq_h[:, :64] * A_SCALE).astype(BF16)

        @pl.when(i == nb - 1)
        def _():
            dk_ref[...] = dk_acc[...].astype(BF16)
            dv_ref[...] = dv_acc[...].astype(BF16)

    return _pcall(
        body, name="att_bwd", grid=(nb,),
        in_specs=[_row_spec(AQ_BLK, ATT_WIDE)] + _win_specs() + _win_specs()
        + [_full_spec((6, AQ_BLK, AK_WIN)), _row_spec(AQ_BLK, 384), _row_spec(AQ_BLK, 128),
           _row_spec(AQ_BLK, ATT_WIDE)],
        out_specs=[_row_spec(AQ_BLK, 384), _full_spec((T, 384)), _full_spec((T, 384)),
                   _full_spec((6, AQ_BLK, AK_WIN))],
        out_shape=[jax.ShapeDtypeStruct((T, 384), BF16), jax.ShapeDtypeStruct((T, 384), BF16),
                   jax.ShapeDtypeStruct((T, 384), BF16), jax.ShapeDtypeStruct((6, AQ_BLK, AK_WIN), F32)],
        scratch_shapes=[pltpu.VMEM((T, 384), F32), pltpu.VMEM((T, 384), F32),
                        pltpu.VMEM((2, AQ_BLK, AK_WIN), F32), pltpu.VMEM((2, AQ_BLK, AK_WIN), F32),
                        pltpu.VMEM((AQ_BLK, 128), F32)],
        sem=("arbitrary",), operands=(q, k, k, k, v, v, v, bias, o, lse, do), jobs=jobs)


FF_BLK = 512
N_FF = 4096 // FF_BLK
MLP_SHARDS = 2


def _outproj_mlp_fwd(h, o_gla, o_conv, o_att, w_out, gamma, w_up, w_down, jobs=()):
    T = h.shape[0]
    tm = 512

    def body(h_ref, og_ref, oc_ref, oa_ref, wo_ref, g_ref, wu_ref, wd_ref, h1_ref, xt_ref, h2_ref, a_ref, acc, xn_ref):
        j = pl.program_id(1)

        @pl.when(j == 0)
        def _():
            wo = wo_ref[...]
            h1 = (h_ref[...] + _dot(og_ref[...], wo[0:384]) + _dot(oc_ref[...], wo[384:640])
                  + _dot(oa_ref[...], wo[640:1024]))
            h1_ref[...] = h1
            r = lax.rsqrt(jnp.mean(h1 * h1, axis=-1, keepdims=True) + EPS)
            xn = (h1 * r * g_ref[...]).astype(BF16)
            xn_ref[...] = xn
            xt_ref[...] = jnp.transpose(xn)
            acc[...] = h1

        xn_ = xn_ref[...]
        down = None
        for s in range(MLP_SHARDS):
            a = jnp.maximum(_dot(xn_, wu_ref[s]), 0.0)
            a_ref[:, s * FF_BLK:(s + 1) * FF_BLK] = a.astype(BF16)
            part = _dot((a * a).astype(BF16), wd_ref[s])
            down = part if down is None else down + part
        acc[...] += down

        @pl.when(j == N_FF // MLP_SHARDS - 1)
        def _():
            h2_ref[...] = acc[...]

    row = lambda n: pl.BlockSpec((tm, n), lambda i, j: (i, 0))
    return _pcall(
        body, name="outproj_mlp_fwd", grid=(T // tm, N_FF // MLP_SHARDS),
        in_specs=[row(D), row(384), row(256), row(384),
                  pl.BlockSpec((D, D), lambda i, j: (0, 0)), pl.BlockSpec((1, D), lambda i, j: (0, 0)),
                  pl.BlockSpec((MLP_SHARDS, D, FF_BLK), lambda i, j: (j, 0, 0)),
                  pl.BlockSpec((MLP_SHARDS, FF_BLK, D), lambda i, j: (j, 0, 0))],
        out_specs=[row(D), pl.BlockSpec((D, tm), lambda i, j: (0, i)), row(D),
                   pl.BlockSpec((tm, MLP_SHARDS * FF_BLK), lambda i, j: (i, j))],
        out_shape=[jax.ShapeDtypeStruct((T, D), F32), jax.ShapeDtypeStruct((D, T), BF16),
                   jax.ShapeDtypeStruct((T, D), F32), jax.ShapeDtypeStruct((T, N_FF * FF_BLK), BF16)],
        scratch_shapes=[pltpu.VMEM((tm, D), F32), pltpu.VMEM((tm, D), BF16)],
        sem=("arbitrary", "arbitrary"), operands=(h, o_gla, o_conv, o_att, w_out, gamma, w_up, w_down), jobs=jobs)


def _mlp_bwd(xn2t, act, h1, dh2, dh2b, gamma, w_up, w_down, jobs=()):
    T = act.shape[0]
    tm = 512
    nt = T // tm
    ns = MLP_SHARDS
    nj = N_FF // ns
    last = nj - 1

    def body(xt_ref, a_ref, h1_ref, dy_ref, dyb_ref, g_ref, wu_ref, wd_ref, dh1_ref, dwu_ref, dwd_ref, dg_ref,
             dxn_acc, acc_u, acc_d):
        j = pl.program_id(0)
        i = pl.program_id(1)
        xt = xt_ref[...]
        dyb = dyb_ref[...]
        rows = pl.ds(pl.multiple_of(i * tm, tm), tm)

        @pl.when(i == 0)
        def _():
            acc_u[...] = jnp.zeros_like(acc_u)
            acc_d[...] = jnp.zeros_like(acc_d)

        @pl.when(j == 0)
        def _():
            dxn_acc[rows, :] = jnp.zeros((tm, D), F32)

        dxn = None
        for s in range(ns):
            a = a_ref[:, s * FF_BLK:(s + 1) * FF_BLK].astype(F32)
            hh = (a * a).astype(BF16)
            du = (_dot_nt(dyb, wd_ref[s]) * (2.0 * a)).astype(BF16)
            acc_d[s] += _dot_tn(hh, dyb)
            acc_u[s] += _dot(xt, du)
            part = _dot_nt(du, wu_ref[s])
            dxn = part if dxn is None else dxn + part
        dxn_acc[rows, :] += dxn

        @pl.when(i == nt - 1)
        def _():
            for s in range(ns):
                dwu_ref[s, 0] = acc_u[s].astype(BF16)
                dwd_ref[s, 0] = acc_d[s].astype(BF16)

        @pl.when(j == last)
        def _():
            @pl.when(i == 0)
            def _():
                dg_ref[...] = jnp.zeros_like(dg_ref)

            h1 = h1_ref[...]
            r = lax.rsqrt(jnp.mean(h1 * h1, axis=-1, keepdims=True) + EPS)
            dx, dgam = _rms_bwd(dxn_acc[rows, :], h1, r, g_ref[...])
            dh1_ref[...] = dy_ref[...] + dx
            dg_ref[...] += dgam

    assert ns == 2
    late = lambda j, i: (jnp.where(j == last, i, 0), 0)
    return _pcall(
        body, name="mlp_bwd", grid=(nj, nt),
        in_specs=[pl.BlockSpec((D, tm), lambda j, i: (0, i)), pl.BlockSpec((tm, ns * FF_BLK), lambda j, i: (i, j)),
                  pl.BlockSpec((tm, D), late), pl.BlockSpec((tm, D), late),
                  pl.BlockSpec((tm, D), lambda j, i: (i, 0)), pl.BlockSpec((1, D), lambda j, i: (0, 0)),
                  pl.BlockSpec((ns, D, FF_BLK), lambda j, i: (j, 0, 0), pipeline_mode=pl.Buffered(1)),
                  pl.BlockSpec((ns, FF_BLK, D), lambda j, i: (j, 0, 0), pipeline_mode=pl.Buffered(1))],
        out_specs=[pl.BlockSpec((tm, D), late),
                   pl.BlockSpec((ns, 1, D, FF_BLK), lambda j, i: (0, j, 0, 0)),
                   pl.BlockSpec((ns, 1, FF_BLK, D), lambda j, i: (0, j, 0, 0)),
                   pl.BlockSpec((1, D), lambda j, i: (0, 0))],
        out_shape=[jax.ShapeDtypeStruct((T, D), F32), jax.ShapeDtypeStruct((2, 4, D, FF_BLK), BF16),
                   jax.ShapeDtypeStruct((2, 4, FF_BLK, D), BF16), jax.ShapeDtypeStruct((1, D), F32)],
        scratch_shapes=[pltpu.VMEM((T, D), F32), pltpu.VMEM((ns, D, FF_BLK), F32), pltpu.VMEM((ns, FF_BLK, D), F32)],
        sem=("arbitrary", "arbitrary"), operands=(xn2t, act, h1, dh2, dh2b, gamma, w_up, w_down), jobs=jobs)


def _outproj_bwd(dh1, o_gla, o_conv, o_att, w_out, jobs=()):
    T = dh1.shape[0]
    tm = 512
    nt = T // tm

    def body(dy_ref, og_ref, oc_ref, oa_ref, wo_ref, dg_ref, dc_ref, da_ref, dw_ref, acc):
        i = pl.program_id(0)

        @pl.when(i == 0)
        def _():
            acc[...] = jnp.zeros_like(acc)

        dyb = dy_ref[...].astype(BF16)
        dm = _dot_nt(dyb, wo_ref[...])
        dg_ref[...] = dm[:, 0:384].astype(BF16)
        dc_ref[...] = dm[:, 384:640].astype(BF16)
        _store_head_padded(da_ref, dm[:, 640:1024].astype(BF16))
        mixed = jnp.concatenate([og_ref[...], oc_ref[...], oa_ref[...]], axis=1)
        acc[...] += _dot_tn(mixed, dyb)

        @pl.when(i == nt - 1)
        def _():
            for j in range(N_DEV):
                dw_ref[j % 2, j // 2] = acc[j * 128:(j + 1) * 128, :].astype(BF16)

    return _pcall(
        body, name="outproj_bwd", grid=(nt,),
        in_specs=[_row_spec(tm, D), _row_spec(tm, 384), _row_spec(tm, 256), _row_spec(tm, 384),
                  _full_spec((D, D))],
        out_specs=[_row_spec(tm, 384), _row_spec(tm, 256), _row_spec(tm, ATT_WIDE), _full_spec((2, 4, 128, D))],
        out_shape=[jax.ShapeDtypeStruct((T, 384), BF16), jax.ShapeDtypeStruct((T, 256), BF16),
                   jax.ShapeDtypeStruct((T, ATT_WIDE), BF16), jax.ShapeDtypeStruct((2, 4, 128, D), BF16)],
        scratch_shapes=[pltpu.VMEM((D, D), F32)],
        sem=("arbitrary",), operands=(dh1, o_gla, o_conv, o_att, w_out), jobs=jobs)


def _loss_fwd_bwd(h, gamma, target):
    T = h.shape[0]
    tm = 512

    def body(h_ref, g_ref, t_ref, loss_ref, dh_ref, dhb_ref, dg_ref):
        @pl.when(pl.program_id(0) == 0)
        def _():
            loss_ref[...] = jnp.zeros_like(loss_ref)
            dg_ref[...] = jnp.zeros_like(dg_ref)

        x = h_ref[...]
        r = lax.rsqrt(jnp.mean(x * x, axis=-1, keepdims=True) + EPS)
        gamma_ = g_ref[...]
        e = x * r * gamma_ - t_ref[...]
        loss_ref[...] += 0.5 * jnp.sum(jnp.mean(e * e, axis=-1, keepdims=True), axis=0, keepdims=True)
        dx, dgam = _rms_bwd(e * (1.0 / D), x, r, gamma_)
        dh_ref[...] = dx
        dhb_ref[...] = dx.astype(BF16)
        dg_ref[...] += dgam

    return pl.pallas_call(
        body, name="loss_fwd_bwd", grid=(T // tm,),
        in_specs=[_row_spec(tm, D), _full_spec((1, D)), _row_spec(tm, D)],
        out_specs=[_full_spec((8, 128)), _row_spec(tm, D), _row_spec(tm, D), _full_spec((1, D))],
        out_shape=[jax.ShapeDtypeStruct((8, 128), F32), jax.ShapeDtypeStruct((T, D), F32),
                   jax.ShapeDtypeStruct((T, D), BF16), jax.ShapeDtypeStruct((1, D), F32)],
        compiler_params=_cp(("arbitrary",)),
    )(h, gamma, target)


def _adamw_math(w, g, m, v):
    m = ADAM_B1 * m + (1.0 - ADAM_B1) * g
    v = ADAM_B2 * v + (1.0 - ADAM_B2) * (g * g)
    m_hat = m / (1.0 - ADAM_B1 ** ADAM_STEP)
    v_hat = v / (1.0 - ADAM_B2 ** ADAM_STEP)
    delta = -ADAM_LR * (m_hat / (jnp.sqrt(v_hat) + ADAM_EPS) + ADAM_WD * w)
    return delta, m, v


def _rs_adamw(a_own, r2, w, m, v, layer, chip_idx, rows_blk, prev=None):
    _, R, C = w.shape
    nblk = R // rows_blk

    def body(chip_ref, a_ref, r_ref, w_ref, m_ref, v_ref, *rest):
        g_out, d_out, m_out, v_out = rest[-4:]
        g = (a_ref[0].astype(F32) + r_ref[0].astype(F32)) + (r_ref[1].astype(F32) + r_ref[2].astype(F32))
        delta, m_new, v_new = _adamw_math(w_ref[0], g, m_ref[0], v_ref[0])
        g_out[0] = g
        d_out[0] = delta
        m_out[0] = m_new
        v_out[0] = v_new

    blk = pl.BlockSpec((1, rows_blk, C), lambda i, chip: (layer, i, 0))
    n_prev = 0 if prev is None else 4
    grid_spec = pltpu.PrefetchScalarGridSpec(
        num_scalar_prefetch=1, grid=(nblk,),
        in_specs=[pl.BlockSpec((1, rows_blk, C), lambda i, chip: (chip[0], i, 0)),
                  pl.BlockSpec((3, rows_blk, C), lambda i, chip: (0, i, 0)), blk, blk, blk]
        + [_any_spec()] * n_prev,
        out_specs=[blk, blk, blk, blk])
    return pl.pallas_call(
        body, name="rs_adamw", grid_spec=grid_spec,
        out_shape=[jax.ShapeDtypeStruct((DEPTH, R, C), F32)] * 4,
        input_output_aliases={6 + t: t for t in range(n_prev)},
        compiler_params=_cp(("arbitrary",)),
    )(chip_idx, a_own, r2, w, m, v, *(prev or ()))


def _pair_sum(g, r1, core_idx, rows_blk):
    _, _, R, C = g.shape
    nblk = R // rows_blk

    def body(core_ref, g_ref, r_ref, o_ref):
        o_ref[...] = (g_ref[0].astype(F32) + r_ref[...].astype(F32)).astype(BF16)

    grid_spec = pltpu.PrefetchScalarGridSpec(
        num_scalar_prefetch=1, grid=(4, nblk),
        in_specs=[pl.BlockSpec((1, 1, rows_blk, C), lambda k, i, core: (core[0], k, i, 0)),
                  pl.BlockSpec((1, rows_blk, C), lambda k, i, core: (k, i, 0))],
        out_specs=pl.BlockSpec((1, rows_blk, C), lambda k, i, core: (k, i, 0)))
    return pl.pallas_call(
        body, name="rs_pair_sum", grid_spec=grid_spec,
        out_shape=jax.ShapeDtypeStruct((4, R, C), BF16),
        compiler_params=_cp(("arbitrary", "arbitrary")),
    )(core_idx, g, r1)


def _small_sum(gathered):
    def body(g_ref, o_ref):
        acc = g_ref[0]
        for d in range(1, N_DEV):
            acc = acc + g_ref[d]
        o_ref[...] = acc

    return pl.pallas_call(
        body, name="small_sum",
        out_shape=jax.ShapeDtypeStruct(gathered.shape[1:], F32),
        compiler_params=_cp(),
    )(gathered)


def _adamw_small(ws, gs, ms, vs):
    n = len(ws)

    def body(*refs):
        w_r, g_r, m_r, v_r = refs[0:n], refs[n:2 * n], refs[2 * n:3 * n], refs[3 * n:4 * n]
        d_o, m_o, v_o = refs[4 * n:5 * n], refs[5 * n:6 * n], refs[6 * n:7 * n]
        for t in range(n):
            delta, m_new, v_new = _adamw_math(w_r[t][...], g_r[t][...], m_r[t][...], v_r[t][...])
            d_o[t][...] = delta
            m_o[t][...] = m_new
            v_o[t][...] = v_new

    shapes = [jax.ShapeDtypeStruct(w.shape, F32) for w in ws]
    outs = pl.pallas_call(
        body, name="adamw_small", out_shape=shapes * 3, compiler_params=_cp(),
    )(*ws, *gs, *ms, *vs)
    return outs[0:n], outs[n:2 * n], outs[2 * n:3 * n]


def _mesh_pos():
    return lax.axis_index("x"), lax.axis_index("y"), lax.axis_index("c")


def _peers():
    x, y, c = _mesh_pos()
    return (x, y, c), (x, y, 1 - c), [(1 - x, y), (x, 1 - y), (1 - x, 1 - y)]


def _slot(ref, pos):
    return ref.at[4 * pos[0] + 2 * pos[1] + pos[2]]


def _remote(src, dst, send_sem, recv_sem, to):
    return pltpu.make_async_remote_copy(src_ref=src, dst_ref=dst, send_sem=send_sem, recv_sem=recv_sem,
                                        device_id=to, device_id_type=MESH)


def _ag_spread(shards):
    n = len(shards)

    def copies(ins, outs, sems):
        send, recv, loc = sems
        me, sibling, chips = _peers()
        peers = [sibling] + [(*chip, me[2]) for chip in chips]
        local = [pltpu.make_async_copy(ins[a], _slot(outs[a], me), loc.at[a]) for a in range(n)]
        sends = [_remote(ins[a], _slot(outs[a], me), send.at[a, k], recv.at[a, k], p)
                 for a in range(n) for k, p in enumerate(peers)]
        recvs = [_remote(ins[a], _slot(outs[a], p), send.at[a, k], recv.at[a, k], p)
                 for a in range(n) for k, p in enumerate(peers)]
        return local, sends, recvs

    def start(ins, outs, sems):
        local, sends, _ = copies(ins, outs, sems)
        for cp in local + sends:
            cp.start()

    def finish(ins, outs, sems):
        local, sends, recvs = copies(ins, outs, sems)
        for cp in sends:
            cp.wait_send()
        for cp in recvs:
            cp.wait_recv()
        for cp in local:
            cp.wait()

    return _Job(shards, [jax.ShapeDtypeStruct((N_DEV,) + a.shape, a.dtype) for a in shards],
                [pltpu.SemaphoreType.DMA((n, 4)), pltpu.SemaphoreType.DMA((n, 4)), pltpu.SemaphoreType.DMA((n,))],
                start, finish)


def _ag_pass(stacks):
    n = len(stacks)

    def copies(ins, outs, sems):
        send, recv = sems
        me, sibling, chips = _peers()
        sends = [_remote(_slot(ins[a], (*chip, me[2])), _slot(outs[a], (*chip, me[2])), send.at[a, j], recv.at[a, j],
                         sibling) for a in range(n) for j, chip in enumerate(chips)]
        recvs = [_remote(_slot(ins[a], (*chip, me[2])), _slot(outs[a], (*chip, 1 - me[2])), send.at[a, j],
                         recv.at[a, j], sibling) for a in range(n) for j, chip in enumerate(chips)]
        return sends, recvs

    def start(ins, outs, sems):
        for cp in copies(ins, outs, sems)[0]:
            cp.start()

    def finish(ins, outs, sems):
        sends, recvs = copies(ins, outs, sems)
        for cp in sends:
            cp.wait_send()
        for cp in recvs:
            cp.wait_recv()

    return _Job(stacks, [jax.ShapeDtypeStruct(a.shape, a.dtype) for a in stacks],
                [pltpu.SemaphoreType.DMA((n, 3)), pltpu.SemaphoreType.DMA((n, 3))],
                start, finish, aliases={a: a for a in range(n)})


def _ag_both(shards):
    spread = _ag_spread(shards)
    fake = [jax.ShapeDtypeStruct((N_DEV,) + a.shape, a.dtype) for a in shards]
    onward = _ag_pass(fake)
    n_sp = len(spread.sems)

    def start(ins, outs, sems):
        spread.start(ins, outs, sems[:n_sp])

    def finish(ins, outs, sems):
        spread.finish(ins, outs, sems[:n_sp])
        onward.start(outs, outs, sems[n_sp:])
        onward.finish(outs, outs, sems[n_sp:])

    return _Job(shards, spread.out_shapes, spread.sems + onward.sems, start, finish)


def _rs_swap(parts):
    n = len(parts)

    def copies(ins, outs, sems):
        send, recv = sems
        me, sibling, _ = _peers()
        return [_remote(ins[a].at[1 - me[2]], outs[a], send.at[a], recv.at[a], sibling) for a in range(n)]

    def start(ins, outs, sems):
        for cp in copies(ins, outs, sems):
            cp.start()

    def finish(ins, outs, sems):
        for cp in copies(ins, outs, sems):
            cp.wait()

    return _Job(parts, [jax.ShapeDtypeStruct(a.shape[1:], a.dtype) for a in parts],
                [pltpu.SemaphoreType.DMA((n,)), pltpu.SemaphoreType.DMA((n,))], start, finish)


def _rs_ici(pairs):
    n = len(pairs)

    def copies(ins, outs, sems):
        send, recv = sems
        me, _, chips = _peers()
        return [_remote(ins[a].at[2 * chip[0] + chip[1]], outs[a].at[j], send.at[a, j], recv.at[a, j],
                        (*chip, me[2])) for a in range(n) for j, chip in enumerate(chips)]

    def start(ins, outs, sems):
        for cp in copies(ins, outs, sems):
            cp.start()

    def finish(ins, outs, sems):
        for cp in copies(ins, outs, sems):
            cp.wait()

    return _Job(pairs, [jax.ShapeDtypeStruct((3,) + a.shape[1:], a.dtype) for a in pairs],
                [pltpu.SemaphoreType.DMA((n, 3)), pltpu.SemaphoreType.DMA((n, 3))], start, finish)


def _comm_call(jobs, name):
    def body():
        pass

    return _pcall(body, name=name, grid=(), in_specs=[], out_specs=[], out_shape=[], operands=(), jobs=jobs)[1]


def _allgather(arrs, name):
    n = len(arrs)

    def body(*refs):
        ins, outs = refs[:n], refs[n:2 * n]
        send_sems, recv_sems, local_sems = refs[2 * n:]
        x, y, c = _mesh_pos()
        me, sibling = (x, y, c), (x, y, 1 - c)
        chips = [(1 - x, y), (x, 1 - y), (1 - x, 1 - y)]

        def slot(a, pos):
            return outs[a].at[4 * pos[0] + 2 * pos[1] + pos[2]]

        def copy(a, k, block, to, src=None):
            return pltpu.make_async_remote_copy(
                src_ref=slot(a, block) if src is None else src, dst_ref=slot(a, block),
                send_sem=send_sems.at[a, k], recv_sem=recv_sems.at[a, k],
                device_id=to, device_id_type=MESH)

        mine = [pltpu.make_async_copy(ins[a], slot(a, me), local_sems.at[a]) for a in range(n)]
        for cp in mine:
            cp.start()
        first = []
        for a in range(n):
            first.append(copy(a, 0, me, sibling, src=ins[a]))
            first += [copy(a, 1 + j, me, (*chip, c), src=ins[a]) for j, chip in enumerate(chips)]
        for cp in first:
            cp.start()
        passed = []
        for j, chip in enumerate(chips):
            for a in range(n):
                copy(a, 1 + j, (*chip, c), me).wait_recv()
                fwd = copy(a, 4 + j, (*chip, c), sibling)
                fwd.start()
                passed.append(fwd)
        for a in range(n):
            copy(a, 0, sibling, me).wait_recv()
            for j, chip in enumerate(chips):
                copy(a, 4 + j, (*chip, 1 - c), me).wait_recv()
        for cp in first + passed:
            cp.wait_send()
        for cp in mine:
            cp.wait()

    return pl.pallas_call(
        body, name=name,
        in_specs=[_any_spec()] * n, out_specs=[_any_spec()] * n,
        out_shape=[jax.ShapeDtypeStruct((N_DEV,) + a.shape, a.dtype) for a in arrs],
        scratch_shapes=[pltpu.SemaphoreType.DMA((n, 7)), pltpu.SemaphoreType.DMA((n, 7)),
                        pltpu.SemaphoreType.DMA((n,))],
        compiler_params=_cp(),
    )(*arrs)


W_IN_SHARD = 354
W_IN_COLS = ((0, 192, OQ), (192, 192, OKK), (384, 384, OV), (768, 384, OG), (1152, 16, OLR), (1168, 512, OCU),
             (1680, 384, OAQ), (2064, 384, OAK), (2448, 384, OAV))


def _w_in_padded(stack):
    new_to_ref = {new: (start, width) for start, width, new in W_IN_COLS}
    cols = []
    for new, padded in IN_GROUPS:
        start, width = new_to_ref[new]
        a = start
        while a < start + width:
            j = a // W_IN_SHARD
            b = min(start + width, (j + 1) * W_IN_SHARD)
            cols.append(stack[j][:, a - j * W_IN_SHARD:b - j * W_IN_SHARD])
            a = b
        if padded > width:
            cols.append(jnp.zeros((stack.shape[1], padded - width), stack.dtype))
    return jnp.concatenate(cols, axis=1)


def _dw_in_shards(dw):
    shards = []
    for j in range(N_DEV):
        lo, hi = j * W_IN_SHARD, (j + 1) * W_IN_SHARD
        segs = []
        for start, width, new in W_IN_COLS:
            a, b = max(lo, start), min(hi, start + width)
            if a < b:
                segs.append(dw[:, new + a - start:new + b - start])
        shards.append(jnp.concatenate(segs, axis=1))
    return jnp.stack([jnp.stack([shards[2 * chip + core] for chip in range(4)]) for core in range(2)])


def _pad_to(a, shape):
    return jnp.pad(a, [(0, s - d) for d, s in zip(a.shape, shape)])


SMALL_LAYOUT = (
    ("norm_mix", 2, 1024), ("norm_ffn", 2, 1024), ("norm_final", 1, 1024), ("gla_norm", 2, 384),
    ("b_gla_gate", 2, 192), ("b_dw", 2, 256), ("conv_ln_g", 2, 256), ("conv_ln_b", 2, 256),
    ("rel_bias", 12, 257), ("w_gla_gate", 32, 192), ("w_dw", 62, 256),
)
SMALL_LANES = 128
SMALL_TILE = 8 * SMALL_LANES


def _small_tile_rows(r, lanes):
    return -(-(r * lanes) // SMALL_TILE) * 8


SMALL_ROWS = sum(_small_tile_rows(r, lanes) for _, r, lanes in SMALL_LAYOUT)


def _pack_small(parts):
    tiles = []
    for name, r, lanes in SMALL_LAYOUT:
        rows = _small_tile_rows(r, lanes)
        flat = _pad_to(parts[name].reshape(r * lanes), (rows * SMALL_LANES,))
        tiles.append(flat.reshape(rows, SMALL_LANES))
    return jnp.concatenate(tiles, axis=0)


def _unpack_small(packed):
    out, r0 = {}, 0
    for name, r, lanes in SMALL_LAYOUT:
        rows = _small_tile_rows(r, lanes)
        out[name] = packed[r0:r0 + rows].reshape(rows * SMALL_LANES)[:r * lanes].reshape(r, lanes)
        r0 += rows
    return out


def _mixers_fwd(h, wl, w_in_p, plan=None):
    plan, res = plan or {}, {}

    def jobs(host):
        return plan[host](res) if host in plan else ()

    (q, k, v, g, cu, aq, ak, av, lr), res["inproj"] = _inproj_fwd(h, wl["norm_mix"], w_in_p, jobs=jobs("inproj"))
    bias = _relbias_expand(wl["rb"])
    (o_att, lse), res["att"] = _att_fwd(aq, ak, av, bias, jobs=jobs("att"))
    (o_gla, states), res["gla"] = _gla_fwd(q, k, v, g, lr, wl["wg"], wl["bg"], wl["gn"], jobs=jobs("gla"))
    (o_conv,), res["conv"] = _conv_fwd(cu, wl["w_dw"], wl["b_dw"], wl["ln_g"], wl["ln_b"], jobs=jobs("conv"))
    sv = dict(h=h, w_in=w_in_p, q=q, k=k, v=v, g=g, cu=cu, aq=aq, ak=ak, av=av, lr=lr,
              o_gla=o_gla, o_conv=o_conv, o_att=o_att, lse=lse, states=states, bias=bias)
    return sv, res


def _mixers_bwd(sv, wl, dh1, d_ogla, d_oconv, att_grads, conv_jobs=(), x_jobs_fn=None):
    daq, dak, dav, dbias = att_grads
    d_rb = _relbias_grad(dbias)
    (dcu, dw_dw, db_dw, dln_g, dln_b), conv_res = _conv_bwd(
        sv["cu"], d_oconv, wl["w_dw"], wl["b_dw"], wl["ln_g"], wl["ln_b"], jobs=conv_jobs)
    dq, dk, dv, dg, dlr, dwg, dbg, dgn = _gla_bwd(sv["q"], sv["k"], sv["v"], sv["g"], sv["lr"], sv["states"],
                                                  d_ogla, wl["wg"], wl["bg"], wl["gn"])
    dparts = (dq, dk, dv, dg, dcu, daq, dak, dav, dlr)
    dw_in = _inproj_bwd_w(sv["h"], wl["norm_mix"], dparts)
    x_jobs = x_jobs_fn(dw_in) if x_jobs_fn is not None else ()
    (dh, dhb, d_nmix), x_res = _inproj_bwd_x(sv["h"], dh1, wl["norm_mix"], sv["w_in"], dparts, jobs=x_jobs)
    small = dict(norm_mix=d_nmix, wg=dwg, bg=dbg, gn=dgn, w_dw=dw_dw, b_dw=db_dw, ln_g=dln_g, ln_b=dln_b, rb=d_rb)
    return (dh, dhb), dw_in, small, conv_res, x_res


def _layer_small(l, w_dw_full, norm_mix, w_gla_gate, b_gla_gate, gla_norm, b_dw, conv_ln_g, conv_ln_b, rel_bias,
                 norm_ffn):
    return dict(
        norm_mix=norm_mix[l][None, :], norm_ffn=norm_ffn[l][None, :],
        wg=_pad_to(w_gla_gate[l], (128, 256)).astype(BF16), bg=_pad_to(b_gla_gate[l][None, :], (1, 256)),
        gn=gla_norm[l][None, :], w_dw=_pad_to(w_dw_full, (32, 256)), b_dw=b_dw[l][None, :],
        ln_g=conv_ln_g[l][None, :], ln_b=conv_ln_b[l][None, :], rb=_pad_to(rel_bias[l], (8, 384)))


RS_ROWS = dict(w_in=512, w_out=128, w_up=512, w_down=256)


def kernel(x, norm_mix, w_in, w_gla_gate, b_gla_gate, gla_norm, w_dw, b_dw, conv_ln_g, conv_ln_b, rel_bias, w_out, norm_ffn, w_up, w_down, norm_final, loss_target, m_norm_mix, m_w_in, m_w_gla_gate, m_b_gla_gate, m_gla_norm, m_w_dw, m_b_dw, m_conv_ln_g, m_conv_ln_b, m_rel_bias, m_w_out, m_norm_ffn, m_w_up, m_w_down, m_norm_final, v_norm_mix, v_w_in, v_w_gla_gate, v_b_gla_gate, v_gla_norm, v_w_dw, v_b_dw, v_conv_ln_g, v_conv_ln_b, v_rel_bias, v_w_out, v_norm_ffn, v_w_up, v_w_down, v_norm_final):
    mx, my, mc = _mesh_pos()
    me = 4 * mx + 2 * my + mc
    chip_idx = (2 * mx + my).astype(jnp.int32).reshape(1)
    core_idx = mc.astype(jnp.int32).reshape(1)
    x0, target = x[0], loss_target[0]

    def pair_sums(parts, r1):
        return [_pair_sum(p, r, core_idx, p.shape[2]) for p, r in zip(parts, r1)]

    sh = [dict(w_in=w_in[l].astype(BF16), w_out=w_out[l].astype(BF16), w_up=w_up[l].astype(BF16),
               w_down=w_down[l].astype(BF16)) for l in range(DEPTH)]
    dw_flat = _pad_to(w_dw, (DEPTH, 32, 32)).reshape(16, 128)
    st_in0, st_dw = _allgather([sh[0]["w_in"], dw_flat], "allgather_first")
    dw_all = st_dw.reshape(N_DEV, DEPTH, 32, 32)[:, :, :KCONV, :]
    dw_all = jnp.transpose(dw_all, (1, 2, 0, 3)).reshape(DEPTH, KCONV, 256)
    wl = [_layer_small(l, dw_all[l], norm_mix, w_gla_gate, b_gla_gate, gla_norm, b_dw, conv_ln_g, conv_ln_b,
                       rel_bias, norm_ffn) for l in range(DEPTH)]

    s0, s1 = sh[0], sh[1]
    half = s0["w_down"].shape[0] // 2
    down0_a, down0_b = s0["w_down"][:half], s0["w_down"][half:]
    sv0, g0 = _mixers_fwd(x0, wl[0], _w_in_padded(st_in0), plan=dict(
        inproj=lambda r: [_ag_spread([s0["w_out"], down0_a])],
        att=lambda r: [_ag_spread([s0["w_up"]]), _ag_pass(r["inproj"])],
        gla=lambda r: [_ag_spread([down0_b]), _ag_pass(r["att"][:1])],
        conv=lambda r: [_ag_pass(r["gla"][:1])]))
    st_out0, st_down0_a = g0["att"][1:]
    st_up0, st_down0_b = g0["gla"][1], g0["conv"][0]
    st_down0 = jnp.concatenate([st_down0_a, st_down0_b], axis=1)
    wo0 = st_out0.reshape(D, D)
    (h1_0, xn2t_0, h2_0, act_0), (st_in1, out1_half) = _outproj_mlp_fwd(
        x0, sv0["o_gla"], sv0["o_conv"], sv0["o_att"], wo0, wl[0]["norm_ffn"], st_up0, st_down0,
        jobs=[_ag_both([s1["w_in"]]), _ag_spread([s1["w_out"]])])

    sv1, g1 = _mixers_fwd(h2_0, wl[1], _w_in_padded(st_in1), plan=dict(
        inproj=lambda r: [_ag_spread([s1["w_up"]]), _ag_pass([out1_half])],
        att=lambda r: [_ag_spread([s1["w_down"]]), _ag_pass(r["inproj"][:1])],
        gla=lambda r: [_ag_pass(r["att"][:1])]))
    st_out1, st_up1, st_down1 = g1["inproj"][1], g1["att"][1], g1["gla"][0]
    wo1 = st_out1.reshape(D, D)
    (h1_1, xn2t_1, h2_1, act_1), _ = _outproj_mlp_fwd(
        h2_0, sv1["o_gla"], sv1["o_conv"], sv1["o_att"], wo1, wl[1]["norm_ffn"], st_up1, st_down1)

    loss8, dh, dhb, d_nf = _loss_fwd_bwd(h2_1, norm_final[None, :], target)
    loss = lax.psum(loss8[0, 0], ("x", "y", "c"))

    def layer_bwd(dh_pair, sv, wl_l, xn2t, act, h1, wo, st_up, st_down, mlp_jobs, x_jobs_fn):
        (dh1, dw_up, dw_down, d_nffn), mlp_res = _mlp_bwd(xn2t, act, h1, dh_pair[0], dh_pair[1], wl_l["norm_ffn"],
                                                           st_up, st_down, jobs=mlp_jobs)
        ud = [dw_up, dw_down]
        (d_ogla, d_oconv, d_oatt, dw_out), r1 = _outproj_bwd(
            dh1, sv["o_gla"], sv["o_conv"], sv["o_att"], wo, jobs=[_rs_swap(ud)])
        pair_ud = pair_sums(ud, r1)
        att_grads, r = _att_bwd(sv["aq"], sv["ak"], sv["av"], sv["bias"], sv["o_att"], sv["lse"], d_oatt,
                                jobs=[_rs_ici(pair_ud), _rs_swap([dw_out])])
        r2_ud, r1_out = r[:2], r[2:]
        pair_out = pair_sums([dw_out], r1_out)
        dh_in, _, small, r2_out, x_res = _mixers_bwd(sv, wl_l, dh1, d_ogla, d_oconv, att_grads,
                                                     conv_jobs=[_rs_ici(pair_out)], x_jobs_fn=x_jobs_fn)
        small["norm_ffn"] = d_nffn
        sums = dict(w_out=(pair_out[0], r2_out[0]), w_up=(pair_ud[0], r2_ud[0]), w_down=(pair_ud[1], r2_ud[1]))
        return dh_in, small, sums, mlp_res, x_res

    stash = {}

    def swap_w_in(dw_in):
        stash["in1"] = [_dw_in_shards(dw_in)]
        return [_rs_swap(stash["in1"])]

    dh_pair, small1, sums1, _, r1_in1 = layer_bwd((dh, dhb), sv1, wl[1], xn2t_1, act_1, h1_1, wo1, st_up1, st_down1,
                                                  (), swap_w_in)
    pair_in1 = pair_sums(stash["in1"], r1_in1)

    def send_w_in(dw_in):
        in0 = [_dw_in_shards(dw_in)]
        stash["pair_in0"] = pair_sums(in0, _comm_call([_rs_swap(in0)], "rs_swap_w_in_0"))
        return [_rs_ici(stash["pair_in0"])]

    (dx, _), small0, sums0, r2_in1, r2_in0 = layer_bwd(dh_pair, sv0, wl[0], xn2t_0, act_0, h1_0, wo0, st_up0, st_down0,
                                                       [_rs_ici(pair_in1)], send_w_in)
    sums1["w_in"] = (pair_in1[0], r2_in1[0])
    sums0["w_in"] = (stash["pair_in0"][0], r2_in0[0])

    big_w = dict(w_in=(w_in, m_w_in, v_w_in), w_out=(w_out, m_w_out, v_w_out), w_up=(w_up, m_w_up, v_w_up),
                 w_down=(w_down, m_w_down, v_w_down))
    pairs = {1: sums1, 0: sums0}
    big_out = {}
    for name, (w_, m_, v_) in big_w.items():
        res = None
        for l in (1, 0):
            a_own, r2_ = pairs[l][name]
            res = _rs_adamw(a_own, r2_, w_, m_, v_, l, chip_idx, RS_ROWS[name], prev=res)
        big_out[name] = res

    grads = (small0, small1)
    parts = dict(
        norm_mix=jnp.concatenate([grads[l]["norm_mix"] for l in range(DEPTH)], axis=0),
        norm_ffn=jnp.concatenate([grads[l]["norm_ffn"] for l in range(DEPTH)], axis=0),
        norm_final=d_nf,
        gla_norm=jnp.concatenate([grads[l]["gn"] for l in range(DEPTH)], axis=0),
        b_gla_gate=jnp.concatenate([grads[l]["bg"][:, :192] for l in range(DEPTH)], axis=0),
        b_dw=jnp.concatenate([grads[l]["b_dw"] for l in range(DEPTH)], axis=0),
        conv_ln_g=jnp.concatenate([grads[l]["ln_g"] for l in range(DEPTH)], axis=0),
        conv_ln_b=jnp.concatenate([grads[l]["ln_b"] for l in range(DEPTH)], axis=0),
        rel_bias=jnp.concatenate([grads[l]["rb"][:6, :N_REL] for l in range(DEPTH)], axis=0),
        w_gla_gate=jnp.concatenate([grads[l]["wg"][:16, :192] for l in range(DEPTH)], axis=0),
        w_dw=jnp.concatenate([grads[l]["w_dw"][:KCONV] for l in range(DEPTH)], axis=0),
    )
    small_all = _allgather([_pack_small(parts)], "allgather_small")[0]
    sg = _unpack_small(_small_sum(small_all))
    dw_grad = lax.dynamic_slice_in_dim(sg["w_dw"].reshape(DEPTH, KCONV, 256), me * 32, 32, axis=2)
    small_g = dict(
        norm_mix=sg["norm_mix"], w_gla_gate=sg["w_gla_gate"].reshape(DEPTH, 16, 192), b_gla_gate=sg["b_gla_gate"],
        gla_norm=sg["gla_norm"], w_dw=dw_grad, b_dw=sg["b_dw"], conv_ln_g=sg["conv_ln_g"],
        conv_ln_b=sg["conv_ln_b"], rel_bias=sg["rel_bias"].reshape(DEPTH, 6, N_REL), norm_ffn=sg["norm_ffn"],
        norm_final=sg["norm_final"].reshape(D))
    small_names = ("norm_mix", "w_gla_gate", "b_gla_gate", "gla_norm", "w_dw", "b_dw", "conv_ln_g", "conv_ln_b",
                   "rel_bias", "norm_ffn", "norm_final")
    small_w = dict(norm_mix=norm_mix, w_gla_gate=w_gla_gate, b_gla_gate=b_gla_gate, gla_norm=gla_norm, w_dw=w_dw,
                   b_dw=b_dw, conv_ln_g=conv_ln_g, conv_ln_b=conv_ln_b, rel_bias=rel_bias, norm_ffn=norm_ffn,
                   norm_final=norm_final)
    small_m = dict(norm_mix=m_norm_mix, w_gla_gate=m_w_gla_gate, b_gla_gate=m_b_gla_gate, gla_norm=m_gla_norm,
                   w_dw=m_w_dw, b_dw=m_b_dw, conv_ln_g=m_conv_ln_g, conv_ln_b=m_conv_ln_b, rel_bias=m_rel_bias,
                   norm_ffn=m_norm_ffn, norm_final=m_norm_final)
    small_v = dict(norm_mix=v_norm_mix, w_gla_gate=v_w_gla_gate, b_gla_gate=v_b_gla_gate, gla_norm=v_gla_norm,
                   w_dw=v_w_dw, b_dw=v_b_dw, conv_ln_g=v_conv_ln_g, conv_ln_b=v_conv_ln_b, rel_bias=v_rel_bias,
                   norm_ffn=v_norm_ffn, norm_final=v_norm_final)
    s_delta, s_m, s_v = _adamw_small([small_w[n] for n in small_names], [small_g[n] for n in small_names],
                                     [small_m[n] for n in small_names], [small_v[n] for n in small_names])
    s_idx = {n: t for t, n in enumerate(small_names)}

    order = ("norm_mix", "w_in", "w_gla_gate", "b_gla_gate", "gla_norm", "w_dw", "b_dw", "conv_ln_g", "conv_ln_b",
             "rel_bias", "w_out", "norm_ffn", "w_up", "w_down", "norm_final")

    def pick(kind, name):
        if name in big_out:
            return big_out[name][kind]
        t = s_idx[name]
        return (small_g[name], s_delta[t], s_m[t], s_v[t])[kind]

    outs = [loss, dx[None]]
    for kind in range(4):
        outs += [pick(kind, n) for n in order]
    return tuple(outs)
```

```python
import functools

import jax
import jax.numpy as jnp
from jax import lax
from jax.experimental import pallas as pl
from jax.experimental.pallas import tpu as pltpu

F32 = jnp.float32
BF16 = jnp.bfloat16
MESH = pl.DeviceIdType.MESH

D = 1024
DEPTH = 2
CH = 64
EPS = 1e-6
NEG = -1e30
N_DEV = 8
N_REL = 257
Q_SCALE = 48.0 ** -0.5
A_SCALE = 64.0 ** -0.5
GATE_TAU = 16.0
KCONV = 31

OQ, OKK, OV, OG, OCU, OAQ, OAK, OAV, OLR, DINP = 0, 256, 512, 896, 1280, 1792, 2176, 2560, 2944, 3072
IN_GROUPS = ((OQ, 256), (OKK, 256), (OV, 384), (OG, 384), (OCU, 512), (OAQ, 384), (OAK, 384), (OAV, 384), (OLR, 128))

AQ_BLK = 256
AK_WIN = 768
WIN_LEFT = 2
RB_W = 1536

ADAM_LR, ADAM_B1, ADAM_B2, ADAM_EPS, ADAM_WD, ADAM_STEP = 0.001, 0.9, 0.999, 1e-08, 0.01, 10


V7X_VMEM_MIB = 64
VMEM_LIMIT_MIB = V7X_VMEM_MIB - 1


def _cp(sem=None):
    kw = {"vmem_limit_bytes": VMEM_LIMIT_MIB * 1024 * 1024}
    if sem is not None:
        kw["dimension_semantics"] = sem
    return pltpu.CompilerParams(**kw)


def _dot(a, b):
    return jnp.dot(a, b, preferred_element_type=F32)


def _dot_nt(a, b):
    return lax.dot_general(a, b, (((1,), (1,)), ((), ())), preferred_element_type=F32)


def _dot_tn(a, b):
    return lax.dot_general(a, b, (((0,), (0,)), ((), ())), preferred_element_type=F32)


def _split2(a):
    hi = a.astype(BF16)
    lo = (a - hi.astype(F32)).astype(BF16)
    return hi, lo


def _split3(a):
    hi = a.astype(BF16)
    r1 = a - hi.astype(F32)
    mid = r1.astype(BF16)
    lo = (r1 - mid.astype(F32)).astype(BF16)
    return hi, mid, lo


def _sigmoid(x):
    return 1.0 / (1.0 + jnp.exp(-x))


def _group(idx, size, n):
    g = jnp.zeros_like(idx)
    for t in range(1, n):
        g = g + (idx >= t * size).astype(jnp.int32)
    return g


def _rms_bwd(dy, x, r, gamma):
    xh = x * r
    dxh = dy * gamma
    dx = r * (dxh - xh * jnp.mean(dxh * xh, axis=-1, keepdims=True))
    return dx, jnp.sum(dy * xh, axis=0, keepdims=True)


def _row_spec(tm, n):
    return pl.BlockSpec((tm, n), lambda i: (i, 0))


def _full_spec(shape):
    nd = len(shape)
    return pl.BlockSpec(shape, lambda *_: (0,) * nd)


def _any_spec():
    return pl.BlockSpec(memory_space=pl.ANY)


class _Job:
    def __init__(self, operands, out_shapes, sems, start, finish, aliases=None):
        self.operands, self.out_shapes, self.sems = list(operands), list(out_shapes), list(sems)
        self.start, self.finish, self.aliases = start, finish, dict(aliases or {})


def _pcall(body, *, name, grid, in_specs, out_specs, out_shape, operands, scratch_shapes=(), sem=None, jobs=()):
    jobs = list(jobs)
    in_specs, out_specs, out_shape = list(in_specs), list(out_specs), list(out_shape)
    scratch_shapes = list(scratch_shapes)
    n_in, n_out, n_scr = len(in_specs), len(out_specs), len(scratch_shapes)
    j_in = [a for j in jobs for a in j.operands]
    j_out = [s for j in jobs for s in j.out_shapes]
    j_sem = [s for j in jobs for s in j.sems]
    aliases, io, oo = {}, n_in, n_out
    for j in jobs:
        for a, b in j.aliases.items():
            aliases[io + a] = oo + b
        io += len(j.operands)
        oo += len(j.out_shapes)

    def wrapped(*refs):
        own_in, ji = refs[:n_in], refs[n_in:n_in + len(j_in)]
        o0 = n_in + len(j_in)
        own_out, jo = refs[o0:o0 + n_out], refs[o0 + n_out:o0 + n_out + len(j_out)]
        s0 = o0 + n_out + len(j_out)
        own_scr, js = refs[s0:s0 + n_scr], refs[s0 + n_scr:]

        def each_job(fn_name):
            a = b = c = 0
            for j in jobs:
                na, nb, nc = len(j.operands), len(j.out_shapes), len(j.sems)
                getattr(j, fn_name)(ji[a:a + na], jo[b:b + nb], js[c:c + nc])
                a, b, c = a + na, b + nb, c + nc

        if jobs and grid:
            pids = [pl.program_id(d) for d in range(len(grid))]
            first = functools.reduce(jnp.logical_and, [p == 0 for p in pids])
            last = functools.reduce(jnp.logical_and, [p == g - 1 for p, g in zip(pids, grid)])
            pl.when(first)(lambda: each_job("start"))
        elif jobs:
            each_job("start")

        body(*own_in, *own_out, *own_scr)

        if jobs and grid:
            pl.when(last)(lambda: each_job("finish"))
        elif jobs:
            each_job("finish")

    res = pl.pallas_call(
        wrapped, name=name, grid=grid,
        in_specs=in_specs + [_any_spec()] * len(j_in), out_specs=out_specs + [_any_spec()] * len(j_out),
        out_shape=out_shape + j_out, scratch_shapes=scratch_shapes + j_sem,
        input_output_aliases=aliases, compiler_params=_cp(sem),
    )(*operands, *j_in)
    return res[:n_out], res[n_out:]


ATT_HEADS = 6
HEAD_PAD = 128
ATT_WIDE = ATT_HEADS * HEAD_PAD
ATT_GROUP_OFFS = (OAQ, OAK, OAV)


def _store_head_padded(o_ref, part):
    o_ref[...] = jnp.zeros_like(o_ref)
    for hd in range(ATT_HEADS):
        o_ref[:, hd * HEAD_PAD:hd * HEAD_PAD + 64] = part[:, hd * 64:(hd + 1) * 64]


def _inproj_fwd(h, gamma, w, jobs=()):
    T = h.shape[0]
    tm = 512

    def body(h_ref, g_ref, w_ref, *outs):
        x = h_ref[...]
        r = lax.rsqrt(jnp.mean(x * x, axis=-1, keepdims=True) + EPS)
        xn = (x * r * g_ref[...]).astype(BF16)
        p = _dot(xn, w_ref[...])
        for o_ref, (off, n) in zip(outs, IN_GROUPS):
            part = p[:, off:off + n].astype(BF16)
            if off in ATT_GROUP_OFFS:
                _store_head_padded(o_ref, part)
            else:
                o_ref[...] = part

    widths = [ATT_WIDE if off in ATT_GROUP_OFFS else n for off, n in IN_GROUPS]
    return _pcall(
        body, name="inproj_fwd", grid=(T // tm,),
        in_specs=[_row_spec(tm, D), _full_spec((1, D)), _full_spec((D, DINP))],
        out_specs=[_row_spec(tm, n) for n in widths],
        out_shape=[jax.ShapeDtypeStruct((T, n), BF16) for n in widths],
        sem=("arbitrary",), operands=(h, gamma, w), jobs=jobs)


def _inproj_norm(h_ref, g_ref):
    x = h_ref[...]
    r = lax.rsqrt(jnp.mean(x * x, axis=-1, keepdims=True) + EPS)
    return x, r, g_ref[...]


def _inproj_bwd_w(h, gamma, dparts):
    T = h.shape[0]
    tm = 512
    nt = T // tm

    def body(h_ref, g_ref, *rest):
        dp_refs = rest[:9]
        dw_ref, acc = rest[9:]
        i = pl.program_id(0)

        @pl.when(i == 0)
        def _():
            acc[...] = jnp.zeros_like(acc)

        x, r, gamma_ = _inproj_norm(h_ref, g_ref)
        xnt = jnp.transpose((x * r * gamma_).astype(BF16))
        for d_ref, (off, n) in zip(dp_refs, IN_GROUPS):
            acc[:, off:off + n] += _dot(xnt, d_ref[...])

        @pl.when(i == nt - 1)
        def _():
            dw_ref[...] = acc[...].astype(BF16)

    return pl.pallas_call(
        body, name="inproj_bwd_w", grid=(nt,),
        in_specs=[_row_spec(tm, D), _full_spec((1, D))] + [_row_spec(tm, n) for _, n in IN_GROUPS],
        out_specs=_full_spec((D, DINP)),
        out_shape=jax.ShapeDtypeStruct((D, DINP), BF16),
        scratch_shapes=[pltpu.VMEM((D, DINP), F32)],
        compiler_params=_cp(("arbitrary",)),
    )(h, gamma, *dparts)


def _inproj_bwd_x(h, dh_in, gamma, w, dparts, jobs=()):
    T = h.shape[0]
    tm = 512

    def body(h_ref, dhin_ref, g_ref, w_ref, *rest):
        dp_refs = rest[:9]
        dh_ref, dhb_ref, dg_ref = rest[9:]

        @pl.when(pl.program_id(0) == 0)
        def _():
            dg_ref[...] = jnp.zeros_like(dg_ref)

        x, r, gamma_ = _inproj_norm(h_ref, g_ref)
        dxn = None
        for d_ref, (off, n) in zip(dp_refs, IN_GROUPS):
            part = _dot_nt(d_ref[...], w_ref[:, off:off + n])
            dxn = part if dxn is None else dxn + part
        dx, dgam = _rms_bwd(dxn, x, r, gamma_)
        dh = dhin_ref[...] + dx
        dh_ref[...] = dh
        dhb_ref[...] = dh.astype(BF16)
        dg_ref[...] += dgam

    return _pcall(
        body, name="inproj_bwd_x", grid=(T // tm,),
        in_specs=[_row_spec(tm, D), _row_spec(tm, D), _full_spec((1, D)), _full_spec((D, DINP))]
        + [_row_spec(tm, n) for _, n in IN_GROUPS],
        out_specs=[_row_spec(tm, D), _row_spec(tm, D), _full_spec((1, D))],
        out_shape=[jax.ShapeDtypeStruct((T, D), F32), jax.ShapeDtypeStruct((T, D), BF16),
                   jax.ShapeDtypeStruct((1, D), F32)],
        sem=("arbitrary",), operands=(h, dh_in, gamma, w, *dparts), jobs=jobs)


GLA_ROWS = 512
GLA_NC = GLA_ROWS // CH


def _gla_consts():
    ri = lax.broadcasted_iota(jnp.int32, (CH, CH), 0)
    ci = lax.broadcasted_iota(jnp.int32, (CH, CH), 1)
    upper = (ci > ri).astype(BF16)
    vv = lax.broadcasted_iota(jnp.int32, (384, 256), 0)
    kk = lax.broadcasted_iota(jnp.int32, (384, 256), 1)
    mask_t = ((_group(vv, 96, 4) == _group(kk, 48, 4)) & (kk < 192)).astype(F32)
    pi = lax.broadcasted_iota(jnp.int32, (384, 384), 0)
    pj = lax.broadcasted_iota(jnp.int32, (384, 384), 1)
    same_head = (_group(pi, 96, 4) == _group(pj, 96, 4)).astype(BF16)
    return upper, mask_t, same_head


def _gla_gate(lr_ref, wg_ref, bg_ref):
    z = _dot(lr_ref[...], wg_ref[...]) + bg_ref[...]
    la = (jnp.minimum(z, 0.0) - jnp.log(1.0 + jnp.exp(-jnp.abs(z)))) * (1.0 / GATE_TAU)
    return z, la


def _gla_chunk_decay(la_c, upper):
    hi, lo = _split2(la_c)
    dec = _dot(upper, hi) + _dot(upper, lo)
    end = jnp.sum(la_c, axis=0, keepdims=True)
    return jnp.exp(dec), jnp.exp(end)


def _head_mean(x, same_head):
    hi, lo = _split2(x)
    return (_dot(hi, same_head) + _dot(lo, same_head)) * (1.0 / 96.0)


def _gla_fwd(q, k, v, g, lr, wg, bg, gn, jobs=()):
    T = q.shape[0]
    nb = T // GLA_ROWS

    def body(q_ref, k_ref, v_ref, g_ref, lr_ref, wg_ref, bg_ref, gn_ref, y_ref, st_ref, s_scr, o_scr, kv_scr):
        upper, mask_t, same_head = _gla_consts()

        @pl.when(pl.program_id(0) == 0)
        def _():
            s_scr[...] = jnp.zeros_like(s_scr)

        _, la = _gla_gate(lr_ref, wg_ref, bg_ref)
        decays = []
        for c in range(GLA_NC):
            rs = slice(c * CH, (c + 1) * CH)
            w, a = _gla_chunk_decay(la[rs], upper)
            decays.append(a)
            kd = (k_ref[rs, :].astype(F32) * w).astype(BF16)
            kv_scr[c] = _dot_tn(v_ref[rs, :], kd) * mask_t
        for c in range(GLA_NC):
            s_new = s_scr[...] * decays[c] + kv_scr[c]
            s_scr[...] = s_new
            st_ref[c] = s_new.astype(BF16)
        for c in range(GLA_NC):
            rs = slice(c * CH, (c + 1) * CH)
            qs = (q_ref[rs, :].astype(F32) * Q_SCALE).astype(BF16)
            o_scr[rs, :] = _dot_nt(qs, st_ref[c])
        o = o_scr[...]
        r = lax.rsqrt(_head_mean(o * o, same_head) + EPS)
        gf = g_ref[...].astype(F32)
        y_ref[...] = (o * r * gn_ref[...] * (gf * _sigmoid(gf))).astype(BF16)

    return _pcall(
        body, name="gla_fwd", grid=(nb,),
        in_specs=[_row_spec(GLA_ROWS, 256), _row_spec(GLA_ROWS, 256), _row_spec(GLA_ROWS, 384),
                  _row_spec(GLA_ROWS, 384), _row_spec(GLA_ROWS, 128),
                  _full_spec((128, 256)), _full_spec((1, 256)), _full_spec((1, 384))],
        out_specs=[_row_spec(GLA_ROWS, 384), pl.BlockSpec((GLA_NC, 384, 256), lambda i: (i, 0, 0))],
        out_shape=[jax.ShapeDtypeStruct((T, 384), BF16), jax.ShapeDtypeStruct((T // CH, 384, 256), BF16)],
        scratch_shapes=[pltpu.VMEM((384, 256), F32), pltpu.VMEM((GLA_ROWS, 384), F32),
                        pltpu.VMEM((GLA_NC, 384, 256), F32)],
        sem=("arbitrary",), operands=(q, k, v, g, lr, wg, bg, gn), jobs=jobs)


def _gla_bwd(q, k, v, g, lr, states, dy, wg, bg, gn):
    T = q.shape[0]
    nb = T // GLA_ROWS

    def rev(s):
        return nb - 1 - s

    def body(q_ref, k_ref, v_ref, g_ref, lr_ref, st_ref, stp_ref, dy_ref, wg_ref, bg_ref, gn_ref,
             dq_ref, dk_ref, dv_ref, dg_ref, dlr_ref, dwg_ref, dbg_ref, dgn_ref,
             d_scr, an_scr, o_scr, do_scr, dla_scr, dst_scr):
        upper, mask_t, same_head = _gla_consts()
        s = pl.program_id(0)
        blk = rev(s)

        @pl.when(s == 0)
        def _():
            d_scr[...] = jnp.zeros_like(d_scr)
            an_scr[...] = jnp.zeros_like(an_scr)
            dwg_ref[...] = jnp.zeros_like(dwg_ref)
            dbg_ref[...] = jnp.zeros_like(dbg_ref)
            dgn_ref[...] = jnp.zeros_like(dgn_ref)

        z, la = _gla_gate(lr_ref, wg_ref, bg_ref)
        ws, as_, qss, kds = [], [], [], []
        for c in range(GLA_NC):
            rs = slice(c * CH, (c + 1) * CH)
            w, a = _gla_chunk_decay(la[rs], upper)
            ws.append(w)
            as_.append(a)
            qs = (q_ref[rs, :].astype(F32) * Q_SCALE).astype(BF16)
            qss.append(qs)
            kds.append((k_ref[rs, :].astype(F32) * w).astype(BF16))
            o_scr[rs, :] = _dot_nt(qs, st_ref[c])
        o = o_scr[...]
        r = lax.rsqrt(_head_mean(o * o, same_head) + EPS)
        on = o * r
        gf = g_ref[...].astype(F32)
        sg = _sigmoid(gf)
        si = gf * sg
        dyf = dy_ref[...].astype(F32)
        gn_ = gn_ref[...]
        dgn_ref[...] += jnp.sum(dyf * si * on, axis=0, keepdims=True)
        dg_ref[...] = (dyf * on * gn_ * (sg * (1.0 + gf * (1.0 - sg)))).astype(BF16)
        d_on = dyf * si * gn_
        do_scr[...] = r * (d_on - on * _head_mean(d_on * on, same_head))

        for c in range(GLA_NC):
            rs = slice(c * CH, (c + 1) * CH)
            dst_scr[c] = _dot_tn(do_scr[rs, :].astype(BF16), qss[c]) * mask_t
        for c in reversed(range(GLA_NC)):
            dt = d_scr[...] * an_scr[...] + dst_scr[c]
            d_scr[...] = dt
            dst_scr[c] = dt
            an_scr[...] = as_[c]
        first = (blk > 0).astype(F32)
        for c in range(GLA_NC):
            rs = slice(c * CH, (c + 1) * CH)
            dob = do_scr[rs, :].astype(BF16)
            if c > 0:
                s_prev = st_ref[c - 1].astype(F32)
            else:
                s_prev = stp_ref[0].astype(F32) * first
            dq_ref[rs, :] = (_dot(dob, st_ref[c]) * Q_SCALE).astype(BF16)
            dt = dst_scr[c]
            da = jnp.sum(dt * s_prev, axis=0, keepdims=True)
            db = dt.astype(BF16)
            dkd = _dot(v_ref[rs, :], db)
            dv_ref[rs, :] = _dot_nt(kds[c], db).astype(BF16)
            dk_ref[rs, :] = (dkd * ws[c]).astype(BF16)
            ddec = dkd * k_ref[rs, :].astype(F32) * ws[c]
            hi, lo = _split2(ddec)
            dla_scr[rs, :] = _dot_tn(upper, hi) + _dot_tn(upper, lo) + as_[c] * da

        dz = dla_scr[...] * (1.0 - _sigmoid(z)) * (1.0 / GATE_TAU)
        dzb = dz.astype(BF16)
        dlr_ref[...] = _dot_nt(dzb, wg_ref[...]).astype(BF16)
        dwg_ref[...] += _dot_tn(lr_ref[...], dzb)
        dbg_ref[...] += jnp.sum(dz, axis=0, keepdims=True)

    def rspec(n):
        return pl.BlockSpec((GLA_ROWS, n), lambda s: (rev(s), 0))

    return pl.pallas_call(
        body, name="gla_bwd", grid=(nb,),
        in_specs=[rspec(256), rspec(256), rspec(384), rspec(384), rspec(128),
                  pl.BlockSpec((GLA_NC, 384, 256), lambda s: (rev(s), 0, 0)),
                  pl.BlockSpec((1, 384, 256), lambda s: (jnp.maximum(rev(s) * GLA_NC - 1, 0), 0, 0)),
                  rspec(384), _full_spec((128, 256)), _full_spec((1, 256)), _full_spec((1, 384))],
        out_specs=[rspec(256), rspec(256), rspec(384), rspec(384), rspec(128),
                   _full_spec((128, 256)), _full_spec((1, 256)), _full_spec((1, 384))],
        out_shape=[jax.ShapeDtypeStruct((T, 256), BF16), jax.ShapeDtypeStruct((T, 256), BF16),
                   jax.ShapeDtypeStruct((T, 384), BF16), jax.ShapeDtypeStruct((T, 384), BF16),
                   jax.ShapeDtypeStruct((T, 128), BF16),
                   jax.ShapeDtypeStruct((128, 256), F32), jax.ShapeDtypeStruct((1, 256), F32),
                   jax.ShapeDtypeStruct((1, 384), F32)],
        scratch_shapes=[pltpu.VMEM((384, 256), F32), pltpu.VMEM((1, 256), F32),
                        pltpu.VMEM((GLA_ROWS, 384), F32), pltpu.VMEM((GLA_ROWS, 384), F32),
                        pltpu.VMEM((GLA_ROWS, 256), F32), pltpu.VMEM((GLA_NC, 384, 256), F32)],
        compiler_params=_cp(("arbitrary",)),
    )(q, k, v, g, lr, states, states, dy, wg, bg, gn)


CONV_ROWS = 512
HALO = 32
SUBL = 8
CONV_SLAB = 32
PHASE_ROWS = CONV_ROWS + HALO - SUBL
FWD_SHIFT = tuple(HALO - (KCONV - 1) + j for j in range(KCONV))
BWD_SHIFT = tuple(KCONV - 1 - j for j in range(KCONV))


def _fill_phases(buf, ph):
    for f in range(1, SUBL):
        ph[f, 0:PHASE_ROWS, :] = buf[pl.ds(f, PHASE_ROWS), :]


def _tap(buf, ph, shift, r, n):
    f, base = shift % SUBL, shift - shift % SUBL
    src = buf if f == 0 else ph.at[f]
    return src[pl.ds(base + r, n), :]


def _taps_apply(w_ref, buf, ph, shifts, out):
    for r in range(0, CONV_ROWS, CONV_SLAB):
        acc = jnp.zeros((CONV_SLAB, 256), F32)
        for j in range(KCONV):
            acc = acc + w_ref[j:j + 1, :] * _tap(buf, ph, shifts[j], r, CONV_SLAB)
        out[r:r + CONV_SLAB, :] = acc


def _conv_scratch():
    return [pltpu.VMEM((CONV_ROWS + HALO, 256), F32), pltpu.VMEM((SUBL, CONV_ROWS + HALO, 256), F32),
            pltpu.VMEM((CONV_ROWS, 256), F32)]


def _conv_common(cu_ref, halo_ref, w_ref, b_ref, lg_ref, lb_ref, buf, ph, cbuf, blk, conv_ref=None):
    u = cu_ref[...].astype(F32)
    a = u[:, :256]
    sb = _sigmoid(u[:, 256:])
    uh = halo_ref[...].astype(F32)
    hh = uh[:, :256] * _sigmoid(uh[:, 256:]) * (blk > 0).astype(F32)
    buf[0:HALO, :] = hh
    buf[HALO:HALO + CONV_ROWS, :] = a * sb
    _fill_phases(buf, ph)
    if conv_ref is None:
        _taps_apply(w_ref, buf, ph, FWD_SHIFT, cbuf)
        conv = cbuf[...]
    else:
        conv = conv_ref[...]
    cc = conv + b_ref[...]
    mu = jnp.mean(cc, axis=-1, keepdims=True)
    xc = cc - mu
    rstd = lax.rsqrt(jnp.mean(xc * xc, axis=-1, keepdims=True) + EPS)
    n = xc * rstd
    yln = n * lg_ref[...] + lb_ref[...]
    return a, sb, n, rstd, yln, conv


def _conv_fwd(cu, w, b, lg, lb, jobs=()):
    T = cu.shape[0]
    nb = T // CONV_ROWS
    per = CONV_ROWS // HALO

    def body(cu_ref, halo_ref, w_ref, b_ref, lg_ref, lb_ref, y_ref, conv_ref, buf, ph, cbuf):
        _, _, _, _, yln, conv = _conv_common(cu_ref, halo_ref, w_ref, b_ref, lg_ref, lb_ref, buf, ph, cbuf,
                                             pl.program_id(0))
        y_ref[...] = (yln * _sigmoid(yln)).astype(BF16)
        conv_ref[...] = conv

    return _pcall(
        body, name="conv_fwd", grid=(nb,),
        in_specs=[_row_spec(CONV_ROWS, 512),
                  pl.BlockSpec((HALO, 512), lambda i: (jnp.maximum(i * per - 1, 0), 0)),
                  _full_spec((32, 256)), _full_spec((1, 256)), _full_spec((1, 256)), _full_spec((1, 256))],
        out_specs=[_row_spec(CONV_ROWS, 256), _row_spec(CONV_ROWS, 256)],
        out_shape=[jax.ShapeDtypeStruct((T, 256), BF16), jax.ShapeDtypeStruct((T, 256), F32)],
        scratch_shapes=_conv_scratch(),
        sem=("arbitrary",), operands=(cu, cu, w, b, lg, lb), jobs=jobs)


def _conv_bwd(cu, conv, dy, w, b, lg, lb, jobs=()):
    T = cu.shape[0]
    nb = T // CONV_ROWS
    per = CONV_ROWS // HALO

    def rev(s):
        return nb - 1 - s

    def body(cu_ref, halo_ref, conv_ref, dy_ref, w_ref, b_ref, lg_ref, lb_ref,
             dcu_ref, dw_ref, db_ref, dlg_ref, dlb_ref, buf, ph, cbuf, dcbuf, dph, carry):
        s = pl.program_id(0)

        @pl.when(s == 0)
        def _():
            carry[...] = jnp.zeros_like(carry)
            dw_ref[...] = jnp.zeros_like(dw_ref)
            db_ref[...] = jnp.zeros_like(db_ref)
            dlg_ref[...] = jnp.zeros_like(dlg_ref)
            dlb_ref[...] = jnp.zeros_like(dlb_ref)

        a, sb, n, rstd, yln, _ = _conv_common(cu_ref, halo_ref, w_ref, b_ref, lg_ref, lb_ref, buf, ph, cbuf, rev(s),
                                              conv_ref=conv_ref)
        sg = _sigmoid(yln)
        dyln = dy_ref[...].astype(F32) * (sg * (1.0 + yln * (1.0 - sg)))
        dlg_ref[...] += jnp.sum(dyln * n, axis=0, keepdims=True)
        dlb_ref[...] += jnp.sum(dyln, axis=0, keepdims=True)
        dn = dyln * lg_ref[...]
        dc = rstd * (dn - jnp.mean(dn, axis=-1, keepdims=True) - n * jnp.mean(dn * n, axis=-1, keepdims=True))
        db_ref[...] += jnp.sum(dc, axis=0, keepdims=True)
        dcbuf[0:CONV_ROWS, :] = dc
        dcbuf[CONV_ROWS:CONV_ROWS + HALO, :] = carry[...]
        carry[...] = dc[0:HALO, :]
        _fill_phases(dcbuf, dph)
        for j in range(KCONV):
            acc = jnp.zeros((SUBL, 256), F32)
            for r in range(0, CONV_ROWS, 2 * CONV_SLAB):
                prod = dcbuf[r:r + 2 * CONV_SLAB, :] * _tap(buf, ph, FWD_SHIFT[j], r, 2 * CONV_SLAB)
                acc = acc + jnp.sum(prod.reshape(2 * CONV_SLAB // SUBL, SUBL, 256), axis=0)
            dw_ref[j:j + 1, :] += jnp.sum(acc, axis=0, keepdims=True)
        _taps_apply(w_ref, dcbuf, dph, BWD_SHIFT, cbuf)
        dhg = cbuf[...]
        dcu_ref[...] = jnp.concatenate([dhg * sb, dhg * a * sb * (1.0 - sb)], axis=1).astype(BF16)

    def rspec(n):
        return pl.BlockSpec((CONV_ROWS, n), lambda s: (rev(s), 0))

    return _pcall(
        body, name="conv_bwd", grid=(nb,),
        in_specs=[rspec(512),
                  pl.BlockSpec((HALO, 512), lambda s: (jnp.maximum(rev(s) * per - 1, 0), 0)),
                  rspec(256), rspec(256),
                  _full_spec((32, 256)), _full_spec((1, 256)), _full_spec((1, 256)), _full_spec((1, 256))],
        out_specs=[rspec(512), _full_spec((32, 256)), _full_spec((1, 256)), _full_spec((1, 256)),
                   _full_spec((1, 256))],
        out_shape=[jax.ShapeDtypeStruct((T, 512), BF16), jax.ShapeDtypeStruct((32, 256), F32),
                   jax.ShapeDtypeStruct((1, 256), F32), jax.ShapeDtypeStruct((1, 256), F32),
                   jax.ShapeDtypeStruct((1, 256), F32)],
        scratch_shapes=_conv_scratch() + [pltpu.VMEM((CONV_ROWS + HALO, 256), F32),
                                          pltpu.VMEM((SUBL, CONV_ROWS + HALO, 256), F32),
                                          pltpu.VMEM((HALO, 256), F32)],
        sem=("arbitrary",), operands=(cu, cu, conv, dy, w, b, lg, lb), jobs=jobs)


def _rel_onehot_t(shift=0):
    r = lax.broadcasted_iota(jnp.int32, (384, RB_W), 0)
    n = lax.broadcasted_iota(jnp.int32, (384, RB_W), 1) - shift
    idx = jnp.clip(1024 - n, -128, 128) + 128
    return (idx == r).astype(BF16)


def _relbias_expand(rb):
    def body(rb_ref, out_ref):
        oh = _rel_onehot_t()
        hi, mid, lo = _split3(rb_ref[...])
        strip = _dot(hi, oh) + _dot(mid, oh) + _dot(lo, oh)
        qi = _group(lax.broadcasted_iota(jnp.int32, (AQ_BLK, AK_WIN), 0), CH, 4)
        kj = _group(lax.broadcasted_iota(jnp.int32, (AQ_BLK, AK_WIN), 1), CH, 12)
        valid = (kj >= qi) & (kj <= qi + 8)
        for hd in range(6):
            x = jnp.broadcast_to(strip[hd:hd + 1, :], (AQ_BLK, RB_W))
            xr = pltpu.roll(x, 0, 1, stride=1, stride_axis=0)
            out_ref[hd] = jnp.where(valid, xr[:, 512:512 + AK_WIN], NEG)

    return pl.pallas_call(
        body, name="relbias_expand",
        out_shape=jax.ShapeDtypeStruct((6, AQ_BLK, AK_WIN), F32),
        compiler_params=_cp(),
    )(rb)


def _relbias_grad(dbias):
    def body(db_ref, out_ref):
        oh = _rel_onehot_t(AQ_BLK - 1)
        ri = lax.broadcasted_iota(jnp.int32, (AQ_BLK, AQ_BLK), 0)
        ci = lax.broadcasted_iota(jnp.int32, (AQ_BLK, AQ_BLK), 1)
        flip = (ri + ci == AQ_BLK - 1).astype(BF16)
        rows = []
        for hd in range(6):
            hi, mid, lo = _split3(db_ref[hd])
            rev = _dot(flip, hi) + _dot(flip, mid) + _dot(flip, lo)
            x = jnp.concatenate([jnp.zeros((AQ_BLK, 512), F32), rev,
                                 jnp.zeros((AQ_BLK, RB_W - 512 - AK_WIN), F32)], axis=1)
            xr = pltpu.roll(x, 0, 1, stride=1, stride_axis=0)
            rows.append(jnp.sum(xr, axis=0, keepdims=True))
        rows.append(jnp.zeros((2, RB_W), F32))
        dstrip = jnp.concatenate(rows, axis=0)
        hi, mid, lo = _split3(dstrip)
        out_ref[...] = _dot_nt(hi, oh) + _dot_nt(mid, oh) + _dot_nt(lo, oh)

    return pl.pallas_call(
        body, name="relbias_grad",
        out_shape=jax.ShapeDtypeStruct((8, 384), F32),
        compiler_params=_cp(),
    )(dbias)


ATT_SLAB = 8


def _att_logits_slab(s_scr, b_ref, hd, rows, first_key):
    kvalid = lax.broadcasted_iota(jnp.int32, (ATT_SLAB, AK_WIN), 1) >= first_key
    return jnp.where(kvalid, s_scr[rows, :] + b_ref[hd, rows, :], NEG)


def _att_softmax_slab(s_scr, b_ref, hd, rows, first_key):
    s = _att_logits_slab(s_scr, b_ref, hd, rows, first_key)
    m = jnp.max(s, axis=-1, keepdims=True)
    p = jnp.exp(s - m)
    total = jnp.sum(p, axis=-1, keepdims=True)
    return p * (1.0 / total), m + jnp.log(total)


def _att_first_key(i):
    return (8 - 4 * i) * CH


def _slab_rows(t):
    return pl.ds(t * ATT_SLAB, ATT_SLAB)


def _head_lanes(hd):
    return slice(hd * 64, (hd + 1) * 64)


WIN_BLKS = AK_WIN // AQ_BLK


def _head_tile(hd):
    return slice(hd * HEAD_PAD, (hd + 1) * HEAD_PAD)


def _win_cols(d):
    return slice(d * AQ_BLK, (d + 1) * AQ_BLK)


def _win_block(i, d):
    return jnp.maximum(i + d - WIN_LEFT, 0)


def _win_specs():
    return [pl.BlockSpec((AQ_BLK, ATT_WIDE), lambda i, d=d: (_win_block(i, d), 0)) for d in range(WIN_BLKS)]


def _att_fwd(q, k, v, bias, jobs=()):
    T = q.shape[0]
    nb = T // AQ_BLK

    def body(q_ref, k0, k1, k2, v0, v1, v2, b_ref, o_ref, lse_ref, s_scr):
        k_refs, v_refs = (k0, k1, k2), (v0, v1, v2)
        first_key = _att_first_key(pl.program_id(0))
        lse_ref[...] = jnp.zeros_like(lse_ref)

        def scores(hd):
            q_h = q_ref[:, _head_tile(hd)] * A_SCALE
            for d in range(WIN_BLKS):
                s_scr[hd % 2, :, _win_cols(d)] = _dot_nt(q_h, k_refs[d][:, _head_tile(hd)])

        scores(0)
        for hd in range(ATT_HEADS):
            if hd + 1 < ATT_HEADS:
                scores(hd + 1)
            s_h = s_scr.at[hd % 2]
            for t in range(AQ_BLK // ATT_SLAB):
                rows = _slab_rows(t)
                s_h[rows, :], lse_ref[rows, hd:hd + 1] = _att_softmax_slab(s_h, b_ref, hd, rows, first_key)
            o_h = _dot(s_h[:, _win_cols(0)].astype(BF16), v_refs[0][:, _head_tile(hd)])
            for d in range(1, WIN_BLKS):
                o_h = o_h + _dot(s_h[:, _win_cols(d)].astype(BF16), v_refs[d][:, _head_tile(hd)])
            o_ref[:, _head_lanes(hd)] = o_h[:, :64].astype(BF16)

    return _pcall(
        body, name="att_fwd", grid=(nb,),
        in_specs=[_row_spec(AQ_BLK, ATT_WIDE)] + _win_specs() + _win_specs() + [_full_spec((6, AQ_BLK, AK_WIN))],
        out_specs=[_row_spec(AQ_BLK, 384), _row_spec(AQ_BLK, 128)],
        out_shape=[jax.ShapeDtypeStruct((T, 384), BF16), jax.ShapeDtypeStruct((T, 128), F32)],
        scratch_shapes=[pltpu.VMEM((2, AQ_BLK, AK_WIN), F32)],
        sem=("arbitrary",), operands=(q, k, k, k, v, v, v, bias), jobs=jobs)


def _att_bwd(q, k, v, bias, o, lse, do, jobs=()):
    T = q.shape[0]
    nb = T // AQ_BLK

    def body(q_ref, k0, k1, k2, v0, v1, v2, b_ref, o_ref, lse_ref, do_ref, dq_ref, dk_ref, dv_ref, db_ref,
             dk_acc, dv_acc, s_scr, dp_scr, delta_scr):
        k_refs, v_refs = (k0, k1, k2), (v0, v1, v2)
        i = pl.program_id(0)

        @pl.when(i == 0)
        def _():
            dk_acc[...] = jnp.zeros_like(dk_acc)
            dv_acc[...] = jnp.zeros_like(dv_acc)
            db_ref[...] = jnp.zeros_like(db_ref)

        first_key = _att_first_key(i)

        def scores(hd):
            q_h = q_ref[:, _head_tile(hd)] * A_SCALE
            do_h = do_ref[:, _head_tile(hd)]
            delta_scr[:, hd:hd + 1] = jnp.sum(do_h[:, :64].astype(F32) * o_ref[:, _head_lanes(hd)].astype(F32),
                                              axis=-1, keepdims=True)
            for d in range(WIN_BLKS):
                s_scr[hd % 2, :, _win_cols(d)] = _dot_nt(q_h, k_refs[d][:, _head_tile(hd)])
                dp_scr[hd % 2, :, _win_cols(d)] = _dot_nt(do_h, v_refs[d][:, _head_tile(hd)])

        scores(0)
        for hd in range(ATT_HEADS):
            if hd + 1 < ATT_HEADS:
                scores(hd + 1)
            s_h, dp_h = s_scr.at[hd % 2], dp_scr.at[hd % 2]
            for t in range(AQ_BLK // ATT_SLAB):
                rows = _slab_rows(t)
                p = jnp.exp(_att_logits_slab(s_h, b_ref, hd, rows, first_key) - lse_ref[rows, hd:hd + 1])
                ds = p * (dp_h[rows, :] - delta_scr[rows, hd:hd + 1])
                db_ref[hd, rows, :] += ds
                s_h[rows, :] = p
                dp_h[rows, :] = ds
            q_h = q_ref[:, _head_tile(hd)] * A_SCALE
            do_h = do_ref[:, _head_tile(hd)]
            ls = _head_lanes(hd)
            dq_h = jnp.zeros((AQ_BLK, HEAD_PAD), F32)
            for d in range(WIN_BLKS):
                pb = s_h[:, _win_cols(d)].astype(BF16)
                dsb = dp_h[:, _win_cols(d)].astype(BF16)
                rows = pl.ds(pl.multiple_of(_win_block(i, d) * AQ_BLK, AQ_BLK), AQ_BLK)
                dv_acc[rows, ls] += _dot_tn(pb, do_h)[:, :64]
                dk_acc[rows, ls] += _dot_tn(dsb, q_h)[:, :64]
                dq_h = dq_h + _dot(dsb, k_refs[d][:, _head_tile(hd)])
            dq_ref[:, ls] = (dq_h[:, :64] * A_SCALE).astype(BF16)

        @pl.when(i == nb - 1)
        def _():
            dk_ref[...] = dk_acc[...].astype(BF16)
            dv_ref[...] = dv_acc[...].astype(BF16)

    return _pcall(
        body, name="att_bwd", grid=(nb,),
        in_specs=[_row_spec(AQ_BLK, ATT_WIDE)] + _win_specs() + _win_specs()
        + [_full_spec((6, AQ_BLK, AK_WIN)), _row_spec(AQ_BLK, 384), _row_spec(AQ_BLK, 128),
           _row_spec(AQ_BLK, ATT_WIDE)],
        out_specs=[_row_spec(AQ_BLK, 384), _full_spec((T, 384)), _full_spec((T, 384)),
                   _full_spec((6, AQ_BLK, AK_WIN))],
        out_shape=[jax.ShapeDtypeStruct((T, 384), BF16), jax.ShapeDtypeStruct((T, 384), BF16),
                   jax.ShapeDtypeStruct((T, 384), BF16), jax.ShapeDtypeStruct((6, AQ_BLK, AK_WIN), F32)],
        scratch_shapes=[pltpu.VMEM((T, 384), F32), pltpu.VMEM((T, 384), F32),
                        pltpu.VMEM((2, AQ_BLK, AK_WIN), F32), pltpu.VMEM((2, AQ_BLK, AK_WIN), F32),
                        pltpu.VMEM((AQ_BLK, 128), F32)],
        sem=("arbitrary",), operands=(q, k, k, k, v, v, v, bias, o, lse, do), jobs=jobs)


FF_BLK = 512
N_FF = 4096 // FF_BLK
MLP_SHARDS = 2


def _outproj_mlp_fwd(h, o_gla, o_conv, o_att, w_out, gamma, w_up, w_down, jobs=()):
    T = h.shape[0]
    tm = 1024

    def body(h_ref, og_ref, oc_ref, oa_ref, wo_ref, g_ref, wu_ref, wd_ref, h1_ref, xt_ref, h2_ref, a_ref, acc, xn_ref):
        j = pl.program_id(1)

        @pl.when(j == 0)
        def _():
            wo = wo_ref[...]
            h1 = (h_ref[...] + _dot(og_ref[...], wo[0:384]) + _dot(oc_ref[...], wo[384:640])
                  + _dot(oa_ref[...], wo[640:1024]))
            h1_ref[...] = h1
            r = lax.rsqrt(jnp.mean(h1 * h1, axis=-1, keepdims=True) + EPS)
            xn = (h1 * r * g_ref[...]).astype(BF16)
            xn_ref[...] = xn
            xt_ref[...] = jnp.transpose(xn)
            acc[...] = h1

        xn_ = xn_ref[...]
        down = None
        for s in range(MLP_SHARDS):
            a = jnp.maximum(_dot(xn_, wu_ref[s]), 0.0)
            a_ref[:, s * FF_BLK:(s + 1) * FF_BLK] = a.astype(BF16)
            part = _dot((a * a).astype(BF16), wd_ref[s])
            down = part if down is None else down + part
        acc[...] += down

        @pl.when(j == N_FF // MLP_SHARDS - 1)
        def _():
            h2_ref[...] = acc[...]

    row = lambda n: pl.BlockSpec((tm, n), lambda i, j: (i, 0))
    return _pcall(
        body, name="outproj_mlp_fwd", grid=(T // tm, N_FF // MLP_SHARDS),
        in_specs=[row(D), row(384), row(256), row(384),
                  pl.BlockSpec((D, D), lambda i, j: (0, 0)), pl.BlockSpec((1, D), lambda i, j: (0, 0)),
                  pl.BlockSpec((MLP_SHARDS, D, FF_BLK), lambda i, j: (j, 0, 0)),
                  pl.BlockSpec((MLP_SHARDS, FF_BLK, D), lambda i, j: (j, 0, 0))],
        out_specs=[row(D), pl.BlockSpec((D, tm), lambda i, j: (0, i)), row(D),
                   pl.BlockSpec((tm, MLP_SHARDS * FF_BLK), lambda i, j: (i, j))],
        out_shape=[jax.ShapeDtypeStruct((T, D), F32), jax.ShapeDtypeStruct((D, T), BF16),
                   jax.ShapeDtypeStruct((T, D), F32), jax.ShapeDtypeStruct((T, N_FF * FF_BLK), BF16)],
        scratch_shapes=[pltpu.VMEM((tm, D), F32), pltpu.VMEM((tm, D), BF16)],
        sem=("arbitrary", "arbitrary"), operands=(h, o_gla, o_conv, o_att, w_out, gamma, w_up, w_down), jobs=jobs)


def _mlp_bwd(xn2t, act, h1, dh2, dh2b, gamma, w_up, w_down, jobs=()):
    T = act.shape[0]
    tm = 512
    nt = T // tm
    ns = MLP_SHARDS
    nj = N_FF // ns
    last = nj - 1

    def body(xt_ref, a_ref, h1_ref, dy_ref, dyb_ref, g_ref, wu_ref, wd_ref, dh1_ref, dwu_ref, dwd_ref, dg_ref,
             dxn_acc, acc_u, acc_d):
        j = pl.program_id(0)
        i = pl.program_id(1)
        xt = xt_ref[...]
        dyb = dyb_ref[...]
        rows = pl.ds(pl.multiple_of(i * tm, tm), tm)

        @pl.when(i == 0)
        def _():
            acc_u[...] = jnp.zeros_like(acc_u)
            acc_d[...] = jnp.zeros_like(acc_d)

        @pl.when(j == 0)
        def _():
            dxn_acc[rows, :] = jnp.zeros((tm, D), F32)

        dxn = None
        for s in range(ns):
            a = a_ref[:, s * FF_BLK:(s + 1) * FF_BLK].astype(F32)
            hh = (a * a).astype(BF16)
            du = (_dot_nt(dyb, wd_ref[s]) * (2.0 * a)).astype(BF16)
            acc_d[s] += _dot_tn(hh, dyb)
            acc_u[s] += _dot(xt, du)
            part = _dot_nt(du, wu_ref[s])
            dxn = part if dxn is None else dxn + part
        dxn_acc[rows, :] += dxn

        @pl.when(i == nt - 1)
        def _():
            for s in range(ns):
                dwu_ref[s, 0] = acc_u[s].astype(BF16)
                dwd_ref[s, 0] = acc_d[s].astype(BF16)

        @pl.when(j == last)
        def _():
            @pl.when(i == 0)
            def _():
                dg_ref[...] = jnp.zeros_like(dg_ref)

            h1 = h1_ref[...]
            r = lax.rsqrt(jnp.mean(h1 * h1, axis=-1, keepdims=True) + EPS)
            dx, dgam = _rms_bwd(dxn_acc[rows, :], h1, r, g_ref[...])
            dh1_ref[...] = dy_ref[...] + dx
            dg_ref[...] += dgam

    assert ns == 2
    late = lambda j, i: (jnp.where(j == last, i, 0), 0)
    return _pcall(
        body, name="mlp_bwd", grid=(nj, nt),
        in_specs=[pl.BlockSpec((D, tm), lambda j, i: (0, i)), pl.BlockSpec((tm, ns * FF_BLK), lambda j, i: (i, j)),
                  pl.BlockSpec((tm, D), late), pl.BlockSpec((tm, D), late),
                  pl.BlockSpec((tm, D), lambda j, i: (i, 0)), pl.BlockSpec((1, D), lambda j, i: (0, 0)),
                  pl.BlockSpec((ns, D, FF_BLK), lambda j, i: (j, 0, 0), pipeline_mode=pl.Buffered(1)),
                  pl.BlockSpec((ns, FF_BLK, D), lambda j, i: (j, 0, 0), pipeline_mode=pl.Buffered(1))],
        out_specs=[pl.BlockSpec((tm, D), late),
                   pl.BlockSpec((ns, 1, D, FF_BLK), lambda j, i: (0, j, 0, 0)),
                   pl.BlockSpec((ns, 1, FF_BLK, D), lambda j, i: (0, j, 0, 0)),
                   pl.BlockSpec((1, D), lambda j, i: (0, 0))],
        out_shape=[jax.ShapeDtypeStruct((T, D), F32), jax.ShapeDtypeStruct((2, 4, D, FF_BLK), BF16),
                   jax.ShapeDtypeStruct((2, 4, FF_BLK, D), BF16), jax.ShapeDtypeStruct((1, D), F32)],
        scratch_shapes=[pltpu.VMEM((T, D), F32), pltpu.VMEM((ns, D, FF_BLK), F32), pltpu.VMEM((ns, FF_BLK, D), F32)],
        sem=("arbitrary", "arbitrary"), operands=(xn2t, act, h1, dh2, dh2b, gamma, w_up, w_down), jobs=jobs)


def _outproj_bwd(dh1, o_gla, o_conv, o_att, w_out, jobs=()):
    T = dh1.shape[0]
    tm = 512
    nt = T // tm

    def body(dy_ref, og_ref, oc_ref, oa_ref, wo_ref, dg_ref, dc_ref, da_ref, dw_ref, acc):
        i = pl.program_id(0)

        @pl.when(i == 0)
        def _():
            acc[...] = jnp.zeros_like(acc)

        dyb = dy_ref[...].astype(BF16)
        dm = _dot_nt(dyb, wo_ref[...])
        dg_ref[...] = dm[:, 0:384].astype(BF16)
        dc_ref[...] = dm[:, 384:640].astype(BF16)
        _store_head_padded(da_ref, dm[:, 640:1024].astype(BF16))
        mixed = jnp.concatenate([og_ref[...], oc_ref[...], oa_ref[...]], axis=1)
        acc[...] += _dot_tn(mixed, dyb)

        @pl.when(i == nt - 1)
        def _():
            for j in range(N_DEV):
                dw_ref[j % 2, j // 2] = acc[j * 128:(j + 1) * 128, :].astype(BF16)

    return _pcall(
        body, name="outproj_bwd", grid=(nt,),
        in_specs=[_row_spec(tm, D), _row_spec(tm, 384), _row_spec(tm, 256), _row_spec(tm, 384),
                  _full_spec((D, D))],
        out_specs=[_row_spec(tm, 384), _row_spec(tm, 256), _row_spec(tm, ATT_WIDE), _full_spec((2, 4, 128, D))],
        out_shape=[jax.ShapeDtypeStruct((T, 384), BF16), jax.ShapeDtypeStruct((T, 256), BF16),
                   jax.ShapeDtypeStruct((T, ATT_WIDE), BF16), jax.ShapeDtypeStruct((2, 4, 128, D), BF16)],
        scratch_shapes=[pltpu.VMEM((D, D), F32)],
        sem=("arbitrary",), operands=(dh1, o_gla, o_conv, o_att, w_out), jobs=jobs)


def _loss_fwd_bwd(h, gamma, target):
    T = h.shape[0]
    tm = 512

    def body(h_ref, g_ref, t_ref, loss_ref, dh_ref, dhb_ref, dg_ref):
        @pl.when(pl.program_id(0) == 0)
        def _():
            loss_ref[...] = jnp.zeros_like(loss_ref)
            dg_ref[...] = jnp.zeros_like(dg_ref)

        x = h_ref[...]
        r = lax.rsqrt(jnp.mean(x * x, axis=-1, keepdims=True) + EPS)
        gamma_ = g_ref[...]
        e = x * r * gamma_ - t_ref[...]
        loss_ref[...] += 0.5 * jnp.sum(jnp.mean(e * e, axis=-1, keepdims=True), axis=0, keepdims=True)
        dx, dgam = _rms_bwd(e * (1.0 / D), x, r, gamma_)
        dh_ref[...] = dx
        dhb_ref[...] = dx.astype(BF16)
        dg_ref[...] += dgam

    return pl.pallas_call(
        body, name="loss_fwd_bwd", grid=(T // tm,),
        in_specs=[_row_spec(tm, D), _full_spec((1, D)), _row_spec(tm, D)],
        out_specs=[_full_spec((8, 128)), _row_spec(tm, D), _row_spec(tm, D), _full_spec((1, D))],
        out_shape=[jax.ShapeDtypeStruct((8, 128), F32), jax.ShapeDtypeStruct((T, D), F32),
                   jax.ShapeDtypeStruct((T, D), BF16), jax.ShapeDtypeStruct((1, D), F32)],
        compiler_params=_cp(("arbitrary",)),
    )(h, gamma, target)


def _adamw_math(w, g, m, v):
    m = ADAM_B1 * m + (1.0 - ADAM_B1) * g
    v = ADAM_B2 * v + (1.0 - ADAM_B2) * (g * g)
    m_hat = m / (1.0 - ADAM_B1 ** ADAM_STEP)
    v_hat = v / (1.0 - ADAM_B2 ** ADAM_STEP)
    delta = -ADAM_LR * (m_hat / (jnp.sqrt(v_hat) + ADAM_EPS) + ADAM_WD * w)
    return delta, m, v


def _rs_adamw(a_own, r2, w, m, v, layer, chip_idx, rows_blk, prev=None):
    _, R, C = w.shape
    nblk = R // rows_blk

    def body(chip_ref, a_ref, r_ref, w_ref, m_ref, v_ref, *rest):
        g_out, d_out, m_out, v_out = rest[-4:]
        g = (a_ref[0].astype(F32) + r_ref[0].astype(F32)) + (r_ref[1].astype(F32) + r_ref[2].astype(F32))
        delta, m_new, v_new = _adamw_math(w_ref[0], g, m_ref[0], v_ref[0])
        g_out[0] = g
        d_out[0] = delta
        m_out[0] = m_new
        v_out[0] = v_new

    blk = pl.BlockSpec((1, rows_blk, C), lambda i, chip: (layer, i, 0))
    n_prev = 0 if prev is None else 4
    grid_spec = pltpu.PrefetchScalarGridSpec(
        num_scalar_prefetch=1, grid=(nblk,),
        in_specs=[pl.BlockSpec((1, rows_blk, C), lambda i, chip: (chip[0], i, 0)),
                  pl.BlockSpec((3, rows_blk, C), lambda i, chip: (0, i, 0)), blk, blk, blk]
        + [_any_spec()] * n_prev,
        out_specs=[blk, blk, blk, blk])
    return pl.pallas_call(
        body, name="rs_adamw", grid_spec=grid_spec,
        out_shape=[jax.ShapeDtypeStruct((DEPTH, R, C), F32)] * 4,
        input_output_aliases={6 + t: t for t in range(n_prev)},
        compiler_params=_cp(("arbitrary",)),
    )(chip_idx, a_own, r2, w, m, v, *(prev or ()))


def _pair_sum(g, r1, core_idx, rows_blk):
    _, _, R, C = g.shape
    nblk = R // rows_blk

    def body(core_ref, g_ref, r_ref, o_ref):
        o_ref[...] = (g_ref[0].astype(F32) + r_ref[...].astype(F32)).astype(BF16)

    grid_spec = pltpu.PrefetchScalarGridSpec(
        num_scalar_prefetch=1, grid=(4, nblk),
        in_specs=[pl.BlockSpec((1, 1, rows_blk, C), lambda k, i, core: (core[0], k, i, 0)),
                  pl.BlockSpec((1, rows_blk, C), lambda k, i, core: (k, i, 0))],
        out_specs=pl.BlockSpec((1, rows_blk, C), lambda k, i, core: (k, i, 0)))
    return pl.pallas_call(
        body, name="rs_pair_sum", grid_spec=grid_spec,
        out_shape=jax.ShapeDtypeStruct((4, R, C), BF16),
        compiler_params=_cp(("arbitrary", "arbitrary")),
    )(core_idx, g, r1)


def _small_sum(gathered):
    def body(g_ref, o_ref):
        acc = g_ref[0]
        for d in range(1, N_DEV):
            acc = acc + g_ref[d]
        o_ref[...] = acc

    return pl.pallas_call(
        body, name="small_sum",
        out_shape=jax.ShapeDtypeStruct(gathered.shape[1:], F32),
        compiler_params=_cp(),
    )(gathered)


def _adamw_small(ws, gs, ms, vs):
    n = len(ws)

    def body(*refs):
        w_r, g_r, m_r, v_r = refs[0:n], refs[n:2 * n], refs[2 * n:3 * n], refs[3 * n:4 * n]
        d_o, m_o, v_o = refs[4 * n:5 * n], refs[5 * n:6 * n], refs[6 * n:7 * n]
        for t in range(n):
            delta, m_new, v_new = _adamw_math(w_r[t][...], g_r[t][...], m_r[t][...], v_r[t][...])
            d_o[t][...] = delta
            m_o[t][...] = m_new
            v_o[t][...] = v_new

    shapes = [jax.ShapeDtypeStruct(w.shape, F32) for w in ws]
    outs = pl.pallas_call(
        body, name="adamw_small", out_shape=shapes * 3, compiler_params=_cp(),
    )(*ws, *gs, *ms, *vs)
    return outs[0:n], outs[n:2 * n], outs[2 * n:3 * n]


def _mesh_pos():
    return lax.axis_index("x"), lax.axis_index("y"), lax.axis_index("c")


def _peers():
    x, y, c = _mesh_pos()
    return (x, y, c), (x, y, 1 - c), [(1 - x, y), (x, 1 - y), (1 - x, 1 - y)]


def _slot(ref, pos):
    return ref.at[4 * pos[0] + 2 * pos[1] + pos[2]]


def _remote(src, dst, send_sem, recv_sem, to):
    return pltpu.make_async_remote_copy(src_ref=src, dst_ref=dst, send_sem=send_sem, recv_sem=recv_sem,
                                        device_id=to, device_id_type=MESH)


def _ag_spread(shards):
    n = len(shards)

    def copies(ins, outs, sems):
        send, recv, loc = sems
        me, sibling, chips = _peers()
        peers = [sibling] + [(*chip, me[2]) for chip in chips]
        local = [pltpu.make_async_copy(ins[a], _slot(outs[a], me), loc.at[a]) for a in range(n)]
        sends = [_remote(ins[a], _slot(outs[a], me), send.at[a, k], recv.at[a, k], p)
                 for a in range(n) for k, p in enumerate(peers)]
        recvs = [_remote(ins[a], _slot(outs[a], p), send.at[a, k], recv.at[a, k], p)
                 for a in range(n) for k, p in enumerate(peers)]
        return local, sends, recvs

    def start(ins, outs, sems):
        local, sends, _ = copies(ins, outs, sems)
        for cp in local + sends:
            cp.start()

    def finish(ins, outs, sems):
        local, sends, recvs = copies(ins, outs, sems)
        for cp in sends:
            cp.wait_send()
        for cp in recvs:
            cp.wait_recv()
        for cp in local:
            cp.wait()

    return _Job(shards, [jax.ShapeDtypeStruct((N_DEV,) + a.shape, a.dtype) for a in shards],
                [pltpu.SemaphoreType.DMA((n, 4)), pltpu.SemaphoreType.DMA((n, 4)), pltpu.SemaphoreType.DMA((n,))],
                start, finish)


def _ag_pass(stacks):
    n = len(stacks)

    def copies(ins, outs, sems):
        send, recv = sems
        me, sibling, chips = _peers()
        sends = [_remote(_slot(ins[a], (*chip, me[2])), _slot(outs[a], (*chip, me[2])), send.at[a, j], recv.at[a, j],
                         sibling) for a in range(n) for j, chip in enumerate(chips)]
        recvs = [_remote(_slot(ins[a], (*chip, me[2])), _slot(outs[a], (*chip, 1 - me[2])), send.at[a, j],
                         recv.at[a, j], sibling) for a in range(n) for j, chip in enumerate(chips)]
        return sends, recvs

    def start(ins, outs, sems):
        for cp in copies(ins, outs, sems)[0]:
            cp.start()

    def finish(ins, outs, sems):
        sends, recvs = copies(ins, outs, sems)
        for cp in sends:
            cp.wait_send()
        for cp in recvs:
            cp.wait_recv()

    return _Job(stacks, [jax.ShapeDtypeStruct(a.shape, a.dtype) for a in stacks],
                [pltpu.SemaphoreType.DMA((n, 3)), pltpu.SemaphoreType.DMA((n, 3))],
                start, finish, aliases={a: a for a in range(n)})


def _ag_both(shards):
    spread = _ag_spread(shards)
    fake = [jax.ShapeDtypeStruct((N_DEV,) + a.shape, a.dtype) for a in shards]
    onward = _ag_pass(fake)
    n_sp = len(spread.sems)

    def start(ins, outs, sems):
        spread.start(ins, outs, sems[:n_sp])

    def finish(ins, outs, sems):
        spread.finish(ins, outs, sems[:n_sp])
        onward.start(outs, outs, sems[n_sp:])
        onward.finish(outs, outs, sems[n_sp:])

    return _Job(shards, spread.out_shapes, spread.sems + onward.sems, start, finish)


def _rs_swap(parts):
    n = len(parts)

    def copies(ins, outs, sems):
        send, recv = sems
        me, sibling, _ = _peers()
        return [_remote(ins[a].at[1 - me[2]], outs[a], send.at[a], recv.at[a], sibling) for a in range(n)]

    def start(ins, outs, sems):
        for cp in copies(ins, outs, sems):
            cp.start()

    def finish(ins, outs, sems):
        for cp in copies(ins, outs, sems):
            cp.wait()

    return _Job(parts, [jax.ShapeDtypeStruct(a.shape[1:], a.dtype) for a in parts],
                [pltpu.SemaphoreType.DMA((n,)), pltpu.SemaphoreType.DMA((n,))], start, finish)


def _rs_ici(pairs):
    n = len(pairs)

    def copies(ins, outs, sems):
        send, recv = sems
        me, _, chips = _peers()
        return [_remote(ins[a].at[2 * chip[0] + chip[1]], outs[a].at[j], send.at[a, j], recv.at[a, j],
                        (*chip, me[2])) for a in range(n) for j, chip in enumerate(chips)]

    def start(ins, outs, sems):
        for cp in copies(ins, outs, sems):
            cp.start()

    def finish(ins, outs, sems):
        for cp in copies(ins, outs, sems):
            cp.wait()

    return _Job(pairs, [jax.ShapeDtypeStruct((3,) + a.shape[1:], a.dtype) for a in pairs],
                [pltpu.SemaphoreType.DMA((n, 3)), pltpu.SemaphoreType.DMA((n, 3))], start, finish)


def _comm_call(jobs, name):
    def body():
        pass

    return _pcall(body, name=name, grid=(), in_specs=[], out_specs=[], out_shape=[], operands=(), jobs=jobs)[1]


def _allgather(arrs, name):
    n = len(arrs)

    def body(*refs):
        ins, outs = refs[:n], refs[n:2 * n]
        send_sems, recv_sems, local_sems = refs[2 * n:]
        x, y, c = _mesh_pos()
        me, sibling = (x, y, c), (x, y, 1 - c)
        chips = [(1 - x, y), (x, 1 - y), (1 - x, 1 - y)]

        def slot(a, pos):
            return outs[a].at[4 * pos[0] + 2 * pos[1] + pos[2]]

        def copy(a, k, block, to, src=None):
            return pltpu.make_async_remote_copy(
                src_ref=slot(a, block) if src is None else src, dst_ref=slot(a, block),
                send_sem=send_sems.at[a, k], recv_sem=recv_sems.at[a, k],
                device_id=to, device_id_type=MESH)

        mine = [pltpu.make_async_copy(ins[a], slot(a, me), local_sems.at[a]) for a in range(n)]
        for cp in mine:
            cp.start()
        first = []
        for a in range(n):
            first.append(copy(a, 0, me, sibling, src=ins[a]))
            first += [copy(a, 1 + j, me, (*chip, c), src=ins[a]) for j, chip in enumerate(chips)]
        for cp in first:
            cp.start()
        passed = []
        for j, chip in enumerate(chips):
            for a in range(n):
                copy(a, 1 + j, (*chip, c), me).wait_recv()
                fwd = copy(a, 4 + j, (*chip, c), sibling)
                fwd.start()
                passed.append(fwd)
        for a in range(n):
            copy(a, 0, sibling, me).wait_recv()
            for j, chip in enumerate(chips):
                copy(a, 4 + j, (*chip, 1 - c), me).wait_recv()
        for cp in first + passed:
            cp.wait_send()
        for cp in mine:
            cp.wait()

    return pl.pallas_call(
        body, name=name,
        in_specs=[_any_spec()] * n, out_specs=[_any_spec()] * n,
        out_shape=[jax.ShapeDtypeStruct((N_DEV,) + a.shape, a.dtype) for a in arrs],
        scratch_shapes=[pltpu.SemaphoreType.DMA((n, 7)), pltpu.SemaphoreType.DMA((n, 7)),
                        pltpu.SemaphoreType.DMA((n,))],
        compiler_params=_cp(),
    )(*arrs)


W_IN_SHARD = 354
W_IN_COLS = ((0, 192, OQ), (192, 192, OKK), (384, 384, OV), (768, 384, OG), (1152, 16, OLR), (1168, 512, OCU),
             (1680, 384, OAQ), (2064, 384, OAK), (2448, 384, OAV))


def _w_in_padded(stack):
    new_to_ref = {new: (start, width) for start, width, new in W_IN_COLS}
    cols = []
    for new, padded in IN_GROUPS:
        start, width = new_to_ref[new]
        a = start
        while a < start + width:
            j = a // W_IN_SHARD
            b = min(start + width, (j + 1) * W_IN_SHARD)
            cols.append(stack[j][:, a - j * W_IN_SHARD:b - j * W_IN_SHARD])
            a = b
        if padded > width:
            cols.append(jnp.zeros((stack.shape[1], padded - width), stack.dtype))
    return jnp.concatenate(cols, axis=1)


def _dw_in_shards(dw):
    shards = []
    for j in range(N_DEV):
        lo, hi = j * W_IN_SHARD, (j + 1) * W_IN_SHARD
        segs = []
        for start, width, new in W_IN_COLS:
            a, b = max(lo, start), min(hi, start + width)
            if a < b:
                segs.append(dw[:, new + a - start:new + b - start])
        shards.append(jnp.concatenate(segs, axis=1))
    return jnp.stack([jnp.stack([shards[2 * chip + core] for chip in range(4)]) for core in range(2)])


def _pad_to(a, shape):
    return jnp.pad(a, [(0, s - d) for d, s in zip(a.shape, shape)])


SMALL_LAYOUT = (
    ("norm_mix", 2, 1024), ("norm_ffn", 2, 1024), ("norm_final", 1, 1024), ("gla_norm", 2, 384),
    ("b_gla_gate", 2, 192), ("b_dw", 2, 256), ("conv_ln_g", 2, 256), ("conv_ln_b", 2, 256),
    ("rel_bias", 12, 257), ("w_gla_gate", 32, 192), ("w_dw", 62, 256),
)
SMALL_LANES = 128
SMALL_TILE = 8 * SMALL_LANES


def _small_tile_rows(r, lanes):
    return -(-(r * lanes) // SMALL_TILE) * 8


SMALL_ROWS = sum(_small_tile_rows(r, lanes) for _, r, lanes in SMALL_LAYOUT)


def _pack_small(parts):
    tiles = []
    for name, r, lanes in SMALL_LAYOUT:
        rows = _small_tile_rows(r, lanes)
        flat = _pad_to(parts[name].reshape(r * lanes), (rows * SMALL_LANES,))
        tiles.append(flat.reshape(rows, SMALL_LANES))
    return jnp.concatenate(tiles, axis=0)


def _unpack_small(packed):
    out, r0 = {}, 0
    for name, r, lanes in SMALL_LAYOUT:
        rows = _small_tile_rows(r, lanes)
        out[name] = packed[r0:r0 + rows].reshape(rows * SMALL_LANES)[:r * lanes].reshape(r, lanes)
        r0 += rows
    return out


def _mixers_fwd(h, wl, w_in_p, plan=None):
    plan, res = plan or {}, {}

    def jobs(host):
        return plan[host](res) if host in plan else ()

    (q, k, v, g, cu, aq, ak, av, lr), res["inproj"] = _inproj_fwd(h, wl["norm_mix"], w_in_p, jobs=jobs("inproj"))
    bias = _relbias_expand(wl["rb"])
    (o_att, lse), res["att"] = _att_fwd(aq, ak, av, bias, jobs=jobs("att"))
    (o_gla, states), res["gla"] = _gla_fwd(q, k, v, g, lr, wl["wg"], wl["bg"], wl["gn"], jobs=jobs("gla"))
    (o_conv, conv), res["conv"] = _conv_fwd(cu, wl["w_dw"], wl["b_dw"], wl["ln_g"], wl["ln_b"], jobs=jobs("conv"))
    sv = dict(h=h, w_in=w_in_p, q=q, k=k, v=v, g=g, cu=cu, aq=aq, ak=ak, av=av, lr=lr,
              o_gla=o_gla, o_conv=o_conv, conv=conv, o_att=o_att, lse=lse, states=states, bias=bias)
    return sv, res


def _mixers_bwd(sv, wl, dh1, d_ogla, d_oconv, att_grads, conv_jobs=(), x_jobs_fn=None):
    daq, dak, dav, dbias = att_grads
    d_rb = _relbias_grad(dbias)
    (dcu, dw_dw, db_dw, dln_g, dln_b), conv_res = _conv_bwd(
        sv["cu"], sv["conv"], d_oconv, wl["w_dw"], wl["b_dw"], wl["ln_g"], wl["ln_b"], jobs=conv_jobs)
    dq, dk, dv, dg, dlr, dwg, dbg, dgn = _gla_bwd(sv["q"], sv["k"], sv["v"], sv["g"], sv["lr"], sv["states"],
                                                  d_ogla, wl["wg"], wl["bg"], wl["gn"])
    dparts = (dq, dk, dv, dg, dcu, daq, dak, dav, dlr)
    dw_in = _inproj_bwd_w(sv["h"], wl["norm_mix"], dparts)
    x_jobs = x_jobs_fn(dw_in) if x_jobs_fn is not None else ()
    (dh, dhb, d_nmix), x_res = _inproj_bwd_x(sv["h"], dh1, wl["norm_mix"], sv["w_in"], dparts, jobs=x_jobs)
    small = dict(norm_mix=d_nmix, wg=dwg, bg=dbg, gn=dgn, w_dw=dw_dw, b_dw=db_dw, ln_g=dln_g, ln_b=dln_b, rb=d_rb)
    return (dh, dhb), dw_in, small, conv_res, x_res


def _layer_small(l, w_dw_full, norm_mix, w_gla_gate, b_gla_gate, gla_norm, b_dw, conv_ln_g, conv_ln_b, rel_bias,
                 norm_ffn):
    return dict(
        norm_mix=norm_mix[l][None, :], norm_ffn=norm_ffn[l][None, :],
        wg=_pad_to(w_gla_gate[l], (128, 256)).astype(BF16), bg=_pad_to(b_gla_gate[l][None, :], (1, 256)),
        gn=gla_norm[l][None, :], w_dw=_pad_to(w_dw_full, (32, 256)), b_dw=b_dw[l][None, :],
        ln_g=conv_ln_g[l][None, :], ln_b=conv_ln_b[l][None, :], rb=_pad_to(rel_bias[l], (8, 384)))


RS_ROWS = dict(w_in=512, w_out=128, w_up=512, w_down=256)


def kernel(x, norm_mix, w_in, w_gla_gate, b_gla_gate, gla_norm, w_dw, b_dw, conv_ln_g, conv_ln_b, rel_bias, w_out, norm_ffn, w_up, w_down, norm_final, loss_target, m_norm_mix, m_w_in, m_w_gla_gate, m_b_gla_gate, m_gla_norm, m_w_dw, m_b_dw, m_conv_ln_g, m_conv_ln_b, m_rel_bias, m_w_out, m_norm_ffn, m_w_up, m_w_down, m_norm_final, v_norm_mix, v_w_in, v_w_gla_gate, v_b_gla_gate, v_gla_norm, v_w_dw, v_b_dw, v_conv_ln_g, v_conv_ln_b, v_rel_bias, v_w_out, v_norm_ffn, v_w_up, v_w_down, v_norm_final):
    mx, my, mc = _mesh_pos()
    me = 4 * mx + 2 * my + mc
    chip_idx = (2 * mx + my).astype(jnp.int32).reshape(1)
    core_idx = mc.astype(jnp.int32).reshape(1)
    x0, target = x[0], loss_target[0]

    def pair_sums(parts, r1):
        return [_pair_sum(p, r, core_idx, p.shape[2]) for p, r in zip(parts, r1)]

    sh = [dict(w_in=w_in[l].astype(BF16), w_out=w_out[l].astype(BF16), w_up=w_up[l].astype(BF16),
               w_down=w_down[l].astype(BF16)) for l in range(DEPTH)]
    dw_flat = _pad_to(w_dw, (DEPTH, 32, 32)).reshape(16, 128)
    st_in0, st_dw = _allgather([sh[0]["w_in"], dw_flat], "allgather_first")
    dw_all = st_dw.reshape(N_DEV, DEPTH, 32, 32)[:, :, :KCONV, :]
    dw_all = jnp.transpose(dw_all, (1, 2, 0, 3)).reshape(DEPTH, KCONV, 256)
    wl = [_layer_small(l, dw_all[l], norm_mix, w_gla_gate, b_gla_gate, gla_norm, b_dw, conv_ln_g, conv_ln_b,
                       rel_bias, norm_ffn) for l in range(DEPTH)]

    s0, s1 = sh[0], sh[1]
    half = s0["w_down"].shape[0] // 2
    down0_a, down0_b = s0["w_down"][:half], s0["w_down"][half:]
    sv0, g0 = _mixers_fwd(x0, wl[0], _w_in_padded(st_in0), plan=dict(
        inproj=lambda r: [_ag_spread([s0["w_out"], down0_a])],
        att=lambda r: [_ag_spread([s0["w_up"]]), _ag_pass(r["inproj"])],
        gla=lambda r: [_ag_spread([down0_b]), _ag_pass(r["att"][:1])],
        conv=lambda r: [_ag_pass(r["gla"][:1])]))
    st_out0, st_down0_a = g0["att"][1:]
    st_up0, st_down0_b = g0["gla"][1], g0["conv"][0]
    st_down0 = jnp.concatenate([st_down0_a, st_down0_b], axis=1)
    wo0 = st_out0.reshape(D, D)
    (h1_0, xn2t_0, h2_0, act_0), (st_in1, out1_half) = _outproj_mlp_fwd(
        x0, sv0["o_gla"], sv0["o_conv"], sv0["o_att"], wo0, wl[0]["norm_ffn"], st_up0, st_down0,
        jobs=[_ag_both([s1["w_in"]]), _ag_spread([s1["w_out"]])])

    sv1, g1 = _mixers_fwd(h2_0, wl[1], _w_in_padded(st_in1), plan=dict(
        inproj=lambda r: [_ag_spread([s1["w_up"]]), _ag_pass([out1_half])],
        att=lambda r: [_ag_spread([s1["w_down"]]), _ag_pass(r["inproj"][:1])],
        gla=lambda r: [_ag_pass(r["att"][:1])]))
    st_out1, st_up1, st_down1 = g1["inproj"][1], g1["att"][1], g1["gla"][0]
    wo1 = st_out1.reshape(D, D)
    (h1_1, xn2t_1, h2_1, act_1), _ = _outproj_mlp_fwd(
        h2_0, sv1["o_gla"], sv1["o_conv"], sv1["o_att"], wo1, wl[1]["norm_ffn"], st_up1, st_down1)

    loss8, dh, dhb, d_nf = _loss_fwd_bwd(h2_1, norm_final[None, :], target)
    loss = lax.psum(loss8[0, 0], ("x", "y", "c"))

    def layer_bwd(dh_pair, sv, wl_l, xn2t, act, h1, wo, st_up, st_down, mlp_jobs, x_jobs_fn):
        (dh1, dw_up, dw_down, d_nffn), mlp_res = _mlp_bwd(xn2t, act, h1, dh_pair[0], dh_pair[1], wl_l["norm_ffn"],
                                                           st_up, st_down, jobs=mlp_jobs)
        ud = [dw_up, dw_down]
        (d_ogla, d_oconv, d_oatt, dw_out), r1 = _outproj_bwd(
            dh1, sv["o_gla"], sv["o_conv"], sv["o_att"], wo, jobs=[_rs_swap(ud)])
        pair_ud = pair_sums(ud, r1)
        att_grads, r = _att_bwd(sv["aq"], sv["ak"], sv["av"], sv["bias"], sv["o_att"], sv["lse"], d_oatt,
                                jobs=[_rs_ici(pair_ud), _rs_swap([dw_out])])
        r2_ud, r1_out = r[:2], r[2:]
        pair_out = pair_sums([dw_out], r1_out)
        dh_in, _, small, r2_out, x_res = _mixers_bwd(sv, wl_l, dh1, d_ogla, d_oconv, att_grads,
                                                     conv_jobs=[_rs_ici(pair_out)], x_jobs_fn=x_jobs_fn)
        small["norm_ffn"] = d_nffn
        sums = dict(w_out=(pair_out[0], r2_out[0]), w_up=(pair_ud[0], r2_ud[0]), w_down=(pair_ud[1], r2_ud[1]))
        return dh_in, small, sums, mlp_res, x_res

    stash = {}

    def swap_w_in(dw_in):
        stash["in1"] = [_dw_in_shards(dw_in)]
        return [_rs_swap(stash["in1"])]

    dh_pair, small1, sums1, _, r1_in1 = layer_bwd((dh, dhb), sv1, wl[1], xn2t_1, act_1, h1_1, wo1, st_up1, st_down1,
                                                  (), swap_w_in)
    pair_in1 = pair_sums(stash["in1"], r1_in1)

    def send_w_in(dw_in):
        in0 = [_dw_in_shards(dw_in)]
        stash["pair_in0"] = pair_sums(in0, _comm_call([_rs_swap(in0)], "rs_swap_w_in_0"))
        return [_rs_ici(stash["pair_in0"])]

    (dx, _), small0, sums0, r2_in1, r2_in0 = layer_bwd(dh_pair, sv0, wl[0], xn2t_0, act_0, h1_0, wo0, st_up0, st_down0,
                                                       [_rs_ici(pair_in1)], send_w_in)
    sums1["w_in"] = (pair_in1[0], r2_in1[0])
    sums0["w_in"] = (stash["pair_in0"][0], r2_in0[0])

    big_w = dict(w_in=(w_in, m_w_in, v_w_in), w_out=(w_out, m_w_out, v_w_out), w_up=(w_up, m_w_up, v_w_up),
                 w_down=(w_down, m_w_down, v_w_down))
    pairs = {1: sums1, 0: sums0}
    big_out = {}
    for name, (w_, m_, v_) in big_w.items():
        res = None
        for l in (1, 0):
            a_own, r2_ = pairs[l][name]
            res = _rs_adamw(a_own, r2_, w_, m_, v_, l, chip_idx, RS_ROWS[name], prev=res)
        big_out[name] = res

    grads = (small0, small1)
    parts = dict(
        norm_mix=jnp.concatenate([grads[l]["norm_mix"] for l in range(DEPTH)], axis=0),
        norm_ffn=jnp.concatenate([grads[l]["norm_ffn"] for l in range(DEPTH)], axis=0),
        norm_final=d_nf,
        gla_norm=jnp.concatenate([grads[l]["gn"] for l in range(DEPTH)], axis=0),
        b_gla_gate=jnp.concatenate([grads[l]["bg"][:, :192] for l in range(DEPTH)], axis=0),
        b_dw=jnp.concatenate([grads[l]["b_dw"] for l in range(DEPTH)], axis=0),
        conv_ln_g=jnp.concatenate([grads[l]["ln_g"] for l in range(DEPTH)], axis=0),
        conv_ln_b=jnp.concatenate([grads[l]["ln_b"] for l in range(DEPTH)], axis=0),
        rel_bias=jnp.concatenate([grads[l]["rb"][:6, :N_REL] for l in range(DEPTH)], axis=0),
        w_gla_gate=jnp.concatenate([grads[l]["wg"][:16, :192] for l in range(DEPTH)], axis=0),
        w_dw=jnp.concatenate([grads[l]["w_dw"][:KCONV] for l in range(DEPTH)], axis=0),
    )
    small_all = _allgather([_pack_small(parts)], "allgather_small")[0]
    sg = _unpack_small(_small_sum(small_all))
    dw_grad = lax.dynamic_slice_in_dim(sg["w_dw"].reshape(DEPTH, KCONV, 256), me * 32, 32, axis=2)
    small_g = dict(
        norm_mix=sg["norm_mix"], w_gla_gate=sg["w_gla_gate"].reshape(DEPTH, 16, 192), b_gla_gate=sg["b_gla_gate"],
        gla_norm=sg["gla_norm"], w_dw=dw_grad, b_dw=sg["b_dw"], conv_ln_g=sg["conv_ln_g"],
        conv_ln_b=sg["conv_ln_b"], rel_bias=sg["rel_bias"].reshape(DEPTH, 6, N_REL), norm_ffn=sg["norm_ffn"],
        norm_final=sg["norm_final"].reshape(D))
    small_names = ("norm_mix", "w_gla_gate", "b_gla_gate", "gla_norm", "w_dw", "b_dw", "conv_ln_g", "conv_ln_b",
                   "rel_bias", "norm_ffn", "norm_final")
    small_w = dict(norm_mix=norm_mix, w_gla_gate=w_gla_gate, b_gla_gate=b_gla_gate, gla_norm=gla_norm, w_dw=w_dw,
                   b_dw=b_dw, conv_ln_g=conv_ln_g, conv_ln_b=conv_ln_b, rel_bias=rel_bias, norm_ffn=norm_ffn,
                   norm_final=norm_final)
    small_m = dict(norm_mix=m_norm_mix, w_gla_gate=m_w_gla_gate, b_gla_gate=m_b_gla_gate, gla_norm=m_gla_norm,
                   w_dw=m_w_dw, b_dw=m_b_dw, conv_ln_g=m_conv_ln_g, conv_ln_b=m_conv_ln_b, rel_bias=m_rel_bias,
                   norm_ffn=m_norm_ffn, norm_final=m_norm_final)
    small_v = dict(norm_mix=v_norm_mix, w_gla_gate=v_w_gla_gate, b_gla_gate=v_b_gla_gate, gla_norm=v_gla_norm,
                   w_dw=v_w_dw, b_dw=v_b_dw, conv_ln_g=v_conv_ln_g, conv_ln_b=v_conv_ln_b, rel_bias=v_rel_bias,
                   norm_ffn=v_norm_ffn, norm_final=v_norm_final)
    s_delta, s_m, s_v = _adamw_small([small_w[n] for n in small_names], [small_g[n] for n in small_names],
                                     [small_m[n] for n in small_names], [small_v[n] for n in small_names])
    s_idx = {n: t for t, n in enumerate(small_names)}

    order = ("norm_mix", "w_in", "w_gla_gate", "b_gla_gate", "gla_norm", "w_dw", "b_dw", "conv_ln_g", "conv_ln_b",
             "rel_bias", "w_out", "norm_ffn", "w_up", "w_down", "norm_final")

    def pick(kind, name):
        if name in big_out:
            return big_out[name][kind]
        t = s_idx[name]
        return (small_g[name], s_delta[t], s_m[t], s_v[t])[kind]

    outs = [loss, dx[None]]
    for kind in range(4):
        outs += [pick(kind, n) for n in order]
    return tuple(outs)
```

```python
import functools

import jax
import jax.numpy as jnp
from jax import lax
from jax.experimental import pallas as pl
from jax.experimental.pallas import tpu as pltpu

F32 = jnp.float32
BF16 = jnp.bfloat16
MESH = pl.DeviceIdType.MESH

D = 1024
DEPTH = 2
CH = 64
EPS = 1e-6
NEG = -1e30
N_DEV = 8
N_REL = 257
Q_SCALE = 48.0 ** -0.5
A_SCALE = 64.0 ** -0.5
GATE_TAU = 16.0
KCONV = 31

OQ, OKK, OV, OG, OCU, OAQ, OAK, OAV, OLR, DINP = 0, 256, 512, 896, 1280, 1792, 2176, 2560, 2944, 3072
IN_GROUPS = ((OQ, 256), (OKK, 256), (OV, 384), (OG, 384), (OCU, 512), (OAQ, 384), (OAK, 384), (OAV, 384), (OLR, 128))

AQ_BLK = 256
AK_WIN = 768
WIN_LEFT = 2
RB_W = 1536

ADAM_LR, ADAM_B1, ADAM_B2, ADAM_EPS, ADAM_WD, ADAM_STEP = 0.001, 0.9, 0.999, 1e-08, 0.01, 10


V7X_VMEM_MIB = 64
VMEM_LIMIT_MIB = V7X_VMEM_MIB - 1


def _cp(sem=None):
    kw = {"vmem_limit_bytes": VMEM_LIMIT_MIB * 1024 * 1024}
    if sem is not None:
        kw["dimension_semantics"] = sem
    return pltpu.CompilerParams(**kw)


def _dot(a, b):
    return jnp.dot(a, b, preferred_element_type=F32)


def _dot_nt(a, b):
    return lax.dot_general(a, b, (((1,), (1,)), ((), ())), preferred_element_type=F32)


def _dot_tn(a, b):
    return lax.dot_general(a, b, (((0,), (0,)), ((), ())), preferred_element_type=F32)


def _split2(a):
    hi = a.astype(BF16)
    lo = (a - hi.astype(F32)).astype(BF16)
    return hi, lo


def _split3(a):
    hi = a.astype(BF16)
    r1 = a - hi.astype(F32)
    mid = r1.astype(BF16)
    lo = (r1 - mid.astype(F32)).astype(BF16)
    return hi, mid, lo


def _sigmoid(x):
    return 1.0 / (1.0 + jnp.exp(-x))


def _group(idx, size, n):
    g = jnp.zeros_like(idx)
    for t in range(1, n):
        g = g + (idx >= t * size).astype(jnp.int32)
    return g


def _rms_bwd(dy, x, r, gamma):
    xh = x * r
    dxh = dy * gamma
    dx = r * (dxh - xh * jnp.mean(dxh * xh, axis=-1, keepdims=True))
    return dx, jnp.sum(dy * xh, axis=0, keepdims=True)


def _row_spec(tm, n):
    return pl.BlockSpec((tm, n), lambda i: (i, 0))


def _full_spec(shape):
    nd = len(shape)
    return pl.BlockSpec(shape, lambda *_: (0,) * nd)


def _any_spec():
    return pl.BlockSpec(memory_space=pl.ANY)


class _Job:
    def __init__(self, operands, out_shapes, sems, start, finish, aliases=None):
        self.operands, self.out_shapes, self.sems = list(operands), list(out_shapes), list(sems)
        self.start, self.finish, self.aliases = start, finish, dict(aliases or {})


def _pcall(body, *, name, grid, in_specs, out_specs, out_shape, operands, scratch_shapes=(), sem=None, jobs=()):
    jobs = list(jobs)
    in_specs, out_specs, out_shape = list(in_specs), list(out_specs), list(out_shape)
    scratch_shapes = list(scratch_shapes)
    n_in, n_out, n_scr = len(in_specs), len(out_specs), len(scratch_shapes)
    j_in = [a for j in jobs for a in j.operands]
    j_out = [s for j in jobs for s in j.out_shapes]
    j_sem = [s for j in jobs for s in j.sems]
    aliases, io, oo = {}, n_in, n_out
    for j in jobs:
        for a, b in j.aliases.items():
            aliases[io + a] = oo + b
        io += len(j.operands)
        oo += len(j.out_shapes)

    def wrapped(*refs):
        own_in, ji = refs[:n_in], refs[n_in:n_in + len(j_in)]
        o0 = n_in + len(j_in)
        own_out, jo = refs[o0:o0 + n_out], refs[o0 + n_out:o0 + n_out + len(j_out)]
        s0 = o0 + n_out + len(j_out)
        own_scr, js = refs[s0:s0 + n_scr], refs[s0 + n_scr:]

        def each_job(fn_name):
            a = b = c = 0
            for j in jobs:
                na, nb, nc = len(j.operands), len(j.out_shapes), len(j.sems)
                getattr(j, fn_name)(ji[a:a + na], jo[b:b + nb], js[c:c + nc])
                a, b, c = a + na, b + nb, c + nc

        if jobs and grid:
            pids = [pl.program_id(d) for d in range(len(grid))]
            first = functools.reduce(jnp.logical_and, [p == 0 for p in pids])
            last = functools.reduce(jnp.logical_and, [p == g - 1 for p, g in zip(pids, grid)])
            pl.when(first)(lambda: each_job("start"))
        elif jobs:
            each_job("start")

        body(*own_in, *own_out, *own_scr)

        if jobs and grid:
            pl.when(last)(lambda: each_job("finish"))
        elif jobs:
            each_job("finish")

    res = pl.pallas_call(
        wrapped, name=name, grid=grid,
        in_specs=in_specs + [_any_spec()] * len(j_in), out_specs=out_specs + [_any_spec()] * len(j_out),
        out_shape=out_shape + j_out, scratch_shapes=scratch_shapes + j_sem,
        input_output_aliases=aliases, compiler_params=_cp(sem),
    )(*operands, *j_in)
    return res[:n_out], res[n_out:]


ATT_HEADS = 6
HEAD_PAD = 128
ATT_WIDE = ATT_HEADS * HEAD_PAD
ATT_GROUP_OFFS = (OAQ, OAK, OAV)


def _store_head_padded(o_ref, part):
    o_ref[...] = jnp.zeros_like(o_ref)
    for hd in range(ATT_HEADS):
        o_ref[:, hd * HEAD_PAD:hd * HEAD_PAD + 64] = part[:, hd * 64:(hd + 1) * 64]


def _inproj_fwd(h, gamma, w, jobs=()):
    T = h.shape[0]
    tm = 512

    def body(h_ref, g_ref, w_ref, *outs):
        x = h_ref[...]
        r = lax.rsqrt(jnp.mean(x * x, axis=-1, keepdims=True) + EPS)
        xn = (x * r * g_ref[...]).astype(BF16)
        p = _dot(xn, w_ref[...])
        for o_ref, (off, n) in zip(outs, IN_GROUPS):
            part = p[:, off:off + n].astype(BF16)
            if off in ATT_GROUP_OFFS:
                _store_head_padded(o_ref, part)
            else:
                o_ref[...] = part

    widths = [ATT_WIDE if off in ATT_GROUP_OFFS else n for off, n in IN_GROUPS]
    return _pcall(
        body, name="inproj_fwd", grid=(T // tm,),
        in_specs=[_row_spec(tm, D), _full_spec((1, D)), _full_spec((D, DINP))],
        out_specs=[_row_spec(tm, n) for n in widths],
        out_shape=[jax.ShapeDtypeStruct((T, n), BF16) for n in widths],
        sem=("arbitrary",), operands=(h, gamma, w), jobs=jobs)


def _inproj_norm(h_ref, g_ref):
    x = h_ref[...]
    r = lax.rsqrt(jnp.mean(x * x, axis=-1, keepdims=True) + EPS)
    return x, r, g_ref[...]


def _inproj_bwd_w(h, gamma, dparts):
    T = h.shape[0]
    tm = 512
    nt = T // tm

    def body(h_ref, g_ref, *rest):
        dp_refs = rest[:9]
        dw_ref, acc = rest[9:]
        i = pl.program_id(0)

        @pl.when(i == 0)
        def _():
            acc[...] = jnp.zeros_like(acc)

        x, r, gamma_ = _inproj_norm(h_ref, g_ref)
        xnt = jnp.transpose((x * r * gamma_).astype(BF16))
        acc[...] += _dot(xnt, jnp.concatenate([d_ref[...] for d_ref in dp_refs], axis=1))

        @pl.when(i == nt - 1)
        def _():
            dw_ref[...] = acc[...].astype(BF16)

    return pl.pallas_call(
        body, name="inproj_bwd_w", grid=(nt,),
        in_specs=[_row_spec(tm, D), _full_spec((1, D))] + [_row_spec(tm, n) for _, n in IN_GROUPS],
        out_specs=_full_spec((D, DINP)),
        out_shape=jax.ShapeDtypeStruct((D, DINP), BF16),
        scratch_shapes=[pltpu.VMEM((D, DINP), F32)],
        compiler_params=_cp(("arbitrary",)),
    )(h, gamma, *dparts)


def _inproj_bwd_x(h, dh_in, gamma, w, dparts, jobs=()):
    T = h.shape[0]
    tm = 512

    def body(h_ref, dhin_ref, g_ref, w_ref, *rest):
        dp_refs = rest[:9]
        dh_ref, dhb_ref, dg_ref = rest[9:]

        @pl.when(pl.program_id(0) == 0)
        def _():
            dg_ref[...] = jnp.zeros_like(dg_ref)

        x, r, gamma_ = _inproj_norm(h_ref, g_ref)
        dxn = _dot_nt(jnp.concatenate([d_ref[...] for d_ref in dp_refs], axis=1), w_ref[...])
        dx, dgam = _rms_bwd(dxn, x, r, gamma_)
        dh = dhin_ref[...] + dx
        dh_ref[...] = dh
        dhb_ref[...] = dh.astype(BF16)
        dg_ref[...] += dgam

    return _pcall(
        body, name="inproj_bwd_x", grid=(T // tm,),
        in_specs=[_row_spec(tm, D), _row_spec(tm, D), _full_spec((1, D)), _full_spec((D, DINP))]
        + [_row_spec(tm, n) for _, n in IN_GROUPS],
        out_specs=[_row_spec(tm, D), _row_spec(tm, D), _full_spec((1, D))],
        out_shape=[jax.ShapeDtypeStruct((T, D), F32), jax.ShapeDtypeStruct((T, D), BF16),
                   jax.ShapeDtypeStruct((1, D), F32)],
        sem=("arbitrary",), operands=(h, dh_in, gamma, w, *dparts), jobs=jobs)


GLA_ROWS = 512
GLA_NC = GLA_ROWS // CH


def _gla_consts():
    ri = lax.broadcasted_iota(jnp.int32, (CH, CH), 0)
    ci = lax.broadcasted_iota(jnp.int32, (CH, CH), 1)
    upper = (ci > ri).astype(BF16)
    vv = lax.broadcasted_iota(jnp.int32, (384, 256), 0)
    kk = lax.broadcasted_iota(jnp.int32, (384, 256), 1)
    mask_t = ((_group(vv, 96, 4) == _group(kk, 48, 4)) & (kk < 192)).astype(F32)
    pi = lax.broadcasted_iota(jnp.int32, (384, 384), 0)
    pj = lax.broadcasted_iota(jnp.int32, (384, 384), 1)
    same_head = (_group(pi, 96, 4) == _group(pj, 96, 4)).astype(BF16)
    return upper, mask_t, same_head


def _gla_gate(lr_ref, wg_ref, bg_ref):
    z = _dot(lr_ref[...], wg_ref[...]) + bg_ref[...]
    la = (jnp.minimum(z, 0.0) - jnp.log(1.0 + jnp.exp(-jnp.abs(z)))) * (1.0 / GATE_TAU)
    return z, la


def _gla_chunk_decay(la_c, upper):
    hi, lo = _split2(la_c)
    dec = _dot(upper, hi) + _dot(upper, lo)
    end = jnp.sum(la_c, axis=0, keepdims=True)
    return jnp.exp(dec), jnp.exp(end)


def _head_mean(x, same_head):
    hi, lo = _split2(x)
    return (_dot(hi, same_head) + _dot(lo, same_head)) * (1.0 / 96.0)


def _gla_fwd(q, k, v, g, lr, wg, bg, gn, jobs=()):
    T = q.shape[0]
    nb = T // GLA_ROWS

    def body(q_ref, k_ref, v_ref, g_ref, lr_ref, wg_ref, bg_ref, gn_ref, y_ref, st_ref, s_scr, o_scr, kv_scr):
        upper, mask_t, same_head = _gla_consts()

        @pl.when(pl.program_id(0) == 0)
        def _():
            s_scr[...] = jnp.zeros_like(s_scr)

        _, la = _gla_gate(lr_ref, wg_ref, bg_ref)
        decays = []
        for c in range(GLA_NC):
            rs = slice(c * CH, (c + 1) * CH)
            w, a = _gla_chunk_decay(la[rs], upper)
            decays.append(a)
            kd = (k_ref[rs, :].astype(F32) * w).astype(BF16)
            kv_scr[c] = _dot_tn(v_ref[rs, :], kd) * mask_t
        for c in range(GLA_NC):
            s_new = s_scr[...] * decays[c] + kv_scr[c]
            s_scr[...] = s_new
            st_ref[c] = s_new.astype(BF16)
        for c in range(GLA_NC):
            rs = slice(c * CH, (c + 1) * CH)
            qs = (q_ref[rs, :].astype(F32) * Q_SCALE).astype(BF16)
            o_scr[rs, :] = _dot_nt(qs, st_ref[c])
        o = o_scr[...]
        r = lax.rsqrt(_head_mean(o * o, same_head) + EPS)
        gf = g_ref[...].astype(F32)
        y_ref[...] = (o * r * gn_ref[...] * (gf * _sigmoid(gf))).astype(BF16)

    return _pcall(
        body, name="gla_fwd", grid=(nb,),
        in_specs=[_row_spec(GLA_ROWS, 256), _row_spec(GLA_ROWS, 256), _row_spec(GLA_ROWS, 384),
                  _row_spec(GLA_ROWS, 384), _row_spec(GLA_ROWS, 128),
                  _full_spec((128, 256)), _full_spec((1, 256)), _full_spec((1, 384))],
        out_specs=[_row_spec(GLA_ROWS, 384), pl.BlockSpec((GLA_NC, 384, 256), lambda i: (i, 0, 0))],
        out_shape=[jax.ShapeDtypeStruct((T, 384), BF16), jax.ShapeDtypeStruct((T // CH, 384, 256), BF16)],
        scratch_shapes=[pltpu.VMEM((384, 256), F32), pltpu.VMEM((GLA_ROWS, 384), F32),
                        pltpu.VMEM((GLA_NC, 384, 256), F32)],
        sem=("arbitrary",), operands=(q, k, v, g, lr, wg, bg, gn), jobs=jobs)


def _gla_bwd(q, k, v, g, lr, states, dy, wg, bg, gn):
    T = q.shape[0]
    nb = T // GLA_ROWS

    def rev(s):
        return nb - 1 - s

    def body(q_ref, k_ref, v_ref, g_ref, lr_ref, st_ref, stp_ref, dy_ref, wg_ref, bg_ref, gn_ref,
             dq_ref, dk_ref, dv_ref, dg_ref, dlr_ref, dwg_ref, dbg_ref, dgn_ref,
             d_scr, an_scr, o_scr, do_scr, dla_scr, dst_scr):
        upper, mask_t, same_head = _gla_consts()
        s = pl.program_id(0)
        blk = rev(s)

        @pl.when(s == 0)
        def _():
            d_scr[...] = jnp.zeros_like(d_scr)
            an_scr[...] = jnp.zeros_like(an_scr)
            dwg_ref[...] = jnp.zeros_like(dwg_ref)
            dbg_ref[...] = jnp.zeros_like(dbg_ref)
            dgn_ref[...] = jnp.zeros_like(dgn_ref)

        z, la = _gla_gate(lr_ref, wg_ref, bg_ref)
        ws, as_, qss, kds = [], [], [], []
        for c in range(GLA_NC):
            rs = slice(c * CH, (c + 1) * CH)
            w, a = _gla_chunk_decay(la[rs], upper)
            ws.append(w)
            as_.append(a)
            qs = (q_ref[rs, :].astype(F32) * Q_SCALE).astype(BF16)
            qss.append(qs)
            kds.append((k_ref[rs, :].astype(F32) * w).astype(BF16))
            o_scr[rs, :] = _dot_nt(qs, st_ref[c])
        o = o_scr[...]
        r = lax.rsqrt(_head_mean(o * o, same_head) + EPS)
        on = o * r
        gf = g_ref[...].astype(F32)
        sg = _sigmoid(gf)
        si = gf * sg
        dyf = dy_ref[...].astype(F32)
        gn_ = gn_ref[...]
        dgn_ref[...] += jnp.sum(dyf * si * on, axis=0, keepdims=True)
        dg_ref[...] = (dyf * on * gn_ * (sg * (1.0 + gf * (1.0 - sg)))).astype(BF16)
        d_on = dyf * si * gn_
        do_scr[...] = r * (d_on - on * _head_mean(d_on * on, same_head))

        for c in range(GLA_NC):
            rs = slice(c * CH, (c + 1) * CH)
            dst_scr[c] = _dot_tn(do_scr[rs, :].astype(BF16), qss[c]) * mask_t
        for c in reversed(range(GLA_NC)):
            dt = d_scr[...] * an_scr[...] + dst_scr[c]
            d_scr[...] = dt
            dst_scr[c] = dt
            an_scr[...] = as_[c]
        first = (blk > 0).astype(F32)
        for c in range(GLA_NC):
            rs = slice(c * CH, (c + 1) * CH)
            dob = do_scr[rs, :].astype(BF16)
            if c > 0:
                s_prev = st_ref[c - 1].astype(F32)
            else:
                s_prev = stp_ref[0].astype(F32) * first
            dq_ref[rs, :] = (_dot(dob, st_ref[c]) * Q_SCALE).astype(BF16)
            dt = dst_scr[c]
            da = jnp.sum(dt * s_prev, axis=0, keepdims=True)
            db = dt.astype(BF16)
            dkd = _dot(v_ref[rs, :], db)
            dv_ref[rs, :] = _dot_nt(kds[c], db).astype(BF16)
            dk_ref[rs, :] = (dkd * ws[c]).astype(BF16)
            ddec = dkd * k_ref[rs, :].astype(F32) * ws[c]
            hi, lo = _split2(ddec)
            dla_scr[rs, :] = _dot_tn(upper, hi) + _dot_tn(upper, lo) + as_[c] * da

        dz = dla_scr[...] * (1.0 - _sigmoid(z)) * (1.0 / GATE_TAU)
        dzb = dz.astype(BF16)
        dlr_ref[...] = _dot_nt(dzb, wg_ref[...]).astype(BF16)
        dwg_ref[...] += _dot_tn(lr_ref[...], dzb)
        dbg_ref[...] += jnp.sum(dz, axis=0, keepdims=True)

    def rspec(n):
        return pl.BlockSpec((GLA_ROWS, n), lambda s: (rev(s), 0))

    return pl.pallas_call(
        body, name="gla_bwd", grid=(nb,),
        in_specs=[rspec(256), rspec(256), rspec(384), rspec(384), rspec(128),
                  pl.BlockSpec((GLA_NC, 384, 256), lambda s: (rev(s), 0, 0)),
                  pl.BlockSpec((1, 384, 256), lambda s: (jnp.maximum(rev(s) * GLA_NC - 1, 0), 0, 0)),
                  rspec(384), _full_spec((128, 256)), _full_spec((1, 256)), _full_spec((1, 384))],
        out_specs=[rspec(256), rspec(256), rspec(384), rspec(384), rspec(128),
                   _full_spec((128, 256)), _full_spec((1, 256)), _full_spec((1, 384))],
        out_shape=[jax.ShapeDtypeStruct((T, 256), BF16), jax.ShapeDtypeStruct((T, 256), BF16),
                   jax.ShapeDtypeStruct((T, 384), BF16), jax.ShapeDtypeStruct((T, 384), BF16),
                   jax.ShapeDtypeStruct((T, 128), BF16),
                   jax.ShapeDtypeStruct((128, 256), F32), jax.ShapeDtypeStruct((1, 256), F32),
                   jax.ShapeDtypeStruct((1, 384), F32)],
        scratch_shapes=[pltpu.VMEM((384, 256), F32), pltpu.VMEM((1, 256), F32),
                        pltpu.VMEM((GLA_ROWS, 384), F32), pltpu.VMEM((GLA_ROWS, 384), F32),
                        pltpu.VMEM((GLA_ROWS, 256), F32), pltpu.VMEM((GLA_NC, 384, 256), F32)],
        compiler_params=_cp(("arbitrary",)),
    )(q, k, v, g, lr, states, states, dy, wg, bg, gn)


CONV_ROWS = 512
HALO = 32
SUBL = 8
CONV_SLAB = 32
PHASE_ROWS = CONV_ROWS + HALO - SUBL
FWD_SHIFT = tuple(HALO - (KCONV - 1) + j for j in range(KCONV))
BWD_SHIFT = tuple(KCONV - 1 - j for j in range(KCONV))


def _fill_phases(buf, ph):
    for f in range(1, SUBL):
        ph[f, 0:PHASE_ROWS, :] = buf[pl.ds(f, PHASE_ROWS), :]


def _tap(buf, ph, shift, r, n):
    f, base = shift % SUBL, shift - shift % SUBL
    src = buf if f == 0 else ph.at[f]
    return src[pl.ds(base + r, n), :]


def _taps_apply(w_ref, buf, ph, shifts, out):
    for r in range(0, CONV_ROWS, CONV_SLAB):
        acc = jnp.zeros((CONV_SLAB, 256), F32)
        for j in range(KCONV):
            acc = acc + w_ref[j:j + 1, :] * _tap(buf, ph, shifts[j], r, CONV_SLAB)
        out[r:r + CONV_SLAB, :] = acc


def _conv_scratch():
    return [pltpu.VMEM((CONV_ROWS + HALO, 256), F32), pltpu.VMEM((SUBL, CONV_ROWS + HALO, 256), F32),
            pltpu.VMEM((CONV_ROWS, 256), F32)]


def _conv_common(cu_ref, halo_ref, w_ref, b_ref, lg_ref, lb_ref, buf, ph, cbuf, blk, conv_ref=None):
    u = cu_ref[...].astype(F32)
    a = u[:, :256]
    sb = _sigmoid(u[:, 256:])
    uh = halo_ref[...].astype(F32)
    hh = uh[:, :256] * _sigmoid(uh[:, 256:]) * (blk > 0).astype(F32)
    buf[0:HALO, :] = hh
    buf[HALO:HALO + CONV_ROWS, :] = a * sb
    _fill_phases(buf, ph)
    if conv_ref is None:
        _taps_apply(w_ref, buf, ph, FWD_SHIFT, cbuf)
        conv = cbuf[...]
    else:
        conv = conv_ref[...]
    cc = conv + b_ref[...]
    mu = jnp.mean(cc, axis=-1, keepdims=True)
    xc = cc - mu
    rstd = lax.rsqrt(jnp.mean(xc * xc, axis=-1, keepdims=True) + EPS)
    n = xc * rstd
    yln = n * lg_ref[...] + lb_ref[...]
    return a, sb, n, rstd, yln, conv


def _conv_fwd(cu, w, b, lg, lb, jobs=()):
    T = cu.shape[0]
    nb = T // CONV_ROWS
    per = CONV_ROWS // HALO

    def body(cu_ref, halo_ref, w_ref, b_ref, lg_ref, lb_ref, y_ref, conv_ref, buf, ph, cbuf):
        _, _, _, _, yln, conv = _conv_common(cu_ref, halo_ref, w_ref, b_ref, lg_ref, lb_ref, buf, ph, cbuf,
                                             pl.program_id(0))
        y_ref[...] = (yln * _sigmoid(yln)).astype(BF16)
        conv_ref[...] = conv

    return _pcall(
        body, name="conv_fwd", grid=(nb,),
        in_specs=[_row_spec(CONV_ROWS, 512),
                  pl.BlockSpec((HALO, 512), lambda i: (jnp.maximum(i * per - 1, 0), 0)),
                  _full_spec((32, 256)), _full_spec((1, 256)), _full_spec((1, 256)), _full_spec((1, 256))],
        out_specs=[_row_spec(CONV_ROWS, 256), _row_spec(CONV_ROWS, 256)],
        out_shape=[jax.ShapeDtypeStruct((T, 256), BF16), jax.ShapeDtypeStruct((T, 256), F32)],
        scratch_shapes=_conv_scratch(),
        sem=("arbitrary",), operands=(cu, cu, w, b, lg, lb), jobs=jobs)


def _conv_bwd(cu, conv, dy, w, b, lg, lb, jobs=()):
    T = cu.shape[0]
    nb = T // CONV_ROWS
    per = CONV_ROWS // HALO

    def rev(s):
        return nb - 1 - s

    def body(cu_ref, halo_ref, conv_ref, dy_ref, w_ref, b_ref, lg_ref, lb_ref,
             dcu_ref, dw_ref, db_ref, dlg_ref, dlb_ref, buf, ph, cbuf, dcbuf, dph, carry):
        s = pl.program_id(0)

        @pl.when(s == 0)
        def _():
            carry[...] = jnp.zeros_like(carry)
            dw_ref[...] = jnp.zeros_like(dw_ref)
            db_ref[...] = jnp.zeros_like(db_ref)
            dlg_ref[...] = jnp.zeros_like(dlg_ref)
            dlb_ref[...] = jnp.zeros_like(dlb_ref)

        a, sb, n, rstd, yln, _ = _conv_common(cu_ref, halo_ref, w_ref, b_ref, lg_ref, lb_ref, buf, ph, cbuf, rev(s),
                                              conv_ref=conv_ref)
        sg = _sigmoid(yln)
        dyln = dy_ref[...].astype(F32) * (sg * (1.0 + yln * (1.0 - sg)))
        dlg_ref[...] += jnp.sum(dyln * n, axis=0, keepdims=True)
        dlb_ref[...] += jnp.sum(dyln, axis=0, keepdims=True)
        dn = dyln * lg_ref[...]
        dc = rstd * (dn - jnp.mean(dn, axis=-1, keepdims=True) - n * jnp.mean(dn * n, axis=-1, keepdims=True))
        db_ref[...] += jnp.sum(dc, axis=0, keepdims=True)
        dcbuf[0:CONV_ROWS, :] = dc
        dcbuf[CONV_ROWS:CONV_ROWS + HALO, :] = carry[...]
        carry[...] = dc[0:HALO, :]
        _fill_phases(dcbuf, dph)
        for j in range(KCONV):
            acc = jnp.zeros((SUBL, 256), F32)
            for r in range(0, CONV_ROWS, 2 * CONV_SLAB):
                prod = dcbuf[r:r + 2 * CONV_SLAB, :] * _tap(buf, ph, FWD_SHIFT[j], r, 2 * CONV_SLAB)
                acc = acc + jnp.sum(prod.reshape(2 * CONV_SLAB // SUBL, SUBL, 256), axis=0)
            dw_ref[j:j + 1, :] += jnp.sum(acc, axis=0, keepdims=True)
        _taps_apply(w_ref, dcbuf, dph, BWD_SHIFT, cbuf)
        dhg = cbuf[...]
        dcu_ref[...] = jnp.concatenate([dhg * sb, dhg * a * sb * (1.0 - sb)], axis=1).astype(BF16)

    def rspec(n):
        return pl.BlockSpec((CONV_ROWS, n), lambda s: (rev(s), 0))

    return _pcall(
        body, name="conv_bwd", grid=(nb,),
        in_specs=[rspec(512),
                  pl.BlockSpec((HALO, 512), lambda s: (jnp.maximum(rev(s) * per - 1, 0), 0)),
                  rspec(256), rspec(256),
                  _full_spec((32, 256)), _full_spec((1, 256)), _full_spec((1, 256)), _full_spec((1, 256))],
        out_specs=[rspec(512), _full_spec((32, 256)), _full_spec((1, 256)), _full_spec((1, 256)),
                   _full_spec((1, 256))],
        out_shape=[jax.ShapeDtypeStruct((T, 512), BF16), jax.ShapeDtypeStruct((32, 256), F32),
                   jax.ShapeDtypeStruct((1, 256), F32), jax.ShapeDtypeStruct((1, 256), F32),
                   jax.ShapeDtypeStruct((1, 256), F32)],
        scratch_shapes=_conv_scratch() + [pltpu.VMEM((CONV_ROWS + HALO, 256), F32),
                                          pltpu.VMEM((SUBL, CONV_ROWS + HALO, 256), F32),
                                          pltpu.VMEM((HALO, 256), F32)],
        sem=("arbitrary",), operands=(cu, cu, conv, dy, w, b, lg, lb), jobs=jobs)


def _rel_onehot_t(shift=0):
    r = lax.broadcasted_iota(jnp.int32, (384, RB_W), 0)
    n = lax.broadcasted_iota(jnp.int32, (384, RB_W), 1) - shift
    idx = jnp.clip(1024 - n, -128, 128) + 128
    return (idx == r).astype(BF16)


def _relbias_expand(rb):
    def body(rb_ref, out_ref):
        oh = _rel_onehot_t()
        hi, mid, lo = _split3(rb_ref[...])
        strip = _dot(hi, oh) + _dot(mid, oh) + _dot(lo, oh)
        qi = _group(lax.broadcasted_iota(jnp.int32, (AQ_BLK, AK_WIN), 0), CH, 4)
        kj = _group(lax.broadcasted_iota(jnp.int32, (AQ_BLK, AK_WIN), 1), CH, 12)
        valid = (kj >= qi) & (kj <= qi + 8)
        for hd in range(6):
            x = jnp.broadcast_to(strip[hd:hd + 1, :], (AQ_BLK, RB_W))
            xr = pltpu.roll(x, 0, 1, stride=1, stride_axis=0)
            out_ref[hd] = jnp.where(valid, xr[:, 512:512 + AK_WIN], NEG)

    return pl.pallas_call(
        body, name="relbias_expand",
        out_shape=jax.ShapeDtypeStruct((6, AQ_BLK, AK_WIN), F32),
        compiler_params=_cp(),
    )(rb)


def _relbias_grad(dbias):
    def body(db_ref, out_ref):
        oh = _rel_onehot_t(AQ_BLK - 1)
        ri = lax.broadcasted_iota(jnp.int32, (AQ_BLK, AQ_BLK), 0)
        ci = lax.broadcasted_iota(jnp.int32, (AQ_BLK, AQ_BLK), 1)
        flip = (ri + ci == AQ_BLK - 1).astype(BF16)
        rows = []
        for hd in range(6):
            hi, mid, lo = _split3(db_ref[hd])
            rev = _dot(flip, hi) + _dot(flip, mid) + _dot(flip, lo)
            x = jnp.concatenate([jnp.zeros((AQ_BLK, 512), F32), rev,
                                 jnp.zeros((AQ_BLK, RB_W - 512 - AK_WIN), F32)], axis=1)
            xr = pltpu.roll(x, 0, 1, stride=1, stride_axis=0)
            rows.append(jnp.sum(xr, axis=0, keepdims=True))
        rows.append(jnp.zeros((2, RB_W), F32))
        dstrip = jnp.concatenate(rows, axis=0)
        hi, mid, lo = _split3(dstrip)
        out_ref[...] = _dot_nt(hi, oh) + _dot_nt(mid, oh) + _dot_nt(lo, oh)

    return pl.pallas_call(
        body, name="relbias_grad",
        out_shape=jax.ShapeDtypeStruct((8, 384), F32),
        compiler_params=_cp(),
    )(dbias)


ATT_SLAB = 8


def _att_logits_slab(s_scr, b_ref, hd, rows, first_key):
    kvalid = lax.broadcasted_iota(jnp.int32, (ATT_SLAB, AK_WIN), 1) >= first_key
    return jnp.where(kvalid, s_scr[rows, :] + b_ref[hd, rows, :], NEG)


def _att_softmax_slab(s_scr, b_ref, hd, rows, first_key):
    s = _att_logits_slab(s_scr, b_ref, hd, rows, first_key)
    m = jnp.max(s, axis=-1, keepdims=True)
    p = jnp.exp(s - m)
    total = jnp.sum(p, axis=-1, keepdims=True)
    return p * (1.0 / total), m + jnp.log(total)


def _att_first_key(i):
    return (8 - 4 * i) * CH


def _slab_rows(t):
    return pl.ds(t * ATT_SLAB, ATT_SLAB)


def _head_lanes(hd):
    return slice(hd * 64, (hd + 1) * 64)


WIN_BLKS = AK_WIN // AQ_BLK


def _head_tile(hd):
    return slice(hd * HEAD_PAD, (hd + 1) * HEAD_PAD)


def _win_cols(d):
    return slice(d * AQ_BLK, (d + 1) * AQ_BLK)


def _win_block(i, d):
    return jnp.maximum(i + d - WIN_LEFT, 0)


def _win_specs():
    return [pl.BlockSpec((AQ_BLK, ATT_WIDE), lambda i, d=d: (_win_block(i, d), 0)) for d in range(WIN_BLKS)]


def _att_fwd(q, k, v, bias, jobs=()):
    T = q.shape[0]
    nb = T // AQ_BLK

    def body(q_ref, k0, k1, k2, v0, v1, v2, b_ref, o_ref, lse_ref, s_scr):
        k_refs, v_refs = (k0, k1, k2), (v0, v1, v2)
        first_key = _att_first_key(pl.program_id(0))
        lse_ref[...] = jnp.zeros_like(lse_ref)

        def scores(hd):
            q_h = q_ref[:, _head_tile(hd)] * A_SCALE
            for d in range(WIN_BLKS):
                s_scr[hd % 2, :, _win_cols(d)] = _dot_nt(q_h, k_refs[d][:, _head_tile(hd)])

        scores(0)
        for hd in range(ATT_HEADS):
            if hd + 1 < ATT_HEADS:
                scores(hd + 1)
            s_h = s_scr.at[hd % 2]
            for t in range(AQ_BLK // ATT_SLAB):
                rows = _slab_rows(t)
                s_h[rows, :], lse_ref[rows, hd:hd + 1] = _att_softmax_slab(s_h, b_ref, hd, rows, first_key)
            o_h = _dot(s_h[:, _win_cols(0)].astype(BF16), v_refs[0][:, _head_tile(hd)])
            for d in range(1, WIN_BLKS):
                o_h = o_h + _dot(s_h[:, _win_cols(d)].astype(BF16), v_refs[d][:, _head_tile(hd)])
            o_ref[:, _head_lanes(hd)] = o_h[:, :64].astype(BF16)

    return _pcall(
        body, name="att_fwd", grid=(nb,),
        in_specs=[_row_spec(AQ_BLK, ATT_WIDE)] + _win_specs() + _win_specs() + [_full_spec((6, AQ_BLK, AK_WIN))],
        out_specs=[_row_spec(AQ_BLK, 384), _row_spec(AQ_BLK, 128)],
        out_shape=[jax.ShapeDtypeStruct((T, 384), BF16), jax.ShapeDtypeStruct((T, 128), F32)],
        scratch_shapes=[pltpu.VMEM((2, AQ_BLK, AK_WIN), F32)],
        sem=("arbitrary",), operands=(q, k, k, k, v, v, v, bias), jobs=jobs)


def _att_bwd(q, k, v, bias, o, lse, do, jobs=()):
    T = q.shape[0]
    nb = T // AQ_BLK

    def body(q_ref, k0, k1, k2, v0, v1, v2, b_ref, o_ref, lse_ref, do_ref, dq_ref, dk_ref, dv_ref, db_ref,
             dk_acc, dv_acc, s_scr, dp_scr, delta_scr):
        k_refs, v_refs = (k0, k1, k2), (v0, v1, v2)
        i = pl.program_id(0)

        @pl.when(i == 0)
        def _():
            dk_acc[...] = jnp.zeros_like(dk_acc)
            dv_acc[...] = jnp.zeros_like(dv_acc)
            db_ref[...] = jnp.zeros_like(db_ref)

        first_key = _att_first_key(i)

        def scores(hd):
            q_h = q_ref[:, _head_tile(hd)] * A_SCALE
            do_h = do_ref[:, _head_tile(hd)]
            delta_scr[:, hd:hd + 1] = jnp.sum(do_h[:, :64].astype(F32) * o_ref[:, _head_lanes(hd)].astype(F32),
                                              axis=-1, keepdims=True)
            for d in range(WIN_BLKS):
                s_scr[hd % 2, :, _win_cols(d)] = _dot_nt(q_h, k_refs[d][:, _head_tile(hd)])
                dp_scr[hd % 2, :, _win_cols(d)] = _dot_nt(do_h, v_refs[d][:, _head_tile(hd)])

        scores(0)
        for hd in range(ATT_HEADS):
            if hd + 1 < ATT_HEADS:
                scores(hd + 1)
            s_h, dp_h = s_scr.at[hd % 2], dp_scr.at[hd % 2]
            for t in range(AQ_BLK // ATT_SLAB):
                rows = _slab_rows(t)
                p = jnp.exp(_att_logits_slab(s_h, b_ref, hd, rows, first_key) - lse_ref[rows, hd:hd + 1])
                ds = p * (dp_h[rows, :] - delta_scr[rows, hd:hd + 1])
                db_ref[hd, rows, :] += ds
                s_h[rows, :] = p
                dp_h[rows, :] = ds
            q_h = q_ref[:, _head_tile(hd)] * A_SCALE
            do_h = do_ref[:, _head_tile(hd)]
            ls = _head_lanes(hd)
            dq_h = jnp.zeros((AQ_BLK, HEAD_PAD), F32)
            for d in range(WIN_BLKS):
                pb = s_h[:, _win_cols(d)].astype(BF16)
                dsb = dp_h[:, _win_cols(d)].astype(BF16)
                rows = pl.ds(pl.multiple_of(_win_block(i, d) * AQ_BLK, AQ_BLK), AQ_BLK)
                dv_acc[rows, ls] += _dot_tn(pb, do_h)[:, :64]
                dk_acc[rows, ls] += _dot_tn(dsb, q_h)[:, :64]
                dq_h = dq_h + _dot(dsb, k_refs[d][:, _head_tile(hd)])
            dq_ref[:, ls] = (dq_h[:, :64] * A_SCALE).astype(BF16)

        @pl.when(i == nb - 1)
        def _():
            dk_ref[...] = dk_acc[...].astype(BF16)
            dv_ref[...] = dv_acc[...].astype(BF16)

    return _pcall(
        body, name="att_bwd", grid=(nb,),
        in_specs=[_row_spec(AQ_BLK, ATT_WIDE)] + _win_specs() + _win_specs()
        + [_full_spec((6, AQ_BLK, AK_WIN)), _row_spec(AQ_BLK, 384), _row_spec(AQ_BLK, 128),
           _row_spec(AQ_BLK, ATT_WIDE)],
        out_specs=[_row_spec(AQ_BLK, 384), _full_spec((T, 384)), _full_spec((T, 384)),
                   _full_spec((6, AQ_BLK, AK_WIN))],
        out_shape=[jax.ShapeDtypeStruct((T, 384), BF16), jax.ShapeDtypeStruct((T, 384), BF16),
                   jax.ShapeDtypeStruct((T, 384), BF16), jax.ShapeDtypeStruct((6, AQ_BLK, AK_WIN), F32)],
        scratch_shapes=[pltpu.VMEM((T, 384), F32), pltpu.VMEM((T, 384), F32),
                        pltpu.VMEM((2, AQ_BLK, AK_WIN), F32), pltpu.VMEM((2, AQ_BLK, AK_WIN), F32),
                        pltpu.VMEM((AQ_BLK, 128), F32)],
        sem=("arbitrary",), operands=(q, k, k, k, v, v, v, bias, o, lse, do), jobs=jobs)


FF_BLK = 512
N_FF = 4096 // FF_BLK
MLP_SHARDS = 2


def _outproj_mlp_fwd(h, o_gla, o_conv, o_att, w_out, gamma, w_up, w_down, jobs=()):
    T = h.shape[0]
    tm = 1024

    def body(h_ref, og_ref, oc_ref, oa_ref, wo_ref, g_ref, wu_ref, wd_ref, h1_ref, xt_ref, h2_ref, a_ref, acc, xn_ref):
        j = pl.program_id(1)

        @pl.when(j == 0)
        def _():
            wo = wo_ref[...]
            h1 = (h_ref[...] + _dot(og_ref[...], wo[0:384]) + _dot(oc_ref[...], wo[384:640])
                  + _dot(oa_ref[...], wo[640:1024]))
            h1_ref[...] = h1
            r = lax.rsqrt(jnp.mean(h1 * h1, axis=-1, keepdims=True) + EPS)
            xn = (h1 * r * g_ref[...]).astype(BF16)
            xn_ref[...] = xn
            xt_ref[...] = jnp.transpose(xn)
            acc[...] = h1

        xn_ = xn_ref[...]
        down = None
        for s in range(MLP_SHARDS):
            a = jnp.maximum(_dot(xn_, wu_ref[s]), 0.0)
            a_ref[:, s * FF_BLK:(s + 1) * FF_BLK] = a.astype(BF16)
            part = _dot((a * a).astype(BF16), wd_ref[s])
            down = part if down is None else down + part
        acc[...] += down

        @pl.when(j == N_FF // MLP_SHARDS - 1)
        def _():
            h2_ref[...] = acc[...]

    row = lambda n: pl.BlockSpec((tm, n), lambda i, j: (i, 0))
    return _pcall(
        body, name="outproj_mlp_fwd", grid=(T // tm, N_FF // MLP_SHARDS),
        in_specs=[row(D), row(384), row(256), row(384),
                  pl.BlockSpec((D, D), lambda i, j: (0, 0)), pl.BlockSpec((1, D), lambda i, j: (0, 0)),
                  pl.BlockSpec((MLP_SHARDS, D, FF_BLK), lambda i, j: (j, 0, 0)),
                  pl.BlockSpec((MLP_SHARDS, FF_BLK, D), lambda i, j: (j, 0, 0))],
        out_specs=[row(D), pl.BlockSpec((D, tm), lambda i, j: (0, i)), row(D),
                   pl.BlockSpec((tm, MLP_SHARDS * FF_BLK), lambda i, j: (i, j))],
        out_shape=[jax.ShapeDtypeStruct((T, D), F32), jax.ShapeDtypeStruct((D, T), BF16),
                   jax.ShapeDtypeStruct((T, D), F32), jax.ShapeDtypeStruct((T, N_FF * FF_BLK), BF16)],
        scratch_shapes=[pltpu.VMEM((tm, D), F32), pltpu.VMEM((tm, D), BF16)],
        sem=("arbitrary", "arbitrary"), operands=(h, o_gla, o_conv, o_att, w_out, gamma, w_up, w_down), jobs=jobs)


def _mlp_bwd(xn2t, act, h1, dh2, dh2b, gamma, w_up, w_down, jobs=()):
    T = act.shape[0]
    tm = 512
    nt = T // tm
    ns = MLP_SHARDS
    nj = N_FF // ns
    last = nj - 1

    def body(xt_ref, a_ref, h1_ref, dy_ref, dyb_ref, g_ref, wu_ref, wd_ref, dh1_ref, dwu_ref, dwd_ref, dg_ref,
             dxn_acc, acc_u, acc_d):
        j = pl.program_id(0)
        i = pl.program_id(1)
        xt = xt_ref[...]
        dyb = dyb_ref[...]
        rows = pl.ds(pl.multiple_of(i * tm, tm), tm)

        @pl.when(i == 0)
        def _():
            acc_u[...] = jnp.zeros_like(acc_u)
            acc_d[...] = jnp.zeros_like(acc_d)

        @pl.when(j == 0)
        def _():
            dxn_acc[rows, :] = jnp.zeros((tm, D), F32)

        dxn = None
        for s in range(ns):
            a = a_ref[:, s * FF_BLK:(s + 1) * FF_BLK].astype(F32)
            hh = (a * a).astype(BF16)
            du = (_dot_nt(dyb, wd_ref[s]) * (2.0 * a)).astype(BF16)
            acc_d[s] += _dot_tn(hh, dyb)
            acc_u[s] += _dot(xt, du)
            part = _dot_nt(du, wu_ref[s])
            dxn = part if dxn is None else dxn + part
        dxn_acc[rows, :] += dxn

        @pl.when(i == nt - 1)
        def _():
            for s in range(ns):
                dwu_ref[s, 0] = acc_u[s].astype(BF16)
                dwd_ref[s, 0] = acc_d[s].astype(BF16)

        @pl.when(j == last)
        def _():
            @pl.when(i == 0)
            def _():
                dg_ref[...] = jnp.zeros_like(dg_ref)

            h1 = h1_ref[...]
            r = lax.rsqrt(jnp.mean(h1 * h1, axis=-1, keepdims=True) + EPS)
            dx, dgam = _rms_bwd(dxn_acc[rows, :], h1, r, g_ref[...])
            dh1_ref[...] = dy_ref[...] + dx
            dg_ref[...] += dgam

    assert ns == 2
    late = lambda j, i: (jnp.where(j == last, i, 0), 0)
    return _pcall(
        body, name="mlp_bwd", grid=(nj, nt),
        in_specs=[pl.BlockSpec((D, tm), lambda j, i: (0, i)), pl.BlockSpec((tm, ns * FF_BLK), lambda j, i: (i, j)),
                  pl.BlockSpec((tm, D), late), pl.BlockSpec((tm, D), late),
                  pl.BlockSpec((tm, D), lambda j, i: (i, 0)), pl.BlockSpec((1, D), lambda j, i: (0, 0)),
                  pl.BlockSpec((ns, D, FF_BLK), lambda j, i: (j, 0, 0), pipeline_mode=pl.Buffered(1)),
                  pl.BlockSpec((ns, FF_BLK, D), lambda j, i: (j, 0, 0), pipeline_mode=pl.Buffered(1))],
        out_specs=[pl.BlockSpec((tm, D), late),
                   pl.BlockSpec((ns, 1, D, FF_BLK), lambda j, i: (0, j, 0, 0)),
                   pl.BlockSpec((ns, 1, FF_BLK, D), lambda j, i: (0, j, 0, 0)),
                   pl.BlockSpec((1, D), lambda j, i: (0, 0))],
        out_shape=[jax.ShapeDtypeStruct((T, D), F32), jax.ShapeDtypeStruct((2, 4, D, FF_BLK), BF16),
                   jax.ShapeDtypeStruct((2, 4, FF_BLK, D), BF16), jax.ShapeDtypeStruct((1, D), F32)],
        scratch_shapes=[pltpu.VMEM((T, D), F32), pltpu.VMEM((ns, D, FF_BLK), F32), pltpu.VMEM((ns, FF_BLK, D), F32)],
        sem=("arbitrary", "arbitrary"), operands=(xn2t, act, h1, dh2, dh2b, gamma, w_up, w_down), jobs=jobs)


def _outproj_bwd(dh1, o_gla, o_conv, o_att, w_out, jobs=()):
    T = dh1.shape[0]
    tm = 512
    nt = T // tm

    def body(dy_ref, og_ref, oc_ref, oa_ref, wo_ref, dg_ref, dc_ref, da_ref, dw_ref, acc):
        i = pl.program_id(0)

        @pl.when(i == 0)
        def _():
            acc[...] = jnp.zeros_like(acc)

        dyb = dy_ref[...].astype(BF16)
        dm = _dot_nt(dyb, wo_ref[...])
        dg_ref[...] = dm[:, 0:384].astype(BF16)
        dc_ref[...] = dm[:, 384:640].astype(BF16)
        _store_head_padded(da_ref, dm[:, 640:1024].astype(BF16))
        mixed = jnp.concatenate([og_ref[...], oc_ref[...], oa_ref[...]], axis=1)
        acc[...] += _dot_tn(mixed, dyb)

        @pl.when(i == nt - 1)
        def _():
            for j in range(N_DEV):
                dw_ref[j % 2, j // 2] = acc[j * 128:(j + 1) * 128, :].astype(BF16)

    return _pcall(
        body, name="outproj_bwd", grid=(nt,),
        in_specs=[_row_spec(tm, D), _row_spec(tm, 384), _row_spec(tm, 256), _row_spec(tm, 384),
                  _full_spec((D, D))],
        out_specs=[_row_spec(tm, 384), _row_spec(tm, 256), _row_spec(tm, ATT_WIDE), _full_spec((2, 4, 128, D))],
        out_shape=[jax.ShapeDtypeStruct((T, 384), BF16), jax.ShapeDtypeStruct((T, 256), BF16),
                   jax.ShapeDtypeStruct((T, ATT_WIDE), BF16), jax.ShapeDtypeStruct((2, 4, 128, D), BF16)],
        scratch_shapes=[pltpu.VMEM((D, D), F32)],
        sem=("arbitrary",), operands=(dh1, o_gla, o_conv, o_att, w_out), jobs=jobs)


def _loss_fwd_bwd(h, gamma, target):
    T = h.shape[0]
    tm = 512

    def body(h_ref, g_ref, t_ref, loss_ref, dh_ref, dhb_ref, dg_ref):
        @pl.when(pl.program_id(0) == 0)
        def _():
            loss_ref[...] = jnp.zeros_like(loss_ref)
            dg_ref[...] = jnp.zeros_like(dg_ref)

        x = h_ref[...]
        r = lax.rsqrt(jnp.mean(x * x, axis=-1, keepdims=True) + EPS)
        gamma_ = g_ref[...]
        e = x * r * gamma_ - t_ref[...]
        loss_ref[...] += 0.5 * jnp.sum(jnp.mean(e * e, axis=-1, keepdims=True), axis=0, keepdims=True)
        dx, dgam = _rms_bwd(e * (1.0 / D), x, r, gamma_)
        dh_ref[...] = dx
        dhb_ref[...] = dx.astype(BF16)
        dg_ref[...] += dgam

    return pl.pallas_call(
        body, name="loss_fwd_bwd", grid=(T // tm,),
        in_specs=[_row_spec(tm, D), _full_spec((1, D)), _row_spec(tm, D)],
        out_specs=[_full_spec((8, 128)), _row_spec(tm, D), _row_spec(tm, D), _full_spec((1, D))],
        out_shape=[jax.ShapeDtypeStruct((8, 128), F32), jax.ShapeDtypeStruct((T, D), F32),
                   jax.ShapeDtypeStruct((T, D), BF16), jax.ShapeDtypeStruct((1, D), F32)],
        compiler_params=_cp(("arbitrary",)),
    )(h, gamma, target)


def _adamw_math(w, g, m, v):
    m = ADAM_B1 * m + (1.0 - ADAM_B1) * g
    v = ADAM_B2 * v + (1.0 - ADAM_B2) * (g * g)
    m_hat = m / (1.0 - ADAM_B1 ** ADAM_STEP)
    v_hat = v / (1.0 - ADAM_B2 ** ADAM_STEP)
    delta = -ADAM_LR * (m_hat / (jnp.sqrt(v_hat) + ADAM_EPS) + ADAM_WD * w)
    return delta, m, v


def _rs_adamw(a_own, r2, w, m, v, layer, chip_idx, rows_blk, prev=None):
    _, R, C = w.shape
    nblk = R // rows_blk

    def body(chip_ref, a_ref, r_ref, w_ref, m_ref, v_ref, *rest):
        g_out, d_out, m_out, v_out = rest[-4:]
        g = (a_ref[0].astype(F32) + r_ref[0].astype(F32)) + (r_ref[1].astype(F32) + r_ref[2].astype(F32))
        delta, m_new, v_new = _adamw_math(w_ref[0], g, m_ref[0], v_ref[0])
        g_out[0] = g
        d_out[0] = delta
        m_out[0] = m_new
        v_out[0] = v_new

    blk = pl.BlockSpec((1, rows_blk, C), lambda i, chip: (layer, i, 0))
    n_prev = 0 if prev is None else 4
    grid_spec = pltpu.PrefetchScalarGridSpec(
        num_scalar_prefetch=1, grid=(nblk,),
        in_specs=[pl.BlockSpec((1, rows_blk, C), lambda i, chip: (chip[0], i, 0)),
                  pl.BlockSpec((3, rows_blk, C), lambda i, chip: (0, i, 0)), blk, blk, blk]
        + [_any_spec()] * n_prev,
        out_specs=[blk, blk, blk, blk])
    return pl.pallas_call(
        body, name="rs_adamw", grid_spec=grid_spec,
        out_shape=[jax.ShapeDtypeStruct((DEPTH, R, C), F32)] * 4,
        input_output_aliases={6 + t: t for t in range(n_prev)},
        compiler_params=_cp(("arbitrary",)),
    )(chip_idx, a_own, r2, w, m, v, *(prev or ()))


def _pair_sum(g, r1, core_idx, rows_blk):
    _, _, R, C = g.shape
    nblk = R // rows_blk

    def body(core_ref, g_ref, r_ref, o_ref):
        o_ref[...] = (g_ref[0].astype(F32) + r_ref[...].astype(F32)).astype(BF16)

    grid_spec = pltpu.PrefetchScalarGridSpec(
        num_scalar_prefetch=1, grid=(4, nblk),
        in_specs=[pl.BlockSpec((1, 1, rows_blk, C), lambda k, i, core: (core[0], k, i, 0)),
                  pl.BlockSpec((1, rows_blk, C), lambda k, i, core: (k, i, 0))],
        out_specs=pl.BlockSpec((1, rows_blk, C), lambda k, i, core: (k, i, 0)))
    return pl.pallas_call(
        body, name="rs_pair_sum", grid_spec=grid_spec,
        out_shape=jax.ShapeDtypeStruct((4, R, C), BF16),
        compiler_params=_cp(("arbitrary", "arbitrary")),
    )(core_idx, g, r1)


def _small_sum(gathered):
    def body(g_ref, o_ref):
        acc = g_ref[0]
        for d in range(1, N_DEV):
            acc = acc + g_ref[d]
        o_ref[...] = acc

    return pl.pallas_call(
        body, name="small_sum",
        out_shape=jax.ShapeDtypeStruct(gathered.shape[1:], F32),
        compiler_params=_cp(),
    )(gathered)


def _adamw_small(ws, gs, ms, vs):
    n = len(ws)

    def body(*refs):
        w_r, g_r, m_r, v_r = refs[0:n], refs[n:2 * n], refs[2 * n:3 * n], refs[3 * n:4 * n]
        d_o, m_o, v_o = refs[4 * n:5 * n], refs[5 * n:6 * n], refs[6 * n:7 * n]
        for t in range(n):
            delta, m_new, v_new = _adamw_math(w_r[t][...], g_r[t][...], m_r[t][...], v_r[t][...])
            d_o[t][...] = delta
            m_o[t][...] = m_new
            v_o[t][...] = v_new

    shapes = [jax.ShapeDtypeStruct(w.shape, F32) for w in ws]
    outs = pl.pallas_call(
        body, name="adamw_small", out_shape=shapes * 3, compiler_params=_cp(),
    )(*ws, *gs, *ms, *vs)
    return outs[0:n], outs[n:2 * n], outs[2 * n:3 * n]


def _mesh_pos():
    return lax.axis_index("x"), lax.axis_index("y"), lax.axis_index("c")


def _peers():
    x, y, c = _mesh_pos()
    return (x, y, c), (x, y, 1 - c), [(1 - x, y), (x, 1 - y), (1 - x, 1 - y)]


def _slot(ref, pos):
    return ref.at[4 * pos[0] + 2 * pos[1] + pos[2]]


def _remote(src, dst, send_sem, recv_sem, to):
    return pltpu.make_async_remote_copy(src_ref=src, dst_ref=dst, send_sem=send_sem, recv_sem=recv_sem,
                                        device_id=to, device_id_type=MESH)


def _ag_spread(shards):
    n = len(shards)

    def copies(ins, outs, sems):
        send, recv, loc = sems
        me, sibling, chips = _peers()
        peers = [sibling] + [(*chip, me[2]) for chip in chips]
        local = [pltpu.make_async_copy(ins[a], _slot(outs[a], me), loc.at[a]) for a in range(n)]
        sends = [_remote(ins[a], _slot(outs[a], me), send.at[a, k], recv.at[a, k], p)
                 for a in range(n) for k, p in enumerate(peers)]
        recvs = [_remote(ins[a], _slot(outs[a], p), send.at[a, k], recv.at[a, k], p)
                 for a in range(n) for k, p in enumerate(peers)]
        return local, sends, recvs

    def start(ins, outs, sems):
        local, sends, _ = copies(ins, outs, sems)
        for cp in local + sends:
            cp.start()

    def finish(ins, outs, sems):
        local, sends, recvs = copies(ins, outs, sems)
        for cp in sends:
            cp.wait_send()
        for cp in recvs:
            cp.wait_recv()
        for cp in local:
            cp.wait()

    return _Job(shards, [jax.ShapeDtypeStruct((N_DEV,) + a.shape, a.dtype) for a in shards],
                [pltpu.SemaphoreType.DMA((n, 4)), pltpu.SemaphoreType.DMA((n, 4)), pltpu.SemaphoreType.DMA((n,))],
                start, finish)


def _ag_pass(stacks):
    n = len(stacks)

    def copies(ins, outs, sems):
        send, recv = sems
        me, sibling, chips = _peers()
        sends = [_remote(_slot(ins[a], (*chip, me[2])), _slot(outs[a], (*chip, me[2])), send.at[a, j], recv.at[a, j],
                         sibling) for a in range(n) for j, chip in enumerate(chips)]
        recvs = [_remote(_slot(ins[a], (*chip, me[2])), _slot(outs[a], (*chip, 1 - me[2])), send.at[a, j],
                         recv.at[a, j], sibling) for a in range(n) for j, chip in enumerate(chips)]
        return sends, recvs

    def start(ins, outs, sems):
        for cp in copies(ins, outs, sems)[0]:
            cp.start()

    def finish(ins, outs, sems):
        sends, recvs = copies(ins, outs, sems)
        for cp in sends:
            cp.wait_send()
        for cp in recvs:
            cp.wait_recv()

    return _Job(stacks, [jax.ShapeDtypeStruct(a.shape, a.dtype) for a in stacks],
                [pltpu.SemaphoreType.DMA((n, 3)), pltpu.SemaphoreType.DMA((n, 3))],
                start, finish, aliases={a: a for a in range(n)})


def _ag_both(shards):
    spread = _ag_spread(shards)
    fake = [jax.ShapeDtypeStruct((N_DEV,) + a.shape, a.dtype) for a in shards]
    onward = _ag_pass(fake)
    n_sp = len(spread.sems)

    def start(ins, outs, sems):
        spread.start(ins, outs, sems[:n_sp])

    def finish(ins, outs, sems):
        spread.finish(ins, outs, sems[:n_sp])
        onward.start(outs, outs, sems[n_sp:])
        onward.finish(outs, outs, sems[n_sp:])

    return _Job(shards, spread.out_shapes, spread.sems + onward.sems, start, finish)


def _rs_swap(parts):
    n = len(parts)

    def copies(ins, outs, sems):
        send, recv = sems
        me, sibling, _ = _peers()
        return [_remote(ins[a].at[1 - me[2]], outs[a], send.at[a], recv.at[a], sibling) for a in range(n)]

    def start(ins, outs, sems):
        for cp in copies(ins, outs, sems):
            cp.start()

    def finish(ins, outs, sems):
        for cp in copies(ins, outs, sems):
            cp.wait()

    return _Job(parts, [jax.ShapeDtypeStruct(a.shape[1:], a.dtype) for a in parts],
                [pltpu.SemaphoreType.DMA((n,)), pltpu.SemaphoreType.DMA((n,))], start, finish)


def _rs_ici(pairs):
    n = len(pairs)

    def copies(ins, outs, sems):
        send, recv = sems
        me, _, chips = _peers()
        return [_remote(ins[a].at[2 * chip[0] + chip[1]], outs[a].at[j], send.at[a, j], recv.at[a, j],
                        (*chip, me[2])) for a in range(n) for j, chip in enumerate(chips)]

    def start(ins, outs, sems):
        for cp in copies(ins, outs, sems):
            cp.start()

    def finish(ins, outs, sems):
        for cp in copies(ins, outs, sems):
            cp.wait()

    return _Job(pairs, [jax.ShapeDtypeStruct((3,) + a.shape[1:], a.dtype) for a in pairs],
                [pltpu.SemaphoreType.DMA((n, 3)), pltpu.SemaphoreType.DMA((n, 3))], start, finish)


def _comm_call(jobs, name):
    def body():
        pass

    return _pcall(body, name=name, grid=(), in_specs=[], out_specs=[], out_shape=[], operands=(), jobs=jobs)[1]


def _allgather(arrs, name):
    n = len(arrs)

    def body(*refs):
        ins, outs = refs[:n], refs[n:2 * n]
        send_sems, recv_sems, local_sems = refs[2 * n:]
        x, y, c = _mesh_pos()
        me, sibling = (x, y, c), (x, y, 1 - c)
        chips = [(1 - x, y), (x, 1 - y), (1 - x, 1 - y)]

        def slot(a, pos):
            return outs[a].at[4 * pos[0] + 2 * pos[1] + pos[2]]

        def copy(a, k, block, to, src=None):
            return pltpu.make_async_remote_copy(
                src_ref=slot(a, block) if src is None else src, dst_ref=slot(a, block),
                send_sem=send_sems.at[a, k], recv_sem=recv_sems.at[a, k],
                device_id=to, device_id_type=MESH)

        mine = [pltpu.make_async_copy(ins[a], slot(a, me), local_sems.at[a]) for a in range(n)]
        for cp in mine:
            cp.start()
        first = []
        for a in range(n):
            first.append(copy(a, 0, me, sibling, src=ins[a]))
            first += [copy(a, 1 + j, me, (*chip, c), src=ins[a]) for j, chip in enumerate(chips)]
        for cp in first:
            cp.start()
        passed = []
        for j, chip in enumerate(chips):
            for a in range(n):
                copy(a, 1 + j, (*chip, c), me).wait_recv()
                fwd = copy(a, 4 + j, (*chip, c), sibling)
                fwd.start()
                passed.append(fwd)
        for a in range(n):
            copy(a, 0, sibling, me).wait_recv()
            for j, chip in enumerate(chips):
                copy(a, 4 + j, (*chip, 1 - c), me).wait_recv()
        for cp in first + passed:
            cp.wait_send()
        for cp in mine:
            cp.wait()

    return pl.pallas_call(
        body, name=name,
        in_specs=[_any_spec()] * n, out_specs=[_any_spec()] * n,
        out_shape=[jax.ShapeDtypeStruct((N_DEV,) + a.shape, a.dtype) for a in arrs],
        scratch_shapes=[pltpu.SemaphoreType.DMA((n, 7)), pltpu.SemaphoreType.DMA((n, 7)),
                        pltpu.SemaphoreType.DMA((n,))],
        compiler_params=_cp(),
    )(*arrs)


W_IN_SHARD = 354
W_IN_COLS = ((0, 192, OQ), (192, 192, OKK), (384, 384, OV), (768, 384, OG), (1152, 16, OLR), (1168, 512, OCU),
             (1680, 384, OAQ), (2064, 384, OAK), (2448, 384, OAV))


def _w_in_padded(stack):
    new_to_ref = {new: (start, width) for start, width, new in W_IN_COLS}
    cols = []
    for new, padded in IN_GROUPS:
        start, width = new_to_ref[new]
        a = start
        while a < start + width:
            j = a // W_IN_SHARD
            b = min(start + width, (j + 1) * W_IN_SHARD)
            cols.append(stack[j][:, a - j * W_IN_SHARD:b - j * W_IN_SHARD])
            a = b
        if padded > width:
            cols.append(jnp.zeros((stack.shape[1], padded - width), stack.dtype))
    return jnp.concatenate(cols, axis=1)


def _dw_in_shards(dw):
    shards = []
    for j in range(N_DEV):
        lo, hi = j * W_IN_SHARD, (j + 1) * W_IN_SHARD
        segs = []
        for start, width, new in W_IN_COLS:
            a, b = max(lo, start), min(hi, start + width)
            if a < b:
                segs.append(dw[:, new + a - start:new + b - start])
        shards.append(jnp.concatenate(segs, axis=1))
    return jnp.stack([jnp.stack([shards[2 * chip + core] for chip in range(4)]) for core in range(2)])


def _pad_to(a, shape):
    return jnp.pad(a, [(0, s - d) for d, s in zip(a.shape, shape)])


SMALL_LAYOUT = (
    ("norm_mix", 2, 1024), ("norm_ffn", 2, 1024), ("norm_final", 1, 1024), ("gla_norm", 2, 384),
    ("b_gla_gate", 2, 192), ("b_dw", 2, 256), ("conv_ln_g", 2, 256), ("conv_ln_b", 2, 256),
    ("rel_bias", 12, 257), ("w_gla_gate", 32, 192), ("w_dw", 62, 256),
)
SMALL_LANES = 128
SMALL_TILE = 8 * SMALL_LANES


def _small_tile_rows(r, lanes):
    return -(-(r * lanes) // SMALL_TILE) * 8


SMALL_ROWS = sum(_small_tile_rows(r, lanes) for _, r, lanes in SMALL_LAYOUT)


def _pack_small(parts):
    tiles = []
    for name, r, lanes in SMALL_LAYOUT:
        rows = _small_tile_rows(r, lanes)
        flat = _pad_to(parts[name].reshape(r * lanes), (rows * SMALL_LANES,))
        tiles.append(flat.reshape(rows, SMALL_LANES))
    return jnp.concatenate(tiles, axis=0)


def _unpack_small(packed):
    out, r0 = {}, 0
    for name, r, lanes in SMALL_LAYOUT:
        rows = _small_tile_rows(r, lanes)
        out[name] = packed[r0:r0 + rows].reshape(rows * SMALL_LANES)[:r * lanes].reshape(r, lanes)
        r0 += rows
    return out


def _mixers_fwd(h, wl, w_in_p, plan=None):
    plan, res = plan or {}, {}

    def jobs(host):
        return plan[host](res) if host in plan else ()

    (q, k, v, g, cu, aq, ak, av, lr), res["inproj"] = _inproj_fwd(h, wl["norm_mix"], w_in_p, jobs=jobs("inproj"))
    bias = _relbias_expand(wl["rb"])
    (o_att, lse), res["att"] = _att_fwd(aq, ak, av, bias, jobs=jobs("att"))
    (o_gla, states), res["gla"] = _gla_fwd(q, k, v, g, lr, wl["wg"], wl["bg"], wl["gn"], jobs=jobs("gla"))
    (o_conv, conv), res["conv"] = _conv_fwd(cu, wl["w_dw"], wl["b_dw"], wl["ln_g"], wl["ln_b"], jobs=jobs("conv"))
    sv = dict(h=h, w_in=w_in_p, q=q, k=k, v=v, g=g, cu=cu, aq=aq, ak=ak, av=av, lr=lr,
              o_gla=o_gla, o_conv=o_conv, conv=conv, o_att=o_att, lse=lse, states=states, bias=bias)
    return sv, res


def _mixers_bwd(sv, wl, dh1, d_ogla, d_oconv, att_grads, conv_jobs=(), x_jobs_fn=None):
    daq, dak, dav, dbias = att_grads
    d_rb = _relbias_grad(dbias)
    (dcu, dw_dw, db_dw, dln_g, dln_b), conv_res = _conv_bwd(
        sv["cu"], sv["conv"], d_oconv, wl["w_dw"], wl["b_dw"], wl["ln_g"], wl["ln_b"], jobs=conv_jobs)
    dq, dk, dv, dg, dlr, dwg, dbg, dgn = _gla_bwd(sv["q"], sv["k"], sv["v"], sv["g"], sv["lr"], sv["states"],
                                                  d_ogla, wl["wg"], wl["bg"], wl["gn"])
    dparts = (dq, dk, dv, dg, dcu, daq, dak, dav, dlr)
    dw_in = _inproj_bwd_w(sv["h"], wl["norm_mix"], dparts)
    x_jobs = x_jobs_fn(dw_in) if x_jobs_fn is not None else ()
    (dh, dhb, d_nmix), x_res = _inproj_bwd_x(sv["h"], dh1, wl["norm_mix"], sv["w_in"], dparts, jobs=x_jobs)
    small = dict(norm_mix=d_nmix, wg=dwg, bg=dbg, gn=dgn, w_dw=dw_dw, b_dw=db_dw, ln_g=dln_g, ln_b=dln_b, rb=d_rb)
    return (dh, dhb), dw_in, small, conv_res, x_res


def _layer_small(l, w_dw_full, norm_mix, w_gla_gate, b_gla_gate, gla_norm, b_dw, conv_ln_g, conv_ln_b, rel_bias,
                 norm_ffn):
    return dict(
        norm_mix=norm_mix[l][None, :], norm_ffn=norm_ffn[l][None, :],
        wg=_pad_to(w_gla_gate[l], (128, 256)).astype(BF16), bg=_pad_to(b_gla_gate[l][None, :], (1, 256)),
        gn=gla_norm[l][None, :], w_dw=_pad_to(w_dw_full, (32, 256)), b_dw=b_dw[l][None, :],
        ln_g=conv_ln_g[l][None, :], ln_b=conv_ln_b[l][None, :], rb=_pad_to(rel_bias[l], (8, 384)))


RS_ROWS = dict(w_in=512, w_out=128, w_up=512, w_down=256)


def kernel(x, norm_mix, w_in, w_gla_gate, b_gla_gate, gla_norm, w_dw, b_dw, conv_ln_g, conv_ln_b, rel_bias, w_out, norm_ffn, w_up, w_down, norm_final, loss_target, m_norm_mix, m_w_in, m_w_gla_gate, m_b_gla_gate, m_gla_norm, m_w_dw, m_b_dw, m_conv_ln_g, m_conv_ln_b, m_rel_bias, m_w_out, m_norm_ffn, m_w_up, m_w_down, m_norm_final, v_norm_mix, v_w_in, v_w_gla_gate, v_b_gla_gate, v_gla_norm, v_w_dw, v_b_dw, v_conv_ln_g, v_conv_ln_b, v_rel_bias, v_w_out, v_norm_ffn, v_w_up, v_w_down, v_norm_final):
    mx, my, mc = _mesh_pos()
    me = 4 * mx + 2 * my + mc
    chip_idx = (2 * mx + my).astype(jnp.int32).reshape(1)
    core_idx = mc.astype(jnp.int32).reshape(1)
    x0, target = x[0], loss_target[0]

    def pair_sums(parts, r1):
        return [_pair_sum(p, r, core_idx, p.shape[2]) for p, r in zip(parts, r1)]

    sh = [dict(w_in=w_in[l].astype(BF16), w_out=w_out[l].astype(BF16), w_up=w_up[l].astype(BF16),
               w_down=w_down[l].astype(BF16)) for l in range(DEPTH)]
    dw_flat = _pad_to(w_dw, (DEPTH, 32, 32)).reshape(16, 128)
    st_in0, st_dw = _allgather([sh[0]["w_in"], dw_flat], "allgather_first")
    dw_all = st_dw.reshape(N_DEV, DEPTH, 32, 32)[:, :, :KCONV, :]
    dw_all = jnp.transpose(dw_all, (1, 2, 0, 3)).reshape(DEPTH, KCONV, 256)
    wl = [_layer_small(l, dw_all[l], norm_mix, w_gla_gate, b_gla_gate, gla_norm, b_dw, conv_ln_g, conv_ln_b,
                       rel_bias, norm_ffn) for l in range(DEPTH)]

    s0, s1 = sh[0], sh[1]
    half = s0["w_down"].shape[0] // 2
    down0_a, down0_b = s0["w_down"][:half], s0["w_down"][half:]
    sv0, g0 = _mixers_fwd(x0, wl[0], _w_in_padded(st_in0), plan=dict(
        inproj=lambda r: [_ag_spread([s0["w_out"], down0_a])],
        att=lambda r: [_ag_spread([s0["w_up"]]), _ag_pass(r["inproj"])],
        gla=lambda r: [_ag_spread([down0_b]), _ag_pass(r["att"][:1])],
        conv=lambda r: [_ag_pass(r["gla"][:1])]))
    st_out0, st_down0_a = g0["att"][1:]
    st_up0, st_down0_b = g0["gla"][1], g0["conv"][0]
    st_down0 = jnp.concatenate([st_down0_a, st_down0_b], axis=1)
    wo0 = st_out0.reshape(D, D)
    (h1_0, xn2t_0, h2_0, act_0), (st_in1, out1_half) = _outproj_mlp_fwd(
        x0, sv0["o_gla"], sv0["o_conv"], sv0["o_att"], wo0, wl[0]["norm_ffn"], st_up0, st_down0,
        jobs=[_ag_both([s1["w_in"]]), _ag_spread([s1["w_out"]])])

    sv1, g1 = _mixers_fwd(h2_0, wl[1], _w_in_padded(st_in1), plan=dict(
        inproj=lambda r: [_ag_spread([s1["w_up"]]), _ag_pass([out1_half])],
        att=lambda r: [_ag_spread([s1["w_down"]]), _ag_pass(r["inproj"][:1])],
        gla=lambda r: [_ag_pass(r["att"][:1])]))
    st_out1, st_up1, st_down1 = g1["inproj"][1], g1["att"][1], g1["gla"][0]
    wo1 = st_out1.reshape(D, D)
    (h1_1, xn2t_1, h2_1, act_1), _ = _outproj_mlp_fwd(
        h2_0, sv1["o_gla"], sv1["o_conv"], sv1["o_att"], wo1, wl[1]["norm_ffn"], st_up1, st_down1)

    loss8, dh, dhb, d_nf = _loss_fwd_bwd(h2_1, norm_final[None, :], target)
    loss = lax.psum(loss8[0, 0], ("x", "y", "c"))

    def layer_bwd(dh_pair, sv, wl_l, xn2t, act, h1, wo, st_up, st_down, mlp_jobs, x_jobs_fn):
        (dh1, dw_up, dw_down, d_nffn), mlp_res = _mlp_bwd(xn2t, act, h1, dh_pair[0], dh_pair[1], wl_l["norm_ffn"],
                                                           st_up, st_down, jobs=mlp_jobs)
        ud = [dw_up, dw_down]
        (d_ogla, d_oconv, d_oatt, dw_out), r1 = _outproj_bwd(
            dh1, sv["o_gla"], sv["o_conv"], sv["o_att"], wo, jobs=[_rs_swap(ud)])
        pair_ud = pair_sums(ud, r1)
        att_grads, r = _att_bwd(sv["aq"], sv["ak"], sv["av"], sv["bias"], sv["o_att"], sv["lse"], d_oatt,
                                jobs=[_rs_ici(pair_ud), _rs_swap([dw_out])])
        r2_ud, r1_out = r[:2], r[2:]
        pair_out = pair_sums([dw_out], r1_out)
        dh_in, _, small, r2_out, x_res = _mixers_bwd(sv, wl_l, dh1, d_ogla, d_oconv, att_grads,
                                                     conv_jobs=[_rs_ici(pair_out)], x_jobs_fn=x_jobs_fn)
        small["norm_ffn"] = d_nffn
        sums = dict(w_out=(pair_out[0], r2_out[0]), w_up=(pair_ud[0], r2_ud[0]), w_down=(pair_ud[1], r2_ud[1]))
        return dh_in, small, sums, mlp_res, x_res

    stash = {}

    def swap_w_in(dw_in):
        stash["in1"] = [_dw_in_shards(dw_in)]
        return [_rs_swap(stash["in1"])]

    dh_pair, small1, sums1, _, r1_in1 = layer_bwd((dh, dhb), sv1, wl[1], xn2t_1, act_1, h1_1, wo1, st_up1, st_down1,
                                                  (), swap_w_in)
    pair_in1 = pair_sums(stash["in1"], r1_in1)

    def send_w_in(dw_in):
        in0 = [_dw_in_shards(dw_in)]
        stash["pair_in0"] = pair_sums(in0, _comm_call([_rs_swap(in0)], "rs_swap_w_in_0"))
        return [_rs_ici(stash["pair_in0"])]

    (dx, _), small0, sums0, r2_in1, r2_in0 = layer_bwd(dh_pair, sv0, wl[0], xn2t_0, act_0, h1_0, wo0, st_up0, st_down0,
                                                       [_rs_ici(pair_in1)], send_w_in)
    sums1["w_in"] = (pair_in1[0], r2_in1[0])
    sums0["w_in"] = (stash["pair_in0"][0], r2_in0[0])

    big_w = dict(w_in=(w_in, m_w_in, v_w_in), w_out=(w_out, m_w_out, v_w_out), w_up=(w_up, m_w_up, v_w_up),
                 w_down=(w_down, m_w_down, v_w_down))
    pairs = {1: sums1, 0: sums0}
    big_out = {}
    for name, (w_, m_, v_) in big_w.items():
        res = None
        for l in (1, 0):
            a_own, r2_ = pairs[l][name]
            res = _rs_adamw(a_own, r2_, w_, m_, v_, l, chip_idx, RS_ROWS[name], prev=res)
        big_out[name] = res

    grads = (small0, small1)
    parts = dict(
        norm_mix=jnp.concatenate([grads[l]["norm_mix"] for l in range(DEPTH)], axis=0),
        norm_ffn=jnp.concatenate([grads[l]["norm_ffn"] for l in range(DEPTH)], axis=0),
        norm_final=d_nf,
        gla_norm=jnp.concatenate([grads[l]["gn"] for l in range(DEPTH)], axis=0),
        b_gla_gate=jnp.concatenate([grads[l]["bg"][:, :192] for l in range(DEPTH)], axis=0),
        b_dw=jnp.concatenate([grads[l]["b_dw"] for l in range(DEPTH)], axis=0),
        conv_ln_g=jnp.concatenate([grads[l]["ln_g"] for l in range(DEPTH)], axis=0),
        conv_ln_b=jnp.concatenate([grads[l]["ln_b"] for l in range(DEPTH)], axis=0),
        rel_bias=jnp.concatenate([grads[l]["rb"][:6, :N_REL] for l in range(DEPTH)], axis=0),
        w_gla_gate=jnp.concatenate([grads[l]["wg"][:16, :192] for l in range(DEPTH)], axis=0),
        w_dw=jnp.concatenate([grads[l]["w_dw"][:KCONV] for l in range(DEPTH)], axis=0),
    )
    small_all = _allgather([_pack_small(parts)], "allgather_small")[0]
    sg = _unpack_small(_small_sum(small_all))
    dw_grad = lax.dynamic_slice_in_dim(sg["w_dw"].reshape(DEPTH, KCONV, 256), me * 32, 32, axis=2)
    small_g = dict(
        norm_mix=sg["norm_mix"], w_gla_gate=sg["w_gla_gate"].reshape(DEPTH, 16, 192), b_gla_gate=sg["b_gla_gate"],
        gla_norm=sg["gla_norm"], w_dw=dw_grad, b_dw=sg["b_dw"], conv_ln_g=sg["conv_ln_g"],
        conv_ln_b=sg["conv_ln_b"], rel_bias=sg["rel_bias"].reshape(DEPTH, 6, N_REL), norm_ffn=sg["norm_ffn"],
        norm_final=sg["norm_final"].reshape(D))
    small_names = ("norm_mix", "w_gla_gate", "b_gla_gate", "gla_norm", "w_dw", "b_dw", "conv_ln_g", "conv_ln_b",
                   "rel_bias", "norm_ffn", "norm_final")
    small_w = dict(norm_mix=norm_mix, w_gla_gate=w_gla_gate, b_gla_gate=b_gla_gate, gla_norm=gla_norm, w_dw=w_dw,
                   b_dw=b_dw, conv_ln_g=conv_ln_g, conv_ln_b=conv_ln_b, rel_bias=rel_bias, norm_ffn=norm_ffn,
                   norm_final=norm_final)
    small_m = dict(norm_mix=m_norm_mix, w_gla_gate=m_w_gla_gate, b_gla_gate=m_b_gla_gate, gla_norm=m_gla_norm,
                   w_dw=m_w_dw, b_dw=m_b_dw, conv_ln_g=m_conv_ln_g, conv_ln_b=m_conv_ln_b, rel_bias=m_rel_bias,
                   norm_ffn=m_norm_ffn, norm_final=m_norm_final)
    small_v = dict(norm_mix=v_norm_mix, w_gla_gate=v_w_gla_gate, b_gla_gate=v_b_gla_gate, gla_norm=v_gla_norm,
                   w_dw=v_w_dw, b_dw=v_b_dw, conv_ln_g=v_conv_ln_g, conv_ln_b=v_conv_ln_b, rel_bias=v_rel_bias,
                   norm_ffn=v_norm_ffn, norm_final=v_norm_final)
    s_delta, s_m, s_v = _adamw_small([small_w[n] for n in small_names], [small_g[n] for n in small_names],
                                     [small_m[n] for n in small_names], [small_v[n] for n in small_names])
    s_idx = {n: t for t, n in enumerate(small_names)}

    order = ("norm_mix", "w_in", "w_gla_gate", "b_gla_gate", "gla_norm", "w_dw", "b_dw", "conv_ln_g", "conv_ln_b",
             "rel_bias", "w_out", "norm_ffn", "w_up", "w_down", "norm_final")

    def pick(kind, name):
        if name in big_out:
            return big_out[name][kind]
        t = s_idx[name]
        return (small_g[name], s_delta[t], s_m[t], s_v[t])[kind]

    outs = [loss, dx[None]]
    for kind in range(4):
        outs += [pick(kind, n) for n in order]
    return tuple(outs)
```

```python
import functools

import jax
import jax.numpy as jnp
from jax import lax
from jax.experimental import pallas as pl
from jax.experimental.pallas import tpu as pltpu

F32 = jnp.float32
BF16 = jnp.bfloat16
MESH = pl.DeviceIdType.MESH

D = 1024
DEPTH = 2
CH = 64
EPS = 1e-6
NEG = -1e30
N_DEV = 8
N_REL = 257
Q_SCALE = 48.0 ** -0.5
A_SCALE = 64.0 ** -0.5
GATE_TAU = 16.0
KCONV = 31

OQ, OKK, OV, OG, OCU, OAQ, OAK, OAV, OLR, DINP = 0, 256, 512, 896, 1280, 1792, 2176, 2560, 2944, 3072
IN_GROUPS = ((OQ, 256), (OKK, 256), (OV, 384), (OG, 384), (OCU, 512), (OAQ, 384), (OAK, 384), (OAV, 384), (OLR, 128))

AQ_BLK = 256
AK_WIN = 768
WIN_LEFT = 2
RB_W = 1536

ADAM_LR, ADAM_B1, ADAM_B2, ADAM_EPS, ADAM_WD, ADAM_STEP = 0.001, 0.9, 0.999, 1e-08, 0.01, 10


V7X_VMEM_MIB = 64
VMEM_LIMIT_MIB = V7X_VMEM_MIB - 1


def _cp(sem=None):
    kw = {"vmem_limit_bytes": VMEM_LIMIT_MIB * 1024 * 1024}
    if sem is not None:
        kw["dimension_semantics"] = sem
    return pltpu.CompilerParams(**kw)


def _dot(a, b):
    return jnp.dot(a, b, preferred_element_type=F32)


def _dot_nt(a, b):
    return lax.dot_general(a, b, (((1,), (1,)), ((), ())), preferred_element_type=F32)


def _dot_tn(a, b):
    return lax.dot_general(a, b, (((0,), (0,)), ((), ())), preferred_element_type=F32)


def _split2(a):
    hi = a.astype(BF16)
    lo = (a - hi.astype(F32)).astype(BF16)
    return hi, lo


def _split3(a):
    hi = a.astype(BF16)
    r1 = a - hi.astype(F32)
    mid = r1.astype(BF16)
    lo = (r1 - mid.astype(F32)).astype(BF16)
    return hi, mid, lo


def _sigmoid(x):
    return 1.0 / (1.0 + jnp.exp(-x))


def _group(idx, size, n):
    g = jnp.zeros_like(idx)
    for t in range(1, n):
        g = g + (idx >= t * size).astype(jnp.int32)
    return g


def _rms_bwd(dy, x, r, gamma):
    xh = x * r
    dxh = dy * gamma
    dx = r * (dxh - xh * jnp.mean(dxh * xh, axis=-1, keepdims=True))
    return dx, jnp.sum(dy * xh, axis=0, keepdims=True)


def _row_spec(tm, n):
    return pl.BlockSpec((tm, n), lambda i: (i, 0))


def _full_spec(shape):
    nd = len(shape)
    return pl.BlockSpec(shape, lambda *_: (0,) * nd)


def _any_spec():
    return pl.BlockSpec(memory_space=pl.ANY)


class _Job:
    def __init__(self, operands, out_shapes, sems, start, finish, aliases=None):
        self.operands, self.out_shapes, self.sems = list(operands), list(out_shapes), list(sems)
        self.start, self.finish, self.aliases = start, finish, dict(aliases or {})


def _pcall(body, *, name, grid, in_specs, out_specs, out_shape, operands, scratch_shapes=(), sem=None, jobs=()):
    jobs = list(jobs)
    in_specs, out_specs, out_shape = list(in_specs), list(out_specs), list(out_shape)
    scratch_shapes = list(scratch_shapes)
    n_in, n_out, n_scr = len(in_specs), len(out_specs), len(scratch_shapes)
    j_in = [a for j in jobs for a in j.operands]
    j_out = [s for j in jobs for s in j.out_shapes]
    j_sem = [s for j in jobs for s in j.sems]
    aliases, io, oo = {}, n_in, n_out
    for j in jobs:
        for a, b in j.aliases.items():
            aliases[io + a] = oo + b
        io += len(j.operands)
        oo += len(j.out_shapes)

    def wrapped(*refs):
        own_in, ji = refs[:n_in], refs[n_in:n_in + len(j_in)]
        o0 = n_in + len(j_in)
        own_out, jo = refs[o0:o0 + n_out], refs[o0 + n_out:o0 + n_out + len(j_out)]
        s0 = o0 + n_out + len(j_out)
        own_scr, js = refs[s0:s0 + n_scr], refs[s0 + n_scr:]

        def each_job(fn_name):
            a = b = c = 0
            for j in jobs:
                na, nb, nc = len(j.operands), len(j.out_shapes), len(j.sems)
                getattr(j, fn_name)(ji[a:a + na], jo[b:b + nb], js[c:c + nc])
                a, b, c = a + na, b + nb, c + nc

        if jobs and grid:
            pids = [pl.program_id(d) for d in range(len(grid))]
            first = functools.reduce(jnp.logical_and, [p == 0 for p in pids])
            last = functools.reduce(jnp.logical_and, [p == g - 1 for p, g in zip(pids, grid)])
            pl.when(first)(lambda: each_job("start"))
        elif jobs:
            each_job("start")

        body(*own_in, *own_out, *own_scr)

        if jobs and grid:
            pl.when(last)(lambda: each_job("finish"))
        elif jobs:
            each_job("finish")

    res = pl.pallas_call(
        wrapped, name=name, grid=grid,
        in_specs=in_specs + [_any_spec()] * len(j_in), out_specs=out_specs + [_any_spec()] * len(j_out),
        out_shape=out_shape + j_out, scratch_shapes=scratch_shapes + j_sem,
        input_output_aliases=aliases, compiler_params=_cp(sem),
    )(*operands, *j_in)
    return res[:n_out], res[n_out:]


ATT_HEADS = 6
HEAD_PAD = 128
ATT_WIDE = ATT_HEADS * HEAD_PAD
ATT_GROUP_OFFS = (OAQ, OAK, OAV)


def _store_head_padded(o_ref, part):
    o_ref[...] = jnp.zeros_like(o_ref)
    for hd in range(ATT_HEADS):
        o_ref[:, hd * HEAD_PAD:hd * HEAD_PAD + 64] = part[:, hd * 64:(hd + 1) * 64]


def _inproj_fwd(h, gamma, w, jobs=()):
    T = h.shape[0]
    tm = 1024

    def body(h_ref, g_ref, w_ref, *outs):
        x = h_ref[...]
        r = lax.rsqrt(jnp.mean(x * x, axis=-1, keepdims=True) + EPS)
        xn = (x * r * g_ref[...]).astype(BF16)
        p = _dot(xn, w_ref[...])
        for o_ref, (off, n) in zip(outs, IN_GROUPS):
            part = p[:, off:off + n].astype(BF16)
            if off in ATT_GROUP_OFFS:
                _store_head_padded(o_ref, part)
            else:
                o_ref[...] = part

    widths = [ATT_WIDE if off in ATT_GROUP_OFFS else n for off, n in IN_GROUPS]
    return _pcall(
        body, name="inproj_fwd", grid=(T // tm,),
        in_specs=[_row_spec(tm, D), _full_spec((1, D)), _full_spec((D, DINP))],
        out_specs=[_row_spec(tm, n) for n in widths],
        out_shape=[jax.ShapeDtypeStruct((T, n), BF16) for n in widths],
        sem=("arbitrary",), operands=(h, gamma, w), jobs=jobs)


def _inproj_norm(h_ref, g_ref):
    x = h_ref[...]
    r = lax.rsqrt(jnp.mean(x * x, axis=-1, keepdims=True) + EPS)
    return x, r, g_ref[...]


def _inproj_bwd_w(h, gamma, dparts):
    T = h.shape[0]
    tm = 1024
    nt = T // tm

    def body(h_ref, g_ref, *rest):
        dp_refs = rest[:9]
        dw_ref, acc = rest[9:]
        i = pl.program_id(0)

        @pl.when(i == 0)
        def _():
            acc[...] = jnp.zeros_like(acc)

        x, r, gamma_ = _inproj_norm(h_ref, g_ref)
        xnt = jnp.transpose((x * r * gamma_).astype(BF16))
        acc[...] += _dot(xnt, jnp.concatenate([d_ref[...] for d_ref in dp_refs], axis=1))

        @pl.when(i == nt - 1)
        def _():
            dw_ref[...] = acc[...].astype(BF16)

    return pl.pallas_call(
        body, name="inproj_bwd_w", grid=(nt,),
        in_specs=[_row_spec(tm, D), _full_spec((1, D))] + [_row_spec(tm, n) for _, n in IN_GROUPS],
        out_specs=_full_spec((D, DINP)),
        out_shape=jax.ShapeDtypeStruct((D, DINP), BF16),
        scratch_shapes=[pltpu.VMEM((D, DINP), F32)],
        compiler_params=_cp(("arbitrary",)),
    )(h, gamma, *dparts)


def _inproj_bwd_x(h, dh_in, gamma, w, dparts, jobs=()):
    T = h.shape[0]
    tm = 512

    def body(h_ref, dhin_ref, g_ref, w_ref, *rest):
        dp_refs = rest[:9]
        dh_ref, dhb_ref, dg_ref = rest[9:]

        @pl.when(pl.program_id(0) == 0)
        def _():
            dg_ref[...] = jnp.zeros_like(dg_ref)

        x, r, gamma_ = _inproj_norm(h_ref, g_ref)
        dxn = _dot_nt(jnp.concatenate([d_ref[...] for d_ref in dp_refs], axis=1), w_ref[...])
        dx, dgam = _rms_bwd(dxn, x, r, gamma_)
        dh = dhin_ref[...] + dx
        dh_ref[...] = dh
        dhb_ref[...] = dh.astype(BF16)
        dg_ref[...] += dgam

    return _pcall(
        body, name="inproj_bwd_x", grid=(T // tm,),
        in_specs=[_row_spec(tm, D), _row_spec(tm, D), _full_spec((1, D)), _full_spec((D, DINP))]
        + [_row_spec(tm, n) for _, n in IN_GROUPS],
        out_specs=[_row_spec(tm, D), _row_spec(tm, D), _full_spec((1, D))],
        out_shape=[jax.ShapeDtypeStruct((T, D), F32), jax.ShapeDtypeStruct((T, D), BF16),
                   jax.ShapeDtypeStruct((1, D), F32)],
        sem=("arbitrary",), operands=(h, dh_in, gamma, w, *dparts), jobs=jobs)


GLA_ROWS = 512
GLA_NC = GLA_ROWS // CH


def _gla_consts():
    ri = lax.broadcasted_iota(jnp.int32, (CH, CH), 0)
    ci = lax.broadcasted_iota(jnp.int32, (CH, CH), 1)
    upper = (ci > ri).astype(BF16)
    vv = lax.broadcasted_iota(jnp.int32, (384, 256), 0)
    kk = lax.broadcasted_iota(jnp.int32, (384, 256), 1)
    mask_t = ((_group(vv, 96, 4) == _group(kk, 48, 4)) & (kk < 192)).astype(F32)
    pi = lax.broadcasted_iota(jnp.int32, (384, 384), 0)
    pj = lax.broadcasted_iota(jnp.int32, (384, 384), 1)
    same_head = (_group(pi, 96, 4) == _group(pj, 96, 4)).astype(BF16)
    return upper, mask_t, same_head


def _gla_gate(lr_ref, wg_ref, bg_ref):
    z = _dot(lr_ref[...], wg_ref[...]) + bg_ref[...]
    la = (jnp.minimum(z, 0.0) - jnp.log(1.0 + jnp.exp(-jnp.abs(z)))) * (1.0 / GATE_TAU)
    return z, la


def _gla_chunk_decay(la_c, upper):
    hi, lo = _split2(la_c)
    dec = _dot(upper, hi) + _dot(upper, lo)
    end = jnp.sum(la_c, axis=0, keepdims=True)
    return jnp.exp(dec), jnp.exp(end)


def _head_mean(x, same_head):
    hi, lo = _split2(x)
    return (_dot(hi, same_head) + _dot(lo, same_head)) * (1.0 / 96.0)


def _gla_fwd(q, k, v, g, lr, wg, bg, gn, jobs=()):
    T = q.shape[0]
    nb = T // GLA_ROWS

    def body(q_ref, k_ref, v_ref, g_ref, lr_ref, wg_ref, bg_ref, gn_ref, y_ref, st_ref, s_scr, o_scr, kv_scr):
        upper, mask_t, same_head = _gla_consts()

        @pl.when(pl.program_id(0) == 0)
        def _():
            s_scr[...] = jnp.zeros_like(s_scr)

        _, la = _gla_gate(lr_ref, wg_ref, bg_ref)
        decays = []
        for c in range(GLA_NC):
            rs = slice(c * CH, (c + 1) * CH)
            w, a = _gla_chunk_decay(la[rs], upper)
            decays.append(a)
            kd = (k_ref[rs, :].astype(F32) * w).astype(BF16)
            kv_scr[c] = _dot_tn(v_ref[rs, :], kd) * mask_t
        for c in range(GLA_NC):
            s_new = s_scr[...] * decays[c] + kv_scr[c]
            s_scr[...] = s_new
            st_ref[c] = s_new.astype(BF16)
        for c in range(GLA_NC):
            rs = slice(c * CH, (c + 1) * CH)
            qs = (q_ref[rs, :].astype(F32) * Q_SCALE).astype(BF16)
            o_scr[rs, :] = _dot_nt(qs, st_ref[c])
        o = o_scr[...]
        r = lax.rsqrt(_head_mean(o * o, same_head) + EPS)
        gf = g_ref[...].astype(F32)
        y_ref[...] = (o * r * gn_ref[...] * (gf * _sigmoid(gf))).astype(BF16)

    return _pcall(
        body, name="gla_fwd", grid=(nb,),
        in_specs=[_row_spec(GLA_ROWS, 256), _row_spec(GLA_ROWS, 256), _row_spec(GLA_ROWS, 384),
                  _row_spec(GLA_ROWS, 384), _row_spec(GLA_ROWS, 128),
                  _full_spec((128, 256)), _full_spec((1, 256)), _full_spec((1, 384))],
        out_specs=[_row_spec(GLA_ROWS, 384), pl.BlockSpec((GLA_NC, 384, 256), lambda i: (i, 0, 0))],
        out_shape=[jax.ShapeDtypeStruct((T, 384), BF16), jax.ShapeDtypeStruct((T // CH, 384, 256), BF16)],
        scratch_shapes=[pltpu.VMEM((384, 256), F32), pltpu.VMEM((GLA_ROWS, 384), F32),
                        pltpu.VMEM((GLA_NC, 384, 256), F32)],
        sem=("arbitrary",), operands=(q, k, v, g, lr, wg, bg, gn), jobs=jobs)


def _gla_bwd(q, k, v, g, lr, states, dy, wg, bg, gn):
    T = q.shape[0]
    nb = T // GLA_ROWS

    def rev(s):
        return nb - 1 - s

    def body(q_ref, k_ref, v_ref, g_ref, lr_ref, st_ref, stp_ref, dy_ref, wg_ref, bg_ref, gn_ref,
             dq_ref, dk_ref, dv_ref, dg_ref, dlr_ref, dwg_ref, dbg_ref, dgn_ref,
             d_scr, an_scr, o_scr, do_scr, dla_scr, dst_scr):
        upper, mask_t, same_head = _gla_consts()
        s = pl.program_id(0)
        blk = rev(s)

        @pl.when(s == 0)
        def _():
            d_scr[...] = jnp.zeros_like(d_scr)
            an_scr[...] = jnp.zeros_like(an_scr)
            dwg_ref[...] = jnp.zeros_like(dwg_ref)
            dbg_ref[...] = jnp.zeros_like(dbg_ref)
            dgn_ref[...] = jnp.zeros_like(dgn_ref)

        z, la = _gla_gate(lr_ref, wg_ref, bg_ref)
        ws, as_, qss, kds = [], [], [], []
        for c in range(GLA_NC):
            rs = slice(c * CH, (c + 1) * CH)
            w, a = _gla_chunk_decay(la[rs], upper)
            ws.append(w)
            as_.append(a)
            qs = (q_ref[rs, :].astype(F32) * Q_SCALE).astype(BF16)
            qss.append(qs)
            kds.append((k_ref[rs, :].astype(F32) * w).astype(BF16))
            o_scr[rs, :] = _dot_nt(qs, st_ref[c])
        o = o_scr[...]
        r = lax.rsqrt(_head_mean(o * o, same_head) + EPS)
        on = o * r
        gf = g_ref[...].astype(F32)
        sg = _sigmoid(gf)
        si = gf * sg
        dyf = dy_ref[...].astype(F32)
        gn_ = gn_ref[...]
        dgn_ref[...] += jnp.sum(dyf * si * on, axis=0, keepdims=True)
        dg_ref[...] = (dyf * on * gn_ * (sg * (1.0 + gf * (1.0 - sg)))).astype(BF16)
        d_on = dyf * si * gn_
        do_scr[...] = r * (d_on - on * _head_mean(d_on * on, same_head))

        for c in range(GLA_NC):
            rs = slice(c * CH, (c + 1) * CH)
            dst_scr[c] = _dot_tn(do_scr[rs, :].astype(BF16), qss[c]) * mask_t
        for c in reversed(range(GLA_NC)):
            dt = d_scr[...] * an_scr[...] + dst_scr[c]
            d_scr[...] = dt
            dst_scr[c] = dt
            an_scr[...] = as_[c]
        first = (blk > 0).astype(F32)
        for c in range(GLA_NC):
            rs = slice(c * CH, (c + 1) * CH)
            dob = do_scr[rs, :].astype(BF16)
            if c > 0:
                s_prev = st_ref[c - 1].astype(F32)
            else:
                s_prev = stp_ref[0].astype(F32) * first
            dq_ref[rs, :] = (_dot(dob, st_ref[c]) * Q_SCALE).astype(BF16)
            dt = dst_scr[c]
            da = jnp.sum(dt * s_prev, axis=0, keepdims=True)
            db = dt.astype(BF16)
            dkd = _dot(v_ref[rs, :], db)
            dv_ref[rs, :] = _dot_nt(kds[c], db).astype(BF16)
            dk_ref[rs, :] = (dkd * ws[c]).astype(BF16)
            ddec = dkd * k_ref[rs, :].astype(F32) * ws[c]
            hi, lo = _split2(ddec)
            dla_scr[rs, :] = _dot_tn(upper, hi) + _dot_tn(upper, lo) + as_[c] * da

        dz = dla_scr[...] * (1.0 - _sigmoid(z)) * (1.0 / GATE_TAU)
        dzb = dz.astype(BF16)
        dlr_ref[...] = _dot_nt(dzb, wg_ref[...]).astype(BF16)
        dwg_ref[...] += _dot_tn(lr_ref[...], dzb)
        dbg_ref[...] += jnp.sum(dz, axis=0, keepdims=True)

    def rspec(n):
        return pl.BlockSpec((GLA_ROWS, n), lambda s: (rev(s), 0))

    return pl.pallas_call(
        body, name="gla_bwd", grid=(nb,),
        in_specs=[rspec(256), rspec(256), rspec(384), rspec(384), rspec(128),
                  pl.BlockSpec((GLA_NC, 384, 256), lambda s: (rev(s), 0, 0)),
                  pl.BlockSpec((1, 384, 256), lambda s: (jnp.maximum(rev(s) * GLA_NC - 1, 0), 0, 0)),
                  rspec(384), _full_spec((128, 256)), _full_spec((1, 256)), _full_spec((1, 384))],
        out_specs=[rspec(256), rspec(256), rspec(384), rspec(384), rspec(128),
                   _full_spec((128, 256)), _full_spec((1, 256)), _full_spec((1, 384))],
        out_shape=[jax.ShapeDtypeStruct((T, 256), BF16), jax.ShapeDtypeStruct((T, 256), BF16),
                   jax.ShapeDtypeStruct((T, 384), BF16), jax.ShapeDtypeStruct((T, 384), BF16),
                   jax.ShapeDtypeStruct((T, 128), BF16),
                   jax.ShapeDtypeStruct((128, 256), F32), jax.ShapeDtypeStruct((1, 256), F32),
                   jax.ShapeDtypeStruct((1, 384), F32)],
        scratch_shapes=[pltpu.VMEM((384, 256), F32), pltpu.VMEM((1, 256), F32),
                        pltpu.VMEM((GLA_ROWS, 384), F32), pltpu.VMEM((GLA_ROWS, 384), F32),
                        pltpu.VMEM((GLA_ROWS, 256), F32), pltpu.VMEM((GLA_NC, 384, 256), F32)],
        compiler_params=_cp(("arbitrary",)),
    )(q, k, v, g, lr, states, states, dy, wg, bg, gn)


CONV_ROWS = 512
HALO = 32
SUBL = 8
CONV_SLAB = 32
PHASE_ROWS = CONV_ROWS + HALO - SUBL
FWD_SHIFT = tuple(HALO - (KCONV - 1) + j for j in range(KCONV))
BWD_SHIFT = tuple(KCONV - 1 - j for j in range(KCONV))


def _fill_phases(buf, ph):
    for f in range(1, SUBL):
        ph[f, 0:PHASE_ROWS, :] = buf[pl.ds(f, PHASE_ROWS), :]


def _tap(buf, ph, shift, r, n):
    f, base = shift % SUBL, shift - shift % SUBL
    src = buf if f == 0 else ph.at[f]
    return src[pl.ds(base + r, n), :]


def _taps_apply(w_ref, buf, ph, shifts, out):
    for r in range(0, CONV_ROWS, CONV_SLAB):
        acc = jnp.zeros((CONV_SLAB, 256), F32)
        for j in range(KCONV):
            acc = acc + w_ref[j:j + 1, :] * _tap(buf, ph, shifts[j], r, CONV_SLAB)
        out[r:r + CONV_SLAB, :] = acc


def _conv_scratch():
    return [pltpu.VMEM((CONV_ROWS + HALO, 256), F32), pltpu.VMEM((SUBL, CONV_ROWS + HALO, 256), F32),
            pltpu.VMEM((CONV_ROWS, 256), F32)]


def _conv_common(cu_ref, halo_ref, w_ref, b_ref, lg_ref, lb_ref, buf, ph, cbuf, blk, conv_ref=None):
    u = cu_ref[...].astype(F32)
    a = u[:, :256]
    sb = _sigmoid(u[:, 256:])
    uh = halo_ref[...].astype(F32)
    hh = uh[:, :256] * _sigmoid(uh[:, 256:]) * (blk > 0).astype(F32)
    buf[0:HALO, :] = hh
    buf[HALO:HALO + CONV_ROWS, :] = a * sb
    _fill_phases(buf, ph)
    if conv_ref is None:
        _taps_apply(w_ref, buf, ph, FWD_SHIFT, cbuf)
        conv = cbuf[...]
    else:
        conv = conv_ref[...]
    cc = conv + b_ref[...]
    mu = jnp.mean(cc, axis=-1, keepdims=True)
    xc = cc - mu
    rstd = lax.rsqrt(jnp.mean(xc * xc, axis=-1, keepdims=True) + EPS)
    n = xc * rstd
    yln = n * lg_ref[...] + lb_ref[...]
    return a, sb, n, rstd, yln, conv


def _conv_fwd(cu, w, b, lg, lb, jobs=()):
    T = cu.shape[0]
    nb = T // CONV_ROWS
    per = CONV_ROWS // HALO

    def body(cu_ref, halo_ref, w_ref, b_ref, lg_ref, lb_ref, y_ref, conv_ref, buf, ph, cbuf):
        _, _, _, _, yln, conv = _conv_common(cu_ref, halo_ref, w_ref, b_ref, lg_ref, lb_ref, buf, ph, cbuf,
                                             pl.program_id(0))
        y_ref[...] = (yln * _sigmoid(yln)).astype(BF16)
        conv_ref[...] = conv

    return _pcall(
        body, name="conv_fwd", grid=(nb,),
        in_specs=[_row_spec(CONV_ROWS, 512),
                  pl.BlockSpec((HALO, 512), lambda i: (jnp.maximum(i * per - 1, 0), 0)),
                  _full_spec((32, 256)), _full_spec((1, 256)), _full_spec((1, 256)), _full_spec((1, 256))],
        out_specs=[_row_spec(CONV_ROWS, 256), _row_spec(CONV_ROWS, 256)],
        out_shape=[jax.ShapeDtypeStruct((T, 256), BF16), jax.ShapeDtypeStruct((T, 256), F32)],
        scratch_shapes=_conv_scratch(),
        sem=("arbitrary",), operands=(cu, cu, w, b, lg, lb), jobs=jobs)


def _conv_bwd(cu, conv, dy, w, b, lg, lb, jobs=()):
    T = cu.shape[0]
    nb = T // CONV_ROWS
    per = CONV_ROWS // HALO

    def rev(s):
        return nb - 1 - s

    def body(cu_ref, halo_ref, conv_ref, dy_ref, w_ref, b_ref, lg_ref, lb_ref,
             dcu_ref, dw_ref, db_ref, dlg_ref, dlb_ref, buf, ph, cbuf, dcbuf, dph, carry):
        s = pl.program_id(0)

        @pl.when(s == 0)
        def _():
            carry[...] = jnp.zeros_like(carry)
            dw_ref[...] = jnp.zeros_like(dw_ref)
            db_ref[...] = jnp.zeros_like(db_ref)
            dlg_ref[...] = jnp.zeros_like(dlg_ref)
            dlb_ref[...] = jnp.zeros_like(dlb_ref)

        a, sb, n, rstd, yln, _ = _conv_common(cu_ref, halo_ref, w_ref, b_ref, lg_ref, lb_ref, buf, ph, cbuf, rev(s),
                                              conv_ref=conv_ref)
        sg = _sigmoid(yln)
        dyln = dy_ref[...].astype(F32) * (sg * (1.0 + yln * (1.0 - sg)))
        dlg_ref[...] += jnp.sum(dyln * n, axis=0, keepdims=True)
        dlb_ref[...] += jnp.sum(dyln, axis=0, keepdims=True)
        dn = dyln * lg_ref[...]
        dc = rstd * (dn - jnp.mean(dn, axis=-1, keepdims=True) - n * jnp.mean(dn * n, axis=-1, keepdims=True))
        db_ref[...] += jnp.sum(dc, axis=0, keepdims=True)
        dcbuf[0:CONV_ROWS, :] = dc
        dcbuf[CONV_ROWS:CONV_ROWS + HALO, :] = carry[...]
        carry[...] = dc[0:HALO, :]
        _fill_phases(dcbuf, dph)
        for j in range(KCONV):
            acc = jnp.zeros((SUBL, 256), F32)
            for r in range(0, CONV_ROWS, 2 * CONV_SLAB):
                prod = dcbuf[r:r + 2 * CONV_SLAB, :] * _tap(buf, ph, FWD_SHIFT[j], r, 2 * CONV_SLAB)
                acc = acc + jnp.sum(prod.reshape(2 * CONV_SLAB // SUBL, SUBL, 256), axis=0)
            dw_ref[j:j + 1, :] += jnp.sum(acc, axis=0, keepdims=True)
        _taps_apply(w_ref, dcbuf, dph, BWD_SHIFT, cbuf)
        dhg = cbuf[...]
        dcu_ref[...] = jnp.concatenate([dhg * sb, dhg * a * sb * (1.0 - sb)], axis=1).astype(BF16)

    def rspec(n):
        return pl.BlockSpec((CONV_ROWS, n), lambda s: (rev(s), 0))

    return _pcall(
        body, name="conv_bwd", grid=(nb,),
        in_specs=[rspec(512),
                  pl.BlockSpec((HALO, 512), lambda s: (jnp.maximum(rev(s) * per - 1, 0), 0)),
                  rspec(256), rspec(256),
                  _full_spec((32, 256)), _full_spec((1, 256)), _full_spec((1, 256)), _full_spec((1, 256))],
        out_specs=[rspec(512), _full_spec((32, 256)), _full_spec((1, 256)), _full_spec((1, 256)),
                   _full_spec((1, 256))],
        out_shape=[jax.ShapeDtypeStruct((T, 512), BF16), jax.ShapeDtypeStruct((32, 256), F32),
                   jax.ShapeDtypeStruct((1, 256), F32), jax.ShapeDtypeStruct((1, 256), F32),
                   jax.ShapeDtypeStruct((1, 256), F32)],
        scratch_shapes=_conv_scratch() + [pltpu.VMEM((CONV_ROWS + HALO, 256), F32),
                                          pltpu.VMEM((SUBL, CONV_ROWS + HALO, 256), F32),
                                          pltpu.VMEM((HALO, 256), F32)],
        sem=("arbitrary",), operands=(cu, cu, conv, dy, w, b, lg, lb), jobs=jobs)


def _rel_onehot_t(shift=0):
    r = lax.broadcasted_iota(jnp.int32, (384, RB_W), 0)
    n = lax.broadcasted_iota(jnp.int32, (384, RB_W), 1) - shift
    idx = jnp.clip(1024 - n, -128, 128) + 128
    return (idx == r).astype(BF16)


def _relbias_expand(rb):
    def body(rb_ref, out_ref):
        oh = _rel_onehot_t()
        hi, mid, lo = _split3(rb_ref[...])
        strip = _dot(hi, oh) + _dot(mid, oh) + _dot(lo, oh)
        qi = _group(lax.broadcasted_iota(jnp.int32, (AQ_BLK, AK_WIN), 0), CH, 4)
        kj = _group(lax.broadcasted_iota(jnp.int32, (AQ_BLK, AK_WIN), 1), CH, 12)
        valid = (kj >= qi) & (kj <= qi + 8)
        for hd in range(6):
            x = jnp.broadcast_to(strip[hd:hd + 1, :], (AQ_BLK, RB_W))
            xr = pltpu.roll(x, 0, 1, stride=1, stride_axis=0)
            out_ref[hd] = jnp.where(valid, xr[:, 512:512 + AK_WIN], NEG)

    return pl.pallas_call(
        body, name="relbias_expand",
        out_shape=jax.ShapeDtypeStruct((6, AQ_BLK, AK_WIN), F32),
        compiler_params=_cp(),
    )(rb)


def _relbias_grad(dbias):
    def body(db_ref, out_ref):
        oh = _rel_onehot_t(AQ_BLK - 1)
        ri = lax.broadcasted_iota(jnp.int32, (AQ_BLK, AQ_BLK), 0)
        ci = lax.broadcasted_iota(jnp.int32, (AQ_BLK, AQ_BLK), 1)
        flip = (ri + ci == AQ_BLK - 1).astype(BF16)
        rows = []
        for hd in range(6):
            hi, mid, lo = _split3(db_ref[hd])
            rev = _dot(flip, hi) + _dot(flip, mid) + _dot(flip, lo)
            x = jnp.concatenate([jnp.zeros((AQ_BLK, 512), F32), rev,
                                 jnp.zeros((AQ_BLK, RB_W - 512 - AK_WIN), F32)], axis=1)
            xr = pltpu.roll(x, 0, 1, stride=1, stride_axis=0)
            rows.append(jnp.sum(xr, axis=0, keepdims=True))
        rows.append(jnp.zeros((2, RB_W), F32))
        dstrip = jnp.concatenate(rows, axis=0)
        hi, mid, lo = _split3(dstrip)
        out_ref[...] = _dot_nt(hi, oh) + _dot_nt(mid, oh) + _dot_nt(lo, oh)

    return pl.pallas_call(
        body, name="relbias_grad",
        out_shape=jax.ShapeDtypeStruct((8, 384), F32),
        compiler_params=_cp(),
    )(dbias)


ATT_SLAB = 8


def _att_logits_slab(s_scr, b_ref, hd, rows, first_key):
    kvalid = lax.broadcasted_iota(jnp.int32, (ATT_SLAB, AK_WIN), 1) >= first_key
    return jnp.where(kvalid, s_scr[rows, :] + b_ref[hd, rows, :], NEG)


def _att_softmax_slab(s_scr, b_ref, hd, rows, first_key):
    s = _att_logits_slab(s_scr, b_ref, hd, rows, first_key)
    m = jnp.max(s, axis=-1, keepdims=True)
    p = jnp.exp(s - m)
    total = jnp.sum(p, axis=-1, keepdims=True)
    return p * (1.0 / total), m + jnp.log(total)


def _att_first_key(i):
    return (8 - 4 * i) * CH


def _slab_rows(t):
    return pl.ds(t * ATT_SLAB, ATT_SLAB)


def _head_lanes(hd):
    return slice(hd * 64, (hd + 1) * 64)


WIN_BLKS = AK_WIN // AQ_BLK


def _head_tile(hd):
    return slice(hd * HEAD_PAD, (hd + 1) * HEAD_PAD)


def _win_cols(d):
    return slice(d * AQ_BLK, (d + 1) * AQ_BLK)


def _win_block(i, d):
    return jnp.maximum(i + d - WIN_LEFT, 0)


def _win_specs():
    return [pl.BlockSpec((AQ_BLK, ATT_WIDE), lambda i, d=d: (_win_block(i, d), 0)) for d in range(WIN_BLKS)]


def _att_fwd(q, k, v, bias, jobs=()):
    T = q.shape[0]
    nb = T // AQ_BLK

    def body(q_ref, k0, k1, k2, v0, v1, v2, b_ref, o_ref, lse_ref, s_scr):
        k_refs, v_refs = (k0, k1, k2), (v0, v1, v2)
        first_key = _att_first_key(pl.program_id(0))
        lse_ref[...] = jnp.zeros_like(lse_ref)

        def scores(hd):
            q_h = q_ref[:, _head_tile(hd)] * A_SCALE
            for d in range(WIN_BLKS):
                s_scr[hd % 2, :, _win_cols(d)] = _dot_nt(q_h, k_refs[d][:, _head_tile(hd)])

        scores(0)
        for hd in range(ATT_HEADS):
            if hd + 1 < ATT_HEADS:
                scores(hd + 1)
            s_h = s_scr.at[hd % 2]
            for t in range(AQ_BLK // ATT_SLAB):
                rows = _slab_rows(t)
                s_h[rows, :], lse_ref[rows, hd:hd + 1] = _att_softmax_slab(s_h, b_ref, hd, rows, first_key)
            o_h = _dot(s_h[:, _win_cols(0)].astype(BF16), v_refs[0][:, _head_tile(hd)])
            for d in range(1, WIN_BLKS):
                o_h = o_h + _dot(s_h[:, _win_cols(d)].astype(BF16), v_refs[d][:, _head_tile(hd)])
            o_ref[:, _head_lanes(hd)] = o_h[:, :64].astype(BF16)

    return _pcall(
        body, name="att_fwd", grid=(nb,),
        in_specs=[_row_spec(AQ_BLK, ATT_WIDE)] + _win_specs() + _win_specs() + [_full_spec((6, AQ_BLK, AK_WIN))],
        out_specs=[_row_spec(AQ_BLK, 384), _row_spec(AQ_BLK, 128)],
        out_shape=[jax.ShapeDtypeStruct((T, 384), BF16), jax.ShapeDtypeStruct((T, 128), F32)],
        scratch_shapes=[pltpu.VMEM((2, AQ_BLK, AK_WIN), F32)],
        sem=("arbitrary",), operands=(q, k, k, k, v, v, v, bias), jobs=jobs)


def _att_bwd(q, k, v, bias, o, lse, do, jobs=()):
    T = q.shape[0]
    nb = T // AQ_BLK

    def body(q_ref, k0, k1, k2, v0, v1, v2, b_ref, o_ref, lse_ref, do_ref, dq_ref, dk_ref, dv_ref, db_ref,
             dk_acc, dv_acc, s_scr, dp_scr, delta_scr):
        k_refs, v_refs = (k0, k1, k2), (v0, v1, v2)
        i = pl.program_id(0)

        @pl.when(i == 0)
        def _():
            dk_acc[...] = jnp.zeros_like(dk_acc)
            dv_acc[...] = jnp.zeros_like(dv_acc)
            db_ref[...] = jnp.zeros_like(db_ref)

        first_key = _att_first_key(i)

        def scores(hd):
            q_h = q_ref[:, _head_tile(hd)] * A_SCALE
            do_h = do_ref[:, _head_tile(hd)]
            delta_scr[:, hd:hd + 1] = jnp.sum(do_h[:, :64].astype(F32) * o_ref[:, _head_lanes(hd)].astype(F32),
                                              axis=-1, keepdims=True)
            for d in range(WIN_BLKS):
                s_scr[hd % 2, :, _win_cols(d)] = _dot_nt(q_h, k_refs[d][:, _head_tile(hd)])
                dp_scr[hd % 2, :, _win_cols(d)] = _dot_nt(do_h, v_refs[d][:, _head_tile(hd)])

        scores(0)
        for hd in range(ATT_HEADS):
            if hd + 1 < ATT_HEADS:
                scores(hd + 1)
            s_h, dp_h = s_scr.at[hd % 2], dp_scr.at[hd % 2]
            for t in range(AQ_BLK // ATT_SLAB):
                rows = _slab_rows(t)
                p = jnp.exp(_att_logits_slab(s_h, b_ref, hd, rows, first_key) - lse_ref[rows, hd:hd + 1])
                ds = p * (dp_h[rows, :] - delta_scr[rows, hd:hd + 1])
                db_ref[hd, rows, :] += ds
                s_h[rows, :] = p
                dp_h[rows, :] = ds
            q_h = q_ref[:, _head_tile(hd)] * A_SCALE
            do_h = do_ref[:, _head_tile(hd)]
            ls = _head_lanes(hd)
            dq_h = jnp.zeros((AQ_BLK, HEAD_PAD), F32)
            for d in range(WIN_BLKS):
                pb = s_h[:, _win_cols(d)].astype(BF16)
                dsb = dp_h[:, _win_cols(d)].astype(BF16)
                rows = pl.ds(pl.multiple_of(_win_block(i, d) * AQ_BLK, AQ_BLK), AQ_BLK)
                dv_acc[rows, ls] += _dot_tn(pb, do_h)[:, :64]
                dk_acc[rows, ls] += _dot_tn(dsb, q_h)[:, :64]
                dq_h = dq_h + _dot(dsb, k_refs[d][:, _head_tile(hd)])
            dq_ref[:, ls] = (dq_h[:, :64] * A_SCALE).astype(BF16)

        @pl.when(i == nb - 1)
        def _():
            dk_ref[...] = dk_acc[...].astype(BF16)
            dv_ref[...] = dv_acc[...].astype(BF16)

    return _pcall(
        body, name="att_bwd", grid=(nb,),
        in_specs=[_row_spec(AQ_BLK, ATT_WIDE)] + _win_specs() + _win_specs()
        + [_full_spec((6, AQ_BLK, AK_WIN)), _row_spec(AQ_BLK, 384), _row_spec(AQ_BLK, 128),
           _row_spec(AQ_BLK, ATT_WIDE)],
        out_specs=[_row_spec(AQ_BLK, 384), _full_spec((T, 384)), _full_spec((T, 384)),
                   _full_spec((6, AQ_BLK, AK_WIN))],
        out_shape=[jax.ShapeDtypeStruct((T, 384), BF16), jax.ShapeDtypeStruct((T, 384), BF16),
                   jax.ShapeDtypeStruct((T, 384), BF16), jax.ShapeDtypeStruct((6, AQ_BLK, AK_WIN), F32)],
        scratch_shapes=[pltpu.VMEM((T, 384), F32), pltpu.VMEM((T, 384), F32),
                        pltpu.VMEM((2, AQ_BLK, AK_WIN), F32), pltpu.VMEM((2, AQ_BLK, AK_WIN), F32),
                        pltpu.VMEM((AQ_BLK, 128), F32)],
        sem=("arbitrary",), operands=(q, k, k, k, v, v, v, bias, o, lse, do), jobs=jobs)


FF_BLK = 512
N_FF = 4096 // FF_BLK
MLP_SHARDS = 2


def _outproj_mlp_fwd(h, o_gla, o_conv, o_att, w_out, gamma, w_up, w_down, jobs=()):
    T = h.shape[0]
    tm = 1024

    def body(h_ref, og_ref, oc_ref, oa_ref, wo_ref, g_ref, wu_ref, wd_ref, h1_ref, xt_ref, h2_ref, a_ref, acc, xn_ref):
        j = pl.program_id(1)

        @pl.when(j == 0)
        def _():
            wo = wo_ref[...]
            h1 = (h_ref[...] + _dot(og_ref[...], wo[0:384]) + _dot(oc_ref[...], wo[384:640])
                  + _dot(oa_ref[...], wo[640:1024]))
            h1_ref[...] = h1
            r = lax.rsqrt(jnp.mean(h1 * h1, axis=-1, keepdims=True) + EPS)
            xn = (h1 * r * g_ref[...]).astype(BF16)
            xn_ref[...] = xn
            xt_ref[...] = jnp.transpose(xn)
            acc[...] = h1

        xn_ = xn_ref[...]
        down = None
        for s in range(MLP_SHARDS):
            a = jnp.maximum(_dot(xn_, wu_ref[s]), 0.0)
            a_ref[:, s * FF_BLK:(s + 1) * FF_BLK] = a.astype(BF16)
            part = _dot((a * a).astype(BF16), wd_ref[s])
            down = part if down is None else down + part
        acc[...] += down

        @pl.when(j == N_FF // MLP_SHARDS - 1)
        def _():
            h2_ref[...] = acc[...]

    row = lambda n: pl.BlockSpec((tm, n), lambda i, j: (i, 0))
    return _pcall(
        body, name="outproj_mlp_fwd", grid=(T // tm, N_FF // MLP_SHARDS),
        in_specs=[row(D), row(384), row(256), row(384),
                  pl.BlockSpec((D, D), lambda i, j: (0, 0)), pl.BlockSpec((1, D), lambda i, j: (0, 0)),
                  pl.BlockSpec((MLP_SHARDS, D, FF_BLK), lambda i, j: (j, 0, 0)),
                  pl.BlockSpec((MLP_SHARDS, FF_BLK, D), lambda i, j: (j, 0, 0))],
        out_specs=[row(D), pl.BlockSpec((D, tm), lambda i, j: (0, i)), row(D),
                   pl.BlockSpec((tm, MLP_SHARDS * FF_BLK), lambda i, j: (i, j))],
        out_shape=[jax.ShapeDtypeStruct((T, D), F32), jax.ShapeDtypeStruct((D, T), BF16),
                   jax.ShapeDtypeStruct((T, D), F32), jax.ShapeDtypeStruct((T, N_FF * FF_BLK), BF16)],
        scratch_shapes=[pltpu.VMEM((tm, D), F32), pltpu.VMEM((tm, D), BF16)],
        sem=("arbitrary", "arbitrary"), operands=(h, o_gla, o_conv, o_att, w_out, gamma, w_up, w_down), jobs=jobs)


def _mlp_bwd(xn2t, act, h1, dh2, dh2b, gamma, w_up, w_down, jobs=()):
    T = act.shape[0]
    tm = 512
    nt = T // tm
    ns = MLP_SHARDS
    nj = N_FF // ns
    last = nj - 1

    def body(xt_ref, a_ref, h1_ref, dy_ref, dyb_ref, g_ref, wu_ref, wd_ref, dh1_ref, dwu_ref, dwd_ref, dg_ref,
             dxn_acc, acc_u, acc_d):
        j = pl.program_id(0)
        i = pl.program_id(1)
        xt = xt_ref[...]
        dyb = dyb_ref[...]
        rows = pl.ds(pl.multiple_of(i * tm, tm), tm)

        @pl.when(i == 0)
        def _():
            acc_u[...] = jnp.zeros_like(acc_u)
            acc_d[...] = jnp.zeros_like(acc_d)

        @pl.when(j == 0)
        def _():
            dxn_acc[rows, :] = jnp.zeros((tm, D), F32)

        dxn = None
        for s in range(ns):
            a = a_ref[:, s * FF_BLK:(s + 1) * FF_BLK].astype(F32)
            hh = (a * a).astype(BF16)
            du = (_dot_nt(dyb, wd_ref[s]) * (2.0 * a)).astype(BF16)
            acc_d[s] += _dot_tn(hh, dyb)
            acc_u[s] += _dot(xt, du)
            part = _dot_nt(du, wu_ref[s])
            dxn = part if dxn is None else dxn + part
        dxn_acc[rows, :] += dxn

        @pl.when(i == nt - 1)
        def _():
            for s in range(ns):
                dwu_ref[s, 0] = acc_u[s].astype(BF16)
                dwd_ref[s, 0] = acc_d[s].astype(BF16)

        @pl.when(j == last)
        def _():
            @pl.when(i == 0)
            def _():
                dg_ref[...] = jnp.zeros_like(dg_ref)

            h1 = h1_ref[...]
            r = lax.rsqrt(jnp.mean(h1 * h1, axis=-1, keepdims=True) + EPS)
            dx, dgam = _rms_bwd(dxn_acc[rows, :], h1, r, g_ref[...])
            dh1_ref[...] = dy_ref[...] + dx
            dg_ref[...] += dgam

    assert ns == 2
    late = lambda j, i: (jnp.where(j == last, i, 0), 0)
    return _pcall(
        body, name="mlp_bwd", grid=(nj, nt),
        in_specs=[pl.BlockSpec((D, tm), lambda j, i: (0, i)), pl.BlockSpec((tm, ns * FF_BLK), lambda j, i: (i, j)),
                  pl.BlockSpec((tm, D), late), pl.BlockSpec((tm, D), late),
                  pl.BlockSpec((tm, D), lambda j, i: (i, 0)), pl.BlockSpec((1, D), lambda j, i: (0, 0)),
                  pl.BlockSpec((ns, D, FF_BLK), lambda j, i: (j, 0, 0), pipeline_mode=pl.Buffered(1)),
                  pl.BlockSpec((ns, FF_BLK, D), lambda j, i: (j, 0, 0), pipeline_mode=pl.Buffered(1))],
        out_specs=[pl.BlockSpec((tm, D), late),
                   pl.BlockSpec((ns, 1, D, FF_BLK), lambda j, i: (0, j, 0, 0)),
                   pl.BlockSpec((ns, 1, FF_BLK, D), lambda j, i: (0, j, 0, 0)),
                   pl.BlockSpec((1, D), lambda j, i: (0, 0))],
        out_shape=[jax.ShapeDtypeStruct((T, D), F32), jax.ShapeDtypeStruct((2, 4, D, FF_BLK), BF16),
                   jax.ShapeDtypeStruct((2, 4, FF_BLK, D), BF16), jax.ShapeDtypeStruct((1, D), F32)],
        scratch_shapes=[pltpu.VMEM((T, D), F32), pltpu.VMEM((ns, D, FF_BLK), F32), pltpu.VMEM((ns, FF_BLK, D), F32)],
        sem=("arbitrary", "arbitrary"), operands=(xn2t, act, h1, dh2, dh2b, gamma, w_up, w_down), jobs=jobs)


def _outproj_bwd(dh1, o_gla, o_conv, o_att, w_out, jobs=()):
    T = dh1.shape[0]
    tm = 1024
    nt = T // tm

    def body(dy_ref, og_ref, oc_ref, oa_ref, wo_ref, dg_ref, dc_ref, da_ref, dw_ref, acc):
        i = pl.program_id(0)

        @pl.when(i == 0)
        def _():
            acc[...] = jnp.zeros_like(acc)

        dyb = dy_ref[...].astype(BF16)
        dm = _dot_nt(dyb, wo_ref[...])
        dg_ref[...] = dm[:, 0:384].astype(BF16)
        dc_ref[...] = dm[:, 384:640].astype(BF16)
        _store_head_padded(da_ref, dm[:, 640:1024].astype(BF16))
        mixed = jnp.concatenate([og_ref[...], oc_ref[...], oa_ref[...]], axis=1)
        acc[...] += _dot_tn(mixed, dyb)

        @pl.when(i == nt - 1)
        def _():
            for j in range(N_DEV):
                dw_ref[j % 2, j // 2] = acc[j * 128:(j + 1) * 128, :].astype(BF16)

    return _pcall(
        body, name="outproj_bwd", grid=(nt,),
        in_specs=[_row_spec(tm, D), _row_spec(tm, 384), _row_spec(tm, 256), _row_spec(tm, 384),
                  _full_spec((D, D))],
        out_specs=[_row_spec(tm, 384), _row_spec(tm, 256), _row_spec(tm, ATT_WIDE), _full_spec((2, 4, 128, D))],
        out_shape=[jax.ShapeDtypeStruct((T, 384), BF16), jax.ShapeDtypeStruct((T, 256), BF16),
                   jax.ShapeDtypeStruct((T, ATT_WIDE), BF16), jax.ShapeDtypeStruct((2, 4, 128, D), BF16)],
        scratch_shapes=[pltpu.VMEM((D, D), F32)],
        sem=("arbitrary",), operands=(dh1, o_gla, o_conv, o_att, w_out), jobs=jobs)


def _loss_fwd_bwd(h, gamma, target):
    T = h.shape[0]
    tm = 512

    def body(h_ref, g_ref, t_ref, loss_ref, dh_ref, dhb_ref, dg_ref):
        @pl.when(pl.program_id(0) == 0)
        def _():
            loss_ref[...] = jnp.zeros_like(loss_ref)
            dg_ref[...] = jnp.zeros_like(dg_ref)

        x = h_ref[...]
        r = lax.rsqrt(jnp.mean(x * x, axis=-1, keepdims=True) + EPS)
        gamma_ = g_ref[...]
        e = x * r * gamma_ - t_ref[...]
        loss_ref[...] += 0.5 * jnp.sum(jnp.mean(e * e, axis=-1, keepdims=True), axis=0, keepdims=True)
        dx, dgam = _rms_bwd(e * (1.0 / D), x, r, gamma_)
        dh_ref[...] = dx
        dhb_ref[...] = dx.astype(BF16)
        dg_ref[...] += dgam

    return pl.pallas_call(
        body, name="loss_fwd_bwd", grid=(T // tm,),
        in_specs=[_row_spec(tm, D), _full_spec((1, D)), _row_spec(tm, D)],
        out_specs=[_full_spec((8, 128)), _row_spec(tm, D), _row_spec(tm, D), _full_spec((1, D))],
        out_shape=[jax.ShapeDtypeStruct((8, 128), F32), jax.ShapeDtypeStruct((T, D), F32),
                   jax.ShapeDtypeStruct((T, D), BF16), jax.ShapeDtypeStruct((1, D), F32)],
        compiler_params=_cp(("arbitrary",)),
    )(h, gamma, target)


def _adamw_math(w, g, m, v):
    m = ADAM_B1 * m + (1.0 - ADAM_B1) * g
    v = ADAM_B2 * v + (1.0 - ADAM_B2) * (g * g)
    m_hat = m / (1.0 - ADAM_B1 ** ADAM_STEP)
    v_hat = v / (1.0 - ADAM_B2 ** ADAM_STEP)
    delta = -ADAM_LR * (m_hat / (jnp.sqrt(v_hat) + ADAM_EPS) + ADAM_WD * w)
    return delta, m, v


def _rs_adamw(a_own, r2, w, m, v, layer, chip_idx, rows_blk, prev=None):
    _, R, C = w.shape
    nblk = R // rows_blk

    def body(chip_ref, a_ref, r_ref, w_ref, m_ref, v_ref, *rest):
        g_out, d_out, m_out, v_out = rest[-4:]
        g = (a_ref[0].astype(F32) + r_ref[0].astype(F32)) + (r_ref[1].astype(F32) + r_ref[2].astype(F32))
        delta, m_new, v_new = _adamw_math(w_ref[0], g, m_ref[0], v_ref[0])
        g_out[0] = g
        d_out[0] = delta
        m_out[0] = m_new
        v_out[0] = v_new

    blk = pl.BlockSpec((1, rows_blk, C), lambda i, chip: (layer, i, 0))
    n_prev = 0 if prev is None else 4
    grid_spec = pltpu.PrefetchScalarGridSpec(
        num_scalar_prefetch=1, grid=(nblk,),
        in_specs=[pl.BlockSpec((1, rows_blk, C), lambda i, chip: (chip[0], i, 0)),
                  pl.BlockSpec((3, rows_blk, C), lambda i, chip: (0, i, 0)), blk, blk, blk]
        + [_any_spec()] * n_prev,
        out_specs=[blk, blk, blk, blk])
    return pl.pallas_call(
        body, name="rs_adamw", grid_spec=grid_spec,
        out_shape=[jax.ShapeDtypeStruct((DEPTH, R, C), F32)] * 4,
        input_output_aliases={6 + t: t for t in range(n_prev)},
        compiler_params=_cp(("arbitrary",)),
    )(chip_idx, a_own, r2, w, m, v, *(prev or ()))


def _pair_sum(g, r1, core_idx, rows_blk):
    _, _, R, C = g.shape
    nblk = R // rows_blk

    def body(core_ref, g_ref, r_ref, o_ref):
        o_ref[...] = (g_ref[0].astype(F32) + r_ref[...].astype(F32)).astype(BF16)

    grid_spec = pltpu.PrefetchScalarGridSpec(
        num_scalar_prefetch=1, grid=(4, nblk),
        in_specs=[pl.BlockSpec((1, 1, rows_blk, C), lambda k, i, core: (core[0], k, i, 0)),
                  pl.BlockSpec((1, rows_blk, C), lambda k, i, core: (k, i, 0))],
        out_specs=pl.BlockSpec((1, rows_blk, C), lambda k, i, core: (k, i, 0)))
    return pl.pallas_call(
        body, name="rs_pair_sum", grid_spec=grid_spec,
        out_shape=jax.ShapeDtypeStruct((4, R, C), BF16),
        compiler_params=_cp(("arbitrary", "arbitrary")),
    )(core_idx, g, r1)


def _small_sum(gathered):
    def body(g_ref, o_ref):
        acc = g_ref[0]
        for d in range(1, N_DEV):
            acc = acc + g_ref[d]
        o_ref[...] = acc

    return pl.pallas_call(
        body, name="small_sum",
        out_shape=jax.ShapeDtypeStruct(gathered.shape[1:], F32),
        compiler_params=_cp(),
    )(gathered)


def _adamw_small(ws, gs, ms, vs):
    n = len(ws)

    def body(*refs):
        w_r, g_r, m_r, v_r = refs[0:n], refs[n:2 * n], refs[2 * n:3 * n], refs[3 * n:4 * n]
        d_o, m_o, v_o = refs[4 * n:5 * n], refs[5 * n:6 * n], refs[6 * n:7 * n]
        for t in range(n):
            delta, m_new, v_new = _adamw_math(w_r[t][...], g_r[t][...], m_r[t][...], v_r[t][...])
            d_o[t][...] = delta
            m_o[t][...] = m_new
            v_o[t][...] = v_new

    shapes = [jax.ShapeDtypeStruct(w.shape, F32) for w in ws]
    outs = pl.pallas_call(
        body, name="adamw_small", out_shape=shapes * 3, compiler_params=_cp(),
    )(*ws, *gs, *ms, *vs)
    return outs[0:n], outs[n:2 * n], outs[2 * n:3 * n]


def _mesh_pos():
    return lax.axis_index("x"), lax.axis_index("y"), lax.axis_index("c")


def _peers():
    x, y, c = _mesh_pos()
    return (x, y, c), (x, y, 1 - c), [(1 - x, y), (x, 1 - y), (1 - x, 1 - y)]


def _slot(ref, pos):
    return ref.at[4 * pos[0] + 2 * pos[1] + pos[2]]


def _remote(src, dst, send_sem, recv_sem, to):
    return pltpu.make_async_remote_copy(src_ref=src, dst_ref=dst, send_sem=send_sem, recv_sem=recv_sem,
                                        device_id=to, device_id_type=MESH)


def _ag_spread(shards):
    n = len(shards)

    def copies(ins, outs, sems):
        send, recv, loc = sems
        me, sibling, chips = _peers()
        peers = [sibling] + [(*chip, me[2]) for chip in chips]
        local = [pltpu.make_async_copy(ins[a], _slot(outs[a], me), loc.at[a]) for a in range(n)]
        sends = [_remote(ins[a], _slot(outs[a], me), send.at[a, k], recv.at[a, k], p)
                 for a in range(n) for k, p in enumerate(peers)]
        recvs = [_remote(ins[a], _slot(outs[a], p), send.at[a, k], recv.at[a, k], p)
                 for a in range(n) for k, p in enumerate(peers)]
        return local, sends, recvs

    def start(ins, outs, sems):
        local, sends, _ = copies(ins, outs, sems)
        for cp in local + sends:
            cp.start()

    def finish(ins, outs, sems):
        local, sends, recvs = copies(ins, outs, sems)
        for cp in sends:
            cp.wait_send()
        for cp in recvs:
            cp.wait_recv()
        for cp in local:
            cp.wait()

    return _Job(shards, [jax.ShapeDtypeStruct((N_DEV,) + a.shape, a.dtype) for a in shards],
                [pltpu.SemaphoreType.DMA((n, 4)), pltpu.SemaphoreType.DMA((n, 4)), pltpu.SemaphoreType.DMA((n,))],
                start, finish)


def _ag_pass(stacks):
    n = len(stacks)

    def copies(ins, outs, sems):
        send, recv = sems
        me, sibling, chips = _peers()
        sends = [_remote(_slot(ins[a], (*chip, me[2])), _slot(outs[a], (*chip, me[2])), send.at[a, j], recv.at[a, j],
                         sibling) for a in range(n) for j, chip in enumerate(chips)]
        recvs = [_remote(_slot(ins[a], (*chip, me[2])), _slot(outs[a], (*chip, 1 - me[2])), send.at[a, j],
                         recv.at[a, j], sibling) for a in range(n) for j, chip in enumerate(chips)]
        return sends, recvs

    def start(ins, outs, sems):
        for cp in copies(ins, outs, sems)[0]:
            cp.start()

    def finish(ins, outs, sems):
        sends, recvs = copies(ins, outs, sems)
        for cp in sends:
            cp.wait_send()
        for cp in recvs:
            cp.wait_recv()

    return _Job(stacks, [jax.ShapeDtypeStruct(a.shape, a.dtype) for a in stacks],
                [pltpu.SemaphoreType.DMA((n, 3)), pltpu.SemaphoreType.DMA((n, 3))],
                start, finish, aliases={a: a for a in range(n)})


def _ag_both(shards):
    spread = _ag_spread(shards)
    fake = [jax.ShapeDtypeStruct((N_DEV,) + a.shape, a.dtype) for a in shards]
    onward = _ag_pass(fake)
    n_sp = len(spread.sems)

    def start(ins, outs, sems):
        spread.start(ins, outs, sems[:n_sp])

    def finish(ins, outs, sems):
        spread.finish(ins, outs, sems[:n_sp])
        onward.start(outs, outs, sems[n_sp:])
        onward.finish(outs, outs, sems[n_sp:])

    return _Job(shards, spread.out_shapes, spread.sems + onward.sems, start, finish)


def _rs_swap(parts):
    n = len(parts)

    def copies(ins, outs, sems):
        send, recv = sems
        me, sibling, _ = _peers()
        return [_remote(ins[a].at[1 - me[2]], outs[a], send.at[a], recv.at[a], sibling) for a in range(n)]

    def start(ins, outs, sems):
        for cp in copies(ins, outs, sems):
            cp.start()

    def finish(ins, outs, sems):
        for cp in copies(ins, outs, sems):
            cp.wait()

    return _Job(parts, [jax.ShapeDtypeStruct(a.shape[1:], a.dtype) for a in parts],
                [pltpu.SemaphoreType.DMA((n,)), pltpu.SemaphoreType.DMA((n,))], start, finish)


def _rs_ici(pairs):
    n = len(pairs)

    def copies(ins, outs, sems):
        send, recv = sems
        me, _, chips = _peers()
        return [_remote(ins[a].at[2 * chip[0] + chip[1]], outs[a].at[j], send.at[a, j], recv.at[a, j],
                        (*chip, me[2])) for a in range(n) for j, chip in enumerate(chips)]

    def start(ins, outs, sems):
        for cp in copies(ins, outs, sems):
            cp.start()

    def finish(ins, outs, sems):
        for cp in copies(ins, outs, sems):
            cp.wait()

    return _Job(pairs, [jax.ShapeDtypeStruct((3,) + a.shape[1:], a.dtype) for a in pairs],
                [pltpu.SemaphoreType.DMA((n, 3)), pltpu.SemaphoreType.DMA((n, 3))], start, finish)


def _comm_call(jobs, name):
    def body():
        pass

    return _pcall(body, name=name, grid=(), in_specs=[], out_specs=[], out_shape=[], operands=(), jobs=jobs)[1]


def _allgather(arrs, name):
    n = len(arrs)

    def body(*refs):
        ins, outs = refs[:n], refs[n:2 * n]
        send_sems, recv_sems, local_sems = refs[2 * n:]
        x, y, c = _mesh_pos()
        me, sibling = (x, y, c), (x, y, 1 - c)
        chips = [(1 - x, y), (x, 1 - y), (1 - x, 1 - y)]

        def slot(a, pos):
            return outs[a].at[4 * pos[0] + 2 * pos[1] + pos[2]]

        def copy(a, k, block, to, src=None):
            return pltpu.make_async_remote_copy(
                src_ref=slot(a, block) if src is None else src, dst_ref=slot(a, block),
                send_sem=send_sems.at[a, k], recv_sem=recv_sems.at[a, k],
                device_id=to, device_id_type=MESH)

        mine = [pltpu.make_async_copy(ins[a], slot(a, me), local_sems.at[a]) for a in range(n)]
        for cp in mine:
            cp.start()
        first = []
        for a in range(n):
            first.append(copy(a, 0, me, sibling, src=ins[a]))
            first += [copy(a, 1 + j, me, (*chip, c), src=ins[a]) for j, chip in enumerate(chips)]
        for cp in first:
            cp.start()
        passed = []
        for j, chip in enumerate(chips):
            for a in range(n):
                copy(a, 1 + j, (*chip, c), me).wait_recv()
                fwd = copy(a, 4 + j, (*chip, c), sibling)
                fwd.start()
                passed.append(fwd)
        for a in range(n):
            copy(a, 0, sibling, me).wait_recv()
            for j, chip in enumerate(chips):
                copy(a, 4 + j, (*chip, 1 - c), me).wait_recv()
        for cp in first + passed:
            cp.wait_send()
        for cp in mine:
            cp.wait()

    return pl.pallas_call(
        body, name=name,
        in_specs=[_any_spec()] * n, out_specs=[_any_spec()] * n,
        out_shape=[jax.ShapeDtypeStruct((N_DEV,) + a.shape, a.dtype) for a in arrs],
        scratch_shapes=[pltpu.SemaphoreType.DMA((n, 7)), pltpu.SemaphoreType.DMA((n, 7)),
                        pltpu.SemaphoreType.DMA((n,))],
        compiler_params=_cp(),
    )(*arrs)


W_IN_SHARD = 354
W_IN_COLS = ((0, 192, OQ), (192, 192, OKK), (384, 384, OV), (768, 384, OG), (1152, 16, OLR), (1168, 512, OCU),
             (1680, 384, OAQ), (2064, 384, OAK), (2448, 384, OAV))


def _w_in_padded(stack):
    new_to_ref = {new: (start, width) for start, width, new in W_IN_COLS}
    cols = []
    for new, padded in IN_GROUPS:
        start, width = new_to_ref[new]
        a = start
        while a < start + width:
            j = a // W_IN_SHARD
            b = min(start + width, (j + 1) * W_IN_SHARD)
            cols.append(stack[j][:, a - j * W_IN_SHARD:b - j * W_IN_SHARD])
            a = b
        if padded > width:
            cols.append(jnp.zeros((stack.shape[1], padded - width), stack.dtype))
    return jnp.concatenate(cols, axis=1)


def _dw_in_shards(dw):
    shards = []
    for j in range(N_DEV):
        lo, hi = j * W_IN_SHARD, (j + 1) * W_IN_SHARD
        segs = []
        for start, width, new in W_IN_COLS:
            a, b = max(lo, start), min(hi, start + width)
            if a < b:
                segs.append(dw[:, new + a - start:new + b - start])
        shards.append(jnp.concatenate(segs, axis=1))
    return jnp.stack([jnp.stack([shards[2 * chip + core] for chip in range(4)]) for core in range(2)])


def _pad_to(a, shape):
    return jnp.pad(a, [(0, s - d) for d, s in zip(a.shape, shape)])


SMALL_LAYOUT = (
    ("norm_mix", 2, 1024), ("norm_ffn", 2, 1024), ("norm_final", 1, 1024), ("gla_norm", 2, 384),
    ("b_gla_gate", 2, 192), ("b_dw", 2, 256), ("conv_ln_g", 2, 256), ("conv_ln_b", 2, 256),
    ("rel_bias", 12, 257), ("w_gla_gate", 32, 192), ("w_dw", 62, 256),
)
SMALL_LANES = 128
SMALL_TILE = 8 * SMALL_LANES


def _small_tile_rows(r, lanes):
    return -(-(r * lanes) // SMALL_TILE) * 8


SMALL_ROWS = sum(_small_tile_rows(r, lanes) for _, r, lanes in SMALL_LAYOUT)


def _pack_small(parts):
    tiles = []
    for name, r, lanes in SMALL_LAYOUT:
        rows = _small_tile_rows(r, lanes)
        flat = _pad_to(parts[name].reshape(r * lanes), (rows * SMALL_LANES,))
        tiles.append(flat.reshape(rows, SMALL_LANES))
    return jnp.concatenate(tiles, axis=0)


def _unpack_small(packed):
    out, r0 = {}, 0
    for name, r, lanes in SMALL_LAYOUT:
        rows = _small_tile_rows(r, lanes)
        out[name] = packed[r0:r0 + rows].reshape(rows * SMALL_LANES)[:r * lanes].reshape(r, lanes)
        r0 += rows
    return out


def _mixers_fwd(h, wl, w_in_p, plan=None):
    plan, res = plan or {}, {}

    def jobs(host):
        return plan[host](res) if host in plan else ()

    (q, k, v, g, cu, aq, ak, av, lr), res["inproj"] = _inproj_fwd(h, wl["norm_mix"], w_in_p, jobs=jobs("inproj"))
    bias = _relbias_expand(wl["rb"])
    (o_att, lse), res["att"] = _att_fwd(aq, ak, av, bias, jobs=jobs("att"))
    (o_gla, states), res["gla"] = _gla_fwd(q, k, v, g, lr, wl["wg"], wl["bg"], wl["gn"], jobs=jobs("gla"))
    (o_conv, conv), res["conv"] = _conv_fwd(cu, wl["w_dw"], wl["b_dw"], wl["ln_g"], wl["ln_b"], jobs=jobs("conv"))
    sv = dict(h=h, w_in=w_in_p, q=q, k=k, v=v, g=g, cu=cu, aq=aq, ak=ak, av=av, lr=lr,
              o_gla=o_gla, o_conv=o_conv, conv=conv, o_att=o_att, lse=lse, states=states, bias=bias)
    return sv, res


def _mixers_bwd(sv, wl, dh1, d_ogla, d_oconv, att_grads, conv_jobs=(), x_jobs_fn=None):
    daq, dak, dav, dbias = att_grads
    d_rb = _relbias_grad(dbias)
    (dcu, dw_dw, db_dw, dln_g, dln_b), conv_res = _conv_bwd(
        sv["cu"], sv["conv"], d_oconv, wl["w_dw"], wl["b_dw"], wl["ln_g"], wl["ln_b"], jobs=conv_jobs)
    dq, dk, dv, dg, dlr, dwg, dbg, dgn = _gla_bwd(sv["q"], sv["k"], sv["v"], sv["g"], sv["lr"], sv["states"],
                                                  d_ogla, wl["wg"], wl["bg"], wl["gn"])
    dparts = (dq, dk, dv, dg, dcu, daq, dak, dav, dlr)
    dw_in = _inproj_bwd_w(sv["h"], wl["norm_mix"], dparts)
    x_jobs = x_jobs_fn(dw_in) if x_jobs_fn is not None else ()
    (dh, dhb, d_nmix), x_res = _inproj_bwd_x(sv["h"], dh1, wl["norm_mix"], sv["w_in"], dparts, jobs=x_jobs)
    small = dict(norm_mix=d_nmix, wg=dwg, bg=dbg, gn=dgn, w_dw=dw_dw, b_dw=db_dw, ln_g=dln_g, ln_b=dln_b, rb=d_rb)
    return (dh, dhb), dw_in, small, conv_res, x_res


def _layer_small(l, w_dw_full, norm_mix, w_gla_gate, b_gla_gate, gla_norm, b_dw, conv_ln_g, conv_ln_b, rel_bias,
                 norm_ffn):
    return dict(
        norm_mix=norm_mix[l][None, :], norm_ffn=norm_ffn[l][None, :],
        wg=_pad_to(w_gla_gate[l], (128, 256)).astype(BF16), bg=_pad_to(b_gla_gate[l][None, :], (1, 256)),
        gn=gla_norm[l][None, :], w_dw=_pad_to(w_dw_full, (32, 256)), b_dw=b_dw[l][None, :],
        ln_g=conv_ln_g[l][None, :], ln_b=conv_ln_b[l][None, :], rb=_pad_to(rel_bias[l], (8, 384)))


RS_ROWS = dict(w_in=512, w_out=128, w_up=512, w_down=256)


def kernel(x, norm_mix, w_in, w_gla_gate, b_gla_gate, gla_norm, w_dw, b_dw, conv_ln_g, conv_ln_b, rel_bias, w_out, norm_ffn, w_up, w_down, norm_final, loss_target, m_norm_mix, m_w_in, m_w_gla_gate, m_b_gla_gate, m_gla_norm, m_w_dw, m_b_dw, m_conv_ln_g, m_conv_ln_b, m_rel_bias, m_w_out, m_norm_ffn, m_w_up, m_w_down, m_norm_final, v_norm_mix, v_w_in, v_w_gla_gate, v_b_gla_gate, v_gla_norm, v_w_dw, v_b_dw, v_conv_ln_g, v_conv_ln_b, v_rel_bias, v_w_out, v_norm_ffn, v_w_up, v_w_down, v_norm_final):
    mx, my, mc = _mesh_pos()
    me = 4 * mx + 2 * my + mc
    chip_idx = (2 * mx + my).astype(jnp.int32).reshape(1)
    core_idx = mc.astype(jnp.int32).reshape(1)
    x0, target = x[0], loss_target[0]

    def pair_sums(parts, r1):
        return [_pair_sum(p, r, core_idx, p.shape[2]) for p, r in zip(parts, r1)]

    sh = [dict(w_in=w_in[l].astype(BF16), w_out=w_out[l].astype(BF16), w_up=w_up[l].astype(BF16),
               w_down=w_down[l].astype(BF16)) for l in range(DEPTH)]
    dw_flat = _pad_to(w_dw, (DEPTH, 32, 32)).reshape(16, 128)
    st_in0, st_dw = _allgather([sh[0]["w_in"], dw_flat], "allgather_first")
    dw_all = st_dw.reshape(N_DEV, DEPTH, 32, 32)[:, :, :KCONV, :]
    dw_all = jnp.transpose(dw_all, (1, 2, 0, 3)).reshape(DEPTH, KCONV, 256)
    wl = [_layer_small(l, dw_all[l], norm_mix, w_gla_gate, b_gla_gate, gla_norm, b_dw, conv_ln_g, conv_ln_b,
                       rel_bias, norm_ffn) for l in range(DEPTH)]

    s0, s1 = sh[0], sh[1]
    half = s0["w_down"].shape[0] // 2
    down0_a, down0_b = s0["w_down"][:half], s0["w_down"][half:]
    sv0, g0 = _mixers_fwd(x0, wl[0], _w_in_padded(st_in0), plan=dict(
        inproj=lambda r: [_ag_spread([s0["w_out"], down0_a])],
        att=lambda r: [_ag_spread([s0["w_up"]]), _ag_pass(r["inproj"])],
        gla=lambda r: [_ag_spread([down0_b]), _ag_pass(r["att"][:1])],
        conv=lambda r: [_ag_pass(r["gla"][:1])]))
    st_out0, st_down0_a = g0["att"][1:]
    st_up0, st_down0_b = g0["gla"][1], g0["conv"][0]
    st_down0 = jnp.concatenate([st_down0_a, st_down0_b], axis=1)
    wo0 = st_out0.reshape(D, D)
    (h1_0, xn2t_0, h2_0, act_0), (st_in1, out1_half) = _outproj_mlp_fwd(
        x0, sv0["o_gla"], sv0["o_conv"], sv0["o_att"], wo0, wl[0]["norm_ffn"], st_up0, st_down0,
        jobs=[_ag_both([s1["w_in"]]), _ag_spread([s1["w_out"]])])

    sv1, g1 = _mixers_fwd(h2_0, wl[1], _w_in_padded(st_in1), plan=dict(
        inproj=lambda r: [_ag_spread([s1["w_up"]]), _ag_pass([out1_half])],
        att=lambda r: [_ag_spread([s1["w_down"]]), _ag_pass(r["inproj"][:1])],
        gla=lambda r: [_ag_pass(r["att"][:1])]))
    st_out1, st_up1, st_down1 = g1["inproj"][1], g1["att"][1], g1["gla"][0]
    wo1 = st_out1.reshape(D, D)
    (h1_1, xn2t_1, h2_1, act_1), _ = _outproj_mlp_fwd(
        h2_0, sv1["o_gla"], sv1["o_conv"], sv1["o_att"], wo1, wl[1]["norm_ffn"], st_up1, st_down1)

    loss8, dh, dhb, d_nf = _loss_fwd_bwd(h2_1, norm_final[None, :], target)
    loss = lax.psum(loss8[0, 0], ("x", "y", "c"))

    def layer_bwd(dh_pair, sv, wl_l, xn2t, act, h1, wo, st_up, st_down, mlp_jobs, x_jobs_fn):
        (dh1, dw_up, dw_down, d_nffn), mlp_res = _mlp_bwd(xn2t, act, h1, dh_pair[0], dh_pair[1], wl_l["norm_ffn"],
                                                           st_up, st_down, jobs=mlp_jobs)
        ud = [dw_up, dw_down]
        (d_ogla, d_oconv, d_oatt, dw_out), r1 = _outproj_bwd(
            dh1, sv["o_gla"], sv["o_conv"], sv["o_att"], wo, jobs=[_rs_swap(ud)])
        pair_ud = pair_sums(ud, r1)
        att_grads, r = _att_bwd(sv["aq"], sv["ak"], sv["av"], sv["bias"], sv["o_att"], sv["lse"], d_oatt,
                                jobs=[_rs_ici(pair_ud), _rs_swap([dw_out])])
        r2_ud, r1_out = r[:2], r[2:]
        pair_out = pair_sums([dw_out], r1_out)
        dh_in, _, small, r2_out, x_res = _mixers_bwd(sv, wl_l, dh1, d_ogla, d_oconv, att_grads,
                                                     conv_jobs=[_rs_ici(pair_out)], x_jobs_fn=x_jobs_fn)
        small["norm_ffn"] = d_nffn
        sums = dict(w_out=(pair_out[0], r2_out[0]), w_up=(pair_ud[0], r2_ud[0]), w_down=(pair_ud[1], r2_ud[1]))
        return dh_in, small, sums, mlp_res, x_res

    stash = {}

    def swap_w_in(dw_in):
        stash["in1"] = [_dw_in_shards(dw_in)]
        return [_rs_swap(stash["in1"])]

    dh_pair, small1, sums1, _, r1_in1 = layer_bwd((dh, dhb), sv1, wl[1], xn2t_1, act_1, h1_1, wo1, st_up1, st_down1,
                                                  (), swap_w_in)
    pair_in1 = pair_sums(stash["in1"], r1_in1)

    def send_w_in(dw_in):
        in0 = [_dw_in_shards(dw_in)]
        stash["pair_in0"] = pair_sums(in0, _comm_call([_rs_swap(in0)], "rs_swap_w_in_0"))
        return [_rs_ici(stash["pair_in0"])]

    (dx, _), small0, sums0, r2_in1, r2_in0 = layer_bwd(dh_pair, sv0, wl[0], xn2t_0, act_0, h1_0, wo0, st_up0, st_down0,
                                                       [_rs_ici(pair_in1)], send_w_in)
    sums1["w_in"] = (pair_in1[0], r2_in1[0])
    sums0["w_in"] = (stash["pair_in0"][0], r2_in0[0])

    big_w = dict(w_in=(w_in, m_w_in, v_w_in), w_out=(w_out, m_w_out, v_w_out), w_up=(w_up, m_w_up, v_w_up),
                 w_down=(w_down, m_w_down, v_w_down))
    pairs = {1: sums1, 0: sums0}
    big_out = {}
    for name, (w_, m_, v_) in big_w.items():
        res = None
        for l in (1, 0):
            a_own, r2_ = pairs[l][name]
            res = _rs_adamw(a_own, r2_, w_, m_, v_, l, chip_idx, RS_ROWS[name], prev=res)
        big_out[name] = res

    grads = (small0, small1)
    parts = dict(
        norm_mix=jnp.concatenate([grads[l]["norm_mix"] for l in range(DEPTH)], axis=0),
        norm_ffn=jnp.concatenate([grads[l]["norm_ffn"] for l in range(DEPTH)], axis=0),
        norm_final=d_nf,
        gla_norm=jnp.concatenate([grads[l]["gn"] for l in range(DEPTH)], axis=0),
        b_gla_gate=jnp.concatenate([grads[l]["bg"][:, :192] for l in range(DEPTH)], axis=0),
        b_dw=jnp.concatenate([grads[l]["b_dw"] for l in range(DEPTH)], axis=0),
        conv_ln_g=jnp.concatenate([grads[l]["ln_g"] for l in range(DEPTH)], axis=0),
        conv_ln_b=jnp.concatenate([grads[l]["ln_b"] for l in range(DEPTH)], axis=0),
        rel_bias=jnp.concatenate([grads[l]["rb"][:6, :N_REL] for l in range(DEPTH)], axis=0),
        w_gla_gate=jnp.concatenate([grads[l]["wg"][:16, :192] for l in range(DEPTH)], axis=0),
        w_dw=jnp.concatenate([grads[l]["w_dw"][:KCONV] for l in range(DEPTH)], axis=0),
    )
    small_all = _allgather([_pack_small(parts)], "allgather_small")[0]
    sg = _unpack_small(_small_sum(small_all))
    dw_grad = lax.dynamic_slice_in_dim(sg["w_dw"].reshape(DEPTH, KCONV, 256), me * 32, 32, axis=2)
    small_g = dict(
        norm_mix=sg["norm_mix"], w_gla_gate=sg["w_gla_gate"].reshape(DEPTH, 16, 192), b_gla_gate=sg["b_gla_gate"],
        gla_norm=sg["gla_norm"], w_dw=dw_grad, b_dw=sg["b_dw"], conv_ln_g=sg["conv_ln_g"],
        conv_ln_b=sg["conv_ln_b"], rel_bias=sg["rel_bias"].reshape(DEPTH, 6, N_REL), norm_ffn=sg["norm_ffn"],
        norm_final=sg["norm_final"].reshape(D))
    small_names = ("norm_mix", "w_gla_gate", "b_gla_gate", "gla_norm", "w_dw", "b_dw", "conv_ln_g", "conv_ln_b",
                   "rel_bias", "norm_ffn", "norm_final")
    small_w = dict(norm_mix=norm_mix, w_gla_gate=w_gla_gate, b_gla_gate=b_gla_gate, gla_norm=gla_norm, w_dw=w_dw,
                   b_dw=b_dw, conv_ln_g=conv_ln_g, conv_ln_b=conv_ln_b, rel_bias=rel_bias, norm_ffn=norm_ffn,
                   norm_final=norm_final)
    small_m = dict(norm_mix=m_norm_mix, w_gla_gate=m_w_gla_gate, b_gla_gate=m_b_gla_gate, gla_norm=m_gla_norm,
                   w_dw=m_w_dw, b_dw=m_b_dw, conv_ln_g=m_conv_ln_g, conv_ln_b=m_conv_ln_b, rel_bias=m_rel_bias,
                   norm_ffn=m_norm_ffn, norm_final=m_norm_final)
    small_v = dict(norm_mix=v_norm_mix, w_gla_gate=v_w_gla_gate, b_gla_gate=v_b_gla_gate, gla_norm=v_gla_norm,
                   w_dw=v_w_dw, b_dw=v_b_dw, conv_ln_g=v_conv_ln_g, conv_ln_b=v_conv_ln_b, rel_bias=v_rel_bias,
                   norm_ffn=v_norm_ffn, norm_final=v_norm_final)
    s_delta, s_m, s_v = _adamw_small([small_w[n] for n in small_names], [small_g[n] for n in small_names],
                                     [small_m[n] for n in small_names], [small_v[n] for n in small_names])
    s_idx = {n: t for t, n in enumerate(small_names)}

    order = ("norm_mix", "w_in", "w_gla_gate", "b_gla_gate", "gla_norm", "w_dw", "b_dw", "conv_ln_g", "conv_ln_b",
             "rel_bias", "w_out", "norm_ffn", "w_up", "w_down", "norm_final")

    def pick(kind, name):
        if name in big_out:
            return big_out[name][kind]
        t = s_idx[name]
        return (small_g[name], s_delta[t], s_m[t], s_v[t])[kind]

    outs = [loss, dx[None]]
    for kind in range(4):
        outs += [pick(kind, n) for n in order]
    return tuple(outs)
```

```python
import functools

import jax
import jax.numpy as jnp
from jax import lax
from jax.experimental import pallas as pl
from jax.experimental.pallas import tpu as pltpu

F32 = jnp.float32
BF16 = jnp.bfloat16
MESH = pl.DeviceIdType.MESH

D = 1024
DEPTH = 2
CH = 64
EPS = 1e-6
NEG = -1e30
N_DEV = 8
N_REL = 257
Q_SCALE = 48.0 ** -0.5
A_SCALE = 64.0 ** -0.5
GATE_TAU = 16.0
KCONV = 31

OQ, OKK, OV, OG, OCU, OAQ, OAK, OAV, OLR, DINP = 0, 256, 512, 896, 1280, 1792, 2176, 2560, 2944, 3072
IN_GROUPS = ((OQ, 256), (OKK, 256), (OV, 384), (OG, 384), (OCU, 512), (OAQ, 384), (OAK, 384), (OAV, 384), (OLR, 128))

AQ_BLK = 256
AK_WIN = 768
WIN_LEFT = 2
RB_W = 1536

ADAM_LR, ADAM_B1, ADAM_B2, ADAM_EPS, ADAM_WD, ADAM_STEP = 0.001, 0.9, 0.999, 1e-08, 0.01, 10


V7X_VMEM_MIB = 64
VMEM_LIMIT_MIB = V7X_VMEM_MIB - 1


def _cp(sem=None):
    kw = {"vmem_limit_bytes": VMEM_LIMIT_MIB * 1024 * 1024}
    if sem is not None:
        kw["dimension_semantics"] = sem
    return pltpu.CompilerParams(**kw)


def _dot(a, b):
    return jnp.dot(a, b, preferred_element_type=F32)


def _dot_nt(a, b):
    return lax.dot_general(a, b, (((1,), (1,)), ((), ())), preferred_element_type=F32)


def _dot_tn(a, b):
    return lax.dot_general(a, b, (((0,), (0,)), ((), ())), preferred_element_type=F32)


def _split2(a):
    hi = a.astype(BF16)
    lo = (a - hi.astype(F32)).astype(BF16)
    return hi, lo


def _split3(a):
    hi = a.astype(BF16)
    r1 = a - hi.astype(F32)
    mid = r1.astype(BF16)
    lo = (r1 - mid.astype(F32)).astype(BF16)
    return hi, mid, lo


def _sigmoid(x):
    return 1.0 / (1.0 + jnp.exp(-x))


def _group(idx, size, n):
    g = jnp.zeros_like(idx)
    for t in range(1, n):
        g = g + (idx >= t * size).astype(jnp.int32)
    return g


def _rms_bwd(dy, x, r, gamma):
    xh = x * r
    dxh = dy * gamma
    dx = r * (dxh - xh * jnp.mean(dxh * xh, axis=-1, keepdims=True))
    return dx, jnp.sum(dy * xh, axis=0, keepdims=True)


def _row_spec(tm, n):
    return pl.BlockSpec((tm, n), lambda i: (i, 0))


def _full_spec(shape):
    nd = len(shape)
    return pl.BlockSpec(shape, lambda *_: (0,) * nd)


def _any_spec():
    return pl.BlockSpec(memory_space=pl.ANY)


class _Job:
    def __init__(self, operands, out_shapes, sems, start, finish, aliases=None):
        self.operands, self.out_shapes, self.sems = list(operands), list(out_shapes), list(sems)
        self.start, self.finish, self.aliases = start, finish, dict(aliases or {})


def _pcall(body, *, name, grid, in_specs, out_specs, out_shape, operands, scratch_shapes=(), sem=None, jobs=()):
    jobs = list(jobs)
    in_specs, out_specs, out_shape = list(in_specs), list(out_specs), list(out_shape)
    scratch_shapes = list(scratch_shapes)
    n_in, n_out, n_scr = len(in_specs), len(out_specs), len(scratch_shapes)
    j_in = [a for j in jobs for a in j.operands]
    j_out = [s for j in jobs for s in j.out_shapes]
    j_sem = [s for j in jobs for s in j.sems]
    aliases, io, oo = {}, n_in, n_out
    for j in jobs:
        for a, b in j.aliases.items():
            aliases[io + a] = oo + b
        io += len(j.operands)
        oo += len(j.out_shapes)

    def wrapped(*refs):
        own_in, ji = refs[:n_in], refs[n_in:n_in + len(j_in)]
        o0 = n_in + len(j_in)
        own_out, jo = refs[o0:o0 + n_out], refs[o0 + n_out:o0 + n_out + len(j_out)]
        s0 = o0 + n_out + len(j_out)
        own_scr, js = refs[s0:s0 + n_scr], refs[s0 + n_scr:]

        def each_job(fn_name):
            a = b = c = 0
            for j in jobs:
                na, nb, nc = len(j.operands), len(j.out_shapes), len(j.sems)
                getattr(j, fn_name)(ji[a:a + na], jo[b:b + nb], js[c:c + nc])
                a, b, c = a + na, b + nb, c + nc

        if jobs and grid:
            pids = [pl.program_id(d) for d in range(len(grid))]
            first = functools.reduce(jnp.logical_and, [p == 0 for p in pids])
            last = functools.reduce(jnp.logical_and, [p == g - 1 for p, g in zip(pids, grid)])
            pl.when(first)(lambda: each_job("start"))
        elif jobs:
            each_job("start")

        body(*own_in, *own_out, *own_scr)

        if jobs and grid:
            pl.when(last)(lambda: each_job("finish"))
        elif jobs:
            each_job("finish")

    res = pl.pallas_call(
        wrapped, name=name, grid=grid,
        in_specs=in_specs + [_any_spec()] * len(j_in), out_specs=out_specs + [_any_spec()] * len(j_out),
        out_shape=out_shape + j_out, scratch_shapes=scratch_shapes + j_sem,
        input_output_aliases=aliases, compiler_params=_cp(sem),
    )(*operands, *j_in)
    return res[:n_out], res[n_out:]


ATT_HEADS = 6
HEAD_PAD = 128
ATT_WIDE = ATT_HEADS * HEAD_PAD
ATT_GROUP_OFFS = (OAQ, OAK, OAV)


def _store_head_padded(o_ref, part):
    o_ref[...] = jnp.zeros_like(o_ref)
    for hd in range(ATT_HEADS):
        o_ref[:, hd * HEAD_PAD:hd * HEAD_PAD + 64] = part[:, hd * 64:(hd + 1) * 64]


def _inproj_fwd(h, gamma, w, jobs=()):
    T = h.shape[0]
    tm = 512

    def body(h_ref, g_ref, w_ref, *outs):
        x = h_ref[...]
        r = lax.rsqrt(jnp.mean(x * x, axis=-1, keepdims=True) + EPS)
        xn = (x * r * g_ref[...]).astype(BF16)
        p = _dot(xn, w_ref[...])
        for o_ref, (off, n) in zip(outs, IN_GROUPS):
            part = p[:, off:off + n].astype(BF16)
            if off in ATT_GROUP_OFFS:
                _store_head_padded(o_ref, part)
            else:
                o_ref[...] = part

    widths = [ATT_WIDE if off in ATT_GROUP_OFFS else n for off, n in IN_GROUPS]
    return _pcall(
        body, name="inproj_fwd", grid=(T // tm,),
        in_specs=[_row_spec(tm, D), _full_spec((1, D)), _full_spec((D, DINP))],
        out_specs=[_row_spec(tm, n) for n in widths],
        out_shape=[jax.ShapeDtypeStruct((T, n), BF16) for n in widths],
        sem=("arbitrary",), operands=(h, gamma, w), jobs=jobs)


def _inproj_norm(h_ref, g_ref):
    x = h_ref[...]
    r = lax.rsqrt(jnp.mean(x * x, axis=-1, keepdims=True) + EPS)
    return x, r, g_ref[...]


def _inproj_bwd_w(h, gamma, dparts):
    T = h.shape[0]
    tm = 512
    nt = T // tm

    def body(h_ref, g_ref, *rest):
        dp_refs = rest[:9]
        dw_ref, acc = rest[9:]
        i = pl.program_id(0)

        @pl.when(i == 0)
        def _():
            acc[...] = jnp.zeros_like(acc)

        x, r, gamma_ = _inproj_norm(h_ref, g_ref)
        xnt = jnp.transpose((x * r * gamma_).astype(BF16))
        acc[...] += _dot(xnt, jnp.concatenate([d_ref[...] for d_ref in dp_refs], axis=1))

        @pl.when(i == nt - 1)
        def _():
            dw_ref[...] = acc[...].astype(BF16)

    return pl.pallas_call(
        body, name="inproj_bwd_w", grid=(nt,),
        in_specs=[_row_spec(tm, D), _full_spec((1, D))] + [_row_spec(tm, n) for _, n in IN_GROUPS],
        out_specs=_full_spec((D, DINP)),
        out_shape=jax.ShapeDtypeStruct((D, DINP), BF16),
        scratch_shapes=[pltpu.VMEM((D, DINP), F32)],
        compiler_params=_cp(("arbitrary",)),
    )(h, gamma, *dparts)


def _inproj_bwd_x(h, dh_in, gamma, w, dparts, jobs=()):
    T = h.shape[0]
    tm = 512

    def body(h_ref, dhin_ref, g_ref, w_ref, *rest):
        dp_refs = rest[:9]
        dh_ref, dhb_ref, dg_ref = rest[9:]

        @pl.when(pl.program_id(0) == 0)
        def _():
            dg_ref[...] = jnp.zeros_like(dg_ref)

        x, r, gamma_ = _inproj_norm(h_ref, g_ref)
        dxn = _dot_nt(jnp.concatenate([d_ref[...] for d_ref in dp_refs], axis=1), w_ref[...])
        dx, dgam = _rms_bwd(dxn, x, r, gamma_)
        dh = dhin_ref[...] + dx
        dh_ref[...] = dh
        dhb_ref[...] = dh.astype(BF16)
        dg_ref[...] += dgam

    return _pcall(
        body, name="inproj_bwd_x", grid=(T // tm,),
        in_specs=[_row_spec(tm, D), _row_spec(tm, D), _full_spec((1, D)), _full_spec((D, DINP))]
        + [_row_spec(tm, n) for _, n in IN_GROUPS],
        out_specs=[_row_spec(tm, D), _row_spec(tm, D), _full_spec((1, D))],
        out_shape=[jax.ShapeDtypeStruct((T, D), F32), jax.ShapeDtypeStruct((T, D), BF16),
                   jax.ShapeDtypeStruct((1, D), F32)],
        sem=("arbitrary",), operands=(h, dh_in, gamma, w, *dparts), jobs=jobs)


GLA_ROWS = 512
GLA_NC = GLA_ROWS // CH


def _gla_consts():
    ri = lax.broadcasted_iota(jnp.int32, (CH, CH), 0)
    ci = lax.broadcasted_iota(jnp.int32, (CH, CH), 1)
    upper = (ci > ri).astype(BF16)
    vv = lax.broadcasted_iota(jnp.int32, (384, 256), 0)
    kk = lax.broadcasted_iota(jnp.int32, (384, 256), 1)
    mask_t = ((_group(vv, 96, 4) == _group(kk, 48, 4)) & (kk < 192)).astype(F32)
    pi = lax.broadcasted_iota(jnp.int32, (384, 384), 0)
    pj = lax.broadcasted_iota(jnp.int32, (384, 384), 1)
    same_head = (_group(pi, 96, 4) == _group(pj, 96, 4)).astype(BF16)
    return upper, mask_t, same_head


def _gla_gate(lr_ref, wg_ref, bg_ref):
    z = _dot(lr_ref[...], wg_ref[...]) + bg_ref[...]
    la = (jnp.minimum(z, 0.0) - jnp.log(1.0 + jnp.exp(-jnp.abs(z)))) * (1.0 / GATE_TAU)
    return z, la


def _gla_chunk_decay(la_c, upper):
    hi, lo = _split2(la_c)
    dec = _dot(upper, hi) + _dot(upper, lo)
    end = jnp.sum(la_c, axis=0, keepdims=True)
    return jnp.exp(dec), jnp.exp(end)


def _head_mean(x, same_head):
    hi, lo = _split2(x)
    return (_dot(hi, same_head) + _dot(lo, same_head)) * (1.0 / 96.0)


def _gla_fwd(q, k, v, g, lr, wg, bg, gn, jobs=()):
    T = q.shape[0]
    nb = T // GLA_ROWS

    def body(q_ref, k_ref, v_ref, g_ref, lr_ref, wg_ref, bg_ref, gn_ref, y_ref, st_ref, s_scr, o_scr, kv_scr):
        upper, mask_t, same_head = _gla_consts()

        @pl.when(pl.program_id(0) == 0)
        def _():
            s_scr[...] = jnp.zeros_like(s_scr)

        _, la = _gla_gate(lr_ref, wg_ref, bg_ref)
        decays = []
        for c in range(GLA_NC):
            rs = slice(c * CH, (c + 1) * CH)
            w, a = _gla_chunk_decay(la[rs], upper)
            decays.append(a)
            kd = (k_ref[rs, :].astype(F32) * w).astype(BF16)
            kv_scr[c] = _dot_tn(v_ref[rs, :], kd) * mask_t
        for c in range(GLA_NC):
            s_new = s_scr[...] * decays[c] + kv_scr[c]
            s_scr[...] = s_new
            st_ref[c] = s_new.astype(BF16)
        for c in range(GLA_NC):
            rs = slice(c * CH, (c + 1) * CH)
            qs = (q_ref[rs, :].astype(F32) * Q_SCALE).astype(BF16)
            o_scr[rs, :] = _dot_nt(qs, st_ref[c])
        o = o_scr[...]
        r = lax.rsqrt(_head_mean(o * o, same_head) + EPS)
        gf = g_ref[...].astype(F32)
        y_ref[...] = (o * r * gn_ref[...] * (gf * _sigmoid(gf))).astype(BF16)

    return _pcall(
        body, name="gla_fwd", grid=(nb,),
        in_specs=[_row_spec(GLA_ROWS, 256), _row_spec(GLA_ROWS, 256), _row_spec(GLA_ROWS, 384),
                  _row_spec(GLA_ROWS, 384), _row_spec(GLA_ROWS, 128),
                  _full_spec((128, 256)), _full_spec((1, 256)), _full_spec((1, 384))],
        out_specs=[_row_spec(GLA_ROWS, 384), pl.BlockSpec((GLA_NC, 384, 256), lambda i: (i, 0, 0))],
        out_shape=[jax.ShapeDtypeStruct((T, 384), BF16), jax.ShapeDtypeStruct((T // CH, 384, 256), BF16)],
        scratch_shapes=[pltpu.VMEM((384, 256), F32), pltpu.VMEM((GLA_ROWS, 384), F32),
                        pltpu.VMEM((GLA_NC, 384, 256), F32)],
        sem=("arbitrary",), operands=(q, k, v, g, lr, wg, bg, gn), jobs=jobs)


def _gla_bwd(q, k, v, g, lr, states, dy, wg, bg, gn):
    T = q.shape[0]
    nb = T // GLA_ROWS

    def rev(s):
        return nb - 1 - s

    def body(q_ref, k_ref, v_ref, g_ref, lr_ref, st_ref, stp_ref, dy_ref, wg_ref, bg_ref, gn_ref,
             dq_ref, dk_ref, dv_ref, dg_ref, dlr_ref, dwg_ref, dbg_ref, dgn_ref,
             d_scr, an_scr, o_scr, do_scr, dla_scr, dst_scr):
        upper, mask_t, same_head = _gla_consts()
        s = pl.program_id(0)
        blk = rev(s)

        @pl.when(s == 0)
        def _():
            d_scr[...] = jnp.zeros_like(d_scr)
            an_scr[...] = jnp.zeros_like(an_scr)
            dwg_ref[...] = jnp.zeros_like(dwg_ref)
            dbg_ref[...] = jnp.zeros_like(dbg_ref)
            dgn_ref[...] = jnp.zeros_like(dgn_ref)

        z, la = _gla_gate(lr_ref, wg_ref, bg_ref)
        ws, as_, qss, kds = [], [], [], []
        for c in range(GLA_NC):
            rs = slice(c * CH, (c + 1) * CH)
            w, a = _gla_chunk_decay(la[rs], upper)
            ws.append(w)
            as_.append(a)
            qs = (q_ref[rs, :].astype(F32) * Q_SCALE).astype(BF16)
            qss.append(qs)
            kds.append((k_ref[rs, :].astype(F32) * w).astype(BF16))
            o_scr[rs, :] = _dot_nt(qs, st_ref[c])
        o = o_scr[...]
        r = lax.rsqrt(_head_mean(o * o, same_head) + EPS)
        on = o * r
        gf = g_ref[...].astype(F32)
        sg = _sigmoid(gf)
        si = gf * sg
        dyf = dy_ref[...].astype(F32)
        gn_ = gn_ref[...]
        dgn_ref[...] += jnp.sum(dyf * si * on, axis=0, keepdims=True)
        dg_ref[...] = (dyf * on * gn_ * (sg * (1.0 + gf * (1.0 - sg)))).astype(BF16)
        d_on = dyf * si * gn_
        do_scr[...] = r * (d_on - on * _head_mean(d_on * on, same_head))

        for c in range(GLA_NC):
            rs = slice(c * CH, (c + 1) * CH)
            dst_scr[c] = _dot_tn(do_scr[rs, :].astype(BF16), qss[c]) * mask_t
        for c in reversed(range(GLA_NC)):
            dt = d_scr[...] * an_scr[...] + dst_scr[c]
            d_scr[...] = dt
            dst_scr[c] = dt
            an_scr[...] = as_[c]
        first = (blk > 0).astype(F32)
        for c in range(GLA_NC):
            rs = slice(c * CH, (c + 1) * CH)
            dob = do_scr[rs, :].astype(BF16)
            if c > 0:
                s_prev = st_ref[c - 1].astype(F32)
            else:
                s_prev = stp_ref[0].astype(F32) * first
            dq_ref[rs, :] = (_dot(dob, st_ref[c]) * Q_SCALE).astype(BF16)
            dt = dst_scr[c]
            da = jnp.sum(dt * s_prev, axis=0, keepdims=True)
            db = dt.astype(BF16)
            dkd = _dot(v_ref[rs, :], db)
            dv_ref[rs, :] = _dot_nt(kds[c], db).astype(BF16)
            dk_ref[rs, :] = (dkd * ws[c]).astype(BF16)
            ddec = dkd * k_ref[rs, :].astype(F32) * ws[c]
            hi, lo = _split2(ddec)
            dla_scr[rs, :] = _dot_tn(upper, hi) + _dot_tn(upper, lo) + as_[c] * da

        dz = dla_scr[...] * (1.0 - _sigmoid(z)) * (1.0 / GATE_TAU)
        dzb = dz.astype(BF16)
        dlr_ref[...] = _dot_nt(dzb, wg_ref[...]).astype(BF16)
        dwg_ref[...] += _dot_tn(lr_ref[...], dzb)
        dbg_ref[...] += jnp.sum(dz, axis=0, keepdims=True)

    def rspec(n):
        return pl.BlockSpec((GLA_ROWS, n), lambda s: (rev(s), 0))

    return pl.pallas_call(
        body, name="gla_bwd", grid=(nb,),
        in_specs=[rspec(256), rspec(256), rspec(384), rspec(384), rspec(128),
                  pl.BlockSpec((GLA_NC, 384, 256), lambda s: (rev(s), 0, 0)),
                  pl.BlockSpec((1, 384, 256), lambda s: (jnp.maximum(rev(s) * GLA_NC - 1, 0), 0, 0)),
                  rspec(384), _full_spec((128, 256)), _full_spec((1, 256)), _full_spec((1, 384))],
        out_specs=[rspec(256), rspec(256), rspec(384), rspec(384), rspec(128),
                   _full_spec((128, 256)), _full_spec((1, 256)), _full_spec((1, 384))],
        out_shape=[jax.ShapeDtypeStruct((T, 256), BF16), jax.ShapeDtypeStruct((T, 256), BF16),
                   jax.ShapeDtypeStruct((T, 384), BF16), jax.ShapeDtypeStruct((T, 384), BF16),
                   jax.ShapeDtypeStruct((T, 128), BF16),
                   jax.ShapeDtypeStruct((128, 256), F32), jax.ShapeDtypeStruct((1, 256), F32),
                   jax.ShapeDtypeStruct((1, 384), F32)],
        scratch_shapes=[pltpu.VMEM((384, 256), F32), pltpu.VMEM((1, 256), F32),
                        pltpu.VMEM((GLA_ROWS, 384), F32), pltpu.VMEM((GLA_ROWS, 384), F32),
                        pltpu.VMEM((GLA_ROWS, 256), F32), pltpu.VMEM((GLA_NC, 384, 256), F32)],
        compiler_params=_cp(("arbitrary",)),
    )(q, k, v, g, lr, states, states, dy, wg, bg, gn)


CONV_ROWS = 512
HALO = 32
SUBL = 8
CONV_SLAB = 32
PHASE_ROWS = CONV_ROWS + HALO - SUBL
FWD_SHIFT = tuple(HALO - (KCONV - 1) + j for j in range(KCONV))
BWD_SHIFT = tuple(KCONV - 1 - j for j in range(KCONV))


def _fill_phases(buf, ph):
    for f in range(1, SUBL):
        ph[f, 0:PHASE_ROWS, :] = buf[pl.ds(f, PHASE_ROWS), :]


def _tap(buf, ph, shift, r, n):
    f, base = shift % SUBL, shift - shift % SUBL
    src = buf if f == 0 else ph.at[f]
    return src[pl.ds(base + r, n), :]


def _taps_apply(w_ref, buf, ph, shifts, out):
    for r in range(0, CONV_ROWS, CONV_SLAB):
        acc = jnp.zeros((CONV_SLAB, 256), F32)
        for j in range(KCONV):
            acc = acc + w_ref[j:j + 1, :] * _tap(buf, ph, shifts[j], r, CONV_SLAB)
        out[r:r + CONV_SLAB, :] = acc


def _conv_scratch():
    return [pltpu.VMEM((CONV_ROWS + HALO, 256), F32), pltpu.VMEM((SUBL, CONV_ROWS + HALO, 256), F32),
            pltpu.VMEM((CONV_ROWS, 256), F32)]


def _conv_common(cu_ref, halo_ref, w_ref, b_ref, lg_ref, lb_ref, buf, ph, cbuf, blk, conv_ref=None):
    u = cu_ref[...].astype(F32)
    a = u[:, :256]
    sb = _sigmoid(u[:, 256:])
    uh = halo_ref[...].astype(F32)
    hh = uh[:, :256] * _sigmoid(uh[:, 256:]) * (blk > 0).astype(F32)
    buf[0:HALO, :] = hh
    buf[HALO:HALO + CONV_ROWS, :] = a * sb
    _fill_phases(buf, ph)
    if conv_ref is None:
        _taps_apply(w_ref, buf, ph, FWD_SHIFT, cbuf)
        conv = cbuf[...]
    else:
        conv = conv_ref[...]
    cc = conv + b_ref[...]
    mu = jnp.mean(cc, axis=-1, keepdims=True)
    xc = cc - mu
    rstd = lax.rsqrt(jnp.mean(xc * xc, axis=-1, keepdims=True) + EPS)
    n = xc * rstd
    yln = n * lg_ref[...] + lb_ref[...]
    return a, sb, n, rstd, yln, conv


def _conv_fwd(cu, w, b, lg, lb, jobs=()):
    T = cu.shape[0]
    nb = T // CONV_ROWS
    per = CONV_ROWS // HALO

    def body(cu_ref, halo_ref, w_ref, b_ref, lg_ref, lb_ref, y_ref, conv_ref, buf, ph, cbuf):
        _, _, _, _, yln, conv = _conv_common(cu_ref, halo_ref, w_ref, b_ref, lg_ref, lb_ref, buf, ph, cbuf,
                                             pl.program_id(0))
        y_ref[...] = (yln * _sigmoid(yln)).astype(BF16)
        conv_ref[...] = conv

    return _pcall(
        body, name="conv_fwd", grid=(nb,),
        in_specs=[_row_spec(CONV_ROWS, 512),
                  pl.BlockSpec((HALO, 512), lambda i: (jnp.maximum(i * per - 1, 0), 0)),
                  _full_spec((32, 256)), _full_spec((1, 256)), _full_spec((1, 256)), _full_spec((1, 256))],
        out_specs=[_row_spec(CONV_ROWS, 256), _row_spec(CONV_ROWS, 256)],
        out_shape=[jax.ShapeDtypeStruct((T, 256), BF16), jax.ShapeDtypeStruct((T, 256), F32)],
        scratch_shapes=_conv_scratch(),
        sem=("arbitrary",), operands=(cu, cu, w, b, lg, lb), jobs=jobs)


def _conv_bwd(cu, conv, dy, w, b, lg, lb, jobs=()):
    T = cu.shape[0]
    nb = T // CONV_ROWS
    per = CONV_ROWS // HALO

    def rev(s):
        return nb - 1 - s

    def body(cu_ref, halo_ref, conv_ref, dy_ref, w_ref, b_ref, lg_ref, lb_ref,
             dcu_ref, dw_ref, db_ref, dlg_ref, dlb_ref, buf, ph, cbuf, dcbuf, dph, carry):
        s = pl.program_id(0)

        @pl.when(s == 0)
        def _():
            carry[...] = jnp.zeros_like(carry)
            dw_ref[...] = jnp.zeros_like(dw_ref)
            db_ref[...] = jnp.zeros_like(db_ref)
            dlg_ref[...] = jnp.zeros_like(dlg_ref)
            dlb_ref[...] = jnp.zeros_like(dlb_ref)

        a, sb, n, rstd, yln, _ = _conv_common(cu_ref, halo_ref, w_ref, b_ref, lg_ref, lb_ref, buf, ph, cbuf, rev(s),
                                              conv_ref=conv_ref)
        sg = _sigmoid(yln)
        dyln = dy_ref[...].astype(F32) * (sg * (1.0 + yln * (1.0 - sg)))
        dlg_ref[...] += jnp.sum(dyln * n, axis=0, keepdims=True)
        dlb_ref[...] += jnp.sum(dyln, axis=0, keepdims=True)
        dn = dyln * lg_ref[...]
        dc = rstd * (dn - jnp.mean(dn, axis=-1, keepdims=True) - n * jnp.mean(dn * n, axis=-1, keepdims=True))
        db_ref[...] += jnp.sum(dc, axis=0, keepdims=True)
        dcbuf[0:CONV_ROWS, :] = dc
        dcbuf[CONV_ROWS:CONV_ROWS + HALO, :] = carry[...]
        carry[...] = dc[0:HALO, :]
        _fill_phases(dcbuf, dph)
        for j in range(KCONV):
            acc = jnp.zeros((SUBL, 256), F32)
            for r in range(0, CONV_ROWS, 2 * CONV_SLAB):
                prod = dcbuf[r:r + 2 * CONV_SLAB, :] * _tap(buf, ph, FWD_SHIFT[j], r, 2 * CONV_SLAB)
                acc = acc + jnp.sum(prod.reshape(2 * CONV_SLAB // SUBL, SUBL, 256), axis=0)
            dw_ref[j:j + 1, :] += jnp.sum(acc, axis=0, keepdims=True)
        _taps_apply(w_ref, dcbuf, dph, BWD_SHIFT, cbuf)
        dhg = cbuf[...]
        dcu_ref[...] = jnp.concatenate([dhg * sb, dhg * a * sb * (1.0 - sb)], axis=1).astype(BF16)

    def rspec(n):
        return pl.BlockSpec((CONV_ROWS, n), lambda s: (rev(s), 0))

    return _pcall(
        body, name="conv_bwd", grid=(nb,),
        in_specs=[rspec(512),
                  pl.BlockSpec((HALO, 512), lambda s: (jnp.maximum(rev(s) * per - 1, 0), 0)),
                  rspec(256), rspec(256),
                  _full_spec((32, 256)), _full_spec((1, 256)), _full_spec((1, 256)), _full_spec((1, 256))],
        out_specs=[rspec(512), _full_spec((32, 256)), _full_spec((1, 256)), _full_spec((1, 256)),
                   _full_spec((1, 256))],
        out_shape=[jax.ShapeDtypeStruct((T, 512), BF16), jax.ShapeDtypeStruct((32, 256), F32),
                   jax.ShapeDtypeStruct((1, 256), F32), jax.ShapeDtypeStruct((1, 256), F32),
                   jax.ShapeDtypeStruct((1, 256), F32)],
        scratch_shapes=_conv_scratch() + [pltpu.VMEM((CONV_ROWS + HALO, 256), F32),
                                          pltpu.VMEM((SUBL, CONV_ROWS + HALO, 256), F32),
                                          pltpu.VMEM((HALO, 256), F32)],
        sem=("arbitrary",), operands=(cu, cu, conv, dy, w, b, lg, lb), jobs=jobs)


def _rel_onehot_t(shift=0):
    r = lax.broadcasted_iota(jnp.int32, (384, RB_W), 0)
    n = lax.broadcasted_iota(jnp.int32, (384, RB_W), 1) - shift
    idx = jnp.clip(1024 - n, -128, 128) + 128
    return (idx == r).astype(BF16)


def _relbias_expand(rb):
    def body(rb_ref, out_ref):
        oh = _rel_onehot_t()
        hi, mid, lo = _split3(rb_ref[...])
        strip = _dot(hi, oh) + _dot(mid, oh) + _dot(lo, oh)
        qi = _group(lax.broadcasted_iota(jnp.int32, (AQ_BLK, AK_WIN), 0), CH, 4)
        kj = _group(lax.broadcasted_iota(jnp.int32, (AQ_BLK, AK_WIN), 1), CH, 12)
        valid = (kj >= qi) & (kj <= qi + 8)
        for hd in range(6):
            x = jnp.broadcast_to(strip[hd:hd + 1, :], (AQ_BLK, RB_W))
            xr = pltpu.roll(x, 0, 1, stride=1, stride_axis=0)
            out_ref[hd] = jnp.where(valid, xr[:, 512:512 + AK_WIN], NEG)

    return pl.pallas_call(
        body, name="relbias_expand",
        out_shape=jax.ShapeDtypeStruct((6, AQ_BLK, AK_WIN), F32),
        compiler_params=_cp(),
    )(rb)


def _relbias_grad(dbias):
    def body(db_ref, out_ref):
        oh = _rel_onehot_t(AQ_BLK - 1)
        ri = lax.broadcasted_iota(jnp.int32, (AQ_BLK, AQ_BLK), 0)
        ci = lax.broadcasted_iota(jnp.int32, (AQ_BLK, AQ_BLK), 1)
        flip = (ri + ci == AQ_BLK - 1).astype(BF16)
        rows = []
        for hd in range(6):
            hi, mid, lo = _split3(db_ref[hd])
            rev = _dot(flip, hi) + _dot(flip, mid) + _dot(flip, lo)
            x = jnp.concatenate([jnp.zeros((AQ_BLK, 512), F32), rev,
                                 jnp.zeros((AQ_BLK, RB_W - 512 - AK_WIN), F32)], axis=1)
            xr = pltpu.roll(x, 0, 1, stride=1, stride_axis=0)
            rows.append(jnp.sum(xr, axis=0, keepdims=True))
        rows.append(jnp.zeros((2, RB_W), F32))
        dstrip = jnp.concatenate(rows, axis=0)
        hi, mid, lo = _split3(dstrip)
        out_ref[...] = _dot_nt(hi, oh) + _dot_nt(mid, oh) + _dot_nt(lo, oh)

    return pl.pallas_call(
        body, name="relbias_grad",
        out_shape=jax.ShapeDtypeStruct((8, 384), F32),
        compiler_params=_cp(),
    )(dbias)


ATT_SLAB = 8


def _att_logits_slab(s_scr, b_ref, hd, rows, first_key):
    kvalid = lax.broadcasted_iota(jnp.int32, (ATT_SLAB, AK_WIN), 1) >= first_key
    return jnp.where(kvalid, s_scr[rows, :] + b_ref[hd, rows, :], NEG)


def _att_softmax_slab(s_scr, b_ref, hd, rows, first_key):
    s = _att_logits_slab(s_scr, b_ref, hd, rows, first_key)
    m = jnp.max(s, axis=-1, keepdims=True)
    p = jnp.exp(s - m)
    total = jnp.sum(p, axis=-1, keepdims=True)
    return p * (1.0 / total), m + jnp.log(total)


def _att_first_key(i):
    return (8 - 4 * i) * CH


def _slab_rows(t):
    return pl.ds(t * ATT_SLAB, ATT_SLAB)


def _head_lanes(hd):
    return slice(hd * 64, (hd + 1) * 64)


WIN_BLKS = AK_WIN // AQ_BLK


def _head_tile(hd):
    return slice(hd * HEAD_PAD, (hd + 1) * HEAD_PAD)


def _win_cols(d):
    return slice(d * AQ_BLK, (d + 1) * AQ_BLK)


def _win_block(i, d):
    return jnp.maximum(i + d - WIN_LEFT, 0)


def _win_specs():
    return [pl.BlockSpec((AQ_BLK, ATT_WIDE), lambda i, d=d: (_win_block(i, d), 0)) for d in range(WIN_BLKS)]


def _att_fwd(q, k, v, bias, jobs=()):
    T = q.shape[0]
    nb = T // AQ_BLK

    def body(q_ref, k0, k1, k2, v0, v1, v2, b_ref, o_ref, lse_ref, s_scr):
        k_refs, v_refs = (k0, k1, k2), (v0, v1, v2)
        first_key = _att_first_key(pl.program_id(0))
        lse_ref[...] = jnp.zeros_like(lse_ref)

        def scores(hd):
            q_h = q_ref[:, _head_tile(hd)] * A_SCALE
            for d in range(WIN_BLKS):
                s_scr[hd % 2, :, _win_cols(d)] = _dot_nt(q_h, k_refs[d][:, _head_tile(hd)])

        scores(0)
        for hd in range(ATT_HEADS):
            if hd + 1 < ATT_HEADS:
                scores(hd + 1)
            s_h = s_scr.at[hd % 2]
            for t in range(AQ_BLK // ATT_SLAB):
                rows = _slab_rows(t)
                s_h[rows, :], lse_ref[rows, hd:hd + 1] = _att_softmax_slab(s_h, b_ref, hd, rows, first_key)
            o_h = _dot(s_h[:, _win_cols(0)].astype(BF16), v_refs[0][:, _head_tile(hd)])
            for d in range(1, WIN_BLKS):
                o_h = o_h + _dot(s_h[:, _win_cols(d)].astype(BF16), v_refs[d][:, _head_tile(hd)])
            o_ref[:, _head_lanes(hd)] = o_h[:, :64].astype(BF16)

    return _pcall(
        body, name="att_fwd", grid=(nb,),
        in_specs=[_row_spec(AQ_BLK, ATT_WIDE)] + _win_specs() + _win_specs() + [_full_spec((6, AQ_BLK, AK_WIN))],
        out_specs=[_row_spec(AQ_BLK, 384), _row_spec(AQ_BLK, 128)],
        out_shape=[jax.ShapeDtypeStruct((T, 384), BF16), jax.ShapeDtypeStruct((T, 128), F32)],
        scratch_shapes=[pltpu.VMEM((2, AQ_BLK, AK_WIN), F32)],
        sem=("arbitrary",), operands=(q, k, k, k, v, v, v, bias), jobs=jobs)


def _att_bwd(q, k, v, bias, o, lse, do, jobs=()):
    T = q.shape[0]
    nb = T // AQ_BLK

    def body(q_ref, k0, k1, k2, v0, v1, v2, b_ref, o_ref, lse_ref, do_ref, dq_ref, dk_ref, dv_ref, db_ref,
             dk_acc, dv_acc, s_scr, dp_scr, delta_scr):
        k_refs, v_refs = (k0, k1, k2), (v0, v1, v2)
        i = pl.program_id(0)

        @pl.when(i == 0)
        def _():
            dk_acc[...] = jnp.zeros_like(dk_acc)
            dv_acc[...] = jnp.zeros_like(dv_acc)
            db_ref[...] = jnp.zeros_like(db_ref)

        first_key = _att_first_key(i)

        def scores(hd):
            q_h = q_ref[:, _head_tile(hd)] * A_SCALE
            do_h = do_ref[:, _head_tile(hd)]
            delta_scr[:, hd:hd + 1] = jnp.sum(do_h[:, :64].astype(F32) * o_ref[:, _head_lanes(hd)].astype(F32),
                                              axis=-1, keepdims=True)
            for d in range(WIN_BLKS):
                s_scr[hd % 2, :, _win_cols(d)] = _dot_nt(q_h, k_refs[d][:, _head_tile(hd)])
                dp_scr[hd % 2, :, _win_cols(d)] = _dot_nt(do_h, v_refs[d][:, _head_tile(hd)])

        scores(0)
        for hd in range(ATT_HEADS):
            if hd + 1 < ATT_HEADS:
                scores(hd + 1)
            s_h, dp_h = s_scr.at[hd % 2], dp_scr.at[hd % 2]
            for t in range(AQ_BLK // ATT_SLAB):
                rows = _slab_rows(t)
                p = jnp.exp(_att_logits_slab(s_h, b_ref, hd, rows, first_key) - lse_ref[rows, hd:hd + 1])
                ds = p * (dp_h[rows, :] - delta_scr[rows, hd:hd + 1])
                db_ref[hd, rows, :] += ds
                s_h[rows, :] = p
                dp_h[rows, :] = ds
            q_h = q_ref[:, _head_tile(hd)] * A_SCALE
            do_h = do_ref[:, _head_tile(hd)]
            ls = _head_lanes(hd)
            dq_h = jnp.zeros((AQ_BLK, HEAD_PAD), F32)
            for d in range(WIN_BLKS):
                pb = s_h[:, _win_cols(d)].astype(BF16)
                dsb = dp_h[:, _win_cols(d)].astype(BF16)
                rows = pl.ds(pl.multiple_of(_win_block(i, d) * AQ_BLK, AQ_BLK), AQ_BLK)
                dv_acc[rows, ls] += _dot_tn(pb, do_h)[:, :64]
                dk_acc[rows, ls] += _dot_tn(dsb, q_h)[:, :64]
                dq_h = dq_h + _dot(dsb, k_refs[d][:, _head_tile(hd)])
            dq_ref[:, ls] = (dq_h[:, :64] * A_SCALE).astype(BF16)

        @pl.when(i == nb - 1)
        def _():
            dk_ref[...] = dk_acc[...].astype(BF16)
            dv_ref[...] = dv_acc[...].astype(BF16)

    return _pcall(
        body, name="att_bwd", grid=(nb,),
        in_specs=[_row_spec(AQ_BLK, ATT_WIDE)] + _win_specs() + _win_specs()
        + [_full_spec((6, AQ_BLK, AK_WIN)), _row_spec(AQ_BLK, 384), _row_spec(AQ_BLK, 128),
           _row_spec(AQ_BLK, ATT_WIDE)],
        out_specs=[_row_spec(AQ_BLK, 384), _full_spec((T, 384)), _full_spec((T, 384)),
                   _full_spec((6, AQ_BLK, AK_WIN))],
        out_shape=[jax.ShapeDtypeStruct((T, 384), BF16), jax.ShapeDtypeStruct((T, 384), BF16),
                   jax.ShapeDtypeStruct((T, 384), BF16), jax.ShapeDtypeStruct((6, AQ_BLK, AK_WIN), F32)],
        scratch_shapes=[pltpu.VMEM((T, 384), F32), pltpu.VMEM((T, 384), F32),
                        pltpu.VMEM((2, AQ_BLK, AK_WIN), F32), pltpu.VMEM((2, AQ_BLK, AK_WIN), F32),
                        pltpu.VMEM((AQ_BLK, 128), F32)],
        sem=("arbitrary",), operands=(q, k, k, k, v, v, v, bias, o, lse, do), jobs=jobs)


FF_BLK = 512
N_FF = 4096 // FF_BLK
MLP_SHARDS = 2


def _outproj_mlp_fwd(h, o_gla, o_conv, o_att, w_out, gamma, w_up, w_down, jobs=()):
    T = h.shape[0]
    tm = 1024

    def body(h_ref, og_ref, oc_ref, oa_ref, wo_ref, g_ref, wu_ref, wd_ref, h1_ref, xt_ref, h2_ref, a_ref, acc, xn_ref):
        j = pl.program_id(1)

        @pl.when(j == 0)
        def _():
            wo = wo_ref[...]
            h1 = (h_ref[...] + _dot(og_ref[...], wo[0:384]) + _dot(oc_ref[...], wo[384:640])
                  + _dot(oa_ref[...], wo[640:1024]))
            h1_ref[...] = h1
            r = lax.rsqrt(jnp.mean(h1 * h1, axis=-1, keepdims=True) + EPS)
            xn = (h1 * r * g_ref[...]).astype(BF16)
            xn_ref[...] = xn
            xt_ref[...] = jnp.transpose(xn)
            acc[...] = h1

        xn_ = xn_ref[...]
        down = None
        for s in range(MLP_SHARDS):
            a = jnp.maximum(_dot(xn_, wu_ref[s]), 0.0)
            a_ref[:, s * FF_BLK:(s + 1) * FF_BLK] = a.astype(BF16)
            part = _dot((a * a).astype(BF16), wd_ref[s])
            down = part if down is None else down + part
        acc[...] += down

        @pl.when(j == N_FF // MLP_SHARDS - 1)
        def _():
            h2_ref[...] = acc[...]

    row = lambda n: pl.BlockSpec((tm, n), lambda i, j: (i, 0))
    return _pcall(
        body, name="outproj_mlp_fwd", grid=(T // tm, N_FF // MLP_SHARDS),
        in_specs=[row(D), row(384), row(256), row(384),
                  pl.BlockSpec((D, D), lambda i, j: (0, 0)), pl.BlockSpec((1, D), lambda i, j: (0, 0)),
                  pl.BlockSpec((MLP_SHARDS, D, FF_BLK), lambda i, j: (j, 0, 0)),
                  pl.BlockSpec((MLP_SHARDS, FF_BLK, D), lambda i, j: (j, 0, 0))],
        out_specs=[row(D), pl.BlockSpec((D, tm), lambda i, j: (0, i)), row(D),
                   pl.BlockSpec((tm, MLP_SHARDS * FF_BLK), lambda i, j: (i, j))],
        out_shape=[jax.ShapeDtypeStruct((T, D), F32), jax.ShapeDtypeStruct((D, T), BF16),
                   jax.ShapeDtypeStruct((T, D), F32), jax.ShapeDtypeStruct((T, N_FF * FF_BLK), BF16)],
        scratch_shapes=[pltpu.VMEM((tm, D), F32), pltpu.VMEM((tm, D), BF16)],
        sem=("arbitrary", "arbitrary"), operands=(h, o_gla, o_conv, o_att, w_out, gamma, w_up, w_down), jobs=jobs)


def _mlp_bwd(xn2t, act, h1, dh2, dh2b, gamma, w_up, w_down, jobs=()):
    T = act.shape[0]
    tm = 512
    nt = T // tm
    ns = MLP_SHARDS
    nj = N_FF // ns
    last = nj - 1

    def body(xt_ref, a_ref, h1_ref, dy_ref, dyb_ref, g_ref, wu_ref, wd_ref, dh1_ref, dwu_ref, dwd_ref, dg_ref,
             dxn_acc, acc_u, acc_d):
        j = pl.program_id(0)
        i = pl.program_id(1)
        xt = xt_ref[...]
        dyb = dyb_ref[...]
        rows = pl.ds(pl.multiple_of(i * tm, tm), tm)

        @pl.when(i == 0)
        def _():
            acc_u[...] = jnp.zeros_like(acc_u)
            acc_d[...] = jnp.zeros_like(acc_d)

        @pl.when(j == 0)
        def _():
            dxn_acc[rows, :] = jnp.zeros((tm, D), F32)

        dxn = None
        for s in range(ns):
            a = a_ref[:, s * FF_BLK:(s + 1) * FF_BLK].astype(F32)
            hh = (a * a).astype(BF16)
            du = (_dot_nt(dyb, wd_ref[s]) * (2.0 * a)).astype(BF16)
            acc_d[s] += _dot_tn(hh, dyb)
            acc_u[s] += _dot(xt, du)
            part = _dot_nt(du, wu_ref[s])
            dxn = part if dxn is None else dxn + part
        dxn_acc[rows, :] += dxn

        @pl.when(i == nt - 1)
        def _():
            for s in range(ns):
                dwu_ref[s, 0] = acc_u[s].astype(BF16)
                dwd_ref[s, 0] = acc_d[s].astype(BF16)

        @pl.when(j == last)
        def _():
            @pl.when(i == 0)
            def _():
                dg_ref[...] = jnp.zeros_like(dg_ref)

            h1 = h1_ref[...]
            r = lax.rsqrt(jnp.mean(h1 * h1, axis=-1, keepdims=True) + EPS)
            dx, dgam = _rms_bwd(dxn_acc[rows, :], h1, r, g_ref[...])
            dh1_ref[...] = dy_ref[...] + dx
            dg_ref[...] += dgam

    assert ns == 2
    late = lambda j, i: (jnp.where(j == last, i, 0), 0)
    return _pcall(
        body, name="mlp_bwd", grid=(nj, nt),
        in_specs=[pl.BlockSpec((D, tm), lambda j, i: (0, i)), pl.BlockSpec((tm, ns * FF_BLK), lambda j, i: (i, j)),
                  pl.BlockSpec((tm, D), late), pl.BlockSpec((tm, D), late),
                  pl.BlockSpec((tm, D), lambda j, i: (i, 0)), pl.BlockSpec((1, D), lambda j, i: (0, 0)),
                  pl.BlockSpec((ns, D, FF_BLK), lambda j, i: (j, 0, 0), pipeline_mode=pl.Buffered(1)),
                  pl.BlockSpec((ns, FF_BLK, D), lambda j, i: (j, 0, 0), pipeline_mode=pl.Buffered(1))],
        out_specs=[pl.BlockSpec((tm, D), late),
                   pl.BlockSpec((ns, 1, D, FF_BLK), lambda j, i: (0, j, 0, 0)),
                   pl.BlockSpec((ns, 1, FF_BLK, D), lambda j, i: (0, j, 0, 0)),
                   pl.BlockSpec((1, D), lambda j, i: (0, 0))],
        out_shape=[jax.ShapeDtypeStruct((T, D), F32), jax.ShapeDtypeStruct((2, 4, D, FF_BLK), BF16),
                   jax.ShapeDtypeStruct((2, 4, FF_BLK, D), BF16), jax.ShapeDtypeStruct((1, D), F32)],
        scratch_shapes=[pltpu.VMEM((T, D), F32), pltpu.VMEM((ns, D, FF_BLK), F32), pltpu.VMEM((ns, FF_BLK, D), F32)],
        sem=("arbitrary", "arbitrary"), operands=(xn2t, act, h1, dh2, dh2b, gamma, w_up, w_down), jobs=jobs)


def _outproj_bwd(dh1, o_gla, o_conv, o_att, w_out, jobs=()):
    T = dh1.shape[0]
    tm = 512
    nt = T // tm

    def body(dy_ref, og_ref, oc_ref, oa_ref, wo_ref, dg_ref, dc_ref, da_ref, dw_ref, acc):
        i = pl.program_id(0)

        @pl.when(i == 0)
        def _():
            acc[...] = jnp.zeros_like(acc)

        dyb = dy_ref[...].astype(BF16)
        dm = _dot_nt(dyb, wo_ref[...])
        dg_ref[...] = dm[:, 0:384].astype(BF16)
        dc_ref[...] = dm[:, 384:640].astype(BF16)
        _store_head_padded(da_ref, dm[:, 640:1024].astype(BF16))
        mixed = jnp.concatenate([og_ref[...], oc_ref[...], oa_ref[...]], axis=1)
        acc[...] += _dot_tn(mixed, dyb)

        @pl.when(i == nt - 1)
        def _():
            for j in range(N_DEV):
                dw_ref[j % 2, j // 2] = acc[j * 128:(j + 1) * 128, :].astype(BF16)

    return _pcall(
        body, name="outproj_bwd", grid=(nt,),
        in_specs=[_row_spec(tm, D), _row_spec(tm, 384), _row_spec(tm, 256), _row_spec(tm, 384),
                  _full_spec((D, D))],
        out_specs=[_row_spec(tm, 384), _row_spec(tm, 256), _row_spec(tm, ATT_WIDE), _full_spec((2, 4, 128, D))],
        out_shape=[jax.ShapeDtypeStruct((T, 384), BF16), jax.ShapeDtypeStruct((T, 256), BF16),
                   jax.ShapeDtypeStruct((T, ATT_WIDE), BF16), jax.ShapeDtypeStruct((2, 4, 128, D), BF16)],
        scratch_shapes=[pltpu.VMEM((D, D), F32)],
        sem=("arbitrary",), operands=(dh1, o_gla, o_conv, o_att, w_out), jobs=jobs)


def _loss_fwd_bwd(h, gamma, target):
    T = h.shape[0]
    tm = 512

    def body(h_ref, g_ref, t_ref, loss_ref, dh_ref, dhb_ref, dg_ref):
        @pl.when(pl.program_id(0) == 0)
        def _():
            loss_ref[...] = jnp.zeros_like(loss_ref)
            dg_ref[...] = jnp.zeros_like(dg_ref)

        x = h_ref[...]
        r = lax.rsqrt(jnp.mean(x * x, axis=-1, keepdims=True) + EPS)
        gamma_ = g_ref[...]
        e = x * r * gamma_ - t_ref[...]
        loss_ref[...] += 0.5 * jnp.sum(jnp.mean(e * e, axis=-1, keepdims=True), axis=0, keepdims=True)
        dx, dgam = _rms_bwd(e * (1.0 / D), x, r, gamma_)
        dh_ref[...] = dx
        dhb_ref[...] = dx.astype(BF16)
        dg_ref[...] += dgam

    return pl.pallas_call(
        body, name="loss_fwd_bwd", grid=(T // tm,),
        in_specs=[_row_spec(tm, D), _full_spec((1, D)), _row_spec(tm, D)],
        out_specs=[_full_spec((8, 128)), _row_spec(tm, D), _row_spec(tm, D), _full_spec((1, D))],
        out_shape=[jax.ShapeDtypeStruct((8, 128), F32), jax.ShapeDtypeStruct((T, D), F32),
                   jax.ShapeDtypeStruct((T, D), BF16), jax.ShapeDtypeStruct((1, D), F32)],
        compiler_params=_cp(("arbitrary",)),
    )(h, gamma, target)


def _adamw_math(w, g, m, v):
    m = ADAM_B1 * m + (1.0 - ADAM_B1) * g
    v = ADAM_B2 * v + (1.0 - ADAM_B2) * (g * g)
    m_hat = m / (1.0 - ADAM_B1 ** ADAM_STEP)
    v_hat = v / (1.0 - ADAM_B2 ** ADAM_STEP)
    delta = -ADAM_LR * (m_hat / (jnp.sqrt(v_hat) + ADAM_EPS) + ADAM_WD * w)
    return delta, m, v


def _rs_adamw(a_own, r2, w, m, v, layer, chip_idx, rows_blk, prev=None):
    _, R, C = w.shape
    nblk = R // rows_blk

    def body(chip_ref, a_ref, r_ref, w_ref, m_ref, v_ref, *rest):
        g_out, d_out, m_out, v_out = rest[-4:]
        g = (a_ref[0].astype(F32) + r_ref[0].astype(F32)) + (r_ref[1].astype(F32) + r_ref[2].astype(F32))
        delta, m_new, v_new = _adamw_math(w_ref[0], g, m_ref[0], v_ref[0])
        g_out[0] = g
        d_out[0] = delta
        m_out[0] = m_new
        v_out[0] = v_new

    blk = pl.BlockSpec((1, rows_blk, C), lambda i, chip: (layer, i, 0))
    n_prev = 0 if prev is None else 4
    grid_spec = pltpu.PrefetchScalarGridSpec(
        num_scalar_prefetch=1, grid=(nblk,),
        in_specs=[pl.BlockSpec((1, rows_blk, C), lambda i, chip: (chip[0], i, 0)),
                  pl.BlockSpec((3, rows_blk, C), lambda i, chip: (0, i, 0)), blk, blk, blk]
        + [_any_spec()] * n_prev,
        out_specs=[blk, blk, blk, blk])
    return pl.pallas_call(
        body, name="rs_adamw", grid_spec=grid_spec,
        out_shape=[jax.ShapeDtypeStruct((DEPTH, R, C), F32)] * 4,
        input_output_aliases={6 + t: t for t in range(n_prev)},
        compiler_params=_cp(("arbitrary",)),
    )(chip_idx, a_own, r2, w, m, v, *(prev or ()))


def _pair_sum(g, r1, core_idx, rows_blk):
    _, _, R, C = g.shape
    nblk = R // rows_blk

    def body(core_ref, g_ref, r_ref, o_ref):
        o_ref[...] = (g_ref[0].astype(F32) + r_ref[...].astype(F32)).astype(BF16)

    grid_spec = pltpu.PrefetchScalarGridSpec(
        num_scalar_prefetch=1, grid=(4, nblk),
        in_specs=[pl.BlockSpec((1, 1, rows_blk, C), lambda k, i, core: (core[0], k, i, 0)),
                  pl.BlockSpec((1, rows_blk, C), lambda k, i, core: (k, i, 0))],
        out_specs=pl.BlockSpec((1, rows_blk, C), lambda k, i, core: (k, i, 0)))
    return pl.pallas_call(
        body, name="rs_pair_sum", grid_spec=grid_spec,
        out_shape=jax.ShapeDtypeStruct((4, R, C), BF16),
        compiler_params=_cp(("arbitrary", "arbitrary")),
    )(core_idx, g, r1)


def _small_sum(gathered):
    def body(g_ref, o_ref):
        acc = g_ref[0]
        for d in range(1, N_DEV):
            acc = acc + g_ref[d]
        o_ref[...] = acc

    return pl.pallas_call(
        body, name="small_sum",
        out_shape=jax.ShapeDtypeStruct(gathered.shape[1:], F32),
        compiler_params=_cp(),
    )(gathered)


def _adamw_small(ws, gs, ms, vs):
    n = len(ws)

    def body(*refs):
        w_r, g_r, m_r, v_r = refs[0:n], refs[n:2 * n], refs[2 * n:3 * n], refs[3 * n:4 * n]
        d_o, m_o, v_o = refs[4 * n:5 * n], refs[5 * n:6 * n], refs[6 * n:7 * n]
        for t in range(n):
            delta, m_new, v_new = _adamw_math(w_r[t][...], g_r[t][...], m_r[t][...], v_r[t][...])
            d_o[t][...] = delta
            m_o[t][...] = m_new
            v_o[t][...] = v_new

    shapes = [jax.ShapeDtypeStruct(w.shape, F32) for w in ws]
    outs = pl.pallas_call(
        body, name="adamw_small", out_shape=shapes * 3, compiler_params=_cp(),
    )(*ws, *gs, *ms, *vs)
    return outs[0:n], outs[n:2 * n], outs[2 * n:3 * n]


def _mesh_pos():
    return lax.axis_index("x"), lax.axis_index("y"), lax.axis_index("c")


def _peers():
    x, y, c = _mesh_pos()
    return (x, y, c), (x, y, 1 - c), [(1 - x, y), (x, 1 - y), (1 - x, 1 - y)]


def _slot(ref, pos):
    return ref.at[4 * pos[0] + 2 * pos[1] + pos[2]]


def _remote(src, dst, send_sem, recv_sem, to):
    return pltpu.make_async_remote_copy(src_ref=src, dst_ref=dst, send_sem=send_sem, recv_sem=recv_sem,
                                        device_id=to, device_id_type=MESH)


def _ag_spread(shards):
    n = len(shards)

    def copies(ins, outs, sems):
        send, recv, loc = sems
        me, sibling, chips = _peers()
        peers = [sibling] + [(*chip, me[2]) for chip in chips]
        local = [pltpu.make_async_copy(ins[a], _slot(outs[a], me), loc.at[a]) for a in range(n)]
        sends = [_remote(ins[a], _slot(outs[a], me), send.at[a, k], recv.at[a, k], p)
                 for a in range(n) for k, p in enumerate(peers)]
        recvs = [_remote(ins[a], _slot(outs[a], p), send.at[a, k], recv.at[a, k], p)
                 for a in range(n) for k, p in enumerate(peers)]
        return local, sends, recvs

    def start(ins, outs, sems):
        local, sends, _ = copies(ins, outs, sems)
        for cp in local + sends:
            cp.start()

    def finish(ins, outs, sems):
        local, sends, recvs = copies(ins, outs, sems)
        for cp in sends:
            cp.wait_send()
        for cp in recvs:
            cp.wait_recv()
        for cp in local:
            cp.wait()

    return _Job(shards, [jax.ShapeDtypeStruct((N_DEV,) + a.shape, a.dtype) for a in shards],
                [pltpu.SemaphoreType.DMA((n, 4)), pltpu.SemaphoreType.DMA((n, 4)), pltpu.SemaphoreType.DMA((n,))],
                start, finish)


def _ag_pass(stacks):
    n = len(stacks)

    def copies(ins, outs, sems):
        send, recv = sems
        me, sibling, chips = _peers()
        sends = [_remote(_slot(ins[a], (*chip, me[2])), _slot(outs[a], (*chip, me[2])), send.at[a, j], recv.at[a, j],
                         sibling) for a in range(n) for j, chip in enumerate(chips)]
        recvs = [_remote(_slot(ins[a], (*chip, me[2])), _slot(outs[a], (*chip, 1 - me[2])), send.at[a, j],
                         recv.at[a, j], sibling) for a in range(n) for j, chip in enumerate(chips)]
        return sends, recvs

    def start(ins, outs, sems):
        for cp in copies(ins, outs, sems)[0]:
            cp.start()

    def finish(ins, outs, sems):
        sends, recvs = copies(ins, outs, sems)
        for cp in sends:
            cp.wait_send()
        for cp in recvs:
            cp.wait_recv()

    return _Job(stacks, [jax.ShapeDtypeStruct(a.shape, a.dtype) for a in stacks],
                [pltpu.SemaphoreType.DMA((n, 3)), pltpu.SemaphoreType.DMA((n, 3))],
                start, finish, aliases={a: a for a in range(n)})


def _ag_both(shards):
    spread = _ag_spread(shards)
    fake = [jax.ShapeDtypeStruct((N_DEV,) + a.shape, a.dtype) for a in shards]
    onward = _ag_pass(fake)
    n_sp = len(spread.sems)

    def start(ins, outs, sems):
        spread.start(ins, outs, sems[:n_sp])

    def finish(ins, outs, sems):
        spread.finish(ins, outs, sems[:n_sp])
        onward.start(outs, outs, sems[n_sp:])
        onward.finish(outs, outs, sems[n_sp:])

    return _Job(shards, spread.out_shapes, spread.sems + onward.sems, start, finish)


def _rs_swap(parts):
    n = len(parts)

    def copies(ins, outs, sems):
        send, recv = sems
        me, sibling, _ = _peers()
        return [_remote(ins[a].at[1 - me[2]], outs[a], send.at[a], recv.at[a], sibling) for a in range(n)]

    def start(ins, outs, sems):
        for cp in copies(ins, outs, sems):
            cp.start()

    def finish(ins, outs, sems):
        for cp in copies(ins, outs, sems):
            cp.wait()

    return _Job(parts, [jax.ShapeDtypeStruct(a.shape[1:], a.dtype) for a in parts],
                [pltpu.SemaphoreType.DMA((n,)), pltpu.SemaphoreType.DMA((n,))], start, finish)


def _rs_ici(pairs):
    n = len(pairs)

    def copies(ins, outs, sems):
        send, recv = sems
        me, _, chips = _peers()
        return [_remote(ins[a].at[2 * chip[0] + chip[1]], outs[a].at[j], send.at[a, j], recv.at[a, j],
                        (*chip, me[2])) for a in range(n) for j, chip in enumerate(chips)]

    def start(ins, outs, sems):
        for cp in copies(ins, outs, sems):
            cp.start()

    def finish(ins, outs, sems):
        for cp in copies(ins, outs, sems):
            cp.wait()

    return _Job(pairs, [jax.ShapeDtypeStruct((3,) + a.shape[1:], a.dtype) for a in pairs],
                [pltpu.SemaphoreType.DMA((n, 3)), pltpu.SemaphoreType.DMA((n, 3))], start, finish)


def _comm_call(jobs, name):
    def body():
        pass

    return _pcall(body, name=name, grid=(), in_specs=[], out_specs=[], out_shape=[], operands=(), jobs=jobs)[1]


def _allgather(arrs, name):
    n = len(arrs)

    def body(*refs):
        ins, outs = refs[:n], refs[n:2 * n]
        send_sems, recv_sems, local_sems = refs[2 * n:]
        x, y, c = _mesh_pos()
        me, sibling = (x, y, c), (x, y, 1 - c)
        chips = [(1 - x, y), (x, 1 - y), (1 - x, 1 - y)]

        def slot(a, pos):
            return outs[a].at[4 * pos[0] + 2 * pos[1] + pos[2]]

        def copy(a, k, block, to, src=None):
            return pltpu.make_async_remote_copy(
                src_ref=slot(a, block) if src is None else src, dst_ref=slot(a, block),
                send_sem=send_sems.at[a, k], recv_sem=recv_sems.at[a, k],
                device_id=to, device_id_type=MESH)

        mine = [pltpu.make_async_copy(ins[a], slot(a, me), local_sems.at[a]) for a in range(n)]
        for cp in mine:
            cp.start()
        first = []
        for a in range(n):
            first.append(copy(a, 0, me, sibling, src=ins[a]))
            first += [copy(a, 1 + j, me, (*chip, c), src=ins[a]) for j, chip in enumerate(chips)]
        for cp in first:
            cp.start()
        passed = []
        for j, chip in enumerate(chips):
            for a in range(n):
                copy(a, 1 + j, (*chip, c), me).wait_recv()
                fwd = copy(a, 4 + j, (*chip, c), sibling)
                fwd.start()
                passed.append(fwd)
        for a in range(n):
            copy(a, 0, sibling, me).wait_recv()
            for j, chip in enumerate(chips):
                copy(a, 4 + j, (*chip, 1 - c), me).wait_recv()
        for cp in first + passed:
            cp.wait_send()
        for cp in mine:
            cp.wait()

    return pl.pallas_call(
        body, name=name,
        in_specs=[_any_spec()] * n, out_specs=[_any_spec()] * n,
        out_shape=[jax.ShapeDtypeStruct((N_DEV,) + a.shape, a.dtype) for a in arrs],
        scratch_shapes=[pltpu.SemaphoreType.DMA((n, 7)), pltpu.SemaphoreType.DMA((n, 7)),
                        pltpu.SemaphoreType.DMA((n,))],
        compiler_params=_cp(),
    )(*arrs)


W_IN_SHARD = 354
W_IN_COLS = ((0, 192, OQ), (192, 192, OKK), (384, 384, OV), (768, 384, OG), (1152, 16, OLR), (1168, 512, OCU),
             (1680, 384, OAQ), (2064, 384, OAK), (2448, 384, OAV))


def _w_in_padded(stack):
    new_to_ref = {new: (start, width) for start, width, new in W_IN_COLS}
    cols = []
    for new, padded in IN_GROUPS:
        start, width = new_to_ref[new]
        a = start
        while a < start + width:
            j = a // W_IN_SHARD
            b = min(start + width, (j + 1) * W_IN_SHARD)
            cols.append(stack[j][:, a - j * W_IN_SHARD:b - j * W_IN_SHARD])
            a = b
        if padded > width:
            cols.append(jnp.zeros((stack.shape[1], padded - width), stack.dtype))
    return jnp.concatenate(cols, axis=1)


def _dw_in_shards(dw):
    shards = []
    for j in range(N_DEV):
        lo, hi = j * W_IN_SHARD, (j + 1) * W_IN_SHARD
        segs = []
        for start, width, new in W_IN_COLS:
            a, b = max(lo, start), min(hi, start + width)
            if a < b:
                segs.append(dw[:, new + a - start:new + b - start])
        shards.append(jnp.concatenate(segs, axis=1))
    return jnp.stack([jnp.stack([shards[2 * chip + core] for chip in range(4)]) for core in range(2)])


def _pad_to(a, shape):
    return jnp.pad(a, [(0, s - d) for d, s in zip(a.shape, shape)])


SMALL_LAYOUT = (
    ("norm_mix", 2, 1024), ("norm_ffn", 2, 1024), ("norm_final", 1, 1024), ("gla_norm", 2, 384),
    ("b_gla_gate", 2, 192), ("b_dw", 2, 256), ("conv_ln_g", 2, 256), ("conv_ln_b", 2, 256),
    ("rel_bias", 12, 257), ("w_gla_gate", 32, 192), ("w_dw", 62, 256),
)
SMALL_LANES = 128
SMALL_TILE = 8 * SMALL_LANES


def _small_tile_rows(r, lanes):
    return -(-(r * lanes) // SMALL_TILE) * 8


SMALL_ROWS = sum(_small_tile_rows(r, lanes) for _, r, lanes in SMALL_LAYOUT)


def _pack_small(parts):
    tiles = []
    for name, r, lanes in SMALL_LAYOUT:
        rows = _small_tile_rows(r, lanes)
        flat = _pad_to(parts[name].reshape(r * lanes), (rows * SMALL_LANES,))
        tiles.append(flat.reshape(rows, SMALL_LANES))
    return jnp.concatenate(tiles, axis=0)


def _unpack_small(packed):
    out, r0 = {}, 0
    for name, r, lanes in SMALL_LAYOUT:
        rows = _small_tile_rows(r, lanes)
        out[name] = packed[r0:r0 + rows].reshape(rows * SMALL_LANES)[:r * lanes].reshape(r, lanes)
        r0 += rows
    return out


def _mixers_fwd(h, wl, w_in_p, plan=None):
    plan, res = plan or {}, {}

    def jobs(host):
        return plan[host](res) if host in plan else ()

    (q, k, v, g, cu, aq, ak, av, lr), res["inproj"] = _inproj_fwd(h, wl["norm_mix"], w_in_p, jobs=jobs("inproj"))
    bias = _relbias_expand(wl["rb"])
    (o_att, lse), res["att"] = _att_fwd(aq, ak, av, bias, jobs=jobs("att"))
    (o_gla, states), res["gla"] = _gla_fwd(q, k, v, g, lr, wl["wg"], wl["bg"], wl["gn"], jobs=jobs("gla"))
    (o_conv, conv), res["conv"] = _conv_fwd(cu, wl["w_dw"], wl["b_dw"], wl["ln_g"], wl["ln_b"], jobs=jobs("conv"))
    sv = dict(h=h, w_in=w_in_p, q=q, k=k, v=v, g=g, cu=cu, aq=aq, ak=ak, av=av, lr=lr,
              o_gla=o_gla, o_conv=o_conv, conv=conv, o_att=o_att, lse=lse, states=states, bias=bias)
    return sv, res


def _mixers_bwd(sv, wl, dh1, d_ogla, d_oconv, att_grads, conv_jobs=(), x_jobs_fn=None):
    daq, dak, dav, dbias = att_grads
    d_rb = _relbias_grad(dbias)
    (dcu, dw_dw, db_dw, dln_g, dln_b), conv_res = _conv_bwd(
        sv["cu"], sv["conv"], d_oconv, wl["w_dw"], wl["b_dw"], wl["ln_g"], wl["ln_b"], jobs=conv_jobs)
    dq, dk, dv, dg, dlr, dwg, dbg, dgn = _gla_bwd(sv["q"], sv["k"], sv["v"], sv["g"], sv["lr"], sv["states"],
                                                  d_ogla, wl["wg"], wl["bg"], wl["gn"])
    dparts = (dq, dk, dv, dg, dcu, daq, dak, dav, dlr)
    dw_in = _inproj_bwd_w(sv["h"], wl["norm_mix"], dparts)
    x_jobs = x_jobs_fn(dw_in) if x_jobs_fn is not None else ()
    (dh, dhb, d_nmix), x_res = _inproj_bwd_x(sv["h"], dh1, wl["norm_mix"], sv["w_in"], dparts, jobs=x_jobs)
    small = dict(norm_mix=d_nmix, wg=dwg, bg=dbg, gn=dgn, w_dw=dw_dw, b_dw=db_dw, ln_g=dln_g, ln_b=dln_b, rb=d_rb)
    return (dh, dhb), dw_in, small, conv_res, x_res


def _layer_small(l, w_dw_full, norm_mix, w_gla_gate, b_gla_gate, gla_norm, b_dw, conv_ln_g, conv_ln_b, rel_bias,
                 norm_ffn):
    return dict(
        norm_mix=norm_mix[l][None, :], norm_ffn=norm_ffn[l][None, :],
        wg=_pad_to(w_gla_gate[l], (128, 256)).astype(BF16), bg=_pad_to(b_gla_gate[l][None, :], (1, 256)),
        gn=gla_norm[l][None, :], w_dw=_pad_to(w_dw_full, (32, 256)), b_dw=b_dw[l][None, :],
        ln_g=conv_ln_g[l][None, :], ln_b=conv_ln_b[l][None, :], rb=_pad_to(rel_bias[l], (8, 384)))


RS_ROWS = dict(w_in=512, w_out=128, w_up=512, w_down=256)


def kernel(x, norm_mix, w_in, w_gla_gate, b_gla_gate, gla_norm, w_dw, b_dw, conv_ln_g, conv_ln_b, rel_bias, w_out, norm_ffn, w_up, w_down, norm_final, loss_target, m_norm_mix, m_w_in, m_w_gla_gate, m_b_gla_gate, m_gla_norm, m_w_dw, m_b_dw, m_conv_ln_g, m_conv_ln_b, m_rel_bias, m_w_out, m_norm_ffn, m_w_up, m_w_down, m_norm_final, v_norm_mix, v_w_in, v_w_gla_gate, v_b_gla_gate, v_gla_norm, v_w_dw, v_b_dw, v_conv_ln_g, v_conv_ln_b, v_rel_bias, v_w_out, v_norm_ffn, v_w_up, v_w_down, v_norm_final):
    mx, my, mc = _mesh_pos()
    me = 4 * mx + 2 * my + mc
    chip_idx = (2 * mx + my).astype(jnp.int32).reshape(1)
    core_idx = mc.astype(jnp.int32).reshape(1)
    x0, target = x[0], loss_target[0]

    def pair_sums(parts, r1):
        return [_pair_sum(p, r, core_idx, p.shape[2]) for p, r in zip(parts, r1)]

    sh = [dict(w_in=w_in[l].astype(BF16), w_out=w_out[l].astype(BF16), w_up=w_up[l].astype(BF16),
               w_down=w_down[l].astype(BF16)) for l in range(DEPTH)]
    dw_flat = _pad_to(w_dw, (DEPTH, 32, 32)).reshape(16, 128)
    st_in0, st_dw = _allgather([sh[0]["w_in"], dw_flat], "allgather_first")
    dw_all = st_dw.reshape(N_DEV, DEPTH, 32, 32)[:, :, :KCONV, :]
    dw_all = jnp.transpose(dw_all, (1, 2, 0, 3)).reshape(DEPTH, KCONV, 256)
    wl = [_layer_small(l, dw_all[l], norm_mix, w_gla_gate, b_gla_gate, gla_norm, b_dw, conv_ln_g, conv_ln_b,
                       rel_bias, norm_ffn) for l in range(DEPTH)]

    s0, s1 = sh[0], sh[1]
    half = s0["w_down"].shape[0] // 2
    down0_a, down0_b = s0["w_down"][:half], s0["w_down"][half:]
    sv0, g0 = _mixers_fwd(x0, wl[0], _w_in_padded(st_in0), plan=dict(
        inproj=lambda r: [_ag_spread([s0["w_out"], down0_a])],
        att=lambda r: [_ag_spread([s0["w_up"]]), _ag_pass(r["inproj"])],
        gla=lambda r: [_ag_spread([down0_b]), _ag_pass(r["att"][:1])],
        conv=lambda r: [_ag_pass(r["gla"][:1])]))
    st_out0, st_down0_a = g0["att"][1:]
    st_up0, st_down0_b = g0["gla"][1], g0["conv"][0]
    st_down0 = jnp.concatenate([st_down0_a, st_down0_b], axis=1)
    wo0 = st_out0.reshape(D, D)
    (h1_0, xn2t_0, h2_0, act_0), (st_in1, out1_half, up1_half) = _outproj_mlp_fwd(
        x0, sv0["o_gla"], sv0["o_conv"], sv0["o_att"], wo0, wl[0]["norm_ffn"], st_up0, st_down0,
        jobs=[_ag_both([s1["w_in"]]), _ag_spread([s1["w_out"], s1["w_up"]])])

    sv1, g1 = _mixers_fwd(h2_0, wl[1], _w_in_padded(st_in1), plan=dict(
        inproj=lambda r: [_ag_pass([out1_half, up1_half])],
        att=lambda r: [_ag_spread([s1["w_down"]])],
        gla=lambda r: [_ag_pass(r["att"])]))
    (st_out1, st_up1), st_down1 = g1["inproj"], g1["gla"][0]
    wo1 = st_out1.reshape(D, D)
    (h1_1, xn2t_1, h2_1, act_1), _ = _outproj_mlp_fwd(
        h2_0, sv1["o_gla"], sv1["o_conv"], sv1["o_att"], wo1, wl[1]["norm_ffn"], st_up1, st_down1)

    loss8, dh, dhb, d_nf = _loss_fwd_bwd(h2_1, norm_final[None, :], target)
    loss = lax.psum(loss8[0, 0], ("x", "y", "c"))

    def layer_bwd(dh_pair, sv, wl_l, xn2t, act, h1, wo, st_up, st_down, mlp_jobs, x_jobs_fn):
        (dh1, dw_up, dw_down, d_nffn), mlp_res = _mlp_bwd(xn2t, act, h1, dh_pair[0], dh_pair[1], wl_l["norm_ffn"],
                                                           st_up, st_down, jobs=mlp_jobs)
        ud = [dw_up, dw_down]
        (d_ogla, d_oconv, d_oatt, dw_out), r1 = _outproj_bwd(
            dh1, sv["o_gla"], sv["o_conv"], sv["o_att"], wo, jobs=[_rs_swap(ud)])
        pair_ud = pair_sums(ud, r1)
        att_grads, r = _att_bwd(sv["aq"], sv["ak"], sv["av"], sv["bias"], sv["o_att"], sv["lse"], d_oatt,
                                jobs=[_rs_ici(pair_ud), _rs_swap([dw_out])])
        r2_ud, r1_out = r[:2], r[2:]
        pair_out = pair_sums([dw_out], r1_out)
        dh_in, _, small, r2_out, x_res = _mixers_bwd(sv, wl_l, dh1, d_ogla, d_oconv, att_grads,
                                                     conv_jobs=[_rs_ici(pair_out)], x_jobs_fn=x_jobs_fn)
        small["norm_ffn"] = d_nffn
        sums = dict(w_out=(pair_out[0], r2_out[0]), w_up=(pair_ud[0], r2_ud[0]), w_down=(pair_ud[1], r2_ud[1]))
        return dh_in, small, sums, mlp_res, x_res

    stash = {}

    def swap_w_in(dw_in):
        stash["in1"] = [_dw_in_shards(dw_in)]
        return [_rs_swap(stash["in1"])]

    dh_pair, small1, sums1, _, r1_in1 = layer_bwd((dh, dhb), sv1, wl[1], xn2t_1, act_1, h1_1, wo1, st_up1, st_down1,
                                                  (), swap_w_in)
    pair_in1 = pair_sums(stash["in1"], r1_in1)

    def send_w_in(dw_in):
        in0 = [_dw_in_shards(dw_in)]
        stash["pair_in0"] = pair_sums(in0, _comm_call([_rs_swap(in0)], "rs_swap_w_in_0"))
        return [_rs_ici(stash["pair_in0"])]

    (dx, _), small0, sums0, r2_in1, r2_in0 = layer_bwd(dh_pair, sv0, wl[0], xn2t_0, act_0, h1_0, wo0, st_up0, st_down0,
                                                       [_rs_ici(pair_in1)], send_w_in)
    sums1["w_in"] = (pair_in1[0], r2_in1[0])
    sums0["w_in"] = (stash["pair_in0"][0], r2_in0[0])

    big_w = dict(w_in=(w_in, m_w_in, v_w_in), w_out=(w_out, m_w_out, v_w_out), w_up=(w_up, m_w_up, v_w_up),
                 w_down=(w_down, m_w_down, v_w_down))
    pairs = {1: sums1, 0: sums0}
    big_out = {}
    for name, (w_, m_, v_) in big_w.items():
        res = None
        for l in (1, 0):
            a_own, r2_ = pairs[l][name]
            res = _rs_adamw(a_own, r2_, w_, m_, v_, l, chip_idx, RS_ROWS[name], prev=res)
        big_out[name] = res

    grads = (small0, small1)
    parts = dict(
        norm_mix=jnp.concatenate([grads[l]["norm_mix"] for l in range(DEPTH)], axis=0),
        norm_ffn=jnp.concatenate([grads[l]["norm_ffn"] for l in range(DEPTH)], axis=0),
        norm_final=d_nf,
        gla_norm=jnp.concatenate([grads[l]["gn"] for l in range(DEPTH)], axis=0),
        b_gla_gate=jnp.concatenate([grads[l]["bg"][:, :192] for l in range(DEPTH)], axis=0),
        b_dw=jnp.concatenate([grads[l]["b_dw"] for l in range(DEPTH)], axis=0),
        conv_ln_g=jnp.concatenate([grads[l]["ln_g"] for l in range(DEPTH)], axis=0),
        conv_ln_b=jnp.concatenate([grads[l]["ln_b"] for l in range(DEPTH)], axis=0),
        rel_bias=jnp.concatenate([grads[l]["rb"][:6, :N_REL] for l in range(DEPTH)], axis=0),
        w_gla_gate=jnp.concatenate([grads[l]["wg"][:16, :192] for l in range(DEPTH)], axis=0),
        w_dw=jnp.concatenate([grads[l]["w_dw"][:KCONV] for l in range(DEPTH)], axis=0),
    )
    small_all = _allgather([_pack_small(parts)], "allgather_small")[0]
    sg = _unpack_small(_small_sum(small_all))
    dw_grad = lax.dynamic_slice_in_dim(sg["w_dw"].reshape(DEPTH, KCONV, 256), me * 32, 32, axis=2)
    small_g = dict(
        norm_mix=sg["norm_mix"], w_gla_gate=sg["w_gla_gate"].reshape(DEPTH, 16, 192), b_gla_gate=sg["b_gla_gate"],
        gla_norm=sg["gla_norm"], w_dw=dw_grad, b_dw=sg["b_dw"], conv_ln_g=sg["conv_ln_g"],
        conv_ln_b=sg["conv_ln_b"], rel_bias=sg["rel_bias"].reshape(DEPTH, 6, N_REL), norm_ffn=sg["norm_ffn"],
        norm_final=sg["norm_final"].reshape(D))
    small_names = ("norm_mix", "w_gla_gate", "b_gla_gate", "gla_norm", "w_dw", "b_dw", "conv_ln_g", "conv_ln_b",
                   "rel_bias", "norm_ffn", "norm_final")
    small_w = dict(norm_mix=norm_mix, w_gla_gate=w_gla_gate, b_gla_gate=b_gla_gate, gla_norm=gla_norm, w_dw=w_dw,
                   b_dw=b_dw, conv_ln_g=conv_ln_g, conv_ln_b=conv_ln_b, rel_bias=rel_bias, norm_ffn=norm_ffn,
                   norm_final=norm_final)
    small_m = dict(norm_mix=m_norm_mix, w_gla_gate=m_w_gla_gate, b_gla_gate=m_b_gla_gate, gla_norm=m_gla_norm,
                   w_dw=m_w_dw, b_dw=m_b_dw, conv_ln_g=m_conv_ln_g, conv_ln_b=m_conv_ln_b, rel_bias=m_rel_bias,
                   norm_ffn=m_norm_ffn, norm_final=m_norm_final)
    small_v = dict(norm_mix=v_norm_mix, w_gla_gate=v_w_gla_gate, b_gla_gate=v_b_gla_gate, gla_norm=v_gla_norm,
                   w_dw=v_w_dw, b_dw=v_b_dw, conv_ln_g=v_conv_ln_g, conv_ln_b=v_conv_ln_b, rel_bias=v_rel_bias,
                   norm_ffn=v_norm_ffn, norm_final=v_norm_final)
    s_delta, s_m, s_v = _adamw_small([small_w[n] for n in small_names], [small_g[n] for n in small_names],
                                     [small_m[n] for n in small_names], [small_v[n] for n in small_names])
    s_idx = {n: t for t, n in enumerate(small_names)}

    order = ("norm_mix", "w_in", "w_gla_gate", "b_gla_gate", "gla_norm", "w_dw", "b_dw", "conv_ln_g", "conv_ln_b",
             "rel_bias", "w_out", "norm_ffn", "w_up", "w_down", "norm_final")

    def pick(kind, name):
        if name in big_out:
            return big_out[name][kind]
        t = s_idx[name]
        return (small_g[name], s_delta[t], s_m[t], s_v[t])[kind]

    outs = [loss, dx[None]]
    for kind in range(4):
        outs += [pick(kind, n) for n in order]
    return tuple(outs)
```

```python
import functools

import jax
import jax.numpy as jnp
from jax import lax
from jax.experimental import pallas as pl
from jax.experimental.pallas import tpu as pltpu

F32 = jnp.float32
BF16 = jnp.bfloat16
MESH = pl.DeviceIdType.MESH

D = 1024
DEPTH = 2
CH = 64
EPS = 1e-6
NEG = -1e30
N_DEV = 8
N_REL = 257
Q_SCALE = 48.0 ** -0.5
A_SCALE = 64.0 ** -0.5
GATE_TAU = 16.0
KCONV = 31

OQ, OKK, OV, OG, OCU, OAQ, OAK, OAV, OLR, DINP = 0, 256, 512, 896, 1280, 1792, 2176, 2560, 2944, 3072
IN_GROUPS = ((OQ, 256), (OKK, 256), (OV, 384), (OG, 384), (OCU, 512), (OAQ, 384), (OAK, 384), (OAV, 384), (OLR, 128))

AQ_BLK = 256
AK_WIN = 768
WIN_LEFT = 2
RB_W = 1536

ADAM_LR, ADAM_B1, ADAM_B2, ADAM_EPS, ADAM_WD, ADAM_STEP = 0.001, 0.9, 0.999, 1e-08, 0.01, 10


V7X_VMEM_MIB = 64
VMEM_LIMIT_MIB = V7X_VMEM_MIB - 1


def _cp(sem=None):
    kw = {"vmem_limit_bytes": VMEM_LIMIT_MIB * 1024 * 1024}
    if sem is not None:
        kw["dimension_semantics"] = sem
    return pltpu.CompilerParams(**kw)


def _dot(a, b):
    return jnp.dot(a, b, preferred_element_type=F32)


def _dot_nt(a, b):
    return lax.dot_general(a, b, (((1,), (1,)), ((), ())), preferred_element_type=F32)


def _dot_tn(a, b):
    return lax.dot_general(a, b, (((0,), (0,)), ((), ())), preferred_element_type=F32)


def _split2(a):
    hi = a.astype(BF16)
    lo = (a - hi.astype(F32)).astype(BF16)
    return hi, lo


def _split3(a):
    hi = a.astype(BF16)
    r1 = a - hi.astype(F32)
    mid = r1.astype(BF16)
    lo = (r1 - mid.astype(F32)).astype(BF16)
    return hi, mid, lo


def _sigmoid(x):
    return 1.0 / (1.0 + jnp.exp(-x))


def _group(idx, size, n):
    g = jnp.zeros_like(idx)
    for t in range(1, n):
        g = g + (idx >= t * size).astype(jnp.int32)
    return g


def _rms_bwd(dy, x, r, gamma):
    xh = x * r
    dxh = dy * gamma
    dx = r * (dxh - xh * jnp.mean(dxh * xh, axis=-1, keepdims=True))
    return dx, jnp.sum(dy * xh, axis=0, keepdims=True)


def _row_spec(tm, n):
    return pl.BlockSpec((tm, n), lambda i: (i, 0))


def _full_spec(shape):
    nd = len(shape)
    return pl.BlockSpec(shape, lambda *_: (0,) * nd)


def _any_spec():
    return pl.BlockSpec(memory_space=pl.ANY)


class _Job:
    def __init__(self, operands, out_shapes, sems, start, finish, aliases=None):
        self.operands, self.out_shapes, self.sems = list(operands), list(out_shapes), list(sems)
        self.start, self.finish, self.aliases = start, finish, dict(aliases or {})


def _pcall(body, *, name, grid, in_specs, out_specs, out_shape, operands, scratch_shapes=(), sem=None, jobs=()):
    jobs = list(jobs)
    in_specs, out_specs, out_shape = list(in_specs), list(out_specs), list(out_shape)
    scratch_shapes = list(scratch_shapes)
    n_in, n_out, n_scr = len(in_specs), len(out_specs), len(scratch_shapes)
    j_in = [a for j in jobs for a in j.operands]
    j_out = [s for j in jobs for s in j.out_shapes]
    j_sem = [s for j in jobs for s in j.sems]
    aliases, io, oo = {}, n_in, n_out
    for j in jobs:
        for a, b in j.aliases.items():
            aliases[io + a] = oo + b
        io += len(j.operands)
        oo += len(j.out_shapes)

    def wrapped(*refs):
        own_in, ji = refs[:n_in], refs[n_in:n_in + len(j_in)]
        o0 = n_in + len(j_in)
        own_out, jo = refs[o0:o0 + n_out], refs[o0 + n_out:o0 + n_out + len(j_out)]
        s0 = o0 + n_out + len(j_out)
        own_scr, js = refs[s0:s0 + n_scr], refs[s0 + n_scr:]

        def each_job(fn_name):
            a = b = c = 0
            for j in jobs:
                na, nb, nc = len(j.operands), len(j.out_shapes), len(j.sems)
                getattr(j, fn_name)(ji[a:a + na], jo[b:b + nb], js[c:c + nc])
                a, b, c = a + na, b + nb, c + nc

        if jobs and grid:
            pids = [pl.program_id(d) for d in range(len(grid))]
            first = functools.reduce(jnp.logical_and, [p == 0 for p in pids])
            last = functools.reduce(jnp.logical_and, [p == g - 1 for p, g in zip(pids, grid)])
            pl.when(first)(lambda: each_job("start"))
        elif jobs:
            each_job("start")

        body(*own_in, *own_out, *own_scr)

        if jobs and grid:
            pl.when(last)(lambda: each_job("finish"))
        elif jobs:
            each_job("finish")

    res = pl.pallas_call(
        wrapped, name=name, grid=grid,
        in_specs=in_specs + [_any_spec()] * len(j_in), out_specs=out_specs + [_any_spec()] * len(j_out),
        out_shape=out_shape + j_out, scratch_shapes=scratch_shapes + j_sem,
        input_output_aliases=aliases, compiler_params=_cp(sem),
    )(*operands, *j_in)
    return res[:n_out], res[n_out:]


ATT_HEADS = 6
HEAD_PAD = 128
ATT_WIDE = ATT_HEADS * HEAD_PAD
ATT_GROUP_OFFS = (OAQ, OAK, OAV)


def _store_head_padded(o_ref, part):
    o_ref[...] = jnp.zeros_like(o_ref)
    for hd in range(ATT_HEADS):
        o_ref[:, hd * HEAD_PAD:hd * HEAD_PAD + 64] = part[:, hd * 64:(hd + 1) * 64]


def _inproj_fwd(h, gamma, w, jobs=()):
    T = h.shape[0]
    tm = 512

    def body(h_ref, g_ref, w_ref, *outs):
        x = h_ref[...]
        r = lax.rsqrt(jnp.mean(x * x, axis=-1, keepdims=True) + EPS)
        xn = (x * r * g_ref[...]).astype(BF16)
        p = _dot(xn, w_ref[...])
        for o_ref, (off, n) in zip(outs, IN_GROUPS):
            part = p[:, off:off + n].astype(BF16)
            if off in ATT_GROUP_OFFS:
                _store_head_padded(o_ref, part)
            else:
                o_ref[...] = part

    widths = [ATT_WIDE if off in ATT_GROUP_OFFS else n for off, n in IN_GROUPS]
    return _pcall(
        body, name="inproj_fwd", grid=(T // tm,),
        in_specs=[_row_spec(tm, D), _full_spec((1, D)), _full_spec((D, DINP))],
        out_specs=[_row_spec(tm, n) for n in widths],
        out_shape=[jax.ShapeDtypeStruct((T, n), BF16) for n in widths],
        sem=("arbitrary",), operands=(h, gamma, w), jobs=jobs)


def _inproj_norm(h_ref, g_ref):
    x = h_ref[...]
    r = lax.rsqrt(jnp.mean(x * x, axis=-1, keepdims=True) + EPS)
    return x, r, g_ref[...]


def _inproj_bwd_w(h, gamma, dparts):
    T = h.shape[0]
    tm = 512
    nt = T // tm

    def body(h_ref, g_ref, *rest):
        dp_refs = rest[:9]
        dw_ref, acc = rest[9:]
        i = pl.program_id(0)

        @pl.when(i == 0)
        def _():
            acc[...] = jnp.zeros_like(acc)

        x, r, gamma_ = _inproj_norm(h_ref, g_ref)
        xnt = jnp.transpose((x * r * gamma_).astype(BF16))
        acc[...] += _dot(xnt, jnp.concatenate([d_ref[...] for d_ref in dp_refs], axis=1))

        @pl.when(i == nt - 1)
        def _():
            dw_ref[...] = acc[...].astype(BF16)

    return pl.pallas_call(
        body, name="inproj_bwd_w", grid=(nt,),
        in_specs=[_row_spec(tm, D), _full_spec((1, D))] + [_row_spec(tm, n) for _, n in IN_GROUPS],
        out_specs=_full_spec((D, DINP)),
        out_shape=jax.ShapeDtypeStruct((D, DINP), BF16),
        scratch_shapes=[pltpu.VMEM((D, DINP), F32)],
        compiler_params=_cp(("arbitrary",)),
    )(h, gamma, *dparts)


def _inproj_bwd_x(h, dh_in, gamma, w, dparts, jobs=()):
    T = h.shape[0]
    tm = 512

    def body(h_ref, dhin_ref, g_ref, w_ref, *rest):
        dp_refs = rest[:9]
        dh_ref, dhb_ref, dg_ref = rest[9:]

        @pl.when(pl.program_id(0) == 0)
        def _():
            dg_ref[...] = jnp.zeros_like(dg_ref)

        x, r, gamma_ = _inproj_norm(h_ref, g_ref)
        dxn = _dot_nt(jnp.concatenate([d_ref[...] for d_ref in dp_refs], axis=1), w_ref[...])
        dx, dgam = _rms_bwd(dxn, x, r, gamma_)
        dh = dhin_ref[...] + dx
        dh_ref[...] = dh
        dhb_ref[...] = dh.astype(BF16)
        dg_ref[...] += dgam

    return _pcall(
        body, name="inproj_bwd_x", grid=(T // tm,),
        in_specs=[_row_spec(tm, D), _row_spec(tm, D), _full_spec((1, D)), _full_spec((D, DINP))]
        + [_row_spec(tm, n) for _, n in IN_GROUPS],
        out_specs=[_row_spec(tm, D), _row_spec(tm, D), _full_spec((1, D))],
        out_shape=[jax.ShapeDtypeStruct((T, D), F32), jax.ShapeDtypeStruct((T, D), BF16),
                   jax.ShapeDtypeStruct((1, D), F32)],
        sem=("arbitrary",), operands=(h, dh_in, gamma, w, *dparts), jobs=jobs)


GLA_ROWS = 512
GLA_NC = GLA_ROWS // CH


def _gla_consts():
    ri = lax.broadcasted_iota(jnp.int32, (CH, CH), 0)
    ci = lax.broadcasted_iota(jnp.int32, (CH, CH), 1)
    upper = (ci > ri).astype(BF16)
    vv = lax.broadcasted_iota(jnp.int32, (384, 256), 0)
    kk = lax.broadcasted_iota(jnp.int32, (384, 256), 1)
    mask_t = ((_group(vv, 96, 4) == _group(kk, 48, 4)) & (kk < 192)).astype(F32)
    pi = lax.broadcasted_iota(jnp.int32, (384, 384), 0)
    pj = lax.broadcasted_iota(jnp.int32, (384, 384), 1)
    same_head = (_group(pi, 96, 4) == _group(pj, 96, 4)).astype(BF16)
    return upper, mask_t, same_head


def _gla_gate(lr_ref, wg_ref, bg_ref):
    z = _dot(lr_ref[...], wg_ref[...]) + bg_ref[...]
    la = (jnp.minimum(z, 0.0) - jnp.log(1.0 + jnp.exp(-jnp.abs(z)))) * (1.0 / GATE_TAU)
    return z, la


def _gla_chunk_decay(la_c, upper):
    hi, lo = _split2(la_c)
    dec = _dot(upper, hi) + _dot(upper, lo)
    end = jnp.sum(la_c, axis=0, keepdims=True)
    return jnp.exp(dec), jnp.exp(end)


def _head_mean(x, same_head):
    hi, lo = _split2(x)
    return (_dot(hi, same_head) + _dot(lo, same_head)) * (1.0 / 96.0)


def _gla_fwd(q, k, v, g, lr, wg, bg, gn, jobs=()):
    T = q.shape[0]
    nb = T // GLA_ROWS

    def body(q_ref, k_ref, v_ref, g_ref, lr_ref, wg_ref, bg_ref, gn_ref, y_ref, st_ref, s_scr, o_scr, kv_scr):
        upper, mask_t, same_head = _gla_consts()

        @pl.when(pl.program_id(0) == 0)
        def _():
            s_scr[...] = jnp.zeros_like(s_scr)

        _, la = _gla_gate(lr_ref, wg_ref, bg_ref)
        decays = []
        for c in range(GLA_NC):
            rs = slice(c * CH, (c + 1) * CH)
            w, a = _gla_chunk_decay(la[rs], upper)
            decays.append(a)
            kd = (k_ref[rs, :].astype(F32) * w).astype(BF16)
            kv_scr[c] = _dot_tn(v_ref[rs, :], kd) * mask_t
        for c in range(GLA_NC):
            s_new = s_scr[...] * decays[c] + kv_scr[c]
            s_scr[...] = s_new
            st_ref[c] = s_new.astype(BF16)
        for c in range(GLA_NC):
            rs = slice(c * CH, (c + 1) * CH)
            qs = (q_ref[rs, :].astype(F32) * Q_SCALE).astype(BF16)
            o_scr[rs, :] = _dot_nt(qs, st_ref[c])
        o = o_scr[...]
        r = lax.rsqrt(_head_mean(o * o, same_head) + EPS)
        gf = g_ref[...].astype(F32)
        y_ref[...] = (o * r * gn_ref[...] * (gf * _sigmoid(gf))).astype(BF16)

    return _pcall(
        body, name="gla_fwd", grid=(nb,),
        in_specs=[_row_spec(GLA_ROWS, 256), _row_spec(GLA_ROWS, 256), _row_spec(GLA_ROWS, 384),
                  _row_spec(GLA_ROWS, 384), _row_spec(GLA_ROWS, 128),
                  _full_spec((128, 256)), _full_spec((1, 256)), _full_spec((1, 384))],
        out_specs=[_row_spec(GLA_ROWS, 384), pl.BlockSpec((GLA_NC, 384, 256), lambda i: (i, 0, 0))],
        out_shape=[jax.ShapeDtypeStruct((T, 384), BF16), jax.ShapeDtypeStruct((T // CH, 384, 256), BF16)],
        scratch_shapes=[pltpu.VMEM((384, 256), F32), pltpu.VMEM((GLA_ROWS, 384), F32),
                        pltpu.VMEM((GLA_NC, 384, 256), F32)],
        sem=("arbitrary",), operands=(q, k, v, g, lr, wg, bg, gn), jobs=jobs)


def _gla_bwd(q, k, v, g, lr, states, dy, wg, bg, gn):
    T = q.shape[0]
    nb = T // GLA_ROWS

    def rev(s):
        return nb - 1 - s

    def body(q_ref, k_ref, v_ref, g_ref, lr_ref, st_ref, stp_ref, dy_ref, wg_ref, bg_ref, gn_ref,
             dq_ref, dk_ref, dv_ref, dg_ref, dlr_ref, dwg_ref, dbg_ref, dgn_ref,
             d_scr, an_scr, o_scr, do_scr, dla_scr, dst_scr):
        upper, mask_t, same_head = _gla_consts()
        s = pl.program_id(0)
        blk = rev(s)

        @pl.when(s == 0)
        def _():
            d_scr[...] = jnp.zeros_like(d_scr)
            an_scr[...] = jnp.zeros_like(an_scr)
            dwg_ref[...] = jnp.zeros_like(dwg_ref)
            dbg_ref[...] = jnp.zeros_like(dbg_ref)
            dgn_ref[...] = jnp.zeros_like(dgn_ref)

        z, la = _gla_gate(lr_ref, wg_ref, bg_ref)
        ws, as_, qss, kds = [], [], [], []
        for c in range(GLA_NC):
            rs = slice(c * CH, (c + 1) * CH)
            w, a = _gla_chunk_decay(la[rs], upper)
            ws.append(w)
            as_.append(a)
            qs = (q_ref[rs, :].astype(F32) * Q_SCALE).astype(BF16)
            qss.append(qs)
            kds.append((k_ref[rs, :].astype(F32) * w).astype(BF16))
            o_scr[rs, :] = _dot_nt(qs, st_ref[c])
        o = o_scr[...]
        r = lax.rsqrt(_head_mean(o * o, same_head) + EPS)
        on = o * r
        gf = g_ref[...].astype(F32)
        sg = _sigmoid(gf)
        si = gf * sg
        dyf = dy_ref[...].astype(F32)
        gn_ = gn_ref[...]
        dgn_ref[...] += jnp.sum(dyf * si * on, axis=0, keepdims=True)
        dg_ref[...] = (dyf * on * gn_ * (sg * (1.0 + gf * (1.0 - sg)))).astype(BF16)
        d_on = dyf * si * gn_
        do_scr[...] = r * (d_on - on * _head_mean(d_on * on, same_head))

        for c in range(GLA_NC):
            rs = slice(c * CH, (c + 1) * CH)
            dst_scr[c] = _dot_tn(do_scr[rs, :].astype(BF16), qss[c]) * mask_t
        for c in reversed(range(GLA_NC)):
            dt = d_scr[...] * an_scr[...] + dst_scr[c]
            d_scr[...] = dt
            dst_scr[c] = dt
            an_scr[...] = as_[c]
        first = (blk > 0).astype(F32)
        for c in range(GLA_NC):
            rs = slice(c * CH, (c + 1) * CH)
            dob = do_scr[rs, :].astype(BF16)
            if c > 0:
                s_prev = st_ref[c - 1].astype(F32)
            else:
                s_prev = stp_ref[0].astype(F32) * first
            dq_ref[rs, :] = (_dot(dob, st_ref[c]) * Q_SCALE).astype(BF16)
            dt = dst_scr[c]
            da = jnp.sum(dt * s_prev, axis=0, keepdims=True)
            db = dt.astype(BF16)
            dkd = _dot(v_ref[rs, :], db)
            dv_ref[rs, :] = _dot_nt(kds[c], db).astype(BF16)
            dk_ref[rs, :] = (dkd * ws[c]).astype(BF16)
            ddec = dkd * k_ref[rs, :].astype(F32) * ws[c]
            hi, lo = _split2(ddec)
            dla_scr[rs, :] = _dot_tn(upper, hi) + _dot_tn(upper, lo) + as_[c] * da

        dz = dla_scr[...] * (1.0 - _sigmoid(z)) * (1.0 / GATE_TAU)
        dzb = dz.astype(BF16)
        dlr_ref[...] = _dot_nt(dzb, wg_ref[...]).astype(BF16)
        dwg_ref[...] += _dot_tn(lr_ref[...], dzb)
        dbg_ref[...] += jnp.sum(dz, axis=0, keepdims=True)

    def rspec(n):
        return pl.BlockSpec((GLA_ROWS, n), lambda s: (rev(s), 0))

    return pl.pallas_call(
        body, name="gla_bwd", grid=(nb,),
        in_specs=[rspec(256), rspec(256), rspec(384), rspec(384), rspec(128),
                  pl.BlockSpec((GLA_NC, 384, 256), lambda s: (rev(s), 0, 0)),
                  pl.BlockSpec((1, 384, 256), lambda s: (jnp.maximum(rev(s) * GLA_NC - 1, 0), 0, 0)),
                  rspec(384), _full_spec((128, 256)), _full_spec((1, 256)), _full_spec((1, 384))],
        out_specs=[rspec(256), rspec(256), rspec(384), rspec(384), rspec(128),
                   _full_spec((128, 256)), _full_spec((1, 256)), _full_spec((1, 384))],
        out_shape=[jax.ShapeDtypeStruct((T, 256), BF16), jax.ShapeDtypeStruct((T, 256), BF16),
                   jax.ShapeDtypeStruct((T, 384), BF16), jax.ShapeDtypeStruct((T, 384), BF16),
                   jax.ShapeDtypeStruct((T, 128), BF16),
                   jax.ShapeDtypeStruct((128, 256), F32), jax.ShapeDtypeStruct((1, 256), F32),
                   jax.ShapeDtypeStruct((1, 384), F32)],
        scratch_shapes=[pltpu.VMEM((384, 256), F32), pltpu.VMEM((1, 256), F32),
                        pltpu.VMEM((GLA_ROWS, 384), F32), pltpu.VMEM((GLA_ROWS, 384), F32),
                        pltpu.VMEM((GLA_ROWS, 256), F32), pltpu.VMEM((GLA_NC, 384, 256), F32)],
        compiler_params=_cp(("arbitrary",)),
    )(q, k, v, g, lr, states, states, dy, wg, bg, gn)


CONV_ROWS = 512
HALO = 32
SUBL = 8
CONV_SLAB = 32
PHASE_ROWS = CONV_ROWS + HALO - SUBL
FWD_SHIFT = tuple(HALO - (KCONV - 1) + j for j in range(KCONV))
BWD_SHIFT = tuple(KCONV - 1 - j for j in range(KCONV))


def _fill_phases(buf, ph):
    for f in range(1, SUBL):
        ph[f, 0:PHASE_ROWS, :] = buf[pl.ds(f, PHASE_ROWS), :]


def _tap(buf, ph, shift, r, n):
    f, base = shift % SUBL, shift - shift % SUBL
    src = buf if f == 0 else ph.at[f]
    return src[pl.ds(base + r, n), :]


def _taps_apply(w_ref, buf, ph, shifts, out):
    for r in range(0, CONV_ROWS, CONV_SLAB):
        acc = jnp.zeros((CONV_SLAB, 256), F32)
        for j in range(KCONV):
            acc = acc + w_ref[j:j + 1, :] * _tap(buf, ph, shifts[j], r, CONV_SLAB)
        out[r:r + CONV_SLAB, :] = acc


def _conv_scratch():
    return [pltpu.VMEM((CONV_ROWS + HALO, 256), F32), pltpu.VMEM((SUBL, CONV_ROWS + HALO, 256), F32),
            pltpu.VMEM((CONV_ROWS, 256), F32)]


def _conv_common(cu_ref, halo_ref, w_ref, b_ref, lg_ref, lb_ref, buf, ph, cbuf, blk, conv_ref=None):
    u = cu_ref[...].astype(F32)
    a = u[:, :256]
    sb = _sigmoid(u[:, 256:])
    uh = halo_ref[...].astype(F32)
    hh = uh[:, :256] * _sigmoid(uh[:, 256:]) * (blk > 0).astype(F32)
    buf[0:HALO, :] = hh
    buf[HALO:HALO + CONV_ROWS, :] = a * sb
    _fill_phases(buf, ph)
    if conv_ref is None:
        _taps_apply(w_ref, buf, ph, FWD_SHIFT, cbuf)
        conv = cbuf[...]
    else:
        conv = conv_ref[...]
    cc = conv + b_ref[...]
    mu = jnp.mean(cc, axis=-1, keepdims=True)
    xc = cc - mu
    rstd = lax.rsqrt(jnp.mean(xc * xc, axis=-1, keepdims=True) + EPS)
    n = xc * rstd
    yln = n * lg_ref[...] + lb_ref[...]
    return a, sb, n, rstd, yln, conv


def _conv_fwd(cu, w, b, lg, lb, jobs=()):
    T = cu.shape[0]
    nb = T // CONV_ROWS
    per = CONV_ROWS // HALO

    def body(cu_ref, halo_ref, w_ref, b_ref, lg_ref, lb_ref, y_ref, conv_ref, buf, ph, cbuf):
        _, _, _, _, yln, conv = _conv_common(cu_ref, halo_ref, w_ref, b_ref, lg_ref, lb_ref, buf, ph, cbuf,
                                             pl.program_id(0))
        y_ref[...] = (yln * _sigmoid(yln)).astype(BF16)
        conv_ref[...] = conv

    return _pcall(
        body, name="conv_fwd", grid=(nb,),
        in_specs=[_row_spec(CONV_ROWS, 512),
                  pl.BlockSpec((HALO, 512), lambda i: (jnp.maximum(i * per - 1, 0), 0)),
                  _full_spec((32, 256)), _full_spec((1, 256)), _full_spec((1, 256)), _full_spec((1, 256))],
        out_specs=[_row_spec(CONV_ROWS, 256), _row_spec(CONV_ROWS, 256)],
        out_shape=[jax.ShapeDtypeStruct((T, 256), BF16), jax.ShapeDtypeStruct((T, 256), F32)],
        scratch_shapes=_conv_scratch(),
        sem=("arbitrary",), operands=(cu, cu, w, b, lg, lb), jobs=jobs)


def _conv_bwd(cu, conv, dy, w, b, lg, lb, jobs=()):
    T = cu.shape[0]
    nb = T // CONV_ROWS
    per = CONV_ROWS // HALO

    def rev(s):
        return nb - 1 - s

    def body(cu_ref, halo_ref, conv_ref, dy_ref, w_ref, b_ref, lg_ref, lb_ref,
             dcu_ref, dw_ref, db_ref, dlg_ref, dlb_ref, buf, ph, cbuf, dcbuf, dph, carry):
        s = pl.program_id(0)

        @pl.when(s == 0)
        def _():
            carry[...] = jnp.zeros_like(carry)
            dw_ref[...] = jnp.zeros_like(dw_ref)
            db_ref[...] = jnp.zeros_like(db_ref)
            dlg_ref[...] = jnp.zeros_like(dlg_ref)
            dlb_ref[...] = jnp.zeros_like(dlb_ref)

        a, sb, n, rstd, yln, _ = _conv_common(cu_ref, halo_ref, w_ref, b_ref, lg_ref, lb_ref, buf, ph, cbuf, rev(s),
                                              conv_ref=conv_ref)
        sg = _sigmoid(yln)
        dyln = dy_ref[...].astype(F32) * (sg * (1.0 + yln * (1.0 - sg)))
        dlg_ref[...] += jnp.sum(dyln * n, axis=0, keepdims=True)
        dlb_ref[...] += jnp.sum(dyln, axis=0, keepdims=True)
        dn = dyln * lg_ref[...]
        dc = rstd * (dn - jnp.mean(dn, axis=-1, keepdims=True) - n * jnp.mean(dn * n, axis=-1, keepdims=True))
        db_ref[...] += jnp.sum(dc, axis=0, keepdims=True)
        dcbuf[0:CONV_ROWS, :] = dc
        dcbuf[CONV_ROWS:CONV_ROWS + HALO, :] = carry[...]
        carry[...] = dc[0:HALO, :]
        _fill_phases(dcbuf, dph)
        for j in range(KCONV):
            acc = jnp.zeros((SUBL, 256), F32)
            for r in range(0, CONV_ROWS, 2 * CONV_SLAB):
                prod = dcbuf[r:r + 2 * CONV_SLAB, :] * _tap(buf, ph, FWD_SHIFT[j], r, 2 * CONV_SLAB)
                acc = acc + jnp.sum(prod.reshape(2 * CONV_SLAB // SUBL, SUBL, 256), axis=0)
            dw_ref[j:j + 1, :] += jnp.sum(acc, axis=0, keepdims=True)
        _taps_apply(w_ref, dcbuf, dph, BWD_SHIFT, cbuf)
        dhg = cbuf[...]
        dcu_ref[...] = jnp.concatenate([dhg * sb, dhg * a * sb * (1.0 - sb)], axis=1).astype(BF16)

    def rspec(n):
        return pl.BlockSpec((CONV_ROWS, n), lambda s: (rev(s), 0))

    return _pcall(
        body, name="conv_bwd", grid=(nb,),
        in_specs=[rspec(512),
                  pl.BlockSpec((HALO, 512), lambda s: (jnp.maximum(rev(s) * per - 1, 0), 0)),
                  rspec(256), rspec(256),
                  _full_spec((32, 256)), _full_spec((1, 256)), _full_spec((1, 256)), _full_spec((1, 256))],
        out_specs=[rspec(512), _full_spec((32, 256)), _full_spec((1, 256)), _full_spec((1, 256)),
                   _full_spec((1, 256))],
        out_shape=[jax.ShapeDtypeStruct((T, 512), BF16), jax.ShapeDtypeStruct((32, 256), F32),
                   jax.ShapeDtypeStruct((1, 256), F32), jax.ShapeDtypeStruct((1, 256), F32),
                   jax.ShapeDtypeStruct((1, 256), F32)],
        scratch_shapes=_conv_scratch() + [pltpu.VMEM((CONV_ROWS + HALO, 256), F32),
                                          pltpu.VMEM((SUBL, CONV_ROWS + HALO, 256), F32),
                                          pltpu.VMEM((HALO, 256), F32)],
        sem=("arbitrary",), operands=(cu, cu, conv, dy, w, b, lg, lb), jobs=jobs)


def _rel_onehot_t(shift=0):
    r = lax.broadcasted_iota(jnp.int32, (384, RB_W), 0)
    n = lax.broadcasted_iota(jnp.int32, (384, RB_W), 1) - shift
    idx = jnp.clip(1024 - n, -128, 128) + 128
    return (idx == r).astype(BF16)


def _relbias_expand(rb):
    def body(rb_ref, out_ref):
        oh = _rel_onehot_t()
        hi, mid, lo = _split3(rb_ref[...])
        strip = _dot(hi, oh) + _dot(mid, oh) + _dot(lo, oh)
        qi = _group(lax.broadcasted_iota(jnp.int32, (AQ_BLK, AK_WIN), 0), CH, 4)
        kj = _group(lax.broadcasted_iota(jnp.int32, (AQ_BLK, AK_WIN), 1), CH, 12)
        valid = (kj >= qi) & (kj <= qi + 8)
        for hd in range(6):
            x = jnp.broadcast_to(strip[hd:hd + 1, :], (AQ_BLK, RB_W))
            xr = pltpu.roll(x, 0, 1, stride=1, stride_axis=0)
            out_ref[hd] = jnp.where(valid, xr[:, 512:512 + AK_WIN], NEG)

    return pl.pallas_call(
        body, name="relbias_expand",
        out_shape=jax.ShapeDtypeStruct((6, AQ_BLK, AK_WIN), F32),
        compiler_params=_cp(),
    )(rb)


def _relbias_grad(dbias):
    def body(db_ref, out_ref):
        oh = _rel_onehot_t(AQ_BLK - 1)
        ri = lax.broadcasted_iota(jnp.int32, (AQ_BLK, AQ_BLK), 0)
        ci = lax.broadcasted_iota(jnp.int32, (AQ_BLK, AQ_BLK), 1)
        flip = (ri + ci == AQ_BLK - 1).astype(BF16)
        rows = []
        for hd in range(6):
            hi, mid, lo = _split3(db_ref[hd])
            rev = _dot(flip, hi) + _dot(flip, mid) + _dot(flip, lo)
            x = jnp.concatenate([jnp.zeros((AQ_BLK, 512), F32), rev,
                                 jnp.zeros((AQ_BLK, RB_W - 512 - AK_WIN), F32)], axis=1)
            xr = pltpu.roll(x, 0, 1, stride=1, stride_axis=0)
            rows.append(jnp.sum(xr, axis=0, keepdims=True))
        rows.append(jnp.zeros((2, RB_W), F32))
        dstrip = jnp.concatenate(rows, axis=0)
        hi, mid, lo = _split3(dstrip)
        out_ref[...] = _dot_nt(hi, oh) + _dot_nt(mid, oh) + _dot_nt(lo, oh)

    return pl.pallas_call(
        body, name="relbias_grad",
        out_shape=jax.ShapeDtypeStruct((8, 384), F32),
        compiler_params=_cp(),
    )(dbias)


ATT_SLAB = 8
ATT_HALF = 384


def _att_logits_slab(s_scr, b_ref, hd, rows, first_key):
    kvalid = lax.broadcasted_iota(jnp.int32, (ATT_SLAB, AK_WIN), 1) >= first_key
    return jnp.where(kvalid, s_scr[rows, :] + b_ref[hd, rows, :], NEG)


def _att_softmax_slab(s_scr, b_ref, hd, rows, first_key):
    s = _att_logits_slab(s_scr, b_ref, hd, rows, first_key)
    m = jnp.max(s, axis=-1, keepdims=True)
    p = jnp.exp(s - m)
    total = jnp.sum(p, axis=-1, keepdims=True)
    return p * (1.0 / total), m + jnp.log(total)


def _att_first_key(i):
    return (8 - 4 * i) * CH


def _slab_rows(t):
    return pl.ds(t * ATT_SLAB, ATT_SLAB)


def _head_lanes(hd):
    return slice(hd * 64, (hd + 1) * 64)


WIN_BLKS = AK_WIN // AQ_BLK


def _head_tile(hd):
    return slice(hd * HEAD_PAD, (hd + 1) * HEAD_PAD)


def _win_cols(d):
    return slice(d * AQ_BLK, (d + 1) * AQ_BLK)


def _win_block(i, d):
    return jnp.maximum(i + d - WIN_LEFT, 0)


def _win_specs():
    return [pl.BlockSpec((AQ_BLK, ATT_WIDE), lambda i, d=d: (_win_block(i, d), 0)) for d in range(WIN_BLKS)]


def _att_fwd(q, k, v, bias, jobs=()):
    T = q.shape[0]
    nb = T // AQ_BLK

    def body(q_ref, k0, k1, k2, v0, v1, v2, b_ref, o_ref, lse_ref, s_scr):
        k_refs, v_refs = (k0, k1, k2), (v0, v1, v2)
        first_key = _att_first_key(pl.program_id(0))
        lse_ref[...] = jnp.zeros_like(lse_ref)

        def scores(hd):
            q_h = q_ref[:, _head_tile(hd)] * A_SCALE
            for d in range(WIN_BLKS):
                s_scr[hd % 2, :, _win_cols(d)] = _dot_nt(q_h, k_refs[d][:, _head_tile(hd)])

        scores(0)
        for hd in range(ATT_HEADS):
            if hd + 1 < ATT_HEADS:
                scores(hd + 1)
            s_h = s_scr.at[hd % 2]
            for t in range(AQ_BLK // ATT_SLAB):
                rows = _slab_rows(t)
                s_h[rows, :], lse_ref[rows, hd:hd + 1] = _att_softmax_slab(s_h, b_ref, hd, rows, first_key)
            o_h = _dot(s_h[:, _win_cols(0)].astype(BF16), v_refs[0][:, _head_tile(hd)])
            for d in range(1, WIN_BLKS):
                o_h = o_h + _dot(s_h[:, _win_cols(d)].astype(BF16), v_refs[d][:, _head_tile(hd)])
            o_ref[:, _head_lanes(hd)] = o_h[:, :64].astype(BF16)

    return _pcall(
        body, name="att_fwd", grid=(nb,),
        in_specs=[_row_spec(AQ_BLK, ATT_WIDE)] + _win_specs() + _win_specs() + [_full_spec((6, AQ_BLK, AK_WIN))],
        out_specs=[_row_spec(AQ_BLK, 384), _row_spec(AQ_BLK, 128)],
        out_shape=[jax.ShapeDtypeStruct((T, 384), BF16), jax.ShapeDtypeStruct((T, 128), F32)],
        scratch_shapes=[pltpu.VMEM((2, AQ_BLK, AK_WIN), F32)],
        sem=("arbitrary",), operands=(q, k, k, k, v, v, v, bias), jobs=jobs)


def _att_bwd(q, k, v, bias, o, lse, do, jobs=()):
    T = q.shape[0]
    nb = T // AQ_BLK

    def body(q_ref, k0, k1, k2, v0, v1, v2, b_ref, o_ref, lse_ref, do_ref, dq_ref, dk_ref, dv_ref, db_ref,
             dk_acc, dv_acc, s_scr, dp_scr, delta_scr):
        k_refs, v_refs = (k0, k1, k2), (v0, v1, v2)
        i = pl.program_id(0)

        @pl.when(i == 0)
        def _():
            dk_acc[...] = jnp.zeros_like(dk_acc)
            dv_acc[...] = jnp.zeros_like(dv_acc)
            db_ref[...] = jnp.zeros_like(db_ref)

        first_key = _att_first_key(i)

        def scores(hd):
            q_h = q_ref[:, _head_tile(hd)] * A_SCALE
            do_h = do_ref[:, _head_tile(hd)]
            delta_scr[:, hd:hd + 1] = jnp.sum(do_h[:, :64].astype(F32) * o_ref[:, _head_lanes(hd)].astype(F32),
                                              axis=-1, keepdims=True)
            for d in range(WIN_BLKS):
                s_scr[hd % 2, :, _win_cols(d)] = _dot_nt(q_h, k_refs[d][:, _head_tile(hd)])
                dp_scr[hd % 2, :, _win_cols(d)] = _dot_nt(do_h, v_refs[d][:, _head_tile(hd)])

        scores(0)
        for hd in range(ATT_HEADS):
            if hd + 1 < ATT_HEADS:
                scores(hd + 1)
            s_h, dp_h = s_scr.at[hd % 2], dp_scr.at[hd % 2]
            for t in range(AQ_BLK // ATT_SLAB):
                rows = _slab_rows(t)
                lse_s, delta_s = lse_ref[rows, hd:hd + 1], delta_scr[rows, hd:hd + 1]
                for c0 in range(0, AK_WIN, ATT_HALF):
                    cols = slice(c0, c0 + ATT_HALF)
                    kvalid = lax.broadcasted_iota(jnp.int32, (ATT_SLAB, ATT_HALF), 1) >= first_key - c0
                    p = jnp.exp(jnp.where(kvalid, s_h[rows, cols] + b_ref[hd, rows, cols], NEG) - lse_s)
                    ds = p * (dp_h[rows, cols] - delta_s)
                    db_ref[hd, rows, cols] += ds
                    s_h[rows, cols] = p
                    dp_h[rows, cols] = ds
            q_h = q_ref[:, _head_tile(hd)] * A_SCALE
            do_h = do_ref[:, _head_tile(hd)]
            ls = _head_lanes(hd)
            dq_h = jnp.zeros((AQ_BLK, HEAD_PAD), F32)
            for d in range(WIN_BLKS):
                pb = s_h[:, _win_cols(d)].astype(BF16)
                dsb = dp_h[:, _win_cols(d)].astype(BF16)
                rows = pl.ds(pl.multiple_of(_win_block(i, d) * AQ_BLK, AQ_BLK), AQ_BLK)
                dv_acc[rows, ls] += _dot_tn(pb, do_h)[:, :64]
                dk_acc[rows, ls] += _dot_tn(dsb, q_h)[:, :64]
                dq_h = dq_h + _dot(dsb, k_refs[d][:, _head_tile(hd)])
            dq_ref[:, ls] = (dq_h[:, :64] * A_SCALE).astype(BF16)

        @pl.when(i == nb - 1)
        def _():
            dk_ref[...] = dk_acc[...].astype(BF16)
            dv_ref[...] = dv_acc[...].astype(BF16)

    return _pcall(
        body, name="att_bwd", grid=(nb,),
        in_specs=[_row_spec(AQ_BLK, ATT_WIDE)] + _win_specs() + _win_specs()
        + [_full_spec((6, AQ_BLK, AK_WIN)), _row_spec(AQ_BLK, 384), _row_spec(AQ_BLK, 128),
           _row_spec(AQ_BLK, ATT_WIDE)],
        out_specs=[_row_spec(AQ_BLK, 384), _full_spec((T, 384)), _full_spec((T, 384)),
                   _full_spec((6, AQ_BLK, AK_WIN))],
        out_shape=[jax.ShapeDtypeStruct((T, 384), BF16), jax.ShapeDtypeStruct((T, 384), BF16),
                   jax.ShapeDtypeStruct((T, 384), BF16), jax.ShapeDtypeStruct((6, AQ_BLK, AK_WIN), F32)],
        scratch_shapes=[pltpu.VMEM((T, 384), F32), pltpu.VMEM((T, 384), F32),
                        pltpu.VMEM((2, AQ_BLK, AK_WIN), F32), pltpu.VMEM((2, AQ_BLK, AK_WIN), F32),
                        pltpu.VMEM((AQ_BLK, 128), F32)],
        sem=("arbitrary",), operands=(q, k, k, k, v, v, v, bias, o, lse, do), jobs=jobs)


FF_BLK = 512
N_FF = 4096 // FF_BLK
MLP_SHARDS = 2


def _outproj_mlp_fwd(h, o_gla, o_conv, o_att, w_out, gamma, w_up, w_down, jobs=()):
    T = h.shape[0]
    tm = 1024

    def body(h_ref, og_ref, oc_ref, oa_ref, wo_ref, g_ref, wu_ref, wd_ref, h1_ref, xt_ref, h2_ref, a_ref, acc, xn_ref):
        j = pl.program_id(1)

        @pl.when(j == 0)
        def _():
            wo = wo_ref[...]
            h1 = (h_ref[...] + _dot(og_ref[...], wo[0:384]) + _dot(oc_ref[...], wo[384:640])
                  + _dot(oa_ref[...], wo[640:1024]))
            h1_ref[...] = h1
            r = lax.rsqrt(jnp.mean(h1 * h1, axis=-1, keepdims=True) + EPS)
            xn = (h1 * r * g_ref[...]).astype(BF16)
            xn_ref[...] = xn
            xt_ref[...] = jnp.transpose(xn)
            acc[...] = h1

        xn_ = xn_ref[...]
        down = None
        for s in range(MLP_SHARDS):
            a = jnp.maximum(_dot(xn_, wu_ref[s]), 0.0)
            a_ref[:, s * FF_BLK:(s + 1) * FF_BLK] = a.astype(BF16)
            part = _dot((a * a).astype(BF16), wd_ref[s])
            down = part if down is None else down + part
        acc[...] += down

        @pl.when(j == N_FF // MLP_SHARDS - 1)
        def _():
            h2_ref[...] = acc[...]

    row = lambda n: pl.BlockSpec((tm, n), lambda i, j: (i, 0))
    return _pcall(
        body, name="outproj_mlp_fwd", grid=(T // tm, N_FF // MLP_SHARDS),
        in_specs=[row(D), row(384), row(256), row(384),
                  pl.BlockSpec((D, D), lambda i, j: (0, 0)), pl.BlockSpec((1, D), lambda i, j: (0, 0)),
                  pl.BlockSpec((MLP_SHARDS, D, FF_BLK), lambda i, j: (j, 0, 0)),
                  pl.BlockSpec((MLP_SHARDS, FF_BLK, D), lambda i, j: (j, 0, 0))],
        out_specs=[row(D), pl.BlockSpec((D, tm), lambda i, j: (0, i)), row(D),
                   pl.BlockSpec((tm, MLP_SHARDS * FF_BLK), lambda i, j: (i, j))],
        out_shape=[jax.ShapeDtypeStruct((T, D), F32), jax.ShapeDtypeStruct((D, T), BF16),
                   jax.ShapeDtypeStruct((T, D), F32), jax.ShapeDtypeStruct((T, N_FF * FF_BLK), BF16)],
        scratch_shapes=[pltpu.VMEM((tm, D), F32), pltpu.VMEM((tm, D), BF16)],
        sem=("arbitrary", "arbitrary"), operands=(h, o_gla, o_conv, o_att, w_out, gamma, w_up, w_down), jobs=jobs)


def _mlp_bwd(xn2t, act, h1, dh2, dh2b, gamma, w_up, w_down, jobs=()):
    T = act.shape[0]
    tm = 512
    nt = T // tm
    ns = MLP_SHARDS
    nj = N_FF // ns
    last = nj - 1

    def body(xt_ref, a_ref, h1_ref, dy_ref, dyb_ref, g_ref, wu_ref, wd_ref, dh1_ref, dwu_ref, dwd_ref, dg_ref,
             dxn_acc, acc_u, acc_d):
        j = pl.program_id(0)
        i = pl.program_id(1)
        xt = xt_ref[...]
        dyb = dyb_ref[...]
        rows = pl.ds(pl.multiple_of(i * tm, tm), tm)

        @pl.when(i == 0)
        def _():
            acc_u[...] = jnp.zeros_like(acc_u)
            acc_d[...] = jnp.zeros_like(acc_d)

        @pl.when(j == 0)
        def _():
            dxn_acc[rows, :] = jnp.zeros((tm, D), F32)

        dxn = None
        for s in range(ns):
            a = a_ref[:, s * FF_BLK:(s + 1) * FF_BLK].astype(F32)
            hh = (a * a).astype(BF16)
            du = (_dot_nt(dyb, wd_ref[s]) * (2.0 * a)).astype(BF16)
            acc_d[s] += _dot_tn(hh, dyb)
            acc_u[s] += _dot(xt, du)
            part = _dot_nt(du, wu_ref[s])
            dxn = part if dxn is None else dxn + part
        dxn_acc[rows, :] += dxn

        @pl.when(i == nt - 1)
        def _():
            for s in range(ns):
                dwu_ref[s, 0] = acc_u[s].astype(BF16)
                dwd_ref[s, 0] = acc_d[s].astype(BF16)

        @pl.when(j == last)
        def _():
            @pl.when(i == 0)
            def _():
                dg_ref[...] = jnp.zeros_like(dg_ref)

            h1 = h1_ref[...]
            r = lax.rsqrt(jnp.mean(h1 * h1, axis=-1, keepdims=True) + EPS)
            dx, dgam = _rms_bwd(dxn_acc[rows, :], h1, r, g_ref[...])
            dh1_ref[...] = dy_ref[...] + dx
            dg_ref[...] += dgam

    assert ns == 2
    late = lambda j, i: (jnp.where(j == last, i, 0), 0)
    return _pcall(
        body, name="mlp_bwd", grid=(nj, nt),
        in_specs=[pl.BlockSpec((D, tm), lambda j, i: (0, i)), pl.BlockSpec((tm, ns * FF_BLK), lambda j, i: (i, j)),
                  pl.BlockSpec((tm, D), late), pl.BlockSpec((tm, D), late),
                  pl.BlockSpec((tm, D), lambda j, i: (i, 0)), pl.BlockSpec((1, D), lambda j, i: (0, 0)),
                  pl.BlockSpec((ns, D, FF_BLK), lambda j, i: (j, 0, 0), pipeline_mode=pl.Buffered(1)),
                  pl.BlockSpec((ns, FF_BLK, D), lambda j, i: (j, 0, 0), pipeline_mode=pl.Buffered(1))],
        out_specs=[pl.BlockSpec((tm, D), late),
                   pl.BlockSpec((ns, 1, D, FF_BLK), lambda j, i: (0, j, 0, 0)),
                   pl.BlockSpec((ns, 1, FF_BLK, D), lambda j, i: (0, j, 0, 0)),
                   pl.BlockSpec((1, D), lambda j, i: (0, 0))],
        out_shape=[jax.ShapeDtypeStruct((T, D), F32), jax.ShapeDtypeStruct((2, 4, D, FF_BLK), BF16),
                   jax.ShapeDtypeStruct((2, 4, FF_BLK, D), BF16), jax.ShapeDtypeStruct((1, D), F32)],
        scratch_shapes=[pltpu.VMEM((T, D), F32), pltpu.VMEM((ns, D, FF_BLK), F32), pltpu.VMEM((ns, FF_BLK, D), F32)],
        sem=("arbitrary", "arbitrary"), operands=(xn2t, act, h1, dh2, dh2b, gamma, w_up, w_down), jobs=jobs)


def _outproj_bwd(dh1, o_gla, o_conv, o_att, w_out, jobs=()):
    T = dh1.shape[0]
    tm = 512
    nt = T // tm

    def body(dy_ref, og_ref, oc_ref, oa_ref, wo_ref, dg_ref, dc_ref, da_ref, dw_ref, acc):
        i = pl.program_id(0)

        @pl.when(i == 0)
        def _():
            acc[...] = jnp.zeros_like(acc)

        dyb = dy_ref[...].astype(BF16)
        dm = _dot_nt(dyb, wo_ref[...])
        dg_ref[...] = dm[:, 0:384].astype(BF16)
        dc_ref[...] = dm[:, 384:640].astype(BF16)
        _store_head_padded(da_ref, dm[:, 640:1024].astype(BF16))
        mixed = jnp.concatenate([og_ref[...], oc_ref[...], oa_ref[...]], axis=1)
        acc[...] += _dot_tn(mixed, dyb)

        @pl.when(i == nt - 1)
        def _():
            for j in range(N_DEV):
                dw_ref[j % 2, j // 2] = acc[j * 128:(j + 1) * 128, :].astype(BF16)

    return _pcall(
        body, name="outproj_bwd", grid=(nt,),
        in_specs=[_row_spec(tm, D), _row_spec(tm, 384), _row_spec(tm, 256), _row_spec(tm, 384),
                  _full_spec((D, D))],
        out_specs=[_row_spec(tm, 384), _row_spec(tm, 256), _row_spec(tm, ATT_WIDE), _full_spec((2, 4, 128, D))],
        out_shape=[jax.ShapeDtypeStruct((T, 384), BF16), jax.ShapeDtypeStruct((T, 256), BF16),
                   jax.ShapeDtypeStruct((T, ATT_WIDE), BF16), jax.ShapeDtypeStruct((2, 4, 128, D), BF16)],
        scratch_shapes=[pltpu.VMEM((D, D), F32)],
        sem=("arbitrary",), operands=(dh1, o_gla, o_conv, o_att, w_out), jobs=jobs)


def _loss_fwd_bwd(h, gamma, target):
    T = h.shape[0]
    tm = 512

    def body(h_ref, g_ref, t_ref, loss_ref, dh_ref, dhb_ref, dg_ref):
        @pl.when(pl.program_id(0) == 0)
        def _():
            loss_ref[...] = jnp.zeros_like(loss_ref)
            dg_ref[...] = jnp.zeros_like(dg_ref)

        x = h_ref[...]
        r = lax.rsqrt(jnp.mean(x * x, axis=-1, keepdims=True) + EPS)
        gamma_ = g_ref[...]
        e = x * r * gamma_ - t_ref[...]
        loss_ref[...] += 0.5 * jnp.sum(jnp.mean(e * e, axis=-1, keepdims=True), axis=0, keepdims=True)
        dx, dgam = _rms_bwd(e * (1.0 / D), x, r, gamma_)
        dh_ref[...] = dx
        dhb_ref[...] = dx.astype(BF16)
        dg_ref[...] += dgam

    return pl.pallas_call(
        body, name="loss_fwd_bwd", grid=(T // tm,),
        in_specs=[_row_spec(tm, D), _full_spec((1, D)), _row_spec(tm, D)],
        out_specs=[_full_spec((8, 128)), _row_spec(tm, D), _row_spec(tm, D), _full_spec((1, D))],
        out_shape=[jax.ShapeDtypeStruct((8, 128), F32), jax.ShapeDtypeStruct((T, D), F32),
                   jax.ShapeDtypeStruct((T, D), BF16), jax.ShapeDtypeStruct((1, D), F32)],
        compiler_params=_cp(("arbitrary",)),
    )(h, gamma, target)


def _adamw_math(w, g, m, v):
    m = ADAM_B1 * m + (1.0 - ADAM_B1) * g
    v = ADAM_B2 * v + (1.0 - ADAM_B2) * (g * g)
    m_hat = m / (1.0 - ADAM_B1 ** ADAM_STEP)
    v_hat = v / (1.0 - ADAM_B2 ** ADAM_STEP)
    delta = -ADAM_LR * (m_hat / (jnp.sqrt(v_hat) + ADAM_EPS) + ADAM_WD * w)
    return delta, m, v


def _rs_adamw(a_own, r2, w, m, v, layer, chip_idx, rows_blk, prev=None):
    _, R, C = w.shape
    nblk = R // rows_blk

    def body(chip_ref, a_ref, r_ref, w_ref, m_ref, v_ref, *rest):
        g_out, d_out, m_out, v_out = rest[-4:]
        g = (a_ref[0].astype(F32) + r_ref[0].astype(F32)) + (r_ref[1].astype(F32) + r_ref[2].astype(F32))
        delta, m_new, v_new = _adamw_math(w_ref[0], g, m_ref[0], v_ref[0])
        g_out[0] = g
        d_out[0] = delta
        m_out[0] = m_new
        v_out[0] = v_new

    blk = pl.BlockSpec((1, rows_blk, C), lambda i, chip: (layer, i, 0))
    n_prev = 0 if prev is None else 4
    grid_spec = pltpu.PrefetchScalarGridSpec(
        num_scalar_prefetch=1, grid=(nblk,),
        in_specs=[pl.BlockSpec((1, rows_blk, C), lambda i, chip: (chip[0], i, 0)),
                  pl.BlockSpec((3, rows_blk, C), lambda i, chip: (0, i, 0)), blk, blk, blk]
        + [_any_spec()] * n_prev,
        out_specs=[blk, blk, blk, blk])
    return pl.pallas_call(
        body, name="rs_adamw", grid_spec=grid_spec,
        out_shape=[jax.ShapeDtypeStruct((DEPTH, R, C), F32)] * 4,
        input_output_aliases={6 + t: t for t in range(n_prev)},
        compiler_params=_cp(("arbitrary",)),
    )(chip_idx, a_own, r2, w, m, v, *(prev or ()))


def _pair_sum(g, r1, core_idx, rows_blk):
    _, _, R, C = g.shape
    nblk = R // rows_blk

    def body(core_ref, g_ref, r_ref, o_ref):
        o_ref[...] = (g_ref[0].astype(F32) + r_ref[...].astype(F32)).astype(BF16)

    grid_spec = pltpu.PrefetchScalarGridSpec(
        num_scalar_prefetch=1, grid=(4, nblk),
        in_specs=[pl.BlockSpec((1, 1, rows_blk, C), lambda k, i, core: (core[0], k, i, 0)),
                  pl.BlockSpec((1, rows_blk, C), lambda k, i, core: (k, i, 0))],
        out_specs=pl.BlockSpec((1, rows_blk, C), lambda k, i, core: (k, i, 0)))
    return pl.pallas_call(
        body, name="rs_pair_sum", grid_spec=grid_spec,
        out_shape=jax.ShapeDtypeStruct((4, R, C), BF16),
        compiler_params=_cp(("arbitrary", "arbitrary")),
    )(core_idx, g, r1)


def _small_sum(gathered):
    def body(g_ref, o_ref):
        acc = g_ref[0]
        for d in range(1, N_DEV):
            acc = acc + g_ref[d]
        o_ref[...] = acc

    return pl.pallas_call(
        body, name="small_sum",
        out_shape=jax.ShapeDtypeStruct(gathered.shape[1:], F32),
        compiler_params=_cp(),
    )(gathered)


def _adamw_small(ws, gs, ms, vs):
    n = len(ws)

    def body(*refs):
        w_r, g_r, m_r, v_r = refs[0:n], refs[n:2 * n], refs[2 * n:3 * n], refs[3 * n:4 * n]
        d_o, m_o, v_o = refs[4 * n:5 * n], refs[5 * n:6 * n], refs[6 * n:7 * n]
        for t in range(n):
            delta, m_new, v_new = _adamw_math(w_r[t][...], g_r[t][...], m_r[t][...], v_r[t][...])
            d_o[t][...] = delta
            m_o[t][...] = m_new
            v_o[t][...] = v_new

    shapes = [jax.ShapeDtypeStruct(w.shape, F32) for w in ws]
    outs = pl.pallas_call(
        body, name="adamw_small", out_shape=shapes * 3, compiler_params=_cp(),
    )(*ws, *gs, *ms, *vs)
    return outs[0:n], outs[n:2 * n], outs[2 * n:3 * n]


def _mesh_pos():
    return lax.axis_index("x"), lax.axis_index("y"), lax.axis_index("c")


def _peers():
    x, y, c = _mesh_pos()
    return (x, y, c), (x, y, 1 - c), [(1 - x, y), (x, 1 - y), (1 - x, 1 - y)]


def _slot(ref, pos):
    return ref.at[4 * pos[0] + 2 * pos[1] + pos[2]]


def _remote(src, dst, send_sem, recv_sem, to):
    return pltpu.make_async_remote_copy(src_ref=src, dst_ref=dst, send_sem=send_sem, recv_sem=recv_sem,
                                        device_id=to, device_id_type=MESH)


def _ag_spread(shards):
    n = len(shards)

    def copies(ins, outs, sems):
        send, recv, loc = sems
        me, sibling, chips = _peers()
        peers = [sibling] + [(*chip, me[2]) for chip in chips]
        local = [pltpu.make_async_copy(ins[a], _slot(outs[a], me), loc.at[a]) for a in range(n)]
        sends = [_remote(ins[a], _slot(outs[a], me), send.at[a, k], recv.at[a, k], p)
                 for a in range(n) for k, p in enumerate(peers)]
        recvs = [_remote(ins[a], _slot(outs[a], p), send.at[a, k], recv.at[a, k], p)
                 for a in range(n) for k, p in enumerate(peers)]
        return local, sends, recvs

    def start(ins, outs, sems):
        local, sends, _ = copies(ins, outs, sems)
        for cp in local + sends:
            cp.start()

    def finish(ins, outs, sems):
        local, sends, recvs = copies(ins, outs, sems)
        for cp in sends:
            cp.wait_send()
        for cp in recvs:
            cp.wait_recv()
        for cp in local:
            cp.wait()

    return _Job(shards, [jax.ShapeDtypeStruct((N_DEV,) + a.shape, a.dtype) for a in shards],
                [pltpu.SemaphoreType.DMA((n, 4)), pltpu.SemaphoreType.DMA((n, 4)), pltpu.SemaphoreType.DMA((n,))],
                start, finish)


def _ag_pass(stacks):
    n = len(stacks)

    def copies(ins, outs, sems):
        send, recv = sems
        me, sibling, chips = _peers()
        sends = [_remote(_slot(ins[a], (*chip, me[2])), _slot(outs[a], (*chip, me[2])), send.at[a, j], recv.at[a, j],
                         sibling) for a in range(n) for j, chip in enumerate(chips)]
        recvs = [_remote(_slot(ins[a], (*chip, me[2])), _slot(outs[a], (*chip, 1 - me[2])), send.at[a, j],
                         recv.at[a, j], sibling) for a in range(n) for j, chip in enumerate(chips)]
        return sends, recvs

    def start(ins, outs, sems):
        for cp in copies(ins, outs, sems)[0]:
            cp.start()

    def finish(ins, outs, sems):
        sends, recvs = copies(ins, outs, sems)
        for cp in sends:
            cp.wait_send()
        for cp in recvs:
            cp.wait_recv()

    return _Job(stacks, [jax.ShapeDtypeStruct(a.shape, a.dtype) for a in stacks],
                [pltpu.SemaphoreType.DMA((n, 3)), pltpu.SemaphoreType.DMA((n, 3))],
                start, finish, aliases={a: a for a in range(n)})


def _ag_both(shards):
    spread = _ag_spread(shards)
    fake = [jax.ShapeDtypeStruct((N_DEV,) + a.shape, a.dtype) for a in shards]
    onward = _ag_pass(fake)
    n_sp = len(spread.sems)

    def start(ins, outs, sems):
        spread.start(ins, outs, sems[:n_sp])

    def finish(ins, outs, sems):
        spread.finish(ins, outs, sems[:n_sp])
        onward.start(outs, outs, sems[n_sp:])
        onward.finish(outs, outs, sems[n_sp:])

    return _Job(shards, spread.out_shapes, spread.sems + onward.sems, start, finish)


def _rs_swap(parts):
    n = len(parts)

    def copies(ins, outs, sems):
        send, recv = sems
        me, sibling, _ = _peers()
        return [_remote(ins[a].at[1 - me[2]], outs[a], send.at[a], recv.at[a], sibling) for a in range(n)]

    def start(ins, outs, sems):
        for cp in copies(ins, outs, sems):
            cp.start()

    def finish(ins, outs, sems):
        for cp in copies(ins, outs, sems):
            cp.wait()

    return _Job(parts, [jax.ShapeDtypeStruct(a.shape[1:], a.dtype) for a in parts],
                [pltpu.SemaphoreType.DMA((n,)), pltpu.SemaphoreType.DMA((n,))], start, finish)


def _rs_ici(pairs):
    n = len(pairs)

    def copies(ins, outs, sems):
        send, recv = sems
        me, _, chips = _peers()
        return [_remote(ins[a].at[2 * chip[0] + chip[1]], outs[a].at[j], send.at[a, j], recv.at[a, j],
                        (*chip, me[2])) for a in range(n) for j, chip in enumerate(chips)]

    def start(ins, outs, sems):
        for cp in copies(ins, outs, sems):
            cp.start()

    def finish(ins, outs, sems):
        for cp in copies(ins, outs, sems):
            cp.wait()

    return _Job(pairs, [jax.ShapeDtypeStruct((3,) + a.shape[1:], a.dtype) for a in pairs],
                [pltpu.SemaphoreType.DMA((n, 3)), pltpu.SemaphoreType.DMA((n, 3))], start, finish)


def _comm_call(jobs, name):
    def body():
        pass

    return _pcall(body, name=name, grid=(), in_specs=[], out_specs=[], out_shape=[], operands=(), jobs=jobs)[1]


def _allgather(arrs, name):
    n = len(arrs)

    def body(*refs):
        ins, outs = refs[:n], refs[n:2 * n]
        send_sems, recv_sems, local_sems = refs[2 * n:]
        x, y, c = _mesh_pos()
        me, sibling = (x, y, c), (x, y, 1 - c)
        chips = [(1 - x, y), (x, 1 - y), (1 - x, 1 - y)]

        def slot(a, pos):
            return outs[a].at[4 * pos[0] + 2 * pos[1] + pos[2]]

        def copy(a, k, block, to, src=None):
            return pltpu.make_async_remote_copy(
                src_ref=slot(a, block) if src is None else src, dst_ref=slot(a, block),
                send_sem=send_sems.at[a, k], recv_sem=recv_sems.at[a, k],
                device_id=to, device_id_type=MESH)

        mine = [pltpu.make_async_copy(ins[a], slot(a, me), local_sems.at[a]) for a in range(n)]
        for cp in mine:
            cp.start()
        first = []
        for a in range(n):
            first.append(copy(a, 0, me, sibling, src=ins[a]))
            first += [copy(a, 1 + j, me, (*chip, c), src=ins[a]) for j, chip in enumerate(chips)]
        for cp in first:
            cp.start()
        passed = []
        for j, chip in enumerate(chips):
            for a in range(n):
                copy(a, 1 + j, (*chip, c), me).wait_recv()
                fwd = copy(a, 4 + j, (*chip, c), sibling)
                fwd.start()
                passed.append(fwd)
        for a in range(n):
            copy(a, 0, sibling, me).wait_recv()
            for j, chip in enumerate(chips):
                copy(a, 4 + j, (*chip, 1 - c), me).wait_recv()
        for cp in first + passed:
            cp.wait_send()
        for cp in mine:
            cp.wait()

    return pl.pallas_call(
        body, name=name,
        in_specs=[_any_spec()] * n, out_specs=[_any_spec()] * n,
        out_shape=[jax.ShapeDtypeStruct((N_DEV,) + a.shape, a.dtype) for a in arrs],
        scratch_shapes=[pltpu.SemaphoreType.DMA((n, 7)), pltpu.SemaphoreType.DMA((n, 7)),
                        pltpu.SemaphoreType.DMA((n,))],
        compiler_params=_cp(),
    )(*arrs)


W_IN_SHARD = 354
W_IN_COLS = ((0, 192, OQ), (192, 192, OKK), (384, 384, OV), (768, 384, OG), (1152, 16, OLR), (1168, 512, OCU),
             (1680, 384, OAQ), (2064, 384, OAK), (2448, 384, OAV))


def _w_in_padded(stack):
    new_to_ref = {new: (start, width) for start, width, new in W_IN_COLS}
    cols = []
    for new, padded in IN_GROUPS:
        start, width = new_to_ref[new]
        a = start
        while a < start + width:
            j = a // W_IN_SHARD
            b = min(start + width, (j + 1) * W_IN_SHARD)
            cols.append(stack[j][:, a - j * W_IN_SHARD:b - j * W_IN_SHARD])
            a = b
        if padded > width:
            cols.append(jnp.zeros((stack.shape[1], padded - width), stack.dtype))
    return jnp.concatenate(cols, axis=1)


def _dw_in_shards(dw):
    shards = []
    for j in range(N_DEV):
        lo, hi = j * W_IN_SHARD, (j + 1) * W_IN_SHARD
        segs = []
        for start, width, new in W_IN_COLS:
            a, b = max(lo, start), min(hi, start + width)
            if a < b:
                segs.append(dw[:, new + a - start:new + b - start])
        shards.append(jnp.concatenate(segs, axis=1))
    return jnp.stack([jnp.stack([shards[2 * chip + core] for chip in range(4)]) for core in range(2)])


def _pad_to(a, shape):
    return jnp.pad(a, [(0, s - d) for d, s in zip(a.shape, shape)])


SMALL_LAYOUT = (
    ("norm_mix", 2, 1024), ("norm_ffn", 2, 1024), ("norm_final", 1, 1024), ("gla_norm", 2, 384),
    ("b_gla_gate", 2, 192), ("b_dw", 2, 256), ("conv_ln_g", 2, 256), ("conv_ln_b", 2, 256),
    ("rel_bias", 12, 257), ("w_gla_gate", 32, 192), ("w_dw", 62, 256),
)
SMALL_LANES = 128
SMALL_TILE = 8 * SMALL_LANES


def _small_tile_rows(r, lanes):
    return -(-(r * lanes) // SMALL_TILE) * 8


SMALL_ROWS = sum(_small_tile_rows(r, lanes) for _, r, lanes in SMALL_LAYOUT)


def _pack_small(parts):
    tiles = []
    for name, r, lanes in SMALL_LAYOUT:
        rows = _small_tile_rows(r, lanes)
        flat = _pad_to(parts[name].reshape(r * lanes), (rows * SMALL_LANES,))
        tiles.append(flat.reshape(rows, SMALL_LANES))
    return jnp.concatenate(tiles, axis=0)


def _unpack_small(packed):
    out, r0 = {}, 0
    for name, r, lanes in SMALL_LAYOUT:
        rows = _small_tile_rows(r, lanes)
        out[name] = packed[r0:r0 + rows].reshape(rows * SMALL_LANES)[:r * lanes].reshape(r, lanes)
        r0 += rows
    return out


def _mixers_fwd(h, wl, w_in_p, plan=None):
    plan, res = plan or {}, {}

    def jobs(host):
        return plan[host](res) if host in plan else ()

    (q, k, v, g, cu, aq, ak, av, lr), res["inproj"] = _inproj_fwd(h, wl["norm_mix"], w_in_p, jobs=jobs("inproj"))
    bias = _relbias_expand(wl["rb"])
    (o_att, lse), res["att"] = _att_fwd(aq, ak, av, bias, jobs=jobs("att"))
    (o_gla, states), res["gla"] = _gla_fwd(q, k, v, g, lr, wl["wg"], wl["bg"], wl["gn"], jobs=jobs("gla"))
    (o_conv, conv), res["conv"] = _conv_fwd(cu, wl["w_dw"], wl["b_dw"], wl["ln_g"], wl["ln_b"], jobs=jobs("conv"))
    sv = dict(h=h, w_in=w_in_p, q=q, k=k, v=v, g=g, cu=cu, aq=aq, ak=ak, av=av, lr=lr,
              o_gla=o_gla, o_conv=o_conv, conv=conv, o_att=o_att, lse=lse, states=states, bias=bias)
    return sv, res


def _mixers_bwd(sv, wl, dh1, d_ogla, d_oconv, att_grads, conv_jobs=(), x_jobs_fn=None):
    daq, dak, dav, dbias = att_grads
    d_rb = _relbias_grad(dbias)
    (dcu, dw_dw, db_dw, dln_g, dln_b), conv_res = _conv_bwd(
        sv["cu"], sv["conv"], d_oconv, wl["w_dw"], wl["b_dw"], wl["ln_g"], wl["ln_b"], jobs=conv_jobs)
    dq, dk, dv, dg, dlr, dwg, dbg, dgn = _gla_bwd(sv["q"], sv["k"], sv["v"], sv["g"], sv["lr"], sv["states"],
                                                  d_ogla, wl["wg"], wl["bg"], wl["gn"])
    dparts = (dq, dk, dv, dg, dcu, daq, dak, dav, dlr)
    dw_in = _inproj_bwd_w(sv["h"], wl["norm_mix"], dparts)
    x_jobs = x_jobs_fn(dw_in) if x_jobs_fn is not None else ()
    (dh, dhb, d_nmix), x_res = _inproj_bwd_x(sv["h"], dh1, wl["norm_mix"], sv["w_in"], dparts, jobs=x_jobs)
    small = dict(norm_mix=d_nmix, wg=dwg, bg=dbg, gn=dgn, w_dw=dw_dw, b_dw=db_dw, ln_g=dln_g, ln_b=dln_b, rb=d_rb)
    return (dh, dhb), dw_in, small, conv_res, x_res


def _layer_small(l, w_dw_full, norm_mix, w_gla_gate, b_gla_gate, gla_norm, b_dw, conv_ln_g, conv_ln_b, rel_bias,
                 norm_ffn):
    return dict(
        norm_mix=norm_mix[l][None, :], norm_ffn=norm_ffn[l][None, :],
        wg=_pad_to(w_gla_gate[l], (128, 256)).astype(BF16), bg=_pad_to(b_gla_gate[l][None, :], (1, 256)),
        gn=gla_norm[l][None, :], w_dw=_pad_to(w_dw_full, (32, 256)), b_dw=b_dw[l][None, :],
        ln_g=conv_ln_g[l][None, :], ln_b=conv_ln_b[l][None, :], rb=_pad_to(rel_bias[l], (8, 384)))


RS_ROWS = dict(w_in=512, w_out=128, w_up=512, w_down=256)


def kernel(x, norm_mix, w_in, w_gla_gate, b_gla_gate, gla_norm, w_dw, b_dw, conv_ln_g, conv_ln_b, rel_bias, w_out, norm_ffn, w_up, w_down, norm_final, loss_target, m_norm_mix, m_w_in, m_w_gla_gate, m_b_gla_gate, m_gla_norm, m_w_dw, m_b_dw, m_conv_ln_g, m_conv_ln_b, m_rel_bias, m_w_out, m_norm_ffn, m_w_up, m_w_down, m_norm_final, v_norm_mix, v_w_in, v_w_gla_gate, v_b_gla_gate, v_gla_norm, v_w_dw, v_b_dw, v_conv_ln_g, v_conv_ln_b, v_rel_bias, v_w_out, v_norm_ffn, v_w_up, v_w_down, v_norm_final):
    mx, my, mc = _mesh_pos()
    me = 4 * mx + 2 * my + mc
    chip_idx = (2 * mx + my).astype(jnp.int32).reshape(1)
    core_idx = mc.astype(jnp.int32).reshape(1)
    x0, target = x[0], loss_target[0]

    def pair_sums(parts, r1):
        return [_pair_sum(p, r, core_idx, p.shape[2]) for p, r in zip(parts, r1)]

    sh = [dict(w_in=w_in[l].astype(BF16), w_out=w_out[l].astype(BF16), w_up=w_up[l].astype(BF16),
               w_down=w_down[l].astype(BF16)) for l in range(DEPTH)]
    dw_flat = _pad_to(w_dw, (DEPTH, 32, 32)).reshape(16, 128)
    st_in0, st_dw = _allgather([sh[0]["w_in"], dw_flat], "allgather_first")
    dw_all = st_dw.reshape(N_DEV, DEPTH, 32, 32)[:, :, :KCONV, :]
    dw_all = jnp.transpose(dw_all, (1, 2, 0, 3)).reshape(DEPTH, KCONV, 256)
    wl = [_layer_small(l, dw_all[l], norm_mix, w_gla_gate, b_gla_gate, gla_norm, b_dw, conv_ln_g, conv_ln_b,
                       rel_bias, norm_ffn) for l in range(DEPTH)]

    s0, s1 = sh[0], sh[1]
    half = s0["w_down"].shape[0] // 2
    down0_a, down0_b = s0["w_down"][:half], s0["w_down"][half:]
    sv0, g0 = _mixers_fwd(x0, wl[0], _w_in_padded(st_in0), plan=dict(
        inproj=lambda r: [_ag_spread([s0["w_out"], down0_a])],
        att=lambda r: [_ag_spread([s0["w_up"]]), _ag_pass(r["inproj"])],
        gla=lambda r: [_ag_spread([down0_b]), _ag_pass(r["att"][:1])],
        conv=lambda r: [_ag_pass(r["gla"][:1])]))
    st_out0, st_down0_a = g0["att"][1:]
    st_up0, st_down0_b = g0["gla"][1], g0["conv"][0]
    st_down0 = jnp.concatenate([st_down0_a, st_down0_b], axis=1)
    wo0 = st_out0.reshape(D, D)
    (h1_0, xn2t_0, h2_0, act_0), (st_in1, out1_half, up1_half) = _outproj_mlp_fwd(
        x0, sv0["o_gla"], sv0["o_conv"], sv0["o_att"], wo0, wl[0]["norm_ffn"], st_up0, st_down0,
        jobs=[_ag_both([s1["w_in"]]), _ag_spread([s1["w_out"], s1["w_up"]])])

    sv1, g1 = _mixers_fwd(h2_0, wl[1], _w_in_padded(st_in1), plan=dict(
        inproj=lambda r: [_ag_pass([out1_half, up1_half])],
        att=lambda r: [_ag_spread([s1["w_down"]])],
        gla=lambda r: [_ag_pass(r["att"])]))
    (st_out1, st_up1), st_down1 = g1["inproj"], g1["gla"][0]
    wo1 = st_out1.reshape(D, D)
    (h1_1, xn2t_1, h2_1, act_1), _ = _outproj_mlp_fwd(
        h2_0, sv1["o_gla"], sv1["o_conv"], sv1["o_att"], wo1, wl[1]["norm_ffn"], st_up1, st_down1)

    loss8, dh, dhb, d_nf = _loss_fwd_bwd(h2_1, norm_final[None, :], target)
    loss = lax.psum(loss8[0, 0], ("x", "y", "c"))

    def layer_bwd(dh_pair, sv, wl_l, xn2t, act, h1, wo, st_up, st_down, mlp_jobs, x_jobs_fn):
        (dh1, dw_up, dw_down, d_nffn), mlp_res = _mlp_bwd(xn2t, act, h1, dh_pair[0], dh_pair[1], wl_l["norm_ffn"],
                                                           st_up, st_down, jobs=mlp_jobs)
        ud = [dw_up, dw_down]
        (d_ogla, d_oconv, d_oatt, dw_out), r1 = _outproj_bwd(
            dh1, sv["o_gla"], sv["o_conv"], sv["o_att"], wo, jobs=[_rs_swap(ud)])
        pair_ud = pair_sums(ud, r1)
        att_grads, r = _att_bwd(sv["aq"], sv["ak"], sv["av"], sv["bias"], sv["o_att"], sv["lse"], d_oatt,
                                jobs=[_rs_ici(pair_ud), _rs_swap([dw_out])])
        r2_ud, r1_out = r[:2], r[2:]
        pair_out = pair_sums([dw_out], r1_out)
        dh_in, _, small, r2_out, x_res = _mixers_bwd(sv, wl_l, dh1, d_ogla, d_oconv, att_grads,
                                                     conv_jobs=[_rs_ici(pair_out)], x_jobs_fn=x_jobs_fn)
        small["norm_ffn"] = d_nffn
        sums = dict(w_out=(pair_out[0], r2_out[0]), w_up=(pair_ud[0], r2_ud[0]), w_down=(pair_ud[1], r2_ud[1]))
        return dh_in, small, sums, mlp_res, x_res

    stash = {}

    def swap_w_in(dw_in):
        stash["in1"] = [_dw_in_shards(dw_in)]
        return [_rs_swap(stash["in1"])]

    dh_pair, small1, sums1, _, r1_in1 = layer_bwd((dh, dhb), sv1, wl[1], xn2t_1, act_1, h1_1, wo1, st_up1, st_down1,
                                                  (), swap_w_in)
    pair_in1 = pair_sums(stash["in1"], r1_in1)

    def send_w_in(dw_in):
        in0 = [_dw_in_shards(dw_in)]
        stash["pair_in0"] = pair_sums(in0, _comm_call([_rs_swap(in0)], "rs_swap_w_in_0"))
        return [_rs_ici(stash["pair_in0"])]

    (dx, _), small0, sums0, r2_in1, r2_in0 = layer_bwd(dh_pair, sv0, wl[0], xn2t_0, act_0, h1_0, wo0, st_up0, st_down0,
                                                       [_rs_ici(pair_in1)], send_w_in)
    sums1["w_in"] = (pair_in1[0], r2_in1[0])
    sums0["w_in"] = (stash["pair_in0"][0], r2_in0[0])

    big_w = dict(w_in=(w_in, m_w_in, v_w_in), w_out=(w_out, m_w_out, v_w_out), w_up=(w_up, m_w_up, v_w_up),
                 w_down=(w_down, m_w_down, v_w_down))
    pairs = {1: sums1, 0: sums0}
    big_out = {}
    for name, (w_, m_, v_) in big_w.items():
        res = None
        for l in (1, 0):
            a_own, r2_ = pairs[l][name]
            res = _rs_adamw(a_own, r2_, w_, m_, v_, l, chip_idx, RS_ROWS[name], prev=res)
        big_out[name] = res

    grads = (small0, small1)
    parts = dict(
        norm_mix=jnp.concatenate([grads[l]["norm_mix"] for l in range(DEPTH)], axis=0),
        norm_ffn=jnp.concatenate([grads[l]["norm_ffn"] for l in range(DEPTH)], axis=0),
        norm_final=d_nf,
        gla_norm=jnp.concatenate([grads[l]["gn"] for l in range(DEPTH)], axis=0),
        b_gla_gate=jnp.concatenate([grads[l]["bg"][:, :192] for l in range(DEPTH)], axis=0),
        b_dw=jnp.concatenate([grads[l]["b_dw"] for l in range(DEPTH)], axis=0),
        conv_ln_g=jnp.concatenate([grads[l]["ln_g"] for l in range(DEPTH)], axis=0),
        conv_ln_b=jnp.concatenate([grads[l]["ln_b"] for l in range(DEPTH)], axis=0),
        rel_bias=jnp.concatenate([grads[l]["rb"][:6, :N_REL] for l in range(DEPTH)], axis=0),
        w_gla_gate=jnp.concatenate([grads[l]["wg"][:16, :192] for l in range(DEPTH)], axis=0),
        w_dw=jnp.concatenate([grads[l]["w_dw"][:KCONV] for l in range(DEPTH)], axis=0),
    )
    small_all = _allgather([_pack_small(parts)], "allgather_small")[0]
    sg = _unpack_small(_small_sum(small_all))
    dw_grad = lax.dynamic_slice_in_dim(sg["w_dw"].reshape(DEPTH, KCONV, 256), me * 32, 32, axis=2)
    small_g = dict(
        norm_mix=sg["norm_mix"], w_gla_gate=sg["w_gla_gate"].reshape(DEPTH, 16, 192), b_gla_gate=sg["b_gla_gate"],
        gla_norm=sg["gla_norm"], w_dw=dw_grad, b_dw=sg["b_dw"], conv_ln_g=sg["conv_ln_g"],
        conv_ln_b=sg["conv_ln_b"], rel_bias=sg["rel_bias"].reshape(DEPTH, 6, N_REL), norm_ffn=sg["norm_ffn"],
        norm_final=sg["norm_final"].reshape(D))
    small_names = ("norm_mix", "w_gla_gate", "b_gla_gate", "gla_norm", "w_dw", "b_dw", "conv_ln_g", "conv_ln_b",
                   "rel_bias", "norm_ffn", "norm_final")
    small_w = dict(norm_mix=norm_mix, w_gla_gate=w_gla_gate, b_gla_gate=b_gla_gate, gla_norm=gla_norm, w_dw=w_dw,
                   b_dw=b_dw, conv_ln_g=conv_ln_g, conv_ln_b=conv_ln_b, rel_bias=rel_bias, norm_ffn=norm_ffn,
                   norm_final=norm_final)
    small_m = dict(norm_mix=m_norm_mix, w_gla_gate=m_w_gla_gate, b_gla_gate=m_b_gla_gate, gla_norm=m_gla_norm,
                   w_dw=m_w_dw, b_dw=m_b_dw, conv_ln_g=m_conv_ln_g, conv_ln_b=m_conv_ln_b, rel_bias=m_rel_bias,
                   norm_ffn=m_norm_ffn, norm_final=m_norm_final)
    small_v = dict(norm_mix=v_norm_mix, w_gla_gate=v_w_gla_gate, b_gla_gate=v_b_gla_gate, gla_norm=v_gla_norm,
                   w_dw=v_w_dw, b_dw=v_b_dw, conv_ln_g=v_conv_ln_g, conv_ln_b=v_conv_ln_b, rel_bias=v_rel_bias,
                   norm_ffn=v_norm_ffn, norm_final=v_norm_final)
    s_delta, s_m, s_v = _adamw_small([small_w[n] for n in small_names], [small_g[n] for n in small_names],
                                     [small_m[n] for n in small_names], [small_v[n] for n in small_names])
    s_idx = {n: t for t, n in enumerate(small_names)}

    order = ("norm_mix", "w_in", "w_gla_gate", "b_gla_gate", "gla_norm", "w_dw", "b_dw", "conv_ln_g", "conv_ln_b",
             "rel_bias", "w_out", "norm_ffn", "w_up", "w_down", "norm_final")

    def pick(kind, name):
        if name in big_out:
            return big_out[name][kind]
        t = s_idx[name]
        return (small_g[name], s_delta[t], s_m[t], s_v[t])[kind]

    outs = [loss, dx[None]]
    for kind in range(4):
        outs += [pick(kind, n) for n in order]
    return tuple(outs)
```
